```python
import math
import jax, jax.numpy as jnp
from jax import lax
import numpy as np


D_MODEL = 2048
BATCH = 4
SEQ = 2048
DEPTH = 1
DEC_BATCH = 128
DEC_SEQ = 4
PAST_LEN = 16384
PAGE_SIZE = 128

D_MIX = D_MODEL
D_A = D_MIX // 2
D_B = D_MIX - D_A
CHUNK = 128
H_A = 8
GA_DIM = D_A // H_A
DK = 128
DV = 128
H_B = D_B // DV
DN_CHUNK = 64
CONV_QKV = 4
CONV_FFN = 3
D_FF = 256 * ((8 * D_MODEL // 3 + 255) // 256)
D_QKV = 3 * D_B
SPLITS = [D_A, 2 * D_A, 2 * D_A + D_QKV, 2 * D_A + D_QKV + D_B, 2 * D_A + D_QKV + D_B + H_B]
D_IN = 2 * D_A + D_QKV + D_B + 2 * H_B
V_COL0 = 2 * D_A + 2 * D_B
ALPHA = (2.0 * DEPTH) ** 0.25
BETA_INIT = (8.0 * DEPTH) ** -0.25
EPS = 1e-5

kernel_name = 'hymba_chunkmlp_gdn_convffn_step'


def layer_norm(x, g, b):
    xf = x.astype(jnp.float32)
    mu = jnp.mean(xf, axis=-1, keepdims=True)
    var = jnp.mean(jnp.square(xf - mu), axis=-1, keepdims=True)
    return ((xf - mu) * lax.rsqrt(var + EPS) * g.astype(jnp.float32) + b.astype(jnp.float32)).astype(x.dtype)


def rms_norm(x, g):
    xf = x.astype(jnp.float32)
    ms = jnp.mean(jnp.square(xf), axis=-1, keepdims=True)
    return (xf * lax.rsqrt(ms + EPS) * g.astype(jnp.float32)).astype(x.dtype)


def l2norm(x):
    xf = x.astype(jnp.float32)
    return xf * lax.rsqrt(jnp.sum(xf * xf, axis=-1, keepdims=True) + 1e-6)


def causal_dwconv(x, buf, w, bias=None):
    width = w.shape[0]
    L = x.shape[1]
    xp = jnp.concatenate([buf.astype(x.dtype), x], axis=1)
    y = w[0] * xp[:, 0:L]
    for j in range(1, width):
        y = y + w[j] * xp[:, j:j + L]
    if bias is not None:
        y = y + bias
    return y, xp[:, -(width - 1):]


def chunk_mlp(u, v, ln_g, ln_b, ws, bs):
    bsz, L, _ = v.shape
    vh = layer_norm(v.reshape(bsz, L, H_A, GA_DIM), ln_g.reshape(H_A, GA_DIM), ln_b.reshape(H_A, GA_DIM))
    c = min(L, CHUNK)
    pad = (-L) % c
    n = (L + pad) // c
    vc = jnp.pad(vh, ((0, 0), (0, pad), (0, 0), (0, 0))).reshape(bsz, n, c, H_A, GA_DIM)
    w = ws[:, :c, :c] * jnp.tril(jnp.ones((c, c), ws.dtype))
    mixed = jnp.einsum('hts,bnshd->bnthd', w, vc) + bs[:, :c].T[None, None, :, :, None]
    mixed = mixed.reshape(bsz, n * c, H_A, GA_DIM)[:, :L].reshape(bsz, L, D_A)
    return u * mixed, vh.reshape(bsz, L, D_A)


def chunk_gated_delta(q, k, v, g, beta, s0):
    f32 = jnp.float32
    bsz, L, H, _ = q.shape
    c = min(L, DN_CHUNK)
    pad = (-L) % c
    n = (L + pad) // c

    def blocks(t):
        t = t.astype(f32)
        t = jnp.pad(t, [(0, 0), (0, pad)] + [(0, 0)] * (t.ndim - 2))
        t = t.reshape((bsz, n, c) + t.shape[2:])
        return jnp.moveaxis(t, (1, 3), (0, 2))

    qf, kf, vf, gf, bf = blocks(q), blocks(k), blocks(v), blocks(g), blocks(beta)
    gc = jnp.cumsum(gf, axis=-1)
    causal = jnp.tril(jnp.ones((c, c), dtype=bool))
    strict = jnp.tril(jnp.ones((c, c), dtype=bool), -1)
    decay = jnp.exp(jnp.where(causal, gc[..., :, None] - gc[..., None, :], -jnp.inf))
    kb = kf * bf[..., None]
    vb = vf * bf[..., None]
    lower = jnp.einsum('nbhcd,nbhsd->nbhcs', kb, kf) * decay
    a_mat = jnp.where(strict, lower, 0.0) + jnp.eye(c, dtype=f32)
    rhs = jnp.concatenate([vb, kb * jnp.exp(gc)[..., None]], axis=-1)
    sol = lax.linalg.triangular_solve(a_mat, rhs, left_side=True, lower=True, unit_diagonal=True)
    u_blk, w_blk = sol[..., :DV], sol[..., DV:]
    qk = jnp.einsum('nbhcd,nbhsd->nbhcs', qf, kf) * decay

    def step(S, xs):
        qi, ki, ui, wi, gci, qki = xs
        v_new = ui - jnp.einsum('bhcd,bhde->bhce', wi, S)
        o = (jnp.einsum('bhcd,bhde->bhce', qi * jnp.exp(gci)[..., None], S)
             + jnp.einsum('bhcs,bhse->bhce', qki, v_new))
        gl = gci[..., -1:]
        S = S * jnp.exp(gl)[..., None] + jnp.einsum('bhcd,bhce->bhde', ki * jnp.exp(gl - gci)[..., None], v_new)
        return S, o

    S, o = lax.scan(step, s0.astype(f32), (qf, kf, u_blk, w_blk, gc, qk))
    o = jnp.moveaxis(o, (0, 2), (1, 3)).reshape(bsz, n * c, H, DV)[:, :L]
    return o, S


def trunk_layer(x, s_dn, buf_qkv, buf_ffn, w_in, conv_qkv_w, dn_a_log, dn_dt_bias, dn_norm_g,
                gm_ln_g, gm_ln_b, gm_ws, gm_bs, gm_out_g, w_out, ln1_g, ln1_b,
                w_up, conv_ffn_w, conv_ffn_b, w_down, ln2_g, ln2_b):
    bsz, L, _ = x.shape
    h = jnp.einsum('bld,de->ble', x, w_in)
    u, v, qkv, z, a, bg = jnp.split(h, SPLITS, axis=-1)

    ya, v_rows = chunk_mlp(jax.nn.gelu(u, approximate=False), jax.nn.gelu(v, approximate=False),
                           gm_ln_g, gm_ln_b, gm_ws, gm_bs)
    ya = rms_norm(ya.reshape(bsz, L, H_A, GA_DIM), gm_out_g.reshape(H_A, GA_DIM)).reshape(bsz, L, D_A)

    qkv_c, buf_qkv_new = causal_dwconv(qkv, buf_qkv, conv_qkv_w)
    q, k, vd = jnp.split(jax.nn.silu(qkv_c), 3, axis=-1)
    q = l2norm(q.reshape(bsz, L, H_B, DK)) * (DK ** -0.5)
    k = l2norm(k.reshape(bsz, L, H_B, DK))
    vd = vd.reshape(bsz, L, H_B, DV)
    g = -jnp.exp(dn_a_log.astype(jnp.float32)) * jax.nn.softplus(a.astype(jnp.float32) + dn_dt_bias.astype(jnp.float32))
    beta = jax.nn.sigmoid(bg.astype(jnp.float32))
    o, s_new = chunk_gated_delta(q, k, vd, g, beta, s_dn)
    o = rms_norm(o.astype(x.dtype), dn_norm_g) * jax.nn.silu(z.reshape(bsz, L, H_B, DV))
    yb = o.reshape(bsz, L, D_B)

    mix = jnp.einsum('ble,ed->bld', jnp.concatenate([ya, yb], axis=-1), w_out)
    x1 = layer_norm(ALPHA * x + mix, ln1_g, ln1_b)

    up = jnp.einsum('bld,df->blf', x1, w_up)
    up_c, buf_ffn_new = causal_dwconv(up, buf_ffn, conv_ffn_w, conv_ffn_b)
    gate, val = jnp.split(up_c, 2, axis=-1)
    f = jnp.einsum('blf,fd->bld', jax.nn.silu(gate) * val, w_down)
    y = layer_norm(ALPHA * x1 + f, ln2_g, ln2_b)
    return y, s_new.astype(s_dn.dtype), buf_qkv_new, buf_ffn_new, v_rows


def setup_inputs(seed: int = 0) -> dict:
    key = jax.random.key(seed)
    ks = jax.random.split(key, 26)
    nrm = jax.random.normal
    f32 = jnp.float32
    col_scale = jnp.ones((D_IN,), f32).at[V_COL0:V_COL0 + D_B].set(BETA_INIT)
    w_in = nrm(ks[0], (DEPTH, D_MODEL, D_IN), f32) * (D_MODEL ** -0.5) * col_scale
    conv_qkv_w = nrm(ks[1], (DEPTH, CONV_QKV, D_QKV), f32) * (CONV_QKV ** -0.5)
    dn_a_log = jnp.log(jax.random.uniform(ks[2], (DEPTH, H_B), f32, 1.0, 16.0))
    dt = jnp.exp(jax.random.uniform(ks[3], (DEPTH, H_B), f32, math.log(1e-3), math.log(1e-1)))
    dn_dt_bias = dt + jnp.log(-jnp.expm1(-dt))
    dn_norm_g = 1.0 + 0.02 * nrm(ks[4], (DEPTH, DV), f32)
    gm_ln_g = 1.0 + 0.02 * nrm(ks[5], (DEPTH, D_A), f32)
    gm_ln_b = 0.02 * nrm(ks[6], (DEPTH, D_A), f32)
    gm_ws = nrm(ks[7], (DEPTH, H_A, CHUNK, CHUNK), f32) * (CHUNK ** -0.5)
    gm_bs = 1.0 + 0.02 * nrm(ks[8], (DEPTH, H_A, CHUNK), f32)
    gm_out_g = 1.0 + 0.02 * nrm(ks[9], (DEPTH, D_A), f32)
    w_out = nrm(ks[10], (DEPTH, D_MIX, D_MODEL), f32) * (D_MIX ** -0.5) * BETA_INIT
    ln1_g = 1.0 + 0.02 * nrm(ks[11], (DEPTH, D_MODEL), f32)
    ln1_b = 0.02 * nrm(ks[12], (DEPTH, D_MODEL), f32)
    w_up = nrm(ks[13], (DEPTH, D_MODEL, 2 * D_FF), f32) * (D_MODEL ** -0.5)
    conv_ffn_w = nrm(ks[14], (DEPTH, CONV_FFN, 2 * D_FF), f32) * (CONV_FFN ** -0.5)
    conv_ffn_b = 0.01 * nrm(ks[15], (DEPTH, 2 * D_FF), f32)
    w_down = nrm(ks[16], (DEPTH, D_FF, D_MODEL), f32) * (D_FF ** -0.5) * BETA_INIT
    ln2_g = 1.0 + 0.02 * nrm(ks[17], (DEPTH, D_MODEL), f32)
    ln2_b = 0.02 * nrm(ks[18], (DEPTH, D_MODEL), f32)
    x_prompt = nrm(ks[19], (BATCH, SEQ, D_MODEL), f32)
    x_sample = nrm(ks[20], (DEC_BATCH, DEC_SEQ, D_MODEL), f32)
    state_dn = 0.1 * nrm(ks[21], (DEPTH, DEC_BATCH, H_B, DK, DV), f32)
    state_conv_qkv = nrm(ks[22], (DEPTH, DEC_BATCH, CONV_QKV - 1, D_QKV), f32)
    state_conv_ffn = nrm(ks[23], (DEPTH, DEC_BATCH, CONV_FFN - 1, 2 * D_FF), f32)
    return {
        'x_prompt': x_prompt, 'x_sample': x_sample,
        'state_dn': state_dn, 'state_conv_qkv': state_conv_qkv, 'state_conv_ffn': state_conv_ffn,
        'w_in': w_in, 'conv_qkv_w': conv_qkv_w, 'dn_a_log': dn_a_log, 'dn_dt_bias': dn_dt_bias,
        'dn_norm_g': dn_norm_g, 'gm_ln_g': gm_ln_g, 'gm_ln_b': gm_ln_b, 'gm_ws': gm_ws, 'gm_bs': gm_bs,
        'gm_out_g': gm_out_g, 'w_out': w_out, 'ln1_g': ln1_g, 'ln1_b': ln1_b,
        'w_up': w_up, 'conv_ffn_w': conv_ffn_w, 'conv_ffn_b': conv_ffn_b, 'w_down': w_down,
        'ln2_g': ln2_g, 'ln2_b': ln2_b,
    }


def reference(x_prompt, x_sample, state_dn, state_conv_qkv, state_conv_ffn, w_in, conv_qkv_w,
              dn_a_log, dn_dt_bias, dn_norm_g, gm_ln_g, gm_ln_b, gm_ws, gm_bs, gm_out_g, w_out,
              ln1_g, ln1_b, w_up, conv_ffn_w, conv_ffn_b, w_down, ln2_g, ln2_b):
    hp, hs = x_prompt, x_sample
    bp = x_prompt.shape[0]
    sd_p, cq_p, cf_p, sd_s, cq_s, cf_s, cv_s = [], [], [], [], [], [], []
    for l in range(DEPTH):
        lw = (w_in[l], conv_qkv_w[l], dn_a_log[l], dn_dt_bias[l], dn_norm_g[l],
              gm_ln_g[l], gm_ln_b[l], gm_ws[l], gm_bs[l], gm_out_g[l], w_out[l], ln1_g[l], ln1_b[l],
              w_up[l], conv_ffn_w[l], conv_ffn_b[l], w_down[l], ln2_g[l], ln2_b[l])
        s0 = jnp.zeros((bp, H_B, DK, DV), state_dn.dtype)
        q0 = jnp.zeros((bp, CONV_QKV - 1, D_QKV), hp.dtype)
        f0 = jnp.zeros((bp, CONV_FFN - 1, 2 * D_FF), hp.dtype)
        hp, s_p, q_p, f_p, _ = trunk_layer(hp, s0, q0, f0, *lw)
        hs, s_s, q_s, f_s, v_s = trunk_layer(hs, state_dn[l], state_conv_qkv[l], state_conv_ffn[l], *lw)
        sd_p.append(s_p); cq_p.append(q_p); cf_p.append(f_p)
        sd_s.append(s_s); cq_s.append(q_s); cf_s.append(f_s); cv_s.append(v_s)
    return (hp, hs, jnp.stack(sd_p), jnp.stack(cq_p), jnp.stack(cf_p),
            jnp.stack(sd_s), jnp.stack(cq_s), jnp.stack(cf_s), jnp.stack(cv_s))
```

```python
import functools

import jax
import jax.numpy as jnp
from jax import lax
from jax.experimental import pallas as pl
from jax.experimental.pallas import tpu as pltpu

F32 = jnp.float32
BF16 = jnp.bfloat16

LANES = 128
SUBLANES = 8
VMEM_LIMIT = 56 * 1024 * 1024

GROUP = 128
MLP_CHUNK = 128
DN_CHUNK = 64
CONV_QKV = 4
CONV_FFN = 3
EPS = 1e-5

_NT = (((1,), (1,)), ((), ()))
_TN = (((0,), (0,)), ((), ()))


def _params(sem):
    return pltpu.CompilerParams(dimension_semantics=sem, vmem_limit_bytes=VMEM_LIMIT)


def _bdot(a, b):
    return jnp.dot(a.astype(BF16), b.astype(BF16), preferred_element_type=F32)


def _gelu(x):
    return 0.5 * x * (1.0 + lax.erf(x * (0.5 ** 0.5)))


def _silu(x):
    return x * jax.nn.sigmoid(x)


def _layer_norm(x, g, b):
    mu = jnp.mean(x, axis=-1, keepdims=True)
    xc = x - mu
    var = jnp.mean(xc * xc, axis=-1, keepdims=True)
    return xc * lax.rsqrt(var + EPS) * g + b


def _rms_norm(x, g):
    ms = jnp.mean(x * x, axis=-1, keepdims=True)
    return x * lax.rsqrt(ms + EPS) * g


def _mm_kernel(x_ref, w_ref, o_ref, xb_ref):
    @pl.when(pl.program_id(1) == 0)
    def _():
        xb_ref[...] = x_ref[...].astype(BF16)

    o_ref[...] = jnp.dot(xb_ref[...], w_ref[...].astype(BF16), preferred_element_type=F32)


def _matmul(x, w, n_cols, tm, tn):
    m, k = x.shape
    return pl.pallas_call(
        _mm_kernel,
        grid=(m // tm, n_cols // tn),
        in_specs=[pl.BlockSpec((tm, k), lambda i, j: (i, 0)),
                  pl.BlockSpec((k, tn), lambda i, j: (0, j))],
        out_specs=pl.BlockSpec((tm, tn), lambda i, j: (i, j)),
        out_shape=jax.ShapeDtypeStruct((m, n_cols), F32),
        scratch_shapes=[pltpu.VMEM((tm, k), BF16)],
        compiler_params=_params(("arbitrary", "arbitrary")),
        name="proj",
    )(x, w)


def _cmlp_prompt_kernel(u_ref, v_ref, lng_ref, lnb_ref, ws_ref, bst_ref, og_ref, ya_ref, *, n_groups):
    row = lax.broadcasted_iota(jnp.int32, (MLP_CHUNK, MLP_CHUNK), 0)
    col = lax.broadcasted_iota(jnp.int32, (MLP_CHUNK, MLP_CHUNK), 1)
    for g in range(n_groups):
        sl = slice(g * GROUP, (g + 1) * GROUP)
        w = jnp.where(row >= col, ws_ref[g], 0.0)
        vh = _layer_norm(_gelu(v_ref[:, sl]), lng_ref[:, sl], lnb_ref[:, sl])
        mixed = _bdot(w, vh) + bst_ref[:, g:g + 1]
        y = _gelu(u_ref[:, sl]) * mixed
        ya_ref[:, sl] = _rms_norm(y, og_ref[:, sl]).astype(ya_ref.dtype)


def _chunk_mlp_prompt(h, ln_g, ln_b, ws, bs, out_g):
    m = h.shape[0]
    n_groups, d_a = ws.shape[0], ws.shape[0] * GROUP
    vec = pl.BlockSpec((1, d_a), lambda i: (0, 0))
    return pl.pallas_call(
        functools.partial(_cmlp_prompt_kernel, n_groups=n_groups),
        grid=(m // MLP_CHUNK,),
        in_specs=[pl.BlockSpec((MLP_CHUNK, d_a), lambda i: (i, 0)),
                  pl.BlockSpec((MLP_CHUNK, d_a), lambda i: (i, 1)),
                  vec, vec,
                  pl.BlockSpec((n_groups, MLP_CHUNK, MLP_CHUNK), lambda i: (0, 0, 0)),
                  pl.BlockSpec((MLP_CHUNK, n_groups), lambda i: (0, 0)),
                  vec],
        out_specs=pl.BlockSpec((MLP_CHUNK, d_a), lambda i: (i, 0)),
        out_shape=jax.ShapeDtypeStruct((m, d_a), BF16),
        compiler_params=_params(("arbitrary",)),
        name="chunk_mlp_prompt",
    )(h, h, ln_g.reshape(1, d_a), ln_b.reshape(1, d_a), ws, bs.T, out_g.reshape(1, d_a))


def _cmlp_sample_kernel(u_ref, v_ref, lng_ref, lnb_ref, wv_ref, bv_ref, og_ref, ya_ref, vr_ref, *, n_pos):
    vh = []
    for t in range(n_pos):
        vt = _layer_norm(_gelu(v_ref[t]), lng_ref[0], lnb_ref[0])
        vr_ref[t] = vt
        vh.append(vt)
    for t in range(n_pos):
        mixed = wv_ref[t, 0:1, :] * vh[0]
        for s in range(1, t + 1):
            mixed = mixed + wv_ref[t, s:s + 1, :] * vh[s]
        mixed = mixed + bv_ref[0, t:t + 1, :]
        y = _gelu(u_ref[t]) * mixed
        ya_ref[t] = _rms_norm(y, og_ref[0]).astype(ya_ref.dtype)


def _chunk_mlp_sample(h3, ln_g, ln_b, ws, bs, out_g):
    n_pos, nb, _ = h3.shape
    n_groups = ws.shape[0]
    d_a = n_groups * GROUP
    wv = jnp.repeat(jnp.transpose(ws[:, :n_pos, :n_pos], (1, 2, 0)), GROUP, axis=-1)
    bv = jnp.repeat(bs[:, :n_pos].T, GROUP, axis=-1).reshape(1, n_pos, d_a)
    vec = pl.BlockSpec((1, 1, GROUP), lambda g: (0, 0, g))
    act = pl.BlockSpec((n_pos, nb, GROUP), lambda g: (0, 0, g))
    return pl.pallas_call(
        functools.partial(_cmlp_sample_kernel, n_pos=n_pos),
        grid=(n_groups,),
        in_specs=[act,
                  pl.BlockSpec((n_pos, nb, GROUP), lambda g: (0, 0, n_groups + g)),
                  vec, vec,
                  pl.BlockSpec((n_pos, n_pos, GROUP), lambda g: (0, 0, g)),
                  pl.BlockSpec((1, n_pos, GROUP), lambda g: (0, 0, g)),
                  vec],
        out_specs=[act, act],
        out_shape=[jax.ShapeDtypeStruct((n_pos, nb, d_a), BF16),
                   jax.ShapeDtypeStruct((n_pos, nb, d_a), F32)],
        compiler_params=_params(("arbitrary",)),
        name="chunk_mlp_sample",
    )(h3, h3, ln_g.reshape(1, 1, d_a), ln_b.reshape(1, 1, d_a), wv, bv, out_g.reshape(1, 1, d_a))


def _conv_taps(xbuf_ref, w_ref, tm, shift, halo, width):
    acc = None
    for j in range(width):
        off = halo - (width - 1 - j) * shift
        term = w_ref[j:j + 1, :] * xbuf_ref[off:off + tm, :]
        acc = term if acc is None else acc + term
    return acc


def _qkv_prep_kernel(*refs, tm, shift, halo, n_heads, has_buf):
    if has_buf:
        (xq_ref, xk_ref, xv_ref, bq_ref, bk_ref, bv_ref, wq_ref, wk_ref, wv_ref,
         q_ref, k_ref, v_ref, xbuf_ref, carry_ref) = refs
        bufs = (bq_ref, bk_ref, bv_ref)
    else:
        (xq_ref, xk_ref, xv_ref, wq_ref, wk_ref, wv_ref,
         q_ref, k_ref, v_ref, xbuf_ref, carry_ref) = refs
        bufs = (None, None, None)
    first = pl.program_id(1) == 0
    parts = ((xq_ref, wq_ref, q_ref, GROUP ** -0.5), (xk_ref, wk_ref, k_ref, 1.0), (xv_ref, wv_ref, v_ref, None))
    for p, (x_ref, w_ref, o_ref, scale) in enumerate(parts):
        @pl.when(first)
        def _(p=p):
            if has_buf:
                xbuf_ref[0:halo, :] = bufs[p][...]
            else:
                xbuf_ref[0:halo, :] = jnp.zeros((halo, xbuf_ref.shape[1]), F32)

        @pl.when(jnp.logical_not(first))
        def _(p=p):
            xbuf_ref[0:halo, :] = carry_ref[p]

        xbuf_ref[halo:halo + tm, :] = x_ref[...]
        carry_ref[p] = xbuf_ref[tm:tm + halo, :]
        y = _silu(_conv_taps(xbuf_ref, w_ref, tm, shift, halo, CONV_QKV))
        if scale is None:
            o_ref[...] = y
        else:
            for h in range(n_heads):
                sl = slice(h * GROUP, (h + 1) * GROUP)
                yh = y[:, sl]
                yn = yh * lax.rsqrt(jnp.sum(yh * yh, axis=-1, keepdims=True) + 1e-6)
                o_ref[:, sl] = yn * scale if scale != 1.0 else yn


def _qkv_prep(h, conv_w, bufs, n_seq, rows_per_seq, tm, shift, col0):
    d_b = conv_w.shape[1] // 3
    n_heads = d_b // GROUP
    has_buf = bufs is not None
    halo = (CONV_QKV - 1) * shift if has_buf else SUBLANES
    tiles = rows_per_seq // tm
    cb = col0 // d_b
    x_specs = [pl.BlockSpec((tm, d_b), lambda s, i, c=c: (s * tiles + i, cb + c)) for c in range(3)]
    w_specs = [pl.BlockSpec((CONV_QKV, d_b), lambda s, i, c=c: (0, c)) for c in range(3)]
    o_spec = pl.BlockSpec((tm, d_b), lambda s, i: (s * tiles + i, 0))
    args, specs = [h, h, h], list(x_specs)
    if has_buf:
        specs += [pl.BlockSpec((halo, d_b), lambda s, i, c=c: (s, c)) for c in range(3)]
        args += [bufs, bufs, bufs]
    specs += w_specs
    args += [conv_w, conv_w, conv_w]
    m = n_seq * rows_per_seq
    return pl.pallas_call(
        functools.partial(_qkv_prep_kernel, tm=tm, shift=shift, halo=halo, n_heads=n_heads, has_buf=has_buf),
        grid=(n_seq, tiles),
        in_specs=specs,
        out_specs=[o_spec, o_spec, o_spec],
        out_shape=[jax.ShapeDtypeStruct((m, d_b), F32)] * 3,
        scratch_shapes=[pltpu.VMEM((halo + tm, d_b), F32), pltpu.VMEM((3, halo, d_b), F32)],
        compiler_params=_params(("arbitrary", "arbitrary")),
        name="qkv_prep",
    )(*args)


def _gates(ab, alog, dtb):
    g = -jnp.exp(alog) * jax.nn.softplus(ab + dtb)
    beta = jax.nn.sigmoid(ab)
    return g, beta


def _delta_prompt_kernel(q_ref, k_ref, v_ref, z_ref, ab_ref, alog_ref, dtb_ref, ng_ref,
                         yb_ref, sout_ref, s_ref, *, n_heads, n_chunks):
    c = DN_CHUNK
    ci = pl.program_id(1)

    @pl.when(ci == 0)
    def _():
        s_ref[...] = jnp.zeros(s_ref.shape, F32)

    g_all, beta_all = _gates(ab_ref[...], alog_ref[...], dtb_ref[...])
    row = lax.broadcasted_iota(jnp.int32, (c, c), 0)
    col = lax.broadcasted_iota(jnp.int32, (c, c), 1)
    causal = row >= col
    strict = row > col
    eye = (row == col).astype(F32)
    gc = jnp.dot(causal.astype(F32), g_all, precision=lax.Precision.HIGHEST, preferred_element_type=F32)
    gct = gc.T
    n_double = (c - 1).bit_length() - 1
    for h in range(n_heads):
        sl = slice(h * GROUP, (h + 1) * GROUP)
        q, k, v = q_ref[:, sl], k_ref[:, sl], v_ref[:, sl]
        gcol = gc[:, h:h + 1]
        beta = beta_all[:, n_heads + h:n_heads + h + 1]
        decay = jnp.exp(jnp.where(causal, gcol - gct[h:h + 1, :], -jnp.inf))
        kb = k * beta
        kbf = k.astype(BF16)
        a = lax.dot_general(kb.astype(BF16), kbf, _NT, preferred_element_type=F32) * decay
        y = jnp.where(strict, -a, 0.0)
        x = eye + y
        for _ in range(n_double):
            y = _bdot(y, y)
            x = x + _bdot(x, y)
        egc = jnp.exp(gcol)
        sol = _bdot(x, jnp.concatenate([v * beta, kb * egc], axis=1))
        u_blk, w_blk = sol[:, :GROUP], sol[:, GROUP:]
        qk = lax.dot_general(q.astype(BF16), kbf, _NT, preferred_element_type=F32) * decay
        s_old = s_ref[h]
        r = _bdot(jnp.concatenate([w_blk, q * egc], axis=0), s_old)
        v_new = u_blk - r[:c]
        o = r[c:] + _bdot(qk, v_new)
        gl = gc[c - 1:c, h:h + 1]
        kt = k * jnp.exp(gl - gcol)
        s_ref[h] = s_old * jnp.exp(gl) + lax.dot_general(
            kt.astype(BF16), v_new.astype(BF16), _TN, preferred_element_type=F32)
        yb_ref[:, sl] = (_rms_norm(o, ng_ref[...]) * _silu(z_ref[:, sl])).astype(yb_ref.dtype)

    @pl.when(ci == n_chunks - 1)
    def _():
        sout_ref[0] = s_ref[...]


def _delta_prompt(q, k, v, h, z_col0, ab, alog, dtb, norm_g, n_seq, rows_per_seq):
    d_b = q.shape[1]
    n_heads = d_b // GROUP
    n_chunks = rows_per_seq // DN_CHUNK
    act = pl.BlockSpec((DN_CHUNK, d_b), lambda s, i: (s * n_chunks + i, 0))
    vec = pl.BlockSpec((1, LANES), lambda s, i: (0, 0))
    zb = z_col0 // d_b
    return pl.pallas_call(
        functools.partial(_delta_prompt_kernel, n_heads=n_heads, n_chunks=n_chunks),
        grid=(n_seq, n_chunks),
        in_specs=[act, act, act,
                  pl.BlockSpec((DN_CHUNK, d_b), lambda s, i: (s * n_chunks + i, zb)),
                  pl.BlockSpec((DN_CHUNK, LANES), lambda s, i: (s * n_chunks + i, 0)),
                  vec, vec, vec],
        out_specs=[act, pl.BlockSpec((1, n_heads, GROUP, GROUP), lambda s, i: (s, 0, 0, 0))],
        out_shape=[jax.ShapeDtypeStruct((n_seq * rows_per_seq, d_b), BF16),
                   jax.ShapeDtypeStruct((n_seq, n_heads, GROUP, GROUP), F32)],
        scratch_shapes=[pltpu.VMEM((n_heads, GROUP, GROUP), F32)],
        compiler_params=_params(("arbitrary", "arbitrary")),
        name="delta_prompt",
    )(q, k, v, h, ab, alog, dtb, norm_g)


def _delta_sample_kernel(q_ref, k_ref, v_ref, z_ref, ab_ref, alog_ref, dtb_ref, ng_ref, s0_ref,
                         yb_ref, s1_ref, lhs_ref, res_ref, kt_ref, vn_ref, gl_ref, *, n_heads, n_pos, bb):
    nt = n_pos
    gates = [_gates(ab_ref[t], alog_ref[...], dtb_ref[...]) for t in range(nt)]
    zeros_pad = jnp.zeros((bb, GROUP), F32)
    for h in range(n_heads):
        sl = slice(h * GROUP, (h + 1) * GROUP)
        q = [q_ref[t, :, sl] for t in range(nt)]
        k = [k_ref[t, :, sl] for t in range(nt)]
        v = [v_ref[t, :, sl] for t in range(nt)]
        beta = [gates[t][1][:, n_heads + h:n_heads + h + 1] for t in range(nt)]
        gc = []
        for t in range(nt):
            gt = gates[t][0][:, h:h + 1]
            gc.append(gt if t == 0 else gc[t - 1] + gt)
        kb = [k[t] * beta[t] for t in range(nt)]
        dec = [[jnp.exp(gc[t] - gc[s]) for s in range(t + 1)] for t in range(nt)]
        a = [[jnp.sum(kb[t] * k[s], axis=-1, keepdims=True) * dec[t][s] for s in range(t)] for t in range(nt)]
        egc = [jnp.exp(gc[t]) for t in range(nt)]
        u_blk, w_blk = [], []
        for t in range(nt):
            ut, wt = v[t] * beta[t], kb[t] * egc[t]
            for s in range(t):
                ut = ut - a[t][s] * u_blk[s]
                wt = wt - a[t][s] * w_blk[s]
            u_blk.append(ut)
            w_blk.append(wt)
        qk = [[jnp.sum(q[t] * k[s], axis=-1, keepdims=True) * dec[t][s] for s in range(t + 1)] for t in range(nt)]
        gl = gc[nt - 1]
        for t in range(nt):
            lhs_ref[:, t, :] = w_blk[t]
            lhs_ref[:, nt + t, :] = q[t] * egc[t]
        gl_ref[...] = jnp.broadcast_to(jnp.exp(gl), (bb, GROUP))

        def state_read(b, carry, h=h):
            res_ref[b] = _bdot(lhs_ref[b], s0_ref[b, h])
            return carry

        lax.fori_loop(0, bb, state_read, 0)
        v_new = [u_blk[t] - res_ref[:, t, :] for t in range(nt)]
        for t in range(nt):
            o = res_ref[:, nt + t, :]
            for s in range(t + 1):
                o = o + qk[t][s] * v_new[s]
            yb_ref[t, :, sl] = (_rms_norm(o, ng_ref[...]) * _silu(z_ref[t, :, sl])).astype(yb_ref.dtype)
            kt_ref[:, t, :] = k[t] * jnp.exp(gl - gc[t])
            vn_ref[:, t, :] = v_new[t]
            kt_ref[:, nt + t, :] = zeros_pad
            vn_ref[:, nt + t, :] = zeros_pad

        def state_write(b, carry, h=h):
            upd = lax.dot_general(kt_ref[b].astype(BF16), vn_ref[b].astype(BF16), _TN, preferred_element_type=F32)
            s1_ref[b, h] = s0_ref[b, h] * gl_ref[pl.ds(b, 1), :] + upd
            return carry

        lax.fori_loop(0, bb, state_write, 0)


def _delta_sample(q, k, v, h3, z_col0, ab, alog, dtb, norm_g, s0, bb):
    n_pos, nb, d_b = q.shape
    n_heads = d_b // GROUP
    assert 2 * n_pos == SUBLANES, "one state read packs [w | q] rows of a sequence into one 8-row tile"
    zb = z_col0 // d_b
    act = pl.BlockSpec((n_pos, bb, d_b), lambda i: (0, i, 0))
    vec = pl.BlockSpec((1, LANES), lambda i: (0, 0))
    st = pl.BlockSpec((bb, n_heads, GROUP, GROUP), lambda i: (i, 0, 0, 0))
    tile = pltpu.VMEM((bb, SUBLANES, GROUP), F32)
    return pl.pallas_call(
        functools.partial(_delta_sample_kernel, n_heads=n_heads, n_pos=n_pos, bb=bb),
        grid=(nb // bb,),
        in_specs=[act, act, act,
                  pl.BlockSpec((n_pos, bb, d_b), lambda i: (0, i, zb)),
                  pl.BlockSpec((n_pos, bb, LANES), lambda i: (0, i, 0)),
                  vec, vec, vec, st],
        out_specs=[act, st],
        out_shape=[jax.ShapeDtypeStruct((n_pos, nb, d_b), BF16),
                   jax.ShapeDtypeStruct(s0.shape, F32)],
        scratch_shapes=[tile, tile, tile, tile, pltpu.VMEM((bb, GROUP), F32)],
        compiler_params=_params(("arbitrary",)),
        name="delta_sample",
    )(q, k, v, h3, ab, alog, dtb, norm_g, s0)


def _accumulate(o_ref, part, kk):
    @pl.when(kk == 0)
    def _():
        o_ref[...] = part

    @pl.when(kk > 0)
    def _():
        o_ref[...] += part


def _finish_ln(o_ref, ob_ref, r_ref, g_ref, b_ref, alpha):
    y = _layer_norm(alpha * r_ref[...] + o_ref[...], g_ref[...], b_ref[...])
    o_ref[...] = y
    if ob_ref is not None:
        ob_ref[...] = y.astype(ob_ref.dtype)


def _out_proj_kernel(ya_ref, yb_ref, w_ref, r_ref, g_ref, b_ref, o_ref, ob_ref, *, ka, nk, alpha):
    kk = pl.program_id(1)

    @pl.when(kk < ka)
    def _():
        _accumulate(o_ref, jnp.dot(ya_ref[...], w_ref[...].astype(BF16), preferred_element_type=F32), kk)

    @pl.when(kk >= ka)
    def _():
        _accumulate(o_ref, jnp.dot(yb_ref[...], w_ref[...].astype(BF16), preferred_element_type=F32), kk)

    @pl.when(kk == nk - 1)
    def _():
        _finish_ln(o_ref, ob_ref, r_ref, g_ref, b_ref, alpha)


def _out_proj_ln(ya, yb, w, resid, g, b, alpha, tm, tk):
    m, d_a = ya.shape
    d = w.shape[1]
    ka, nk = d_a // tk, w.shape[0] // tk
    row = pl.BlockSpec((tm, d), lambda i, kk: (i, 0))
    vec = pl.BlockSpec((1, d), lambda i, kk: (0, 0))
    return pl.pallas_call(
        functools.partial(_out_proj_kernel, ka=ka, nk=nk, alpha=alpha),
        grid=(m // tm, nk),
        in_specs=[pl.BlockSpec((tm, tk), lambda i, kk: (i, jnp.minimum(kk, ka - 1))),
                  pl.BlockSpec((tm, tk), lambda i, kk: (i, jnp.maximum(kk - ka, 0))),
                  pl.BlockSpec((tk, d), lambda i, kk: (kk, 0)),
                  row, vec, vec],
        out_specs=[row, row],
        out_shape=[jax.ShapeDtypeStruct((m, d), F32), jax.ShapeDtypeStruct((m, d), BF16)],
        compiler_params=_params(("arbitrary", "arbitrary")),
        name="out_proj_ln1",
    )(ya, yb, w, resid, g.reshape(1, d), b.reshape(1, d))


def _down_kernel(x_ref, w_ref, r_ref, g_ref, b_ref, o_ref, *, nk, alpha):
    kk = pl.program_id(1)
    _accumulate(o_ref, jnp.dot(x_ref[...], w_ref[...].astype(BF16), preferred_element_type=F32), kk)

    @pl.when(kk == nk - 1)
    def _():
        _finish_ln(o_ref, None, r_ref, g_ref, b_ref, alpha)


def _down_ln(x, w, resid, g, b, alpha, tm, tk):
    m, kdim = x.shape
    d = w.shape[1]
    nk = kdim // tk
    row = pl.BlockSpec((tm, d), lambda i, kk: (i, 0))
    vec = pl.BlockSpec((1, d), lambda i, kk: (0, 0))
    return pl.pallas_call(
        functools.partial(_down_kernel, nk=nk, alpha=alpha),
        grid=(m // tm, nk),
        in_specs=[pl.BlockSpec((tm, tk), lambda i, kk: (i, kk)),
                  pl.BlockSpec((tk, d), lambda i, kk: (kk, 0)),
                  row, vec, vec],
        out_specs=row,
        out_shape=jax.ShapeDtypeStruct((m, d), F32),
        compiler_params=_params(("arbitrary", "arbitrary")),
        name="down_ln2",
    )(x, w, resid, g.reshape(1, d), b.reshape(1, d))


def _ffn_up_kernel(*refs, tm, shift, halo, has_buf):
    if has_buf:
        (x_ref, wg_ref, wv_ref, bg_ref, bv_ref, cwg_ref, cwv_ref, cbg_ref, cbv_ref,
         act_ref, tg_ref, tv_ref, wgb_ref, wvb_ref, xbuf_ref, carry_ref) = refs
        bufs = (bg_ref, bv_ref)
    else:
        (x_ref, wg_ref, wv_ref, cwg_ref, cwv_ref, cbg_ref, cbv_ref,
         act_ref, tg_ref, tv_ref, wgb_ref, wvb_ref, xbuf_ref, carry_ref) = refs
        bufs = (None, None)
    first_tile = jnp.logical_and(pl.program_id(1) == 0, pl.program_id(2) == 0)
    first = pl.program_id(2) == 0

    @pl.when(first_tile)
    def _():
        wgb_ref[...] = wg_ref[...].astype(BF16)
        wvb_ref[...] = wv_ref[...].astype(BF16)

    x = x_ref[...]
    outs = []
    parts = ((wgb_ref, cwg_ref, cbg_ref, tg_ref), (wvb_ref, cwv_ref, cbv_ref, tv_ref))
    for p, (wb_ref, cw_ref, cb_ref, t_ref) in enumerate(parts):
        @pl.when(first)
        def _(p=p):
            if has_buf:
                xbuf_ref[0:halo, :] = bufs[p][...]
            else:
                xbuf_ref[0:halo, :] = jnp.zeros((halo, xbuf_ref.shape[1]), F32)

        @pl.when(jnp.logical_not(first))
        def _(p=p):
            xbuf_ref[0:halo, :] = carry_ref[p]

        xbuf_ref[halo:halo + tm, :] = jnp.dot(x, wb_ref[...], preferred_element_type=F32)
        tail = xbuf_ref[tm:tm + halo, :]
        carry_ref[p] = tail
        t_ref[0] = tail
        outs.append(_conv_taps(xbuf_ref, cw_ref, tm, shift, halo, CONV_FFN) + cb_ref[...])
    act_ref[...] = (_silu(outs[0]) * outs[1]).astype(act_ref.dtype)


def _ffn_up(x, w_up, conv_w, conv_b, bufs, n_seq, rows_per_seq, tm, tn, shift):
    m, d = x.shape
    d_ff = w_up.shape[1] // 2
    nj = d_ff // tn
    has_buf = bufs is not None
    halo = (CONV_FFN - 1) * shift if has_buf else SUBLANES
    tiles = rows_per_seq // tm
    wspec = lambda off: pl.BlockSpec((d, tn), lambda j, s, i: (0, off + j))
    cwspec = lambda off: pl.BlockSpec((CONV_FFN, tn), lambda j, s, i: (0, off + j))
    cbspec = lambda off: pl.BlockSpec((1, tn), lambda j, s, i: (0, off + j))
    tspec = pl.BlockSpec((1, halo, tn), lambda j, s, i: (s, 0, j))
    specs = [pl.BlockSpec((tm, d), lambda j, s, i: (s * tiles + i, 0)), wspec(0), wspec(nj)]
    args = [x, w_up, w_up]
    if has_buf:
        specs += [pl.BlockSpec((halo, tn), lambda j, s, i, off=off: (s, off + j)) for off in (0, nj)]
        args += [bufs, bufs]
    specs += [cwspec(0), cwspec(nj), cbspec(0), cbspec(nj)]
    cb2 = conv_b.reshape(1, 2 * d_ff)
    args += [conv_w, conv_w, cb2, cb2]
    return pl.pallas_call(
        functools.partial(_ffn_up_kernel, tm=tm, shift=shift, halo=halo, has_buf=has_buf),
        grid=(nj, n_seq, tiles),
        in_specs=specs,
        out_specs=[pl.BlockSpec((tm, tn), lambda j, s, i: (s * tiles + i, j)), tspec, tspec],
        out_shape=[jax.ShapeDtypeStruct((m, d_ff), BF16),
                   jax.ShapeDtypeStruct((n_seq, halo, d_ff), F32),
                   jax.ShapeDtypeStruct((n_seq, halo, d_ff), F32)],
        scratch_shapes=[pltpu.VMEM((d, tn), BF16), pltpu.VMEM((d, tn), BF16),
                        pltpu.VMEM((halo + tm, tn), F32), pltpu.VMEM((2, halo, tn), F32)],
        compiler_params=_params(("arbitrary", "arbitrary", "arbitrary")),
        name="ffn_up",
    )(*args)


def _pad_lanes(vec, offset=0):
    out = jnp.zeros((1, LANES), F32)
    return out.at[0, offset:offset + vec.shape[0]].set(vec.astype(F32))


def _layer(x2, lw, alpha, n_seq, rows_per_seq, shift, state):
    (w_in, conv_qkv_w, a_log, dt_bias, norm_g, gm_ln_g, gm_ln_b, gm_ws, gm_bs, gm_out_g, w_out,
     ln1_g, ln1_b, w_up, conv_ffn_w, conv_ffn_b, w_down, ln2_g, ln2_b) = lw
    m, d_model = x2.shape
    n_groups = gm_ws.shape[0]
    d_a = n_groups * GROUP
    d_qkv = conv_qkv_w.shape[1]
    d_b = d_qkv // 3
    n_heads = d_b // GROUP
    main_cols = 2 * d_a + d_qkv + d_b
    fresh = state is None
    tm = min(m, 1024)

    h = _matmul(x2, w_in, main_cols, tm, 512)
    w_ab = jnp.pad(w_in[:, main_cols:], ((0, 0), (0, LANES - 2 * n_heads)))
    ab = _matmul(x2, w_ab, LANES, tm, LANES)
    alog, dtb, ng = _pad_lanes(a_log), _pad_lanes(dt_bias), norm_g.reshape(1, GROUP).astype(F32)

    if fresh:
        ya = _chunk_mlp_prompt(h, gm_ln_g, gm_ln_b, gm_ws, gm_bs, gm_out_g)
        v_rows = None
        q, k, v = _qkv_prep(h, conv_qkv_w, None, n_seq, rows_per_seq, 512, shift, 2 * d_a)
        yb, s_new = _delta_prompt(q, k, v, h, 2 * d_a + d_qkv, ab, alog, dtb, ng, n_seq, rows_per_seq)
        ffn_bufs = None
        tm_ffn = 512
    else:
        s_dn, buf_qkv, buf_ffn = state
        n_pos = rows_per_seq // shift
        h3 = h.reshape(n_pos, shift, main_cols)
        ya3, v_rows = _chunk_mlp_sample(h3, gm_ln_g, gm_ln_b, gm_ws, gm_bs, gm_out_g)
        ya = ya3.reshape(m, d_a)
        q, k, v = _qkv_prep(h, conv_qkv_w, buf_qkv, n_seq, rows_per_seq, rows_per_seq, shift, 2 * d_a)
        r3 = lambda t: t.reshape(n_pos, shift, t.shape[-1])
        yb3, s_new = _delta_sample(r3(q), r3(k), r3(v), h3, 2 * d_a + d_qkv, r3(ab), alog, dtb, ng, s_dn, 8)
        yb = yb3.reshape(m, d_b)
        ffn_bufs = buf_ffn
        tm_ffn = rows_per_seq

    x1, x1b = _out_proj_ln(ya, yb, w_out, x2, ln1_g, ln1_b, alpha, min(m, 512), 512)
    act, tail_g, tail_v = _ffn_up(x1b, w_up, conv_ffn_w, conv_ffn_b, ffn_bufs, n_seq, rows_per_seq, tm_ffn, 512, shift)
    y = _down_ln(act, w_down, x1, ln2_g, ln2_b, alpha, min(m, 512), 512)
    return y, s_new, h, tail_g, tail_v, v_rows


def kernel(x_prompt, x_sample, state_dn, state_conv_qkv, state_conv_ffn, w_in, conv_qkv_w, dn_a_log, dn_dt_bias, dn_norm_g, gm_ln_g, gm_ln_b, gm_ws, gm_bs, gm_out_g, w_out, ln1_g, ln1_b, w_up, conv_ffn_w, conv_ffn_b, w_down, ln2_g, ln2_b):
    depth = w_in.shape[0]
    bp, lp, d_model = x_prompt.shape
    bs, ls, _ = x_sample.shape
    d_qkv = conv_qkv_w.shape[-1]
    d_a = gm_ws.shape[1] * GROUP
    alpha = (2.0 * depth) ** 0.25

    hp = x_prompt.reshape(bp * lp, d_model)
    hs = jnp.transpose(x_sample, (1, 0, 2)).reshape(ls * bs, d_model)
    outs = [[] for _ in range(7)]
    for l in range(depth):
        lw = tuple(t[l] for t in (w_in, conv_qkv_w, dn_a_log, dn_dt_bias, dn_norm_g, gm_ln_g, gm_ln_b, gm_ws,
                                  gm_bs, gm_out_g, w_out, ln1_g, ln1_b, w_up, conv_ffn_w, conv_ffn_b, w_down,
                                  ln2_g, ln2_b))
        hp, s_p, h_p, tg_p, tv_p, _ = _layer(hp, lw, alpha, bp, lp, 1, None)
        h_p3 = h_p.reshape(bp, lp, -1)
        cq_p = h_p3[:, lp - (CONV_QKV - 1):, 2 * d_a:2 * d_a + d_qkv]
        n_tail = tg_p.shape[1]
        cf_p = jnp.concatenate([tg_p, tv_p], axis=-1)[:, n_tail - (CONV_FFN - 1):]
        bq = jnp.transpose(state_conv_qkv[l], (1, 0, 2)).reshape((CONV_QKV - 1) * bs, d_qkv)
        bf = jnp.transpose(state_conv_ffn[l], (1, 0, 2)).reshape((CONV_FFN - 1) * bs, -1)
        hs, s_s, h_s, tg_s, tv_s, v_s = _layer(hs, lw, alpha, 1, ls * bs, bs, (state_dn[l], bq, bf))
        h_s3 = h_s.reshape(ls, bs, -1)
        cq_s = jnp.transpose(h_s3[ls - (CONV_QKV - 1):, :, 2 * d_a:2 * d_a + d_qkv], (1, 0, 2))
        cf_s = jnp.transpose(jnp.concatenate([tg_s, tv_s], axis=-1).reshape(CONV_FFN - 1, bs, -1), (1, 0, 2))
        for acc, val in zip(outs, (s_p, cq_p, cf_p, s_s, cq_s, cf_s, jnp.transpose(v_s, (1, 0, 2)))):
            acc.append(val)
    y_p = hp.reshape(bp, lp, d_model)
    y_s = jnp.transpose(hs.reshape(ls, bs, d_model), (1, 0, 2))
    return (y_p, y_s) + tuple(jnp.stack(o) for o in outs)
```

```python
import functools

import jax
import jax.numpy as jnp
from jax import lax
from jax.experimental import pallas as pl
from jax.experimental.pallas import tpu as pltpu

F32 = jnp.float32
BF16 = jnp.bfloat16

LANES = 128
SUBLANES = 8
VMEM_LIMIT = 56 * 1024 * 1024

GROUP = 128
MLP_CHUNK = 128
DN_CHUNK = 64
CONV_QKV = 4
CONV_FFN = 3
EPS = 1e-5

_NT = (((1,), (1,)), ((), ()))
_TN = (((0,), (0,)), ((), ()))


def _params(sem):
    return pltpu.CompilerParams(dimension_semantics=sem, vmem_limit_bytes=VMEM_LIMIT)


def _bdot(a, b):
    return jnp.dot(a.astype(BF16), b.astype(BF16), preferred_element_type=F32)


def _gelu(x):
    return 0.5 * x * (1.0 + lax.erf(x * (0.5 ** 0.5)))


def _silu(x):
    return x * jax.nn.sigmoid(x)


def _layer_norm(x, g, b):
    mu = jnp.mean(x, axis=-1, keepdims=True)
    xc = x - mu
    var = jnp.mean(xc * xc, axis=-1, keepdims=True)
    return xc * lax.rsqrt(var + EPS) * g + b


def _rms_norm(x, g):
    ms = jnp.mean(x * x, axis=-1, keepdims=True)
    return x * lax.rsqrt(ms + EPS) * g


def _mm_kernel(x_ref, w_ref, o_ref, xb_ref):
    @pl.when(pl.program_id(1) == 0)
    def _():
        xb_ref[...] = x_ref[...].astype(BF16)

    o_ref[...] = jnp.dot(xb_ref[...], w_ref[...].astype(BF16), preferred_element_type=F32)


def _matmul(x, w, n_cols, tm, tn):
    m, k = x.shape
    return pl.pallas_call(
        _mm_kernel,
        grid=(m // tm, n_cols // tn),
        in_specs=[pl.BlockSpec((tm, k), lambda i, j: (i, 0)),
                  pl.BlockSpec((k, tn), lambda i, j: (0, j))],
        out_specs=pl.BlockSpec((tm, tn), lambda i, j: (i, j)),
        out_shape=jax.ShapeDtypeStruct((m, n_cols), F32),
        scratch_shapes=[pltpu.VMEM((tm, k), BF16)],
        compiler_params=_params(("arbitrary", "arbitrary")),
        name="proj",
    )(x, w)


def _cmlp_prompt_kernel(u_ref, v_ref, lng_ref, lnb_ref, ws_ref, bst_ref, og_ref, ya_ref, *, n_groups):
    row = lax.broadcasted_iota(jnp.int32, (MLP_CHUNK, MLP_CHUNK), 0)
    col = lax.broadcasted_iota(jnp.int32, (MLP_CHUNK, MLP_CHUNK), 1)
    for g in range(n_groups):
        sl = slice(g * GROUP, (g + 1) * GROUP)
        w = jnp.where(row >= col, ws_ref[g], 0.0)
        vh = _layer_norm(_gelu(v_ref[:, sl]), lng_ref[:, sl], lnb_ref[:, sl])
        mixed = _bdot(w, vh) + bst_ref[:, g:g + 1]
        y = _gelu(u_ref[:, sl]) * mixed
        ya_ref[:, sl] = _rms_norm(y, og_ref[:, sl]).astype(ya_ref.dtype)


def _chunk_mlp_prompt(h, ln_g, ln_b, ws, bs, out_g):
    m = h.shape[0]
    n_groups, d_a = ws.shape[0], ws.shape[0] * GROUP
    vec = pl.BlockSpec((1, d_a), lambda i: (0, 0))
    return pl.pallas_call(
        functools.partial(_cmlp_prompt_kernel, n_groups=n_groups),
        grid=(m // MLP_CHUNK,),
        in_specs=[pl.BlockSpec((MLP_CHUNK, d_a), lambda i: (i, 0)),
                  pl.BlockSpec((MLP_CHUNK, d_a), lambda i: (i, 1)),
                  vec, vec,
                  pl.BlockSpec((n_groups, MLP_CHUNK, MLP_CHUNK), lambda i: (0, 0, 0)),
                  pl.BlockSpec((MLP_CHUNK, n_groups), lambda i: (0, 0)),
                  vec],
        out_specs=pl.BlockSpec((MLP_CHUNK, d_a), lambda i: (i, 0)),
        out_shape=jax.ShapeDtypeStruct((m, d_a), BF16),
        compiler_params=_params(("arbitrary",)),
        name="chunk_mlp_prompt",
    )(h, h, ln_g.reshape(1, d_a), ln_b.reshape(1, d_a), ws, bs.T, out_g.reshape(1, d_a))


def _cmlp_sample_kernel(u_ref, v_ref, lng_ref, lnb_ref, wv_ref, bv_ref, og_ref, ya_ref, vr_ref, *, n_pos):
    vh = []
    for t in range(n_pos):
        vt = _layer_norm(_gelu(v_ref[t]), lng_ref[0], lnb_ref[0])
        vr_ref[t] = vt
        vh.append(vt)
    for t in range(n_pos):
        mixed = wv_ref[t, 0:1, :] * vh[0]
        for s in range(1, t + 1):
            mixed = mixed + wv_ref[t, s:s + 1, :] * vh[s]
        mixed = mixed + bv_ref[0, t:t + 1, :]
        y = _gelu(u_ref[t]) * mixed
        ya_ref[t] = _rms_norm(y, og_ref[0]).astype(ya_ref.dtype)


def _chunk_mlp_sample(h3, ln_g, ln_b, ws, bs, out_g):
    n_pos, nb, _ = h3.shape
    n_groups = ws.shape[0]
    d_a = n_groups * GROUP
    wv = jnp.repeat(jnp.transpose(ws[:, :n_pos, :n_pos], (1, 2, 0)), GROUP, axis=-1)
    bv = jnp.repeat(bs[:, :n_pos].T, GROUP, axis=-1).reshape(1, n_pos, d_a)
    vec = pl.BlockSpec((1, 1, GROUP), lambda g: (0, 0, g))
    act = pl.BlockSpec((n_pos, nb, GROUP), lambda g: (0, 0, g))
    return pl.pallas_call(
        functools.partial(_cmlp_sample_kernel, n_pos=n_pos),
        grid=(n_groups,),
        in_specs=[act,
                  pl.BlockSpec((n_pos, nb, GROUP), lambda g: (0, 0, n_groups + g)),
                  vec, vec,
                  pl.BlockSpec((n_pos, n_pos, GROUP), lambda g: (0, 0, g)),
                  pl.BlockSpec((1, n_pos, GROUP), lambda g: (0, 0, g)),
                  vec],
        out_specs=[act, act],
        out_shape=[jax.ShapeDtypeStruct((n_pos, nb, d_a), BF16),
                   jax.ShapeDtypeStruct((n_pos, nb, d_a), F32)],
        compiler_params=_params(("arbitrary",)),
        name="chunk_mlp_sample",
    )(h3, h3, ln_g.reshape(1, 1, d_a), ln_b.reshape(1, 1, d_a), wv, bv, out_g.reshape(1, 1, d_a))


def _conv_taps(xbuf_ref, w_ref, tm, shift, halo, width):
    acc = None
    for j in range(width):
        off = halo - (width - 1 - j) * shift
        term = w_ref[j:j + 1, :] * xbuf_ref[off:off + tm, :]
        acc = term if acc is None else acc + term
    return acc


def _qkv_prep_kernel(*refs, tm, shift, halo, n_heads, has_buf):
    if has_buf:
        (xq_ref, xk_ref, xv_ref, bq_ref, bk_ref, bv_ref, wq_ref, wk_ref, wv_ref,
         q_ref, k_ref, v_ref, xbuf_ref, carry_ref) = refs
        bufs = (bq_ref, bk_ref, bv_ref)
    else:
        (xq_ref, xk_ref, xv_ref, wq_ref, wk_ref, wv_ref,
         q_ref, k_ref, v_ref, xbuf_ref, carry_ref) = refs
        bufs = (None, None, None)
    first = pl.program_id(1) == 0
    parts = ((xq_ref, wq_ref, q_ref, GROUP ** -0.5), (xk_ref, wk_ref, k_ref, 1.0), (xv_ref, wv_ref, v_ref, None))
    for p, (x_ref, w_ref, o_ref, scale) in enumerate(parts):
        @pl.when(first)
        def _(p=p):
            if has_buf:
                xbuf_ref[0:halo, :] = bufs[p][...]
            else:
                xbuf_ref[0:halo, :] = jnp.zeros((halo, xbuf_ref.shape[1]), F32)

        @pl.when(jnp.logical_not(first))
        def _(p=p):
            xbuf_ref[0:halo, :] = carry_ref[p]

        xbuf_ref[halo:halo + tm, :] = x_ref[...]
        carry_ref[p] = xbuf_ref[tm:tm + halo, :]
        y = _silu(_conv_taps(xbuf_ref, w_ref, tm, shift, halo, CONV_QKV))
        if scale is None:
            o_ref[...] = y
        else:
            for h in range(n_heads):
                sl = slice(h * GROUP, (h + 1) * GROUP)
                yh = y[:, sl]
                yn = yh * lax.rsqrt(jnp.sum(yh * yh, axis=-1, keepdims=True) + 1e-6)
                o_ref[:, sl] = yn * scale if scale != 1.0 else yn


def _qkv_prep(h, conv_w, bufs, n_seq, rows_per_seq, tm, shift, col0):
    d_b = conv_w.shape[1] // 3
    n_heads = d_b // GROUP
    has_buf = bufs is not None
    halo = (CONV_QKV - 1) * shift if has_buf else SUBLANES
    tiles = rows_per_seq // tm
    cb = col0 // d_b
    x_specs = [pl.BlockSpec((tm, d_b), lambda s, i, c=c: (s * tiles + i, cb + c)) for c in range(3)]
    w_specs = [pl.BlockSpec((CONV_QKV, d_b), lambda s, i, c=c: (0, c)) for c in range(3)]
    o_spec = pl.BlockSpec((tm, d_b), lambda s, i: (s * tiles + i, 0))
    args, specs = [h, h, h], list(x_specs)
    if has_buf:
        specs += [pl.BlockSpec((halo, d_b), lambda s, i, c=c: (s, c)) for c in range(3)]
        args += [bufs, bufs, bufs]
    specs += w_specs
    args += [conv_w, conv_w, conv_w]
    m = n_seq * rows_per_seq
    return pl.pallas_call(
        functools.partial(_qkv_prep_kernel, tm=tm, shift=shift, halo=halo, n_heads=n_heads, has_buf=has_buf),
        grid=(n_seq, tiles),
        in_specs=specs,
        out_specs=[o_spec, o_spec, o_spec],
        out_shape=[jax.ShapeDtypeStruct((m, d_b), F32)] * 3,
        scratch_shapes=[pltpu.VMEM((halo + tm, d_b), F32), pltpu.VMEM((3, halo, d_b), F32)],
        compiler_params=_params(("arbitrary", "arbitrary")),
        name="qkv_prep",
    )(*args)


def _gates(ab, alog, dtb):
    g = -jnp.exp(alog) * jax.nn.softplus(ab + dtb)
    beta = jax.nn.sigmoid(ab)
    return g, beta


def _delta_prompt_kernel(q_ref, k_ref, v_ref, z_ref, ab_ref, alog_ref, dtb_ref, ng_ref,
                         yb_ref, sout_ref, s_ref, *, n_heads, n_chunks):
    c = DN_CHUNK
    ci = pl.program_id(1)

    @pl.when(ci == 0)
    def _():
        s_ref[...] = jnp.zeros(s_ref.shape, F32)

    g_all, beta_all = _gates(ab_ref[...], alog_ref[...], dtb_ref[...])
    row = lax.broadcasted_iota(jnp.int32, (c, c), 0)
    col = lax.broadcasted_iota(jnp.int32, (c, c), 1)
    causal = row >= col
    strict = row > col
    eye = (row == col).astype(F32)
    gc = jnp.dot(causal.astype(F32), g_all, precision=lax.Precision.HIGHEST, preferred_element_type=F32)
    gct = gc.T
    n_double = (c - 1).bit_length() - 1
    hs = range(n_heads)
    sls = [slice(h * GROUP, (h + 1) * GROUP) for h in hs]
    q = [q_ref[:, sl] for sl in sls]
    k = [k_ref[:, sl] for sl in sls]
    gcol = [gc[:, h:h + 1] for h in hs]
    beta = [beta_all[:, n_heads + h:n_heads + h + 1] for h in hs]
    decay = [jnp.exp(jnp.where(causal, gcol[h] - gct[h:h + 1, :], -jnp.inf)) for h in hs]
    kb = [k[h] * beta[h] for h in hs]
    kbf = [k[h].astype(BF16) for h in hs]
    a = [lax.dot_general(kb[h].astype(BF16), kbf[h], _NT, preferred_element_type=F32) * decay[h] for h in hs]
    y = [jnp.where(strict, -a[h], 0.0) for h in hs]
    x = [eye + y[h] for h in hs]
    for _ in range(n_double):
        y = [_bdot(y[h], y[h]) for h in hs]
        x = [x[h] + _bdot(x[h], y[h]) for h in hs]
    egc = [jnp.exp(gcol[h]) for h in hs]
    sol = [_bdot(x[h], jnp.concatenate([v_ref[:, sls[h]] * beta[h], kb[h] * egc[h]], axis=1)) for h in hs]
    qk = [lax.dot_general(q[h].astype(BF16), kbf[h], _NT, preferred_element_type=F32) * decay[h] for h in hs]
    s_old = [s_ref[h] for h in hs]
    r = [_bdot(jnp.concatenate([sol[h][:, GROUP:], q[h] * egc[h]], axis=0), s_old[h]) for h in hs]
    v_new = [sol[h][:, :GROUP] - r[h][:c] for h in hs]
    o = [r[h][c:] + _bdot(qk[h], v_new[h]) for h in hs]
    gl = [gc[c - 1:c, h:h + 1] for h in hs]
    upd = [lax.dot_general((k[h] * jnp.exp(gl[h] - gcol[h])).astype(BF16), v_new[h].astype(BF16), _TN,
                           preferred_element_type=F32) for h in hs]
    for h in hs:
        s_ref[h] = s_old[h] * jnp.exp(gl[h]) + upd[h]
        yb_ref[:, sls[h]] = (_rms_norm(o[h], ng_ref[...]) * _silu(z_ref[:, sls[h]])).astype(yb_ref.dtype)

    @pl.when(ci == n_chunks - 1)
    def _():
        sout_ref[0] = s_ref[...]


def _delta_prompt(q, k, v, h, z_col0, ab, alog, dtb, norm_g, n_seq, rows_per_seq):
    d_b = q.shape[1]
    n_heads = d_b // GROUP
    n_chunks = rows_per_seq // DN_CHUNK
    act = pl.BlockSpec((DN_CHUNK, d_b), lambda s, i: (s * n_chunks + i, 0))
    vec = pl.BlockSpec((1, LANES), lambda s, i: (0, 0))
    zb = z_col0 // d_b
    return pl.pallas_call(
        functools.partial(_delta_prompt_kernel, n_heads=n_heads, n_chunks=n_chunks),
        grid=(n_seq, n_chunks),
        in_specs=[act, act, act,
                  pl.BlockSpec((DN_CHUNK, d_b), lambda s, i: (s * n_chunks + i, zb)),
                  pl.BlockSpec((DN_CHUNK, LANES), lambda s, i: (s * n_chunks + i, 0)),
                  vec, vec, vec],
        out_specs=[act, pl.BlockSpec((1, n_heads, GROUP, GROUP), lambda s, i: (s, 0, 0, 0))],
        out_shape=[jax.ShapeDtypeStruct((n_seq * rows_per_seq, d_b), BF16),
                   jax.ShapeDtypeStruct((n_seq, n_heads, GROUP, GROUP), F32)],
        scratch_shapes=[pltpu.VMEM((n_heads, GROUP, GROUP), F32)],
        compiler_params=_params(("arbitrary", "arbitrary")),
        name="delta_prompt",
    )(q, k, v, h, ab, alog, dtb, norm_g)


def _delta_sample_kernel(q_ref, k_ref, v_ref, z_ref, ab_ref, alog_ref, dtb_ref, ng_ref, s0_ref,
                         yb_ref, s1_ref, lhs_ref, res_ref, kt_ref, vn_ref, gl_ref, *, n_heads, n_pos, bb):
    nt = n_pos
    hs = range(n_heads)
    sls = [slice(h * GROUP, (h + 1) * GROUP) for h in hs]
    gates = [_gates(ab_ref[t], alog_ref[...], dtb_ref[...]) for t in range(nt)]
    zeros_pad = jnp.zeros((bb, GROUP), F32)
    u_all, qk_all, gc_all = [], [], []
    for h in hs:
        q = [q_ref[t, :, sls[h]] for t in range(nt)]
        k = [k_ref[t, :, sls[h]] for t in range(nt)]
        beta = [gates[t][1][:, n_heads + h:n_heads + h + 1] for t in range(nt)]
        gc = []
        for t in range(nt):
            gt = gates[t][0][:, h:h + 1]
            gc.append(gt if t == 0 else gc[t - 1] + gt)
        kb = [k[t] * beta[t] for t in range(nt)]
        dec = [[jnp.exp(gc[t] - gc[s]) for s in range(t + 1)] for t in range(nt)]
        a = [[jnp.sum(kb[t] * k[s], axis=-1, keepdims=True) * dec[t][s] for s in range(t)] for t in range(nt)]
        egc = [jnp.exp(gc[t]) for t in range(nt)]
        u_blk, w_blk = [], []
        for t in range(nt):
            ut, wt = v_ref[t, :, sls[h]] * beta[t], kb[t] * egc[t]
            for s in range(t):
                ut = ut - a[t][s] * u_blk[s]
                wt = wt - a[t][s] * w_blk[s]
            u_blk.append(ut)
            w_blk.append(wt)
        qk_all.append([[jnp.sum(q[t] * k[s], axis=-1, keepdims=True) * dec[t][s] for s in range(t + 1)]
                       for t in range(nt)])
        u_all.append(u_blk)
        gc_all.append(gc)
        for t in range(nt):
            lhs_ref[h, :, t, :] = w_blk[t]
            lhs_ref[h, :, nt + t, :] = q[t] * egc[t]
        gl_ref[h] = jnp.broadcast_to(jnp.exp(gc[nt - 1]), (bb, GROUP))

    for h in hs:
        for b in range(bb):
            res_ref[h, b] = _bdot(lhs_ref[h, b], s0_ref[b, h])

    for h in hs:
        gc, gl = gc_all[h], gc_all[h][nt - 1]
        v_new = [u_all[h][t] - res_ref[h, :, t, :] for t in range(nt)]
        for t in range(nt):
            o = res_ref[h, :, nt + t, :]
            for s in range(t + 1):
                o = o + qk_all[h][t][s] * v_new[s]
            yb_ref[t, :, sls[h]] = (_rms_norm(o, ng_ref[...]) * _silu(z_ref[t, :, sls[h]])).astype(yb_ref.dtype)
            kt_ref[h, :, t, :] = k_ref[t, :, sls[h]] * jnp.exp(gl - gc[t])
            vn_ref[h, :, t, :] = v_new[t]
            kt_ref[h, :, nt + t, :] = zeros_pad
            vn_ref[h, :, nt + t, :] = zeros_pad

    for h in hs:
        for b in range(bb):
            upd = lax.dot_general(kt_ref[h, b].astype(BF16), vn_ref[h, b].astype(BF16), _TN,
                                  preferred_element_type=F32)
            s1_ref[b, h] = s0_ref[b, h] * gl_ref[h, b:b + 1, :] + upd


def _delta_sample(q, k, v, h3, z_col0, ab, alog, dtb, norm_g, s0, bb):
    n_pos, nb, d_b = q.shape
    n_heads = d_b // GROUP
    assert 2 * n_pos == SUBLANES, "one state read packs [w | q] rows of a sequence into one 8-row tile"
    zb = z_col0 // d_b
    act = pl.BlockSpec((n_pos, bb, d_b), lambda i: (0, i, 0))
    vec = pl.BlockSpec((1, LANES), lambda i: (0, 0))
    st = pl.BlockSpec((bb, n_heads, GROUP, GROUP), lambda i: (i, 0, 0, 0))
    tile = pltpu.VMEM((n_heads, bb, SUBLANES, GROUP), F32)
    return pl.pallas_call(
        functools.partial(_delta_sample_kernel, n_heads=n_heads, n_pos=n_pos, bb=bb),
        grid=(nb // bb,),
        in_specs=[act, act, act,
                  pl.BlockSpec((n_pos, bb, d_b), lambda i: (0, i, zb)),
                  pl.BlockSpec((n_pos, bb, LANES), lambda i: (0, i, 0)),
                  vec, vec, vec, st],
        out_specs=[act, st],
        out_shape=[jax.ShapeDtypeStruct((n_pos, nb, d_b), BF16),
                   jax.ShapeDtypeStruct(s0.shape, F32)],
        scratch_shapes=[tile, tile, tile, tile, pltpu.VMEM((n_heads, bb, GROUP), F32)],
        compiler_params=_params(("arbitrary",)),
        name="delta_sample",
    )(q, k, v, h3, ab, alog, dtb, norm_g, s0)


def _accumulate(o_ref, part, kk):
    @pl.when(kk == 0)
    def _():
        o_ref[...] = part

    @pl.when(kk > 0)
    def _():
        o_ref[...] += part


def _finish_ln(o_ref, ob_ref, r_ref, g_ref, b_ref, alpha):
    y = _layer_norm(alpha * r_ref[...] + o_ref[...], g_ref[...], b_ref[...])
    o_ref[...] = y
    if ob_ref is not None:
        ob_ref[...] = y.astype(ob_ref.dtype)


def _out_proj_kernel(ya_ref, yb_ref, w_ref, r_ref, g_ref, b_ref, o_ref, ob_ref, *, ka, nk, alpha):
    kk = pl.program_id(1)

    @pl.when(kk < ka)
    def _():
        _accumulate(o_ref, jnp.dot(ya_ref[...], w_ref[...].astype(BF16), preferred_element_type=F32), kk)

    @pl.when(kk >= ka)
    def _():
        _accumulate(o_ref, jnp.dot(yb_ref[...], w_ref[...].astype(BF16), preferred_element_type=F32), kk)

    @pl.when(kk == nk - 1)
    def _():
        _finish_ln(o_ref, ob_ref, r_ref, g_ref, b_ref, alpha)


def _out_proj_ln(ya, yb, w, resid, g, b, alpha, tm, tk):
    m, d_a = ya.shape
    d = w.shape[1]
    ka, nk = d_a // tk, w.shape[0] // tk
    row = pl.BlockSpec((tm, d), lambda i, kk: (i, 0))
    vec = pl.BlockSpec((1, d), lambda i, kk: (0, 0))
    return pl.pallas_call(
        functools.partial(_out_proj_kernel, ka=ka, nk=nk, alpha=alpha),
        grid=(m // tm, nk),
        in_specs=[pl.BlockSpec((tm, tk), lambda i, kk: (i, jnp.minimum(kk, ka - 1))),
                  pl.BlockSpec((tm, tk), lambda i, kk: (i, jnp.maximum(kk - ka, 0))),
                  pl.BlockSpec((tk, d), lambda i, kk: (kk, 0)),
                  row, vec, vec],
        out_specs=[row, row],
        out_shape=[jax.ShapeDtypeStruct((m, d), F32), jax.ShapeDtypeStruct((m, d), BF16)],
        compiler_params=_params(("arbitrary", "arbitrary")),
        name="out_proj_ln1",
    )(ya, yb, w, resid, g.reshape(1, d), b.reshape(1, d))


def _down_kernel(x_ref, w_ref, r_ref, g_ref, b_ref, o_ref, *, nk, alpha):
    kk = pl.program_id(1)
    _accumulate(o_ref, jnp.dot(x_ref[...], w_ref[...].astype(BF16), preferred_element_type=F32), kk)

    @pl.when(kk == nk - 1)
    def _():
        _finish_ln(o_ref, None, r_ref, g_ref, b_ref, alpha)


def _down_ln(x, w, resid, g, b, alpha, tm, tk):
    m, kdim = x.shape
    d = w.shape[1]
    nk = kdim // tk
    row = pl.BlockSpec((tm, d), lambda i, kk: (i, 0))
    vec = pl.BlockSpec((1, d), lambda i, kk: (0, 0))
    return pl.pallas_call(
        functools.partial(_down_kernel, nk=nk, alpha=alpha),
        grid=(m // tm, nk),
        in_specs=[pl.BlockSpec((tm, tk), lambda i, kk: (i, kk)),
                  pl.BlockSpec((tk, d), lambda i, kk: (kk, 0)),
                  row, vec, vec],
        out_specs=row,
        out_shape=jax.ShapeDtypeStruct((m, d), F32),
        compiler_params=_params(("arbitrary", "arbitrary")),
        name="down_ln2",
    )(x, w, resid, g.reshape(1, d), b.reshape(1, d))


def _ffn_up_kernel(*refs, tm, shift, halo, has_buf):
    if has_buf:
        (x_ref, wg_ref, wv_ref, bg_ref, bv_ref, cwg_ref, cwv_ref, cbg_ref, cbv_ref,
         act_ref, tg_ref, tv_ref, xg_ref, xv_ref, carry_ref) = refs
        bufs = (bg_ref, bv_ref)
    else:
        (x_ref, wg_ref, wv_ref, cwg_ref, cwv_ref, cbg_ref, cbv_ref,
         act_ref, tg_ref, tv_ref, xg_ref, xv_ref, carry_ref) = refs
        bufs = (None, None)
    first = pl.program_id(2) == 0

    @pl.when(first)
    def _():
        for p, xbuf_ref in enumerate((xg_ref, xv_ref)):
            if has_buf:
                xbuf_ref[0:halo, :] = bufs[p][...]
            else:
                xbuf_ref[0:halo, :] = jnp.zeros((halo, xbuf_ref.shape[1]), F32)

    @pl.when(jnp.logical_not(first))
    def _():
        xg_ref[0:halo, :] = carry_ref[0]
        xv_ref[0:halo, :] = carry_ref[1]

    x = x_ref[...]
    outs = []
    parts = ((wg_ref, cwg_ref, cbg_ref, tg_ref, xg_ref), (wv_ref, cwv_ref, cbv_ref, tv_ref, xv_ref))
    for p, (w_ref, cw_ref, cb_ref, t_ref, xbuf_ref) in enumerate(parts):
        xbuf_ref[halo:halo + tm, :] = jnp.dot(x, w_ref[...], preferred_element_type=F32)
        tail = xbuf_ref[tm:tm + halo, :]
        carry_ref[p] = tail
        t_ref[0] = tail
        outs.append(_conv_taps(xbuf_ref, cw_ref, tm, shift, halo, CONV_FFN) + cb_ref[...])
    act_ref[...] = (_silu(outs[0]) * outs[1]).astype(act_ref.dtype)


def _ffn_up(x, w_up, conv_w, conv_b, bufs, n_seq, rows_per_seq, tm, tn, shift):
    m, d = x.shape
    d_ff = w_up.shape[1] // 2
    nj = d_ff // tn
    has_buf = bufs is not None
    halo = (CONV_FFN - 1) * shift if has_buf else SUBLANES
    tiles = rows_per_seq // tm
    wspec = lambda off: pl.BlockSpec((d, tn), lambda j, s, i: (0, off + j))
    cwspec = lambda off: pl.BlockSpec((CONV_FFN, tn), lambda j, s, i: (0, off + j))
    cbspec = lambda off: pl.BlockSpec((1, tn), lambda j, s, i: (0, off + j))
    tspec = pl.BlockSpec((1, halo, tn), lambda j, s, i: (s, 0, j))
    specs = [pl.BlockSpec((tm, d), lambda j, s, i: (s * tiles + i, 0)), wspec(0), wspec(nj)]
    args = [x, w_up, w_up]
    if has_buf:
        specs += [pl.BlockSpec((halo, tn), lambda j, s, i, off=off: (s, off + j)) for off in (0, nj)]
        args += [bufs, bufs]
    specs += [cwspec(0), cwspec(nj), cbspec(0), cbspec(nj)]
    cb2 = conv_b.reshape(1, 2 * d_ff)
    args += [conv_w, conv_w, cb2, cb2]
    return pl.pallas_call(
        functools.partial(_ffn_up_kernel, tm=tm, shift=shift, halo=halo, has_buf=has_buf),
        grid=(nj, n_seq, tiles),
        in_specs=specs,
        out_specs=[pl.BlockSpec((tm, tn), lambda j, s, i: (s * tiles + i, j)), tspec, tspec],
        out_shape=[jax.ShapeDtypeStruct((m, d_ff), BF16),
                   jax.ShapeDtypeStruct((n_seq, halo, d_ff), F32),
                   jax.ShapeDtypeStruct((n_seq, halo, d_ff), F32)],
        scratch_shapes=[pltpu.VMEM((halo + tm, tn), F32), pltpu.VMEM((halo + tm, tn), F32),
                        pltpu.VMEM((2, halo, tn), F32)],
        compiler_params=_params(("arbitrary", "arbitrary", "arbitrary")),
        name="ffn_up",
    )(*args)


def _pad_lanes(vec, offset=0):
    out = jnp.zeros((1, LANES), F32)
    return out.at[0, offset:offset + vec.shape[0]].set(vec.astype(F32))


def _layer(x2, lw, alpha, n_seq, rows_per_seq, shift, state):
    (w_in, conv_qkv_w, a_log, dt_bias, norm_g, gm_ln_g, gm_ln_b, gm_ws, gm_bs, gm_out_g, w_out,
     ln1_g, ln1_b, w_up, conv_ffn_w, conv_ffn_b, w_down, ln2_g, ln2_b) = lw
    m, d_model = x2.shape
    n_groups = gm_ws.shape[0]
    d_a = n_groups * GROUP
    d_qkv = conv_qkv_w.shape[1]
    d_b = d_qkv // 3
    n_heads = d_b // GROUP
    main_cols = 2 * d_a + d_qkv + d_b
    fresh = state is None
    tm = min(m, 1024)

    h = _matmul(x2, w_in, main_cols, tm, 1024)
    w_ab = jnp.pad(w_in[:, main_cols:], ((0, 0), (0, LANES - 2 * n_heads)))
    ab = _matmul(x2, w_ab, LANES, tm, LANES)
    alog, dtb, ng = _pad_lanes(a_log), _pad_lanes(dt_bias), norm_g.reshape(1, GROUP).astype(F32)

    if fresh:
        ya = _chunk_mlp_prompt(h, gm_ln_g, gm_ln_b, gm_ws, gm_bs, gm_out_g)
        v_rows = None
        q, k, v = _qkv_prep(h, conv_qkv_w, None, n_seq, rows_per_seq, 512, shift, 2 * d_a)
        yb, s_new = _delta_prompt(q, k, v, h, 2 * d_a + d_qkv, ab, alog, dtb, ng, n_seq, rows_per_seq)
        ffn_bufs = None
        tm_ffn = 1024
    else:
        s_dn, buf_qkv, buf_ffn = state
        n_pos = rows_per_seq // shift
        h3 = h.reshape(n_pos, shift, main_cols)
        ya3, v_rows = _chunk_mlp_sample(h3, gm_ln_g, gm_ln_b, gm_ws, gm_bs, gm_out_g)
        ya = ya3.reshape(m, d_a)
        q, k, v = _qkv_prep(h, conv_qkv_w, buf_qkv, n_seq, rows_per_seq, rows_per_seq, shift, 2 * d_a)
        r3 = lambda t: t.reshape(n_pos, shift, t.shape[-1])
        yb3, s_new = _delta_sample(r3(q), r3(k), r3(v), h3, 2 * d_a + d_qkv, r3(ab), alog, dtb, ng, s_dn, 8)
        yb = yb3.reshape(m, d_b)
        ffn_bufs = buf_ffn
        tm_ffn = rows_per_seq

    x1, x1b = _out_proj_ln(ya, yb, w_out, x2, ln1_g, ln1_b, alpha, min(m, 512), 1024)
    act, tail_g, tail_v = _ffn_up(x1b, w_up, conv_ffn_w, conv_ffn_b, ffn_bufs, n_seq, rows_per_seq, tm_ffn, 512, shift)
    y = _down_ln(act, w_down, x1, ln2_g, ln2_b, alpha, min(m, 512), 1408)
    return y, s_new, h, tail_g, tail_v, v_rows


def kernel(x_prompt, x_sample, state_dn, state_conv_qkv, state_conv_ffn, w_in, conv_qkv_w, dn_a_log, dn_dt_bias, dn_norm_g, gm_ln_g, gm_ln_b, gm_ws, gm_bs, gm_out_g, w_out, ln1_g, ln1_b, w_up, conv_ffn_w, conv_ffn_b, w_down, ln2_g, ln2_b):
    depth = w_in.shape[0]
    bp, lp, d_model = x_prompt.shape
    bs, ls, _ = x_sample.shape
    d_qkv = conv_qkv_w.shape[-1]
    d_a = gm_ws.shape[1] * GROUP
    alpha = (2.0 * depth) ** 0.25

    hp = x_prompt.reshape(bp * lp, d_model)
    hs = jnp.transpose(x_sample, (1, 0, 2)).reshape(ls * bs, d_model)
    outs = [[] for _ in range(7)]
    w_in, w_out, w_up, w_down = (w.astype(BF16) for w in (w_in, w_out, w_up, w_down))
    for l in range(depth):
        lw = tuple(t[l] for t in (w_in, conv_qkv_w, dn_a_log, dn_dt_bias, dn_norm_g, gm_ln_g, gm_ln_b, gm_ws,
                                  gm_bs, gm_out_g, w_out, ln1_g, ln1_b, w_up, conv_ffn_w, conv_ffn_b, w_down,
                                  ln2_g, ln2_b))
        hp, s_p, h_p, tg_p, tv_p, _ = _layer(hp, lw, alpha, bp, lp, 1, None)
        h_p3 = h_p.reshape(bp, lp, -1)
        cq_p = h_p3[:, lp - (CONV_QKV - 1):, 2 * d_a:2 * d_a + d_qkv]
        n_tail = tg_p.shape[1]
        cf_p = jnp.concatenate([tg_p, tv_p], axis=-1)[:, n_tail - (CONV_FFN - 1):]
        bq = jnp.transpose(state_conv_qkv[l], (1, 0, 2)).reshape((CONV_QKV - 1) * bs, d_qkv)
        bf = jnp.transpose(state_conv_ffn[l], (1, 0, 2)).reshape((CONV_FFN - 1) * bs, -1)
        hs, s_s, h_s, tg_s, tv_s, v_s = _layer(hs, lw, alpha, 1, ls * bs, bs, (state_dn[l], bq, bf))
        h_s3 = h_s.reshape(ls, bs, -1)
        cq_s = jnp.transpose(h_s3[ls - (CONV_QKV - 1):, :, 2 * d_a:2 * d_a + d_qkv], (1, 0, 2))
        cf_s = jnp.transpose(jnp.concatenate([tg_s, tv_s], axis=-1).reshape(CONV_FFN - 1, bs, -1), (1, 0, 2))
        for acc, val in zip(outs, (s_p, cq_p, cf_p, s_s, cq_s, cf_s, jnp.transpose(v_s, (1, 0, 2)))):
            acc.append(val)
    y_p = hp.reshape(bp, lp, d_model)
    y_s = jnp.transpose(hs.reshape(ls, bs, d_model), (1, 0, 2))
    return (y_p, y_s) + tuple(jnp.stack(o) for o in outs)
```

```python
import functools

import jax
import jax.numpy as jnp
from jax import lax
from jax.experimental import pallas as pl
from jax.experimental.pallas import tpu as pltpu

F32 = jnp.float32
BF16 = jnp.bfloat16

LANES = 128
SUBLANES = 8
VMEM_LIMIT = 56 * 1024 * 1024

GROUP = 128
MLP_CHUNK = 128
DN_CHUNK = 64
CONV_QKV = 4
CONV_FFN = 3
FFN_SUB = 1
EPS = 1e-5

_NT = (((1,), (1,)), ((), ()))
_TN = (((0,), (0,)), ((), ()))


def _params(sem):
    return pltpu.CompilerParams(dimension_semantics=sem, vmem_limit_bytes=VMEM_LIMIT)


def _bdot(a, b):
    return jnp.dot(a.astype(BF16), b.astype(BF16), preferred_element_type=F32)


def _gelu(x):
    return 0.5 * x * (1.0 + lax.erf(x * (0.5 ** 0.5)))


def _silu(x):
    return x * jax.nn.sigmoid(x)


def _layer_norm(x, g, b):
    mu = jnp.mean(x, axis=-1, keepdims=True)
    xc = x - mu
    var = jnp.mean(xc * xc, axis=-1, keepdims=True)
    return xc * lax.rsqrt(var + EPS) * g + b


def _rms_norm(x, g):
    ms = jnp.mean(x * x, axis=-1, keepdims=True)
    return x * lax.rsqrt(ms + EPS) * g


def _mm_kernel(x_ref, w_ref, o_ref, xb_ref):
    @pl.when(pl.program_id(1) == 0)
    def _():
        xb_ref[...] = x_ref[...].astype(BF16)

    o_ref[...] = jnp.dot(xb_ref[...], w_ref[...].astype(BF16), preferred_element_type=F32)


def _matmul(x, w, n_cols, tm, tn):
    m, k = x.shape
    return pl.pallas_call(
        _mm_kernel,
        grid=(m // tm, n_cols // tn),
        in_specs=[pl.BlockSpec((tm, k), lambda i, j: (i, 0)),
                  pl.BlockSpec((k, tn), lambda i, j: (0, j))],
        out_specs=pl.BlockSpec((tm, tn), lambda i, j: (i, j)),
        out_shape=jax.ShapeDtypeStruct((m, n_cols), F32),
        scratch_shapes=[pltpu.VMEM((tm, k), BF16)],
        compiler_params=_params(("arbitrary", "arbitrary")),
        name="proj",
    )(x, w)


def _cmlp_prompt_kernel(u_ref, v_ref, lng_ref, lnb_ref, ws_ref, bst_ref, og_ref, ya_ref, *, n_groups):
    row = lax.broadcasted_iota(jnp.int32, (MLP_CHUNK, MLP_CHUNK), 0)
    col = lax.broadcasted_iota(jnp.int32, (MLP_CHUNK, MLP_CHUNK), 1)
    gs = range(n_groups)
    sls = [slice(g * GROUP, (g + 1) * GROUP) for g in gs]
    vh = [_layer_norm(_gelu(v_ref[:, sl]), lng_ref[:, sl], lnb_ref[:, sl]) for sl in sls]
    mixed = [_bdot(jnp.where(row >= col, ws_ref[g], 0.0), vh[g]) + bst_ref[:, g:g + 1] for g in gs]
    y = [_gelu(u_ref[:, sls[g]]) * mixed[g] for g in gs]
    for g in gs:
        ya_ref[:, sls[g]] = _rms_norm(y[g], og_ref[:, sls[g]]).astype(ya_ref.dtype)


def _chunk_mlp_prompt(h, ln_g, ln_b, ws, bs, out_g):
    m = h.shape[0]
    n_groups, d_a = ws.shape[0], ws.shape[0] * GROUP
    vec = pl.BlockSpec((1, d_a), lambda i: (0, 0))
    return pl.pallas_call(
        functools.partial(_cmlp_prompt_kernel, n_groups=n_groups),
        grid=(m // MLP_CHUNK,),
        in_specs=[pl.BlockSpec((MLP_CHUNK, d_a), lambda i: (i, 0)),
                  pl.BlockSpec((MLP_CHUNK, d_a), lambda i: (i, 1)),
                  vec, vec,
                  pl.BlockSpec((n_groups, MLP_CHUNK, MLP_CHUNK), lambda i: (0, 0, 0)),
                  pl.BlockSpec((MLP_CHUNK, n_groups), lambda i: (0, 0)),
                  vec],
        out_specs=pl.BlockSpec((MLP_CHUNK, d_a), lambda i: (i, 0)),
        out_shape=jax.ShapeDtypeStruct((m, d_a), BF16),
        compiler_params=_params(("arbitrary",)),
        name="chunk_mlp_prompt",
    )(h, h, ln_g.reshape(1, d_a), ln_b.reshape(1, d_a), ws, bs.T, out_g.reshape(1, d_a))


def _cmlp_sample_kernel(u_ref, v_ref, lng_ref, lnb_ref, wv_ref, bv_ref, og_ref, ya_ref, vr_ref, *, n_pos):
    vh = []
    for t in range(n_pos):
        vt = _layer_norm(_gelu(v_ref[t]), lng_ref[0], lnb_ref[0])
        vr_ref[t] = vt
        vh.append(vt)
    for t in range(n_pos):
        mixed = wv_ref[t, 0:1, :] * vh[0]
        for s in range(1, t + 1):
            mixed = mixed + wv_ref[t, s:s + 1, :] * vh[s]
        mixed = mixed + bv_ref[0, t:t + 1, :]
        y = _gelu(u_ref[t]) * mixed
        ya_ref[t] = _rms_norm(y, og_ref[0]).astype(ya_ref.dtype)


def _chunk_mlp_sample(h3, ln_g, ln_b, ws, bs, out_g):
    n_pos, nb, _ = h3.shape
    n_groups = ws.shape[0]
    d_a = n_groups * GROUP
    wv = jnp.repeat(jnp.transpose(ws[:, :n_pos, :n_pos], (1, 2, 0)), GROUP, axis=-1)
    bv = jnp.repeat(bs[:, :n_pos].T, GROUP, axis=-1).reshape(1, n_pos, d_a)
    vec = pl.BlockSpec((1, 1, GROUP), lambda g: (0, 0, g))
    act = pl.BlockSpec((n_pos, nb, GROUP), lambda g: (0, 0, g))
    return pl.pallas_call(
        functools.partial(_cmlp_sample_kernel, n_pos=n_pos),
        grid=(n_groups,),
        in_specs=[act,
                  pl.BlockSpec((n_pos, nb, GROUP), lambda g: (0, 0, n_groups + g)),
                  vec, vec,
                  pl.BlockSpec((n_pos, n_pos, GROUP), lambda g: (0, 0, g)),
                  pl.BlockSpec((1, n_pos, GROUP), lambda g: (0, 0, g)),
                  vec],
        out_specs=[act, act],
        out_shape=[jax.ShapeDtypeStruct((n_pos, nb, d_a), BF16),
                   jax.ShapeDtypeStruct((n_pos, nb, d_a), F32)],
        compiler_params=_params(("arbitrary",)),
        name="chunk_mlp_sample",
    )(h3, h3, ln_g.reshape(1, 1, d_a), ln_b.reshape(1, 1, d_a), wv, bv, out_g.reshape(1, 1, d_a))


def _conv_taps(xbuf_ref, w_ref, tm, shift, halo, width, cols=slice(None)):
    acc = None
    for j in range(width):
        off = halo - (width - 1 - j) * shift
        term = w_ref[j:j + 1, cols] * xbuf_ref[off:off + tm, :]
        acc = term if acc is None else acc + term
    return acc


def _qkv_prep_kernel(*refs, tm, shift, halo, n_heads, has_buf):
    if has_buf:
        (xq_ref, xk_ref, xv_ref, bq_ref, bk_ref, bv_ref, wq_ref, wk_ref, wv_ref,
         q_ref, k_ref, v_ref, xbuf_ref, carry_ref) = refs
        bufs = (bq_ref, bk_ref, bv_ref)
    else:
        (xq_ref, xk_ref, xv_ref, wq_ref, wk_ref, wv_ref,
         q_ref, k_ref, v_ref, xbuf_ref, carry_ref) = refs
        bufs = (None, None, None)
    first = pl.program_id(1) == 0
    parts = ((xq_ref, wq_ref, q_ref, GROUP ** -0.5), (xk_ref, wk_ref, k_ref, 1.0), (xv_ref, wv_ref, v_ref, None))
    for p, (x_ref, w_ref, o_ref, scale) in enumerate(parts):
        @pl.when(first)
        def _(p=p):
            if has_buf:
                xbuf_ref[0:halo, :] = bufs[p][...]
            else:
                xbuf_ref[0:halo, :] = jnp.zeros((halo, xbuf_ref.shape[1]), F32)

        @pl.when(jnp.logical_not(first))
        def _(p=p):
            xbuf_ref[0:halo, :] = carry_ref[p]

        xbuf_ref[halo:halo + tm, :] = x_ref[...]
        carry_ref[p] = xbuf_ref[tm:tm + halo, :]
        y = _silu(_conv_taps(xbuf_ref, w_ref, tm, shift, halo, CONV_QKV))
        if scale is None:
            o_ref[...] = y
        else:
            for h in range(n_heads):
                sl = slice(h * GROUP, (h + 1) * GROUP)
                yh = y[:, sl]
                yn = yh * lax.rsqrt(jnp.sum(yh * yh, axis=-1, keepdims=True) + 1e-6)
                o_ref[:, sl] = yn * scale if scale != 1.0 else yn


def _qkv_prep(h, conv_w, bufs, n_seq, rows_per_seq, tm, shift, col0):
    d_b = conv_w.shape[1] // 3
    n_heads = d_b // GROUP
    has_buf = bufs is not None
    halo = (CONV_QKV - 1) * shift if has_buf else SUBLANES
    tiles = rows_per_seq // tm
    cb = col0 // d_b
    x_specs = [pl.BlockSpec((tm, d_b), lambda s, i, c=c: (s * tiles + i, cb + c)) for c in range(3)]
    w_specs = [pl.BlockSpec((CONV_QKV, d_b), lambda s, i, c=c: (0, c)) for c in range(3)]
    o_spec = pl.BlockSpec((tm, d_b), lambda s, i: (s * tiles + i, 0))
    args, specs = [h, h, h], list(x_specs)
    if has_buf:
        specs += [pl.BlockSpec((halo, d_b), lambda s, i, c=c: (s, c)) for c in range(3)]
        args += [bufs, bufs, bufs]
    specs += w_specs
    args += [conv_w, conv_w, conv_w]
    m = n_seq * rows_per_seq
    return pl.pallas_call(
        functools.partial(_qkv_prep_kernel, tm=tm, shift=shift, halo=halo, n_heads=n_heads, has_buf=has_buf),
        grid=(n_seq, tiles),
        in_specs=specs,
        out_specs=[o_spec, o_spec, o_spec],
        out_shape=[jax.ShapeDtypeStruct((m, d_b), F32)] * 3,
        scratch_shapes=[pltpu.VMEM((halo + tm, d_b), F32), pltpu.VMEM((3, halo, d_b), F32)],
        compiler_params=_params(("arbitrary", "arbitrary")),
        name="qkv_prep",
    )(*args)


def _gates(ab, alog, dtb):
    g = -jnp.exp(alog) * jax.nn.softplus(ab + dtb)
    beta = jax.nn.sigmoid(ab)
    return g, beta


def _delta_prompt_kernel(q_ref, k_ref, v_ref, z_ref, ab_ref, alog_ref, dtb_ref, ng_ref,
                         yb_ref, sout_ref, s_ref, *, n_heads, n_chunks, ns):
    c = DN_CHUNK
    ci = pl.program_id(1)

    @pl.when(ci == 0)
    def _():
        s_ref[...] = jnp.zeros(s_ref.shape, F32)

    row = lax.broadcasted_iota(jnp.int32, (c, c), 0)
    col = lax.broadcasted_iota(jnp.int32, (c, c), 1)
    causal = row >= col
    strict = row > col
    eye = (row == col).astype(F32)
    tril = causal.astype(F32)
    n_double = (c - 1).bit_length() - 1
    ch = [(s, h) for s in range(ns) for h in range(n_heads)]
    sl = lambda h: slice(h * GROUP, (h + 1) * GROUP)
    gates = [_gates(ab_ref[s], alog_ref[...], dtb_ref[...]) for s in range(ns)]
    gc_s = [jnp.dot(tril, gates[s][0], precision=lax.Precision.HIGHEST, preferred_element_type=F32)
            for s in range(ns)]
    gct_s = [g.T for g in gc_s]
    q = [q_ref[s, :, sl(h)] for s, h in ch]
    k = [k_ref[s, :, sl(h)] for s, h in ch]
    gcol = [gc_s[s][:, h:h + 1] for s, h in ch]
    beta = [gates[s][1][:, n_heads + h:n_heads + h + 1] for s, h in ch]
    n = range(len(ch))
    decay = [jnp.exp(jnp.where(causal, gcol[i] - gct_s[s][h:h + 1, :], -jnp.inf)) for i, (s, h) in enumerate(ch)]
    kb = [k[i] * beta[i] for i in n]
    kbf = [k[i].astype(BF16) for i in n]
    a = [lax.dot_general(kb[i].astype(BF16), kbf[i], _NT, preferred_element_type=F32) * decay[i] for i in n]
    y = [jnp.where(strict, -a[i], 0.0) for i in n]
    x = [eye + y[i] for i in n]
    for _ in range(n_double):
        y = [_bdot(y[i], y[i]) for i in n]
        x = [x[i] + _bdot(x[i], y[i]) for i in n]
    egc = [jnp.exp(gcol[i]) for i in n]
    sol = [_bdot(x[i], jnp.concatenate([v_ref[s, :, sl(h)] * beta[i], kb[i] * egc[i]], axis=1))
           for i, (s, h) in enumerate(ch)]
    qk = [lax.dot_general(q[i].astype(BF16), kbf[i], _NT, preferred_element_type=F32) * decay[i] for i in n]
    s_old = [s_ref[s, h] for s, h in ch]
    r = [_bdot(jnp.concatenate([sol[i][:, GROUP:], q[i] * egc[i]], axis=0), s_old[i]) for i in n]
    v_new = [sol[i][:, :GROUP] - r[i][:c] for i in n]
    o = [r[i][c:] + _bdot(qk[i], v_new[i]) for i in n]
    gl = [gc_s[s][c - 1:c, h:h + 1] for s, h in ch]
    upd = [lax.dot_general((k[i] * jnp.exp(gl[i] - gcol[i])).astype(BF16), v_new[i].astype(BF16), _TN,
                           preferred_element_type=F32) for i in n]
    for i, (s, h) in enumerate(ch):
        s_ref[s, h] = s_old[i] * jnp.exp(gl[i]) + upd[i]
        yb_ref[s, :, sl(h)] = (_rms_norm(o[i], ng_ref[...]) * _silu(z_ref[s, :, sl(h)])).astype(yb_ref.dtype)

    @pl.when(ci == n_chunks - 1)
    def _():
        sout_ref[...] = s_ref[...]


def _delta_prompt(q, k, v, h, z_col0, ab, alog, dtb, norm_g, n_seq, rows_per_seq, ns):
    d_b = q.shape[1]
    n_heads = d_b // GROUP
    n_chunks = rows_per_seq // DN_CHUNK
    r3 = lambda t: t.reshape(n_seq, rows_per_seq, t.shape[-1])
    act = pl.BlockSpec((ns, DN_CHUNK, d_b), lambda s, i: (s, i, 0))
    vec = pl.BlockSpec((1, LANES), lambda s, i: (0, 0))
    zb = z_col0 // d_b
    yb, s_new = pl.pallas_call(
        functools.partial(_delta_prompt_kernel, n_heads=n_heads, n_chunks=n_chunks, ns=ns),
        grid=(n_seq // ns, n_chunks),
        in_specs=[act, act, act,
                  pl.BlockSpec((ns, DN_CHUNK, d_b), lambda s, i: (s, i, zb)),
                  pl.BlockSpec((ns, DN_CHUNK, LANES), lambda s, i: (s, i, 0)),
                  vec, vec, vec],
        out_specs=[act, pl.BlockSpec((ns, n_heads, GROUP, GROUP), lambda s, i: (s, 0, 0, 0))],
        out_shape=[jax.ShapeDtypeStruct((n_seq, rows_per_seq, d_b), BF16),
                   jax.ShapeDtypeStruct((n_seq, n_heads, GROUP, GROUP), F32)],
        scratch_shapes=[pltpu.VMEM((ns, n_heads, GROUP, GROUP), F32)],
        compiler_params=_params(("arbitrary", "arbitrary")),
        name="delta_prompt",
    )(r3(q), r3(k), r3(v), r3(h), r3(ab), alog, dtb, norm_g)
    return yb.reshape(n_seq * rows_per_seq, d_b), s_new


def _delta_sample_kernel(q_ref, k_ref, v_ref, z_ref, ab_ref, alog_ref, dtb_ref, ng_ref, s0_ref,
                         yb_ref, s1_ref, lhs_ref, res_ref, kt_ref, vn_ref, gl_ref, *, n_heads, n_pos, bb):
    nt = n_pos
    hs = range(n_heads)
    sls = [slice(h * GROUP, (h + 1) * GROUP) for h in hs]
    gates = [_gates(ab_ref[t], alog_ref[...], dtb_ref[...]) for t in range(nt)]
    zeros_pad = jnp.zeros((bb, GROUP), F32)
    u_all, qk_all, gc_all = [], [], []
    for h in hs:
        q = [q_ref[t, :, sls[h]] for t in range(nt)]
        k = [k_ref[t, :, sls[h]] for t in range(nt)]
        beta = [gates[t][1][:, n_heads + h:n_heads + h + 1] for t in range(nt)]
        gc = []
        for t in range(nt):
            gt = gates[t][0][:, h:h + 1]
            gc.append(gt if t == 0 else gc[t - 1] + gt)
        kb = [k[t] * beta[t] for t in range(nt)]
        dec = [[jnp.exp(gc[t] - gc[s]) for s in range(t + 1)] for t in range(nt)]
        a = [[jnp.sum(kb[t] * k[s], axis=-1, keepdims=True) * dec[t][s] for s in range(t)] for t in range(nt)]
        egc = [jnp.exp(gc[t]) for t in range(nt)]
        u_blk, w_blk = [], []
        for t in range(nt):
            ut, wt = v_ref[t, :, sls[h]] * beta[t], kb[t] * egc[t]
            for s in range(t):
                ut = ut - a[t][s] * u_blk[s]
                wt = wt - a[t][s] * w_blk[s]
            u_blk.append(ut)
            w_blk.append(wt)
        qk_all.append([[jnp.sum(q[t] * k[s], axis=-1, keepdims=True) * dec[t][s] for s in range(t + 1)]
                       for t in range(nt)])
        u_all.append(u_blk)
        gc_all.append(gc)
        for t in range(nt):
            lhs_ref[h, :, t, :] = w_blk[t]
            lhs_ref[h, :, nt + t, :] = q[t] * egc[t]
        gl_ref[h] = jnp.broadcast_to(jnp.exp(gc[nt - 1]), (bb, GROUP))

    for h in hs:
        for b in range(bb):
            res_ref[h, b] = _bdot(lhs_ref[h, b], s0_ref[b, h])

    for h in hs:
        gc, gl = gc_all[h], gc_all[h][nt - 1]
        v_new = [u_all[h][t] - res_ref[h, :, t, :] for t in range(nt)]
        for t in range(nt):
            o = res_ref[h, :, nt + t, :]
            for s in range(t + 1):
                o = o + qk_all[h][t][s] * v_new[s]
            yb_ref[t, :, sls[h]] = (_rms_norm(o, ng_ref[...]) * _silu(z_ref[t, :, sls[h]])).astype(yb_ref.dtype)
            kt_ref[h, :, t, :] = k_ref[t, :, sls[h]] * jnp.exp(gl - gc[t])
            vn_ref[h, :, t, :] = v_new[t]
            kt_ref[h, :, nt + t, :] = zeros_pad
            vn_ref[h, :, nt + t, :] = zeros_pad

    for h in hs:
        for b in range(bb):
            upd = lax.dot_general(kt_ref[h, b].astype(BF16), vn_ref[h, b].astype(BF16), _TN,
                                  preferred_element_type=F32)
            s1_ref[b, h] = s0_ref[b, h] * gl_ref[h, b:b + 1, :] + upd


def _delta_sample(q, k, v, h3, z_col0, ab, alog, dtb, norm_g, s0, bb):
    n_pos, nb, d_b = q.shape
    n_heads = d_b // GROUP
    assert 2 * n_pos == SUBLANES, "one state read packs [w | q] rows of a sequence into one 8-row tile"
    zb = z_col0 // d_b
    act = pl.BlockSpec((n_pos, bb, d_b), lambda i: (0, i, 0))
    vec = pl.BlockSpec((1, LANES), lambda i: (0, 0))
    st = pl.BlockSpec((bb, n_heads, GROUP, GROUP), lambda i: (i, 0, 0, 0))
    tile = pltpu.VMEM((n_heads, bb, SUBLANES, GROUP), F32)
    return pl.pallas_call(
        functools.partial(_delta_sample_kernel, n_heads=n_heads, n_pos=n_pos, bb=bb),
        grid=(nb // bb,),
        in_specs=[act, act, act,
                  pl.BlockSpec((n_pos, bb, d_b), lambda i: (0, i, zb)),
                  pl.BlockSpec((n_pos, bb, LANES), lambda i: (0, i, 0)),
                  vec, vec, vec, st],
        out_specs=[act, st],
        out_shape=[jax.ShapeDtypeStruct((n_pos, nb, d_b), BF16),
                   jax.ShapeDtypeStruct(s0.shape, F32)],
        scratch_shapes=[tile, tile, tile, tile, pltpu.VMEM((n_heads, bb, GROUP), F32)],
        compiler_params=_params(("arbitrary",)),
        name="delta_sample",
    )(q, k, v, h3, ab, alog, dtb, norm_g, s0)


def _out_proj_kernel(ya_ref, yb_ref, w_ref, r_ref, g_ref, b_ref, o_ref, ob_ref, z_ref, *, n_tiles, d_a, alpha):
    i = pl.program_id(0)

    @pl.when(i == 0)
    def _():
        z_ref[...] = jnp.zeros(z_ref.shape, F32)

    def emit_ln():
        y = _layer_norm(z_ref[...], g_ref[...], b_ref[...])
        o_ref[...] = y
        ob_ref[...] = y.astype(ob_ref.dtype)

    @pl.when(i < n_tiles)
    def _():
        emit_ln()
        acc = (jnp.dot(ya_ref[...], w_ref[0:d_a, :], preferred_element_type=F32)
               + jnp.dot(yb_ref[...], w_ref[d_a:, :], preferred_element_type=F32))
        z_ref[...] = alpha * r_ref[...] + acc

    @pl.when(i == n_tiles)
    def _():
        emit_ln()


def _out_proj_ln(ya, yb, w, resid, g, b, alpha, tm):
    m, d_a = ya.shape
    k, d = w.shape
    n_tiles = m // tm
    cur = lambda i: (jnp.minimum(i, n_tiles - 1), 0)
    prev = lambda i: (jnp.maximum(i - 1, 0), 0)
    vec = pl.BlockSpec((1, d), lambda i: (0, 0))
    return pl.pallas_call(
        functools.partial(_out_proj_kernel, n_tiles=n_tiles, d_a=d_a, alpha=alpha),
        grid=(n_tiles + 1,),
        in_specs=[pl.BlockSpec((tm, d_a), cur),
                  pl.BlockSpec((tm, k - d_a), cur),
                  pl.BlockSpec((k, d), lambda i: (0, 0), pipeline_mode=pl.Buffered(1)),
                  pl.BlockSpec((tm, d), cur), vec, vec],
        out_specs=[pl.BlockSpec((tm, d), prev), pl.BlockSpec((tm, d), prev)],
        out_shape=[jax.ShapeDtypeStruct((m, d), F32), jax.ShapeDtypeStruct((m, d), BF16)],
        scratch_shapes=[pltpu.VMEM((tm, d), F32)],
        compiler_params=_params(("arbitrary",)),
        name="out_proj_ln1",
    )(ya, yb, w, resid, g.reshape(1, d), b.reshape(1, d))


def _down_kernel(x_ref, w_ref, r_ref, g_ref, b_ref, o_ref, acc_ref, z_ref, *, n_tiles, nk, alpha):
    i, kk = pl.program_id(0), pl.program_id(1)

    @pl.when(jnp.logical_and(i == 0, kk == 0))
    def _():
        z_ref[...] = jnp.zeros(z_ref.shape, F32)

    @pl.when(kk == 0)
    def _():
        acc_ref[...] = jnp.dot(x_ref[...], w_ref[...], preferred_element_type=F32)
        o_ref[...] = _layer_norm(z_ref[...], g_ref[...], b_ref[...])

    @pl.when(jnp.logical_and(kk > 0, i < n_tiles))
    def _():
        acc_ref[...] += jnp.dot(x_ref[...], w_ref[...], preferred_element_type=F32)

    @pl.when(jnp.logical_and(kk == nk - 1, i < n_tiles))
    def _():
        z_ref[...] = alpha * r_ref[...] + acc_ref[...]


def _down_ln(x, w, resid, g, b, alpha, tm, tk):
    m, kdim = x.shape
    d = w.shape[1]
    nk = kdim // tk
    n_tiles = m // tm
    row = lambda i: jnp.minimum(i, n_tiles - 1)
    kt = lambda i, kk: jnp.where(i == n_tiles, 0, kk)
    vec = pl.BlockSpec((1, d), lambda i, kk: (0, 0))
    return pl.pallas_call(
        functools.partial(_down_kernel, n_tiles=n_tiles, nk=nk, alpha=alpha),
        grid=(n_tiles + 1, nk),
        in_specs=[pl.BlockSpec((tm, tk), lambda i, kk: (row(i), kt(i, kk))),
                  pl.BlockSpec((tk, d), lambda i, kk: (kt(i, kk), 0)),
                  pl.BlockSpec((tm, d), lambda i, kk: (row(i), 0)), vec, vec],
        out_specs=pl.BlockSpec((tm, d), lambda i, kk: (jnp.where(kk == 0, jnp.maximum(i - 1, 0), row(i)), 0)),
        out_shape=jax.ShapeDtypeStruct((m, d), F32),
        scratch_shapes=[pltpu.VMEM((tm, d), F32), pltpu.VMEM((tm, d), F32)],
        compiler_params=_params(("arbitrary", "arbitrary")),
        name="down_ln2",
    )(x, w, resid, g.reshape(1, d), b.reshape(1, d))


def _ffn_up_kernel(*refs, tm, shift, halo, has_buf, n_sub):
    n_in = 9 if has_buf else 7
    if has_buf:
        x_ref, wg_ref, wv_ref, bg_ref, bv_ref, cwg_ref, cwv_ref, cbg_ref, cbv_ref = refs[:n_in]
        bufs = (bg_ref, bv_ref)
    else:
        x_ref, wg_ref, wv_ref, cwg_ref, cwv_ref, cbg_ref, cbv_ref = refs[:n_in]
        bufs = (None, None)
    act_ref, tg_ref, tv_ref = refs[n_in:n_in + 3]
    xbufs, carry_ref = refs[n_in + 3:-1], refs[-1]
    wsub = act_ref.shape[1] // n_sub
    cols = [slice(c * wsub, (c + 1) * wsub) for c in range(n_sub)]
    first = pl.program_id(2) == 0

    @pl.when(first)
    def _():
        for c in range(n_sub):
            for p in range(2):
                if has_buf:
                    xbufs[2 * c + p][0:halo, :] = bufs[p][:, cols[c]]
                else:
                    xbufs[2 * c + p][0:halo, :] = jnp.zeros((halo, wsub), F32)

    @pl.when(jnp.logical_not(first))
    def _():
        for c in range(n_sub):
            for p in range(2):
                xbufs[2 * c + p][0:halo, :] = carry_ref[p, :, cols[c]]

    x = x_ref[...]
    parts = ((wg_ref, cwg_ref, cbg_ref, tg_ref), (wv_ref, cwv_ref, cbv_ref, tv_ref))
    for c in range(n_sub):
        conv = []
        for p, (w_ref, cw_ref, cb_ref, t_ref) in enumerate(parts):
            xbuf_ref = xbufs[2 * c + p]
            xbuf_ref[halo:halo + tm, :] = jnp.dot(x, w_ref[:, cols[c]], preferred_element_type=F32)
            tail = xbuf_ref[tm:tm + halo, :]
            carry_ref[p, :, cols[c]] = tail
            t_ref[0, :, cols[c]] = tail
            conv.append(_conv_taps(xbuf_ref, cw_ref, tm, shift, halo, CONV_FFN, cols[c]) + cb_ref[:, cols[c]])
        act_ref[:, cols[c]] = (_silu(conv[0]) * conv[1]).astype(act_ref.dtype)


def _ffn_up(x, w_up, conv_w, conv_b, bufs, n_seq, rows_per_seq, tm, tn, shift):
    m, d = x.shape
    d_ff = w_up.shape[1] // 2
    nj = d_ff // tn
    has_buf = bufs is not None
    halo = (CONV_FFN - 1) * shift if has_buf else SUBLANES
    tiles = rows_per_seq // tm
    wspec = lambda off: pl.BlockSpec((d, tn), lambda j, s, i: (0, off + j))
    cwspec = lambda off: pl.BlockSpec((CONV_FFN, tn), lambda j, s, i: (0, off + j))
    cbspec = lambda off: pl.BlockSpec((1, tn), lambda j, s, i: (0, off + j))
    tspec = pl.BlockSpec((1, halo, tn), lambda j, s, i: (s, 0, j))
    specs = [pl.BlockSpec((tm, d), lambda j, s, i: (s * tiles + i, 0)), wspec(0), wspec(nj)]
    args = [x, w_up, w_up]
    if has_buf:
        specs += [pl.BlockSpec((halo, tn), lambda j, s, i, off=off: (s, off + j)) for off in (0, nj)]
        args += [bufs, bufs]
    specs += [cwspec(0), cwspec(nj), cbspec(0), cbspec(nj)]
    cb2 = conv_b.reshape(1, 2 * d_ff)
    args += [conv_w, conv_w, cb2, cb2]
    return pl.pallas_call(
        functools.partial(_ffn_up_kernel, tm=tm, shift=shift, halo=halo, has_buf=has_buf, n_sub=FFN_SUB),
        grid=(nj, n_seq, tiles),
        in_specs=specs,
        out_specs=[pl.BlockSpec((tm, tn), lambda j, s, i: (s * tiles + i, j)), tspec, tspec],
        out_shape=[jax.ShapeDtypeStruct((m, d_ff), BF16),
                   jax.ShapeDtypeStruct((n_seq, halo, d_ff), F32),
                   jax.ShapeDtypeStruct((n_seq, halo, d_ff), F32)],
        scratch_shapes=[pltpu.VMEM((halo + tm, tn // FFN_SUB), F32)] * (2 * FFN_SUB)
                       + [pltpu.VMEM((2, halo, tn), F32)],
        compiler_params=_params(("arbitrary", "arbitrary", "arbitrary")),
        name="ffn_up",
    )(*args)


def _pad_lanes(vec, offset=0):
    out = jnp.zeros((1, LANES), F32)
    return out.at[0, offset:offset + vec.shape[0]].set(vec.astype(F32))


def _layer(x2, lw, alpha, n_seq, rows_per_seq, shift, state):
    (w_in, conv_qkv_w, a_log, dt_bias, norm_g, gm_ln_g, gm_ln_b, gm_ws, gm_bs, gm_out_g, w_out,
     ln1_g, ln1_b, w_up, conv_ffn_w, conv_ffn_b, w_down, ln2_g, ln2_b) = lw
    m, d_model = x2.shape
    n_groups = gm_ws.shape[0]
    d_a = n_groups * GROUP
    d_qkv = conv_qkv_w.shape[1]
    d_b = d_qkv // 3
    n_heads = d_b // GROUP
    main_cols = 2 * d_a + d_qkv + d_b
    fresh = state is None
    tm = min(m, 1024)

    h = _matmul(x2, w_in, main_cols, tm, 1024)
    w_ab = jnp.pad(w_in[:, main_cols:], ((0, 0), (0, LANES - 2 * n_heads)))
    ab = _matmul(x2, w_ab, LANES, tm, LANES)
    alog, dtb, ng = _pad_lanes(a_log), _pad_lanes(dt_bias), norm_g.reshape(1, GROUP).astype(F32)

    if fresh:
        ya = _chunk_mlp_prompt(h, gm_ln_g, gm_ln_b, gm_ws, gm_bs, gm_out_g)
        v_rows = None
        q, k, v = _qkv_prep(h, conv_qkv_w, None, n_seq, rows_per_seq, 512, shift, 2 * d_a)
        yb, s_new = _delta_prompt(q, k, v, h, 2 * d_a + d_qkv, ab, alog, dtb, ng, n_seq, rows_per_seq, 4)
        ffn_bufs = None
        tm_ffn = 1024
    else:
        s_dn, buf_qkv, buf_ffn = state
        n_pos = rows_per_seq // shift
        h3 = h.reshape(n_pos, shift, main_cols)
        ya3, v_rows = _chunk_mlp_sample(h3, gm_ln_g, gm_ln_b, gm_ws, gm_bs, gm_out_g)
        ya = ya3.reshape(m, d_a)
        q, k, v = _qkv_prep(h, conv_qkv_w, buf_qkv, n_seq, rows_per_seq, rows_per_seq, shift, 2 * d_a)
        r3 = lambda t: t.reshape(n_pos, shift, t.shape[-1])
        yb3, s_new = _delta_sample(r3(q), r3(k), r3(v), h3, 2 * d_a + d_qkv, r3(ab), alog, dtb, ng, s_dn, 8)
        yb = yb3.reshape(m, d_b)
        ffn_bufs = buf_ffn
        tm_ffn = rows_per_seq

    x1, x1b = _out_proj_ln(ya, yb, w_out, x2, ln1_g, ln1_b, alpha, min(m, 512))
    act, tail_g, tail_v = _ffn_up(x1b, w_up, conv_ffn_w, conv_ffn_b, ffn_bufs, n_seq, rows_per_seq, tm_ffn, 512, shift)
    y = _down_ln(act, w_down, x1, ln2_g, ln2_b, alpha, min(m, 512), 1408)
    return y, s_new, h, tail_g, tail_v, v_rows


def kernel(x_prompt, x_sample, state_dn, state_conv_qkv, state_conv_ffn, w_in, conv_qkv_w, dn_a_log, dn_dt_bias, dn_norm_g, gm_ln_g, gm_ln_b, gm_ws, gm_bs, gm_out_g, w_out, ln1_g, ln1_b, w_up, conv_ffn_w, conv_ffn_b, w_down, ln2_g, ln2_b):
    depth = w_in.shape[0]
    bp, lp, d_model = x_prompt.shape
    bs, ls, _ = x_sample.shape
    d_qkv = conv_qkv_w.shape[-1]
    d_a = gm_ws.shape[1] * GROUP
    alpha = (2.0 * depth) ** 0.25

    hp = x_prompt.reshape(bp * lp, d_model)
    hs = jnp.transpose(x_sample, (1, 0, 2)).reshape(ls * bs, d_model)
    outs = [[] for _ in range(7)]
    w_in, w_out, w_up, w_down = (w.astype(BF16) for w in (w_in, w_out, w_up, w_down))
    for l in range(depth):
        lw = tuple(t[l] for t in (w_in, conv_qkv_w, dn_a_log, dn_dt_bias, dn_norm_g, gm_ln_g, gm_ln_b, gm_ws,
                                  gm_bs, gm_out_g, w_out, ln1_g, ln1_b, w_up, conv_ffn_w, conv_ffn_b, w_down,
                                  ln2_g, ln2_b))
        hp, s_p, h_p, tg_p, tv_p, _ = _layer(hp, lw, alpha, bp, lp, 1, None)
        h_p3 = h_p.reshape(bp, lp, -1)
        cq_p = h_p3[:, lp - (CONV_QKV - 1):, 2 * d_a:2 * d_a + d_qkv]
        n_tail = tg_p.shape[1]
        cf_p = jnp.concatenate([tg_p, tv_p], axis=-1)[:, n_tail - (CONV_FFN - 1):]
        bq = jnp.transpose(state_conv_qkv[l], (1, 0, 2)).reshape((CONV_QKV - 1) * bs, d_qkv)
        bf = jnp.transpose(state_conv_ffn[l], (1, 0, 2)).reshape((CONV_FFN - 1) * bs, -1)
        hs, s_s, h_s, tg_s, tv_s, v_s = _layer(hs, lw, alpha, 1, ls * bs, bs, (state_dn[l], bq, bf))
        h_s3 = h_s.reshape(ls, bs, -1)
        cq_s = jnp.transpose(h_s3[ls - (CONV_QKV - 1):, :, 2 * d_a:2 * d_a + d_qkv], (1, 0, 2))
        cf_s = jnp.transpose(jnp.concatenate([tg_s, tv_s], axis=-1).reshape(CONV_FFN - 1, bs, -1), (1, 0, 2))
        for acc, val in zip(outs, (s_p, cq_p, cf_p, s_s, cq_s, cf_s, jnp.transpose(v_s, (1, 0, 2)))):
            acc.append(val)
    y_p = hp.reshape(bp, lp, d_model)
    y_s = jnp.transpose(hs.reshape(ls, bs, d_model), (1, 0, 2))
    return (y_p, y_s) + tuple(jnp.stack(o) for o in outs)
```

```python
import functools

import jax
import jax.numpy as jnp
from jax import lax
from jax.experimental import pallas as pl
from jax.experimental.pallas import tpu as pltpu

F32 = jnp.float32
BF16 = jnp.bfloat16

LANES = 128
SUBLANES = 8
VMEM_LIMIT = 56 * 1024 * 1024

GROUP = 128
MLP_CHUNK = 128
DN_CHUNK = 64
CONV_QKV = 4
CONV_FFN = 3
FFN_SUB = 1
EPS = 1e-5

_NT = (((1,), (1,)), ((), ()))
_TN = (((0,), (0,)), ((), ()))


def _params(sem):
    return pltpu.CompilerParams(dimension_semantics=sem, vmem_limit_bytes=VMEM_LIMIT)


def _bdot(a, b):
    return jnp.dot(a.astype(BF16), b.astype(BF16), preferred_element_type=F32)


def _gelu(x):
    return 0.5 * x * (1.0 + lax.erf(x * (0.5 ** 0.5)))


def _silu(x):
    return x * jax.nn.sigmoid(x)


def _layer_norm(x, g, b):
    mu = jnp.mean(x, axis=-1, keepdims=True)
    xc = x - mu
    var = jnp.mean(xc * xc, axis=-1, keepdims=True)
    return xc * lax.rsqrt(var + EPS) * g + b


def _rms_norm(x, g):
    ms = jnp.mean(x * x, axis=-1, keepdims=True)
    return x * lax.rsqrt(ms + EPS) * g


def _mm_kernel(x_ref, w_ref, o_ref, *rest):
    xb_ref = rest[-1]

    @pl.when(pl.program_id(1) == 0)
    def _():
        xb_ref[...] = x_ref[...].astype(BF16)

    wb = w_ref[...].astype(BF16)
    if len(rest) == 2:
        rest[0][...] = wb
    o_ref[...] = jnp.dot(xb_ref[...], wb, preferred_element_type=F32)


def _matmul(x, w, n_cols, tm, tn, emit_w=False):
    m, k = x.shape
    out_specs = [pl.BlockSpec((tm, tn), lambda i, j: (i, j))]
    out_shape = [jax.ShapeDtypeStruct((m, n_cols), F32)]
    if emit_w:
        out_specs.append(pl.BlockSpec((k, tn), lambda i, j: (0, j)))
        out_shape.append(jax.ShapeDtypeStruct((k, n_cols), BF16))
    outs = pl.pallas_call(
        _mm_kernel,
        grid=(m // tm, n_cols // tn),
        in_specs=[pl.BlockSpec((tm, k), lambda i, j: (i, 0)),
                  pl.BlockSpec((k, tn), lambda i, j: (0, j))],
        out_specs=out_specs,
        out_shape=out_shape,
        scratch_shapes=[pltpu.VMEM((tm, k), BF16)],
        compiler_params=_params(("arbitrary", "arbitrary")),
        name="proj",
    )(x, w)
    return tuple(outs) if emit_w else outs[0]


def _cmlp_prompt_kernel(u_ref, v_ref, lng_ref, lnb_ref, ws_ref, bst_ref, og_ref, ya_ref, *, n_groups):
    row = lax.broadcasted_iota(jnp.int32, (MLP_CHUNK, MLP_CHUNK), 0)
    col = lax.broadcasted_iota(jnp.int32, (MLP_CHUNK, MLP_CHUNK), 1)
    gs = range(n_groups)
    sls = [slice(g * GROUP, (g + 1) * GROUP) for g in gs]
    vh = [_layer_norm(_gelu(v_ref[:, sl]), lng_ref[:, sl], lnb_ref[:, sl]) for sl in sls]
    mixed = [_bdot(jnp.where(row >= col, ws_ref[g], 0.0), vh[g]) + bst_ref[:, g:g + 1] for g in gs]
    y = [_gelu(u_ref[:, sls[g]]) * mixed[g] for g in gs]
    for g in gs:
        ya_ref[:, sls[g]] = _rms_norm(y[g], og_ref[:, sls[g]]).astype(ya_ref.dtype)


def _chunk_mlp_prompt(h, ln_g, ln_b, ws, bs, out_g):
    m = h.shape[0]
    n_groups, d_a = ws.shape[0], ws.shape[0] * GROUP
    vec = pl.BlockSpec((1, d_a), lambda i: (0, 0))
    return pl.pallas_call(
        functools.partial(_cmlp_prompt_kernel, n_groups=n_groups),
        grid=(m // MLP_CHUNK,),
        in_specs=[pl.BlockSpec((MLP_CHUNK, d_a), lambda i: (i, 0)),
                  pl.BlockSpec((MLP_CHUNK, d_a), lambda i: (i, 1)),
                  vec, vec,
                  pl.BlockSpec((n_groups, MLP_CHUNK, MLP_CHUNK), lambda i: (0, 0, 0)),
                  pl.BlockSpec((MLP_CHUNK, n_groups), lambda i: (0, 0)),
                  vec],
        out_specs=pl.BlockSpec((MLP_CHUNK, d_a), lambda i: (i, 0)),
        out_shape=jax.ShapeDtypeStruct((m, d_a), BF16),
        compiler_params=_params(("arbitrary",)),
        name="chunk_mlp_prompt",
    )(h, h, ln_g.reshape(1, d_a), ln_b.reshape(1, d_a), ws, bs.T, out_g.reshape(1, d_a))


def _cmlp_sample_kernel(u_ref, v_ref, lng_ref, lnb_ref, wv_ref, bv_ref, og_ref, ya_ref, vr_ref, *, n_pos):
    vh = []
    for t in range(n_pos):
        vt = _layer_norm(_gelu(v_ref[t]), lng_ref[0], lnb_ref[0])
        vr_ref[t] = vt
        vh.append(vt)
    for t in range(n_pos):
        mixed = wv_ref[t, 0:1, :] * vh[0]
        for s in range(1, t + 1):
            mixed = mixed + wv_ref[t, s:s + 1, :] * vh[s]
        mixed = mixed + bv_ref[0, t:t + 1, :]
        y = _gelu(u_ref[t]) * mixed
        ya_ref[t] = _rms_norm(y, og_ref[0]).astype(ya_ref.dtype)


def _chunk_mlp_sample(h3, ln_g, ln_b, ws, bs, out_g):
    n_pos, nb, _ = h3.shape
    n_groups = ws.shape[0]
    d_a = n_groups * GROUP
    wv = jnp.repeat(jnp.transpose(ws[:, :n_pos, :n_pos], (1, 2, 0)), GROUP, axis=-1)
    bv = jnp.repeat(bs[:, :n_pos].T, GROUP, axis=-1).reshape(1, n_pos, d_a)
    vec = pl.BlockSpec((1, 1, GROUP), lambda g: (0, 0, g))
    act = pl.BlockSpec((n_pos, nb, GROUP), lambda g: (0, 0, g))
    return pl.pallas_call(
        functools.partial(_cmlp_sample_kernel, n_pos=n_pos),
        grid=(n_groups,),
        in_specs=[act,
                  pl.BlockSpec((n_pos, nb, GROUP), lambda g: (0, 0, n_groups + g)),
                  vec, vec,
                  pl.BlockSpec((n_pos, n_pos, GROUP), lambda g: (0, 0, g)),
                  pl.BlockSpec((1, n_pos, GROUP), lambda g: (0, 0, g)),
                  vec],
        out_specs=[act, act],
        out_shape=[jax.ShapeDtypeStruct((n_pos, nb, d_a), BF16),
                   jax.ShapeDtypeStruct((n_pos, nb, d_a), F32)],
        compiler_params=_params(("arbitrary",)),
        name="chunk_mlp_sample",
    )(h3, h3, ln_g.reshape(1, 1, d_a), ln_b.reshape(1, 1, d_a), wv, bv, out_g.reshape(1, 1, d_a))


def _conv_taps(xbuf_ref, w_ref, tm, shift, halo, width, cols=slice(None)):
    acc = None
    for j in range(width):
        off = halo - (width - 1 - j) * shift
        term = w_ref[j:j + 1, cols] * xbuf_ref[off:off + tm, :]
        acc = term if acc is None else acc + term
    return acc


def _qkv_prep_kernel(*refs, tm, shift, halo, n_heads, has_buf):
    if has_buf:
        (xq_ref, xk_ref, xv_ref, bq_ref, bk_ref, bv_ref, wq_ref, wk_ref, wv_ref,
         q_ref, k_ref, v_ref, xbuf_ref, carry_ref) = refs
        bufs = (bq_ref, bk_ref, bv_ref)
    else:
        (xq_ref, xk_ref, xv_ref, wq_ref, wk_ref, wv_ref,
         q_ref, k_ref, v_ref, xbuf_ref, carry_ref) = refs
        bufs = (None, None, None)
    first = pl.program_id(1) == 0
    parts = ((xq_ref, wq_ref, q_ref, GROUP ** -0.5), (xk_ref, wk_ref, k_ref, 1.0), (xv_ref, wv_ref, v_ref, None))
    for p, (x_ref, w_ref, o_ref, scale) in enumerate(parts):
        @pl.when(first)
        def _(p=p):
            if has_buf:
                xbuf_ref[0:halo, :] = bufs[p][...]
            else:
                xbuf_ref[0:halo, :] = jnp.zeros((halo, xbuf_ref.shape[1]), F32)

        @pl.when(jnp.logical_not(first))
        def _(p=p):
            xbuf_ref[0:halo, :] = carry_ref[p]

        xbuf_ref[halo:halo + tm, :] = x_ref[...]
        carry_ref[p] = xbuf_ref[tm:tm + halo, :]
        y = _silu(_conv_taps(xbuf_ref, w_ref, tm, shift, halo, CONV_QKV))
        if scale is None:
            o_ref[...] = y
        else:
            for h in range(n_heads):
                sl = slice(h * GROUP, (h + 1) * GROUP)
                yh = y[:, sl]
                yn = yh * lax.rsqrt(jnp.sum(yh * yh, axis=-1, keepdims=True) + 1e-6)
                o_ref[:, sl] = yn * scale if scale != 1.0 else yn


def _qkv_prep(h, conv_w, bufs, n_seq, rows_per_seq, tm, shift, col0):
    d_b = conv_w.shape[1] // 3
    n_heads = d_b // GROUP
    has_buf = bufs is not None
    halo = (CONV_QKV - 1) * shift if has_buf else SUBLANES
    tiles = rows_per_seq // tm
    cb = col0 // d_b
    x_specs = [pl.BlockSpec((tm, d_b), lambda s, i, c=c: (s * tiles + i, cb + c)) for c in range(3)]
    w_specs = [pl.BlockSpec((CONV_QKV, d_b), lambda s, i, c=c: (0, c)) for c in range(3)]
    o_spec = pl.BlockSpec((tm, d_b), lambda s, i: (s * tiles + i, 0))
    args, specs = [h, h, h], list(x_specs)
    if has_buf:
        specs += [pl.BlockSpec((halo, d_b), lambda s, i, c=c: (s, c)) for c in range(3)]
        args += [bufs, bufs, bufs]
    specs += w_specs
    args += [conv_w, conv_w, conv_w]
    m = n_seq * rows_per_seq
    return pl.pallas_call(
        functools.partial(_qkv_prep_kernel, tm=tm, shift=shift, halo=halo, n_heads=n_heads, has_buf=has_buf),
        grid=(n_seq, tiles),
        in_specs=specs,
        out_specs=[o_spec, o_spec, o_spec],
        out_shape=[jax.ShapeDtypeStruct((m, d_b), F32)] * 3,
        scratch_shapes=[pltpu.VMEM((halo + tm, d_b), F32), pltpu.VMEM((3, halo, d_b), F32)],
        compiler_params=_params(("arbitrary", "arbitrary")),
        name="qkv_prep",
    )(*args)


def _gates(ab, alog, dtb):
    g = -jnp.exp(alog) * jax.nn.softplus(ab + dtb)
    beta = jax.nn.sigmoid(ab)
    return g, beta


def _delta_prompt_kernel(q_ref, k_ref, v_ref, z_ref, ab_ref, alog_ref, dtb_ref, ng_ref,
                         yb_ref, sout_ref, s_ref, *, n_heads, n_chunks, ns):
    c = DN_CHUNK
    ci = pl.program_id(1)

    @pl.when(ci == 0)
    def _():
        s_ref[...] = jnp.zeros(s_ref.shape, F32)

    row = lax.broadcasted_iota(jnp.int32, (c, c), 0)
    col = lax.broadcasted_iota(jnp.int32, (c, c), 1)
    causal = row >= col
    strict = row > col
    eye = (row == col).astype(F32)
    tril = causal.astype(F32)
    n_double = (c - 1).bit_length() - 1
    ch = [(s, h) for s in range(ns) for h in range(n_heads)]
    sl = lambda h: slice(h * GROUP, (h + 1) * GROUP)
    gates = [_gates(ab_ref[s], alog_ref[...], dtb_ref[...]) for s in range(ns)]
    gc_s = [jnp.dot(tril, gates[s][0], precision=lax.Precision.HIGHEST, preferred_element_type=F32)
            for s in range(ns)]
    gct_s = [g.T for g in gc_s]
    q = [q_ref[s, :, sl(h)] for s, h in ch]
    k = [k_ref[s, :, sl(h)] for s, h in ch]
    gcol = [gc_s[s][:, h:h + 1] for s, h in ch]
    beta = [gates[s][1][:, n_heads + h:n_heads + h + 1] for s, h in ch]
    n = range(len(ch))
    decay = [jnp.exp(jnp.where(causal, gcol[i] - gct_s[s][h:h + 1, :], -jnp.inf)) for i, (s, h) in enumerate(ch)]
    kb = [k[i] * beta[i] for i in n]
    kbf = [k[i].astype(BF16) for i in n]
    a = [lax.dot_general(kb[i].astype(BF16), kbf[i], _NT, preferred_element_type=F32) * decay[i] for i in n]
    y = [jnp.where(strict, -a[i], 0.0) for i in n]
    x = [eye + y[i] for i in n]
    for _ in range(n_double):
        y = [_bdot(y[i], y[i]) for i in n]
        x = [x[i] + _bdot(x[i], y[i]) for i in n]
    egc = [jnp.exp(gcol[i]) for i in n]
    sol = [_bdot(x[i], jnp.concatenate([v_ref[s, :, sl(h)] * beta[i], kb[i] * egc[i]], axis=1))
           for i, (s, h) in enumerate(ch)]
    qk = [lax.dot_general(q[i].astype(BF16), kbf[i], _NT, preferred_element_type=F32) * decay[i] for i in n]
    s_old = [s_ref[s, h] for s, h in ch]
    r = [_bdot(jnp.concatenate([sol[i][:, GROUP:], q[i] * egc[i]], axis=0), s_old[i]) for i in n]
    v_new = [sol[i][:, :GROUP] - r[i][:c] for i in n]
    o = [r[i][c:] + _bdot(qk[i], v_new[i]) for i in n]
    gl = [gc_s[s][c - 1:c, h:h + 1] for s, h in ch]
    upd = [lax.dot_general((k[i] * jnp.exp(gl[i] - gcol[i])).astype(BF16), v_new[i].astype(BF16), _TN,
                           preferred_element_type=F32) for i in n]
    for i, (s, h) in enumerate(ch):
        s_ref[s, h] = s_old[i] * jnp.exp(gl[i]) + upd[i]
        yb_ref[s, :, sl(h)] = (_rms_norm(o[i], ng_ref[...]) * _silu(z_ref[s, :, sl(h)])).astype(yb_ref.dtype)

    @pl.when(ci == n_chunks - 1)
    def _():
        sout_ref[...] = s_ref[...]


def _delta_prompt(q, k, v, h, z_col0, ab, alog, dtb, norm_g, n_seq, rows_per_seq, ns):
    d_b = q.shape[1]
    n_heads = d_b // GROUP
    n_chunks = rows_per_seq // DN_CHUNK
    r3 = lambda t: t.reshape(n_seq, rows_per_seq, t.shape[-1])
    act = pl.BlockSpec((ns, DN_CHUNK, d_b), lambda s, i: (s, i, 0))
    vec = pl.BlockSpec((1, LANES), lambda s, i: (0, 0))
    zb = z_col0 // d_b
    yb, s_new = pl.pallas_call(
        functools.partial(_delta_prompt_kernel, n_heads=n_heads, n_chunks=n_chunks, ns=ns),
        grid=(n_seq // ns, n_chunks),
        in_specs=[act, act, act,
                  pl.BlockSpec((ns, DN_CHUNK, d_b), lambda s, i: (s, i, zb)),
                  pl.BlockSpec((ns, DN_CHUNK, LANES), lambda s, i: (s, i, 0)),
                  vec, vec, vec],
        out_specs=[act, pl.BlockSpec((ns, n_heads, GROUP, GROUP), lambda s, i: (s, 0, 0, 0))],
        out_shape=[jax.ShapeDtypeStruct((n_seq, rows_per_seq, d_b), BF16),
                   jax.ShapeDtypeStruct((n_seq, n_heads, GROUP, GROUP), F32)],
        scratch_shapes=[pltpu.VMEM((ns, n_heads, GROUP, GROUP), F32)],
        compiler_params=_params(("arbitrary", "arbitrary")),
        name="delta_prompt",
    )(r3(q), r3(k), r3(v), r3(h), r3(ab), alog, dtb, norm_g)
    return yb.reshape(n_seq * rows_per_seq, d_b), s_new


def _delta_sample_kernel(q_ref, k_ref, v_ref, z_ref, ab_ref, alog_ref, dtb_ref, ng_ref, s0_ref,
                         yb_ref, s1_ref, lhs_ref, res_ref, kt_ref, vn_ref, gl_ref, *, n_heads, n_pos, bb):
    nt = n_pos
    hs = range(n_heads)
    sls = [slice(h * GROUP, (h + 1) * GROUP) for h in hs]
    gates = [_gates(ab_ref[t], alog_ref[...], dtb_ref[...]) for t in range(nt)]
    zeros_pad = jnp.zeros((bb, GROUP), F32)
    u_all, qk_all, gc_all = [], [], []
    for h in hs:
        q = [q_ref[t, :, sls[h]] for t in range(nt)]
        k = [k_ref[t, :, sls[h]] for t in range(nt)]
        beta = [gates[t][1][:, n_heads + h:n_heads + h + 1] for t in range(nt)]
        gc = []
        for t in range(nt):
            gt = gates[t][0][:, h:h + 1]
            gc.append(gt if t == 0 else gc[t - 1] + gt)
        kb = [k[t] * beta[t] for t in range(nt)]
        dec = [[jnp.exp(gc[t] - gc[s]) for s in range(t + 1)] for t in range(nt)]
        a = [[jnp.sum(kb[t] * k[s], axis=-1, keepdims=True) * dec[t][s] for s in range(t)] for t in range(nt)]
        egc = [jnp.exp(gc[t]) for t in range(nt)]
        u_blk, w_blk = [], []
        for t in range(nt):
            ut, wt = v_ref[t, :, sls[h]] * beta[t], kb[t] * egc[t]
            for s in range(t):
                ut = ut - a[t][s] * u_blk[s]
                wt = wt - a[t][s] * w_blk[s]
            u_blk.append(ut)
            w_blk.append(wt)
        qk_all.append([[jnp.sum(q[t] * k[s], axis=-1, keepdims=True) * dec[t][s] for s in range(t + 1)]
                       for t in range(nt)])
        u_all.append(u_blk)
        gc_all.append(gc)
        for t in range(nt):
            lhs_ref[h, :, t, :] = w_blk[t]
            lhs_ref[h, :, nt + t, :] = q[t] * egc[t]
        gl_ref[h] = jnp.broadcast_to(jnp.exp(gc[nt - 1]), (bb, GROUP))

    for h in hs:
        for b in range(bb):
            res_ref[h, b] = _bdot(lhs_ref[h, b], s0_ref[b, h])

    for h in hs:
        gc, gl = gc_all[h], gc_all[h][nt - 1]
        v_new = [u_all[h][t] - res_ref[h, :, t, :] for t in range(nt)]
        for t in range(nt):
            o = res_ref[h, :, nt + t, :]
            for s in range(t + 1):
                o = o + qk_all[h][t][s] * v_new[s]
            yb_ref[t, :, sls[h]] = (_rms_norm(o, ng_ref[...]) * _silu(z_ref[t, :, sls[h]])).astype(yb_ref.dtype)
            kt_ref[h, :, t, :] = k_ref[t, :, sls[h]] * jnp.exp(gl - gc[t])
            vn_ref[h, :, t, :] = v_new[t]
            kt_ref[h, :, nt + t, :] = zeros_pad
            vn_ref[h, :, nt + t, :] = zeros_pad

    for h in hs:
        for b in range(bb):
            upd = lax.dot_general(kt_ref[h, b].astype(BF16), vn_ref[h, b].astype(BF16), _TN,
                                  preferred_element_type=F32)
            s1_ref[b, h] = s0_ref[b, h] * gl_ref[h, b:b + 1, :] + upd


def _delta_sample(q, k, v, h3, z_col0, ab, alog, dtb, norm_g, s0, bb):
    n_pos, nb, d_b = q.shape
    n_heads = d_b // GROUP
    assert 2 * n_pos == SUBLANES, "one state read packs [w | q] rows of a sequence into one 8-row tile"
    zb = z_col0 // d_b
    act = pl.BlockSpec((n_pos, bb, d_b), lambda i: (0, i, 0))
    vec = pl.BlockSpec((1, LANES), lambda i: (0, 0))
    st = pl.BlockSpec((bb, n_heads, GROUP, GROUP), lambda i: (i, 0, 0, 0))
    tile = pltpu.VMEM((n_heads, bb, SUBLANES, GROUP), F32)
    return pl.pallas_call(
        functools.partial(_delta_sample_kernel, n_heads=n_heads, n_pos=n_pos, bb=bb),
        grid=(nb // bb,),
        in_specs=[act, act, act,
                  pl.BlockSpec((n_pos, bb, d_b), lambda i: (0, i, zb)),
                  pl.BlockSpec((n_pos, bb, LANES), lambda i: (0, i, 0)),
                  vec, vec, vec, st],
        out_specs=[act, st],
        out_shape=[jax.ShapeDtypeStruct((n_pos, nb, d_b), BF16),
                   jax.ShapeDtypeStruct(s0.shape, F32)],
        scratch_shapes=[tile, tile, tile, tile, pltpu.VMEM((n_heads, bb, GROUP), F32)],
        compiler_params=_params(("arbitrary",)),
        name="delta_sample",
    )(q, k, v, h3, ab, alog, dtb, norm_g, s0)


def _out_proj_kernel(ya_ref, yb_ref, w_ref, r_ref, g_ref, b_ref, o_ref, ob_ref, z_ref, *, n_tiles, d_a, alpha):
    i = pl.program_id(0)

    @pl.when(i == 0)
    def _():
        z_ref[...] = jnp.zeros(z_ref.shape, F32)

    def emit_ln():
        y = _layer_norm(z_ref[...], g_ref[...], b_ref[...])
        o_ref[...] = y
        ob_ref[...] = y.astype(ob_ref.dtype)

    @pl.when(i < n_tiles)
    def _():
        emit_ln()
        acc = (jnp.dot(ya_ref[...], w_ref[0:d_a, :], preferred_element_type=F32)
               + jnp.dot(yb_ref[...], w_ref[d_a:, :], preferred_element_type=F32))
        z_ref[...] = alpha * r_ref[...] + acc

    @pl.when(i == n_tiles)
    def _():
        emit_ln()


def _out_proj_ln(ya, yb, w, resid, g, b, alpha, tm):
    m, d_a = ya.shape
    k, d = w.shape
    n_tiles = m // tm
    cur = lambda i: (jnp.minimum(i, n_tiles - 1), 0)
    prev = lambda i: (jnp.maximum(i - 1, 0), 0)
    vec = pl.BlockSpec((1, d), lambda i: (0, 0))
    return pl.pallas_call(
        functools.partial(_out_proj_kernel, n_tiles=n_tiles, d_a=d_a, alpha=alpha),
        grid=(n_tiles + 1,),
        in_specs=[pl.BlockSpec((tm, d_a), cur),
                  pl.BlockSpec((tm, k - d_a), cur),
                  pl.BlockSpec((k, d), lambda i: (0, 0), pipeline_mode=pl.Buffered(1)),
                  pl.BlockSpec((tm, d), cur), vec, vec],
        out_specs=[pl.BlockSpec((tm, d), prev), pl.BlockSpec((tm, d), prev)],
        out_shape=[jax.ShapeDtypeStruct((m, d), F32), jax.ShapeDtypeStruct((m, d), BF16)],
        scratch_shapes=[pltpu.VMEM((tm, d), F32)],
        compiler_params=_params(("arbitrary",)),
        name="out_proj_ln1",
    )(ya, yb, w, resid, g.reshape(1, d), b.reshape(1, d))


def _down_kernel(x_ref, w_ref, r_ref, g_ref, b_ref, o_ref, *rest, n_tiles, nj, alpha):
    z_ref = rest[-1]
    i, j = pl.program_id(0), pl.program_id(1)
    tn = z_ref.shape[-1]
    cur, prev = i % 2, (i + 1) % 2

    @pl.when(jnp.logical_and(i == 0, j == 0))
    def _():
        z_ref[1] = jnp.zeros(z_ref.shape[1:], F32)

    wb = w_ref[...].astype(BF16)
    if len(rest) == 2:
        rest[0][...] = wb

    @pl.when(j == 0)
    def _():
        z_ref[cur, 0] = alpha * r_ref[...] + jnp.dot(x_ref[...], wb, preferred_element_type=F32)
        z = [z_ref[prev, c] for c in range(nj)]
        inv_d = 1.0 / (nj * tn)
        mu = sum(jnp.sum(zc, axis=-1, keepdims=True) for zc in z) * inv_d
        xc = [zc - mu for zc in z]
        var = sum(jnp.sum(c * c, axis=-1, keepdims=True) for c in xc) * inv_d
        rs = lax.rsqrt(var + EPS)
        for c in range(nj):
            cols = slice(c * tn, (c + 1) * tn)
            o_ref[:, cols] = xc[c] * rs * g_ref[:, cols] + b_ref[:, cols]

    @pl.when(jnp.logical_and(j > 0, i < n_tiles))
    def _():
        z_ref[cur, j] = alpha * r_ref[...] + jnp.dot(x_ref[...], wb, preferred_element_type=F32)


def _down_ln(x, w, resid, g, b, alpha, tm, tn, emit_w=False):
    m, kdim = x.shape
    d = w.shape[1]
    nj = d // tn
    n_tiles = m // tm
    row = lambda i: jnp.minimum(i, n_tiles - 1)
    jt = lambda i, j: jnp.where(i == n_tiles, nj - 1, j)
    vec = pl.BlockSpec((1, d), lambda i, j: (0, 0))
    wspec = pl.BlockSpec((kdim, tn), lambda i, j: (0, jt(i, j)))
    out_specs = [pl.BlockSpec((tm, d), lambda i, j: (jnp.where(j == 0, jnp.maximum(i - 1, 0), row(i)), 0))]
    out_shape = [jax.ShapeDtypeStruct((m, d), F32)]
    if emit_w:
        out_specs.append(wspec)
        out_shape.append(jax.ShapeDtypeStruct(w.shape, BF16))
    outs = pl.pallas_call(
        functools.partial(_down_kernel, n_tiles=n_tiles, nj=nj, alpha=alpha),
        grid=(n_tiles + 1, nj),
        in_specs=[pl.BlockSpec((tm, kdim), lambda i, j: (row(i), 0)), wspec,
                  pl.BlockSpec((tm, tn), lambda i, j: (row(i), jt(i, j))), vec, vec],
        out_specs=out_specs,
        out_shape=out_shape,
        scratch_shapes=[pltpu.VMEM((2, nj, tm, tn), F32)],
        compiler_params=_params(("arbitrary", "arbitrary")),
        name="down_ln2",
    )(x, w, resid, g.reshape(1, d), b.reshape(1, d))
    return tuple(outs) if emit_w else outs[0]


def _ffn_up_kernel(*refs, tm, shift, halo, has_buf, n_sub):
    n_in = 9 if has_buf else 7
    if has_buf:
        x_ref, wg_ref, wv_ref, bg_ref, bv_ref, cwg_ref, cwv_ref, cbg_ref, cbv_ref = refs[:n_in]
        bufs = (bg_ref, bv_ref)
    else:
        x_ref, wg_ref, wv_ref, cwg_ref, cwv_ref, cbg_ref, cbv_ref = refs[:n_in]
        bufs = (None, None)
    act_ref, tg_ref, tv_ref = refs[n_in:n_in + 3]
    wgb_ref, wvb_ref = refs[n_in + 3:n_in + 5]
    xbufs, carry_ref = refs[n_in + 5:-1], refs[-1]

    @pl.when(jnp.logical_and(pl.program_id(1) == 0, pl.program_id(2) == 0))
    def _():
        wgb_ref[...] = wg_ref[...].astype(BF16)
        wvb_ref[...] = wv_ref[...].astype(BF16)

    wsub = act_ref.shape[1] // n_sub
    cols = [slice(c * wsub, (c + 1) * wsub) for c in range(n_sub)]
    first = pl.program_id(2) == 0

    @pl.when(first)
    def _():
        for c in range(n_sub):
            for p in range(2):
                if has_buf:
                    xbufs[2 * c + p][0:halo, :] = bufs[p][:, cols[c]]
                else:
                    xbufs[2 * c + p][0:halo, :] = jnp.zeros((halo, wsub), F32)

    @pl.when(jnp.logical_not(first))
    def _():
        for c in range(n_sub):
            for p in range(2):
                xbufs[2 * c + p][0:halo, :] = carry_ref[p, :, cols[c]]

    x = x_ref[...]
    parts = ((wgb_ref, cwg_ref, cbg_ref, tg_ref), (wvb_ref, cwv_ref, cbv_ref, tv_ref))
    for c in range(n_sub):
        conv = []
        for p, (w_ref, cw_ref, cb_ref, t_ref) in enumerate(parts):
            xbuf_ref = xbufs[2 * c + p]
            xbuf_ref[halo:halo + tm, :] = jnp.dot(x, w_ref[:, cols[c]], preferred_element_type=F32)
            tail = xbuf_ref[tm:tm + halo, :]
            carry_ref[p, :, cols[c]] = tail
            t_ref[0, :, cols[c]] = tail
            conv.append(_conv_taps(xbuf_ref, cw_ref, tm, shift, halo, CONV_FFN, cols[c]) + cb_ref[:, cols[c]])
        act_ref[:, cols[c]] = (_silu(conv[0]) * conv[1]).astype(act_ref.dtype)


def _ffn_up(x, w_up, conv_w, conv_b, bufs, n_seq, rows_per_seq, tm, tn, shift):
    m, d = x.shape
    d_ff = w_up.shape[1] // 2
    nj = d_ff // tn
    has_buf = bufs is not None
    halo = (CONV_FFN - 1) * shift if has_buf else SUBLANES
    tiles = rows_per_seq // tm
    wspec = lambda off: pl.BlockSpec((d, tn), lambda j, s, i: (0, off + j))
    cwspec = lambda off: pl.BlockSpec((CONV_FFN, tn), lambda j, s, i: (0, off + j))
    cbspec = lambda off: pl.BlockSpec((1, tn), lambda j, s, i: (0, off + j))
    tspec = pl.BlockSpec((1, halo, tn), lambda j, s, i: (s, 0, j))
    specs = [pl.BlockSpec((tm, d), lambda j, s, i: (s * tiles + i, 0)), wspec(0), wspec(nj)]
    args = [x, w_up, w_up]
    if has_buf:
        specs += [pl.BlockSpec((halo, tn), lambda j, s, i, off=off: (s, off + j)) for off in (0, nj)]
        args += [bufs, bufs]
    specs += [cwspec(0), cwspec(nj), cbspec(0), cbspec(nj)]
    cb2 = conv_b.reshape(1, 2 * d_ff)
    args += [conv_w, conv_w, cb2, cb2]
    return pl.pallas_call(
        functools.partial(_ffn_up_kernel, tm=tm, shift=shift, halo=halo, has_buf=has_buf, n_sub=FFN_SUB),
        grid=(nj, n_seq, tiles),
        in_specs=specs,
        out_specs=[pl.BlockSpec((tm, tn), lambda j, s, i: (s * tiles + i, j)), tspec, tspec],
        out_shape=[jax.ShapeDtypeStruct((m, d_ff), BF16),
                   jax.ShapeDtypeStruct((n_seq, halo, d_ff), F32),
                   jax.ShapeDtypeStruct((n_seq, halo, d_ff), F32)],
        scratch_shapes=[pltpu.VMEM((d, tn), BF16)] * 2
                       + [pltpu.VMEM((halo + tm, tn // FFN_SUB), F32)] * (2 * FFN_SUB)
                       + [pltpu.VMEM((2, halo, tn), F32)],
        compiler_params=_params(("arbitrary", "arbitrary", "arbitrary")),
        name="ffn_up",
    )(*args)


def _pad_lanes(vec, offset=0):
    out = jnp.zeros((1, LANES), F32)
    return out.at[0, offset:offset + vec.shape[0]].set(vec.astype(F32))


def _layer(x2, lw, alpha, n_seq, rows_per_seq, shift, state, wb):
    (w_in, conv_qkv_w, a_log, dt_bias, norm_g, gm_ln_g, gm_ln_b, gm_ws, gm_bs, gm_out_g, w_out,
     ln1_g, ln1_b, w_up, conv_ffn_w, conv_ffn_b, w_down, ln2_g, ln2_b) = lw
    m, d_model = x2.shape
    n_groups = gm_ws.shape[0]
    d_a = n_groups * GROUP
    d_qkv = conv_qkv_w.shape[1]
    d_b = d_qkv // 3
    n_heads = d_b // GROUP
    main_cols = 2 * d_a + d_qkv + d_b
    fresh = state is None
    tm = min(m, 1024)

    if wb is None:
        h, w_in_b = _matmul(x2, w_in, main_cols, tm, 1024, emit_w=True)
    else:
        w_in_b = wb[0]
        h = _matmul(x2, w_in_b, main_cols, tm, 1024)
    w_ab = jnp.pad(w_in[:, main_cols:].astype(BF16), ((0, 0), (0, LANES - 2 * n_heads)))
    ab = _matmul(x2, w_ab, LANES, tm, LANES)
    alog, dtb, ng = _pad_lanes(a_log), _pad_lanes(dt_bias), norm_g.reshape(1, GROUP).astype(F32)

    if fresh:
        ya = _chunk_mlp_prompt(h, gm_ln_g, gm_ln_b, gm_ws, gm_bs, gm_out_g)
        v_rows = None
        q, k, v = _qkv_prep(h, conv_qkv_w, None, n_seq, rows_per_seq, 512, shift, 2 * d_a)
        yb, s_new = _delta_prompt(q, k, v, h, 2 * d_a + d_qkv, ab, alog, dtb, ng, n_seq, rows_per_seq, 4)
        ffn_bufs = None
        tm_ffn = 1024
    else:
        s_dn, buf_qkv, buf_ffn = state
        n_pos = rows_per_seq // shift
        h3 = h.reshape(n_pos, shift, main_cols)
        ya3, v_rows = _chunk_mlp_sample(h3, gm_ln_g, gm_ln_b, gm_ws, gm_bs, gm_out_g)
        ya = ya3.reshape(m, d_a)
        q, k, v = _qkv_prep(h, conv_qkv_w, buf_qkv, n_seq, rows_per_seq, rows_per_seq, shift, 2 * d_a)
        r3 = lambda t: t.reshape(n_pos, shift, t.shape[-1])
        yb3, s_new = _delta_sample(r3(q), r3(k), r3(v), h3, 2 * d_a + d_qkv, r3(ab), alog, dtb, ng, s_dn, 8)
        yb = yb3.reshape(m, d_b)
        ffn_bufs = buf_ffn
        tm_ffn = rows_per_seq

    x1, x1b = _out_proj_ln(ya, yb, w_out, x2, ln1_g, ln1_b, alpha, min(m, 512))
    act, tail_g, tail_v = _ffn_up(x1b, w_up, conv_ffn_w, conv_ffn_b, ffn_bufs, n_seq, rows_per_seq, tm_ffn, 512, shift)
    if wb is None:
        y, w_down_b = _down_ln(act, w_down, x1, ln2_g, ln2_b, alpha, min(m, 512), 256, emit_w=True)
    else:
        w_down_b = wb[1]
        y = _down_ln(act, w_down_b, x1, ln2_g, ln2_b, alpha, min(m, 512), 512)
    return y, s_new, h, tail_g, tail_v, v_rows, (w_in_b, w_down_b)


def kernel(x_prompt, x_sample, state_dn, state_conv_qkv, state_conv_ffn, w_in, conv_qkv_w, dn_a_log, dn_dt_bias, dn_norm_g, gm_ln_g, gm_ln_b, gm_ws, gm_bs, gm_out_g, w_out, ln1_g, ln1_b, w_up, conv_ffn_w, conv_ffn_b, w_down, ln2_g, ln2_b):
    depth = w_in.shape[0]
    bp, lp, d_model = x_prompt.shape
    bs, ls, _ = x_sample.shape
    d_qkv = conv_qkv_w.shape[-1]
    d_a = gm_ws.shape[1] * GROUP
    alpha = (2.0 * depth) ** 0.25

    hp = x_prompt.reshape(bp * lp, d_model)
    hs = jnp.transpose(x_sample, (1, 0, 2)).reshape(ls * bs, d_model)
    outs = [[] for _ in range(7)]
    w_out = w_out.astype(BF16)
    for l in range(depth):
        lw = tuple(t[l] for t in (w_in, conv_qkv_w, dn_a_log, dn_dt_bias, dn_norm_g, gm_ln_g, gm_ln_b, gm_ws,
                                  gm_bs, gm_out_g, w_out, ln1_g, ln1_b, w_up, conv_ffn_w, conv_ffn_b, w_down,
                                  ln2_g, ln2_b))
        bq = jnp.transpose(state_conv_qkv[l], (1, 0, 2)).reshape((CONV_QKV - 1) * bs, d_qkv)
        bf = jnp.transpose(state_conv_ffn[l], (1, 0, 2)).reshape((CONV_FFN - 1) * bs, -1)
        hs, s_s, h_s, tg_s, tv_s, v_s, wb = _layer(hs, lw, alpha, 1, ls * bs, bs, (state_dn[l], bq, bf), None)
        h_s3 = h_s.reshape(ls, bs, -1)
        cq_s = jnp.transpose(h_s3[ls - (CONV_QKV - 1):, :, 2 * d_a:2 * d_a + d_qkv], (1, 0, 2))
        cf_s = jnp.transpose(jnp.concatenate([tg_s, tv_s], axis=-1).reshape(CONV_FFN - 1, bs, -1), (1, 0, 2))
        hp, s_p, h_p, tg_p, tv_p, _, _ = _layer(hp, lw, alpha, bp, lp, 1, None, wb)
        h_p3 = h_p.reshape(bp, lp, -1)
        cq_p = h_p3[:, lp - (CONV_QKV - 1):, 2 * d_a:2 * d_a + d_qkv]
        n_tail = tg_p.shape[1]
        cf_p = jnp.concatenate([tg_p, tv_p], axis=-1)[:, n_tail - (CONV_FFN - 1):]
        for acc, val in zip(outs, (s_p, cq_p, cf_p, s_s, cq_s, cf_s, jnp.transpose(v_s, (1, 0, 2)))):
            acc.append(val)
    y_p = hp.reshape(bp, lp, d_model)
    y_s = jnp.transpose(hs.reshape(ls, bs, d_model), (1, 0, 2))
    return (y_p, y_s) + tuple(jnp.stack(o) for o in outs)
```

```python
import functools

import jax
import jax.numpy as jnp
from jax import lax
from jax.experimental import pallas as pl
from jax.experimental.pallas import tpu as pltpu

F32 = jnp.float32
BF16 = jnp.bfloat16

LANES = 128
SUBLANES = 8
VMEM_LIMIT = 56 * 1024 * 1024

GROUP = 128
MLP_CHUNK = 128
DN_CHUNK = 64
CONV_QKV = 4
CONV_FFN = 3
FFN_SUB = 1
EPS = 1e-5

_NT = (((1,), (1,)), ((), ()))
_TN = (((0,), (0,)), ((), ()))


def _params(sem):
    return pltpu.CompilerParams(dimension_semantics=sem, vmem_limit_bytes=VMEM_LIMIT)


def _bdot(a, b):
    return jnp.dot(a.astype(BF16), b.astype(BF16), preferred_element_type=F32)


def _gelu(x):
    return 0.5 * x * (1.0 + lax.erf(x * (0.5 ** 0.5)))


def _silu(x):
    return x * jax.nn.sigmoid(x)


def _layer_norm(x, g, b):
    mu = jnp.mean(x, axis=-1, keepdims=True)
    xc = x - mu
    var = jnp.mean(xc * xc, axis=-1, keepdims=True)
    return xc * lax.rsqrt(var + EPS) * g + b


def _rms_norm(x, g):
    ms = jnp.mean(x * x, axis=-1, keepdims=True)
    return x * lax.rsqrt(ms + EPS) * g


def _mm_kernel(x_ref, w_ref, o_ref, *rest):
    xb_ref = rest[-1]

    @pl.when(pl.program_id(1) == 0)
    def _():
        xb_ref[...] = x_ref[...].astype(BF16)

    wb = w_ref[...].astype(BF16)
    if len(rest) == 2:
        rest[0][...] = wb
    o_ref[...] = lax.dot_general(xb_ref[...], wb, _NT, preferred_element_type=F32)


def _matmul(x, wt, n_cols, tm, tn, emit_w=False):
    m, k = x.shape
    out_specs = [pl.BlockSpec((tm, tn), lambda i, j: (i, j))]
    out_shape = [jax.ShapeDtypeStruct((m, n_cols), F32)]
    if emit_w:
        out_specs.append(pl.BlockSpec((tn, k), lambda i, j: (j, 0)))
        out_shape.append(jax.ShapeDtypeStruct((n_cols, k), BF16))
    outs = pl.pallas_call(
        _mm_kernel,
        grid=(m // tm, n_cols // tn),
        in_specs=[pl.BlockSpec((tm, k), lambda i, j: (i, 0)),
                  pl.BlockSpec((tn, k), lambda i, j: (j, 0))],
        out_specs=out_specs,
        out_shape=out_shape,
        scratch_shapes=[pltpu.VMEM((tm, k), BF16)],
        compiler_params=_params(("arbitrary", "arbitrary")),
        name="proj",
    )(x, wt)
    return tuple(outs) if emit_w else outs[0]


def _cmlp_prompt_kernel(u_ref, v_ref, lng_ref, lnb_ref, ws_ref, bst_ref, og_ref, ya_ref, *, n_groups):
    row = lax.broadcasted_iota(jnp.int32, (MLP_CHUNK, MLP_CHUNK), 0)
    col = lax.broadcasted_iota(jnp.int32, (MLP_CHUNK, MLP_CHUNK), 1)
    gs = range(n_groups)
    sls = [slice(g * GROUP, (g + 1) * GROUP) for g in gs]
    vh = [_layer_norm(_gelu(v_ref[:, sl]), lng_ref[:, sl], lnb_ref[:, sl]) for sl in sls]
    mixed = [_bdot(jnp.where(row >= col, ws_ref[g], 0.0), vh[g]) + bst_ref[:, g:g + 1] for g in gs]
    y = [_gelu(u_ref[:, sls[g]]) * mixed[g] for g in gs]
    for g in gs:
        ya_ref[:, sls[g]] = _rms_norm(y[g], og_ref[:, sls[g]]).astype(ya_ref.dtype)


def _chunk_mlp_prompt(h, ln_g, ln_b, ws, bs, out_g):
    m = h.shape[0]
    n_groups, d_a = ws.shape[0], ws.shape[0] * GROUP
    vec = pl.BlockSpec((1, d_a), lambda i: (0, 0))
    return pl.pallas_call(
        functools.partial(_cmlp_prompt_kernel, n_groups=n_groups),
        grid=(m // MLP_CHUNK,),
        in_specs=[pl.BlockSpec((MLP_CHUNK, d_a), lambda i: (i, 0)),
                  pl.BlockSpec((MLP_CHUNK, d_a), lambda i: (i, 1)),
                  vec, vec,
                  pl.BlockSpec((n_groups, MLP_CHUNK, MLP_CHUNK), lambda i: (0, 0, 0)),
                  pl.BlockSpec((MLP_CHUNK, n_groups), lambda i: (0, 0)),
                  vec],
        out_specs=pl.BlockSpec((MLP_CHUNK, d_a), lambda i: (i, 0)),
        out_shape=jax.ShapeDtypeStruct((m, d_a), BF16),
        compiler_params=_params(("arbitrary",)),
        name="chunk_mlp_prompt",
    )(h, h, ln_g.reshape(1, d_a), ln_b.reshape(1, d_a), ws, bs.T, out_g.reshape(1, d_a))


def _cmlp_sample_kernel(u_ref, v_ref, lng_ref, lnb_ref, wv_ref, bv_ref, og_ref, ya_ref, vr_ref, *, n_pos):
    vh = []
    for t in range(n_pos):
        vt = _layer_norm(_gelu(v_ref[t]), lng_ref[0], lnb_ref[0])
        vr_ref[t] = vt
        vh.append(vt)
    for t in range(n_pos):
        mixed = wv_ref[t, 0:1, :] * vh[0]
        for s in range(1, t + 1):
            mixed = mixed + wv_ref[t, s:s + 1, :] * vh[s]
        mixed = mixed + bv_ref[0, t:t + 1, :]
        y = _gelu(u_ref[t]) * mixed
        ya_ref[t] = _rms_norm(y, og_ref[0]).astype(ya_ref.dtype)


def _chunk_mlp_sample(h3, ln_g, ln_b, ws, bs, out_g):
    n_pos, nb, _ = h3.shape
    n_groups = ws.shape[0]
    d_a = n_groups * GROUP
    wv = jnp.repeat(jnp.transpose(ws[:, :n_pos, :n_pos], (1, 2, 0)), GROUP, axis=-1)
    bv = jnp.repeat(bs[:, :n_pos].T, GROUP, axis=-1).reshape(1, n_pos, d_a)
    vec = pl.BlockSpec((1, 1, GROUP), lambda g: (0, 0, g))
    act = pl.BlockSpec((n_pos, nb, GROUP), lambda g: (0, 0, g))
    return pl.pallas_call(
        functools.partial(_cmlp_sample_kernel, n_pos=n_pos),
        grid=(n_groups,),
        in_specs=[act,
                  pl.BlockSpec((n_pos, nb, GROUP), lambda g: (0, 0, n_groups + g)),
                  vec, vec,
                  pl.BlockSpec((n_pos, n_pos, GROUP), lambda g: (0, 0, g)),
                  pl.BlockSpec((1, n_pos, GROUP), lambda g: (0, 0, g)),
                  vec],
        out_specs=[act, act],
        out_shape=[jax.ShapeDtypeStruct((n_pos, nb, d_a), BF16),
                   jax.ShapeDtypeStruct((n_pos, nb, d_a), F32)],
        compiler_params=_params(("arbitrary",)),
        name="chunk_mlp_sample",
    )(h3, h3, ln_g.reshape(1, 1, d_a), ln_b.reshape(1, 1, d_a), wv, bv, out_g.reshape(1, 1, d_a))


def _conv_taps(xbuf_ref, w_ref, tm, shift, halo, width, cols=slice(None)):
    acc = None
    for j in range(width):
        off = halo - (width - 1 - j) * shift
        term = w_ref[j:j + 1, cols] * xbuf_ref[off:off + tm, :]
        acc = term if acc is None else acc + term
    return acc


def _qkv_prep_kernel(*refs, tm, shift, halo, n_heads, has_buf):
    if has_buf:
        (xq_ref, xk_ref, xv_ref, bq_ref, bk_ref, bv_ref, wq_ref, wk_ref, wv_ref,
         q_ref, k_ref, v_ref, xbuf_ref, carry_ref) = refs
        bufs = (bq_ref, bk_ref, bv_ref)
    else:
        (xq_ref, xk_ref, xv_ref, wq_ref, wk_ref, wv_ref,
         q_ref, k_ref, v_ref, xbuf_ref, carry_ref) = refs
        bufs = (None, None, None)
    first = pl.program_id(1) == 0
    parts = ((xq_ref, wq_ref, q_ref, GROUP ** -0.5), (xk_ref, wk_ref, k_ref, 1.0), (xv_ref, wv_ref, v_ref, None))
    for p, (x_ref, w_ref, o_ref, scale) in enumerate(parts):
        @pl.when(first)
        def _(p=p):
            if has_buf:
                xbuf_ref[0:halo, :] = bufs[p][...]
            else:
                xbuf_ref[0:halo, :] = jnp.zeros((halo, xbuf_ref.shape[1]), F32)

        @pl.when(jnp.logical_not(first))
        def _(p=p):
            xbuf_ref[0:halo, :] = carry_ref[p]

        xbuf_ref[halo:halo + tm, :] = x_ref[...]
        carry_ref[p] = xbuf_ref[tm:tm + halo, :]
        y = _silu(_conv_taps(xbuf_ref, w_ref, tm, shift, halo, CONV_QKV))
        if scale is None:
            o_ref[...] = y
        else:
            for h in range(n_heads):
                sl = slice(h * GROUP, (h + 1) * GROUP)
                yh = y[:, sl]
                yn = yh * lax.rsqrt(jnp.sum(yh * yh, axis=-1, keepdims=True) + 1e-6)
                o_ref[:, sl] = yn * scale if scale != 1.0 else yn


def _qkv_prep(h, conv_w, bufs, n_seq, rows_per_seq, tm, shift, col0):
    d_b = conv_w.shape[1] // 3
    n_heads = d_b // GROUP
    has_buf = bufs is not None
    halo = (CONV_QKV - 1) * shift if has_buf else SUBLANES
    tiles = rows_per_seq // tm
    cb = col0 // d_b
    x_specs = [pl.BlockSpec((tm, d_b), lambda s, i, c=c: (s * tiles + i, cb + c)) for c in range(3)]
    w_specs = [pl.BlockSpec((CONV_QKV, d_b), lambda s, i, c=c: (0, c)) for c in range(3)]
    o_spec = pl.BlockSpec((tm, d_b), lambda s, i: (s * tiles + i, 0))
    args, specs = [h, h, h], list(x_specs)
    if has_buf:
        specs += [pl.BlockSpec((halo, d_b), lambda s, i, c=c: (s, c)) for c in range(3)]
        args += [bufs, bufs, bufs]
    specs += w_specs
    args += [conv_w, conv_w, conv_w]
    m = n_seq * rows_per_seq
    return pl.pallas_call(
        functools.partial(_qkv_prep_kernel, tm=tm, shift=shift, halo=halo, n_heads=n_heads, has_buf=has_buf),
        grid=(n_seq, tiles),
        in_specs=specs,
        out_specs=[o_spec, o_spec, o_spec],
        out_shape=[jax.ShapeDtypeStruct((m, d_b), F32)] * 3,
        scratch_shapes=[pltpu.VMEM((halo + tm, d_b), F32), pltpu.VMEM((3, halo, d_b), F32)],
        compiler_params=_params(("arbitrary", "arbitrary")),
        name="qkv_prep",
    )(*args)


def _gates(ab, alog, dtb):
    g = -jnp.exp(alog) * jax.nn.softplus(ab + dtb)
    beta = jax.nn.sigmoid(ab)
    return g, beta


def _delta_prompt_kernel(q_ref, k_ref, v_ref, z_ref, ab_ref, alog_ref, dtb_ref, ng_ref,
                         yb_ref, sout_ref, s_ref, *, n_heads, n_chunks, ns):
    c = DN_CHUNK
    ci = pl.program_id(1)

    @pl.when(ci == 0)
    def _():
        s_ref[...] = jnp.zeros(s_ref.shape, F32)

    row = lax.broadcasted_iota(jnp.int32, (c, c), 0)
    col = lax.broadcasted_iota(jnp.int32, (c, c), 1)
    causal = row >= col
    strict = row > col
    eye = (row == col).astype(F32)
    tril = causal.astype(F32)
    n_double = (c - 1).bit_length() - 1
    ch = [(s, h) for s in range(ns) for h in range(n_heads)]
    sl = lambda h: slice(h * GROUP, (h + 1) * GROUP)
    gates = [_gates(ab_ref[s], alog_ref[...], dtb_ref[...]) for s in range(ns)]
    gc_s = [jnp.dot(tril, gates[s][0], precision=lax.Precision.HIGHEST, preferred_element_type=F32)
            for s in range(ns)]
    gct_s = [g.T for g in gc_s]
    q = [q_ref[s, :, sl(h)] for s, h in ch]
    k = [k_ref[s, :, sl(h)] for s, h in ch]
    gcol = [gc_s[s][:, h:h + 1] for s, h in ch]
    beta = [gates[s][1][:, n_heads + h:n_heads + h + 1] for s, h in ch]
    n = range(len(ch))
    decay = [jnp.exp(jnp.where(causal, gcol[i] - gct_s[s][h:h + 1, :], -jnp.inf)) for i, (s, h) in enumerate(ch)]
    kb = [k[i] * beta[i] for i in n]
    kbf = [k[i].astype(BF16) for i in n]
    a = [lax.dot_general(kb[i].astype(BF16), kbf[i], _NT, preferred_element_type=F32) * decay[i] for i in n]
    y = [jnp.where(strict, -a[i], 0.0) for i in n]
    x = [eye + y[i] for i in n]
    for _ in range(n_double):
        y = [_bdot(y[i], y[i]) for i in n]
        x = [x[i] + _bdot(x[i], y[i]) for i in n]
    egc = [jnp.exp(gcol[i]) for i in n]
    sol = [_bdot(x[i], jnp.concatenate([v_ref[s, :, sl(h)] * beta[i], kb[i] * egc[i]], axis=1))
           for i, (s, h) in enumerate(ch)]
    qk = [lax.dot_general(q[i].astype(BF16), kbf[i], _NT, preferred_element_type=F32) * decay[i] for i in n]
    s_old = [s_ref[s, h] for s, h in ch]
    r = [_bdot(jnp.concatenate([sol[i][:, GROUP:], q[i] * egc[i]], axis=0), s_old[i]) for i in n]
    v_new = [sol[i][:, :GROUP] - r[i][:c] for i in n]
    o = [r[i][c:] + _bdot(qk[i], v_new[i]) for i in n]
    gl = [gc_s[s][c - 1:c, h:h + 1] for s, h in ch]
    upd = [lax.dot_general((k[i] * jnp.exp(gl[i] - gcol[i])).astype(BF16), v_new[i].astype(BF16), _TN,
                           preferred_element_type=F32) for i in n]
    for i, (s, h) in enumerate(ch):
        s_ref[s, h] = s_old[i] * jnp.exp(gl[i]) + upd[i]
        yb_ref[s, :, sl(h)] = (_rms_norm(o[i], ng_ref[...]) * _silu(z_ref[s, :, sl(h)])).astype(yb_ref.dtype)

    @pl.when(ci == n_chunks - 1)
    def _():
        sout_ref[...] = s_ref[...]


def _delta_prompt(q, k, v, h, z_col0, ab, alog, dtb, norm_g, n_seq, rows_per_seq, ns):
    d_b = q.shape[1]
    n_heads = d_b // GROUP
    n_chunks = rows_per_seq // DN_CHUNK
    r3 = lambda t: t.reshape(n_seq, rows_per_seq, t.shape[-1])
    act = pl.BlockSpec((ns, DN_CHUNK, d_b), lambda s, i: (s, i, 0))
    vec = pl.BlockSpec((1, LANES), lambda s, i: (0, 0))
    zb = z_col0 // d_b
    yb, s_new = pl.pallas_call(
        functools.partial(_delta_prompt_kernel, n_heads=n_heads, n_chunks=n_chunks, ns=ns),
        grid=(n_seq // ns, n_chunks),
        in_specs=[act, act, act,
                  pl.BlockSpec((ns, DN_CHUNK, d_b), lambda s, i: (s, i, zb)),
                  pl.BlockSpec((ns, DN_CHUNK, LANES), lambda s, i: (s, i, 0)),
                  vec, vec, vec],
        out_specs=[act, pl.BlockSpec((ns, n_heads, GROUP, GROUP), lambda s, i: (s, 0, 0, 0))],
        out_shape=[jax.ShapeDtypeStruct((n_seq, rows_per_seq, d_b), BF16),
                   jax.ShapeDtypeStruct((n_seq, n_heads, GROUP, GROUP), F32)],
        scratch_shapes=[pltpu.VMEM((ns, n_heads, GROUP, GROUP), F32)],
        compiler_params=_params(("arbitrary", "arbitrary")),
        name="delta_prompt",
    )(r3(q), r3(k), r3(v), r3(h), r3(ab), alog, dtb, norm_g)
    return yb.reshape(n_seq * rows_per_seq, d_b), s_new


def _delta_sample_kernel(q_ref, k_ref, v_ref, z_ref, ab_ref, alog_ref, dtb_ref, ng_ref, s0_ref,
                         yb_ref, s1_ref, lhs_ref, res_ref, kt_ref, vn_ref, gl_ref, *, n_heads, n_pos, bb):
    nt = n_pos
    hs = range(n_heads)
    sls = [slice(h * GROUP, (h + 1) * GROUP) for h in hs]
    gates = [_gates(ab_ref[t], alog_ref[...], dtb_ref[...]) for t in range(nt)]
    zeros_pad = jnp.zeros((bb, GROUP), F32)
    u_all, qk_all, gc_all = [], [], []
    for h in hs:
        q = [q_ref[t, :, sls[h]] for t in range(nt)]
        k = [k_ref[t, :, sls[h]] for t in range(nt)]
        beta = [gates[t][1][:, n_heads + h:n_heads + h + 1] for t in range(nt)]
        gc = []
        for t in range(nt):
            gt = gates[t][0][:, h:h + 1]
            gc.append(gt if t == 0 else gc[t - 1] + gt)
        kb = [k[t] * beta[t] for t in range(nt)]
        dec = [[jnp.exp(gc[t] - gc[s]) for s in range(t + 1)] for t in range(nt)]
        a = [[jnp.sum(kb[t] * k[s], axis=-1, keepdims=True) * dec[t][s] for s in range(t)] for t in range(nt)]
        egc = [jnp.exp(gc[t]) for t in range(nt)]
        u_blk, w_blk = [], []
        for t in range(nt):
            ut, wt = v_ref[t, :, sls[h]] * beta[t], kb[t] * egc[t]
            for s in range(t):
                ut = ut - a[t][s] * u_blk[s]
                wt = wt - a[t][s] * w_blk[s]
            u_blk.append(ut)
            w_blk.append(wt)
        qk_all.append([[jnp.sum(q[t] * k[s], axis=-1, keepdims=True) * dec[t][s] for s in range(t + 1)]
                       for t in range(nt)])
        u_all.append(u_blk)
        gc_all.append(gc)
        for t in range(nt):
            lhs_ref[h, :, t, :] = w_blk[t]
            lhs_ref[h, :, nt + t, :] = q[t] * egc[t]
        gl_ref[h] = jnp.broadcast_to(jnp.exp(gc[nt - 1]), (bb, GROUP))

    for h in hs:
        for b in range(bb):
            res_ref[h, b] = _bdot(lhs_ref[h, b], s0_ref[b, h])

    for h in hs:
        gc, gl = gc_all[h], gc_all[h][nt - 1]
        v_new = [u_all[h][t] - res_ref[h, :, t, :] for t in range(nt)]
        for t in range(nt):
            o = res_ref[h, :, nt + t, :]
            for s in range(t + 1):
                o = o + qk_all[h][t][s] * v_new[s]
            yb_ref[t, :, sls[h]] = (_rms_norm(o, ng_ref[...]) * _silu(z_ref[t, :, sls[h]])).astype(yb_ref.dtype)
            kt_ref[h, :, t, :] = k_ref[t, :, sls[h]] * jnp.exp(gl - gc[t])
            vn_ref[h, :, t, :] = v_new[t]
            kt_ref[h, :, nt + t, :] = zeros_pad
            vn_ref[h, :, nt + t, :] = zeros_pad

    for h in hs:
        for b in range(bb):
            upd = lax.dot_general(kt_ref[h, b].astype(BF16), vn_ref[h, b].astype(BF16), _TN,
                                  preferred_element_type=F32)
            s1_ref[b, h] = s0_ref[b, h] * gl_ref[h, b:b + 1, :] + upd


def _delta_sample(q, k, v, h3, z_col0, ab, alog, dtb, norm_g, s0, bb):
    n_pos, nb, d_b = q.shape
    n_heads = d_b // GROUP
    assert 2 * n_pos == SUBLANES, "one state read packs [w | q] rows of a sequence into one 8-row tile"
    zb = z_col0 // d_b
    act = pl.BlockSpec((n_pos, bb, d_b), lambda i: (0, i, 0))
    vec = pl.BlockSpec((1, LANES), lambda i: (0, 0))
    st = pl.BlockSpec((bb, n_heads, GROUP, GROUP), lambda i: (i, 0, 0, 0))
    tile = pltpu.VMEM((n_heads, bb, SUBLANES, GROUP), F32)
    return pl.pallas_call(
        functools.partial(_delta_sample_kernel, n_heads=n_heads, n_pos=n_pos, bb=bb),
        grid=(nb // bb,),
        in_specs=[act, act, act,
                  pl.BlockSpec((n_pos, bb, d_b), lambda i: (0, i, zb)),
                  pl.BlockSpec((n_pos, bb, LANES), lambda i: (0, i, 0)),
                  vec, vec, vec, st],
        out_specs=[act, st],
        out_shape=[jax.ShapeDtypeStruct((n_pos, nb, d_b), BF16),
                   jax.ShapeDtypeStruct(s0.shape, F32)],
        scratch_shapes=[tile, tile, tile, tile, pltpu.VMEM((n_heads, bb, GROUP), F32)],
        compiler_params=_params(("arbitrary",)),
        name="delta_sample",
    )(q, k, v, h3, ab, alog, dtb, norm_g, s0)


def _out_proj_kernel(ya_ref, yb_ref, w_ref, r_ref, g_ref, b_ref, o_ref, ob_ref, z_ref, *, n_tiles, d_a, alpha):
    i = pl.program_id(0)

    @pl.when(i == 0)
    def _():
        z_ref[...] = jnp.zeros(z_ref.shape, F32)

    def emit_ln():
        y = _layer_norm(z_ref[...], g_ref[...], b_ref[...])
        o_ref[...] = y
        ob_ref[...] = y.astype(ob_ref.dtype)

    @pl.when(i < n_tiles)
    def _():
        emit_ln()
        acc = (jnp.dot(ya_ref[...], w_ref[0:d_a, :], preferred_element_type=F32)
               + jnp.dot(yb_ref[...], w_ref[d_a:, :], preferred_element_type=F32))
        z_ref[...] = alpha * r_ref[...] + acc

    @pl.when(i == n_tiles)
    def _():
        emit_ln()


def _out_proj_ln(ya, yb, w, resid, g, b, alpha, tm):
    m, d_a = ya.shape
    k, d = w.shape
    n_tiles = m // tm
    cur = lambda i: (jnp.minimum(i, n_tiles - 1), 0)
    prev = lambda i: (jnp.maximum(i - 1, 0), 0)
    vec = pl.BlockSpec((1, d), lambda i: (0, 0))
    return pl.pallas_call(
        functools.partial(_out_proj_kernel, n_tiles=n_tiles, d_a=d_a, alpha=alpha),
        grid=(n_tiles + 1,),
        in_specs=[pl.BlockSpec((tm, d_a), cur),
                  pl.BlockSpec((tm, k - d_a), cur),
                  pl.BlockSpec((k, d), lambda i: (0, 0), pipeline_mode=pl.Buffered(1)),
                  pl.BlockSpec((tm, d), cur), vec, vec],
        out_specs=[pl.BlockSpec((tm, d), prev), pl.BlockSpec((tm, d), prev)],
        out_shape=[jax.ShapeDtypeStruct((m, d), F32), jax.ShapeDtypeStruct((m, d), BF16)],
        scratch_shapes=[pltpu.VMEM((tm, d), F32)],
        compiler_params=_params(("arbitrary",)),
        name="out_proj_ln1",
    )(ya, yb, w, resid, g.reshape(1, d), b.reshape(1, d))


def _down_kernel(x_ref, w_ref, r_ref, g_ref, b_ref, o_ref, *rest, n_tiles, nk, alpha):
    acc_ref, z_ref = rest[-2:]
    i, kk = pl.program_id(0), pl.program_id(1)

    @pl.when(jnp.logical_and(i == 0, kk == 0))
    def _():
        z_ref[...] = jnp.zeros(z_ref.shape, F32)

    wb = w_ref[...].astype(BF16)
    if len(rest) == 3:
        rest[0][...] = wb

    @pl.when(kk == 0)
    def _():
        acc_ref[...] = alpha * r_ref[...] + jnp.dot(x_ref[...], wb, preferred_element_type=F32)
        o_ref[...] = _layer_norm(z_ref[...], g_ref[...], b_ref[...])

    @pl.when(jnp.logical_and(jnp.logical_and(kk > 0, kk < nk - 1), i < n_tiles))
    def _():
        acc_ref[...] += jnp.dot(x_ref[...], wb, preferred_element_type=F32)

    @pl.when(jnp.logical_and(kk == nk - 1, i < n_tiles))
    def _():
        z_ref[...] = acc_ref[...] + jnp.dot(x_ref[...], wb, preferred_element_type=F32)


def _down_ln(x, w, resid, g, b, alpha, tm, tk, emit_w=False):
    m, kdim = x.shape
    d = w.shape[1]
    nk = kdim // tk
    assert nk >= 2
    n_tiles = m // tm
    row = lambda i: jnp.minimum(i, n_tiles - 1)
    kt = lambda i, kk: jnp.where(i == n_tiles, nk - 1, kk)
    vec = pl.BlockSpec((1, d), lambda i, kk: (0, 0))
    wspec = pl.BlockSpec((tk, d), lambda i, kk: (kt(i, kk), 0))
    out_specs = [pl.BlockSpec((tm, d), lambda i, kk: (jnp.where(kk == 0, jnp.maximum(i - 1, 0), row(i)), 0))]
    out_shape = [jax.ShapeDtypeStruct((m, d), F32)]
    if emit_w:
        out_specs.append(wspec)
        out_shape.append(jax.ShapeDtypeStruct(w.shape, BF16))
    outs = pl.pallas_call(
        functools.partial(_down_kernel, n_tiles=n_tiles, nk=nk, alpha=alpha),
        grid=(n_tiles + 1, nk),
        in_specs=[pl.BlockSpec((tm, tk), lambda i, kk: (row(i), kt(i, kk))), wspec,
                  pl.BlockSpec((tm, d), lambda i, kk: (row(i), 0)), vec, vec],
        out_specs=out_specs,
        out_shape=out_shape,
        scratch_shapes=[pltpu.VMEM((tm, d), F32), pltpu.VMEM((tm, d), F32)],
        compiler_params=_params(("arbitrary", "arbitrary")),
        name="down_ln2",
    )(x, w, resid, g.reshape(1, d), b.reshape(1, d))
    return tuple(outs) if emit_w else outs[0]


def _ffn_up_kernel(*refs, tm, shift, halo, has_buf, n_sub):
    n_in = 9 if has_buf else 7
    if has_buf:
        x_ref, wg_ref, wv_ref, bg_ref, bv_ref, cwg_ref, cwv_ref, cbg_ref, cbv_ref = refs[:n_in]
        bufs = (bg_ref, bv_ref)
    else:
        x_ref, wg_ref, wv_ref, cwg_ref, cwv_ref, cbg_ref, cbv_ref = refs[:n_in]
        bufs = (None, None)
    act_ref, tg_ref, tv_ref = refs[n_in:n_in + 3]
    wgb_ref, wvb_ref = refs[n_in + 3:n_in + 5]
    xbufs, carry_ref = refs[n_in + 5:-1], refs[-1]

    @pl.when(jnp.logical_and(pl.program_id(1) == 0, pl.program_id(2) == 0))
    def _():
        wgb_ref[...] = wg_ref[...].astype(BF16)
        wvb_ref[...] = wv_ref[...].astype(BF16)

    wsub = act_ref.shape[1] // n_sub
    cols = [slice(c * wsub, (c + 1) * wsub) for c in range(n_sub)]
    first = pl.program_id(2) == 0

    @pl.when(first)
    def _():
        for c in range(n_sub):
            for p in range(2):
                if has_buf:
                    xbufs[2 * c + p][0:halo, :] = bufs[p][:, cols[c]]
                else:
                    xbufs[2 * c + p][0:halo, :] = jnp.zeros((halo, wsub), F32)

    @pl.when(jnp.logical_not(first))
    def _():
        for c in range(n_sub):
            for p in range(2):
                xbufs[2 * c + p][0:halo, :] = carry_ref[p, :, cols[c]]

    x = x_ref[...]
    parts = ((wgb_ref, cwg_ref, cbg_ref, tg_ref), (wvb_ref, cwv_ref, cbv_ref, tv_ref))
    for c in range(n_sub):
        conv = []
        for p, (w_ref, cw_ref, cb_ref, t_ref) in enumerate(parts):
            xbuf_ref = xbufs[2 * c + p]
            xbuf_ref[halo:halo + tm, :] = jnp.dot(x, w_ref[:, cols[c]], preferred_element_type=F32)
            tail = xbuf_ref[tm:tm + halo, :]
            carry_ref[p, :, cols[c]] = tail
            t_ref[0, :, cols[c]] = tail
            conv.append(_conv_taps(xbuf_ref, cw_ref, tm, shift, halo, CONV_FFN, cols[c]) + cb_ref[:, cols[c]])
        act_ref[:, cols[c]] = (_silu(conv[0]) * conv[1]).astype(act_ref.dtype)


def _ffn_up(x, w_up, conv_w, conv_b, bufs, n_seq, rows_per_seq, tm, tn, shift):
    m, d = x.shape
    d_ff = w_up.shape[1] // 2
    nj = d_ff // tn
    has_buf = bufs is not None
    halo = (CONV_FFN - 1) * shift if has_buf else SUBLANES
    tiles = rows_per_seq // tm
    wspec = lambda off: pl.BlockSpec((d, tn), lambda j, s, i: (0, off + j))
    cwspec = lambda off: pl.BlockSpec((CONV_FFN, tn), lambda j, s, i: (0, off + j))
    cbspec = lambda off: pl.BlockSpec((1, tn), lambda j, s, i: (0, off + j))
    tspec = pl.BlockSpec((1, halo, tn), lambda j, s, i: (s, 0, j))
    specs = [pl.BlockSpec((tm, d), lambda j, s, i: (s * tiles + i, 0)), wspec(0), wspec(nj)]
    args = [x, w_up, w_up]
    if has_buf:
        specs += [pl.BlockSpec((halo, tn), lambda j, s, i, off=off: (s, off + j)) for off in (0, nj)]
        args += [bufs, bufs]
    specs += [cwspec(0), cwspec(nj), cbspec(0), cbspec(nj)]
    cb2 = conv_b.reshape(1, 2 * d_ff)
    args += [conv_w, conv_w, cb2, cb2]
    return pl.pallas_call(
        functools.partial(_ffn_up_kernel, tm=tm, shift=shift, halo=halo, has_buf=has_buf, n_sub=FFN_SUB),
        grid=(nj, n_seq, tiles),
        in_specs=specs,
        out_specs=[pl.BlockSpec((tm, tn), lambda j, s, i: (s * tiles + i, j)), tspec, tspec],
        out_shape=[jax.ShapeDtypeStruct((m, d_ff), BF16),
                   jax.ShapeDtypeStruct((n_seq, halo, d_ff), F32),
                   jax.ShapeDtypeStruct((n_seq, halo, d_ff), F32)],
        scratch_shapes=[pltpu.VMEM((d, tn), BF16)] * 2
                       + [pltpu.VMEM((halo + tm, tn // FFN_SUB), F32)] * (2 * FFN_SUB)
                       + [pltpu.VMEM((2, halo, tn), F32)],
        compiler_params=_params(("arbitrary", "arbitrary", "arbitrary")),
        name="ffn_up",
    )(*args)


def _pad_lanes(vec, offset=0):
    out = jnp.zeros((1, LANES), F32)
    return out.at[0, offset:offset + vec.shape[0]].set(vec.astype(F32))


def _layer(x2, lw, alpha, n_seq, rows_per_seq, shift, state, wb):
    (w_in, conv_qkv_w, a_log, dt_bias, norm_g, gm_ln_g, gm_ln_b, gm_ws, gm_bs, gm_out_g, w_out,
     ln1_g, ln1_b, w_up, conv_ffn_w, conv_ffn_b, w_down, ln2_g, ln2_b) = lw
    m, d_model = x2.shape
    n_groups = gm_ws.shape[0]
    d_a = n_groups * GROUP
    d_qkv = conv_qkv_w.shape[1]
    d_b = d_qkv // 3
    n_heads = d_b // GROUP
    main_cols = 2 * d_a + d_qkv + d_b
    fresh = state is None
    tm = min(m, 1024)

    if wb is None:
        h, w_in_b = _matmul(x2, w_in, main_cols, tm, 1024, emit_w=True)
    else:
        w_in_b = wb[0]
        h = _matmul(x2, w_in_b, main_cols, tm, 1024)
    w_ab = jnp.pad(w_in[main_cols:].astype(BF16), ((0, LANES - 2 * n_heads), (0, 0)))
    ab = _matmul(x2, w_ab, LANES, tm, LANES)
    alog, dtb, ng = _pad_lanes(a_log), _pad_lanes(dt_bias), norm_g.reshape(1, GROUP).astype(F32)

    if fresh:
        ya = _chunk_mlp_prompt(h, gm_ln_g, gm_ln_b, gm_ws, gm_bs, gm_out_g)
        v_rows = None
        q, k, v = _qkv_prep(h, conv_qkv_w, None, n_seq, rows_per_seq, 512, shift, 2 * d_a)
        yb, s_new = _delta_prompt(q, k, v, h, 2 * d_a + d_qkv, ab, alog, dtb, ng, n_seq, rows_per_seq, 4)
        ffn_bufs = None
        tm_ffn = 1024
    else:
        s_dn, buf_qkv, buf_ffn = state
        n_pos = rows_per_seq // shift
        h3 = h.reshape(n_pos, shift, main_cols)
        ya3, v_rows = _chunk_mlp_sample(h3, gm_ln_g, gm_ln_b, gm_ws, gm_bs, gm_out_g)
        ya = ya3.reshape(m, d_a)
        q, k, v = _qkv_prep(h, conv_qkv_w, buf_qkv, n_seq, rows_per_seq, rows_per_seq, shift, 2 * d_a)
        r3 = lambda t: t.reshape(n_pos, shift, t.shape[-1])
        yb3, s_new = _delta_sample(r3(q), r3(k), r3(v), h3, 2 * d_a + d_qkv, r3(ab), alog, dtb, ng, s_dn, 8)
        yb = yb3.reshape(m, d_b)
        ffn_bufs = buf_ffn
        tm_ffn = rows_per_seq

    x1, x1b = _out_proj_ln(ya, yb, w_out, x2, ln1_g, ln1_b, alpha, min(m, 512))
    act, tail_g, tail_v = _ffn_up(x1b, w_up, conv_ffn_w, conv_ffn_b, ffn_bufs, n_seq, rows_per_seq, tm_ffn, 512, shift)
    if wb is None:
        y, w_down_b = _down_ln(act, w_down, x1, ln2_g, ln2_b, alpha, min(m, 512), 512, emit_w=True)
    else:
        w_down_b = wb[1]
        y = _down_ln(act, w_down_b, x1, ln2_g, ln2_b, alpha, min(m, 512), 1408)
    return y, s_new, h, tail_g, tail_v, v_rows, (w_in_b, w_down_b)


def kernel(x_prompt, x_sample, state_dn, state_conv_qkv, state_conv_ffn, w_in, conv_qkv_w, dn_a_log, dn_dt_bias, dn_norm_g, gm_ln_g, gm_ln_b, gm_ws, gm_bs, gm_out_g, w_out, ln1_g, ln1_b, w_up, conv_ffn_w, conv_ffn_b, w_down, ln2_g, ln2_b):
    depth = w_in.shape[0]
    bp, lp, d_model = x_prompt.shape
    bs, ls, _ = x_sample.shape
    d_qkv = conv_qkv_w.shape[-1]
    d_a = gm_ws.shape[1] * GROUP
    alpha = (2.0 * depth) ** 0.25

    hp = x_prompt.reshape(bp * lp, d_model)
    hs = jnp.transpose(x_sample, (1, 0, 2)).reshape(ls * bs, d_model)
    outs = [[] for _ in range(7)]
    w_out = w_out.astype(BF16)
    w_in = jnp.swapaxes(w_in, 1, 2)
    for l in range(depth):
        lw = tuple(t[l] for t in (w_in, conv_qkv_w, dn_a_log, dn_dt_bias, dn_norm_g, gm_ln_g, gm_ln_b, gm_ws,
                                  gm_bs, gm_out_g, w_out, ln1_g, ln1_b, w_up, conv_ffn_w, conv_ffn_b, w_down,
                                  ln2_g, ln2_b))
        bq = jnp.transpose(state_conv_qkv[l], (1, 0, 2)).reshape((CONV_QKV - 1) * bs, d_qkv)
        bf = jnp.transpose(state_conv_ffn[l], (1, 0, 2)).reshape((CONV_FFN - 1) * bs, -1)
        hs, s_s, h_s, tg_s, tv_s, v_s, wb = _layer(hs, lw, alpha, 1, ls * bs, bs, (state_dn[l], bq, bf), None)
        h_s3 = h_s.reshape(ls, bs, -1)
        cq_s = jnp.transpose(h_s3[ls - (CONV_QKV - 1):, :, 2 * d_a:2 * d_a + d_qkv], (1, 0, 2))
        cf_s = jnp.transpose(jnp.concatenate([tg_s, tv_s], axis=-1).reshape(CONV_FFN - 1, bs, -1), (1, 0, 2))
        hp, s_p, h_p, tg_p, tv_p, _, _ = _layer(hp, lw, alpha, bp, lp, 1, None, wb)
        h_p3 = h_p.reshape(bp, lp, -1)
        cq_p = h_p3[:, lp - (CONV_QKV - 1):, 2 * d_a:2 * d_a + d_qkv]
        n_tail = tg_p.shape[1]
        cf_p = jnp.concatenate([tg_p, tv_p], axis=-1)[:, n_tail - (CONV_FFN - 1):]
        for acc, val in zip(outs, (s_p, cq_p, cf_p, s_s, cq_s, cf_s, jnp.transpose(v_s, (1, 0, 2)))):
            acc.append(val)
    y_p = hp.reshape(bp, lp, d_model)
    y_s = jnp.transpose(hs.reshape(ls, bs, d_model), (1, 0, 2))
    return (y_p, y_s) + tuple(jnp.stack(o) for o in outs)
```

```python
import functools

import jax
import jax.numpy as jnp
from jax import lax
from jax.experimental import pallas as pl
from jax.experimental.pallas import tpu as pltpu

F32 = jnp.float32
BF16 = jnp.bfloat16

LANES = 128
SUBLANES = 8
VMEM_LIMIT = 56 * 1024 * 1024

GROUP = 128
MLP_CHUNK = 128
DN_CHUNK = 64
CONV_QKV = 4
CONV_FFN = 3
FFN_SUB = 1
CMLP_CHUNKS = 2
EPS = 1e-5

_NT = (((1,), (1,)), ((), ()))
_TN = (((0,), (0,)), ((), ()))


def _params(sem):
    return pltpu.CompilerParams(dimension_semantics=sem, vmem_limit_bytes=VMEM_LIMIT)


def _bdot(a, b):
    return jnp.dot(a.astype(BF16), b.astype(BF16), preferred_element_type=F32)


def _gelu(x):
    return 0.5 * x * (1.0 + lax.erf(x * (0.5 ** 0.5)))


def _silu(x):
    return x * jax.nn.sigmoid(x)


def _layer_norm(x, g, b):
    mu = jnp.mean(x, axis=-1, keepdims=True)
    xc = x - mu
    var = jnp.mean(xc * xc, axis=-1, keepdims=True)
    return xc * lax.rsqrt(var + EPS) * g + b


def _rms_norm(x, g):
    ms = jnp.mean(x * x, axis=-1, keepdims=True)
    return x * lax.rsqrt(ms + EPS) * g


def _mm_kernel(x_ref, w_ref, we_ref, o_ref, e_ref, *rest):
    xb_ref = rest[-1]

    @pl.when(pl.program_id(1) == 0)
    def _():
        xb = x_ref[...].astype(BF16)
        xb_ref[...] = xb
        e_ref[...] = lax.dot_general(xb, we_ref[...], _NT, preferred_element_type=F32)

    wb = w_ref[...].astype(BF16)
    if len(rest) == 2:
        rest[0][...] = wb
    o_ref[...] = lax.dot_general(xb_ref[...], wb, _NT, preferred_element_type=F32)


def _matmul(x, wt, wt_extra, n_cols, tm, tn, emit_w=False):
    m, k = x.shape
    ne = wt_extra.shape[0]
    out_specs = [pl.BlockSpec((tm, tn), lambda i, j: (i, j)), pl.BlockSpec((tm, ne), lambda i, j: (i, 0))]
    out_shape = [jax.ShapeDtypeStruct((m, n_cols), F32), jax.ShapeDtypeStruct((m, ne), F32)]
    if emit_w:
        out_specs.append(pl.BlockSpec((tn, k), lambda i, j: (j, 0)))
        out_shape.append(jax.ShapeDtypeStruct((n_cols, k), BF16))
    return pl.pallas_call(
        _mm_kernel,
        grid=(m // tm, n_cols // tn),
        in_specs=[pl.BlockSpec((tm, k), lambda i, j: (i, 0)),
                  pl.BlockSpec((tn, k), lambda i, j: (j, 0)),
                  pl.BlockSpec((ne, k), lambda i, j: (0, 0))],
        out_specs=out_specs,
        out_shape=out_shape,
        scratch_shapes=[pltpu.VMEM((tm, k), BF16)],
        compiler_params=_params(("arbitrary", "arbitrary")),
        name="proj",
    )(x, wt, wt_extra)


def _cmlp_prompt_kernel(u_ref, v_ref, lng_ref, lnb_ref, ws_ref, bst_ref, og_ref, ya_ref, *, n_groups, n_chunks):
    row = lax.broadcasted_iota(jnp.int32, (MLP_CHUNK, MLP_CHUNK), 0)
    col = lax.broadcasted_iota(jnp.int32, (MLP_CHUNK, MLP_CHUNK), 1)
    cg = [(c, g) for c in range(n_chunks) for g in range(n_groups)]
    rows = lambda c: slice(c * MLP_CHUNK, (c + 1) * MLP_CHUNK)
    cols = lambda g: slice(g * GROUP, (g + 1) * GROUP)
    w = [jnp.where(row >= col, ws_ref[g], 0.0).astype(BF16) for g in range(n_groups)]
    vh = [_layer_norm(_gelu(v_ref[rows(c), cols(g)]), lng_ref[:, cols(g)], lnb_ref[:, cols(g)]) for c, g in cg]
    mixed = [jnp.dot(w[g], vh[i].astype(BF16), preferred_element_type=F32) + bst_ref[:, g:g + 1]
             for i, (c, g) in enumerate(cg)]
    y = [_gelu(u_ref[rows(c), cols(g)]) * mixed[i] for i, (c, g) in enumerate(cg)]
    for i, (c, g) in enumerate(cg):
        ya_ref[rows(c), cols(g)] = _rms_norm(y[i], og_ref[:, cols(g)]).astype(ya_ref.dtype)


def _chunk_mlp_prompt(h, ln_g, ln_b, ws, bs, out_g):
    m = h.shape[0]
    n_groups, d_a = ws.shape[0], ws.shape[0] * GROUP
    vec = pl.BlockSpec((1, d_a), lambda i: (0, 0))
    tm = CMLP_CHUNKS * MLP_CHUNK
    return pl.pallas_call(
        functools.partial(_cmlp_prompt_kernel, n_groups=n_groups, n_chunks=CMLP_CHUNKS),
        grid=(m // tm,),
        in_specs=[pl.BlockSpec((tm, d_a), lambda i: (i, 0)),
                  pl.BlockSpec((tm, d_a), lambda i: (i, 1)),
                  vec, vec,
                  pl.BlockSpec((n_groups, MLP_CHUNK, MLP_CHUNK), lambda i: (0, 0, 0)),
                  pl.BlockSpec((MLP_CHUNK, n_groups), lambda i: (0, 0)),
                  vec],
        out_specs=pl.BlockSpec((tm, d_a), lambda i: (i, 0)),
        out_shape=jax.ShapeDtypeStruct((m, d_a), BF16),
        compiler_params=_params(("arbitrary",)),
        name="chunk_mlp_prompt",
    )(h, h, ln_g.reshape(1, d_a), ln_b.reshape(1, d_a), ws, bs.T, out_g.reshape(1, d_a))


def _cmlp_sample_kernel(u_ref, v_ref, lng_ref, lnb_ref, wv_ref, bv_ref, og_ref, ya_ref, vr_ref, *, n_pos):
    vh = []
    for t in range(n_pos):
        vt = _layer_norm(_gelu(v_ref[t]), lng_ref[0], lnb_ref[0])
        vr_ref[:, t, :] = vt
        vh.append(vt)
    for t in range(n_pos):
        mixed = wv_ref[t, 0:1, :] * vh[0]
        for s in range(1, t + 1):
            mixed = mixed + wv_ref[t, s:s + 1, :] * vh[s]
        mixed = mixed + bv_ref[0, t:t + 1, :]
        y = _gelu(u_ref[t]) * mixed
        ya_ref[t] = _rms_norm(y, og_ref[0]).astype(ya_ref.dtype)


def _chunk_mlp_sample(h3, ln_g, ln_b, ws, bs, out_g):
    n_pos, nb, _ = h3.shape
    n_groups = ws.shape[0]
    d_a = n_groups * GROUP
    wv = jnp.repeat(jnp.transpose(ws[:, :n_pos, :n_pos], (1, 2, 0)), GROUP, axis=-1)
    bv = jnp.repeat(bs[:, :n_pos].T, GROUP, axis=-1).reshape(1, n_pos, d_a)
    vec = pl.BlockSpec((1, 1, GROUP), lambda g: (0, 0, g))
    act = pl.BlockSpec((n_pos, nb, GROUP), lambda g: (0, 0, g))
    return pl.pallas_call(
        functools.partial(_cmlp_sample_kernel, n_pos=n_pos),
        grid=(n_groups,),
        in_specs=[act,
                  pl.BlockSpec((n_pos, nb, GROUP), lambda g: (0, 0, n_groups + g)),
                  vec, vec,
                  pl.BlockSpec((n_pos, n_pos, GROUP), lambda g: (0, 0, g)),
                  pl.BlockSpec((1, n_pos, GROUP), lambda g: (0, 0, g)),
                  vec],
        out_specs=[act, pl.BlockSpec((nb, n_pos, GROUP), lambda g: (0, 0, g))],
        out_shape=[jax.ShapeDtypeStruct((n_pos, nb, d_a), BF16),
                   jax.ShapeDtypeStruct((nb, n_pos, d_a), F32)],
        compiler_params=_params(("arbitrary",)),
        name="chunk_mlp_sample",
    )(h3, h3, ln_g.reshape(1, 1, d_a), ln_b.reshape(1, 1, d_a), wv, bv, out_g.reshape(1, 1, d_a))


def _conv_taps(xbuf_ref, w_ref, tm, shift, halo, width, cols=slice(None)):
    acc = None
    for j in range(width):
        off = halo - (width - 1 - j) * shift
        term = w_ref[j:j + 1, cols] * xbuf_ref[off:off + tm, :]
        acc = term if acc is None else acc + term
    return acc


def _qkv_prep_kernel(*refs, tm, shift, halo, n_heads, has_buf):
    if has_buf:
        (xq_ref, xk_ref, xv_ref, bq_ref, bk_ref, bv_ref, wq_ref, wk_ref, wv_ref,
         q_ref, k_ref, v_ref, xbuf_ref, carry_ref) = refs
        bufs = (bq_ref, bk_ref, bv_ref)
    else:
        (xq_ref, xk_ref, xv_ref, wq_ref, wk_ref, wv_ref,
         q_ref, k_ref, v_ref, xbuf_ref, carry_ref) = refs
        bufs = (None, None, None)
    first = pl.program_id(1) == 0
    parts = ((xq_ref, wq_ref, q_ref, GROUP ** -0.5), (xk_ref, wk_ref, k_ref, 1.0), (xv_ref, wv_ref, v_ref, None))
    for p, (x_ref, w_ref, o_ref, scale) in enumerate(parts):
        @pl.when(first)
        def _(p=p):
            if has_buf:
                xbuf_ref[0:halo, :] = bufs[p][...]
            else:
                xbuf_ref[0:halo, :] = jnp.zeros((halo, xbuf_ref.shape[1]), F32)

        @pl.when(jnp.logical_not(first))
        def _(p=p):
            xbuf_ref[0:halo, :] = carry_ref[p]

        xbuf_ref[halo:halo + tm, :] = x_ref[...]
        carry_ref[p] = xbuf_ref[tm:tm + halo, :]
        y = _silu(_conv_taps(xbuf_ref, w_ref, tm, shift, halo, CONV_QKV))
        if scale is None:
            o_ref[...] = y
        else:
            for h in range(n_heads):
                sl = slice(h * GROUP, (h + 1) * GROUP)
                yh = y[:, sl]
                yn = yh * lax.rsqrt(jnp.sum(yh * yh, axis=-1, keepdims=True) + 1e-6)
                o_ref[:, sl] = yn * scale if scale != 1.0 else yn


def _qkv_prep(h, conv_w, bufs, n_seq, rows_per_seq, tm, shift, col0):
    d_b = conv_w.shape[1] // 3
    n_heads = d_b // GROUP
    has_buf = bufs is not None
    halo = (CONV_QKV - 1) * shift if has_buf else SUBLANES
    tiles = rows_per_seq // tm
    cb = col0 // d_b
    x_specs = [pl.BlockSpec((tm, d_b), lambda s, i, c=c: (s * tiles + i, cb + c)) for c in range(3)]
    w_specs = [pl.BlockSpec((CONV_QKV, d_b), lambda s, i, c=c: (0, c)) for c in range(3)]
    o_spec = pl.BlockSpec((tm, d_b), lambda s, i: (s * tiles + i, 0))
    args, specs = [h, h, h], list(x_specs)
    if has_buf:
        specs += [pl.BlockSpec((halo, d_b), lambda s, i, c=c: (s, c)) for c in range(3)]
        args += [bufs, bufs, bufs]
    specs += w_specs
    args += [conv_w, conv_w, conv_w]
    m = n_seq * rows_per_seq
    return pl.pallas_call(
        functools.partial(_qkv_prep_kernel, tm=tm, shift=shift, halo=halo, n_heads=n_heads, has_buf=has_buf),
        grid=(n_seq, tiles),
        in_specs=specs,
        out_specs=[o_spec, o_spec, o_spec],
        out_shape=[jax.ShapeDtypeStruct((m, d_b), F32)] * 3,
        scratch_shapes=[pltpu.VMEM((halo + tm, d_b), F32), pltpu.VMEM((3, halo, d_b), F32)],
        compiler_params=_params(("arbitrary", "arbitrary")),
        name="qkv_prep",
    )(*args)


def _gates(ab, alog, dtb):
    g = -jnp.exp(alog) * jax.nn.softplus(ab + dtb)
    beta = jax.nn.sigmoid(ab)
    return g, beta


def _delta_prompt_kernel(q_ref, k_ref, v_ref, z_ref, ab_ref, alog_ref, dtb_ref, ng_ref,
                         yb_ref, sout_ref, s_ref, *, n_heads, n_chunks, ns):
    c = DN_CHUNK
    ci = pl.program_id(1)

    @pl.when(ci == 0)
    def _():
        s_ref[...] = jnp.zeros(s_ref.shape, F32)

    row = lax.broadcasted_iota(jnp.int32, (c, c), 0)
    col = lax.broadcasted_iota(jnp.int32, (c, c), 1)
    causal = row >= col
    strict = row > col
    eye = (row == col).astype(F32)
    tril = causal.astype(F32)
    n_double = (c - 1).bit_length() - 1
    ch = [(s, h) for s in range(ns) for h in range(n_heads)]
    sl = lambda h: slice(h * GROUP, (h + 1) * GROUP)
    gates = [_gates(ab_ref[s], alog_ref[...], dtb_ref[...]) for s in range(ns)]
    gc_s = [jnp.dot(tril, gates[s][0], precision=lax.Precision.HIGHEST, preferred_element_type=F32)
            for s in range(ns)]
    gct_s = [g.T for g in gc_s]
    q = [q_ref[s, :, sl(h)] for s, h in ch]
    k = [k_ref[s, :, sl(h)] for s, h in ch]
    gcol = [gc_s[s][:, h:h + 1] for s, h in ch]
    beta = [gates[s][1][:, n_heads + h:n_heads + h + 1] for s, h in ch]
    n = range(len(ch))
    decay = [jnp.exp(jnp.where(causal, gcol[i] - gct_s[s][h:h + 1, :], -jnp.inf)) for i, (s, h) in enumerate(ch)]
    kb = [k[i] * beta[i] for i in n]
    kbf = [k[i].astype(BF16) for i in n]
    a = [lax.dot_general(kb[i].astype(BF16), kbf[i], _NT, preferred_element_type=F32) * decay[i] for i in n]
    y = [jnp.where(strict, -a[i], 0.0) for i in n]
    x = [eye + y[i] for i in n]
    for _ in range(n_double):
        y = [_bdot(y[i], y[i]) for i in n]
        x = [x[i] + _bdot(x[i], y[i]) for i in n]
    egc = [jnp.exp(gcol[i]) for i in n]
    sol = [_bdot(x[i], jnp.concatenate([v_ref[s, :, sl(h)] * beta[i], kb[i] * egc[i]], axis=1))
           for i, (s, h) in enumerate(ch)]
    qk = [lax.dot_general(q[i].astype(BF16), kbf[i], _NT, preferred_element_type=F32) * decay[i] for i in n]
    s_old = [s_ref[s, h] for s, h in ch]
    r = [_bdot(jnp.concatenate([sol[i][:, GROUP:], q[i] * egc[i]], axis=0), s_old[i]) for i in n]
    v_new = [sol[i][:, :GROUP] - r[i][:c] for i in n]
    o = [r[i][c:] + _bdot(qk[i], v_new[i]) for i in n]
    gl = [gc_s[s][c - 1:c, h:h + 1] for s, h in ch]
    upd = [lax.dot_general((k[i] * jnp.exp(gl[i] - gcol[i])).astype(BF16), v_new[i].astype(BF16), _TN,
                           preferred_element_type=F32) for i in n]
    for i, (s, h) in enumerate(ch):
        s_ref[s, h] = s_old[i] * jnp.exp(gl[i]) + upd[i]
        yb_ref[s, :, sl(h)] = (_rms_norm(o[i], ng_ref[...]) * _silu(z_ref[s, :, sl(h)])).astype(yb_ref.dtype)

    @pl.when(ci == n_chunks - 1)
    def _():
        sout_ref[...] = s_ref[...]


def _delta_prompt(q, k, v, h, z_col0, ab, alog, dtb, norm_g, n_seq, rows_per_seq, ns):
    d_b = q.shape[1]
    n_heads = d_b // GROUP
    n_chunks = rows_per_seq // DN_CHUNK
    r3 = lambda t: t.reshape(n_seq, rows_per_seq, t.shape[-1])
    act = pl.BlockSpec((ns, DN_CHUNK, d_b), lambda s, i: (s, i, 0))
    vec = pl.BlockSpec((1, LANES), lambda s, i: (0, 0))
    zb = z_col0 // d_b
    yb, s_new = pl.pallas_call(
        functools.partial(_delta_prompt_kernel, n_heads=n_heads, n_chunks=n_chunks, ns=ns),
        grid=(n_seq // ns, n_chunks),
        in_specs=[act, act, act,
                  pl.BlockSpec((ns, DN_CHUNK, d_b), lambda s, i: (s, i, zb)),
                  pl.BlockSpec((ns, DN_CHUNK, LANES), lambda s, i: (s, i, 0)),
                  vec, vec, vec],
        out_specs=[act, pl.BlockSpec((ns, n_heads, GROUP, GROUP), lambda s, i: (s, 0, 0, 0))],
        out_shape=[jax.ShapeDtypeStruct((n_seq, rows_per_seq, d_b), BF16),
                   jax.ShapeDtypeStruct((n_seq, n_heads, GROUP, GROUP), F32)],
        scratch_shapes=[pltpu.VMEM((ns, n_heads, GROUP, GROUP), F32)],
        compiler_params=_params(("arbitrary", "arbitrary")),
        name="delta_prompt",
    )(r3(q), r3(k), r3(v), r3(h), r3(ab), alog, dtb, norm_g)
    return yb.reshape(n_seq * rows_per_seq, d_b), s_new


def _delta_sample_kernel(q_ref, k_ref, v_ref, z_ref, ab_ref, alog_ref, dtb_ref, ng_ref, s0_ref,
                         yb_ref, s1_ref, lhs_ref, res_ref, kt_ref, vn_ref, gl_ref, *, n_heads, n_pos, bb):
    nt = n_pos
    hs = range(n_heads)
    sls = [slice(h * GROUP, (h + 1) * GROUP) for h in hs]
    gates = [_gates(ab_ref[t], alog_ref[...], dtb_ref[...]) for t in range(nt)]
    zeros_pad = jnp.zeros((bb, GROUP), F32)
    u_all, qk_all, gc_all = [], [], []
    for h in hs:
        q = [q_ref[t, :, sls[h]] for t in range(nt)]
        k = [k_ref[t, :, sls[h]] for t in range(nt)]
        beta = [gates[t][1][:, n_heads + h:n_heads + h + 1] for t in range(nt)]
        gc = []
        for t in range(nt):
            gt = gates[t][0][:, h:h + 1]
            gc.append(gt if t == 0 else gc[t - 1] + gt)
        kb = [k[t] * beta[t] for t in range(nt)]
        dec = [[jnp.exp(gc[t] - gc[s]) for s in range(t + 1)] for t in range(nt)]
        a = [[jnp.sum(kb[t] * k[s], axis=-1, keepdims=True) * dec[t][s] for s in range(t)] for t in range(nt)]
        egc = [jnp.exp(gc[t]) for t in range(nt)]
        u_blk, w_blk = [], []
        for t in range(nt):
            ut, wt = v_ref[t, :, sls[h]] * beta[t], kb[t] * egc[t]
            for s in range(t):
                ut = ut - a[t][s] * u_blk[s]
                wt = wt - a[t][s] * w_blk[s]
            u_blk.append(ut)
            w_blk.append(wt)
        qk_all.append([[jnp.sum(q[t] * k[s], axis=-1, keepdims=True) * dec[t][s] for s in range(t + 1)]
                       for t in range(nt)])
        u_all.append(u_blk)
        gc_all.append(gc)
        for t in range(nt):
            lhs_ref[h, :, t, :] = w_blk[t]
            lhs_ref[h, :, nt + t, :] = q[t] * egc[t]
        gl_ref[h] = jnp.broadcast_to(jnp.exp(gc[nt - 1]), (bb, GROUP))

    for h in hs:
        for b in range(bb):
            res_ref[h, b] = _bdot(lhs_ref[h, b], s0_ref[b, h])

    for h in hs:
        gc, gl = gc_all[h], gc_all[h][nt - 1]
        v_new = [u_all[h][t] - res_ref[h, :, t, :] for t in range(nt)]
        for t in range(nt):
            o = res_ref[h, :, nt + t, :]
            for s in range(t + 1):
                o = o + qk_all[h][t][s] * v_new[s]
            yb_ref[t, :, sls[h]] = (_rms_norm(o, ng_ref[...]) * _silu(z_ref[t, :, sls[h]])).astype(yb_ref.dtype)
            kt_ref[h, :, t, :] = k_ref[t, :, sls[h]] * jnp.exp(gl - gc[t])
            vn_ref[h, :, t, :] = v_new[t]
            kt_ref[h, :, nt + t, :] = zeros_pad
            vn_ref[h, :, nt + t, :] = zeros_pad

    for h in hs:
        for b in range(bb):
            upd = lax.dot_general(kt_ref[h, b].astype(BF16), vn_ref[h, b].astype(BF16), _TN,
                                  preferred_element_type=F32)
            s1_ref[b, h] = s0_ref[b, h] * gl_ref[h, b:b + 1, :] + upd


def _delta_sample(q, k, v, h3, z_col0, ab, alog, dtb, norm_g, s0, bb):
    n_pos, nb, d_b = q.shape
    n_heads = d_b // GROUP
    assert 2 * n_pos == SUBLANES, "one state read packs [w | q] rows of a sequence into one 8-row tile"
    zb = z_col0 // d_b
    act = pl.BlockSpec((n_pos, bb, d_b), lambda i: (0, i, 0))
    vec = pl.BlockSpec((1, LANES), lambda i: (0, 0))
    st = pl.BlockSpec((bb, n_heads, GROUP, GROUP), lambda i: (i, 0, 0, 0))
    tile = pltpu.VMEM((n_heads, bb, SUBLANES, GROUP), F32)
    return pl.pallas_call(
        functools.partial(_delta_sample_kernel, n_heads=n_heads, n_pos=n_pos, bb=bb),
        grid=(nb // bb,),
        in_specs=[act, act, act,
                  pl.BlockSpec((n_pos, bb, d_b), lambda i: (0, i, zb)),
                  pl.BlockSpec((n_pos, bb, LANES), lambda i: (0, i, 0)),
                  vec, vec, vec, st],
        out_specs=[act, st],
        out_shape=[jax.ShapeDtypeStruct((n_pos, nb, d_b), BF16),
                   jax.ShapeDtypeStruct(s0.shape, F32)],
        scratch_shapes=[tile, tile, tile, tile, pltpu.VMEM((n_heads, bb, GROUP), F32)],
        compiler_params=_params(("arbitrary",)),
        name="delta_sample",
    )(q, k, v, h3, ab, alog, dtb, norm_g, s0)


def _out_proj_kernel(ya_ref, yb_ref, w_ref, r_ref, g_ref, b_ref, o_ref, ob_ref, z_ref, *, n_tiles, d_a, alpha):
    i = pl.program_id(0)

    @pl.when(i == 0)
    def _():
        z_ref[...] = jnp.zeros(z_ref.shape, F32)

    def emit_ln():
        y = _layer_norm(z_ref[...], g_ref[...], b_ref[...])
        o_ref[...] = y
        ob_ref[...] = y.astype(ob_ref.dtype)

    @pl.when(i < n_tiles)
    def _():
        emit_ln()
        acc = (jnp.dot(ya_ref[...], w_ref[0:d_a, :], preferred_element_type=F32)
               + jnp.dot(yb_ref[...], w_ref[d_a:, :], preferred_element_type=F32))
        z_ref[...] = alpha * r_ref[...] + acc

    @pl.when(i == n_tiles)
    def _():
        emit_ln()


def _out_proj_ln(ya, yb, w, resid, g, b, alpha, tm):
    m, d_a = ya.shape
    k, d = w.shape
    n_tiles = m // tm
    cur = lambda i: (jnp.minimum(i, n_tiles - 1), 0)
    prev = lambda i: (jnp.maximum(i - 1, 0), 0)
    vec = pl.BlockSpec((1, d), lambda i: (0, 0))
    return pl.pallas_call(
        functools.partial(_out_proj_kernel, n_tiles=n_tiles, d_a=d_a, alpha=alpha),
        grid=(n_tiles + 1,),
        in_specs=[pl.BlockSpec((tm, d_a), cur),
                  pl.BlockSpec((tm, k - d_a), cur),
                  pl.BlockSpec((k, d), lambda i: (0, 0), pipeline_mode=pl.Buffered(1)),
                  pl.BlockSpec((tm, d), cur), vec, vec],
        out_specs=[pl.BlockSpec((tm, d), prev), pl.BlockSpec((tm, d), prev)],
        out_shape=[jax.ShapeDtypeStruct((m, d), F32), jax.ShapeDtypeStruct((m, d), BF16)],
        scratch_shapes=[pltpu.VMEM((tm, d), F32)],
        compiler_params=_params(("arbitrary",)),
        name="out_proj_ln1",
    )(ya, yb, w, resid, g.reshape(1, d), b.reshape(1, d))


def _down_kernel(x_ref, w_ref, r_ref, g_ref, b_ref, o_ref, *rest, n_tiles, nk, alpha, n_pos):
    acc_ref, z_ref = rest[-2:]
    i, kk = pl.program_id(0), pl.program_id(1)

    def emit_ln():
        y = _layer_norm(z_ref[...], g_ref[...], b_ref[...])
        if n_pos:
            nb = y.shape[0] // n_pos
            for t in range(n_pos):
                o_ref[:, t, :] = y[t * nb:(t + 1) * nb]
        else:
            o_ref[...] = y

    @pl.when(jnp.logical_and(i == 0, kk == 0))
    def _():
        z_ref[...] = jnp.zeros(z_ref.shape, F32)

    wb = w_ref[...].astype(BF16)
    if len(rest) == 3:
        rest[0][...] = wb

    @pl.when(kk == 0)
    def _():
        acc_ref[...] = alpha * r_ref[...] + jnp.dot(x_ref[...], wb, preferred_element_type=F32)
        emit_ln()

    @pl.when(jnp.logical_and(jnp.logical_and(kk > 0, kk < nk - 1), i < n_tiles))
    def _():
        acc_ref[...] += jnp.dot(x_ref[...], wb, preferred_element_type=F32)

    @pl.when(jnp.logical_and(kk == nk - 1, i < n_tiles))
    def _():
        z_ref[...] = acc_ref[...] + jnp.dot(x_ref[...], wb, preferred_element_type=F32)


def _down_ln(x, w, resid, g, b, alpha, tm, tk, emit_w=False, n_pos=0):
    m, kdim = x.shape
    d = w.shape[1]
    nk = kdim // tk
    assert nk >= 2
    n_tiles = m // tm
    row = lambda i: jnp.minimum(i, n_tiles - 1)
    kt = lambda i, kk: jnp.where(i == n_tiles, nk - 1, kk)
    vec = pl.BlockSpec((1, d), lambda i, kk: (0, 0))
    wspec = pl.BlockSpec((tk, d), lambda i, kk: (kt(i, kk), 0))
    if n_pos:
        assert n_tiles == 1
        out_specs = [pl.BlockSpec((m // n_pos, n_pos, d), lambda i, kk: (0, 0, 0))]
        out_shape = [jax.ShapeDtypeStruct((m // n_pos, n_pos, d), F32)]
    else:
        out_specs = [pl.BlockSpec((tm, d), lambda i, kk: (jnp.where(kk == 0, jnp.maximum(i - 1, 0), row(i)), 0))]
        out_shape = [jax.ShapeDtypeStruct((m, d), F32)]
    if emit_w:
        out_specs.append(wspec)
        out_shape.append(jax.ShapeDtypeStruct(w.shape, BF16))
    outs = pl.pallas_call(
        functools.partial(_down_kernel, n_tiles=n_tiles, nk=nk, alpha=alpha, n_pos=n_pos),
        grid=(n_tiles + 1, nk),
        in_specs=[pl.BlockSpec((tm, tk), lambda i, kk: (row(i), kt(i, kk))), wspec,
                  pl.BlockSpec((tm, d), lambda i, kk: (row(i), 0)), vec, vec],
        out_specs=out_specs,
        out_shape=out_shape,
        scratch_shapes=[pltpu.VMEM((tm, d), F32), pltpu.VMEM((tm, d), F32)],
        compiler_params=_params(("arbitrary", "arbitrary")),
        name="down_ln2",
    )(x, w, resid, g.reshape(1, d), b.reshape(1, d))
    return tuple(outs) if emit_w else outs[0]


def _ffn_up_kernel(*refs, tm, shift, halo, has_buf, n_sub):
    n_in = 9 if has_buf else 7
    if has_buf:
        x_ref, wg_ref, wv_ref, bg_ref, bv_ref, cwg_ref, cwv_ref, cbg_ref, cbv_ref = refs[:n_in]
        bufs = (bg_ref, bv_ref)
    else:
        x_ref, wg_ref, wv_ref, cwg_ref, cwv_ref, cbg_ref, cbv_ref = refs[:n_in]
        bufs = (None, None)
    act_ref, tg_ref, tv_ref = refs[n_in:n_in + 3]
    wgb_ref, wvb_ref = refs[n_in + 3:n_in + 5]
    xbufs, carry_ref = refs[n_in + 5:-1], refs[-1]

    @pl.when(jnp.logical_and(pl.program_id(1) == 0, pl.program_id(2) == 0))
    def _():
        wgb_ref[...] = wg_ref[...].astype(BF16)
        wvb_ref[...] = wv_ref[...].astype(BF16)

    wsub = act_ref.shape[1] // n_sub
    cols = [slice(c * wsub, (c + 1) * wsub) for c in range(n_sub)]
    first = pl.program_id(2) == 0

    @pl.when(first)
    def _():
        for c in range(n_sub):
            for p in range(2):
                if has_buf:
                    xbufs[2 * c + p][0:halo, :] = bufs[p][:, cols[c]]
                else:
                    xbufs[2 * c + p][0:halo, :] = jnp.zeros((halo, wsub), F32)

    @pl.when(jnp.logical_not(first))
    def _():
        for c in range(n_sub):
            for p in range(2):
                xbufs[2 * c + p][0:halo, :] = carry_ref[p, :, cols[c]]

    x = x_ref[...]
    parts = ((wgb_ref, cwg_ref, cbg_ref, tg_ref), (wvb_ref, cwv_ref, cbv_ref, tv_ref))
    for c in range(n_sub):
        conv = []
        for p, (w_ref, cw_ref, cb_ref, t_ref) in enumerate(parts):
            xbuf_ref = xbufs[2 * c + p]
            xbuf_ref[halo:halo + tm, :] = jnp.dot(x, w_ref[:, cols[c]], preferred_element_type=F32)
            tail = xbuf_ref[tm:tm + halo, :]
            carry_ref[p, :, cols[c]] = tail
            if has_buf:
                for t in range(halo // shift):
                    t_ref[:, t, cols[c]] = tail[t * shift:(t + 1) * shift]
            else:
                t_ref[0, :, cols[c]] = tail
            conv.append(_conv_taps(xbuf_ref, cw_ref, tm, shift, halo, CONV_FFN, cols[c]) + cb_ref[:, cols[c]])
        act_ref[:, cols[c]] = (_silu(conv[0]) * conv[1]).astype(act_ref.dtype)


def _ffn_up(x, w_up, conv_w, conv_b, bufs, n_seq, rows_per_seq, tm, tn, shift):
    m, d = x.shape
    d_ff = w_up.shape[1] // 2
    nj = d_ff // tn
    has_buf = bufs is not None
    halo = (CONV_FFN - 1) * shift if has_buf else SUBLANES
    tiles = rows_per_seq // tm
    wspec = lambda off: pl.BlockSpec((d, tn), lambda j, s, i: (0, off + j))
    cwspec = lambda off: pl.BlockSpec((CONV_FFN, tn), lambda j, s, i: (0, off + j))
    cbspec = lambda off: pl.BlockSpec((1, tn), lambda j, s, i: (0, off + j))
    if has_buf:
        assert n_seq == 1
        tspec = pl.BlockSpec((shift, halo // shift, tn), lambda j, s, i: (0, 0, j))
        tshape = jax.ShapeDtypeStruct((shift, halo // shift, d_ff), F32)
    else:
        tspec = pl.BlockSpec((1, halo, tn), lambda j, s, i: (s, 0, j))
        tshape = jax.ShapeDtypeStruct((n_seq, halo, d_ff), F32)
    specs = [pl.BlockSpec((tm, d), lambda j, s, i: (s * tiles + i, 0)), wspec(0), wspec(nj)]
    args = [x, w_up, w_up]
    if has_buf:
        specs += [pl.BlockSpec((halo, tn), lambda j, s, i, off=off: (s, off + j)) for off in (0, nj)]
        args += [bufs, bufs]
    specs += [cwspec(0), cwspec(nj), cbspec(0), cbspec(nj)]
    cb2 = conv_b.reshape(1, 2 * d_ff)
    args += [conv_w, conv_w, cb2, cb2]
    return pl.pallas_call(
        functools.partial(_ffn_up_kernel, tm=tm, shift=shift, halo=halo, has_buf=has_buf, n_sub=FFN_SUB),
        grid=(nj, n_seq, tiles),
        in_specs=specs,
        out_specs=[pl.BlockSpec((tm, tn), lambda j, s, i: (s * tiles + i, j)), tspec, tspec],
        out_shape=[jax.ShapeDtypeStruct((m, d_ff), BF16), tshape, tshape],
        scratch_shapes=[pltpu.VMEM((d, tn), BF16)] * 2
                       + [pltpu.VMEM((halo + tm, tn // FFN_SUB), F32)] * (2 * FFN_SUB)
                       + [pltpu.VMEM((2, halo, tn), F32)],
        compiler_params=_params(("arbitrary", "arbitrary", "arbitrary")),
        name="ffn_up",
    )(*args)


def _pad_lanes(vec, offset=0):
    out = jnp.zeros((1, LANES), F32)
    return out.at[0, offset:offset + vec.shape[0]].set(vec.astype(F32))


def _layer(x2, lw, alpha, n_seq, rows_per_seq, shift, state, wb, last):
    (w_in, conv_qkv_w, a_log, dt_bias, norm_g, gm_ln_g, gm_ln_b, gm_ws, gm_bs, gm_out_g, w_out,
     ln1_g, ln1_b, w_up, conv_ffn_w, conv_ffn_b, w_down, ln2_g, ln2_b) = lw
    m, d_model = x2.shape
    n_groups = gm_ws.shape[0]
    d_a = n_groups * GROUP
    d_qkv = conv_qkv_w.shape[1]
    d_b = d_qkv // 3
    n_heads = d_b // GROUP
    main_cols = 2 * d_a + d_qkv + d_b
    fresh = state is None
    n_pos = rows_per_seq // shift
    tm = min(m, 1024)

    w_ab = jnp.pad(w_in[main_cols:].astype(BF16), ((0, LANES - 2 * n_heads), (0, 0)))
    if wb is None:
        h, ab, w_in_b = _matmul(x2, w_in, w_ab, main_cols, tm, 1024, emit_w=True)
    else:
        w_in_b = wb[0]
        h, ab = _matmul(x2, w_in_b, w_ab, main_cols, tm, 1024)
    alog, dtb, ng = _pad_lanes(a_log), _pad_lanes(dt_bias), norm_g.reshape(1, GROUP).astype(F32)

    if fresh:
        ya = _chunk_mlp_prompt(h, gm_ln_g, gm_ln_b, gm_ws, gm_bs, gm_out_g)
        v_rows = None
        q, k, v = _qkv_prep(h, conv_qkv_w, None, n_seq, rows_per_seq, 512, shift, 2 * d_a)
        yb, s_new = _delta_prompt(q, k, v, h, 2 * d_a + d_qkv, ab, alog, dtb, ng, n_seq, rows_per_seq, 4)
        ffn_bufs = None
        tm_ffn = 1024
    else:
        s_dn, buf_qkv, buf_ffn = state
        h3 = h.reshape(n_pos, shift, main_cols)
        ya3, v_rows = _chunk_mlp_sample(h3, gm_ln_g, gm_ln_b, gm_ws, gm_bs, gm_out_g)
        ya = ya3.reshape(m, d_a)
        q, k, v = _qkv_prep(h, conv_qkv_w, buf_qkv, n_seq, rows_per_seq, rows_per_seq, shift, 2 * d_a)
        r3 = lambda t: t.reshape(n_pos, shift, t.shape[-1])
        yb3, s_new = _delta_sample(r3(q), r3(k), r3(v), h3, 2 * d_a + d_qkv, r3(ab), alog, dtb, ng, s_dn, 8)
        yb = yb3.reshape(m, d_b)
        ffn_bufs = buf_ffn
        tm_ffn = rows_per_seq

    x1, x1b = _out_proj_ln(ya, yb, w_out, x2, ln1_g, ln1_b, alpha, min(m, 512))
    act, tail_g, tail_v = _ffn_up(x1b, w_up, conv_ffn_w, conv_ffn_b, ffn_bufs, n_seq, rows_per_seq, tm_ffn, 512, shift)
    if wb is None:
        y, w_down_b = _down_ln(act, w_down, x1, ln2_g, ln2_b, alpha, min(m, 512), 512, emit_w=True,
                               n_pos=n_pos if (last and not fresh) else 0)
    else:
        w_down_b = wb[1]
        y = _down_ln(act, w_down_b, x1, ln2_g, ln2_b, alpha, min(m, 512), 1408)
    return y, s_new, h, tail_g, tail_v, v_rows, (w_in_b, w_down_b)


def kernel(x_prompt, x_sample, state_dn, state_conv_qkv, state_conv_ffn, w_in, conv_qkv_w, dn_a_log, dn_dt_bias, dn_norm_g, gm_ln_g, gm_ln_b, gm_ws, gm_bs, gm_out_g, w_out, ln1_g, ln1_b, w_up, conv_ffn_w, conv_ffn_b, w_down, ln2_g, ln2_b):
    depth = w_in.shape[0]
    bp, lp, d_model = x_prompt.shape
    bs, ls, _ = x_sample.shape
    d_qkv = conv_qkv_w.shape[-1]
    d_a = gm_ws.shape[1] * GROUP
    alpha = (2.0 * depth) ** 0.25

    hp = x_prompt.reshape(bp * lp, d_model)
    hs = jnp.transpose(x_sample, (1, 0, 2)).reshape(ls * bs, d_model)
    outs = [[] for _ in range(7)]
    w_out = w_out.astype(BF16)
    w_in = jnp.swapaxes(w_in, 1, 2)
    for l in range(depth):
        lw = tuple(t[l] for t in (w_in, conv_qkv_w, dn_a_log, dn_dt_bias, dn_norm_g, gm_ln_g, gm_ln_b, gm_ws,
                                  gm_bs, gm_out_g, w_out, ln1_g, ln1_b, w_up, conv_ffn_w, conv_ffn_b, w_down,
                                  ln2_g, ln2_b))
        last = l == depth - 1
        bq = jnp.transpose(state_conv_qkv[l], (1, 0, 2)).reshape((CONV_QKV - 1) * bs, d_qkv)
        bf = jnp.transpose(state_conv_ffn[l], (1, 0, 2)).reshape((CONV_FFN - 1) * bs, -1)
        hs, s_s, h_s, tg_s, tv_s, v_s, wb = _layer(hs, lw, alpha, 1, ls * bs, bs, (state_dn[l], bq, bf), None, last)
        h_s3 = h_s.reshape(ls, bs, -1)
        cq_s = jnp.transpose(h_s3[ls - (CONV_QKV - 1):, :, 2 * d_a:2 * d_a + d_qkv], (1, 0, 2))
        cf_s = jnp.concatenate([tg_s, tv_s], axis=-1)
        hp, s_p, h_p, tg_p, tv_p, _, _ = _layer(hp, lw, alpha, bp, lp, 1, None, wb, last)
        h_p3 = h_p.reshape(bp, lp, -1)
        cq_p = h_p3[:, lp - (CONV_QKV - 1):, 2 * d_a:2 * d_a + d_qkv]
        n_tail = tg_p.shape[1]
        cf_p = jnp.concatenate([tg_p, tv_p], axis=-1)[:, n_tail - (CONV_FFN - 1):]
        for acc, val in zip(outs, (s_p, cq_p, cf_p, s_s, cq_s, cf_s, v_s)):
            acc.append(val)
    return (hp.reshape(bp, lp, d_model), hs) + tuple(jnp.stack(o) for o in outs)
```

```python
import functools

import jax
import jax.numpy as jnp
from jax import lax
from jax.experimental import pallas as pl
from jax.experimental.pallas import tpu as pltpu

F32 = jnp.float32
BF16 = jnp.bfloat16

LANES = 128
SUBLANES = 8
VMEM_LIMIT = 56 * 1024 * 1024

GROUP = 128
MLP_CHUNK = 128
DN_CHUNK = 64
CONV_QKV = 4
CONV_FFN = 3
FFN_SUB = 1
CMLP_CHUNKS = 2
EPS = 1e-5

_NT = (((1,), (1,)), ((), ()))
_TN = (((0,), (0,)), ((), ()))


def _params(sem):
    return pltpu.CompilerParams(dimension_semantics=sem, vmem_limit_bytes=VMEM_LIMIT)


def _bdot(a, b):
    return jnp.dot(a.astype(BF16), b.astype(BF16), preferred_element_type=F32)


def _gelu(x):
    return 0.5 * x * (1.0 + lax.erf(x * (0.5 ** 0.5)))


def _silu(x):
    return x * jax.nn.sigmoid(x)


def _layer_norm(x, g, b):
    mu = jnp.mean(x, axis=-1, keepdims=True)
    xc = x - mu
    var = jnp.mean(xc * xc, axis=-1, keepdims=True)
    return xc * lax.rsqrt(var + EPS) * g + b


def _rms_norm(x, g):
    ms = jnp.mean(x * x, axis=-1, keepdims=True)
    return x * lax.rsqrt(ms + EPS) * g


def _mm_kernel(x_ref, w_ref, we_ref, o_ref, e_ref, *rest):
    xb_ref = rest[-1]

    @pl.when(pl.program_id(1) == 0)
    def _():
        xb = x_ref[...].astype(BF16)
        xb_ref[...] = xb
        e_ref[...] = lax.dot_general(xb, we_ref[...], _NT, preferred_element_type=F32)

    wb = w_ref[...].astype(BF16)
    if len(rest) == 2:
        rest[0][...] = wb
    o_ref[...] = lax.dot_general(xb_ref[...], wb, _NT, preferred_element_type=F32)


def _matmul(x, wt, wt_extra, n_cols, tm, tn, emit_w=False):
    m, k = x.shape
    ne = wt_extra.shape[0]
    out_specs = [pl.BlockSpec((tm, tn), lambda i, j: (i, j)), pl.BlockSpec((tm, ne), lambda i, j: (i, 0))]
    out_shape = [jax.ShapeDtypeStruct((m, n_cols), F32), jax.ShapeDtypeStruct((m, ne), F32)]
    if emit_w:
        out_specs.append(pl.BlockSpec((tn, k), lambda i, j: (j, 0)))
        out_shape.append(jax.ShapeDtypeStruct((n_cols, k), BF16))
    return pl.pallas_call(
        _mm_kernel,
        grid=(m // tm, n_cols // tn),
        in_specs=[pl.BlockSpec((tm, k), lambda i, j: (i, 0)),
                  pl.BlockSpec((tn, k), lambda i, j: (j, 0)),
                  pl.BlockSpec((ne, k), lambda i, j: (0, 0))],
        out_specs=out_specs,
        out_shape=out_shape,
        scratch_shapes=[pltpu.VMEM((tm, k), BF16)],
        compiler_params=_params(("arbitrary", "arbitrary")),
        name="proj",
    )(x, wt, wt_extra)


def _cmlp_prompt_kernel(u_ref, v_ref, lng_ref, lnb_ref, ws_ref, bst_ref, og_ref, ya_ref, *, n_groups, n_chunks):
    row = lax.broadcasted_iota(jnp.int32, (MLP_CHUNK, MLP_CHUNK), 0)
    col = lax.broadcasted_iota(jnp.int32, (MLP_CHUNK, MLP_CHUNK), 1)
    cg = [(c, g) for c in range(n_chunks) for g in range(n_groups)]
    rows = lambda c: slice(c * MLP_CHUNK, (c + 1) * MLP_CHUNK)
    cols = lambda g: slice(g * GROUP, (g + 1) * GROUP)
    w = [jnp.where(row >= col, ws_ref[g], 0.0).astype(BF16) for g in range(n_groups)]
    vh = [_layer_norm(_gelu(v_ref[rows(c), cols(g)]), lng_ref[:, cols(g)], lnb_ref[:, cols(g)]) for c, g in cg]
    mixed = [jnp.dot(w[g], vh[i].astype(BF16), preferred_element_type=F32) + bst_ref[:, g:g + 1]
             for i, (c, g) in enumerate(cg)]
    y = [_gelu(u_ref[rows(c), cols(g)]) * mixed[i] for i, (c, g) in enumerate(cg)]
    for i, (c, g) in enumerate(cg):
        ya_ref[rows(c), cols(g)] = _rms_norm(y[i], og_ref[:, cols(g)]).astype(ya_ref.dtype)


def _chunk_mlp_prompt(h, ln_g, ln_b, ws, bs, out_g):
    m = h.shape[0]
    n_groups, d_a = ws.shape[0], ws.shape[0] * GROUP
    vec = pl.BlockSpec((1, d_a), lambda i: (0, 0))
    tm = CMLP_CHUNKS * MLP_CHUNK
    return pl.pallas_call(
        functools.partial(_cmlp_prompt_kernel, n_groups=n_groups, n_chunks=CMLP_CHUNKS),
        grid=(m // tm,),
        in_specs=[pl.BlockSpec((tm, d_a), lambda i: (i, 0)),
                  pl.BlockSpec((tm, d_a), lambda i: (i, 1)),
                  vec, vec,
                  pl.BlockSpec((n_groups, MLP_CHUNK, MLP_CHUNK), lambda i: (0, 0, 0)),
                  pl.BlockSpec((MLP_CHUNK, n_groups), lambda i: (0, 0)),
                  vec],
        out_specs=pl.BlockSpec((tm, d_a), lambda i: (i, 0)),
        out_shape=jax.ShapeDtypeStruct((m, d_a), BF16),
        compiler_params=_params(("arbitrary",)),
        name="chunk_mlp_prompt",
    )(h, h, ln_g.reshape(1, d_a), ln_b.reshape(1, d_a), ws, bs.T, out_g.reshape(1, d_a))


def _cmlp_sample_kernel(u_ref, v_ref, lng_ref, lnb_ref, wv_ref, bv_ref, og_ref, ya_ref, vr_ref, *, n_pos):
    vh = []
    for t in range(n_pos):
        vt = _layer_norm(_gelu(v_ref[t]), lng_ref[0], lnb_ref[0])
        vr_ref[:, t, :] = vt
        vh.append(vt)
    for t in range(n_pos):
        mixed = wv_ref[t, 0:1, :] * vh[0]
        for s in range(1, t + 1):
            mixed = mixed + wv_ref[t, s:s + 1, :] * vh[s]
        mixed = mixed + bv_ref[0, t:t + 1, :]
        y = _gelu(u_ref[t]) * mixed
        ya_ref[t] = _rms_norm(y, og_ref[0]).astype(ya_ref.dtype)


def _chunk_mlp_sample(h3, ln_g, ln_b, ws, bs, out_g):
    n_pos, nb, _ = h3.shape
    n_groups = ws.shape[0]
    d_a = n_groups * GROUP
    wv = jnp.repeat(jnp.transpose(ws[:, :n_pos, :n_pos], (1, 2, 0)), GROUP, axis=-1)
    bv = jnp.repeat(bs[:, :n_pos].T, GROUP, axis=-1).reshape(1, n_pos, d_a)
    vec = pl.BlockSpec((1, 1, GROUP), lambda g: (0, 0, g))
    act = pl.BlockSpec((n_pos, nb, GROUP), lambda g: (0, 0, g))
    return pl.pallas_call(
        functools.partial(_cmlp_sample_kernel, n_pos=n_pos),
        grid=(n_groups,),
        in_specs=[act,
                  pl.BlockSpec((n_pos, nb, GROUP), lambda g: (0, 0, n_groups + g)),
                  vec, vec,
                  pl.BlockSpec((n_pos, n_pos, GROUP), lambda g: (0, 0, g)),
                  pl.BlockSpec((1, n_pos, GROUP), lambda g: (0, 0, g)),
                  vec],
        out_specs=[act, pl.BlockSpec((nb, n_pos, GROUP), lambda g: (0, 0, g))],
        out_shape=[jax.ShapeDtypeStruct((n_pos, nb, d_a), BF16),
                   jax.ShapeDtypeStruct((nb, n_pos, d_a), F32)],
        compiler_params=_params(("arbitrary",)),
        name="chunk_mlp_sample",
    )(h3, h3, ln_g.reshape(1, 1, d_a), ln_b.reshape(1, 1, d_a), wv, bv, out_g.reshape(1, 1, d_a))


def _conv_taps(xbuf_ref, w_ref, tm, shift, halo, width, cols=slice(None)):
    acc = None
    for j in range(width):
        off = halo - (width - 1 - j) * shift
        term = w_ref[j:j + 1, cols] * xbuf_ref[off:off + tm, :]
        acc = term if acc is None else acc + term
    return acc


def _qkv_prep_kernel(*refs, tm, shift, halo, n_heads, has_buf):
    if has_buf:
        (xq_ref, xk_ref, xv_ref, bq_ref, bk_ref, bv_ref, wq_ref, wk_ref, wv_ref,
         q_ref, k_ref, v_ref, xbuf_ref, carry_ref) = refs
        bufs = (bq_ref, bk_ref, bv_ref)
    else:
        (xq_ref, xk_ref, xv_ref, wq_ref, wk_ref, wv_ref,
         q_ref, k_ref, v_ref, xbuf_ref, carry_ref) = refs
        bufs = (None, None, None)
    first = pl.program_id(1) == 0
    parts = ((xq_ref, wq_ref, q_ref, GROUP ** -0.5), (xk_ref, wk_ref, k_ref, 1.0), (xv_ref, wv_ref, v_ref, None))
    for p, (x_ref, w_ref, o_ref, scale) in enumerate(parts):
        @pl.when(first)
        def _(p=p):
            if has_buf:
                xbuf_ref[0:halo, :] = bufs[p][...]
            else:
                xbuf_ref[0:halo, :] = jnp.zeros((halo, xbuf_ref.shape[1]), F32)

        @pl.when(jnp.logical_not(first))
        def _(p=p):
            xbuf_ref[0:halo, :] = carry_ref[p]

        xbuf_ref[halo:halo + tm, :] = x_ref[...]
        carry_ref[p] = xbuf_ref[tm:tm + halo, :]
        y = _silu(_conv_taps(xbuf_ref, w_ref, tm, shift, halo, CONV_QKV))
        if scale is None:
            o_ref[...] = y
        else:
            for h in range(n_heads):
                sl = slice(h * GROUP, (h + 1) * GROUP)
                yh = y[:, sl]
                yn = yh * lax.rsqrt(jnp.sum(yh * yh, axis=-1, keepdims=True) + 1e-6)
                o_ref[:, sl] = yn * scale if scale != 1.0 else yn


def _qkv_prep(h, conv_w, bufs, n_seq, rows_per_seq, tm, shift, col0):
    d_b = conv_w.shape[1] // 3
    n_heads = d_b // GROUP
    has_buf = bufs is not None
    halo = (CONV_QKV - 1) * shift if has_buf else SUBLANES
    tiles = rows_per_seq // tm
    cb = col0 // d_b
    x_specs = [pl.BlockSpec((tm, d_b), lambda s, i, c=c: (s * tiles + i, cb + c)) for c in range(3)]
    w_specs = [pl.BlockSpec((CONV_QKV, d_b), lambda s, i, c=c: (0, c)) for c in range(3)]
    o_spec = pl.BlockSpec((tm, d_b), lambda s, i: (s * tiles + i, 0))
    args, specs = [h, h, h], list(x_specs)
    if has_buf:
        specs += [pl.BlockSpec((halo, d_b), lambda s, i, c=c: (s, c)) for c in range(3)]
        args += [bufs, bufs, bufs]
    specs += w_specs
    args += [conv_w, conv_w, conv_w]
    m = n_seq * rows_per_seq
    return pl.pallas_call(
        functools.partial(_qkv_prep_kernel, tm=tm, shift=shift, halo=halo, n_heads=n_heads, has_buf=has_buf),
        grid=(n_seq, tiles),
        in_specs=specs,
        out_specs=[o_spec, o_spec, o_spec],
        out_shape=[jax.ShapeDtypeStruct((m, d_b), F32)] * 3,
        scratch_shapes=[pltpu.VMEM((halo + tm, d_b), F32), pltpu.VMEM((3, halo, d_b), F32)],
        compiler_params=_params(("arbitrary", "arbitrary")),
        name="qkv_prep",
    )(*args)


def _gates(ab, alog, dtb):
    g = -jnp.exp(alog) * jax.nn.softplus(ab + dtb)
    beta = jax.nn.sigmoid(ab)
    return g, beta


def _delta_prompt_kernel(q_ref, k_ref, v_ref, z_ref, ab_ref, alog_ref, dtb_ref, ng_ref,
                         yb_ref, sout_ref, s_ref, *, n_heads, n_chunks, ns):
    c = DN_CHUNK
    ci = pl.program_id(1)

    @pl.when(ci == 0)
    def _():
        s_ref[...] = jnp.zeros(s_ref.shape, F32)

    row = lax.broadcasted_iota(jnp.int32, (c, c), 0)
    col = lax.broadcasted_iota(jnp.int32, (c, c), 1)
    causal = row >= col
    strict = row > col
    eye = (row == col).astype(F32)
    tril = causal.astype(F32)
    n_double = (c - 1).bit_length() - 1
    ch = [(s, h) for s in range(ns) for h in range(n_heads)]
    sl = lambda h: slice(h * GROUP, (h + 1) * GROUP)
    gates = [_gates(ab_ref[s], alog_ref[...], dtb_ref[...]) for s in range(ns)]
    gc_s = [jnp.dot(tril, gates[s][0], precision=lax.Precision.HIGHEST, preferred_element_type=F32)
            for s in range(ns)]
    gct_s = [g.T for g in gc_s]
    q = [q_ref[s, :, sl(h)] for s, h in ch]
    k = [k_ref[s, :, sl(h)] for s, h in ch]
    gcol = [gc_s[s][:, h:h + 1] for s, h in ch]
    beta = [gates[s][1][:, n_heads + h:n_heads + h + 1] for s, h in ch]
    n = range(len(ch))
    decay = [jnp.exp(jnp.where(causal, gcol[i] - gct_s[s][h:h + 1, :], -jnp.inf)) for i, (s, h) in enumerate(ch)]
    kb = [k[i] * beta[i] for i in n]
    kbf = [k[i].astype(BF16) for i in n]
    a = [lax.dot_general(kb[i].astype(BF16), kbf[i], _NT, preferred_element_type=F32) * decay[i] for i in n]
    y = [jnp.where(strict, -a[i], 0.0) for i in n]
    x = [eye + y[i] for i in n]
    for _ in range(n_double):
        y = [_bdot(y[i], y[i]) for i in n]
        x = [x[i] + _bdot(x[i], y[i]) for i in n]
    egc = [jnp.exp(gcol[i]) for i in n]
    sol = [_bdot(x[i], jnp.concatenate([v_ref[s, :, sl(h)] * beta[i], kb[i] * egc[i]], axis=1))
           for i, (s, h) in enumerate(ch)]
    qk = [lax.dot_general(q[i].astype(BF16), kbf[i], _NT, preferred_element_type=F32) * decay[i] for i in n]
    s_old = [s_ref[s, h] for s, h in ch]
    r = [_bdot(jnp.concatenate([sol[i][:, GROUP:], q[i] * egc[i]], axis=0), s_old[i]) for i in n]
    v_new = [sol[i][:, :GROUP] - r[i][:c] for i in n]
    o = [r[i][c:] + _bdot(qk[i], v_new[i]) for i in n]
    gl = [gc_s[s][c - 1:c, h:h + 1] for s, h in ch]
    upd = [lax.dot_general((k[i] * jnp.exp(gl[i] - gcol[i])).astype(BF16), v_new[i].astype(BF16), _TN,
                           preferred_element_type=F32) for i in n]
    for i, (s, h) in enumerate(ch):
        s_ref[s, h] = s_old[i] * jnp.exp(gl[i]) + upd[i]
        yb_ref[s, :, sl(h)] = (_rms_norm(o[i], ng_ref[...]) * _silu(z_ref[s, :, sl(h)])).astype(yb_ref.dtype)

    @pl.when(ci == n_chunks - 1)
    def _():
        sout_ref[...] = s_ref[...]


def _delta_prompt(q, k, v, h, z_col0, ab, alog, dtb, norm_g, n_seq, rows_per_seq, ns):
    d_b = q.shape[1]
    n_heads = d_b // GROUP
    n_chunks = rows_per_seq // DN_CHUNK
    r3 = lambda t: t.reshape(n_seq, rows_per_seq, t.shape[-1])
    act = pl.BlockSpec((ns, DN_CHUNK, d_b), lambda s, i: (s, i, 0))
    vec = pl.BlockSpec((1, LANES), lambda s, i: (0, 0))
    zb = z_col0 // d_b
    yb, s_new = pl.pallas_call(
        functools.partial(_delta_prompt_kernel, n_heads=n_heads, n_chunks=n_chunks, ns=ns),
        grid=(n_seq // ns, n_chunks),
        in_specs=[act, act, act,
                  pl.BlockSpec((ns, DN_CHUNK, d_b), lambda s, i: (s, i, zb)),
                  pl.BlockSpec((ns, DN_CHUNK, LANES), lambda s, i: (s, i, 0)),
                  vec, vec, vec],
        out_specs=[act, pl.BlockSpec((ns, n_heads, GROUP, GROUP), lambda s, i: (s, 0, 0, 0))],
        out_shape=[jax.ShapeDtypeStruct((n_seq, rows_per_seq, d_b), BF16),
                   jax.ShapeDtypeStruct((n_seq, n_heads, GROUP, GROUP), F32)],
        scratch_shapes=[pltpu.VMEM((ns, n_heads, GROUP, GROUP), F32)],
        compiler_params=_params(("arbitrary", "arbitrary")),
        name="delta_prompt",
    )(r3(q), r3(k), r3(v), r3(h), r3(ab), alog, dtb, norm_g)
    return yb.reshape(n_seq * rows_per_seq, d_b), s_new


def _delta_sample_kernel(q_ref, k_ref, v_ref, z_ref, ab_ref, alog_ref, dtb_ref, ng_ref, s0_ref,
                         yb_ref, s1_ref, lhs_ref, res_ref, kt_ref, vn_ref, gl_ref, *, n_heads, n_pos, bb):
    nt = n_pos
    hs = range(n_heads)
    sls = [slice(h * GROUP, (h + 1) * GROUP) for h in hs]
    gates = [_gates(ab_ref[t], alog_ref[...], dtb_ref[...]) for t in range(nt)]
    zeros_pad = jnp.zeros((bb, GROUP), F32)
    u_all, qk_all, gc_all = [], [], []
    for h in hs:
        q = [q_ref[t, :, sls[h]] for t in range(nt)]
        k = [k_ref[t, :, sls[h]] for t in range(nt)]
        beta = [gates[t][1][:, n_heads + h:n_heads + h + 1] for t in range(nt)]
        gc = []
        for t in range(nt):
            gt = gates[t][0][:, h:h + 1]
            gc.append(gt if t == 0 else gc[t - 1] + gt)
        kb = [k[t] * beta[t] for t in range(nt)]
        dec = [[jnp.exp(gc[t] - gc[s]) for s in range(t + 1)] for t in range(nt)]
        a = [[jnp.sum(kb[t] * k[s], axis=-1, keepdims=True) * dec[t][s] for s in range(t)] for t in range(nt)]
        egc = [jnp.exp(gc[t]) for t in range(nt)]
        u_blk, w_blk = [], []
        for t in range(nt):
            ut, wt = v_ref[t, :, sls[h]] * beta[t], kb[t] * egc[t]
            for s in range(t):
                ut = ut - a[t][s] * u_blk[s]
                wt = wt - a[t][s] * w_blk[s]
            u_blk.append(ut)
            w_blk.append(wt)
        qk_all.append([[jnp.sum(q[t] * k[s], axis=-1, keepdims=True) * dec[t][s] for s in range(t + 1)]
                       for t in range(nt)])
        u_all.append(u_blk)
        gc_all.append(gc)
        for t in range(nt):
            lhs_ref[h, :, t, :] = w_blk[t]
            lhs_ref[h, :, nt + t, :] = q[t] * egc[t]
        gl_ref[h] = jnp.broadcast_to(jnp.exp(gc[nt - 1]), (bb, GROUP))

    for h in hs:
        for b in range(bb):
            res_ref[h, b] = _bdot(lhs_ref[h, b], s0_ref[b, h])

    for h in hs:
        gc, gl = gc_all[h], gc_all[h][nt - 1]
        v_new = [u_all[h][t] - res_ref[h, :, t, :] for t in range(nt)]
        for t in range(nt):
            o = res_ref[h, :, nt + t, :]
            for s in range(t + 1):
                o = o + qk_all[h][t][s] * v_new[s]
            yb_ref[t, :, sls[h]] = (_rms_norm(o, ng_ref[...]) * _silu(z_ref[t, :, sls[h]])).astype(yb_ref.dtype)
            kt_ref[h, :, t, :] = k_ref[t, :, sls[h]] * jnp.exp(gl - gc[t])
            vn_ref[h, :, t, :] = v_new[t]
            kt_ref[h, :, nt + t, :] = zeros_pad
            vn_ref[h, :, nt + t, :] = zeros_pad

    for h in hs:
        for b in range(bb):
            upd = lax.dot_general(kt_ref[h, b].astype(BF16), vn_ref[h, b].astype(BF16), _TN,
                                  preferred_element_type=F32)
            s1_ref[b, h] = s0_ref[b, h] * gl_ref[h, b:b + 1, :] + upd


def _delta_sample(q, k, v, h3, z_col0, ab, alog, dtb, norm_g, s0, bb):
    n_pos, nb, d_b = q.shape
    n_heads = d_b // GROUP
    assert 2 * n_pos == SUBLANES, "one state read packs [w | q] rows of a sequence into one 8-row tile"
    zb = z_col0 // d_b
    act = pl.BlockSpec((n_pos, bb, d_b), lambda i: (0, i, 0))
    vec = pl.BlockSpec((1, LANES), lambda i: (0, 0))
    st = pl.BlockSpec((bb, n_heads, GROUP, GROUP), lambda i: (i, 0, 0, 0))
    tile = pltpu.VMEM((n_heads, bb, SUBLANES, GROUP), F32)
    return pl.pallas_call(
        functools.partial(_delta_sample_kernel, n_heads=n_heads, n_pos=n_pos, bb=bb),
        grid=(nb // bb,),
        in_specs=[act, act, act,
                  pl.BlockSpec((n_pos, bb, d_b), lambda i: (0, i, zb)),
                  pl.BlockSpec((n_pos, bb, LANES), lambda i: (0, i, 0)),
                  vec, vec, vec, st],
        out_specs=[act, st],
        out_shape=[jax.ShapeDtypeStruct((n_pos, nb, d_b), BF16),
                   jax.ShapeDtypeStruct(s0.shape, F32)],
        scratch_shapes=[tile, tile, tile, tile, pltpu.VMEM((n_heads, bb, GROUP), F32)],
        compiler_params=_params(("arbitrary",)),
        name="delta_sample",
    )(q, k, v, h3, ab, alog, dtb, norm_g, s0)


def _out_proj_kernel(ya_ref, yb_ref, w_ref, r_ref, g_ref, b_ref, o_ref, ob_ref, z_ref, *, n_tiles, d_a, alpha):
    i = pl.program_id(0)

    @pl.when(i == 0)
    def _():
        z_ref[...] = jnp.zeros(z_ref.shape, F32)

    def emit_ln():
        y = _layer_norm(z_ref[...], g_ref[...], b_ref[...])
        o_ref[...] = y
        ob_ref[...] = y.astype(ob_ref.dtype)

    @pl.when(i < n_tiles)
    def _():
        emit_ln()
        acc = (jnp.dot(ya_ref[...], w_ref[0:d_a, :], preferred_element_type=F32)
               + jnp.dot(yb_ref[...], w_ref[d_a:, :], preferred_element_type=F32))
        z_ref[...] = alpha * r_ref[...] + acc

    @pl.when(i == n_tiles)
    def _():
        emit_ln()


def _out_proj_ln(ya, yb, w, resid, g, b, alpha, tm):
    m, d_a = ya.shape
    k, d = w.shape
    n_tiles = m // tm
    cur = lambda i: (jnp.minimum(i, n_tiles - 1), 0)
    prev = lambda i: (jnp.maximum(i - 1, 0), 0)
    vec = pl.BlockSpec((1, d), lambda i: (0, 0))
    return pl.pallas_call(
        functools.partial(_out_proj_kernel, n_tiles=n_tiles, d_a=d_a, alpha=alpha),
        grid=(n_tiles + 1,),
        in_specs=[pl.BlockSpec((tm, d_a), cur),
                  pl.BlockSpec((tm, k - d_a), cur),
                  pl.BlockSpec((k, d), lambda i: (0, 0), pipeline_mode=pl.Buffered(1)),
                  pl.BlockSpec((tm, d), cur), vec, vec],
        out_specs=[pl.BlockSpec((tm, d), prev), pl.BlockSpec((tm, d), prev)],
        out_shape=[jax.ShapeDtypeStruct((m, d), F32), jax.ShapeDtypeStruct((m, d), BF16)],
        scratch_shapes=[pltpu.VMEM((tm, d), F32)],
        compiler_params=_params(("arbitrary",)),
        name="out_proj_ln1",
    )(ya, yb, w, resid, g.reshape(1, d), b.reshape(1, d))


def _down_kernel(x_ref, w_ref, r_ref, g_ref, b_ref, o_ref, *rest, n_tiles, nk, alpha, n_pos):
    acc_ref, z_ref = rest[-2:]
    i, kk = pl.program_id(0), pl.program_id(1)

    def emit_ln():
        y = _layer_norm(z_ref[...], g_ref[...], b_ref[...])
        if n_pos:
            nb = y.shape[0] // n_pos
            for t in range(n_pos):
                o_ref[:, t, :] = y[t * nb:(t + 1) * nb]
        else:
            o_ref[...] = y

    @pl.when(jnp.logical_and(i == 0, kk == 0))
    def _():
        z_ref[...] = jnp.zeros(z_ref.shape, F32)

    wb = w_ref[...].astype(BF16)
    if len(rest) == 3:
        rest[0][...] = wb

    @pl.when(kk == 0)
    def _():
        acc_ref[...] = alpha * r_ref[...] + jnp.dot(x_ref[...], wb, preferred_element_type=F32)
        emit_ln()

    @pl.when(jnp.logical_and(jnp.logical_and(kk > 0, kk < nk - 1), i < n_tiles))
    def _():
        acc_ref[...] += jnp.dot(x_ref[...], wb, preferred_element_type=F32)

    @pl.when(jnp.logical_and(kk == nk - 1, i < n_tiles))
    def _():
        z_ref[...] = acc_ref[...] + jnp.dot(x_ref[...], wb, preferred_element_type=F32)


def _down_ln(x, w, resid, g, b, alpha, tm, tk, emit_w=False, n_pos=0):
    m, kdim = x.shape
    d = w.shape[1]
    nk = kdim // tk
    assert nk >= 2
    n_tiles = m // tm
    row = lambda i: jnp.minimum(i, n_tiles - 1)
    kt = lambda i, kk: jnp.where(i == n_tiles, nk - 1, kk)
    vec = pl.BlockSpec((1, d), lambda i, kk: (0, 0))
    wspec = pl.BlockSpec((tk, d), lambda i, kk: (kt(i, kk), 0))
    if n_pos:
        assert n_tiles == 1
        out_specs = [pl.BlockSpec((m // n_pos, n_pos, d), lambda i, kk: (0, 0, 0))]
        out_shape = [jax.ShapeDtypeStruct((m // n_pos, n_pos, d), F32)]
    else:
        out_specs = [pl.BlockSpec((tm, d), lambda i, kk: (jnp.where(kk == 0, jnp.maximum(i - 1, 0), row(i)), 0))]
        out_shape = [jax.ShapeDtypeStruct((m, d), F32)]
    if emit_w:
        out_specs.append(wspec)
        out_shape.append(jax.ShapeDtypeStruct(w.shape, BF16))
    outs = pl.pallas_call(
        functools.partial(_down_kernel, n_tiles=n_tiles, nk=nk, alpha=alpha, n_pos=n_pos),
        grid=(n_tiles + 1, nk),
        in_specs=[pl.BlockSpec((tm, tk), lambda i, kk: (row(i), kt(i, kk))), wspec,
                  pl.BlockSpec((tm, d), lambda i, kk: (row(i), 0)), vec, vec],
        out_specs=out_specs,
        out_shape=out_shape,
        scratch_shapes=[pltpu.VMEM((tm, d), F32), pltpu.VMEM((tm, d), F32)],
        compiler_params=_params(("arbitrary", "arbitrary")),
        name="down_ln2",
    )(x, w, resid, g.reshape(1, d), b.reshape(1, d))
    return tuple(outs) if emit_w else outs[0]


def _ffn_up_kernel(*refs, tm, shift, halo, has_buf, n_sub, cast_w):
    n_in = 9 if has_buf else 7
    if has_buf:
        x_ref, wg_ref, wv_ref, bg_ref, bv_ref, cwg_ref, cwv_ref, cbg_ref, cbv_ref = refs[:n_in]
        bufs = (bg_ref, bv_ref)
    else:
        x_ref, wg_ref, wv_ref, cwg_ref, cwv_ref, cbg_ref, cbv_ref = refs[:n_in]
        bufs = (None, None)
    act_ref, tg_ref, tv_ref = refs[n_in:n_in + 3]
    if cast_w:
        wgo_ref, wvo_ref, wgb_ref, wvb_ref = refs[n_in + 3:n_in + 7]
        scratch = refs[n_in + 7:]

        @pl.when(jnp.logical_and(pl.program_id(1) == 0, pl.program_id(2) == 0))
        def _():
            for w_ref, wb_ref, wo_ref in ((wg_ref, wgb_ref, wgo_ref), (wv_ref, wvb_ref, wvo_ref)):
                wb = w_ref[...].astype(BF16)
                wb_ref[...] = wb
                wo_ref[...] = wb
    else:
        wgb_ref, wvb_ref = wg_ref, wv_ref
        scratch = refs[n_in + 3:]
    xbufs, carry_ref = scratch[:-1], scratch[-1]

    wsub = act_ref.shape[1] // n_sub
    cols = [slice(c * wsub, (c + 1) * wsub) for c in range(n_sub)]
    first = pl.program_id(2) == 0

    @pl.when(first)
    def _():
        for c in range(n_sub):
            for p in range(2):
                if has_buf:
                    xbufs[2 * c + p][0:halo, :] = bufs[p][:, cols[c]]
                else:
                    xbufs[2 * c + p][0:halo, :] = jnp.zeros((halo, wsub), F32)

    @pl.when(jnp.logical_not(first))
    def _():
        for c in range(n_sub):
            for p in range(2):
                xbufs[2 * c + p][0:halo, :] = carry_ref[p, :, cols[c]]

    x = x_ref[...]
    parts = ((wgb_ref, cwg_ref, cbg_ref, tg_ref), (wvb_ref, cwv_ref, cbv_ref, tv_ref))
    for c in range(n_sub):
        conv = []
        for p, (w_ref, cw_ref, cb_ref, t_ref) in enumerate(parts):
            xbuf_ref = xbufs[2 * c + p]
            xbuf_ref[halo:halo + tm, :] = jnp.dot(x, w_ref[:, cols[c]], preferred_element_type=F32)
            tail = xbuf_ref[tm:tm + halo, :]
            carry_ref[p, :, cols[c]] = tail
            if has_buf:
                for t in range(halo // shift):
                    t_ref[:, t, cols[c]] = tail[t * shift:(t + 1) * shift]
            else:
                t_ref[0, :, cols[c]] = tail
            conv.append(_conv_taps(xbuf_ref, cw_ref, tm, shift, halo, CONV_FFN, cols[c]) + cb_ref[:, cols[c]])
        act_ref[:, cols[c]] = (_silu(conv[0]) * conv[1]).astype(act_ref.dtype)


def _ffn_up(x, w_up, conv_w, conv_b, bufs, n_seq, rows_per_seq, tm, tn, shift):
    m, d = x.shape
    cast_w = not isinstance(w_up, tuple)
    d_ff = w_up.shape[1] // 2 if cast_w else w_up[0].shape[1]
    nj = d_ff // tn
    has_buf = bufs is not None
    halo = (CONV_FFN - 1) * shift if has_buf else SUBLANES
    tiles = rows_per_seq // tm
    wspec = lambda off: pl.BlockSpec((d, tn), lambda j, s, i: (0, off + j))
    cwspec = lambda off: pl.BlockSpec((CONV_FFN, tn), lambda j, s, i: (0, off + j))
    cbspec = lambda off: pl.BlockSpec((1, tn), lambda j, s, i: (0, off + j))
    if has_buf:
        assert n_seq == 1
        tspec = pl.BlockSpec((shift, halo // shift, tn), lambda j, s, i: (0, 0, j))
        tshape = jax.ShapeDtypeStruct((shift, halo // shift, d_ff), F32)
    else:
        tspec = pl.BlockSpec((1, halo, tn), lambda j, s, i: (s, 0, j))
        tshape = jax.ShapeDtypeStruct((n_seq, halo, d_ff), F32)
    specs = [pl.BlockSpec((tm, d), lambda j, s, i: (s * tiles + i, 0)), wspec(0), wspec(nj if cast_w else 0)]
    args = [x, w_up, w_up] if cast_w else [x, w_up[0], w_up[1]]
    out_specs = [pl.BlockSpec((tm, tn), lambda j, s, i: (s * tiles + i, j)), tspec, tspec]
    out_shape = [jax.ShapeDtypeStruct((m, d_ff), BF16), tshape, tshape]
    scratch = []
    if cast_w:
        out_specs += [wspec(0), wspec(0)]
        out_shape += [jax.ShapeDtypeStruct((d, d_ff), BF16)] * 2
        scratch += [pltpu.VMEM((d, tn), BF16)] * 2
    if has_buf:
        specs += [pl.BlockSpec((halo, tn), lambda j, s, i, off=off: (s, off + j)) for off in (0, nj)]
        args += [bufs, bufs]
    specs += [cwspec(0), cwspec(nj), cbspec(0), cbspec(nj)]
    cb2 = conv_b.reshape(1, 2 * d_ff)
    args += [conv_w, conv_w, cb2, cb2]
    return pl.pallas_call(
        functools.partial(_ffn_up_kernel, tm=tm, shift=shift, halo=halo, has_buf=has_buf, n_sub=FFN_SUB,
                          cast_w=cast_w),
        grid=(nj, n_seq, tiles),
        in_specs=specs,
        out_specs=out_specs,
        out_shape=out_shape,
        scratch_shapes=scratch + [pltpu.VMEM((halo + tm, tn // FFN_SUB), F32)] * (2 * FFN_SUB)
                       + [pltpu.VMEM((2, halo, tn), F32)],
        compiler_params=_params(("arbitrary", "arbitrary", "arbitrary")),
        name="ffn_up",
    )(*args)


def _pad_lanes(vec, offset=0):
    out = jnp.zeros((1, LANES), F32)
    return out.at[0, offset:offset + vec.shape[0]].set(vec.astype(F32))


def _layer(x2, lw, alpha, n_seq, rows_per_seq, shift, state, wb, last):
    (w_in, conv_qkv_w, a_log, dt_bias, norm_g, gm_ln_g, gm_ln_b, gm_ws, gm_bs, gm_out_g, w_out,
     ln1_g, ln1_b, w_up, conv_ffn_w, conv_ffn_b, w_down, ln2_g, ln2_b) = lw
    m, d_model = x2.shape
    n_groups = gm_ws.shape[0]
    d_a = n_groups * GROUP
    d_qkv = conv_qkv_w.shape[1]
    d_b = d_qkv // 3
    n_heads = d_b // GROUP
    main_cols = 2 * d_a + d_qkv + d_b
    fresh = state is None
    n_pos = rows_per_seq // shift
    tm = min(m, 1024)

    w_ab = jnp.pad(w_in[main_cols:].astype(BF16), ((0, LANES - 2 * n_heads), (0, 0)))
    if wb is None:
        h, ab, w_in_b = _matmul(x2, w_in, w_ab, main_cols, tm, 1024, emit_w=True)
    else:
        w_in_b = wb[0]
        h, ab = _matmul(x2, w_in_b, w_ab, main_cols, tm, 1024)
    alog, dtb, ng = _pad_lanes(a_log), _pad_lanes(dt_bias), norm_g.reshape(1, GROUP).astype(F32)

    if fresh:
        ya = _chunk_mlp_prompt(h, gm_ln_g, gm_ln_b, gm_ws, gm_bs, gm_out_g)
        v_rows = None
        q, k, v = _qkv_prep(h, conv_qkv_w, None, n_seq, rows_per_seq, 512, shift, 2 * d_a)
        yb, s_new = _delta_prompt(q, k, v, h, 2 * d_a + d_qkv, ab, alog, dtb, ng, n_seq, rows_per_seq, 4)
        ffn_bufs = None
        tm_ffn = 1024
    else:
        s_dn, buf_qkv, buf_ffn = state
        h3 = h.reshape(n_pos, shift, main_cols)
        ya3, v_rows = _chunk_mlp_sample(h3, gm_ln_g, gm_ln_b, gm_ws, gm_bs, gm_out_g)
        ya = ya3.reshape(m, d_a)
        q, k, v = _qkv_prep(h, conv_qkv_w, buf_qkv, n_seq, rows_per_seq, rows_per_seq, shift, 2 * d_a)
        r3 = lambda t: t.reshape(n_pos, shift, t.shape[-1])
        yb3, s_new = _delta_sample(r3(q), r3(k), r3(v), h3, 2 * d_a + d_qkv, r3(ab), alog, dtb, ng, s_dn, 16)
        yb = yb3.reshape(m, d_b)
        ffn_bufs = buf_ffn
        tm_ffn = rows_per_seq

    x1, x1b = _out_proj_ln(ya, yb, w_out, x2, ln1_g, ln1_b, alpha, min(m, 512))
    if wb is None:
        act, tail_g, tail_v, w_gate_b, w_val_b = _ffn_up(x1b, w_up, conv_ffn_w, conv_ffn_b, ffn_bufs,
                                                         n_seq, rows_per_seq, tm_ffn, 512, shift)
        y, w_down_b = _down_ln(act, w_down, x1, ln2_g, ln2_b, alpha, min(m, 512), 512, emit_w=True,
                               n_pos=n_pos if (last and not fresh) else 0)
    else:
        _, w_down_b, w_gate_b, w_val_b = wb
        act, tail_g, tail_v = _ffn_up(x1b, (w_gate_b, w_val_b), conv_ffn_w, conv_ffn_b, ffn_bufs,
                                      n_seq, rows_per_seq, tm_ffn, 512, shift)
        y = _down_ln(act, w_down_b, x1, ln2_g, ln2_b, alpha, min(m, 512), 1408)
    return y, s_new, h, tail_g, tail_v, v_rows, (w_in_b, w_down_b, w_gate_b, w_val_b)


def kernel(x_prompt, x_sample, state_dn, state_conv_qkv, state_conv_ffn, w_in, conv_qkv_w, dn_a_log, dn_dt_bias, dn_norm_g, gm_ln_g, gm_ln_b, gm_ws, gm_bs, gm_out_g, w_out, ln1_g, ln1_b, w_up, conv_ffn_w, conv_ffn_b, w_down, ln2_g, ln2_b):
    depth = w_in.shape[0]
    bp, lp, d_model = x_prompt.shape
    bs, ls, _ = x_sample.shape
    d_qkv = conv_qkv_w.shape[-1]
    d_a = gm_ws.shape[1] * GROUP
    alpha = (2.0 * depth) ** 0.25

    hp = x_prompt.reshape(bp * lp, d_model)
    hs = jnp.transpose(x_sample, (1, 0, 2)).reshape(ls * bs, d_model)
    outs = [[] for _ in range(7)]
    w_out = w_out.astype(BF16)
    w_in = jnp.swapaxes(w_in, 1, 2)
    for l in range(depth):
        lw = tuple(t[l] for t in (w_in, conv_qkv_w, dn_a_log, dn_dt_bias, dn_norm_g, gm_ln_g, gm_ln_b, gm_ws,
                                  gm_bs, gm_out_g, w_out, ln1_g, ln1_b, w_up, conv_ffn_w, conv_ffn_b, w_down,
                                  ln2_g, ln2_b))
        last = l == depth - 1
        bq = jnp.transpose(state_conv_qkv[l], (1, 0, 2)).reshape((CONV_QKV - 1) * bs, d_qkv)
        bf = jnp.transpose(state_conv_ffn[l], (1, 0, 2)).reshape((CONV_FFN - 1) * bs, -1)
        hs, s_s, h_s, tg_s, tv_s, v_s, wb = _layer(hs, lw, alpha, 1, ls * bs, bs, (state_dn[l], bq, bf), None, last)
        h_s3 = h_s.reshape(ls, bs, -1)
        cq_s = jnp.transpose(h_s3[ls - (CONV_QKV - 1):, :, 2 * d_a:2 * d_a + d_qkv], (1, 0, 2))
        cf_s = jnp.concatenate([tg_s, tv_s], axis=-1)
        hp, s_p, h_p, tg_p, tv_p, _, _ = _layer(hp, lw, alpha, bp, lp, 1, None, wb, last)
        h_p3 = h_p.reshape(bp, lp, -1)
        cq_p = h_p3[:, lp - (CONV_QKV - 1):, 2 * d_a:2 * d_a + d_qkv]
        n_tail = tg_p.shape[1]
        cf_p = jnp.concatenate([tg_p, tv_p], axis=-1)[:, n_tail - (CONV_FFN - 1):]
        for acc, val in zip(outs, (s_p, cq_p, cf_p, s_s, cq_s, cf_s, v_s)):
            acc.append(val)
    return (hp.reshape(bp, lp, d_model), hs) + tuple(jnp.stack(o) for o in outs)
```

```python
import functools

import jax
import jax.numpy as jnp
from jax import lax
from jax.experimental import pallas as pl
from jax.experimental.pallas import tpu as pltpu

F32 = jnp.float32
BF16 = jnp.bfloat16

LANES = 128
SUBLANES = 8
VMEM_LIMIT = 56 * 1024 * 1024

GROUP = 128
MLP_CHUNK = 128
DN_CHUNK = 64
CONV_QKV = 4
CONV_FFN = 3
FFN_SUB = 1
CMLP_CHUNKS = 2
EPS = 1e-5

_NT = (((1,), (1,)), ((), ()))
_TN = (((0,), (0,)), ((), ()))


def _params(sem):
    return pltpu.CompilerParams(dimension_semantics=sem, vmem_limit_bytes=VMEM_LIMIT)


def _bdot(a, b):
    return jnp.dot(a.astype(BF16), b.astype(BF16), preferred_element_type=F32)


def _gelu(x):
    return 0.5 * x * (1.0 + lax.erf(x * (0.5 ** 0.5)))


def _silu(x):
    return x * jax.nn.sigmoid(x)


def _layer_norm(x, g, b):
    mu = jnp.mean(x, axis=-1, keepdims=True)
    xc = x - mu
    var = jnp.mean(xc * xc, axis=-1, keepdims=True)
    return xc * lax.rsqrt(var + EPS) * g + b


def _rms_norm(x, g):
    ms = jnp.mean(x * x, axis=-1, keepdims=True)
    return x * lax.rsqrt(ms + EPS) * g


def _mm_kernel(x_ref, w_ref, we_ref, o_ref, e_ref, *rest):
    xb_ref = rest[-1]

    @pl.when(pl.program_id(1) == 0)
    def _():
        xb = x_ref[...].astype(BF16)
        xb_ref[...] = xb
        e_ref[...] = lax.dot_general(xb, we_ref[...], _NT, preferred_element_type=F32)

    wb = w_ref[...].astype(BF16)
    if len(rest) == 2:
        rest[0][...] = wb
    o_ref[...] = lax.dot_general(xb_ref[...], wb, _NT, preferred_element_type=F32)


def _matmul(x, wt, wt_extra, n_cols, tm, tn, emit_w=False):
    m, k = x.shape
    ne = wt_extra.shape[0]
    out_specs = [pl.BlockSpec((tm, tn), lambda i, j: (i, j)), pl.BlockSpec((tm, ne), lambda i, j: (i, 0))]
    out_shape = [jax.ShapeDtypeStruct((m, n_cols), F32), jax.ShapeDtypeStruct((m, ne), F32)]
    if emit_w:
        out_specs.append(pl.BlockSpec((tn, k), lambda i, j: (j, 0)))
        out_shape.append(jax.ShapeDtypeStruct((n_cols, k), BF16))
    return pl.pallas_call(
        _mm_kernel,
        grid=(m // tm, n_cols // tn),
        in_specs=[pl.BlockSpec((tm, k), lambda i, j: (i, 0)),
                  pl.BlockSpec((tn, k), lambda i, j: (j, 0)),
                  pl.BlockSpec((ne, k), lambda i, j: (0, 0))],
        out_specs=out_specs,
        out_shape=out_shape,
        scratch_shapes=[pltpu.VMEM((tm, k), BF16)],
        compiler_params=_params(("arbitrary", "arbitrary")),
        name="proj",
    )(x, wt, wt_extra)


def _cmlp_prompt_kernel(u_ref, v_ref, lng_ref, lnb_ref, ws_ref, bst_ref, og_ref, ya_ref, *, n_groups, n_chunks):
    row = lax.broadcasted_iota(jnp.int32, (MLP_CHUNK, MLP_CHUNK), 0)
    col = lax.broadcasted_iota(jnp.int32, (MLP_CHUNK, MLP_CHUNK), 1)
    cg = [(c, g) for c in range(n_chunks) for g in range(n_groups)]
    rows = lambda c: slice(c * MLP_CHUNK, (c + 1) * MLP_CHUNK)
    cols = lambda g: slice(g * GROUP, (g + 1) * GROUP)
    w = [jnp.where(row >= col, ws_ref[g], 0.0).astype(BF16) for g in range(n_groups)]
    n = range(len(cg))
    gv = [_gelu(v_ref[rows(c), cols(g)]) for c, g in cg]
    mu = [jnp.mean(gv[i], axis=-1, keepdims=True) for i in n]
    xc = [gv[i] - mu[i] for i in n]
    var = [jnp.mean(xc[i] * xc[i], axis=-1, keepdims=True) for i in n]
    vh = [xc[i] * lax.rsqrt(var[i] + EPS) * lng_ref[:, cols(g)] + lnb_ref[:, cols(g)] for i, (c, g) in enumerate(cg)]
    mixed = [jnp.dot(w[g], vh[i].astype(BF16), preferred_element_type=F32) + bst_ref[:, g:g + 1]
             for i, (c, g) in enumerate(cg)]
    y = [_gelu(u_ref[rows(c), cols(g)]) * mixed[i] for i, (c, g) in enumerate(cg)]
    ms = [jnp.mean(y[i] * y[i], axis=-1, keepdims=True) for i in n]
    for i, (c, g) in enumerate(cg):
        ya_ref[rows(c), cols(g)] = (y[i] * lax.rsqrt(ms[i] + EPS) * og_ref[:, cols(g)]).astype(ya_ref.dtype)


def _chunk_mlp_prompt(h, ln_g, ln_b, ws, bs, out_g):
    m = h.shape[0]
    n_groups, d_a = ws.shape[0], ws.shape[0] * GROUP
    vec = pl.BlockSpec((1, d_a), lambda i: (0, 0))
    tm = CMLP_CHUNKS * MLP_CHUNK
    return pl.pallas_call(
        functools.partial(_cmlp_prompt_kernel, n_groups=n_groups, n_chunks=CMLP_CHUNKS),
        grid=(m // tm,),
        in_specs=[pl.BlockSpec((tm, d_a), lambda i: (i, 0)),
                  pl.BlockSpec((tm, d_a), lambda i: (i, 1)),
                  vec, vec,
                  pl.BlockSpec((n_groups, MLP_CHUNK, MLP_CHUNK), lambda i: (0, 0, 0)),
                  pl.BlockSpec((MLP_CHUNK, n_groups), lambda i: (0, 0)),
                  vec],
        out_specs=pl.BlockSpec((tm, d_a), lambda i: (i, 0)),
        out_shape=jax.ShapeDtypeStruct((m, d_a), BF16),
        compiler_params=_params(("arbitrary",)),
        name="chunk_mlp_prompt",
    )(h, h, ln_g.reshape(1, d_a), ln_b.reshape(1, d_a), ws, bs.T, out_g.reshape(1, d_a))


def _cmlp_sample_kernel(u_ref, v_ref, lng_ref, lnb_ref, wv_ref, bv_ref, og_ref, ya_ref, vr_ref, *, n_pos):
    vh = []
    for t in range(n_pos):
        vt = _layer_norm(_gelu(v_ref[t]), lng_ref[0], lnb_ref[0])
        vr_ref[:, t, :] = vt
        vh.append(vt)
    for t in range(n_pos):
        mixed = wv_ref[t, 0:1, :] * vh[0]
        for s in range(1, t + 1):
            mixed = mixed + wv_ref[t, s:s + 1, :] * vh[s]
        mixed = mixed + bv_ref[0, t:t + 1, :]
        y = _gelu(u_ref[t]) * mixed
        ya_ref[t] = _rms_norm(y, og_ref[0]).astype(ya_ref.dtype)


def _chunk_mlp_sample(h3, ln_g, ln_b, ws, bs, out_g):
    n_pos, nb, _ = h3.shape
    n_groups = ws.shape[0]
    d_a = n_groups * GROUP
    wv = jnp.repeat(jnp.transpose(ws[:, :n_pos, :n_pos], (1, 2, 0)), GROUP, axis=-1)
    bv = jnp.repeat(bs[:, :n_pos].T, GROUP, axis=-1).reshape(1, n_pos, d_a)
    vec = pl.BlockSpec((1, 1, GROUP), lambda g: (0, 0, g))
    act = pl.BlockSpec((n_pos, nb, GROUP), lambda g: (0, 0, g))
    return pl.pallas_call(
        functools.partial(_cmlp_sample_kernel, n_pos=n_pos),
        grid=(n_groups,),
        in_specs=[act,
                  pl.BlockSpec((n_pos, nb, GROUP), lambda g: (0, 0, n_groups + g)),
                  vec, vec,
                  pl.BlockSpec((n_pos, n_pos, GROUP), lambda g: (0, 0, g)),
                  pl.BlockSpec((1, n_pos, GROUP), lambda g: (0, 0, g)),
                  vec],
        out_specs=[act, pl.BlockSpec((nb, n_pos, GROUP), lambda g: (0, 0, g))],
        out_shape=[jax.ShapeDtypeStruct((n_pos, nb, d_a), BF16),
                   jax.ShapeDtypeStruct((nb, n_pos, d_a), F32)],
        compiler_params=_params(("arbitrary",)),
        name="chunk_mlp_sample",
    )(h3, h3, ln_g.reshape(1, 1, d_a), ln_b.reshape(1, 1, d_a), wv, bv, out_g.reshape(1, 1, d_a))


def _conv_taps(xbuf_ref, w_ref, tm, shift, halo, width, cols=slice(None)):
    acc = None
    for j in range(width):
        off = halo - (width - 1 - j) * shift
        term = w_ref[j:j + 1, cols] * xbuf_ref[off:off + tm, :]
        acc = term if acc is None else acc + term
    return acc


def _qkv_prep_kernel(*refs, tm, shift, halo, n_heads, has_buf):
    if has_buf:
        (xq_ref, xk_ref, xv_ref, bq_ref, bk_ref, bv_ref, wq_ref, wk_ref, wv_ref,
         q_ref, k_ref, v_ref, xbuf_ref, carry_ref) = refs
        bufs = (bq_ref, bk_ref, bv_ref)
    else:
        (xq_ref, xk_ref, xv_ref, wq_ref, wk_ref, wv_ref,
         q_ref, k_ref, v_ref, xbuf_ref, carry_ref) = refs
        bufs = (None, None, None)
    first = pl.program_id(1) == 0
    parts = ((xq_ref, wq_ref, q_ref, GROUP ** -0.5), (xk_ref, wk_ref, k_ref, 1.0), (xv_ref, wv_ref, v_ref, None))
    for p, (x_ref, w_ref, o_ref, scale) in enumerate(parts):
        @pl.when(first)
        def _(p=p):
            if has_buf:
                xbuf_ref[0:halo, :] = bufs[p][...]
            else:
                xbuf_ref[0:halo, :] = jnp.zeros((halo, xbuf_ref.shape[1]), F32)

        @pl.when(jnp.logical_not(first))
        def _(p=p):
            xbuf_ref[0:halo, :] = carry_ref[p]

        xbuf_ref[halo:halo + tm, :] = x_ref[...]
        carry_ref[p] = xbuf_ref[tm:tm + halo, :]
        y = _silu(_conv_taps(xbuf_ref, w_ref, tm, shift, halo, CONV_QKV))
        if scale is None:
            o_ref[...] = y
        else:
            for h in range(n_heads):
                sl = slice(h * GROUP, (h + 1) * GROUP)
                yh = y[:, sl]
                yn = yh * lax.rsqrt(jnp.sum(yh * yh, axis=-1, keepdims=True) + 1e-6)
                o_ref[:, sl] = yn * scale if scale != 1.0 else yn


def _qkv_prep(h, conv_w, bufs, n_seq, rows_per_seq, tm, shift, col0):
    d_b = conv_w.shape[1] // 3
    n_heads = d_b // GROUP
    has_buf = bufs is not None
    halo = (CONV_QKV - 1) * shift if has_buf else SUBLANES
    tiles = rows_per_seq // tm
    cb = col0 // d_b
    x_specs = [pl.BlockSpec((tm, d_b), lambda s, i, c=c: (s * tiles + i, cb + c)) for c in range(3)]
    w_specs = [pl.BlockSpec((CONV_QKV, d_b), lambda s, i, c=c: (0, c)) for c in range(3)]
    o_spec = pl.BlockSpec((tm, d_b), lambda s, i: (s * tiles + i, 0))
    args, specs = [h, h, h], list(x_specs)
    if has_buf:
        specs += [pl.BlockSpec((halo, d_b), lambda s, i, c=c: (s, c)) for c in range(3)]
        args += [bufs, bufs, bufs]
    specs += w_specs
    args += [conv_w, conv_w, conv_w]
    m = n_seq * rows_per_seq
    return pl.pallas_call(
        functools.partial(_qkv_prep_kernel, tm=tm, shift=shift, halo=halo, n_heads=n_heads, has_buf=has_buf),
        grid=(n_seq, tiles),
        in_specs=specs,
        out_specs=[o_spec, o_spec, o_spec],
        out_shape=[jax.ShapeDtypeStruct((m, d_b), F32)] * 3,
        scratch_shapes=[pltpu.VMEM((halo + tm, d_b), F32), pltpu.VMEM((3, halo, d_b), F32)],
        compiler_params=_params(("arbitrary", "arbitrary")),
        name="qkv_prep",
    )(*args)


def _gates(ab, alog, dtb):
    g = -jnp.exp(alog) * jax.nn.softplus(ab + dtb)
    beta = jax.nn.sigmoid(ab)
    return g, beta


def _delta_prompt_kernel(q_ref, k_ref, v_ref, z_ref, ab_ref, alog_ref, dtb_ref, ng_ref,
                         yb_ref, sout_ref, s_ref, *, n_heads, n_chunks, ns):
    c = DN_CHUNK
    ci = pl.program_id(1)

    @pl.when(ci == 0)
    def _():
        s_ref[...] = jnp.zeros(s_ref.shape, F32)

    row = lax.broadcasted_iota(jnp.int32, (c, c), 0)
    col = lax.broadcasted_iota(jnp.int32, (c, c), 1)
    causal = row >= col
    strict = row > col
    eye = (row == col).astype(F32)
    tril = causal.astype(F32)
    n_double = (c - 1).bit_length() - 1
    ch = [(s, h) for s in range(ns) for h in range(n_heads)]
    sl = lambda h: slice(h * GROUP, (h + 1) * GROUP)
    gates = [_gates(ab_ref[s], alog_ref[...], dtb_ref[...]) for s in range(ns)]
    gc_s = [jnp.dot(tril, gates[s][0], precision=lax.Precision.HIGHEST, preferred_element_type=F32)
            for s in range(ns)]
    gct_s = [g.T for g in gc_s]
    q = [q_ref[s, :, sl(h)] for s, h in ch]
    k = [k_ref[s, :, sl(h)] for s, h in ch]
    gcol = [gc_s[s][:, h:h + 1] for s, h in ch]
    beta = [gates[s][1][:, n_heads + h:n_heads + h + 1] for s, h in ch]
    n = range(len(ch))
    decay = [jnp.exp(jnp.where(causal, gcol[i] - gct_s[s][h:h + 1, :], -jnp.inf)) for i, (s, h) in enumerate(ch)]
    kb = [k[i] * beta[i] for i in n]
    kbf = [k[i].astype(BF16) for i in n]
    kq = [lax.dot_general(jnp.concatenate([kb[i], q[i]], axis=0).astype(BF16), kbf[i], _NT,
                          preferred_element_type=F32) for i in n]
    qk = [kq[i][c:] * decay[i] for i in n]
    y = [jnp.where(strict, -(kq[i][:c] * decay[i]), 0.0) for i in n]
    x = [eye + y[i] for i in n]
    y = [_bdot(y[i], y[i]) for i in n]
    for _ in range(n_double - 1):
        t = [_bdot(jnp.concatenate([x[i], y[i]], axis=0), y[i]) for i in n]
        x = [x[i] + t[i][:c] for i in n]
        y = [t[i][c:] for i in n]
    x = [x[i] + _bdot(x[i], y[i]) for i in n]
    egc = [jnp.exp(gcol[i]) for i in n]
    sol = [_bdot(x[i], jnp.concatenate([v_ref[s, :, sl(h)] * beta[i], kb[i] * egc[i]], axis=1))
           for i, (s, h) in enumerate(ch)]
    s_old = [s_ref[s, h] for s, h in ch]
    r = [_bdot(jnp.concatenate([sol[i][:, GROUP:], q[i] * egc[i]], axis=0), s_old[i]) for i in n]
    v_new = [sol[i][:, :GROUP] - r[i][:c] for i in n]
    o = [r[i][c:] + _bdot(qk[i], v_new[i]) for i in n]
    gl = [gc_s[s][c - 1:c, h:h + 1] for s, h in ch]
    upd = [lax.dot_general((k[i] * jnp.exp(gl[i] - gcol[i])).astype(BF16), v_new[i].astype(BF16), _TN,
                           preferred_element_type=F32) for i in n]
    ms = [jnp.mean(o[i] * o[i], axis=-1, keepdims=True) for i in n]
    for i, (s, h) in enumerate(ch):
        s_ref[s, h] = s_old[i] * jnp.exp(gl[i]) + upd[i]
        on = o[i] * lax.rsqrt(ms[i] + EPS) * ng_ref[...]
        yb_ref[s, :, sl(h)] = (on * _silu(z_ref[s, :, sl(h)])).astype(yb_ref.dtype)

    @pl.when(ci == n_chunks - 1)
    def _():
        sout_ref[...] = s_ref[...]


def _delta_prompt(q, k, v, h, z_col0, ab, alog, dtb, norm_g, n_seq, rows_per_seq, ns):
    d_b = q.shape[1]
    n_heads = d_b // GROUP
    n_chunks = rows_per_seq // DN_CHUNK
    r3 = lambda t: t.reshape(n_seq, rows_per_seq, t.shape[-1])
    act = pl.BlockSpec((ns, DN_CHUNK, d_b), lambda s, i: (s, i, 0))
    vec = pl.BlockSpec((1, LANES), lambda s, i: (0, 0))
    zb = z_col0 // d_b
    yb, s_new = pl.pallas_call(
        functools.partial(_delta_prompt_kernel, n_heads=n_heads, n_chunks=n_chunks, ns=ns),
        grid=(n_seq // ns, n_chunks),
        in_specs=[act, act, act,
                  pl.BlockSpec((ns, DN_CHUNK, d_b), lambda s, i: (s, i, zb)),
                  pl.BlockSpec((ns, DN_CHUNK, LANES), lambda s, i: (s, i, 0)),
                  vec, vec, vec],
        out_specs=[act, pl.BlockSpec((ns, n_heads, GROUP, GROUP), lambda s, i: (s, 0, 0, 0))],
        out_shape=[jax.ShapeDtypeStruct((n_seq, rows_per_seq, d_b), BF16),
                   jax.ShapeDtypeStruct((n_seq, n_heads, GROUP, GROUP), F32)],
        scratch_shapes=[pltpu.VMEM((ns, n_heads, GROUP, GROUP), F32)],
        compiler_params=_params(("arbitrary", "arbitrary")),
        name="delta_prompt",
    )(r3(q), r3(k), r3(v), r3(h), r3(ab), alog, dtb, norm_g)
    return yb.reshape(n_seq * rows_per_seq, d_b), s_new


def _delta_sample_kernel(q_ref, k_ref, v_ref, z_ref, ab_ref, alog_ref, dtb_ref, ng_ref, s0_ref,
                         yb_ref, s1_ref, lhs_ref, res_ref, kt_ref, vn_ref, gl_ref, *, n_heads, n_pos, bb):
    nt = n_pos
    hs = range(n_heads)
    sls = [slice(h * GROUP, (h + 1) * GROUP) for h in hs]
    gates = [_gates(ab_ref[t], alog_ref[...], dtb_ref[...]) for t in range(nt)]
    zeros_pad = jnp.zeros((bb, GROUP), F32)
    ts = range(nt)
    q = [[q_ref[t, :, sls[h]] for t in ts] for h in hs]
    k = [[k_ref[t, :, sls[h]] for t in ts] for h in hs]
    beta = [[gates[t][1][:, n_heads + h:n_heads + h + 1] for t in ts] for h in hs]
    gc = []
    for h in hs:
        g = [gates[t][0][:, h:h + 1] for t in ts]
        for t in range(1, nt):
            g[t] = g[t - 1] + g[t]
        gc.append(g)
    kb = [[k[h][t] * beta[h][t] for t in ts] for h in hs]
    kk = [[[jnp.sum(kb[h][t] * k[h][s], axis=-1, keepdims=True) for s in range(t)] for t in ts] for h in hs]
    qk = [[[jnp.sum(q[h][t] * k[h][s], axis=-1, keepdims=True) for s in range(t + 1)] for t in ts] for h in hs]
    dec = [[[jnp.exp(gc[h][t] - gc[h][s]) for s in range(t + 1)] for t in ts] for h in hs]
    a = [[[kk[h][t][s] * dec[h][t][s] for s in range(t)] for t in ts] for h in hs]
    qk = [[[qk[h][t][s] * dec[h][t][s] for s in range(t + 1)] for t in ts] for h in hs]
    egc = [[jnp.exp(gc[h][t]) for t in ts] for h in hs]
    u_blk = [[v_ref[t, :, sls[h]] * beta[h][t] for t in ts] for h in hs]
    w_blk = [[kb[h][t] * egc[h][t] for t in ts] for h in hs]
    for t in ts:
        for h in hs:
            for s in range(t):
                u_blk[h][t] = u_blk[h][t] - a[h][t][s] * u_blk[h][s]
                w_blk[h][t] = w_blk[h][t] - a[h][t][s] * w_blk[h][s]
    for h in hs:
        for t in ts:
            lhs_ref[h, :, t, :] = w_blk[h][t]
            lhs_ref[h, :, nt + t, :] = q[h][t] * egc[h][t]
        gl_ref[h] = jnp.broadcast_to(egc[h][nt - 1], (bb, GROUP))

    for h in hs:
        for b in range(bb):
            res_ref[h, b] = _bdot(lhs_ref[h, b], s0_ref[b, h])

    v_new = [[u_blk[h][t] - res_ref[h, :, t, :] for t in ts] for h in hs]
    o = [[res_ref[h, :, nt + t, :] for t in ts] for h in hs]
    for h in hs:
        for t in ts:
            for s in range(t + 1):
                o[h][t] = o[h][t] + qk[h][t][s] * v_new[h][s]
    ms = [[jnp.mean(o[h][t] * o[h][t], axis=-1, keepdims=True) for t in ts] for h in hs]
    for h in hs:
        gl = gc[h][nt - 1]
        for t in ts:
            on = o[h][t] * lax.rsqrt(ms[h][t] + EPS) * ng_ref[...]
            yb_ref[t, :, sls[h]] = (on * _silu(z_ref[t, :, sls[h]])).astype(yb_ref.dtype)
            kt_ref[h, :, t, :] = k[h][t] * jnp.exp(gl - gc[h][t])
            vn_ref[h, :, t, :] = v_new[h][t]
            kt_ref[h, :, nt + t, :] = zeros_pad
            vn_ref[h, :, nt + t, :] = zeros_pad

    for h in hs:
        for b in range(bb):
            upd = lax.dot_general(kt_ref[h, b].astype(BF16), vn_ref[h, b].astype(BF16), _TN,
                                  preferred_element_type=F32)
            s1_ref[b, h] = s0_ref[b, h] * gl_ref[h, b:b + 1, :] + upd


def _delta_sample(q, k, v, h3, z_col0, ab, alog, dtb, norm_g, s0, bb):
    n_pos, nb, d_b = q.shape
    n_heads = d_b // GROUP
    assert 2 * n_pos == SUBLANES, "one state read packs [w | q] rows of a sequence into one 8-row tile"
    zb = z_col0 // d_b
    act = pl.BlockSpec((n_pos, bb, d_b), lambda i: (0, i, 0))
    vec = pl.BlockSpec((1, LANES), lambda i: (0, 0))
    st = pl.BlockSpec((bb, n_heads, GROUP, GROUP), lambda i: (i, 0, 0, 0))
    tile = pltpu.VMEM((n_heads, bb, SUBLANES, GROUP), F32)
    return pl.pallas_call(
        functools.partial(_delta_sample_kernel, n_heads=n_heads, n_pos=n_pos, bb=bb),
        grid=(nb // bb,),
        in_specs=[act, act, act,
                  pl.BlockSpec((n_pos, bb, d_b), lambda i: (0, i, zb)),
                  pl.BlockSpec((n_pos, bb, LANES), lambda i: (0, i, 0)),
                  vec, vec, vec, st],
        out_specs=[act, st],
        out_shape=[jax.ShapeDtypeStruct((n_pos, nb, d_b), BF16),
                   jax.ShapeDtypeStruct(s0.shape, F32)],
        scratch_shapes=[tile, tile, tile, tile, pltpu.VMEM((n_heads, bb, GROUP), F32)],
        compiler_params=_params(("arbitrary",)),
        name="delta_sample",
    )(q, k, v, h3, ab, alog, dtb, norm_g, s0)


def _out_proj_kernel(ya_ref, yb_ref, w_ref, r_ref, g_ref, b_ref, o_ref, ob_ref, z_ref, *, n_tiles, d_a, alpha):
    i = pl.program_id(0)

    @pl.when(i == 0)
    def _():
        z_ref[...] = jnp.zeros(z_ref.shape, F32)

    def emit_ln():
        y = _layer_norm(z_ref[...], g_ref[...], b_ref[...])
        o_ref[...] = y
        ob_ref[...] = y.astype(ob_ref.dtype)

    @pl.when(i < n_tiles)
    def _():
        emit_ln()
        acc = (jnp.dot(ya_ref[...], w_ref[0:d_a, :], preferred_element_type=F32)
               + jnp.dot(yb_ref[...], w_ref[d_a:, :], preferred_element_type=F32))
        z_ref[...] = alpha * r_ref[...] + acc

    @pl.when(i == n_tiles)
    def _():
        emit_ln()


def _out_proj_ln(ya, yb, w, resid, g, b, alpha, tm):
    m, d_a = ya.shape
    k, d = w.shape
    n_tiles = m // tm
    cur = lambda i: (jnp.minimum(i, n_tiles - 1), 0)
    prev = lambda i: (jnp.maximum(i - 1, 0), 0)
    vec = pl.BlockSpec((1, d), lambda i: (0, 0))
    return pl.pallas_call(
        functools.partial(_out_proj_kernel, n_tiles=n_tiles, d_a=d_a, alpha=alpha),
        grid=(n_tiles + 1,),
        in_specs=[pl.BlockSpec((tm, d_a), cur),
                  pl.BlockSpec((tm, k - d_a), cur),
                  pl.BlockSpec((k, d), lambda i: (0, 0), pipeline_mode=pl.Buffered(1)),
                  pl.BlockSpec((tm, d), cur), vec, vec],
        out_specs=[pl.BlockSpec((tm, d), prev), pl.BlockSpec((tm, d), prev)],
        out_shape=[jax.ShapeDtypeStruct((m, d), F32), jax.ShapeDtypeStruct((m, d), BF16)],
        scratch_shapes=[pltpu.VMEM((tm, d), F32)],
        compiler_params=_params(("arbitrary",)),
        name="out_proj_ln1",
    )(ya, yb, w, resid, g.reshape(1, d), b.reshape(1, d))


def _down_kernel(x_ref, w_ref, r_ref, g_ref, b_ref, o_ref, *rest, n_tiles, nk, alpha, n_pos):
    acc_ref, z_ref = rest[-2:]
    i, kk = pl.program_id(0), pl.program_id(1)

    def emit_ln():
        y = _layer_norm(z_ref[...], g_ref[...], b_ref[...])
        if n_pos:
            nb = y.shape[0] // n_pos
            for t in range(n_pos):
                o_ref[:, t, :] = y[t * nb:(t + 1) * nb]
        else:
            o_ref[...] = y

    @pl.when(jnp.logical_and(i == 0, kk == 0))
    def _():
        z_ref[...] = jnp.zeros(z_ref.shape, F32)

    wb = w_ref[...].astype(BF16)
    if len(rest) == 3:
        rest[0][...] = wb

    @pl.when(kk == 0)
    def _():
        acc_ref[...] = alpha * r_ref[...] + jnp.dot(x_ref[...], wb, preferred_element_type=F32)
        emit_ln()

    @pl.when(jnp.logical_and(jnp.logical_and(kk > 0, kk < nk - 1), i < n_tiles))
    def _():
        acc_ref[...] += jnp.dot(x_ref[...], wb, preferred_element_type=F32)

    @pl.when(jnp.logical_and(kk == nk - 1, i < n_tiles))
    def _():
        z_ref[...] = acc_ref[...] + jnp.dot(x_ref[...], wb, preferred_element_type=F32)


def _down_ln(x, w, resid, g, b, alpha, tm, tk, emit_w=False, n_pos=0):
    m, kdim = x.shape
    d = w.shape[1]
    nk = kdim // tk
    assert nk >= 2
    n_tiles = m // tm
    row = lambda i: jnp.minimum(i, n_tiles - 1)
    kt = lambda i, kk: jnp.where(i == n_tiles, nk - 1, kk)
    vec = pl.BlockSpec((1, d), lambda i, kk: (0, 0))
    wspec = pl.BlockSpec((tk, d), lambda i, kk: (kt(i, kk), 0))
    if n_pos:
        assert n_tiles == 1
        out_specs = [pl.BlockSpec((m // n_pos, n_pos, d), lambda i, kk: (0, 0, 0))]
        out_shape = [jax.ShapeDtypeStruct((m // n_pos, n_pos, d), F32)]
    else:
        out_specs = [pl.BlockSpec((tm, d), lambda i, kk: (jnp.where(kk == 0, jnp.maximum(i - 1, 0), row(i)), 0))]
        out_shape = [jax.ShapeDtypeStruct((m, d), F32)]
    if emit_w:
        out_specs.append(wspec)
        out_shape.append(jax.ShapeDtypeStruct(w.shape, BF16))
    outs = pl.pallas_call(
        functools.partial(_down_kernel, n_tiles=n_tiles, nk=nk, alpha=alpha, n_pos=n_pos),
        grid=(n_tiles + 1, nk),
        in_specs=[pl.BlockSpec((tm, tk), lambda i, kk: (row(i), kt(i, kk))), wspec,
                  pl.BlockSpec((tm, d), lambda i, kk: (row(i), 0)), vec, vec],
        out_specs=out_specs,
        out_shape=out_shape,
        scratch_shapes=[pltpu.VMEM((tm, d), F32), pltpu.VMEM((tm, d), F32)],
        compiler_params=_params(("arbitrary", "arbitrary")),
        name="down_ln2",
    )(x, w, resid, g.reshape(1, d), b.reshape(1, d))
    return tuple(outs) if emit_w else outs[0]


def _ffn_up_kernel(*refs, tm, shift, halo, has_buf, n_sub):
    n_in = 9 if has_buf else 7
    if has_buf:
        x_ref, wg_ref, wv_ref, bg_ref, bv_ref, cwg_ref, cwv_ref, cbg_ref, cbv_ref = refs[:n_in]
        bufs = (bg_ref, bv_ref)
    else:
        x_ref, wg_ref, wv_ref, cwg_ref, cwv_ref, cbg_ref, cbv_ref = refs[:n_in]
        bufs = (None, None)
    act_ref, tg_ref, tv_ref = refs[n_in:n_in + 3]
    wgb_ref, wvb_ref = refs[n_in + 3:n_in + 5]
    xbufs, carry_ref = refs[n_in + 5:-1], refs[-1]

    @pl.when(jnp.logical_and(pl.program_id(1) == 0, pl.program_id(2) == 0))
    def _():
        wgb_ref[...] = wg_ref[...].astype(BF16)
        wvb_ref[...] = wv_ref[...].astype(BF16)

    wsub = act_ref.shape[1] // n_sub
    cols = [slice(c * wsub, (c + 1) * wsub) for c in range(n_sub)]
    first = pl.program_id(2) == 0

    @pl.when(first)
    def _():
        for c in range(n_sub):
            for p in range(2):
                if has_buf:
                    xbufs[2 * c + p][0:halo, :] = bufs[p][:, cols[c]]
                else:
                    xbufs[2 * c + p][0:halo, :] = jnp.zeros((halo, wsub), F32)

    @pl.when(jnp.logical_not(first))
    def _():
        for c in range(n_sub):
            for p in range(2):
                xbufs[2 * c + p][0:halo, :] = carry_ref[p, :, cols[c]]

    x = x_ref[...]
    parts = ((wgb_ref, cwg_ref, cbg_ref, tg_ref), (wvb_ref, cwv_ref, cbv_ref, tv_ref))
    for c in range(n_sub):
        conv = []
        for p, (w_ref, cw_ref, cb_ref, t_ref) in enumerate(parts):
            xbuf_ref = xbufs[2 * c + p]
            xbuf_ref[halo:halo + tm, :] = jnp.dot(x, w_ref[:, cols[c]], preferred_element_type=F32)
            tail = xbuf_ref[tm:tm + halo, :]
            carry_ref[p, :, cols[c]] = tail
            if has_buf:
                for t in range(halo // shift):
                    t_ref[:, t, cols[c]] = tail[t * shift:(t + 1) * shift]
            else:
                t_ref[0, :, cols[c]] = tail
            conv.append(_conv_taps(xbuf_ref, cw_ref, tm, shift, halo, CONV_FFN, cols[c]) + cb_ref[:, cols[c]])
        act_ref[:, cols[c]] = (_silu(conv[0]) * conv[1]).astype(act_ref.dtype)


def _ffn_up(x, w_up, conv_w, conv_b, bufs, n_seq, rows_per_seq, tm, tn, shift):
    m, d = x.shape
    d_ff = w_up.shape[1] // 2
    nj = d_ff // tn
    has_buf = bufs is not None
    halo = (CONV_FFN - 1) * shift if has_buf else SUBLANES
    tiles = rows_per_seq // tm
    wspec = lambda off: pl.BlockSpec((d, tn), lambda j, s, i: (0, off + j))
    cwspec = lambda off: pl.BlockSpec((CONV_FFN, tn), lambda j, s, i: (0, off + j))
    cbspec = lambda off: pl.BlockSpec((1, tn), lambda j, s, i: (0, off + j))
    if has_buf:
        assert n_seq == 1
        tspec = pl.BlockSpec((shift, halo // shift, tn), lambda j, s, i: (0, 0, j))
        tshape = jax.ShapeDtypeStruct((shift, halo // shift, d_ff), F32)
    else:
        tspec = pl.BlockSpec((1, halo, tn), lambda j, s, i: (s, 0, j))
        tshape = jax.ShapeDtypeStruct((n_seq, halo, d_ff), F32)
    specs = [pl.BlockSpec((tm, d), lambda j, s, i: (s * tiles + i, 0)), wspec(0), wspec(nj)]
    args = [x, w_up, w_up]
    if has_buf:
        specs += [pl.BlockSpec((halo, tn), lambda j, s, i, off=off: (s, off + j)) for off in (0, nj)]
        args += [bufs, bufs]
    specs += [cwspec(0), cwspec(nj), cbspec(0), cbspec(nj)]
    cb2 = conv_b.reshape(1, 2 * d_ff)
    args += [conv_w, conv_w, cb2, cb2]
    return pl.pallas_call(
        functools.partial(_ffn_up_kernel, tm=tm, shift=shift, halo=halo, has_buf=has_buf, n_sub=FFN_SUB),
        grid=(nj, n_seq, tiles),
        in_specs=specs,
        out_specs=[pl.BlockSpec((tm, tn), lambda j, s, i: (s * tiles + i, j)), tspec, tspec],
        out_shape=[jax.ShapeDtypeStruct((m, d_ff), BF16), tshape, tshape],
        scratch_shapes=[pltpu.VMEM((d, tn), BF16)] * 2
                       + [pltpu.VMEM((halo + tm, tn // FFN_SUB), F32)] * (2 * FFN_SUB)
                       + [pltpu.VMEM((2, halo, tn), F32)],
        compiler_params=_params(("arbitrary", "arbitrary", "arbitrary")),
        name="ffn_up",
    )(*args)


def _pad_lanes(vec, offset=0):
    out = jnp.zeros((1, LANES), F32)
    return out.at[0, offset:offset + vec.shape[0]].set(vec.astype(F32))


def _layer(x2, lw, alpha, n_seq, rows_per_seq, shift, state, wb, last):
    (w_in, conv_qkv_w, a_log, dt_bias, norm_g, gm_ln_g, gm_ln_b, gm_ws, gm_bs, gm_out_g, w_out,
     ln1_g, ln1_b, w_up, conv_ffn_w, conv_ffn_b, w_down, ln2_g, ln2_b) = lw
    m, d_model = x2.shape
    n_groups = gm_ws.shape[0]
    d_a = n_groups * GROUP
    d_qkv = conv_qkv_w.shape[1]
    d_b = d_qkv // 3
    n_heads = d_b // GROUP
    main_cols = 2 * d_a + d_qkv + d_b
    fresh = state is None
    n_pos = rows_per_seq // shift
    tm = min(m, 1024)

    w_ab = jnp.pad(w_in[main_cols:].astype(BF16), ((0, LANES - 2 * n_heads), (0, 0)))
    if wb is None:
        h, ab, w_in_b = _matmul(x2, w_in, w_ab, main_cols, tm, 1024, emit_w=True)
    else:
        w_in_b = wb[0]
        h, ab = _matmul(x2, w_in_b, w_ab, main_cols, tm, 1024)
    alog, dtb, ng = _pad_lanes(a_log), _pad_lanes(dt_bias), norm_g.reshape(1, GROUP).astype(F32)

    if fresh:
        ya = _chunk_mlp_prompt(h, gm_ln_g, gm_ln_b, gm_ws, gm_bs, gm_out_g)
        v_rows = None
        q, k, v = _qkv_prep(h, conv_qkv_w, None, n_seq, rows_per_seq, 512, shift, 2 * d_a)
        yb, s_new = _delta_prompt(q, k, v, h, 2 * d_a + d_qkv, ab, alog, dtb, ng, n_seq, rows_per_seq, 4)
        ffn_bufs = None
        tm_ffn = 1024
    else:
        s_dn, buf_qkv, buf_ffn = state
        h3 = h.reshape(n_pos, shift, main_cols)
        ya3, v_rows = _chunk_mlp_sample(h3, gm_ln_g, gm_ln_b, gm_ws, gm_bs, gm_out_g)
        ya = ya3.reshape(m, d_a)
        q, k, v = _qkv_prep(h, conv_qkv_w, buf_qkv, n_seq, rows_per_seq, rows_per_seq, shift, 2 * d_a)
        r3 = lambda t: t.reshape(n_pos, shift, t.shape[-1])
        yb3, s_new = _delta_sample(r3(q), r3(k), r3(v), h3, 2 * d_a + d_qkv, r3(ab), alog, dtb, ng, s_dn, 16)
        yb = yb3.reshape(m, d_b)
        ffn_bufs = buf_ffn
        tm_ffn = rows_per_seq

    x1, x1b = _out_proj_ln(ya, yb, w_out, x2, ln1_g, ln1_b, alpha, min(m, 512))
    act, tail_g, tail_v = _ffn_up(x1b, w_up, conv_ffn_w, conv_ffn_b, ffn_bufs, n_seq, rows_per_seq, tm_ffn, 512, shift)
    if wb is None:
        y, w_down_b = _down_ln(act, w_down, x1, ln2_g, ln2_b, alpha, min(m, 512), 512, emit_w=True,
                               n_pos=n_pos if (last and not fresh) else 0)
    else:
        w_down_b = wb[1]
        y = _down_ln(act, w_down_b, x1, ln2_g, ln2_b, alpha, min(m, 512), 1408)
    return y, s_new, h, tail_g, tail_v, v_rows, (w_in_b, w_down_b)


def kernel(x_prompt, x_sample, state_dn, state_conv_qkv, state_conv_ffn, w_in, conv_qkv_w, dn_a_log, dn_dt_bias, dn_norm_g, gm_ln_g, gm_ln_b, gm_ws, gm_bs, gm_out_g, w_out, ln1_g, ln1_b, w_up, conv_ffn_w, conv_ffn_b, w_down, ln2_g, ln2_b):
    depth = w_in.shape[0]
    bp, lp, d_model = x_prompt.shape
    bs, ls, _ = x_sample.shape
    d_qkv = conv_qkv_w.shape[-1]
    d_a = gm_ws.shape[1] * GROUP
    alpha = (2.0 * depth) ** 0.25

    hp = x_prompt.reshape(bp * lp, d_model)
    hs = jnp.transpose(x_sample, (1, 0, 2)).reshape(ls * bs, d_model)
    outs = [[] for _ in range(7)]
    w_out = w_out.astype(BF16)
    w_in = jnp.swapaxes(w_in, 1, 2)
    for l in range(depth):
        lw = tuple(t[l] for t in (w_in, conv_qkv_w, dn_a_log, dn_dt_bias, dn_norm_g, gm_ln_g, gm_ln_b, gm_ws,
                                  gm_bs, gm_out_g, w_out, ln1_g, ln1_b, w_up, conv_ffn_w, conv_ffn_b, w_down,
                                  ln2_g, ln2_b))
        last = l == depth - 1
        bq = jnp.transpose(state_conv_qkv[l], (1, 0, 2)).reshape((CONV_QKV - 1) * bs, d_qkv)
        bf = jnp.transpose(state_conv_ffn[l], (1, 0, 2)).reshape((CONV_FFN - 1) * bs, -1)
        hs, s_s, h_s, tg_s, tv_s, v_s, wb = _layer(hs, lw, alpha, 1, ls * bs, bs, (state_dn[l], bq, bf), None, last)
        h_s3 = h_s.reshape(ls, bs, -1)
        cq_s = jnp.transpose(h_s3[ls - (CONV_QKV - 1):, :, 2 * d_a:2 * d_a + d_qkv], (1, 0, 2))
        cf_s = jnp.concatenate([tg_s, tv_s], axis=-1)
        hp, s_p, h_p, tg_p, tv_p, _, _ = _layer(hp, lw, alpha, bp, lp, 1, None, wb, last)
        h_p3 = h_p.reshape(bp, lp, -1)
        cq_p = h_p3[:, lp - (CONV_QKV - 1):, 2 * d_a:2 * d_a + d_qkv]
        n_tail = tg_p.shape[1]
        cf_p = jnp.concatenate([tg_p, tv_p], axis=-1)[:, n_tail - (CONV_FFN - 1):]
        for acc, val in zip(outs, (s_p, cq_p, cf_p, s_s, cq_s, cf_s, v_s)):
            acc.append(val)
    return (hp.reshape(bp, lp, d_model), hs) + tuple(jnp.stack(o) for o in outs)
```

```python
import functools

import jax
import jax.numpy as jnp
from jax import lax
from jax.experimental import pallas as pl
from jax.experimental.pallas import tpu as pltpu

F32 = jnp.float32
BF16 = jnp.bfloat16

LANES = 128
SUBLANES = 8
VMEM_LIMIT = 56 * 1024 * 1024
DOWN_VMEM_LIMIT = 62 * 1024 * 1024

GROUP = 128
MLP_CHUNK = 128
DN_CHUNK = 64
CONV_QKV = 4
CONV_FFN = 3
FFN_SUB = 1
CMLP_CHUNKS = 2
EPS = 1e-5

_NT = (((1,), (1,)), ((), ()))
_TN = (((0,), (0,)), ((), ()))


def _params(sem, vmem_limit=VMEM_LIMIT):
    return pltpu.CompilerParams(dimension_semantics=sem, vmem_limit_bytes=vmem_limit)


def _bdot(a, b):
    return jnp.dot(a.astype(BF16), b.astype(BF16), preferred_element_type=F32)


def _gelu(x):
    return 0.5 * x * (1.0 + lax.erf(x * (0.5 ** 0.5)))


def _silu(x):
    return x * jax.nn.sigmoid(x)


def _layer_norm(x, g, b):
    mu = jnp.mean(x, axis=-1, keepdims=True)
    xc = x - mu
    var = jnp.mean(xc * xc, axis=-1, keepdims=True)
    return xc * lax.rsqrt(var + EPS) * g + b


def _rms_norm(x, g):
    ms = jnp.mean(x * x, axis=-1, keepdims=True)
    return x * lax.rsqrt(ms + EPS) * g


def _mm_kernel(x_ref, w_ref, we_ref, o_ref, e_ref, *rest):
    xb_ref = rest[-1]

    @pl.when(pl.program_id(1) == 0)
    def _():
        xb = x_ref[...].astype(BF16)
        xb_ref[...] = xb
        e_ref[...] = lax.dot_general(xb, we_ref[...], _NT, preferred_element_type=F32)

    wb = w_ref[...].astype(BF16)
    if len(rest) == 2:
        rest[0][...] = wb
    o_ref[...] = lax.dot_general(xb_ref[...], wb, _NT, preferred_element_type=F32)


def _matmul(x, wt, wt_extra, n_cols, tm, tn, emit_w=False):
    m, k = x.shape
    ne = wt_extra.shape[0]
    out_specs = [pl.BlockSpec((tm, tn), lambda i, j: (i, j)), pl.BlockSpec((tm, ne), lambda i, j: (i, 0))]
    out_shape = [jax.ShapeDtypeStruct((m, n_cols), F32), jax.ShapeDtypeStruct((m, ne), F32)]
    if emit_w:
        out_specs.append(pl.BlockSpec((tn, k), lambda i, j: (j, 0)))
        out_shape.append(jax.ShapeDtypeStruct((n_cols, k), BF16))
    return pl.pallas_call(
        _mm_kernel,
        grid=(m // tm, n_cols // tn),
        in_specs=[pl.BlockSpec((tm, k), lambda i, j: (i, 0)),
                  pl.BlockSpec((tn, k), lambda i, j: (j, 0)),
                  pl.BlockSpec((ne, k), lambda i, j: (0, 0))],
        out_specs=out_specs,
        out_shape=out_shape,
        scratch_shapes=[pltpu.VMEM((tm, k), BF16)],
        compiler_params=_params(("arbitrary", "arbitrary")),
        name="proj",
    )(x, wt, wt_extra)


def _cmlp_prompt_kernel(u_ref, v_ref, lng_ref, lnb_ref, ws_ref, bst_ref, og_ref, ya_ref, *, n_groups, n_chunks):
    row = lax.broadcasted_iota(jnp.int32, (MLP_CHUNK, MLP_CHUNK), 0)
    col = lax.broadcasted_iota(jnp.int32, (MLP_CHUNK, MLP_CHUNK), 1)
    cg = [(c, g) for c in range(n_chunks) for g in range(n_groups)]
    rows = lambda c: slice(c * MLP_CHUNK, (c + 1) * MLP_CHUNK)
    cols = lambda g: slice(g * GROUP, (g + 1) * GROUP)
    w = [jnp.where(row >= col, ws_ref[g], 0.0).astype(BF16) for g in range(n_groups)]
    n = range(len(cg))
    gv = [_gelu(v_ref[rows(c), cols(g)]) for c, g in cg]
    mu = [jnp.mean(gv[i], axis=-1, keepdims=True) for i in n]
    xc = [gv[i] - mu[i] for i in n]
    var = [jnp.mean(xc[i] * xc[i], axis=-1, keepdims=True) for i in n]
    vh = [xc[i] * lax.rsqrt(var[i] + EPS) * lng_ref[:, cols(g)] + lnb_ref[:, cols(g)] for i, (c, g) in enumerate(cg)]
    mixed = [jnp.dot(w[g], vh[i].astype(BF16), preferred_element_type=F32) + bst_ref[:, g:g + 1]
             for i, (c, g) in enumerate(cg)]
    y = [_gelu(u_ref[rows(c), cols(g)]) * mixed[i] for i, (c, g) in enumerate(cg)]
    ms = [jnp.mean(y[i] * y[i], axis=-1, keepdims=True) for i in n]
    for i, (c, g) in enumerate(cg):
        ya_ref[rows(c), cols(g)] = (y[i] * lax.rsqrt(ms[i] + EPS) * og_ref[:, cols(g)]).astype(ya_ref.dtype)


def _chunk_mlp_prompt(h, ln_g, ln_b, ws, bs, out_g):
    m = h.shape[0]
    n_groups, d_a = ws.shape[0], ws.shape[0] * GROUP
    vec = pl.BlockSpec((1, d_a), lambda i: (0, 0))
    tm = CMLP_CHUNKS * MLP_CHUNK
    return pl.pallas_call(
        functools.partial(_cmlp_prompt_kernel, n_groups=n_groups, n_chunks=CMLP_CHUNKS),
        grid=(m // tm,),
        in_specs=[pl.BlockSpec((tm, d_a), lambda i: (i, 0)),
                  pl.BlockSpec((tm, d_a), lambda i: (i, 1)),
                  vec, vec,
                  pl.BlockSpec((n_groups, MLP_CHUNK, MLP_CHUNK), lambda i: (0, 0, 0)),
                  pl.BlockSpec((MLP_CHUNK, n_groups), lambda i: (0, 0)),
                  vec],
        out_specs=pl.BlockSpec((tm, d_a), lambda i: (i, 0)),
        out_shape=jax.ShapeDtypeStruct((m, d_a), BF16),
        compiler_params=_params(("arbitrary",)),
        name="chunk_mlp_prompt",
    )(h, h, ln_g.reshape(1, d_a), ln_b.reshape(1, d_a), ws, bs.T, out_g.reshape(1, d_a))


def _cmlp_sample_kernel(u_ref, v_ref, lng_ref, lnb_ref, wv_ref, bv_ref, og_ref, ya_ref, vr_ref, *, n_pos):
    vh = []
    for t in range(n_pos):
        vt = _layer_norm(_gelu(v_ref[t]), lng_ref[0], lnb_ref[0])
        vr_ref[:, t, :] = vt
        vh.append(vt)
    for t in range(n_pos):
        mixed = wv_ref[t, 0:1, :] * vh[0]
        for s in range(1, t + 1):
            mixed = mixed + wv_ref[t, s:s + 1, :] * vh[s]
        mixed = mixed + bv_ref[0, t:t + 1, :]
        y = _gelu(u_ref[t]) * mixed
        ya_ref[t] = _rms_norm(y, og_ref[0]).astype(ya_ref.dtype)


def _chunk_mlp_sample(h3, ln_g, ln_b, ws, bs, out_g):
    n_pos, nb, _ = h3.shape
    n_groups = ws.shape[0]
    d_a = n_groups * GROUP
    wv = jnp.repeat(jnp.transpose(ws[:, :n_pos, :n_pos], (1, 2, 0)), GROUP, axis=-1)
    bv = jnp.repeat(bs[:, :n_pos].T, GROUP, axis=-1).reshape(1, n_pos, d_a)
    vec = pl.BlockSpec((1, 1, GROUP), lambda g: (0, 0, g))
    act = pl.BlockSpec((n_pos, nb, GROUP), lambda g: (0, 0, g))
    return pl.pallas_call(
        functools.partial(_cmlp_sample_kernel, n_pos=n_pos),
        grid=(n_groups,),
        in_specs=[act,
                  pl.BlockSpec((n_pos, nb, GROUP), lambda g: (0, 0, n_groups + g)),
                  vec, vec,
                  pl.BlockSpec((n_pos, n_pos, GROUP), lambda g: (0, 0, g)),
                  pl.BlockSpec((1, n_pos, GROUP), lambda g: (0, 0, g)),
                  vec],
        out_specs=[act, pl.BlockSpec((nb, n_pos, GROUP), lambda g: (0, 0, g))],
        out_shape=[jax.ShapeDtypeStruct((n_pos, nb, d_a), BF16),
                   jax.ShapeDtypeStruct((nb, n_pos, d_a), F32)],
        compiler_params=_params(("arbitrary",)),
        name="chunk_mlp_sample",
    )(h3, h3, ln_g.reshape(1, 1, d_a), ln_b.reshape(1, 1, d_a), wv, bv, out_g.reshape(1, 1, d_a))


def _conv_taps(xbuf_ref, w_ref, tm, shift, halo, width, cols=slice(None)):
    acc = None
    for j in range(width):
        off = halo - (width - 1 - j) * shift
        term = w_ref[j:j + 1, cols] * xbuf_ref[off:off + tm, :]
        acc = term if acc is None else acc + term
    return acc


def _qkv_prep_kernel(*refs, tm, shift, halo, n_heads, has_buf):
    if has_buf:
        (xq_ref, xk_ref, xv_ref, bq_ref, bk_ref, bv_ref, wq_ref, wk_ref, wv_ref,
         q_ref, k_ref, v_ref, xbuf_ref, carry_ref) = refs
        bufs = (bq_ref, bk_ref, bv_ref)
    else:
        (xq_ref, xk_ref, xv_ref, wq_ref, wk_ref, wv_ref,
         q_ref, k_ref, v_ref, xbuf_ref, carry_ref) = refs
        bufs = (None, None, None)
    first = pl.program_id(1) == 0
    parts = ((xq_ref, wq_ref, q_ref, GROUP ** -0.5), (xk_ref, wk_ref, k_ref, 1.0), (xv_ref, wv_ref, v_ref, None))
    for p, (x_ref, w_ref, o_ref, scale) in enumerate(parts):
        @pl.when(first)
        def _(p=p):
            if has_buf:
                xbuf_ref[0:halo, :] = bufs[p][...]
            else:
                xbuf_ref[0:halo, :] = jnp.zeros((halo, xbuf_ref.shape[1]), F32)

        @pl.when(jnp.logical_not(first))
        def _(p=p):
            xbuf_ref[0:halo, :] = carry_ref[p]

        xbuf_ref[halo:halo + tm, :] = x_ref[...]
        carry_ref[p] = xbuf_ref[tm:tm + halo, :]
        y = _silu(_conv_taps(xbuf_ref, w_ref, tm, shift, halo, CONV_QKV))
        if scale is None:
            o_ref[...] = y
        else:
            for h in range(n_heads):
                sl = slice(h * GROUP, (h + 1) * GROUP)
                yh = y[:, sl]
                yn = yh * lax.rsqrt(jnp.sum(yh * yh, axis=-1, keepdims=True) + 1e-6)
                o_ref[:, sl] = yn * scale if scale != 1.0 else yn


def _qkv_prep(h, conv_w, bufs, n_seq, rows_per_seq, tm, shift, col0):
    d_b = conv_w.shape[1] // 3
    n_heads = d_b // GROUP
    has_buf = bufs is not None
    halo = (CONV_QKV - 1) * shift if has_buf else SUBLANES
    tiles = rows_per_seq // tm
    cb = col0 // d_b
    x_specs = [pl.BlockSpec((tm, d_b), lambda s, i, c=c: (s * tiles + i, cb + c)) for c in range(3)]
    w_specs = [pl.BlockSpec((CONV_QKV, d_b), lambda s, i, c=c: (0, c)) for c in range(3)]
    o_spec = pl.BlockSpec((tm, d_b), lambda s, i: (s * tiles + i, 0))
    args, specs = [h, h, h], list(x_specs)
    if has_buf:
        specs += [pl.BlockSpec((halo, d_b), lambda s, i, c=c: (s, c)) for c in range(3)]
        args += [bufs, bufs, bufs]
    specs += w_specs
    args += [conv_w, conv_w, conv_w]
    m = n_seq * rows_per_seq
    return pl.pallas_call(
        functools.partial(_qkv_prep_kernel, tm=tm, shift=shift, halo=halo, n_heads=n_heads, has_buf=has_buf),
        grid=(n_seq, tiles),
        in_specs=specs,
        out_specs=[o_spec, o_spec, o_spec],
        out_shape=[jax.ShapeDtypeStruct((m, d_b), F32)] * 3,
        scratch_shapes=[pltpu.VMEM((halo + tm, d_b), F32), pltpu.VMEM((3, halo, d_b), F32)],
        compiler_params=_params(("arbitrary", "arbitrary")),
        name="qkv_prep",
    )(*args)


def _gates(ab, alog, dtb):
    g = -jnp.exp(alog) * jax.nn.softplus(ab + dtb)
    beta = jax.nn.sigmoid(ab)
    return g, beta


def _delta_prompt_kernel(q_ref, k_ref, v_ref, z_ref, ab_ref, alog_ref, dtb_ref, ng_ref,
                         yb_ref, sout_ref, s_ref, *, n_heads, n_chunks, ns):
    c = DN_CHUNK
    ci = pl.program_id(1)

    @pl.when(ci == 0)
    def _():
        s_ref[...] = jnp.zeros(s_ref.shape, F32)

    row = lax.broadcasted_iota(jnp.int32, (c, c), 0)
    col = lax.broadcasted_iota(jnp.int32, (c, c), 1)
    causal = row >= col
    strict = row > col
    eye = (row == col).astype(F32)
    tril = causal.astype(F32)
    n_double = (c - 1).bit_length() - 1
    ch = [(s, h) for s in range(ns) for h in range(n_heads)]
    sl = lambda h: slice(h * GROUP, (h + 1) * GROUP)
    gates = [_gates(ab_ref[s], alog_ref[...], dtb_ref[...]) for s in range(ns)]
    gc_s = [jnp.dot(tril, gates[s][0], precision=lax.Precision.HIGHEST, preferred_element_type=F32)
            for s in range(ns)]
    gct_s = [g.T for g in gc_s]
    q = [q_ref[s, :, sl(h)] for s, h in ch]
    k = [k_ref[s, :, sl(h)] for s, h in ch]
    gcol = [gc_s[s][:, h:h + 1] for s, h in ch]
    beta = [gates[s][1][:, n_heads + h:n_heads + h + 1] for s, h in ch]
    n = range(len(ch))
    decay = [jnp.exp(jnp.where(causal, gcol[i] - gct_s[s][h:h + 1, :], -jnp.inf)) for i, (s, h) in enumerate(ch)]
    kb = [k[i] * beta[i] for i in n]
    kbf = [k[i].astype(BF16) for i in n]
    kq = [lax.dot_general(jnp.concatenate([kb[i], q[i]], axis=0).astype(BF16), kbf[i], _NT,
                          preferred_element_type=F32) for i in n]
    qk = [kq[i][c:] * decay[i] for i in n]
    y = [jnp.where(strict, -(kq[i][:c] * decay[i]), 0.0) for i in n]
    x = [eye + y[i] for i in n]
    y = [_bdot(y[i], y[i]) for i in n]
    for _ in range(n_double - 1):
        t = [_bdot(jnp.concatenate([x[i], y[i]], axis=0), y[i]) for i in n]
        x = [x[i] + t[i][:c] for i in n]
        y = [t[i][c:] for i in n]
    x = [x[i] + _bdot(x[i], y[i]) for i in n]
    egc = [jnp.exp(gcol[i]) for i in n]
    sol = [_bdot(x[i], jnp.concatenate([v_ref[s, :, sl(h)] * beta[i], kb[i] * egc[i]], axis=1))
           for i, (s, h) in enumerate(ch)]
    s_old = [s_ref[s, h] for s, h in ch]
    r = [_bdot(jnp.concatenate([sol[i][:, GROUP:], q[i] * egc[i]], axis=0), s_old[i]) for i in n]
    v_new = [sol[i][:, :GROUP] - r[i][:c] for i in n]
    o = [r[i][c:] + _bdot(qk[i], v_new[i]) for i in n]
    gl = [gc_s[s][c - 1:c, h:h + 1] for s, h in ch]
    upd = [lax.dot_general((k[i] * jnp.exp(gl[i] - gcol[i])).astype(BF16), v_new[i].astype(BF16), _TN,
                           preferred_element_type=F32) for i in n]
    ms = [jnp.mean(o[i] * o[i], axis=-1, keepdims=True) for i in n]
    for i, (s, h) in enumerate(ch):
        s_ref[s, h] = s_old[i] * jnp.exp(gl[i]) + upd[i]
        on = o[i] * lax.rsqrt(ms[i] + EPS) * ng_ref[...]
        yb_ref[s, :, sl(h)] = (on * _silu(z_ref[s, :, sl(h)])).astype(yb_ref.dtype)

    @pl.when(ci == n_chunks - 1)
    def _():
        sout_ref[...] = s_ref[...]


def _delta_prompt(q, k, v, h, z_col0, ab, alog, dtb, norm_g, n_seq, rows_per_seq, ns):
    d_b = q.shape[1]
    n_heads = d_b // GROUP
    n_chunks = rows_per_seq // DN_CHUNK
    r3 = lambda t: t.reshape(n_seq, rows_per_seq, t.shape[-1])
    act = pl.BlockSpec((ns, DN_CHUNK, d_b), lambda s, i: (s, i, 0))
    vec = pl.BlockSpec((1, LANES), lambda s, i: (0, 0))
    zb = z_col0 // d_b
    yb, s_new = pl.pallas_call(
        functools.partial(_delta_prompt_kernel, n_heads=n_heads, n_chunks=n_chunks, ns=ns),
        grid=(n_seq // ns, n_chunks),
        in_specs=[act, act, act,
                  pl.BlockSpec((ns, DN_CHUNK, d_b), lambda s, i: (s, i, zb)),
                  pl.BlockSpec((ns, DN_CHUNK, LANES), lambda s, i: (s, i, 0)),
                  vec, vec, vec],
        out_specs=[act, pl.BlockSpec((ns, n_heads, GROUP, GROUP), lambda s, i: (s, 0, 0, 0))],
        out_shape=[jax.ShapeDtypeStruct((n_seq, rows_per_seq, d_b), BF16),
                   jax.ShapeDtypeStruct((n_seq, n_heads, GROUP, GROUP), F32)],
        scratch_shapes=[pltpu.VMEM((ns, n_heads, GROUP, GROUP), F32)],
        compiler_params=_params(("arbitrary", "arbitrary")),
        name="delta_prompt",
    )(r3(q), r3(k), r3(v), r3(h), r3(ab), alog, dtb, norm_g)
    return yb.reshape(n_seq * rows_per_seq, d_b), s_new


def _delta_sample_kernel(q_ref, k_ref, v_ref, z_ref, ab_ref, alog_ref, dtb_ref, ng_ref, s0_ref,
                         yb_ref, s1_ref, lhs_ref, res_ref, kt_ref, vn_ref, gl_ref, *, n_heads, n_pos, bb):
    nt = n_pos
    hs = range(n_heads)
    sls = [slice(h * GROUP, (h + 1) * GROUP) for h in hs]
    gates = [_gates(ab_ref[t], alog_ref[...], dtb_ref[...]) for t in range(nt)]
    zeros_pad = jnp.zeros((bb, GROUP), F32)
    ts = range(nt)
    q = [[q_ref[t, :, sls[h]] for t in ts] for h in hs]
    k = [[k_ref[t, :, sls[h]] for t in ts] for h in hs]
    beta = [[gates[t][1][:, n_heads + h:n_heads + h + 1] for t in ts] for h in hs]
    gc = []
    for h in hs:
        g = [gates[t][0][:, h:h + 1] for t in ts]
        for t in range(1, nt):
            g[t] = g[t - 1] + g[t]
        gc.append(g)
    kb = [[k[h][t] * beta[h][t] for t in ts] for h in hs]
    kk = [[[jnp.sum(kb[h][t] * k[h][s], axis=-1, keepdims=True) for s in range(t)] for t in ts] for h in hs]
    qk = [[[jnp.sum(q[h][t] * k[h][s], axis=-1, keepdims=True) for s in range(t + 1)] for t in ts] for h in hs]
    dec = [[[jnp.exp(gc[h][t] - gc[h][s]) for s in range(t + 1)] for t in ts] for h in hs]
    a = [[[kk[h][t][s] * dec[h][t][s] for s in range(t)] for t in ts] for h in hs]
    qk = [[[qk[h][t][s] * dec[h][t][s] for s in range(t + 1)] for t in ts] for h in hs]
    egc = [[jnp.exp(gc[h][t]) for t in ts] for h in hs]
    u_blk = [[v_ref[t, :, sls[h]] * beta[h][t] for t in ts] for h in hs]
    w_blk = [[kb[h][t] * egc[h][t] for t in ts] for h in hs]
    for t in ts:
        for h in hs:
            for s in range(t):
                u_blk[h][t] = u_blk[h][t] - a[h][t][s] * u_blk[h][s]
                w_blk[h][t] = w_blk[h][t] - a[h][t][s] * w_blk[h][s]
    for h in hs:
        for t in ts:
            lhs_ref[h, :, t, :] = w_blk[h][t]
            lhs_ref[h, :, nt + t, :] = q[h][t] * egc[h][t]
        gl_ref[h] = jnp.broadcast_to(egc[h][nt - 1], (bb, GROUP))

    for h in hs:
        for b in range(bb):
            res_ref[h, b] = _bdot(lhs_ref[h, b], s0_ref[b, h])

    v_new = [[u_blk[h][t] - res_ref[h, :, t, :] for t in ts] for h in hs]
    o = [[res_ref[h, :, nt + t, :] for t in ts] for h in hs]
    for h in hs:
        for t in ts:
            for s in range(t + 1):
                o[h][t] = o[h][t] + qk[h][t][s] * v_new[h][s]
    ms = [[jnp.mean(o[h][t] * o[h][t], axis=-1, keepdims=True) for t in ts] for h in hs]
    for h in hs:
        gl = gc[h][nt - 1]
        for t in ts:
            on = o[h][t] * lax.rsqrt(ms[h][t] + EPS) * ng_ref[...]
            yb_ref[t, :, sls[h]] = (on * _silu(z_ref[t, :, sls[h]])).astype(yb_ref.dtype)
            kt_ref[h, :, t, :] = k[h][t] * jnp.exp(gl - gc[h][t])
            vn_ref[h, :, t, :] = v_new[h][t]
            kt_ref[h, :, nt + t, :] = zeros_pad
            vn_ref[h, :, nt + t, :] = zeros_pad

    for h in hs:
        for b in range(bb):
            upd = lax.dot_general(kt_ref[h, b].astype(BF16), vn_ref[h, b].astype(BF16), _TN,
                                  preferred_element_type=F32)
            s1_ref[b, h] = s0_ref[b, h] * gl_ref[h, b:b + 1, :] + upd


def _delta_sample(q, k, v, h3, z_col0, ab, alog, dtb, norm_g, s0, bb):
    n_pos, nb, d_b = q.shape
    n_heads = d_b // GROUP
    assert 2 * n_pos == SUBLANES, "one state read packs [w | q] rows of a sequence into one 8-row tile"
    zb = z_col0 // d_b
    act = pl.BlockSpec((n_pos, bb, d_b), lambda i: (0, i, 0))
    vec = pl.BlockSpec((1, LANES), lambda i: (0, 0))
    st = pl.BlockSpec((bb, n_heads, GROUP, GROUP), lambda i: (i, 0, 0, 0))
    tile = pltpu.VMEM((n_heads, bb, SUBLANES, GROUP), F32)
    return pl.pallas_call(
        functools.partial(_delta_sample_kernel, n_heads=n_heads, n_pos=n_pos, bb=bb),
        grid=(nb // bb,),
        in_specs=[act, act, act,
                  pl.BlockSpec((n_pos, bb, d_b), lambda i: (0, i, zb)),
                  pl.BlockSpec((n_pos, bb, LANES), lambda i: (0, i, 0)),
                  vec, vec, vec, st],
        out_specs=[act, st],
        out_shape=[jax.ShapeDtypeStruct((n_pos, nb, d_b), BF16),
                   jax.ShapeDtypeStruct(s0.shape, F32)],
        scratch_shapes=[tile, tile, tile, tile, pltpu.VMEM((n_heads, bb, GROUP), F32)],
        compiler_params=_params(("arbitrary",)),
        name="delta_sample",
    )(q, k, v, h3, ab, alog, dtb, norm_g, s0)


def _out_proj_kernel(ya_ref, yb_ref, w_ref, r_ref, g_ref, b_ref, o_ref, ob_ref, z_ref, *, n_tiles, d_a, alpha):
    i = pl.program_id(0)

    @pl.when(i == 0)
    def _():
        z_ref[...] = jnp.zeros(z_ref.shape, F32)

    def emit_ln():
        y = _layer_norm(z_ref[...], g_ref[...], b_ref[...])
        o_ref[...] = y
        ob_ref[...] = y.astype(ob_ref.dtype)

    @pl.when(i < n_tiles)
    def _():
        emit_ln()
        acc = (jnp.dot(ya_ref[...], w_ref[0:d_a, :], preferred_element_type=F32)
               + jnp.dot(yb_ref[...], w_ref[d_a:, :], preferred_element_type=F32))
        z_ref[...] = alpha * r_ref[...] + acc

    @pl.when(i == n_tiles)
    def _():
        emit_ln()


def _out_proj_ln(ya, yb, w, resid, g, b, alpha, tm):
    m, d_a = ya.shape
    k, d = w.shape
    n_tiles = m // tm
    cur = lambda i: (jnp.minimum(i, n_tiles - 1), 0)
    prev = lambda i: (jnp.maximum(i - 1, 0), 0)
    vec = pl.BlockSpec((1, d), lambda i: (0, 0))
    return pl.pallas_call(
        functools.partial(_out_proj_kernel, n_tiles=n_tiles, d_a=d_a, alpha=alpha),
        grid=(n_tiles + 1,),
        in_specs=[pl.BlockSpec((tm, d_a), cur),
                  pl.BlockSpec((tm, k - d_a), cur),
                  pl.BlockSpec((k, d), lambda i: (0, 0), pipeline_mode=pl.Buffered(1)),
                  pl.BlockSpec((tm, d), cur), vec, vec],
        out_specs=[pl.BlockSpec((tm, d), prev), pl.BlockSpec((tm, d), prev)],
        out_shape=[jax.ShapeDtypeStruct((m, d), F32), jax.ShapeDtypeStruct((m, d), BF16)],
        scratch_shapes=[pltpu.VMEM((tm, d), F32)],
        compiler_params=_params(("arbitrary",)),
        name="out_proj_ln1",
    )(ya, yb, w, resid, g.reshape(1, d), b.reshape(1, d))


def _down_kernel(x_ref, w_ref, r_ref, g_ref, b_ref, o_ref, *rest, n_tiles, nk, nc, alpha, n_pos):
    z_ref = rest[-1]
    cw = z_ref.shape[-1]
    slab = lambda c: slice(c * cw, (c + 1) * cw)
    i, kk = pl.program_id(0), pl.program_id(1)

    def emit_ln():
        z = [z_ref[c] for c in range(nc)]
        inv_d = 1.0 / (nc * cw)
        mu = sum(jnp.sum(zc, axis=-1, keepdims=True) for zc in z) * inv_d
        xc = [zc - mu for zc in z]
        var = sum(jnp.sum(c * c, axis=-1, keepdims=True) for c in xc) * inv_d
        rs = lax.rsqrt(var + EPS)
        for c in range(nc):
            y = xc[c] * rs * g_ref[:, slab(c)] + b_ref[:, slab(c)]
            if n_pos:
                nb = y.shape[0] // n_pos
                for t in range(n_pos):
                    o_ref[:, t, slab(c)] = y[t * nb:(t + 1) * nb]
            else:
                o_ref[:, slab(c)] = y

    @pl.when(jnp.logical_and(i == 0, kk == 0))
    def _():
        z_ref[...] = jnp.zeros(z_ref.shape, F32)

    wb = w_ref[...].astype(BF16)
    if len(rest) == 2:
        rest[0][...] = wb

    @pl.when(kk == 0)
    def _():
        emit_ln()
        part = jnp.dot(x_ref[...], wb, preferred_element_type=F32)
        for c in range(nc):
            if nc == nk:
                z_ref[c] = part[:, slab(c)] + alpha * r_ref[...] if c == 0 else part[:, slab(c)]
            else:
                z_ref[c] = part[:, slab(c)] + alpha * r_ref[:, slab(c)]

    @pl.when(jnp.logical_and(kk > 0, i < n_tiles))
    def _():
        part = jnp.dot(x_ref[...], wb, preferred_element_type=F32)
        for c in range(nc):
            z_ref[c] += part[:, slab(c)]
        if nc == nk:
            z_ref[kk] += alpha * r_ref[...]


def _down_ln(x, w, resid, g, b, alpha, tm, tk, emit_w=False, n_pos=0):
    m, kdim = x.shape
    d = w.shape[1]
    nk = kdim // tk
    n_tiles = m // tm
    nc = nk if (d % nk == 0 and (d // nk) % LANES == 0) else 1
    row = lambda i: jnp.minimum(i, n_tiles - 1)
    kt = lambda i, kk: jnp.where(i == n_tiles, nk - 1, kk)
    vec = pl.BlockSpec((1, d), lambda i, kk: (0, 0))
    wspec = pl.BlockSpec((tk, d), lambda i, kk: (kt(i, kk), 0))
    if nc == nk:
        rspec = pl.BlockSpec((tm, d // nc), lambda i, kk: (row(i), kt(i, kk)))
    else:
        rspec = pl.BlockSpec((tm, d), lambda i, kk: (row(i), 0))
    if n_pos:
        assert n_tiles == 1
        out_specs = [pl.BlockSpec((m // n_pos, n_pos, d), lambda i, kk: (0, 0, 0))]
        out_shape = [jax.ShapeDtypeStruct((m // n_pos, n_pos, d), F32)]
    else:
        out_specs = [pl.BlockSpec((tm, d), lambda i, kk: (jnp.where(kk == 0, jnp.maximum(i - 1, 0), row(i)), 0))]
        out_shape = [jax.ShapeDtypeStruct((m, d), F32)]
    if emit_w:
        out_specs.append(wspec)
        out_shape.append(jax.ShapeDtypeStruct(w.shape, BF16))
    outs = pl.pallas_call(
        functools.partial(_down_kernel, n_tiles=n_tiles, nk=nk, nc=nc, alpha=alpha, n_pos=n_pos),
        grid=(n_tiles + 1, nk),
        in_specs=[pl.BlockSpec((tm, tk), lambda i, kk: (row(i), kt(i, kk))), wspec, rspec, vec, vec],
        out_specs=out_specs,
        out_shape=out_shape,
        scratch_shapes=[pltpu.VMEM((nc, tm, d // nc), F32)],
        compiler_params=_params(("arbitrary", "arbitrary"), DOWN_VMEM_LIMIT),
        name="down_ln2",
    )(x, w, resid, g.reshape(1, d), b.reshape(1, d))
    return tuple(outs) if emit_w else outs[0]


def _ffn_up_kernel(*refs, tm, shift, halo, has_buf, n_sub):
    n_in = 9 if has_buf else 7
    if has_buf:
        x_ref, wg_ref, wv_ref, bg_ref, bv_ref, cwg_ref, cwv_ref, cbg_ref, cbv_ref = refs[:n_in]
        bufs = (bg_ref, bv_ref)
    else:
        x_ref, wg_ref, wv_ref, cwg_ref, cwv_ref, cbg_ref, cbv_ref = refs[:n_in]
        bufs = (None, None)
    act_ref, tg_ref, tv_ref = refs[n_in:n_in + 3]
    wgb_ref, wvb_ref = refs[n_in + 3:n_in + 5]
    xbufs, carry_ref = refs[n_in + 5:-1], refs[-1]

    @pl.when(jnp.logical_and(pl.program_id(1) == 0, pl.program_id(2) == 0))
    def _():
        wgb_ref[...] = wg_ref[...].astype(BF16)
        wvb_ref[...] = wv_ref[...].astype(BF16)

    wsub = act_ref.shape[1] // n_sub
    cols = [slice(c * wsub, (c + 1) * wsub) for c in range(n_sub)]
    first = pl.program_id(2) == 0

    @pl.when(first)
    def _():
        for c in range(n_sub):
            for p in range(2):
                if has_buf:
                    xbufs[2 * c + p][0:halo, :] = bufs[p][:, cols[c]]
                else:
                    xbufs[2 * c + p][0:halo, :] = jnp.zeros((halo, wsub), F32)

    @pl.when(jnp.logical_not(first))
    def _():
        for c in range(n_sub):
            for p in range(2):
                xbufs[2 * c + p][0:halo, :] = carry_ref[p, :, cols[c]]

    x = x_ref[...]
    parts = ((wgb_ref, cwg_ref, cbg_ref, tg_ref), (wvb_ref, cwv_ref, cbv_ref, tv_ref))
    for c in range(n_sub):
        conv = []
        for p, (w_ref, cw_ref, cb_ref, t_ref) in enumerate(parts):
            xbuf_ref = xbufs[2 * c + p]
            xbuf_ref[halo:halo + tm, :] = jnp.dot(x, w_ref[:, cols[c]], preferred_element_type=F32)
            tail = xbuf_ref[tm:tm + halo, :]
            carry_ref[p, :, cols[c]] = tail
            if has_buf:
                for t in range(halo // shift):
                    t_ref[:, t, cols[c]] = tail[t * shift:(t + 1) * shift]
            else:
                t_ref[0, :, cols[c]] = tail
            conv.append(_conv_taps(xbuf_ref, cw_ref, tm, shift, halo, CONV_FFN, cols[c]) + cb_ref[:, cols[c]])
        act_ref[:, cols[c]] = (_silu(conv[0]) * conv[1]).astype(act_ref.dtype)


def _ffn_up(x, w_up, conv_w, conv_b, bufs, n_seq, rows_per_seq, tm, tn, shift):
    m, d = x.shape
    d_ff = w_up.shape[1] // 2
    nj = d_ff // tn
    has_buf = bufs is not None
    halo = (CONV_FFN - 1) * shift if has_buf else SUBLANES
    tiles = rows_per_seq // tm
    wspec = lambda off: pl.BlockSpec((d, tn), lambda j, s, i: (0, off + j))
    cwspec = lambda off: pl.BlockSpec((CONV_FFN, tn), lambda j, s, i: (0, off + j))
    cbspec = lambda off: pl.BlockSpec((1, tn), lambda j, s, i: (0, off + j))
    if has_buf:
        assert n_seq == 1
        tspec = pl.BlockSpec((shift, halo // shift, tn), lambda j, s, i: (0, 0, j))
        tshape = jax.ShapeDtypeStruct((shift, halo // shift, d_ff), F32)
    else:
        tspec = pl.BlockSpec((1, halo, tn), lambda j, s, i: (s, 0, j))
        tshape = jax.ShapeDtypeStruct((n_seq, halo, d_ff), F32)
    specs = [pl.BlockSpec((tm, d), lambda j, s, i: (s * tiles + i, 0)), wspec(0), wspec(nj)]
    args = [x, w_up, w_up]
    if has_buf:
        specs += [pl.BlockSpec((halo, tn), lambda j, s, i, off=off: (s, off + j)) for off in (0, nj)]
        args += [bufs, bufs]
    specs += [cwspec(0), cwspec(nj), cbspec(0), cbspec(nj)]
    cb2 = conv_b.reshape(1, 2 * d_ff)
    args += [conv_w, conv_w, cb2, cb2]
    return pl.pallas_call(
        functools.partial(_ffn_up_kernel, tm=tm, shift=shift, halo=halo, has_buf=has_buf, n_sub=FFN_SUB),
        grid=(nj, n_seq, tiles),
        in_specs=specs,
        out_specs=[pl.BlockSpec((tm, tn), lambda j, s, i: (s * tiles + i, j)), tspec, tspec],
        out_shape=[jax.ShapeDtypeStruct((m, d_ff), BF16), tshape, tshape],
        scratch_shapes=[pltpu.VMEM((d, tn), BF16)] * 2
                       + [pltpu.VMEM((halo + tm, tn // FFN_SUB), F32)] * (2 * FFN_SUB)
                       + [pltpu.VMEM((2, halo, tn), F32)],
        compiler_params=_params(("arbitrary", "arbitrary", "arbitrary")),
        name="ffn_up",
    )(*args)


def _pad_lanes(vec, offset=0):
    out = jnp.zeros((1, LANES), F32)
    return out.at[0, offset:offset + vec.shape[0]].set(vec.astype(F32))


def _layer(x2, lw, alpha, n_seq, rows_per_seq, shift, state, wb, last):
    (w_in, conv_qkv_w, a_log, dt_bias, norm_g, gm_ln_g, gm_ln_b, gm_ws, gm_bs, gm_out_g, w_out,
     ln1_g, ln1_b, w_up, conv_ffn_w, conv_ffn_b, w_down, ln2_g, ln2_b) = lw
    m, d_model = x2.shape
    n_groups = gm_ws.shape[0]
    d_a = n_groups * GROUP
    d_qkv = conv_qkv_w.shape[1]
    d_b = d_qkv // 3
    n_heads = d_b // GROUP
    main_cols = 2 * d_a + d_qkv + d_b
    fresh = state is None
    n_pos = rows_per_seq // shift
    tm = min(m, 1024)

    w_ab = jnp.pad(w_in[main_cols:].astype(BF16), ((0, LANES - 2 * n_heads), (0, 0)))
    if wb is None:
        h, ab, w_in_b = _matmul(x2, w_in, w_ab, main_cols, tm, 1024, emit_w=True)
    else:
        w_in_b = wb[0]
        h, ab = _matmul(x2, w_in_b, w_ab, main_cols, tm, 1024)
    alog, dtb, ng = _pad_lanes(a_log), _pad_lanes(dt_bias), norm_g.reshape(1, GROUP).astype(F32)

    if fresh:
        ya = _chunk_mlp_prompt(h, gm_ln_g, gm_ln_b, gm_ws, gm_bs, gm_out_g)
        v_rows = None
        q, k, v = _qkv_prep(h, conv_qkv_w, None, n_seq, rows_per_seq, 512, shift, 2 * d_a)
        yb, s_new = _delta_prompt(q, k, v, h, 2 * d_a + d_qkv, ab, alog, dtb, ng, n_seq, rows_per_seq, 4)
        ffn_bufs = None
        tm_ffn = 1024
    else:
        s_dn, buf_qkv, buf_ffn = state
        h3 = h.reshape(n_pos, shift, main_cols)
        ya3, v_rows = _chunk_mlp_sample(h3, gm_ln_g, gm_ln_b, gm_ws, gm_bs, gm_out_g)
        ya = ya3.reshape(m, d_a)
        q, k, v = _qkv_prep(h, conv_qkv_w, buf_qkv, n_seq, rows_per_seq, rows_per_seq, shift, 2 * d_a)
        r3 = lambda t: t.reshape(n_pos, shift, t.shape[-1])
        yb3, s_new = _delta_sample(r3(q), r3(k), r3(v), h3, 2 * d_a + d_qkv, r3(ab), alog, dtb, ng, s_dn, 16)
        yb = yb3.reshape(m, d_b)
        ffn_bufs = buf_ffn
        tm_ffn = rows_per_seq

    x1, x1b = _out_proj_ln(ya, yb, w_out, x2, ln1_g, ln1_b, alpha, min(m, 512))
    act, tail_g, tail_v = _ffn_up(x1b, w_up, conv_ffn_w, conv_ffn_b, ffn_bufs, n_seq, rows_per_seq, tm_ffn, 512, shift)
    if wb is None:
        y, w_down_b = _down_ln(act, w_down, x1, ln2_g, ln2_b, alpha, min(m, 512), 512, emit_w=True,
                               n_pos=n_pos if (last and not fresh) else 0)
    else:
        w_down_b = wb[1]
        y = _down_ln(act, w_down_b, x1, ln2_g, ln2_b, alpha, min(m, 1024), 1408)
    return y, s_new, h, tail_g, tail_v, v_rows, (w_in_b, w_down_b)


def kernel(x_prompt, x_sample, state_dn, state_conv_qkv, state_conv_ffn, w_in, conv_qkv_w, dn_a_log, dn_dt_bias, dn_norm_g, gm_ln_g, gm_ln_b, gm_ws, gm_bs, gm_out_g, w_out, ln1_g, ln1_b, w_up, conv_ffn_w, conv_ffn_b, w_down, ln2_g, ln2_b):
    depth = w_in.shape[0]
    bp, lp, d_model = x_prompt.shape
    bs, ls, _ = x_sample.shape
    d_qkv = conv_qkv_w.shape[-1]
    d_a = gm_ws.shape[1] * GROUP
    alpha = (2.0 * depth) ** 0.25

    hp = x_prompt.reshape(bp * lp, d_model)
    hs = jnp.transpose(x_sample, (1, 0, 2)).reshape(ls * bs, d_model)
    outs = [[] for _ in range(7)]
    w_out = w_out.astype(BF16)
    w_in = jnp.swapaxes(w_in, 1, 2)
    for l in range(depth):
        lw = tuple(t[l] for t in (w_in, conv_qkv_w, dn_a_log, dn_dt_bias, dn_norm_g, gm_ln_g, gm_ln_b, gm_ws,
                                  gm_bs, gm_out_g, w_out, ln1_g, ln1_b, w_up, conv_ffn_w, conv_ffn_b, w_down,
                                  ln2_g, ln2_b))
        last = l == depth - 1
        bq = jnp.transpose(state_conv_qkv[l], (1, 0, 2)).reshape((CONV_QKV - 1) * bs, d_qkv)
        bf = jnp.transpose(state_conv_ffn[l], (1, 0, 2)).reshape((CONV_FFN - 1) * bs, -1)
        hs, s_s, h_s, tg_s, tv_s, v_s, wb = _layer(hs, lw, alpha, 1, ls * bs, bs, (state_dn[l], bq, bf), None, last)
        h_s3 = h_s.reshape(ls, bs, -1)
        cq_s = jnp.transpose(h_s3[ls - (CONV_QKV - 1):, :, 2 * d_a:2 * d_a + d_qkv], (1, 0, 2))
        cf_s = jnp.concatenate([tg_s, tv_s], axis=-1)
        hp, s_p, h_p, tg_p, tv_p, _, _ = _layer(hp, lw, alpha, bp, lp, 1, None, wb, last)
        h_p3 = h_p.reshape(bp, lp, -1)
        cq_p = h_p3[:, lp - (CONV_QKV - 1):, 2 * d_a:2 * d_a + d_qkv]
        n_tail = tg_p.shape[1]
        cf_p = jnp.concatenate([tg_p, tv_p], axis=-1)[:, n_tail - (CONV_FFN - 1):]
        for acc, val in zip(outs, (s_p, cq_p, cf_p, s_s, cq_s, cf_s, v_s)):
            acc.append(val)
    return (hp.reshape(bp, lp, d_model), hs) + tuple(jnp.stack(o) for o in outs)
```

```python
import functools

import jax
import jax.numpy as jnp
from jax import lax
from jax.experimental import pallas as pl
from jax.experimental.pallas import tpu as pltpu

F32 = jnp.float32
BF16 = jnp.bfloat16

LANES = 128
SUBLANES = 8
VMEM_LIMIT = 56 * 1024 * 1024
DOWN_VMEM_LIMIT = 62 * 1024 * 1024

GROUP = 128
MLP_CHUNK = 128
DN_CHUNK = 64
CONV_QKV = 4
CONV_FFN = 3
FFN_SUB = 1
CMLP_CHUNKS = 8
EPS = 1e-5

PROJ_TM, PROJ_TN = 1024, 1024
QKV_TM = 512
DELTA_SEQS = 4
DELTA_SAMPLE_ROWS = 16
OUT_TM = 512
FFN_TM, FFN_TN = 1024, 512
DOWN_TM, DOWN_TK = 1024, 1408
DOWN_EMIT_TM, DOWN_EMIT_TK = 512, 512

_NT = (((1,), (1,)), ((), ()))
_TN = (((0,), (0,)), ((), ()))


def _params(sem, vmem_limit=VMEM_LIMIT):
    return pltpu.CompilerParams(dimension_semantics=sem, vmem_limit_bytes=vmem_limit)


def _bdot(a, b):
    return jnp.dot(a.astype(BF16), b.astype(BF16), preferred_element_type=F32)


def _gelu(x):
    return 0.5 * x * (1.0 + lax.erf(x * (0.5 ** 0.5)))


def _silu(x):
    return x * jax.nn.sigmoid(x)


def _layer_norm(x, g, b):
    mu = jnp.mean(x, axis=-1, keepdims=True)
    xc = x - mu
    var = jnp.mean(xc * xc, axis=-1, keepdims=True)
    return xc * lax.rsqrt(var + EPS) * g + b


def _rms_norm(x, g):
    ms = jnp.mean(x * x, axis=-1, keepdims=True)
    return x * lax.rsqrt(ms + EPS) * g


def _mm_kernel(x_ref, w_ref, we_ref, o_ref, e_ref, *rest):
    xb_ref = rest[-1]

    @pl.when(pl.program_id(1) == 0)
    def _():
        xb = x_ref[...].astype(BF16)
        xb_ref[...] = xb
        e_ref[...] = lax.dot_general(xb, we_ref[...], _NT, preferred_element_type=F32)

    wb = w_ref[...].astype(BF16)
    if len(rest) == 2:
        rest[0][...] = wb
    o_ref[...] = lax.dot_general(xb_ref[...], wb, _NT, preferred_element_type=F32)


def _matmul(x, wt, wt_extra, n_cols, tm, tn, emit_w=False):
    m, k = x.shape
    ne = wt_extra.shape[0]
    out_specs = [pl.BlockSpec((tm, tn), lambda i, j: (i, j)), pl.BlockSpec((tm, ne), lambda i, j: (i, 0))]
    out_shape = [jax.ShapeDtypeStruct((m, n_cols), F32), jax.ShapeDtypeStruct((m, ne), F32)]
    if emit_w:
        out_specs.append(pl.BlockSpec((tn, k), lambda i, j: (j, 0)))
        out_shape.append(jax.ShapeDtypeStruct((n_cols, k), BF16))
    return pl.pallas_call(
        _mm_kernel,
        grid=(m // tm, n_cols // tn),
        in_specs=[pl.BlockSpec((tm, k), lambda i, j: (i, 0)),
                  pl.BlockSpec((tn, k), lambda i, j: (j, 0)),
                  pl.BlockSpec((ne, k), lambda i, j: (0, 0))],
        out_specs=out_specs,
        out_shape=out_shape,
        scratch_shapes=[pltpu.VMEM((tm, k), BF16)],
        compiler_params=_params(("arbitrary", "arbitrary")),
        name="proj",
    )(x, wt, wt_extra)


def _cmlp_prompt_kernel(u_ref, v_ref, lng_ref, lnb_ref, ws_ref, bst_ref, og_ref, ya_ref, *, n_groups, n_chunks):
    row = lax.broadcasted_iota(jnp.int32, (MLP_CHUNK, MLP_CHUNK), 0)
    col = lax.broadcasted_iota(jnp.int32, (MLP_CHUNK, MLP_CHUNK), 1)
    cg = [(c, g) for c in range(n_chunks) for g in range(n_groups)]
    rows = lambda c: slice(c * MLP_CHUNK, (c + 1) * MLP_CHUNK)
    cols = lambda g: slice(g * GROUP, (g + 1) * GROUP)
    w = [jnp.where(row >= col, ws_ref[g], 0.0).astype(BF16) for g in range(n_groups)]
    n = range(len(cg))
    gv = [_gelu(v_ref[rows(c), cols(g)]) for c, g in cg]
    mu = [jnp.mean(gv[i], axis=-1, keepdims=True) for i in n]
    xc = [gv[i] - mu[i] for i in n]
    var = [jnp.mean(xc[i] * xc[i], axis=-1, keepdims=True) for i in n]
    vh = [xc[i] * lax.rsqrt(var[i] + EPS) * lng_ref[:, cols(g)] + lnb_ref[:, cols(g)] for i, (c, g) in enumerate(cg)]
    mixed = [jnp.dot(w[g], vh[i].astype(BF16), preferred_element_type=F32) + bst_ref[:, g:g + 1]
             for i, (c, g) in enumerate(cg)]
    y = [_gelu(u_ref[rows(c), cols(g)]) * mixed[i] for i, (c, g) in enumerate(cg)]
    ms = [jnp.mean(y[i] * y[i], axis=-1, keepdims=True) for i in n]
    for i, (c, g) in enumerate(cg):
        ya_ref[rows(c), cols(g)] = (y[i] * lax.rsqrt(ms[i] + EPS) * og_ref[:, cols(g)]).astype(ya_ref.dtype)


def _chunk_mlp_prompt(h, ln_g, ln_b, ws, bs, out_g):
    m = h.shape[0]
    n_groups, d_a = ws.shape[0], ws.shape[0] * GROUP
    vec = pl.BlockSpec((1, d_a), lambda i: (0, 0))
    tm = CMLP_CHUNKS * MLP_CHUNK
    return pl.pallas_call(
        functools.partial(_cmlp_prompt_kernel, n_groups=n_groups, n_chunks=CMLP_CHUNKS),
        grid=(m // tm,),
        in_specs=[pl.BlockSpec((tm, d_a), lambda i: (i, 0)),
                  pl.BlockSpec((tm, d_a), lambda i: (i, 1)),
                  vec, vec,
                  pl.BlockSpec((n_groups, MLP_CHUNK, MLP_CHUNK), lambda i: (0, 0, 0)),
                  pl.BlockSpec((MLP_CHUNK, n_groups), lambda i: (0, 0)),
                  vec],
        out_specs=pl.BlockSpec((tm, d_a), lambda i: (i, 0)),
        out_shape=jax.ShapeDtypeStruct((m, d_a), BF16),
        compiler_params=_params(("arbitrary",)),
        name="chunk_mlp_prompt",
    )(h, h, ln_g.reshape(1, d_a), ln_b.reshape(1, d_a), ws, bs.T, out_g.reshape(1, d_a))


def _cmlp_sample_kernel(u_ref, v_ref, lng_ref, lnb_ref, wv_ref, bv_ref, og_ref, ya_ref, vr_ref, *, n_pos):
    vh = []
    for t in range(n_pos):
        vt = _layer_norm(_gelu(v_ref[t]), lng_ref[0], lnb_ref[0])
        vr_ref[:, t, :] = vt
        vh.append(vt)
    for t in range(n_pos):
        mixed = wv_ref[t, 0:1, :] * vh[0]
        for s in range(1, t + 1):
            mixed = mixed + wv_ref[t, s:s + 1, :] * vh[s]
        mixed = mixed + bv_ref[0, t:t + 1, :]
        y = _gelu(u_ref[t]) * mixed
        ya_ref[t] = _rms_norm(y, og_ref[0]).astype(ya_ref.dtype)


def _chunk_mlp_sample(h3, ln_g, ln_b, ws, bs, out_g):
    n_pos, nb, _ = h3.shape
    n_groups = ws.shape[0]
    d_a = n_groups * GROUP
    wv = jnp.repeat(jnp.transpose(ws[:, :n_pos, :n_pos], (1, 2, 0)), GROUP, axis=-1)
    bv = jnp.repeat(bs[:, :n_pos].T, GROUP, axis=-1).reshape(1, n_pos, d_a)
    vec = pl.BlockSpec((1, 1, GROUP), lambda g: (0, 0, g))
    act = pl.BlockSpec((n_pos, nb, GROUP), lambda g: (0, 0, g))
    return pl.pallas_call(
        functools.partial(_cmlp_sample_kernel, n_pos=n_pos),
        grid=(n_groups,),
        in_specs=[act,
                  pl.BlockSpec((n_pos, nb, GROUP), lambda g: (0, 0, n_groups + g)),
                  vec, vec,
                  pl.BlockSpec((n_pos, n_pos, GROUP), lambda g: (0, 0, g)),
                  pl.BlockSpec((1, n_pos, GROUP), lambda g: (0, 0, g)),
                  vec],
        out_specs=[act, pl.BlockSpec((nb, n_pos, GROUP), lambda g: (0, 0, g))],
        out_shape=[jax.ShapeDtypeStruct((n_pos, nb, d_a), BF16),
                   jax.ShapeDtypeStruct((nb, n_pos, d_a), F32)],
        compiler_params=_params(("arbitrary",)),
        name="chunk_mlp_sample",
    )(h3, h3, ln_g.reshape(1, 1, d_a), ln_b.reshape(1, 1, d_a), wv, bv, out_g.reshape(1, 1, d_a))


def _conv_taps(xbuf_ref, w_ref, tm, shift, halo, width, cols=slice(None)):
    acc = None
    for j in range(width):
        off = halo - (width - 1 - j) * shift
        term = w_ref[j:j + 1, cols] * xbuf_ref[off:off + tm, :]
        acc = term if acc is None else acc + term
    return acc


def _qkv_prep_kernel(*refs, tm, shift, halo, n_heads, has_buf):
    if has_buf:
        (xq_ref, xk_ref, xv_ref, bq_ref, bk_ref, bv_ref, wq_ref, wk_ref, wv_ref,
         q_ref, k_ref, v_ref, xbuf_ref, carry_ref) = refs
        bufs = (bq_ref, bk_ref, bv_ref)
    else:
        (xq_ref, xk_ref, xv_ref, wq_ref, wk_ref, wv_ref,
         q_ref, k_ref, v_ref, xbuf_ref, carry_ref) = refs
        bufs = (None, None, None)
    first = pl.program_id(1) == 0
    parts = ((xq_ref, wq_ref, q_ref, GROUP ** -0.5), (xk_ref, wk_ref, k_ref, 1.0), (xv_ref, wv_ref, v_ref, None))
    for p, (x_ref, w_ref, o_ref, scale) in enumerate(parts):
        @pl.when(first)
        def _(p=p):
            if has_buf:
                xbuf_ref[0:halo, :] = bufs[p][...]
            else:
                xbuf_ref[0:halo, :] = jnp.zeros((halo, xbuf_ref.shape[1]), F32)

        @pl.when(jnp.logical_not(first))
        def _(p=p):
            xbuf_ref[0:halo, :] = carry_ref[p]

        xbuf_ref[halo:halo + tm, :] = x_ref[...]
        carry_ref[p] = xbuf_ref[tm:tm + halo, :]
        y = _silu(_conv_taps(xbuf_ref, w_ref, tm, shift, halo, CONV_QKV))
        if scale is None:
            o_ref[...] = y
        else:
            for h in range(n_heads):
                sl = slice(h * GROUP, (h + 1) * GROUP)
                yh = y[:, sl]
                inv = lax.rsqrt(jnp.sum(yh * yh, axis=-1, keepdims=True) + 1e-6)
                o_ref[:, sl] = yh * (inv * scale if scale != 1.0 else inv)


def _qkv_prep(h, conv_w, bufs, n_seq, rows_per_seq, tm, shift, col0):
    d_b = conv_w.shape[1] // 3
    n_heads = d_b // GROUP
    has_buf = bufs is not None
    halo = (CONV_QKV - 1) * shift if has_buf else SUBLANES
    tiles = rows_per_seq // tm
    cb = col0 // d_b
    x_specs = [pl.BlockSpec((tm, d_b), lambda s, i, c=c: (s * tiles + i, cb + c)) for c in range(3)]
    w_specs = [pl.BlockSpec((CONV_QKV, d_b), lambda s, i, c=c: (0, c)) for c in range(3)]
    o_spec = pl.BlockSpec((tm, d_b), lambda s, i: (s * tiles + i, 0))
    args, specs = [h, h, h], list(x_specs)
    if has_buf:
        specs += [pl.BlockSpec((halo, d_b), lambda s, i, c=c: (s, c)) for c in range(3)]
        args += [bufs, bufs, bufs]
    specs += w_specs
    args += [conv_w, conv_w, conv_w]
    m = n_seq * rows_per_seq
    return pl.pallas_call(
        functools.partial(_qkv_prep_kernel, tm=tm, shift=shift, halo=halo, n_heads=n_heads, has_buf=has_buf),
        grid=(n_seq, tiles),
        in_specs=specs,
        out_specs=[o_spec, o_spec, o_spec],
        out_shape=[jax.ShapeDtypeStruct((m, d_b), F32)] * 3,
        scratch_shapes=[pltpu.VMEM((halo + tm, d_b), F32), pltpu.VMEM((3, halo, d_b), F32)],
        compiler_params=_params(("arbitrary", "arbitrary")),
        name="qkv_prep",
    )(*args)


def _gates(ab, alog, dtb):
    g = -jnp.exp(alog) * jax.nn.softplus(ab + dtb)
    beta = jax.nn.sigmoid(ab)
    return g, beta


def _delta_prompt_kernel(q_ref, k_ref, v_ref, z_ref, ab_ref, alog_ref, dtb_ref, ng_ref,
                         yb_ref, sout_ref, s_ref, *, n_heads, n_chunks, ns):
    c = DN_CHUNK
    ci = pl.program_id(1)

    @pl.when(ci == 0)
    def _():
        s_ref[...] = jnp.zeros(s_ref.shape, F32)

    row = lax.broadcasted_iota(jnp.int32, (c, c), 0)
    col = lax.broadcasted_iota(jnp.int32, (c, c), 1)
    causal = row >= col
    strict = row > col
    eye = (row == col).astype(F32)
    tril = causal.astype(F32)
    n_double = (c - 1).bit_length() - 1
    ch = [(s, h) for s in range(ns) for h in range(n_heads)]
    sl = lambda h: slice(h * GROUP, (h + 1) * GROUP)
    gates = [_gates(ab_ref[s], alog_ref[...], dtb_ref[...]) for s in range(ns)]
    gc_s = [jnp.dot(tril, gates[s][0], precision=lax.Precision.HIGHEST, preferred_element_type=F32)
            for s in range(ns)]
    gct_s = [g.T for g in gc_s]
    q = [q_ref[s, :, sl(h)] for s, h in ch]
    k = [k_ref[s, :, sl(h)] for s, h in ch]
    gcol = [gc_s[s][:, h:h + 1] for s, h in ch]
    beta = [gates[s][1][:, n_heads + h:n_heads + h + 1] for s, h in ch]
    n = range(len(ch))
    decay = [jnp.exp(jnp.where(causal, gcol[i] - gct_s[s][h:h + 1, :], -jnp.inf)) for i, (s, h) in enumerate(ch)]
    kb = [k[i] * beta[i] for i in n]
    kbf = [k[i].astype(BF16) for i in n]
    kq = [lax.dot_general(jnp.concatenate([kb[i], q[i]], axis=0).astype(BF16), kbf[i], _NT,
                          preferred_element_type=F32) for i in n]
    qk = [kq[i][c:] * decay[i] for i in n]
    y = [jnp.where(strict, -(kq[i][:c] * decay[i]), 0.0) for i in n]
    x = [eye + y[i] for i in n]
    y = [_bdot(y[i], y[i]) for i in n]
    for _ in range(n_double - 1):
        t = [_bdot(jnp.concatenate([x[i], y[i]], axis=0), y[i]) for i in n]
        x = [x[i] + t[i][:c] for i in n]
        y = [t[i][c:] for i in n]
    x = [x[i] + _bdot(x[i], y[i]) for i in n]
    egc = [jnp.exp(gcol[i]) for i in n]
    sol = [_bdot(x[i], jnp.concatenate([v_ref[s, :, sl(h)] * beta[i], kb[i] * egc[i]], axis=1))
           for i, (s, h) in enumerate(ch)]
    s_old = [s_ref[s, h] for s, h in ch]
    r = [_bdot(jnp.concatenate([sol[i][:, GROUP:], q[i] * egc[i]], axis=0), s_old[i]) for i in n]
    v_new = [sol[i][:, :GROUP] - r[i][:c] for i in n]
    o = [r[i][c:] + _bdot(qk[i], v_new[i]) for i in n]
    gl = [gc_s[s][c - 1:c, h:h + 1] for s, h in ch]
    upd = [lax.dot_general((k[i] * jnp.exp(gl[i] - gcol[i])).astype(BF16), v_new[i].astype(BF16), _TN,
                           preferred_element_type=F32) for i in n]
    ms = [jnp.mean(o[i] * o[i], axis=-1, keepdims=True) for i in n]
    for i, (s, h) in enumerate(ch):
        s_ref[s, h] = s_old[i] * jnp.exp(gl[i]) + upd[i]
        on = o[i] * lax.rsqrt(ms[i] + EPS) * ng_ref[...]
        yb_ref[s, :, sl(h)] = (on * _silu(z_ref[s, :, sl(h)])).astype(yb_ref.dtype)

    @pl.when(ci == n_chunks - 1)
    def _():
        sout_ref[...] = s_ref[...]


def _delta_prompt(q, k, v, h, z_col0, ab, alog, dtb, norm_g, n_seq, rows_per_seq, ns):
    d_b = q.shape[1]
    n_heads = d_b // GROUP
    n_chunks = rows_per_seq // DN_CHUNK
    r3 = lambda t: t.reshape(n_seq, rows_per_seq, t.shape[-1])
    act = pl.BlockSpec((ns, DN_CHUNK, d_b), lambda s, i: (s, i, 0))
    vec = pl.BlockSpec((1, LANES), lambda s, i: (0, 0))
    zb = z_col0 // d_b
    yb, s_new = pl.pallas_call(
        functools.partial(_delta_prompt_kernel, n_heads=n_heads, n_chunks=n_chunks, ns=ns),
        grid=(n_seq // ns, n_chunks),
        in_specs=[act, act, act,
                  pl.BlockSpec((ns, DN_CHUNK, d_b), lambda s, i: (s, i, zb)),
                  pl.BlockSpec((ns, DN_CHUNK, LANES), lambda s, i: (s, i, 0)),
                  vec, vec, vec],
        out_specs=[act, pl.BlockSpec((ns, n_heads, GROUP, GROUP), lambda s, i: (s, 0, 0, 0))],
        out_shape=[jax.ShapeDtypeStruct((n_seq, rows_per_seq, d_b), BF16),
                   jax.ShapeDtypeStruct((n_seq, n_heads, GROUP, GROUP), F32)],
        scratch_shapes=[pltpu.VMEM((ns, n_heads, GROUP, GROUP), F32)],
        compiler_params=_params(("arbitrary", "arbitrary")),
        name="delta_prompt",
    )(r3(q), r3(k), r3(v), r3(h), r3(ab), alog, dtb, norm_g)
    return yb.reshape(n_seq * rows_per_seq, d_b), s_new


def _delta_sample_kernel(q_ref, k_ref, v_ref, z_ref, ab_ref, alog_ref, dtb_ref, ng_ref, s0_ref,
                         yb_ref, s1_ref, lhs_ref, res_ref, kt_ref, vn_ref, gl_ref, *, n_heads, n_pos, bb):
    nt = n_pos
    hs = range(n_heads)
    sls = [slice(h * GROUP, (h + 1) * GROUP) for h in hs]
    gates = [_gates(ab_ref[t], alog_ref[...], dtb_ref[...]) for t in range(nt)]
    zeros_pad = jnp.zeros((bb, GROUP), F32)
    ts = range(nt)
    q = [[q_ref[t, :, sls[h]] for t in ts] for h in hs]
    k = [[k_ref[t, :, sls[h]] for t in ts] for h in hs]
    beta = [[gates[t][1][:, n_heads + h:n_heads + h + 1] for t in ts] for h in hs]
    gc = []
    for h in hs:
        g = [gates[t][0][:, h:h + 1] for t in ts]
        for t in range(1, nt):
            g[t] = g[t - 1] + g[t]
        gc.append(g)
    kb = [[k[h][t] * beta[h][t] for t in ts] for h in hs]
    kk = [[[jnp.sum(kb[h][t] * k[h][s], axis=-1, keepdims=True) for s in range(t)] for t in ts] for h in hs]
    qk = [[[jnp.sum(q[h][t] * k[h][s], axis=-1, keepdims=True) for s in range(t + 1)] for t in ts] for h in hs]
    dec = [[[jnp.exp(gc[h][t] - gc[h][s]) for s in range(t + 1)] for t in ts] for h in hs]
    a = [[[kk[h][t][s] * dec[h][t][s] for s in range(t)] for t in ts] for h in hs]
    qk = [[[qk[h][t][s] * dec[h][t][s] for s in range(t + 1)] for t in ts] for h in hs]
    egc = [[jnp.exp(gc[h][t]) for t in ts] for h in hs]
    u_blk = [[v_ref[t, :, sls[h]] * beta[h][t] for t in ts] for h in hs]
    w_blk = [[kb[h][t] * egc[h][t] for t in ts] for h in hs]
    for t in ts:
        for h in hs:
            for s in range(t):
                u_blk[h][t] = u_blk[h][t] - a[h][t][s] * u_blk[h][s]
                w_blk[h][t] = w_blk[h][t] - a[h][t][s] * w_blk[h][s]
    for h in hs:
        for t in ts:
            lhs_ref[h, :, t, :] = w_blk[h][t]
            lhs_ref[h, :, nt + t, :] = q[h][t] * egc[h][t]
        gl_ref[h] = jnp.broadcast_to(egc[h][nt - 1], (bb, GROUP))

    for h in hs:
        for b in range(bb):
            res_ref[h, b] = _bdot(lhs_ref[h, b], s0_ref[b, h])

    v_new = [[u_blk[h][t] - res_ref[h, :, t, :] for t in ts] for h in hs]
    o = [[res_ref[h, :, nt + t, :] for t in ts] for h in hs]
    for h in hs:
        for t in ts:
            for s in range(t + 1):
                o[h][t] = o[h][t] + qk[h][t][s] * v_new[h][s]
    ms = [[jnp.mean(o[h][t] * o[h][t], axis=-1, keepdims=True) for t in ts] for h in hs]
    for h in hs:
        gl = gc[h][nt - 1]
        for t in ts:
            on = o[h][t] * lax.rsqrt(ms[h][t] + EPS) * ng_ref[...]
            yb_ref[t, :, sls[h]] = (on * _silu(z_ref[t, :, sls[h]])).astype(yb_ref.dtype)
            kt_ref[h, :, t, :] = k[h][t] * jnp.exp(gl - gc[h][t])
            vn_ref[h, :, t, :] = v_new[h][t]
            kt_ref[h, :, nt + t, :] = zeros_pad
            vn_ref[h, :, nt + t, :] = zeros_pad

    for h in hs:
        for b in range(bb):
            upd = lax.dot_general(kt_ref[h, b].astype(BF16), vn_ref[h, b].astype(BF16), _TN,
                                  preferred_element_type=F32)
            s1_ref[b, h] = s0_ref[b, h] * gl_ref[h, b:b + 1, :] + upd


def _delta_sample(q, k, v, h3, z_col0, ab, alog, dtb, norm_g, s0, bb):
    n_pos, nb, d_b = q.shape
    n_heads = d_b // GROUP
    assert 2 * n_pos == SUBLANES, "one state read packs [w | q] rows of a sequence into one 8-row tile"
    zb = z_col0 // d_b
    act = pl.BlockSpec((n_pos, bb, d_b), lambda i: (0, i, 0))
    vec = pl.BlockSpec((1, LANES), lambda i: (0, 0))
    st = pl.BlockSpec((bb, n_heads, GROUP, GROUP), lambda i: (i, 0, 0, 0))
    tile = pltpu.VMEM((n_heads, bb, SUBLANES, GROUP), F32)
    return pl.pallas_call(
        functools.partial(_delta_sample_kernel, n_heads=n_heads, n_pos=n_pos, bb=bb),
        grid=(nb // bb,),
        in_specs=[act, act, act,
                  pl.BlockSpec((n_pos, bb, d_b), lambda i: (0, i, zb)),
                  pl.BlockSpec((n_pos, bb, LANES), lambda i: (0, i, 0)),
                  vec, vec, vec, st],
        out_specs=[act, st],
        out_shape=[jax.ShapeDtypeStruct((n_pos, nb, d_b), BF16),
                   jax.ShapeDtypeStruct(s0.shape, F32)],
        scratch_shapes=[tile, tile, tile, tile, pltpu.VMEM((n_heads, bb, GROUP), F32)],
        compiler_params=_params(("arbitrary",)),
        name="delta_sample",
    )(q, k, v, h3, ab, alog, dtb, norm_g, s0)


def _out_proj_kernel(ya_ref, yb_ref, w_ref, r_ref, g_ref, b_ref, o_ref, ob_ref, z_ref, *, n_tiles, d_a, alpha):
    i = pl.program_id(0)

    @pl.when(i == 0)
    def _():
        z_ref[...] = jnp.zeros(z_ref.shape, F32)

    def emit_ln():
        y = _layer_norm(z_ref[...], g_ref[...], b_ref[...])
        o_ref[...] = y
        ob_ref[...] = y.astype(ob_ref.dtype)

    @pl.when(i < n_tiles)
    def _():
        emit_ln()
        acc = (jnp.dot(ya_ref[...], w_ref[0:d_a, :], preferred_element_type=F32)
               + jnp.dot(yb_ref[...], w_ref[d_a:, :], preferred_element_type=F32))
        z_ref[...] = alpha * r_ref[...] + acc

    @pl.when(i == n_tiles)
    def _():
        emit_ln()


def _out_proj_ln(ya, yb, w, resid, g, b, alpha, tm):
    m, d_a = ya.shape
    k, d = w.shape
    n_tiles = m // tm
    cur = lambda i: (jnp.minimum(i, n_tiles - 1), 0)
    prev = lambda i: (jnp.maximum(i - 1, 0), 0)
    vec = pl.BlockSpec((1, d), lambda i: (0, 0))
    return pl.pallas_call(
        functools.partial(_out_proj_kernel, n_tiles=n_tiles, d_a=d_a, alpha=alpha),
        grid=(n_tiles + 1,),
        in_specs=[pl.BlockSpec((tm, d_a), cur),
                  pl.BlockSpec((tm, k - d_a), cur),
                  pl.BlockSpec((k, d), lambda i: (0, 0), pipeline_mode=pl.Buffered(1)),
                  pl.BlockSpec((tm, d), cur), vec, vec],
        out_specs=[pl.BlockSpec((tm, d), prev), pl.BlockSpec((tm, d), prev)],
        out_shape=[jax.ShapeDtypeStruct((m, d), F32), jax.ShapeDtypeStruct((m, d), BF16)],
        scratch_shapes=[pltpu.VMEM((tm, d), F32)],
        compiler_params=_params(("arbitrary",)),
        name="out_proj_ln1",
    )(ya, yb, w, resid, g.reshape(1, d), b.reshape(1, d))


def _down_kernel(x_ref, w_ref, r_ref, g_ref, b_ref, o_ref, *rest, n_tiles, nk, nc, alpha, n_pos):
    z_ref = rest[-1]
    cw = z_ref.shape[-1]
    slab = lambda c: slice(c * cw, (c + 1) * cw)
    i, kk = pl.program_id(0), pl.program_id(1)

    def emit_ln():
        z = [z_ref[c] for c in range(nc)]
        inv_d = 1.0 / (nc * cw)
        mu = sum(jnp.sum(zc, axis=-1, keepdims=True) for zc in z) * inv_d
        xc = [zc - mu for zc in z]
        var = sum(jnp.sum(c * c, axis=-1, keepdims=True) for c in xc) * inv_d
        rs = lax.rsqrt(var + EPS)
        for c in range(nc):
            y = xc[c] * rs * g_ref[:, slab(c)] + b_ref[:, slab(c)]
            if n_pos:
                nb = y.shape[0] // n_pos
                for t in range(n_pos):
                    o_ref[:, t, slab(c)] = y[t * nb:(t + 1) * nb]
            else:
                o_ref[:, slab(c)] = y

    @pl.when(jnp.logical_and(i == 0, kk == 0))
    def _():
        z_ref[...] = jnp.zeros(z_ref.shape, F32)

    wb = w_ref[...].astype(BF16)
    if len(rest) == 2:
        rest[0][...] = wb

    @pl.when(kk == 0)
    def _():
        emit_ln()
        part = jnp.dot(x_ref[...], wb, preferred_element_type=F32)
        for c in range(nc):
            if nc == nk:
                z_ref[c] = part[:, slab(c)] + alpha * r_ref[...] if c == 0 else part[:, slab(c)]
            else:
                z_ref[c] = part[:, slab(c)] + alpha * r_ref[:, slab(c)]

    @pl.when(jnp.logical_and(kk > 0, i < n_tiles))
    def _():
        part = jnp.dot(x_ref[...], wb, preferred_element_type=F32)
        for c in range(nc):
            z_ref[c] += part[:, slab(c)]
        if nc == nk:
            z_ref[kk] += alpha * r_ref[...]


def _down_ln(x, w, resid, g, b, alpha, tm, tk, emit_w=False, n_pos=0):
    m, kdim = x.shape
    d = w.shape[1]
    nk = kdim // tk
    n_tiles = m // tm
    nc = nk if (d % nk == 0 and (d // nk) % LANES == 0) else 1
    row = lambda i: jnp.minimum(i, n_tiles - 1)
    kt = lambda i, kk: jnp.where(i == n_tiles, nk - 1, kk)
    vec = pl.BlockSpec((1, d), lambda i, kk: (0, 0))
    wspec = pl.BlockSpec((tk, d), lambda i, kk: (kt(i, kk), 0))
    if nc == nk:
        rspec = pl.BlockSpec((tm, d // nc), lambda i, kk: (row(i), kt(i, kk)))
    else:
        rspec = pl.BlockSpec((tm, d), lambda i, kk: (row(i), 0))
    if n_pos:
        assert n_tiles == 1
        out_specs = [pl.BlockSpec((m // n_pos, n_pos, d), lambda i, kk: (0, 0, 0))]
        out_shape = [jax.ShapeDtypeStruct((m // n_pos, n_pos, d), F32)]
    else:
        out_specs = [pl.BlockSpec((tm, d), lambda i, kk: (jnp.where(kk == 0, jnp.maximum(i - 1, 0), row(i)), 0))]
        out_shape = [jax.ShapeDtypeStruct((m, d), F32)]
    if emit_w:
        out_specs.append(wspec)
        out_shape.append(jax.ShapeDtypeStruct(w.shape, BF16))
    outs = pl.pallas_call(
        functools.partial(_down_kernel, n_tiles=n_tiles, nk=nk, nc=nc, alpha=alpha, n_pos=n_pos),
        grid=(n_tiles + 1, nk),
        in_specs=[pl.BlockSpec((tm, tk), lambda i, kk: (row(i), kt(i, kk))), wspec, rspec, vec, vec],
        out_specs=out_specs,
        out_shape=out_shape,
        scratch_shapes=[pltpu.VMEM((nc, tm, d // nc), F32)],
        compiler_params=_params(("arbitrary", "arbitrary"), DOWN_VMEM_LIMIT),
        name="down_ln2",
    )(x, w, resid, g.reshape(1, d), b.reshape(1, d))
    return tuple(outs) if emit_w else outs[0]


def _ffn_up_kernel(*refs, tm, shift, halo, has_buf, n_sub):
    n_in = 9 if has_buf else 7
    if has_buf:
        x_ref, wg_ref, wv_ref, bg_ref, bv_ref, cwg_ref, cwv_ref, cbg_ref, cbv_ref = refs[:n_in]
        bufs = (bg_ref, bv_ref)
    else:
        x_ref, wg_ref, wv_ref, cwg_ref, cwv_ref, cbg_ref, cbv_ref = refs[:n_in]
        bufs = (None, None)
    act_ref, tg_ref, tv_ref = refs[n_in:n_in + 3]
    wgb_ref, wvb_ref = refs[n_in + 3:n_in + 5]
    xbufs, carry_ref = refs[n_in + 5:-1], refs[-1]

    @pl.when(jnp.logical_and(pl.program_id(1) == 0, pl.program_id(2) == 0))
    def _():
        wgb_ref[...] = wg_ref[...].astype(BF16)
        wvb_ref[...] = wv_ref[...].astype(BF16)

    wsub = act_ref.shape[1] // n_sub
    cols = [slice(c * wsub, (c + 1) * wsub) for c in range(n_sub)]
    first = pl.program_id(2) == 0

    @pl.when(first)
    def _():
        for c in range(n_sub):
            for p in range(2):
                if has_buf:
                    xbufs[2 * c + p][0:halo, :] = bufs[p][:, cols[c]]
                else:
                    xbufs[2 * c + p][0:halo, :] = jnp.zeros((halo, wsub), F32)

    @pl.when(jnp.logical_not(first))
    def _():
        for c in range(n_sub):
            for p in range(2):
                xbufs[2 * c + p][0:halo, :] = carry_ref[p, :, cols[c]]

    x = x_ref[...]
    parts = ((wgb_ref, cwg_ref, cbg_ref, tg_ref), (wvb_ref, cwv_ref, cbv_ref, tv_ref))
    def conv_block(c, p):
        w_ref, cw_ref, cb_ref, t_ref = parts[p]
        xbuf_ref = xbufs[2 * c + p]
        xbuf_ref[halo:halo + tm, :] = jnp.dot(x, w_ref[:, cols[c]], preferred_element_type=F32)
        tail = xbuf_ref[tm:tm + halo, :]
        carry_ref[p, :, cols[c]] = tail
        if has_buf:
            for t in range(halo // shift):
                t_ref[:, t, cols[c]] = tail[t * shift:(t + 1) * shift]
        else:
            t_ref[0, :, cols[c]] = tail
        return _conv_taps(xbuf_ref, cw_ref, tm, shift, halo, CONV_FFN, cols[c]) + cb_ref[:, cols[c]]

    gate = [conv_block(c, 0) for c in range(n_sub)]
    for c in range(n_sub):
        act_ref[:, cols[c]] = (_silu(gate[c]) * conv_block(c, 1)).astype(act_ref.dtype)


def _ffn_up(x, w_up, conv_w, conv_b, bufs, n_seq, rows_per_seq, tm, tn, shift):
    m, d = x.shape
    d_ff = w_up.shape[1] // 2
    nj = d_ff // tn
    has_buf = bufs is not None
    halo = (CONV_FFN - 1) * shift if has_buf else SUBLANES
    tiles = rows_per_seq // tm
    wspec = lambda off: pl.BlockSpec((d, tn), lambda j, s, i: (0, off + j))
    cwspec = lambda off: pl.BlockSpec((CONV_FFN, tn), lambda j, s, i: (0, off + j))
    cbspec = lambda off: pl.BlockSpec((1, tn), lambda j, s, i: (0, off + j))
    if has_buf:
        assert n_seq == 1
        tspec = pl.BlockSpec((shift, halo // shift, tn), lambda j, s, i: (0, 0, j))
        tshape = jax.ShapeDtypeStruct((shift, halo // shift, d_ff), F32)
    else:
        tspec = pl.BlockSpec((1, halo, tn), lambda j, s, i: (s, 0, j))
        tshape = jax.ShapeDtypeStruct((n_seq, halo, d_ff), F32)
    specs = [pl.BlockSpec((tm, d), lambda j, s, i: (s * tiles + i, 0)), wspec(0), wspec(nj)]
    args = [x, w_up, w_up]
    if has_buf:
        specs += [pl.BlockSpec((halo, tn), lambda j, s, i, off=off: (s, off + j)) for off in (0, nj)]
        args += [bufs, bufs]
    specs += [cwspec(0), cwspec(nj), cbspec(0), cbspec(nj)]
    cb2 = conv_b.reshape(1, 2 * d_ff)
    args += [conv_w, conv_w, cb2, cb2]
    return pl.pallas_call(
        functools.partial(_ffn_up_kernel, tm=tm, shift=shift, halo=halo, has_buf=has_buf, n_sub=FFN_SUB),
        grid=(nj, n_seq, tiles),
        in_specs=specs,
        out_specs=[pl.BlockSpec((tm, tn), lambda j, s, i: (s * tiles + i, j)), tspec, tspec],
        out_shape=[jax.ShapeDtypeStruct((m, d_ff), BF16), tshape, tshape],
        scratch_shapes=[pltpu.VMEM((d, tn), BF16)] * 2
                       + [pltpu.VMEM((halo + tm, tn // FFN_SUB), F32)] * (2 * FFN_SUB)
                       + [pltpu.VMEM((2, halo, tn), F32)],
        compiler_params=_params(("arbitrary", "arbitrary", "arbitrary")),
        name="ffn_up",
    )(*args)


def _pad_lanes(vec, offset=0):
    out = jnp.zeros((1, LANES), F32)
    return out.at[0, offset:offset + vec.shape[0]].set(vec.astype(F32))


def _layer(x2, lw, alpha, n_seq, rows_per_seq, shift, state, wb, last):
    (w_in, conv_qkv_w, a_log, dt_bias, norm_g, gm_ln_g, gm_ln_b, gm_ws, gm_bs, gm_out_g, w_out,
     ln1_g, ln1_b, w_up, conv_ffn_w, conv_ffn_b, w_down, ln2_g, ln2_b) = lw
    m, d_model = x2.shape
    n_groups = gm_ws.shape[0]
    d_a = n_groups * GROUP
    d_qkv = conv_qkv_w.shape[1]
    d_b = d_qkv // 3
    n_heads = d_b // GROUP
    main_cols = 2 * d_a + d_qkv + d_b
    fresh = state is None
    n_pos = rows_per_seq // shift
    w_ab = jnp.pad(w_in[main_cols:].astype(BF16), ((0, LANES - 2 * n_heads), (0, 0)))
    if wb is None:
        h, ab, w_in_b = _matmul(x2, w_in, w_ab, main_cols, min(m, PROJ_TM), PROJ_TN, emit_w=True)
    else:
        w_in_b = wb[0]
        h, ab = _matmul(x2, w_in_b, w_ab, main_cols, min(m, PROJ_TM), PROJ_TN)
    alog, dtb, ng = _pad_lanes(a_log), _pad_lanes(dt_bias), norm_g.reshape(1, GROUP).astype(F32)

    if fresh:
        ya = _chunk_mlp_prompt(h, gm_ln_g, gm_ln_b, gm_ws, gm_bs, gm_out_g)
        v_rows = None
        q, k, v = _qkv_prep(h, conv_qkv_w, None, n_seq, rows_per_seq, QKV_TM, shift, 2 * d_a)
        yb, s_new = _delta_prompt(q, k, v, h, 2 * d_a + d_qkv, ab, alog, dtb, ng, n_seq, rows_per_seq, DELTA_SEQS)
        ffn_bufs = None
        tm_ffn = FFN_TM
    else:
        s_dn, buf_qkv, buf_ffn = state
        h3 = h.reshape(n_pos, shift, main_cols)
        ya3, v_rows = _chunk_mlp_sample(h3, gm_ln_g, gm_ln_b, gm_ws, gm_bs, gm_out_g)
        ya = ya3.reshape(m, d_a)
        q, k, v = _qkv_prep(h, conv_qkv_w, buf_qkv, n_seq, rows_per_seq, rows_per_seq, shift, 2 * d_a)
        r3 = lambda t: t.reshape(n_pos, shift, t.shape[-1])
        yb3, s_new = _delta_sample(r3(q), r3(k), r3(v), h3, 2 * d_a + d_qkv, r3(ab), alog, dtb, ng, s_dn,
                                   DELTA_SAMPLE_ROWS)
        yb = yb3.reshape(m, d_b)
        ffn_bufs = buf_ffn
        tm_ffn = rows_per_seq

    x1, x1b = _out_proj_ln(ya, yb, w_out, x2, ln1_g, ln1_b, alpha, min(m, OUT_TM))
    act, tail_g, tail_v = _ffn_up(x1b, w_up, conv_ffn_w, conv_ffn_b, ffn_bufs, n_seq, rows_per_seq, tm_ffn, FFN_TN,
                                  shift)
    if wb is None:
        y, w_down_b = _down_ln(act, w_down, x1, ln2_g, ln2_b, alpha, min(m, DOWN_EMIT_TM), DOWN_EMIT_TK,
                               emit_w=True, n_pos=n_pos if (last and not fresh) else 0)
    else:
        w_down_b = wb[1]
        y = _down_ln(act, w_down_b, x1, ln2_g, ln2_b, alpha, min(m, DOWN_TM), DOWN_TK)
    return y, s_new, h, tail_g, tail_v, v_rows, (w_in_b, w_down_b)


def kernel(x_prompt, x_sample, state_dn, state_conv_qkv, state_conv_ffn, w_in, conv_qkv_w, dn_a_log, dn_dt_bias, dn_norm_g, gm_ln_g, gm_ln_b, gm_ws, gm_bs, gm_out_g, w_out, ln1_g, ln1_b, w_up, conv_ffn_w, conv_ffn_b, w_down, ln2_g, ln2_b):
    depth = w_in.shape[0]
    bp, lp, d_model = x_prompt.shape
    bs, ls, _ = x_sample.shape
    d_qkv = conv_qkv_w.shape[-1]
    d_a = gm_ws.shape[1] * GROUP
    alpha = (2.0 * depth) ** 0.25

    hp = x_prompt.reshape(bp * lp, d_model)
    hs = jnp.transpose(x_sample, (1, 0, 2)).reshape(ls * bs, d_model)
    outs = [[] for _ in range(7)]
    w_out = w_out.astype(BF16)
    w_in = jnp.swapaxes(w_in, 1, 2)
    for l in range(depth):
        lw = tuple(t[l] for t in (w_in, conv_qkv_w, dn_a_log, dn_dt_bias, dn_norm_g, gm_ln_g, gm_ln_b, gm_ws,
                                  gm_bs, gm_out_g, w_out, ln1_g, ln1_b, w_up, conv_ffn_w, conv_ffn_b, w_down,
                                  ln2_g, ln2_b))
        last = l == depth - 1
        bq = jnp.transpose(state_conv_qkv[l], (1, 0, 2)).reshape((CONV_QKV - 1) * bs, d_qkv)
        bf = jnp.transpose(state_conv_ffn[l], (1, 0, 2)).reshape((CONV_FFN - 1) * bs, -1)
        hs, s_s, h_s, tg_s, tv_s, v_s, wb = _layer(hs, lw, alpha, 1, ls * bs, bs, (state_dn[l], bq, bf), None, last)
        h_s3 = h_s.reshape(ls, bs, -1)
        cq_s = jnp.transpose(h_s3[ls - (CONV_QKV - 1):, :, 2 * d_a:2 * d_a + d_qkv], (1, 0, 2))
        cf_s = jnp.concatenate([tg_s, tv_s], axis=-1)
        hp, s_p, h_p, tg_p, tv_p, _, _ = _layer(hp, lw, alpha, bp, lp, 1, None, wb, last)
        h_p3 = h_p.reshape(bp, lp, -1)
        cq_p = h_p3[:, lp - (CONV_QKV - 1):, 2 * d_a:2 * d_a + d_qkv]
        n_tail = tg_p.shape[1]
        cf_p = jnp.concatenate([tg_p, tv_p], axis=-1)[:, n_tail - (CONV_FFN - 1):]
        for acc, val in zip(outs, (s_p, cq_p, cf_p, s_s, cq_s, cf_s, v_s)):
            acc.append(val)
    return (hp.reshape(bp, lp, d_model), hs) + tuple(jnp.stack(o) for o in outs)
```

```python
import functools

import jax
import jax.numpy as jnp
from jax import lax
from jax.experimental import pallas as pl
from jax.experimental.pallas import tpu as pltpu

F32 = jnp.float32
BF16 = jnp.bfloat16

LANES = 128
SUBLANES = 8
VMEM_LIMIT = 56 * 1024 * 1024
DOWN_VMEM_LIMIT = 62 * 1024 * 1024

GROUP = 128
MLP_CHUNK = 128
DN_CHUNK = 64
CONV_QKV = 4
CONV_FFN = 3
FFN_SUB = 1
CMLP_CHUNKS = 8
EPS = 1e-5

PROJ_TM, PROJ_TN = 1024, 1024
QKV_TM = 512
DELTA_SEQS = 4
DELTA_SAMPLE_ROWS = 16
OUT_TM = 512
FFN_TM, FFN_TN = 1024, 512
DOWN_TM, DOWN_TK = 1024, 1408
DOWN_EMIT_TM, DOWN_EMIT_TK = 512, 512

_NT = (((1,), (1,)), ((), ()))
_TN = (((0,), (0,)), ((), ()))


def _params(sem, vmem_limit=VMEM_LIMIT):
    return pltpu.CompilerParams(dimension_semantics=sem, vmem_limit_bytes=vmem_limit)


def _bdot(a, b):
    return jnp.dot(a.astype(BF16), b.astype(BF16), preferred_element_type=F32)


def _gelu(x):
    return 0.5 * x * (1.0 + lax.erf(x * (0.5 ** 0.5)))


def _silu(x):
    return x * jax.nn.sigmoid(x)


def _layer_norm(x, g, b):
    mu = jnp.mean(x, axis=-1, keepdims=True)
    xc = x - mu
    var = jnp.mean(xc * xc, axis=-1, keepdims=True)
    return xc * lax.rsqrt(var + EPS) * g + b


def _rms_norm(x, g):
    ms = jnp.mean(x * x, axis=-1, keepdims=True)
    return x * lax.rsqrt(ms + EPS) * g


def _mm_kernel(x_ref, w_ref, we_ref, o_ref, e_ref, *rest):
    xb_ref = rest[-1]

    @pl.when(pl.program_id(1) == 0)
    def _():
        xb = x_ref[...].astype(BF16)
        xb_ref[...] = xb
        e_ref[...] = lax.dot_general(xb, we_ref[...], _NT, preferred_element_type=F32)

    wb = w_ref[...].astype(BF16)
    if len(rest) == 2:
        rest[0][...] = wb
    o_ref[...] = lax.dot_general(xb_ref[...], wb, _NT, preferred_element_type=F32)


def _matmul(x, wt, wt_extra, n_cols, tm, tn, emit_w=False):
    m, k = x.shape
    ne = wt_extra.shape[0]
    out_specs = [pl.BlockSpec((tm, tn), lambda i, j: (i, j)), pl.BlockSpec((tm, ne), lambda i, j: (i, 0))]
    out_shape = [jax.ShapeDtypeStruct((m, n_cols), F32), jax.ShapeDtypeStruct((m, ne), F32)]
    if emit_w:
        out_specs.append(pl.BlockSpec((tn, k), lambda i, j: (j, 0)))
        out_shape.append(jax.ShapeDtypeStruct((n_cols, k), BF16))
    return pl.pallas_call(
        _mm_kernel,
        grid=(m // tm, n_cols // tn),
        in_specs=[pl.BlockSpec((tm, k), lambda i, j: (i, 0)),
                  pl.BlockSpec((tn, k), lambda i, j: (j, 0)),
                  pl.BlockSpec((ne, k), lambda i, j: (0, 0))],
        out_specs=out_specs,
        out_shape=out_shape,
        scratch_shapes=[pltpu.VMEM((tm, k), BF16)],
        compiler_params=_params(("arbitrary", "arbitrary")),
        name="proj",
    )(x, wt, wt_extra)


def _cmlp_prompt_kernel(u_ref, v_ref, lng_ref, lnb_ref, ws_ref, bst_ref, og_ref, ya_ref, *, n_groups, n_chunks):
    row = lax.broadcasted_iota(jnp.int32, (MLP_CHUNK, MLP_CHUNK), 0)
    col = lax.broadcasted_iota(jnp.int32, (MLP_CHUNK, MLP_CHUNK), 1)
    cg = [(c, g) for c in range(n_chunks) for g in range(n_groups)]
    rows = lambda c: slice(c * MLP_CHUNK, (c + 1) * MLP_CHUNK)
    cols = lambda g: slice(g * GROUP, (g + 1) * GROUP)
    w = [jnp.where(row >= col, ws_ref[g], 0.0).astype(BF16) for g in range(n_groups)]
    n = range(len(cg))
    gv = [_gelu(v_ref[rows(c), cols(g)]) for c, g in cg]
    mu = [jnp.mean(gv[i], axis=-1, keepdims=True) for i in n]
    xc = [gv[i] - mu[i] for i in n]
    var = [jnp.mean(xc[i] * xc[i], axis=-1, keepdims=True) for i in n]
    vh = [xc[i] * lax.rsqrt(var[i] + EPS) * lng_ref[:, cols(g)] + lnb_ref[:, cols(g)] for i, (c, g) in enumerate(cg)]
    mixed = [jnp.dot(w[g], vh[i].astype(BF16), preferred_element_type=F32) + bst_ref[:, g:g + 1]
             for i, (c, g) in enumerate(cg)]
    y = [_gelu(u_ref[rows(c), cols(g)]) * mixed[i] for i, (c, g) in enumerate(cg)]
    ms = [jnp.mean(y[i] * y[i], axis=-1, keepdims=True) for i in n]
    for i, (c, g) in enumerate(cg):
        ya_ref[rows(c), cols(g)] = (y[i] * lax.rsqrt(ms[i] + EPS) * og_ref[:, cols(g)]).astype(ya_ref.dtype)


def _chunk_mlp_prompt(h, ln_g, ln_b, ws, bs, out_g):
    m = h.shape[0]
    n_groups, d_a = ws.shape[0], ws.shape[0] * GROUP
    vec = pl.BlockSpec((1, d_a), lambda i: (0, 0))
    tm = CMLP_CHUNKS * MLP_CHUNK
    return pl.pallas_call(
        functools.partial(_cmlp_prompt_kernel, n_groups=n_groups, n_chunks=CMLP_CHUNKS),
        grid=(m // tm,),
        in_specs=[pl.BlockSpec((tm, d_a), lambda i: (i, 0)),
                  pl.BlockSpec((tm, d_a), lambda i: (i, 1)),
                  vec, vec,
                  pl.BlockSpec((n_groups, MLP_CHUNK, MLP_CHUNK), lambda i: (0, 0, 0)),
                  pl.BlockSpec((MLP_CHUNK, n_groups), lambda i: (0, 0)),
                  vec],
        out_specs=pl.BlockSpec((tm, d_a), lambda i: (i, 0)),
        out_shape=jax.ShapeDtypeStruct((m, d_a), BF16),
        compiler_params=_params(("arbitrary",)),
        name="chunk_mlp_prompt",
    )(h, h, ln_g.reshape(1, d_a), ln_b.reshape(1, d_a), ws, bs.T, out_g.reshape(1, d_a))


def _cmlp_sample_kernel(u_ref, v_ref, lng_ref, lnb_ref, wv_ref, bv_ref, og_ref, ya_ref, vr_ref, *, n_pos):
    vh = []
    for t in range(n_pos):
        vt = _layer_norm(_gelu(v_ref[t]), lng_ref[0], lnb_ref[0])
        vr_ref[:, t, :] = vt
        vh.append(vt)
    for t in range(n_pos):
        mixed = wv_ref[t, 0:1, :] * vh[0]
        for s in range(1, t + 1):
            mixed = mixed + wv_ref[t, s:s + 1, :] * vh[s]
        mixed = mixed + bv_ref[0, t:t + 1, :]
        y = _gelu(u_ref[t]) * mixed
        ya_ref[t] = _rms_norm(y, og_ref[0]).astype(ya_ref.dtype)


def _chunk_mlp_sample(h3, ln_g, ln_b, ws, bs, out_g):
    n_pos, nb, _ = h3.shape
    n_groups = ws.shape[0]
    d_a = n_groups * GROUP
    wv = jnp.repeat(jnp.transpose(ws[:, :n_pos, :n_pos], (1, 2, 0)), GROUP, axis=-1)
    bv = jnp.repeat(bs[:, :n_pos].T, GROUP, axis=-1).reshape(1, n_pos, d_a)
    vec = pl.BlockSpec((1, 1, GROUP), lambda g: (0, 0, g))
    act = pl.BlockSpec((n_pos, nb, GROUP), lambda g: (0, 0, g))
    return pl.pallas_call(
        functools.partial(_cmlp_sample_kernel, n_pos=n_pos),
        grid=(n_groups,),
        in_specs=[act,
                  pl.BlockSpec((n_pos, nb, GROUP), lambda g: (0, 0, n_groups + g)),
                  vec, vec,
                  pl.BlockSpec((n_pos, n_pos, GROUP), lambda g: (0, 0, g)),
                  pl.BlockSpec((1, n_pos, GROUP), lambda g: (0, 0, g)),
                  vec],
        out_specs=[act, pl.BlockSpec((nb, n_pos, GROUP), lambda g: (0, 0, g))],
        out_shape=[jax.ShapeDtypeStruct((n_pos, nb, d_a), BF16),
                   jax.ShapeDtypeStruct((nb, n_pos, d_a), F32)],
        compiler_params=_params(("arbitrary",)),
        name="chunk_mlp_sample",
    )(h3, h3, ln_g.reshape(1, 1, d_a), ln_b.reshape(1, 1, d_a), wv, bv, out_g.reshape(1, 1, d_a))


def _conv_taps(xbuf_ref, w_ref, tm, shift, halo, width, cols=slice(None)):
    acc = None
    for j in range(width):
        off = halo - (width - 1 - j) * shift
        term = w_ref[j:j + 1, cols] * xbuf_ref[off:off + tm, :]
        acc = term if acc is None else acc + term
    return acc


def _qkv_prep_kernel(*refs, tm, shift, halo, n_heads, has_buf):
    if has_buf:
        (xq_ref, xk_ref, xv_ref, bq_ref, bk_ref, bv_ref, wq_ref, wk_ref, wv_ref,
         q_ref, k_ref, v_ref, xbuf_ref, carry_ref) = refs
        bufs = (bq_ref, bk_ref, bv_ref)
    else:
        (xq_ref, xk_ref, xv_ref, wq_ref, wk_ref, wv_ref,
         q_ref, k_ref, v_ref, xbuf_ref, carry_ref) = refs
        bufs = (None, None, None)
    first = pl.program_id(1) == 0
    parts = ((xq_ref, wq_ref, q_ref, GROUP ** -0.5), (xk_ref, wk_ref, k_ref, 1.0), (xv_ref, wv_ref, v_ref, None))
    for p, (x_ref, w_ref, o_ref, scale) in enumerate(parts):
        @pl.when(first)
        def _(p=p):
            if has_buf:
                xbuf_ref[0:halo, :] = bufs[p][...]
            else:
                xbuf_ref[0:halo, :] = jnp.zeros((halo, xbuf_ref.shape[1]), F32)

        @pl.when(jnp.logical_not(first))
        def _(p=p):
            xbuf_ref[0:halo, :] = carry_ref[p]

        xbuf_ref[halo:halo + tm, :] = x_ref[...]
        carry_ref[p] = xbuf_ref[tm:tm + halo, :]
        y = _silu(_conv_taps(xbuf_ref, w_ref, tm, shift, halo, CONV_QKV))
        if scale is None:
            o_ref[...] = y
        else:
            for h in range(n_heads):
                sl = slice(h * GROUP, (h + 1) * GROUP)
                yh = y[:, sl]
                inv = lax.rsqrt(jnp.sum(yh * yh, axis=-1, keepdims=True) + 1e-6)
                o_ref[:, sl] = yh * (inv * scale if scale != 1.0 else inv)


def _qkv_prep(h, conv_w, bufs, n_seq, rows_per_seq, tm, shift, col0):
    d_b = conv_w.shape[1] // 3
    n_heads = d_b // GROUP
    has_buf = bufs is not None
    halo = (CONV_QKV - 1) * shift if has_buf else SUBLANES
    tiles = rows_per_seq // tm
    cb = col0 // d_b
    x_specs = [pl.BlockSpec((tm, d_b), lambda s, i, c=c: (s * tiles + i, cb + c)) for c in range(3)]
    w_specs = [pl.BlockSpec((CONV_QKV, d_b), lambda s, i, c=c: (0, c)) for c in range(3)]
    o_spec = pl.BlockSpec((tm, d_b), lambda s, i: (s * tiles + i, 0))
    args, specs = [h, h, h], list(x_specs)
    if has_buf:
        specs += [pl.BlockSpec((halo, d_b), lambda s, i, c=c: (s, c)) for c in range(3)]
        args += [bufs, bufs, bufs]
    specs += w_specs
    args += [conv_w, conv_w, conv_w]
    m = n_seq * rows_per_seq
    return pl.pallas_call(
        functools.partial(_qkv_prep_kernel, tm=tm, shift=shift, halo=halo, n_heads=n_heads, has_buf=has_buf),
        grid=(n_seq, tiles),
        in_specs=specs,
        out_specs=[o_spec, o_spec, o_spec],
        out_shape=[jax.ShapeDtypeStruct((m, d_b), F32)] * 3,
        scratch_shapes=[pltpu.VMEM((halo + tm, d_b), F32), pltpu.VMEM((3, halo, d_b), F32)],
        compiler_params=_params(("arbitrary", "arbitrary")),
        name="qkv_prep",
    )(*args)


def _gates(ab, alog, dtb):
    g = -jnp.exp(alog) * jax.nn.softplus(ab + dtb)
    beta = jax.nn.sigmoid(ab)
    return g, beta


def _delta_prompt_kernel(q_ref, k_ref, v_ref, z_ref, ab_ref, alog_ref, dtb_ref, ng_ref,
                         yb_ref, sout_ref, s_ref, *, n_heads, n_chunks, ns):
    c = DN_CHUNK
    ci = pl.program_id(1)

    @pl.when(ci == 0)
    def _():
        s_ref[...] = jnp.zeros(s_ref.shape, F32)

    row = lax.broadcasted_iota(jnp.int32, (c, c), 0)
    col = lax.broadcasted_iota(jnp.int32, (c, c), 1)
    causal = row >= col
    strict = row > col
    eye = (row == col).astype(F32)
    tril = causal.astype(F32)
    n_double = (c - 1).bit_length() - 1
    ch = [(s, h) for s in range(ns) for h in range(n_heads)]
    sl = lambda h: slice(h * GROUP, (h + 1) * GROUP)
    gates = [_gates(ab_ref[s], alog_ref[...], dtb_ref[...]) for s in range(ns)]
    gc_s = [jnp.dot(tril, gates[s][0], precision=lax.Precision.HIGHEST, preferred_element_type=F32)
            for s in range(ns)]
    gct_s = [g.T for g in gc_s]
    q = [q_ref[s, :, sl(h)] for s, h in ch]
    k = [k_ref[s, :, sl(h)] for s, h in ch]
    gcol = [gc_s[s][:, h:h + 1] for s, h in ch]
    beta = [gates[s][1][:, n_heads + h:n_heads + h + 1] for s, h in ch]
    n = range(len(ch))
    decay = [jnp.exp(jnp.where(causal, gcol[i] - gct_s[s][h:h + 1, :], -jnp.inf)) for i, (s, h) in enumerate(ch)]
    kb = [k[i] * beta[i] for i in n]
    kbf = [k[i].astype(BF16) for i in n]
    kq = [lax.dot_general(jnp.concatenate([kb[i], q[i]], axis=0).astype(BF16), kbf[i], _NT,
                          preferred_element_type=F32) for i in n]
    qk = [kq[i][c:] * decay[i] for i in n]
    y = [jnp.where(strict, -(kq[i][:c] * decay[i]), 0.0) for i in n]
    x = [eye + y[i] for i in n]
    y = [_bdot(y[i], y[i]) for i in n]
    for _ in range(n_double - 1):
        t = [_bdot(jnp.concatenate([x[i], y[i]], axis=0), y[i]) for i in n]
        x = [x[i] + t[i][:c] for i in n]
        y = [t[i][c:] for i in n]
    x = [x[i] + _bdot(x[i], y[i]) for i in n]
    egc = [jnp.exp(gcol[i]) for i in n]
    sol = [_bdot(x[i], jnp.concatenate([v_ref[s, :, sl(h)] * beta[i], kb[i] * egc[i]], axis=1))
           for i, (s, h) in enumerate(ch)]
    s_old = [s_ref[s, h] for s, h in ch]
    r = [_bdot(jnp.concatenate([sol[i][:, GROUP:], q[i] * egc[i]], axis=0), s_old[i]) for i in n]
    v_new = [sol[i][:, :GROUP] - r[i][:c] for i in n]
    o = [r[i][c:] + _bdot(qk[i], v_new[i]) for i in n]
    gl = [gc_s[s][c - 1:c, h:h + 1] for s, h in ch]
    upd = [lax.dot_general((k[i] * jnp.exp(gl[i] - gcol[i])).astype(BF16), v_new[i].astype(BF16), _TN,
                           preferred_element_type=F32) for i in n]
    ms = [jnp.mean(o[i] * o[i], axis=-1, keepdims=True) for i in n]
    for i, (s, h) in enumerate(ch):
        s_ref[s, h] = s_old[i] * jnp.exp(gl[i]) + upd[i]
        on = o[i] * lax.rsqrt(ms[i] + EPS) * ng_ref[...]
        yb_ref[s, :, sl(h)] = (on * _silu(z_ref[s, :, sl(h)])).astype(yb_ref.dtype)

    @pl.when(ci == n_chunks - 1)
    def _():
        sout_ref[...] = s_ref[...]


def _delta_prompt(q, k, v, h, z_col0, ab, alog, dtb, norm_g, n_seq, rows_per_seq, ns):
    d_b = q.shape[1]
    n_heads = d_b // GROUP
    n_chunks = rows_per_seq // DN_CHUNK
    r3 = lambda t: t.reshape(n_seq, rows_per_seq, t.shape[-1])
    act = pl.BlockSpec((ns, DN_CHUNK, d_b), lambda s, i: (s, i, 0))
    vec = pl.BlockSpec((1, LANES), lambda s, i: (0, 0))
    zb = z_col0 // d_b
    yb, s_new = pl.pallas_call(
        functools.partial(_delta_prompt_kernel, n_heads=n_heads, n_chunks=n_chunks, ns=ns),
        grid=(n_seq // ns, n_chunks),
        in_specs=[act, act, act,
                  pl.BlockSpec((ns, DN_CHUNK, d_b), lambda s, i: (s, i, zb)),
                  pl.BlockSpec((ns, DN_CHUNK, LANES), lambda s, i: (s, i, 0)),
                  vec, vec, vec],
        out_specs=[act, pl.BlockSpec((ns, n_heads, GROUP, GROUP), lambda s, i: (s, 0, 0, 0))],
        out_shape=[jax.ShapeDtypeStruct((n_seq, rows_per_seq, d_b), BF16),
                   jax.ShapeDtypeStruct((n_seq, n_heads, GROUP, GROUP), F32)],
        scratch_shapes=[pltpu.VMEM((ns, n_heads, GROUP, GROUP), F32)],
        compiler_params=_params(("arbitrary", "arbitrary")),
        name="delta_prompt",
    )(r3(q), r3(k), r3(v), r3(h), r3(ab), alog, dtb, norm_g)
    return yb.reshape(n_seq * rows_per_seq, d_b), s_new


def _delta_sample_kernel(q_ref, k_ref, v_ref, z_ref, ab_ref, alog_ref, dtb_ref, ng_ref, s0_ref,
                         yb_ref, s1_ref, lhs_ref, res_ref, kt_ref, vn_ref, gl_ref, *, n_heads, n_pos, bb):
    nt = n_pos
    hs = range(n_heads)
    sls = [slice(h * GROUP, (h + 1) * GROUP) for h in hs]
    gates = [_gates(ab_ref[t], alog_ref[...], dtb_ref[...]) for t in range(nt)]
    zeros_pad = jnp.zeros((bb, GROUP), F32)
    ts = range(nt)
    q = [[q_ref[t, :, sls[h]] for t in ts] for h in hs]
    k = [[k_ref[t, :, sls[h]] for t in ts] for h in hs]
    beta = [[gates[t][1][:, n_heads + h:n_heads + h + 1] for t in ts] for h in hs]
    gc = []
    for h in hs:
        g = [gates[t][0][:, h:h + 1] for t in ts]
        for t in range(1, nt):
            g[t] = g[t - 1] + g[t]
        gc.append(g)
    kb = [[k[h][t] * beta[h][t] for t in ts] for h in hs]
    kk = [[[jnp.sum(kb[h][t] * k[h][s], axis=-1, keepdims=True) for s in range(t)] for t in ts] for h in hs]
    qk = [[[jnp.sum(q[h][t] * k[h][s], axis=-1, keepdims=True) for s in range(t + 1)] for t in ts] for h in hs]
    dec = [[[jnp.exp(gc[h][t] - gc[h][s]) for s in range(t + 1)] for t in ts] for h in hs]
    a = [[[kk[h][t][s] * dec[h][t][s] for s in range(t)] for t in ts] for h in hs]
    qk = [[[qk[h][t][s] * dec[h][t][s] for s in range(t + 1)] for t in ts] for h in hs]
    egc = [[jnp.exp(gc[h][t]) for t in ts] for h in hs]
    u_blk = [[v_ref[t, :, sls[h]] * beta[h][t] for t in ts] for h in hs]
    w_blk = [[kb[h][t] * egc[h][t] for t in ts] for h in hs]
    for t in ts:
        for h in hs:
            for s in range(t):
                u_blk[h][t] = u_blk[h][t] - a[h][t][s] * u_blk[h][s]
                w_blk[h][t] = w_blk[h][t] - a[h][t][s] * w_blk[h][s]
    for h in hs:
        for t in ts:
            lhs_ref[h, :, t, :] = w_blk[h][t]
            lhs_ref[h, :, nt + t, :] = q[h][t] * egc[h][t]
        gl_ref[h] = jnp.broadcast_to(egc[h][nt - 1], (bb, GROUP))

    for h in hs:
        for b in range(bb):
            res_ref[h, b] = _bdot(lhs_ref[h, b], s0_ref[b, h])

    v_new = [[u_blk[h][t] - res_ref[h, :, t, :] for t in ts] for h in hs]
    o = [[res_ref[h, :, nt + t, :] for t in ts] for h in hs]
    for h in hs:
        for t in ts:
            for s in range(t + 1):
                o[h][t] = o[h][t] + qk[h][t][s] * v_new[h][s]
    ms = [[jnp.mean(o[h][t] * o[h][t], axis=-1, keepdims=True) for t in ts] for h in hs]
    for h in hs:
        gl = gc[h][nt - 1]
        for t in ts:
            on = o[h][t] * lax.rsqrt(ms[h][t] + EPS) * ng_ref[...]
            yb_ref[t, :, sls[h]] = (on * _silu(z_ref[t, :, sls[h]])).astype(yb_ref.dtype)
            kt_ref[h, :, t, :] = k[h][t] * jnp.exp(gl - gc[h][t])
            vn_ref[h, :, t, :] = v_new[h][t]
            kt_ref[h, :, nt + t, :] = zeros_pad
            vn_ref[h, :, nt + t, :] = zeros_pad

    for h in hs:
        for b in range(bb):
            upd = lax.dot_general(kt_ref[h, b].astype(BF16), vn_ref[h, b].astype(BF16), _TN,
                                  preferred_element_type=F32)
            s1_ref[b, h] = s0_ref[b, h] * gl_ref[h, b:b + 1, :] + upd


def _delta_sample(q, k, v, h3, z_col0, ab, alog, dtb, norm_g, s0, bb):
    n_pos, nb, d_b = q.shape
    n_heads = d_b // GROUP
    assert 2 * n_pos == SUBLANES, "one state read packs [w | q] rows of a sequence into one 8-row tile"
    zb = z_col0 // d_b
    act = pl.BlockSpec((n_pos, bb, d_b), lambda i: (0, i, 0))
    vec = pl.BlockSpec((1, LANES), lambda i: (0, 0))
    st = pl.BlockSpec((bb, n_heads, GROUP, GROUP), lambda i: (i, 0, 0, 0))
    tile = pltpu.VMEM((n_heads, bb, SUBLANES, GROUP), F32)
    return pl.pallas_call(
        functools.partial(_delta_sample_kernel, n_heads=n_heads, n_pos=n_pos, bb=bb),
        grid=(nb // bb,),
        in_specs=[act, act, act,
                  pl.BlockSpec((n_pos, bb, d_b), lambda i: (0, i, zb)),
                  pl.BlockSpec((n_pos, bb, LANES), lambda i: (0, i, 0)),
                  vec, vec, vec, st],
        out_specs=[act, st],
        out_shape=[jax.ShapeDtypeStruct((n_pos, nb, d_b), BF16),
                   jax.ShapeDtypeStruct(s0.shape, F32)],
        scratch_shapes=[tile, tile, tile, tile, pltpu.VMEM((n_heads, bb, GROUP), F32)],
        compiler_params=_params(("arbitrary",)),
        name="delta_sample",
    )(q, k, v, h3, ab, alog, dtb, norm_g, s0)


def _out_proj_kernel(ya_ref, yb_ref, w_ref, r_ref, g_ref, b_ref, o_ref, ob_ref, z_ref, *, n_tiles, d_a, alpha):
    i = pl.program_id(0)

    @pl.when(i == 0)
    def _():
        z_ref[...] = jnp.zeros(z_ref.shape, F32)

    def emit_ln():
        y = _layer_norm(z_ref[...], g_ref[...], b_ref[...])
        o_ref[...] = y
        ob_ref[...] = y.astype(ob_ref.dtype)

    @pl.when(i < n_tiles)
    def _():
        emit_ln()
        acc = (jnp.dot(ya_ref[...], w_ref[0:d_a, :], preferred_element_type=F32)
               + jnp.dot(yb_ref[...], w_ref[d_a:, :], preferred_element_type=F32))
        z_ref[...] = alpha * r_ref[...] + acc

    @pl.when(i == n_tiles)
    def _():
        emit_ln()


def _out_proj_ln(ya, yb, w, resid, g, b, alpha, tm):
    m, d_a = ya.shape
    k, d = w.shape
    n_tiles = m // tm
    cur = lambda i: (jnp.minimum(i, n_tiles - 1), 0)
    prev = lambda i: (jnp.maximum(i - 1, 0), 0)
    vec = pl.BlockSpec((1, d), lambda i: (0, 0))
    return pl.pallas_call(
        functools.partial(_out_proj_kernel, n_tiles=n_tiles, d_a=d_a, alpha=alpha),
        grid=(n_tiles + 1,),
        in_specs=[pl.BlockSpec((tm, d_a), cur),
                  pl.BlockSpec((tm, k - d_a), cur),
                  pl.BlockSpec((k, d), lambda i: (0, 0), pipeline_mode=pl.Buffered(1)),
                  pl.BlockSpec((tm, d), cur), vec, vec],
        out_specs=[pl.BlockSpec((tm, d), prev), pl.BlockSpec((tm, d), prev)],
        out_shape=[jax.ShapeDtypeStruct((m, d), F32), jax.ShapeDtypeStruct((m, d), BF16)],
        scratch_shapes=[pltpu.VMEM((tm, d), F32)],
        compiler_params=_params(("arbitrary",)),
        name="out_proj_ln1",
    )(ya, yb, w, resid, g.reshape(1, d), b.reshape(1, d))


def _down_kernel(x_ref, w_ref, r_ref, g_ref, b_ref, o_ref, *rest, n_tiles, nk, nc, alpha, n_pos):
    z_ref = rest[-1]
    cw = z_ref.shape[-1]
    slab = lambda c: slice(c * cw, (c + 1) * cw)
    i, kk = pl.program_id(0), pl.program_id(1)

    def emit_ln():
        z = [z_ref[c] for c in range(nc)]
        inv_d = 1.0 / (nc * cw)
        mu = sum(jnp.sum(zc, axis=-1, keepdims=True) for zc in z) * inv_d
        xc = [zc - mu for zc in z]
        var = sum(jnp.sum(c * c, axis=-1, keepdims=True) for c in xc) * inv_d
        rs = lax.rsqrt(var + EPS)
        for c in range(nc):
            y = xc[c] * rs * g_ref[:, slab(c)] + b_ref[:, slab(c)]
            if n_pos:
                nb = y.shape[0] // n_pos
                for t in range(n_pos):
                    o_ref[:, t, slab(c)] = y[t * nb:(t + 1) * nb]
            else:
                o_ref[:, slab(c)] = y

    @pl.when(jnp.logical_and(i == 0, kk == 0))
    def _():
        z_ref[...] = jnp.zeros(z_ref.shape, F32)

    wb = w_ref[...].astype(BF16)
    if len(rest) == 2:
        rest[0][...] = wb

    @pl.when(kk == 0)
    def _():
        emit_ln()
        part = jnp.dot(x_ref[...], wb, preferred_element_type=F32)
        for c in range(nc):
            if nc == nk:
                z_ref[c] = part[:, slab(c)] + alpha * r_ref[...] if c == 0 else part[:, slab(c)]
            else:
                z_ref[c] = part[:, slab(c)] + alpha * r_ref[:, slab(c)]

    @pl.when(jnp.logical_and(kk > 0, i < n_tiles))
    def _():
        part = jnp.dot(x_ref[...], wb, preferred_element_type=F32)
        for c in range(nc):
            z_ref[c] += part[:, slab(c)]
        if nc == nk:
            z_ref[kk] += alpha * r_ref[...]


def _down_ln(x, w, resid, g, b, alpha, tm, tk, emit_w=False, n_pos=0):
    m, kdim = x.shape
    d = w.shape[1]
    nk = kdim // tk
    n_tiles = m // tm
    nc = nk if (d % nk == 0 and (d // nk) % LANES == 0) else 1
    row = lambda i: jnp.minimum(i, n_tiles - 1)
    kt = lambda i, kk: jnp.where(i == n_tiles, nk - 1, kk)
    vec = pl.BlockSpec((1, d), lambda i, kk: (0, 0))
    wspec = pl.BlockSpec((tk, d), lambda i, kk: (kt(i, kk), 0))
    if nc == nk:
        rspec = pl.BlockSpec((tm, d // nc), lambda i, kk: (row(i), kt(i, kk)))
    else:
        rspec = pl.BlockSpec((tm, d), lambda i, kk: (row(i), 0))
    if n_pos:
        assert n_tiles == 1
        out_specs = [pl.BlockSpec((m // n_pos, n_pos, d), lambda i, kk: (0, 0, 0))]
        out_shape = [jax.ShapeDtypeStruct((m // n_pos, n_pos, d), F32)]
    else:
        out_specs = [pl.BlockSpec((tm, d), lambda i, kk: (jnp.where(kk == 0, jnp.maximum(i - 1, 0), row(i)), 0))]
        out_shape = [jax.ShapeDtypeStruct((m, d), F32)]
    if emit_w:
        out_specs.append(wspec)
        out_shape.append(jax.ShapeDtypeStruct(w.shape, BF16))
    outs = pl.pallas_call(
        functools.partial(_down_kernel, n_tiles=n_tiles, nk=nk, nc=nc, alpha=alpha, n_pos=n_pos),
        grid=(n_tiles + 1, nk),
        in_specs=[pl.BlockSpec((tm, tk), lambda i, kk: (row(i), kt(i, kk))), wspec, rspec, vec, vec],
        out_specs=out_specs,
        out_shape=out_shape,
        scratch_shapes=[pltpu.VMEM((nc, tm, d // nc), F32)],
        compiler_params=_params(("arbitrary", "arbitrary"), DOWN_VMEM_LIMIT),
        name="down_ln2",
    )(x, w, resid, g.reshape(1, d), b.reshape(1, d))
    return tuple(outs) if emit_w else outs[0]


def _ffn_up_kernel(*refs, tm, shift, halo, has_buf, n_sub, cast_w):
    n_in = 9 if has_buf else 7
    if has_buf:
        x_ref, wg_ref, wv_ref, bg_ref, bv_ref, cwg_ref, cwv_ref, cbg_ref, cbv_ref = refs[:n_in]
        bufs = (bg_ref, bv_ref)
    else:
        x_ref, wg_ref, wv_ref, cwg_ref, cwv_ref, cbg_ref, cbv_ref = refs[:n_in]
        bufs = (None, None)
    act_ref, tg_ref, tv_ref = refs[n_in:n_in + 3]
    if cast_w:
        wgo_ref, wvo_ref, wgb_ref, wvb_ref = refs[n_in + 3:n_in + 7]
        scratch = refs[n_in + 7:]

        @pl.when(jnp.logical_and(pl.program_id(1) == 0, pl.program_id(2) == 0))
        def _():
            for w_ref, wb_ref, wo_ref in ((wg_ref, wgb_ref, wgo_ref), (wv_ref, wvb_ref, wvo_ref)):
                wb = w_ref[...].astype(BF16)
                wb_ref[...] = wb
                wo_ref[...] = wb
    else:
        wgb_ref, wvb_ref = wg_ref, wv_ref
        scratch = refs[n_in + 3:]
    xbufs, carry_ref = scratch[:-1], scratch[-1]

    wsub = act_ref.shape[1] // n_sub
    cols = [slice(c * wsub, (c + 1) * wsub) for c in range(n_sub)]
    first = pl.program_id(2) == 0

    @pl.when(first)
    def _():
        for c in range(n_sub):
            for p in range(2):
                if has_buf:
                    xbufs[2 * c + p][0:halo, :] = bufs[p][:, cols[c]]
                else:
                    xbufs[2 * c + p][0:halo, :] = jnp.zeros((halo, wsub), F32)

    @pl.when(jnp.logical_not(first))
    def _():
        for c in range(n_sub):
            for p in range(2):
                xbufs[2 * c + p][0:halo, :] = carry_ref[p, :, cols[c]]

    x = x_ref[...]
    parts = ((wgb_ref, cwg_ref, cbg_ref, tg_ref), (wvb_ref, cwv_ref, cbv_ref, tv_ref))
    def conv_block(c, p):
        w_ref, cw_ref, cb_ref, t_ref = parts[p]
        xbuf_ref = xbufs[2 * c + p]
        xbuf_ref[halo:halo + tm, :] = jnp.dot(x, w_ref[:, cols[c]], preferred_element_type=F32)
        tail = xbuf_ref[tm:tm + halo, :]
        carry_ref[p, :, cols[c]] = tail
        if has_buf:
            for t in range(halo // shift):
                t_ref[:, t, cols[c]] = tail[t * shift:(t + 1) * shift]
        else:
            t_ref[0, :, cols[c]] = tail
        return _conv_taps(xbuf_ref, cw_ref, tm, shift, halo, CONV_FFN, cols[c]) + cb_ref[:, cols[c]]

    gate = [conv_block(c, 0) for c in range(n_sub)]
    for c in range(n_sub):
        act_ref[:, cols[c]] = (_silu(gate[c]) * conv_block(c, 1)).astype(act_ref.dtype)


def _ffn_up(x, w_up, conv_w, conv_b, bufs, n_seq, rows_per_seq, tm, tn, shift):
    m, d = x.shape
    cast_w = not isinstance(w_up, tuple)
    d_ff = w_up.shape[1] // 2 if cast_w else w_up[0].shape[1]
    nj = d_ff // tn
    has_buf = bufs is not None
    halo = (CONV_FFN - 1) * shift if has_buf else SUBLANES
    tiles = rows_per_seq // tm
    wspec = lambda off: pl.BlockSpec((d, tn), lambda j, s, i: (0, off + j))
    cwspec = lambda off: pl.BlockSpec((CONV_FFN, tn), lambda j, s, i: (0, off + j))
    cbspec = lambda off: pl.BlockSpec((1, tn), lambda j, s, i: (0, off + j))
    if has_buf:
        assert n_seq == 1
        tspec = pl.BlockSpec((shift, halo // shift, tn), lambda j, s, i: (0, 0, j))
        tshape = jax.ShapeDtypeStruct((shift, halo // shift, d_ff), F32)
    else:
        tspec = pl.BlockSpec((1, halo, tn), lambda j, s, i: (s, 0, j))
        tshape = jax.ShapeDtypeStruct((n_seq, halo, d_ff), F32)
    specs = [pl.BlockSpec((tm, d), lambda j, s, i: (s * tiles + i, 0)), wspec(0), wspec(nj if cast_w else 0)]
    args = [x, w_up, w_up] if cast_w else [x, w_up[0], w_up[1]]
    out_specs = [pl.BlockSpec((tm, tn), lambda j, s, i: (s * tiles + i, j)), tspec, tspec]
    out_shape = [jax.ShapeDtypeStruct((m, d_ff), BF16), tshape, tshape]
    scratch = []
    if cast_w:
        out_specs += [wspec(0), wspec(0)]
        out_shape += [jax.ShapeDtypeStruct((d, d_ff), BF16)] * 2
        scratch += [pltpu.VMEM((d, tn), BF16)] * 2
    if has_buf:
        specs += [pl.BlockSpec((halo, tn), lambda j, s, i, off=off: (s, off + j)) for off in (0, nj)]
        args += [bufs, bufs]
    specs += [cwspec(0), cwspec(nj), cbspec(0), cbspec(nj)]
    cb2 = conv_b.reshape(1, 2 * d_ff)
    args += [conv_w, conv_w, cb2, cb2]
    return pl.pallas_call(
        functools.partial(_ffn_up_kernel, tm=tm, shift=shift, halo=halo, has_buf=has_buf, n_sub=FFN_SUB,
                          cast_w=cast_w),
        grid=(nj, n_seq, tiles),
        in_specs=specs,
        out_specs=out_specs,
        out_shape=out_shape,
        scratch_shapes=scratch + [pltpu.VMEM((halo + tm, tn // FFN_SUB), F32)] * (2 * FFN_SUB)
                       + [pltpu.VMEM((2, halo, tn), F32)],
        compiler_params=_params(("arbitrary", "arbitrary", "arbitrary")),
        name="ffn_up",
    )(*args)


def _pad_lanes(vec, offset=0):
    out = jnp.zeros((1, LANES), F32)
    return out.at[0, offset:offset + vec.shape[0]].set(vec.astype(F32))


def _mixer_half(x2, lw, alpha, n_seq, rows_per_seq, shift, state, w_in_b):
    (w_in, conv_qkv_w, a_log, dt_bias, norm_g, gm_ln_g, gm_ln_b, gm_ws, gm_bs, gm_out_g, w_out,
     ln1_g, ln1_b) = lw
    m, d_model = x2.shape
    n_groups = gm_ws.shape[0]
    d_a = n_groups * GROUP
    d_qkv = conv_qkv_w.shape[1]
    d_b = d_qkv // 3
    n_heads = d_b // GROUP
    main_cols = 2 * d_a + d_qkv + d_b
    w_ab = jnp.pad(w_in[main_cols:].astype(BF16), ((0, LANES - 2 * n_heads), (0, 0)))
    if w_in_b is None:
        h, ab, w_in_b = _matmul(x2, w_in, w_ab, main_cols, min(m, PROJ_TM), PROJ_TN, emit_w=True)
    else:
        h, ab = _matmul(x2, w_in_b, w_ab, main_cols, min(m, PROJ_TM), PROJ_TN)
    alog, dtb, ng = _pad_lanes(a_log), _pad_lanes(dt_bias), norm_g.reshape(1, GROUP).astype(F32)

    if state is None:
        ya = _chunk_mlp_prompt(h, gm_ln_g, gm_ln_b, gm_ws, gm_bs, gm_out_g)
        v_rows = None
        q, k, v = _qkv_prep(h, conv_qkv_w, None, n_seq, rows_per_seq, QKV_TM, shift, 2 * d_a)
        yb, s_new = _delta_prompt(q, k, v, h, 2 * d_a + d_qkv, ab, alog, dtb, ng, n_seq, rows_per_seq, DELTA_SEQS)
    else:
        s_dn, buf_qkv = state
        n_pos = rows_per_seq // shift
        h3 = h.reshape(n_pos, shift, main_cols)
        ya3, v_rows = _chunk_mlp_sample(h3, gm_ln_g, gm_ln_b, gm_ws, gm_bs, gm_out_g)
        ya = ya3.reshape(m, d_a)
        q, k, v = _qkv_prep(h, conv_qkv_w, buf_qkv, n_seq, rows_per_seq, rows_per_seq, shift, 2 * d_a)
        r3 = lambda t: t.reshape(n_pos, shift, t.shape[-1])
        yb3, s_new = _delta_sample(r3(q), r3(k), r3(v), h3, 2 * d_a + d_qkv, r3(ab), alog, dtb, ng, s_dn,
                                   DELTA_SAMPLE_ROWS)
        yb = yb3.reshape(m, d_b)

    x1, x1b = _out_proj_ln(ya, yb, w_out, x2, ln1_g, ln1_b, alpha, min(m, OUT_TM))
    return x1, x1b, s_new, h, v_rows, w_in_b


def kernel(x_prompt, x_sample, state_dn, state_conv_qkv, state_conv_ffn, w_in, conv_qkv_w, dn_a_log, dn_dt_bias, dn_norm_g, gm_ln_g, gm_ln_b, gm_ws, gm_bs, gm_out_g, w_out, ln1_g, ln1_b, w_up, conv_ffn_w, conv_ffn_b, w_down, ln2_g, ln2_b):
    depth = w_in.shape[0]
    bp, lp, d_model = x_prompt.shape
    bs, ls, _ = x_sample.shape
    d_qkv = conv_qkv_w.shape[-1]
    d_a = gm_ws.shape[1] * GROUP
    alpha = (2.0 * depth) ** 0.25

    hp = x_prompt.reshape(bp * lp, d_model)
    hs = jnp.transpose(x_sample, (1, 0, 2)).reshape(ls * bs, d_model)
    outs = [[] for _ in range(7)]
    w_out = w_out.astype(BF16)
    w_in = jnp.swapaxes(w_in, 1, 2)
    for l in range(depth):
        lw = tuple(t[l] for t in (w_in, conv_qkv_w, dn_a_log, dn_dt_bias, dn_norm_g, gm_ln_g, gm_ln_b, gm_ws,
                                  gm_bs, gm_out_g, w_out, ln1_g, ln1_b))
        ffn_w = (conv_ffn_w[l], conv_ffn_b[l])
        ln2 = (ln2_g[l], ln2_b[l], alpha)
        n_pos = ls if l == depth - 1 else 0
        bq = jnp.transpose(state_conv_qkv[l], (1, 0, 2)).reshape((CONV_QKV - 1) * bs, d_qkv)
        bf = jnp.transpose(state_conv_ffn[l], (1, 0, 2)).reshape((CONV_FFN - 1) * bs, -1)
        x1_s, x1b_s, s_s, h_s, v_s, w_in_b = _mixer_half(hs, lw, alpha, 1, ls * bs, bs, (state_dn[l], bq), None)
        x1_p, x1b_p, s_p, h_p, _, _ = _mixer_half(hp, lw, alpha, bp, lp, 1, None, w_in_b)
        act_p, tg_p, tv_p, w_gate_b, w_val_b = _ffn_up(x1b_p, w_up[l], *ffn_w, None, bp, lp, FFN_TM, FFN_TN, 1)
        act_s, tg_s, tv_s = _ffn_up(x1b_s, (w_gate_b, w_val_b), *ffn_w, bf, 1, ls * bs, ls * bs, FFN_TN, bs)
        hs, w_down_b = _down_ln(act_s, w_down[l], x1_s, *ln2, min(ls * bs, DOWN_EMIT_TM), DOWN_EMIT_TK,
                                emit_w=True, n_pos=n_pos)
        hp = _down_ln(act_p, w_down_b, x1_p, *ln2, min(bp * lp, DOWN_TM), DOWN_TK)
        h_s3 = h_s.reshape(ls, bs, -1)
        cq_s = jnp.transpose(h_s3[ls - (CONV_QKV - 1):, :, 2 * d_a:2 * d_a + d_qkv], (1, 0, 2))
        cf_s = jnp.concatenate([tg_s, tv_s], axis=-1)
        h_p3 = h_p.reshape(bp, lp, -1)
        cq_p = h_p3[:, lp - (CONV_QKV - 1):, 2 * d_a:2 * d_a + d_qkv]
        n_tail = tg_p.shape[1]
        cf_p = jnp.concatenate([tg_p, tv_p], axis=-1)[:, n_tail - (CONV_FFN - 1):]
        for acc, val in zip(outs, (s_p, cq_p, cf_p, s_s, cq_s, cf_s, v_s)):
            acc.append(val)
    return (hp.reshape(bp, lp, d_model), hs) + tuple(jnp.stack(o) for o in outs)
```

```python
import functools

import jax
import jax.numpy as jnp
from jax import lax
from jax.experimental import pallas as pl
from jax.experimental.pallas import tpu as pltpu

F32 = jnp.float32
BF16 = jnp.bfloat16

LANES = 128
SUBLANES = 8
VMEM_LIMIT = 56 * 1024 * 1024
DOWN_VMEM_LIMIT = 62 * 1024 * 1024

GROUP = 128
MLP_CHUNK = 128
DN_CHUNK = 64
CONV_QKV = 4
CONV_FFN = 3
CMLP_CHUNKS = 8
EPS = 1e-5

PROJ_TM, PROJ_TN = 1024, 1024
QKV_TM = 512
DELTA_SEQS = 4
DELTA_SAMPLE_ROWS = 16
OUT_TM = 512
FFN_TM, FFN_TN = 1024, 512
DOWN_TM, DOWN_TK = 1024, 1408
DOWN_EMIT_TM, DOWN_EMIT_TK = 512, 512

_NT = (((1,), (1,)), ((), ()))
_TN = (((0,), (0,)), ((), ()))


def _params(sem, vmem_limit=VMEM_LIMIT):
    return pltpu.CompilerParams(dimension_semantics=sem, vmem_limit_bytes=vmem_limit)


def _bdot(a, b):
    return jnp.dot(a.astype(BF16), b.astype(BF16), preferred_element_type=F32)


def _gelu(x):
    return 0.5 * x * (1.0 + lax.erf(x * (0.5 ** 0.5)))


def _silu(x):
    return x * jax.nn.sigmoid(x)


def _layer_norm(x, g, b):
    mu = jnp.mean(x, axis=-1, keepdims=True)
    xc = x - mu
    var = jnp.mean(xc * xc, axis=-1, keepdims=True)
    return xc * lax.rsqrt(var + EPS) * g + b


def _rms_norm(x, g):
    ms = jnp.mean(x * x, axis=-1, keepdims=True)
    return x * lax.rsqrt(ms + EPS) * g


def _mm_kernel(x_ref, w_ref, we_ref, o_ref, e_ref, *rest):
    xb_ref = rest[-1]

    @pl.when(pl.program_id(1) == 0)
    def _():
        xb = x_ref[...].astype(BF16)
        xb_ref[...] = xb
        e_ref[...] = lax.dot_general(xb, we_ref[...], _NT, preferred_element_type=F32)

    wb = w_ref[...].astype(BF16)
    if len(rest) == 2:
        rest[0][...] = wb
    o_ref[...] = lax.dot_general(xb_ref[...], wb, _NT, preferred_element_type=F32)


def _matmul(x, wt, wt_extra, n_cols, tm, tn, emit_w=False):
    m, k = x.shape
    ne = wt_extra.shape[0]
    out_specs = [pl.BlockSpec((tm, tn), lambda i, j: (i, j)), pl.BlockSpec((tm, ne), lambda i, j: (i, 0))]
    out_shape = [jax.ShapeDtypeStruct((m, n_cols), F32), jax.ShapeDtypeStruct((m, ne), F32)]
    if emit_w:
        out_specs.append(pl.BlockSpec((tn, k), lambda i, j: (j, 0)))
        out_shape.append(jax.ShapeDtypeStruct((n_cols, k), BF16))
    return pl.pallas_call(
        _mm_kernel,
        grid=(m // tm, n_cols // tn),
        in_specs=[pl.BlockSpec((tm, k), lambda i, j: (i, 0)),
                  pl.BlockSpec((tn, k), lambda i, j: (j, 0)),
                  pl.BlockSpec((ne, k), lambda i, j: (0, 0))],
        out_specs=out_specs,
        out_shape=out_shape,
        scratch_shapes=[pltpu.VMEM((tm, k), BF16)],
        compiler_params=_params(("arbitrary", "arbitrary")),
        name="proj",
    )(x, wt, wt_extra)


def _cmlp_prompt_kernel(u_ref, v_ref, lng_ref, lnb_ref, ws_ref, bst_ref, og_ref, ya_ref, *, n_groups, n_chunks):
    row = lax.broadcasted_iota(jnp.int32, (MLP_CHUNK, MLP_CHUNK), 0)
    col = lax.broadcasted_iota(jnp.int32, (MLP_CHUNK, MLP_CHUNK), 1)
    cg = [(c, g) for c in range(n_chunks) for g in range(n_groups)]
    rows = lambda c: slice(c * MLP_CHUNK, (c + 1) * MLP_CHUNK)
    cols = lambda g: slice(g * GROUP, (g + 1) * GROUP)
    w = [jnp.where(row >= col, ws_ref[g], 0.0).astype(BF16) for g in range(n_groups)]
    n = range(len(cg))
    gv = [_gelu(v_ref[rows(c), cols(g)]) for c, g in cg]
    mu = [jnp.mean(gv[i], axis=-1, keepdims=True) for i in n]
    xc = [gv[i] - mu[i] for i in n]
    var = [jnp.mean(xc[i] * xc[i], axis=-1, keepdims=True) for i in n]
    vh = [xc[i] * lax.rsqrt(var[i] + EPS) * lng_ref[:, cols(g)] + lnb_ref[:, cols(g)] for i, (c, g) in enumerate(cg)]
    mixed = [jnp.dot(w[g], vh[i].astype(BF16), preferred_element_type=F32) + bst_ref[:, g:g + 1]
             for i, (c, g) in enumerate(cg)]
    y = [_gelu(u_ref[rows(c), cols(g)]) * mixed[i] for i, (c, g) in enumerate(cg)]
    ms = [jnp.mean(y[i] * y[i], axis=-1, keepdims=True) for i in n]
    for i, (c, g) in enumerate(cg):
        ya_ref[rows(c), cols(g)] = (y[i] * lax.rsqrt(ms[i] + EPS) * og_ref[:, cols(g)]).astype(ya_ref.dtype)


def _chunk_mlp_prompt(h, ln_g, ln_b, ws, bs, out_g):
    m = h.shape[0]
    n_groups, d_a = ws.shape[0], ws.shape[0] * GROUP
    vec = pl.BlockSpec((1, d_a), lambda i: (0, 0))
    tm = CMLP_CHUNKS * MLP_CHUNK
    return pl.pallas_call(
        functools.partial(_cmlp_prompt_kernel, n_groups=n_groups, n_chunks=CMLP_CHUNKS),
        grid=(m // tm,),
        in_specs=[pl.BlockSpec((tm, d_a), lambda i: (i, 0)),
                  pl.BlockSpec((tm, d_a), lambda i: (i, 1)),
                  vec, vec,
                  pl.BlockSpec((n_groups, MLP_CHUNK, MLP_CHUNK), lambda i: (0, 0, 0)),
                  pl.BlockSpec((MLP_CHUNK, n_groups), lambda i: (0, 0)),
                  vec],
        out_specs=pl.BlockSpec((tm, d_a), lambda i: (i, 0)),
        out_shape=jax.ShapeDtypeStruct((m, d_a), BF16),
        compiler_params=_params(("arbitrary",)),
        name="chunk_mlp_prompt",
    )(h, h, ln_g.reshape(1, d_a), ln_b.reshape(1, d_a), ws, bs.T, out_g.reshape(1, d_a))


def _cmlp_sample_kernel(u_ref, v_ref, lng_ref, lnb_ref, wv_ref, bv_ref, og_ref, ya_ref, vr_ref, *, n_pos):
    vh = []
    for t in range(n_pos):
        vt = _layer_norm(_gelu(v_ref[t]), lng_ref[0], lnb_ref[0])
        vr_ref[:, t, :] = vt
        vh.append(vt)
    for t in range(n_pos):
        mixed = wv_ref[t, 0:1, :] * vh[0]
        for s in range(1, t + 1):
            mixed = mixed + wv_ref[t, s:s + 1, :] * vh[s]
        mixed = mixed + bv_ref[0, t:t + 1, :]
        y = _gelu(u_ref[t]) * mixed
        ya_ref[t] = _rms_norm(y, og_ref[0]).astype(ya_ref.dtype)


def _chunk_mlp_sample(h3, ln_g, ln_b, ws, bs, out_g):
    n_pos, nb, _ = h3.shape
    n_groups = ws.shape[0]
    d_a = n_groups * GROUP
    wv = jnp.repeat(jnp.transpose(ws[:, :n_pos, :n_pos], (1, 2, 0)), GROUP, axis=-1)
    bv = jnp.repeat(bs[:, :n_pos].T, GROUP, axis=-1).reshape(1, n_pos, d_a)
    vec = pl.BlockSpec((1, 1, GROUP), lambda g: (0, 0, g))
    act = pl.BlockSpec((n_pos, nb, GROUP), lambda g: (0, 0, g))
    return pl.pallas_call(
        functools.partial(_cmlp_sample_kernel, n_pos=n_pos),
        grid=(n_groups,),
        in_specs=[act,
                  pl.BlockSpec((n_pos, nb, GROUP), lambda g: (0, 0, n_groups + g)),
                  vec, vec,
                  pl.BlockSpec((n_pos, n_pos, GROUP), lambda g: (0, 0, g)),
                  pl.BlockSpec((1, n_pos, GROUP), lambda g: (0, 0, g)),
                  vec],
        out_specs=[act, pl.BlockSpec((nb, n_pos, GROUP), lambda g: (0, 0, g))],
        out_shape=[jax.ShapeDtypeStruct((n_pos, nb, d_a), BF16),
                   jax.ShapeDtypeStruct((nb, n_pos, d_a), F32)],
        compiler_params=_params(("arbitrary",)),
        name="chunk_mlp_sample",
    )(h3, h3, ln_g.reshape(1, 1, d_a), ln_b.reshape(1, 1, d_a), wv, bv, out_g.reshape(1, 1, d_a))


def _conv_taps(xbuf_ref, w_ref, tm, shift, halo, width):
    acc = None
    for j in range(width):
        off = halo - (width - 1 - j) * shift
        term = w_ref[j:j + 1, :] * xbuf_ref[off:off + tm, :]
        acc = term if acc is None else acc + term
    return acc


def _qkv_prep_kernel(*refs, tm, shift, halo, n_heads, has_buf):
    if has_buf:
        (xq_ref, xk_ref, xv_ref, bq_ref, bk_ref, bv_ref, wq_ref, wk_ref, wv_ref,
         q_ref, k_ref, v_ref, xbuf_ref, carry_ref) = refs
        bufs = (bq_ref, bk_ref, bv_ref)
    else:
        (xq_ref, xk_ref, xv_ref, wq_ref, wk_ref, wv_ref,
         q_ref, k_ref, v_ref, xbuf_ref, carry_ref) = refs
        bufs = (None, None, None)
    first = pl.program_id(1) == 0
    parts = ((xq_ref, wq_ref, q_ref, GROUP ** -0.5), (xk_ref, wk_ref, k_ref, 1.0), (xv_ref, wv_ref, v_ref, None))
    for p, (x_ref, w_ref, o_ref, scale) in enumerate(parts):
        @pl.when(first)
        def _(p=p):
            if has_buf:
                xbuf_ref[0:halo, :] = bufs[p][...]
            else:
                xbuf_ref[0:halo, :] = jnp.zeros((halo, xbuf_ref.shape[1]), F32)

        @pl.when(jnp.logical_not(first))
        def _(p=p):
            xbuf_ref[0:halo, :] = carry_ref[p]

        xbuf_ref[halo:halo + tm, :] = x_ref[...]
        carry_ref[p] = xbuf_ref[tm:tm + halo, :]
        y = _silu(_conv_taps(xbuf_ref, w_ref, tm, shift, halo, CONV_QKV))
        if scale is None:
            o_ref[...] = y
        else:
            for h in range(n_heads):
                sl = slice(h * GROUP, (h + 1) * GROUP)
                yh = y[:, sl]
                inv = lax.rsqrt(jnp.sum(yh * yh, axis=-1, keepdims=True) + 1e-6)
                o_ref[:, sl] = yh * (inv * scale if scale != 1.0 else inv)


def _qkv_prep(h, conv_w, bufs, n_seq, rows_per_seq, tm, shift, col0):
    d_b = conv_w.shape[1] // 3
    n_heads = d_b // GROUP
    has_buf = bufs is not None
    halo = (CONV_QKV - 1) * shift if has_buf else SUBLANES
    tiles = rows_per_seq // tm
    cb = col0 // d_b
    x_specs = [pl.BlockSpec((tm, d_b), lambda s, i, c=c: (s * tiles + i, cb + c)) for c in range(3)]
    w_specs = [pl.BlockSpec((CONV_QKV, d_b), lambda s, i, c=c: (0, c)) for c in range(3)]
    o_spec = pl.BlockSpec((tm, d_b), lambda s, i: (s * tiles + i, 0))
    args, specs = [h, h, h], list(x_specs)
    if has_buf:
        specs += [pl.BlockSpec((halo, d_b), lambda s, i, c=c: (s, c)) for c in range(3)]
        args += [bufs, bufs, bufs]
    specs += w_specs
    args += [conv_w, conv_w, conv_w]
    m = n_seq * rows_per_seq
    return pl.pallas_call(
        functools.partial(_qkv_prep_kernel, tm=tm, shift=shift, halo=halo, n_heads=n_heads, has_buf=has_buf),
        grid=(n_seq, tiles),
        in_specs=specs,
        out_specs=[o_spec, o_spec, o_spec],
        out_shape=[jax.ShapeDtypeStruct((m, d_b), F32)] * 3,
        scratch_shapes=[pltpu.VMEM((halo + tm, d_b), F32), pltpu.VMEM((3, halo, d_b), F32)],
        compiler_params=_params(("arbitrary", "arbitrary")),
        name="qkv_prep",
    )(*args)


def _gates(ab, alog, dtb):
    g = -jnp.exp(alog) * jax.nn.softplus(ab + dtb)
    beta = jax.nn.sigmoid(ab)
    return g, beta


def _delta_prompt_kernel(q_ref, k_ref, v_ref, z_ref, ab_ref, alog_ref, dtb_ref, ng_ref,
                         yb_ref, sout_ref, s_ref, *, n_heads, n_chunks, ns):
    c = DN_CHUNK
    ci = pl.program_id(1)

    @pl.when(ci == 0)
    def _():
        s_ref[...] = jnp.zeros(s_ref.shape, F32)

    row = lax.broadcasted_iota(jnp.int32, (c, c), 0)
    col = lax.broadcasted_iota(jnp.int32, (c, c), 1)
    causal = row >= col
    strict = row > col
    eye = (row == col).astype(F32)
    tril = causal.astype(F32)
    n_double = (c - 1).bit_length() - 1
    ch = [(s, h) for s in range(ns) for h in range(n_heads)]
    sl = lambda h: slice(h * GROUP, (h + 1) * GROUP)
    gates = [_gates(ab_ref[s], alog_ref[...], dtb_ref[...]) for s in range(ns)]
    gc_s = [jnp.dot(tril, gates[s][0], precision=lax.Precision.HIGHEST, preferred_element_type=F32)
            for s in range(ns)]
    gct_s = [g.T for g in gc_s]
    q = [q_ref[s, :, sl(h)] for s, h in ch]
    k = [k_ref[s, :, sl(h)] for s, h in ch]
    gcol = [gc_s[s][:, h:h + 1] for s, h in ch]
    beta = [gates[s][1][:, n_heads + h:n_heads + h + 1] for s, h in ch]
    n = range(len(ch))
    decay = [jnp.exp(jnp.where(causal, gcol[i] - gct_s[s][h:h + 1, :], -jnp.inf)) for i, (s, h) in enumerate(ch)]
    kb = [k[i] * beta[i] for i in n]
    kbf = [k[i].astype(BF16) for i in n]
    kq = [lax.dot_general(jnp.concatenate([kb[i], q[i]], axis=0).astype(BF16), kbf[i], _NT,
                          preferred_element_type=F32) for i in n]
    qk = [kq[i][c:] * decay[i] for i in n]
    y = [jnp.where(strict, -(kq[i][:c] * decay[i]), 0.0) for i in n]
    x = [eye + y[i] for i in n]
    y = [_bdot(y[i], y[i]) for i in n]
    for _ in range(n_double - 1):
        t = [_bdot(jnp.concatenate([x[i], y[i]], axis=0), y[i]) for i in n]
        x = [x[i] + t[i][:c] for i in n]
        y = [t[i][c:] for i in n]
    x = [x[i] + _bdot(x[i], y[i]) for i in n]
    egc = [jnp.exp(gcol[i]) for i in n]
    sol = [_bdot(x[i], jnp.concatenate([v_ref[s, :, sl(h)] * beta[i], kb[i] * egc[i]], axis=1))
           for i, (s, h) in enumerate(ch)]
    s_old = [s_ref[s, h] for s, h in ch]
    r = [_bdot(jnp.concatenate([sol[i][:, GROUP:], q[i] * egc[i]], axis=0), s_old[i]) for i in n]
    v_new = [sol[i][:, :GROUP] - r[i][:c] for i in n]
    o = [r[i][c:] + _bdot(qk[i], v_new[i]) for i in n]
    gl = [gc_s[s][c - 1:c, h:h + 1] for s, h in ch]
    upd = [lax.dot_general((k[i] * jnp.exp(gl[i] - gcol[i])).astype(BF16), v_new[i].astype(BF16), _TN,
                           preferred_element_type=F32) for i in n]
    ms = [jnp.mean(o[i] * o[i], axis=-1, keepdims=True) for i in n]
    for i, (s, h) in enumerate(ch):
        s_ref[s, h] = s_old[i] * jnp.exp(gl[i]) + upd[i]
        on = o[i] * lax.rsqrt(ms[i] + EPS) * ng_ref[...]
        yb_ref[s, :, sl(h)] = (on * _silu(z_ref[s, :, sl(h)])).astype(yb_ref.dtype)

    @pl.when(ci == n_chunks - 1)
    def _():
        sout_ref[...] = s_ref[...]


def _delta_prompt(q, k, v, h, z_col0, ab, alog, dtb, norm_g, n_seq, rows_per_seq, ns):
    d_b = q.shape[1]
    n_heads = d_b // GROUP
    n_chunks = rows_per_seq // DN_CHUNK
    r3 = lambda t: t.reshape(n_seq, rows_per_seq, t.shape[-1])
    act = pl.BlockSpec((ns, DN_CHUNK, d_b), lambda s, i: (s, i, 0))
    vec = pl.BlockSpec((1, LANES), lambda s, i: (0, 0))
    zb = z_col0 // d_b
    yb, s_new = pl.pallas_call(
        functools.partial(_delta_prompt_kernel, n_heads=n_heads, n_chunks=n_chunks, ns=ns),
        grid=(n_seq // ns, n_chunks),
        in_specs=[act, act, act,
                  pl.BlockSpec((ns, DN_CHUNK, d_b), lambda s, i: (s, i, zb)),
                  pl.BlockSpec((ns, DN_CHUNK, LANES), lambda s, i: (s, i, 0)),
                  vec, vec, vec],
        out_specs=[act, pl.BlockSpec((ns, n_heads, GROUP, GROUP), lambda s, i: (s, 0, 0, 0))],
        out_shape=[jax.ShapeDtypeStruct((n_seq, rows_per_seq, d_b), BF16),
                   jax.ShapeDtypeStruct((n_seq, n_heads, GROUP, GROUP), F32)],
        scratch_shapes=[pltpu.VMEM((ns, n_heads, GROUP, GROUP), F32)],
        compiler_params=_params(("arbitrary", "arbitrary")),
        name="delta_prompt",
    )(r3(q), r3(k), r3(v), r3(h), r3(ab), alog, dtb, norm_g)
    return yb.reshape(n_seq * rows_per_seq, d_b), s_new


def _delta_sample_kernel(q_ref, k_ref, v_ref, z_ref, ab_ref, alog_ref, dtb_ref, ng_ref, s0_ref,
                         yb_ref, s1_ref, lhs_ref, res_ref, kt_ref, vn_ref, gl_ref, *, n_heads, n_pos, bb):
    nt = n_pos
    hs = range(n_heads)
    sls = [slice(h * GROUP, (h + 1) * GROUP) for h in hs]
    gates = [_gates(ab_ref[t], alog_ref[...], dtb_ref[...]) for t in range(nt)]
    zeros_pad = jnp.zeros((bb, GROUP), F32)
    ts = range(nt)
    q = [[q_ref[t, :, sls[h]] for t in ts] for h in hs]
    k = [[k_ref[t, :, sls[h]] for t in ts] for h in hs]
    beta = [[gates[t][1][:, n_heads + h:n_heads + h + 1] for t in ts] for h in hs]
    gc = []
    for h in hs:
        g = [gates[t][0][:, h:h + 1] for t in ts]
        for t in range(1, nt):
            g[t] = g[t - 1] + g[t]
        gc.append(g)
    kb = [[k[h][t] * beta[h][t] for t in ts] for h in hs]
    kk = [[[jnp.sum(kb[h][t] * k[h][s], axis=-1, keepdims=True) for s in range(t)] for t in ts] for h in hs]
    qk = [[[jnp.sum(q[h][t] * k[h][s], axis=-1, keepdims=True) for s in range(t + 1)] for t in ts] for h in hs]
    dec = [[[jnp.exp(gc[h][t] - gc[h][s]) for s in range(t + 1)] for t in ts] for h in hs]
    a = [[[kk[h][t][s] * dec[h][t][s] for s in range(t)] for t in ts] for h in hs]
    qk = [[[qk[h][t][s] * dec[h][t][s] for s in range(t + 1)] for t in ts] for h in hs]
    egc = [[jnp.exp(gc[h][t]) for t in ts] for h in hs]
    u_blk = [[v_ref[t, :, sls[h]] * beta[h][t] for t in ts] for h in hs]
    w_blk = [[kb[h][t] * egc[h][t] for t in ts] for h in hs]
    for t in ts:
        for h in hs:
            for s in range(t):
                u_blk[h][t] = u_blk[h][t] - a[h][t][s] * u_blk[h][s]
                w_blk[h][t] = w_blk[h][t] - a[h][t][s] * w_blk[h][s]
    for h in hs:
        for t in ts:
            lhs_ref[h, :, t, :] = w_blk[h][t]
            lhs_ref[h, :, nt + t, :] = q[h][t] * egc[h][t]
        gl_ref[h] = jnp.broadcast_to(egc[h][nt - 1], (bb, GROUP))

    for h in hs:
        for b in range(bb):
            res_ref[h, b] = _bdot(lhs_ref[h, b], s0_ref[b, h])

    v_new = [[u_blk[h][t] - res_ref[h, :, t, :] for t in ts] for h in hs]
    o = [[res_ref[h, :, nt + t, :] for t in ts] for h in hs]
    for h in hs:
        for t in ts:
            for s in range(t + 1):
                o[h][t] = o[h][t] + qk[h][t][s] * v_new[h][s]
    ms = [[jnp.mean(o[h][t] * o[h][t], axis=-1, keepdims=True) for t in ts] for h in hs]
    for h in hs:
        gl = gc[h][nt - 1]
        for t in ts:
            on = o[h][t] * lax.rsqrt(ms[h][t] + EPS) * ng_ref[...]
            yb_ref[t, :, sls[h]] = (on * _silu(z_ref[t, :, sls[h]])).astype(yb_ref.dtype)
            kt_ref[h, :, t, :] = k[h][t] * jnp.exp(gl - gc[h][t])
            vn_ref[h, :, t, :] = v_new[h][t]
            kt_ref[h, :, nt + t, :] = zeros_pad
            vn_ref[h, :, nt + t, :] = zeros_pad

    for h in hs:
        for b in range(bb):
            upd = lax.dot_general(kt_ref[h, b].astype(BF16), vn_ref[h, b].astype(BF16), _TN,
                                  preferred_element_type=F32)
            s1_ref[b, h] = s0_ref[b, h] * gl_ref[h, b:b + 1, :] + upd


def _delta_sample(q, k, v, h3, z_col0, ab, alog, dtb, norm_g, s0, bb):
    n_pos, nb, d_b = q.shape
    n_heads = d_b // GROUP
    assert 2 * n_pos == SUBLANES, "one state read packs [w | q] rows of a sequence into one 8-row tile"
    zb = z_col0 // d_b
    act = pl.BlockSpec((n_pos, bb, d_b), lambda i: (0, i, 0))
    vec = pl.BlockSpec((1, LANES), lambda i: (0, 0))
    st = pl.BlockSpec((bb, n_heads, GROUP, GROUP), lambda i: (i, 0, 0, 0))
    tile = pltpu.VMEM((n_heads, bb, SUBLANES, GROUP), F32)
    return pl.pallas_call(
        functools.partial(_delta_sample_kernel, n_heads=n_heads, n_pos=n_pos, bb=bb),
        grid=(nb // bb,),
        in_specs=[act, act, act,
                  pl.BlockSpec((n_pos, bb, d_b), lambda i: (0, i, zb)),
                  pl.BlockSpec((n_pos, bb, LANES), lambda i: (0, i, 0)),
                  vec, vec, vec, st],
        out_specs=[act, st],
        out_shape=[jax.ShapeDtypeStruct((n_pos, nb, d_b), BF16),
                   jax.ShapeDtypeStruct(s0.shape, F32)],
        scratch_shapes=[tile, tile, tile, tile, pltpu.VMEM((n_heads, bb, GROUP), F32)],
        compiler_params=_params(("arbitrary",)),
        name="delta_sample",
    )(q, k, v, h3, ab, alog, dtb, norm_g, s0)


def _out_proj_kernel(ya_ref, yb_ref, w_ref, r_ref, g_ref, b_ref, o_ref, ob_ref, z_ref, *, n_tiles, d_a, alpha):
    i = pl.program_id(0)

    @pl.when(i == 0)
    def _():
        z_ref[...] = jnp.zeros(z_ref.shape, F32)

    def emit_ln():
        y = _layer_norm(z_ref[...], g_ref[...], b_ref[...])
        o_ref[...] = y
        ob_ref[...] = y.astype(ob_ref.dtype)

    @pl.when(i < n_tiles)
    def _():
        emit_ln()
        acc = (jnp.dot(ya_ref[...], w_ref[0:d_a, :], preferred_element_type=F32)
               + jnp.dot(yb_ref[...], w_ref[d_a:, :], preferred_element_type=F32))
        z_ref[...] = alpha * r_ref[...] + acc

    @pl.when(i == n_tiles)
    def _():
        emit_ln()


def _out_proj_ln(ya, yb, w, resid, g, b, alpha, tm):
    m, d_a = ya.shape
    k, d = w.shape
    n_tiles = m // tm
    cur = lambda i: (jnp.minimum(i, n_tiles - 1), 0)
    prev = lambda i: (jnp.maximum(i - 1, 0), 0)
    vec = pl.BlockSpec((1, d), lambda i: (0, 0))
    return pl.pallas_call(
        functools.partial(_out_proj_kernel, n_tiles=n_tiles, d_a=d_a, alpha=alpha),
        grid=(n_tiles + 1,),
        in_specs=[pl.BlockSpec((tm, d_a), cur),
                  pl.BlockSpec((tm, k - d_a), cur),
                  pl.BlockSpec((k, d), lambda i: (0, 0), pipeline_mode=pl.Buffered(1)),
                  pl.BlockSpec((tm, d), cur), vec, vec],
        out_specs=[pl.BlockSpec((tm, d), prev), pl.BlockSpec((tm, d), prev)],
        out_shape=[jax.ShapeDtypeStruct((m, d), F32), jax.ShapeDtypeStruct((m, d), BF16)],
        scratch_shapes=[pltpu.VMEM((tm, d), F32)],
        compiler_params=_params(("arbitrary",)),
        name="out_proj_ln1",
    )(ya, yb, w, resid, g.reshape(1, d), b.reshape(1, d))


def _down_kernel(x_ref, w_ref, r_ref, g_ref, b_ref, o_ref, *rest, n_tiles, nk, nc, alpha, n_pos):
    z_ref = rest[-1]
    cw = z_ref.shape[-1]
    slab = lambda c: slice(c * cw, (c + 1) * cw)
    i, kk = pl.program_id(0), pl.program_id(1)

    def emit_ln():
        z = [z_ref[c] for c in range(nc)]
        inv_d = 1.0 / (nc * cw)
        mu = sum(jnp.sum(zc, axis=-1, keepdims=True) for zc in z) * inv_d
        xc = [zc - mu for zc in z]
        var = sum(jnp.sum(c * c, axis=-1, keepdims=True) for c in xc) * inv_d
        rs = lax.rsqrt(var + EPS)
        for c in range(nc):
            y = xc[c] * rs * g_ref[:, slab(c)] + b_ref[:, slab(c)]
            if n_pos:
                nb = y.shape[0] // n_pos
                for t in range(n_pos):
                    o_ref[:, t, slab(c)] = y[t * nb:(t + 1) * nb]
            else:
                o_ref[:, slab(c)] = y

    @pl.when(jnp.logical_and(i == 0, kk == 0))
    def _():
        z_ref[...] = jnp.zeros(z_ref.shape, F32)

    wb = w_ref[...].astype(BF16)
    if len(rest) == 2:
        rest[0][...] = wb

    @pl.when(kk == 0)
    def _():
        emit_ln()
        part = jnp.dot(x_ref[...], wb, preferred_element_type=F32)
        for c in range(nc):
            if nc == nk:
                z_ref[c] = part[:, slab(c)] + alpha * r_ref[...] if c == 0 else part[:, slab(c)]
            else:
                z_ref[c] = part[:, slab(c)] + alpha * r_ref[:, slab(c)]

    @pl.when(jnp.logical_and(kk > 0, i < n_tiles))
    def _():
        part = jnp.dot(x_ref[...], wb, preferred_element_type=F32)
        for c in range(nc):
            z_ref[c] += part[:, slab(c)]
        if nc == nk:
            z_ref[kk] += alpha * r_ref[...]


def _down_ln(x, w, resid, g, b, alpha, tm, tk, emit_w=False, n_pos=0):
    m, kdim = x.shape
    d = w.shape[1]
    nk = kdim // tk
    n_tiles = m // tm
    nc = nk if (d % nk == 0 and (d // nk) % LANES == 0) else 1
    row = lambda i: jnp.minimum(i, n_tiles - 1)
    kt = lambda i, kk: jnp.where(i == n_tiles, nk - 1, kk)
    vec = pl.BlockSpec((1, d), lambda i, kk: (0, 0))
    wspec = pl.BlockSpec((tk, d), lambda i, kk: (kt(i, kk), 0))
    if nc == nk:
        rspec = pl.BlockSpec((tm, d // nc), lambda i, kk: (row(i), kt(i, kk)))
    else:
        rspec = pl.BlockSpec((tm, d), lambda i, kk: (row(i), 0))
    if n_pos:
        assert n_tiles == 1
        out_specs = [pl.BlockSpec((m // n_pos, n_pos, d), lambda i, kk: (0, 0, 0))]
        out_shape = [jax.ShapeDtypeStruct((m // n_pos, n_pos, d), F32)]
    else:
        out_specs = [pl.BlockSpec((tm, d), lambda i, kk: (jnp.where(kk == 0, jnp.maximum(i - 1, 0), row(i)), 0))]
        out_shape = [jax.ShapeDtypeStruct((m, d), F32)]
    if emit_w:
        out_specs.append(wspec)
        out_shape.append(jax.ShapeDtypeStruct(w.shape, BF16))
    outs = pl.pallas_call(
        functools.partial(_down_kernel, n_tiles=n_tiles, nk=nk, nc=nc, alpha=alpha, n_pos=n_pos),
        grid=(n_tiles + 1, nk),
        in_specs=[pl.BlockSpec((tm, tk), lambda i, kk: (row(i), kt(i, kk))), wspec, rspec, vec, vec],
        out_specs=out_specs,
        out_shape=out_shape,
        scratch_shapes=[pltpu.VMEM((nc, tm, d // nc), F32)],
        compiler_params=_params(("arbitrary", "arbitrary"), DOWN_VMEM_LIMIT),
        name="down_ln2",
    )(x, w, resid, g.reshape(1, d), b.reshape(1, d))
    return tuple(outs) if emit_w else outs[0]


def _ffn_up_kernel(x_ref, wg_ref, wv_ref, cwg_ref, cwv_ref, cbg_ref, cbv_ref,
                   act_ref, tg_ref, tv_ref, wgo_ref, wvo_ref, wgb_ref, wvb_ref, xg_ref, xv_ref, carry_ref, *, tm):
    halo = SUBLANES
    first = pl.program_id(2) == 0

    @pl.when(jnp.logical_and(pl.program_id(1) == 0, first))
    def _():
        for w_ref, wb_ref, wo_ref in ((wg_ref, wgb_ref, wgo_ref), (wv_ref, wvb_ref, wvo_ref)):
            wb = w_ref[...].astype(BF16)
            wb_ref[...] = wb
            wo_ref[...] = wb

    @pl.when(first)
    def _():
        xg_ref[0:halo, :] = jnp.zeros((halo, xg_ref.shape[1]), F32)
        xv_ref[0:halo, :] = jnp.zeros((halo, xv_ref.shape[1]), F32)

    @pl.when(jnp.logical_not(first))
    def _():
        xg_ref[0:halo, :] = carry_ref[0]
        xv_ref[0:halo, :] = carry_ref[1]

    x = x_ref[...]
    conv = []
    parts = ((wgb_ref, cwg_ref, cbg_ref, tg_ref, xg_ref), (wvb_ref, cwv_ref, cbv_ref, tv_ref, xv_ref))
    for p, (w_ref, cw_ref, cb_ref, t_ref, xbuf_ref) in enumerate(parts):
        xbuf_ref[halo:halo + tm, :] = jnp.dot(x, w_ref[...], preferred_element_type=F32)
        tail = xbuf_ref[tm:tm + halo, :]
        carry_ref[p] = tail
        t_ref[0] = tail
        conv.append(_conv_taps(xbuf_ref, cw_ref, tm, 1, halo, CONV_FFN) + cb_ref[...])
    act_ref[...] = (_silu(conv[0]) * conv[1]).astype(act_ref.dtype)


def _ffn_up_carried_kernel(x_ref, wg_ref, wv_ref, buf_ref, cw_ref, cb_ref, act_ref, t_ref, xbuf_ref, gate_ref,
                           *, shift):
    p, j = pl.program_id(0), pl.program_id(1)
    tm, halo = x_ref.shape[0], buf_ref.shape[0]
    xbuf_ref[0:halo, :] = buf_ref[...]

    def conv_of(w_ref):
        xbuf_ref[halo:halo + tm, :] = jnp.dot(x_ref[...], w_ref[...], preferred_element_type=F32)
        tail = xbuf_ref[tm:tm + halo, :]
        for t in range(halo // shift):
            t_ref[:, t, :] = tail[t * shift:(t + 1) * shift]
        return _conv_taps(xbuf_ref, cw_ref, tm, shift, halo, CONV_FFN) + cb_ref[...]

    @pl.when(p == 0)
    def _():
        gate_ref[j] = conv_of(wg_ref)

    @pl.when(p == 1)
    def _():
        act_ref[...] = (_silu(gate_ref[j]) * conv_of(wv_ref)).astype(act_ref.dtype)


def _ffn_up_carried(x, w_gate_b, w_val_b, conv_w, conv_b, bufs, tn, shift):
    m, d = x.shape
    d_ff = w_gate_b.shape[1]
    nj = d_ff // tn
    halo = (CONV_FFN - 1) * shift
    col = lambda p, j: (0, p * nj + j)
    return pl.pallas_call(
        functools.partial(_ffn_up_carried_kernel, shift=shift),
        grid=(2, nj),
        in_specs=[pl.BlockSpec((m, d), lambda p, j: (0, 0)),
                  pl.BlockSpec((d, tn), lambda p, j: (0, jnp.where(p == 0, j, nj - 1))),
                  pl.BlockSpec((d, tn), lambda p, j: (0, jnp.where(p == 1, j, 0))),
                  pl.BlockSpec((halo, tn), col),
                  pl.BlockSpec((CONV_FFN, tn), col),
                  pl.BlockSpec((1, tn), col)],
        out_specs=[pl.BlockSpec((m, tn), lambda p, j: (0, jnp.where(p == 1, j, 0))),
                   pl.BlockSpec((shift, halo // shift, tn), lambda p, j: (0, 0, p * nj + j))],
        out_shape=[jax.ShapeDtypeStruct((m, d_ff), BF16),
                   jax.ShapeDtypeStruct((shift, halo // shift, 2 * d_ff), F32)],
        scratch_shapes=[pltpu.VMEM((halo + m, tn), F32), pltpu.VMEM((nj, m, tn), F32)],
        compiler_params=_params(("arbitrary", "arbitrary")),
        name="ffn_up_carried",
    )(x, w_gate_b, w_val_b, bufs, conv_w, conv_b.reshape(1, 2 * d_ff))


def _ffn_up(x, w_up, conv_w, conv_b, n_seq, rows_per_seq, tm, tn):
    m, d = x.shape
    d_ff = w_up.shape[1] // 2
    nj = d_ff // tn
    halo = SUBLANES
    tiles = rows_per_seq // tm
    wspec = lambda off: pl.BlockSpec((d, tn), lambda j, s, i: (0, off + j))
    cwspec = lambda off: pl.BlockSpec((CONV_FFN, tn), lambda j, s, i: (0, off + j))
    cbspec = lambda off: pl.BlockSpec((1, tn), lambda j, s, i: (0, off + j))
    tspec = pl.BlockSpec((1, halo, tn), lambda j, s, i: (s, 0, j))
    tshape = jax.ShapeDtypeStruct((n_seq, halo, d_ff), F32)
    cb2 = conv_b.reshape(1, 2 * d_ff)
    return pl.pallas_call(
        functools.partial(_ffn_up_kernel, tm=tm),
        grid=(nj, n_seq, tiles),
        in_specs=[pl.BlockSpec((tm, d), lambda j, s, i: (s * tiles + i, 0)), wspec(0), wspec(nj),
                  cwspec(0), cwspec(nj), cbspec(0), cbspec(nj)],
        out_specs=[pl.BlockSpec((tm, tn), lambda j, s, i: (s * tiles + i, j)), tspec, tspec, wspec(0), wspec(0)],
        out_shape=[jax.ShapeDtypeStruct((m, d_ff), BF16), tshape, tshape]
                  + [jax.ShapeDtypeStruct((d, d_ff), BF16)] * 2,
        scratch_shapes=[pltpu.VMEM((d, tn), BF16)] * 2 + [pltpu.VMEM((halo + tm, tn), F32)] * 2
                       + [pltpu.VMEM((2, halo, tn), F32)],
        compiler_params=_params(("arbitrary", "arbitrary", "arbitrary")),
        name="ffn_up",
    )(x, w_up, w_up, conv_w, conv_w, cb2, cb2)


def _pad_lanes(vec, offset=0):
    out = jnp.zeros((1, LANES), F32)
    return out.at[0, offset:offset + vec.shape[0]].set(vec.astype(F32))


def _mixer_half(x2, lw, alpha, n_seq, rows_per_seq, shift, state, w_in_b):
    (w_in, conv_qkv_w, a_log, dt_bias, norm_g, gm_ln_g, gm_ln_b, gm_ws, gm_bs, gm_out_g, w_out,
     ln1_g, ln1_b) = lw
    m, d_model = x2.shape
    n_groups = gm_ws.shape[0]
    d_a = n_groups * GROUP
    d_qkv = conv_qkv_w.shape[1]
    d_b = d_qkv // 3
    n_heads = d_b // GROUP
    main_cols = 2 * d_a + d_qkv + d_b
    w_ab = jnp.pad(w_in[main_cols:].astype(BF16), ((0, LANES - 2 * n_heads), (0, 0)))
    if w_in_b is None:
        h, ab, w_in_b = _matmul(x2, w_in, w_ab, main_cols, min(m, PROJ_TM), PROJ_TN, emit_w=True)
    else:
        h, ab = _matmul(x2, w_in_b, w_ab, main_cols, min(m, PROJ_TM), PROJ_TN)
    alog, dtb, ng = _pad_lanes(a_log), _pad_lanes(dt_bias), norm_g.reshape(1, GROUP).astype(F32)

    if state is None:
        ya = _chunk_mlp_prompt(h, gm_ln_g, gm_ln_b, gm_ws, gm_bs, gm_out_g)
        v_rows = None
        q, k, v = _qkv_prep(h, conv_qkv_w, None, n_seq, rows_per_seq, QKV_TM, shift, 2 * d_a)
        yb, s_new = _delta_prompt(q, k, v, h, 2 * d_a + d_qkv, ab, alog, dtb, ng, n_seq, rows_per_seq, DELTA_SEQS)
    else:
        s_dn, buf_qkv = state
        n_pos = rows_per_seq // shift
        h3 = h.reshape(n_pos, shift, main_cols)
        ya3, v_rows = _chunk_mlp_sample(h3, gm_ln_g, gm_ln_b, gm_ws, gm_bs, gm_out_g)
        ya = ya3.reshape(m, d_a)
        q, k, v = _qkv_prep(h, conv_qkv_w, buf_qkv, n_seq, rows_per_seq, rows_per_seq, shift, 2 * d_a)
        r3 = lambda t: t.reshape(n_pos, shift, t.shape[-1])
        yb3, s_new = _delta_sample(r3(q), r3(k), r3(v), h3, 2 * d_a + d_qkv, r3(ab), alog, dtb, ng, s_dn,
                                   DELTA_SAMPLE_ROWS)
        yb = yb3.reshape(m, d_b)

    x1, x1b = _out_proj_ln(ya, yb, w_out, x2, ln1_g, ln1_b, alpha, min(m, OUT_TM))
    return x1, x1b, s_new, h, v_rows, w_in_b


def kernel(x_prompt, x_sample, state_dn, state_conv_qkv, state_conv_ffn, w_in, conv_qkv_w, dn_a_log, dn_dt_bias, dn_norm_g, gm_ln_g, gm_ln_b, gm_ws, gm_bs, gm_out_g, w_out, ln1_g, ln1_b, w_up, conv_ffn_w, conv_ffn_b, w_down, ln2_g, ln2_b):
    depth = w_in.shape[0]
    bp, lp, d_model = x_prompt.shape
    bs, ls, _ = x_sample.shape
    d_qkv = conv_qkv_w.shape[-1]
    d_a = gm_ws.shape[1] * GROUP
    alpha = (2.0 * depth) ** 0.25

    hp = x_prompt.reshape(bp * lp, d_model)
    hs = jnp.transpose(x_sample, (1, 0, 2)).reshape(ls * bs, d_model)
    outs = [[] for _ in range(7)]
    w_out = w_out.astype(BF16)
    w_in = jnp.swapaxes(w_in, 1, 2)
    for l in range(depth):
        lw = tuple(t[l] for t in (w_in, conv_qkv_w, dn_a_log, dn_dt_bias, dn_norm_g, gm_ln_g, gm_ln_b, gm_ws,
                                  gm_bs, gm_out_g, w_out, ln1_g, ln1_b))
        ffn_w = (conv_ffn_w[l], conv_ffn_b[l])
        ln2 = (ln2_g[l], ln2_b[l], alpha)
        n_pos = ls if l == depth - 1 else 0
        bq = jnp.transpose(state_conv_qkv[l], (1, 0, 2)).reshape((CONV_QKV - 1) * bs, d_qkv)
        bf = jnp.transpose(state_conv_ffn[l], (1, 0, 2)).reshape((CONV_FFN - 1) * bs, -1)
        x1_s, x1b_s, s_s, h_s, v_s, w_in_b = _mixer_half(hs, lw, alpha, 1, ls * bs, bs, (state_dn[l], bq), None)
        x1_p, x1b_p, s_p, h_p, _, _ = _mixer_half(hp, lw, alpha, bp, lp, 1, None, w_in_b)
        act_p, tg_p, tv_p, w_gate_b, w_val_b = _ffn_up(x1b_p, w_up[l], *ffn_w, bp, lp, FFN_TM, FFN_TN)
        act_s, cf_s = _ffn_up_carried(x1b_s, w_gate_b, w_val_b, *ffn_w, bf, FFN_TN, bs)
        hs, w_down_b = _down_ln(act_s, w_down[l], x1_s, *ln2, min(ls * bs, DOWN_EMIT_TM), DOWN_EMIT_TK,
                                emit_w=True, n_pos=n_pos)
        hp = _down_ln(act_p, w_down_b, x1_p, *ln2, min(bp * lp, DOWN_TM), DOWN_TK)
        h_s3 = h_s.reshape(ls, bs, -1)
        cq_s = jnp.transpose(h_s3[ls - (CONV_QKV - 1):, :, 2 * d_a:2 * d_a + d_qkv], (1, 0, 2))
        h_p3 = h_p.reshape(bp, lp, -1)
        cq_p = h_p3[:, lp - (CONV_QKV - 1):, 2 * d_a:2 * d_a + d_qkv]
        n_tail = tg_p.shape[1]
        cf_p = jnp.concatenate([tg_p, tv_p], axis=-1)[:, n_tail - (CONV_FFN - 1):]
        for acc, val in zip(outs, (s_p, cq_p, cf_p, s_s, cq_s, cf_s, v_s)):
            acc.append(val)
    return (hp.reshape(bp, lp, d_model), hs) + tuple(jnp.stack(o) for o in outs)
```

```python
import functools

import jax
import jax.numpy as jnp
from jax import lax
from jax.experimental import pallas as pl
from jax.experimental.pallas import tpu as pltpu

F32 = jnp.float32
BF16 = jnp.bfloat16

LANES = 128
SUBLANES = 8
VMEM_LIMIT = 56 * 1024 * 1024
DOWN_VMEM_LIMIT = 62 * 1024 * 1024

GROUP = 128
MLP_CHUNK = 128
DN_CHUNK = 64
CONV_QKV = 4
CONV_FFN = 3
CMLP_CHUNKS = 8
EPS = 1e-5

PROJ_TM, PROJ_TN = 1024, 1024
QKV_TM = 512
DELTA_SEQS = 4
DELTA_SAMPLE_ROWS = 16
OUT_TM = 512
FFN_TM, FFN_TN = 1024, 512
DOWN_TM, DOWN_TK = 1024, 1408
DOWN_EMIT_TM, DOWN_EMIT_TK = 512, 512

_NT = (((1,), (1,)), ((), ()))
_TN = (((0,), (0,)), ((), ()))


def _params(sem, vmem_limit=VMEM_LIMIT):
    return pltpu.CompilerParams(dimension_semantics=sem, vmem_limit_bytes=vmem_limit)


def _bdot(a, b):
    return jnp.dot(a.astype(BF16), b.astype(BF16), preferred_element_type=F32)


def _gelu(x):
    return 0.5 * x * (1.0 + lax.erf(x * (0.5 ** 0.5)))


def _silu(x):
    return x * jax.nn.sigmoid(x)


def _layer_norm(x, g, b):
    mu = jnp.mean(x, axis=-1, keepdims=True)
    xc = x - mu
    var = jnp.mean(xc * xc, axis=-1, keepdims=True)
    return xc * lax.rsqrt(var + EPS) * g + b


def _rms_norm(x, g):
    ms = jnp.mean(x * x, axis=-1, keepdims=True)
    return x * lax.rsqrt(ms + EPS) * g


def _mm_kernel(x_ref, w_ref, we_ref, o_ref, e_ref, *rest):
    xb_ref = rest[-1]

    @pl.when(pl.program_id(1) == 0)
    def _():
        xb = x_ref[...].astype(BF16)
        xb_ref[...] = xb
        e_ref[...] = lax.dot_general(xb, we_ref[...], _NT, preferred_element_type=F32)

    wb = w_ref[...].astype(BF16)
    if len(rest) == 2:
        rest[0][...] = wb
    o_ref[...] = lax.dot_general(xb_ref[...], wb, _NT, preferred_element_type=F32)


def _matmul(x, wt, wt_extra, n_cols, tm, tn, emit_w=False):
    m, k = x.shape
    ne = wt_extra.shape[0]
    out_specs = [pl.BlockSpec((tm, tn), lambda i, j: (i, j)), pl.BlockSpec((tm, ne), lambda i, j: (i, 0))]
    out_shape = [jax.ShapeDtypeStruct((m, n_cols), F32), jax.ShapeDtypeStruct((m, ne), F32)]
    if emit_w:
        out_specs.append(pl.BlockSpec((tn, k), lambda i, j: (j, 0)))
        out_shape.append(jax.ShapeDtypeStruct((n_cols, k), BF16))
    return pl.pallas_call(
        _mm_kernel,
        grid=(m // tm, n_cols // tn),
        in_specs=[pl.BlockSpec((tm, k), lambda i, j: (i, 0)),
                  pl.BlockSpec((tn, k), lambda i, j: (j, 0)),
                  pl.BlockSpec((ne, k), lambda i, j: (0, 0))],
        out_specs=out_specs,
        out_shape=out_shape,
        scratch_shapes=[pltpu.VMEM((tm, k), BF16)],
        compiler_params=_params(("arbitrary", "arbitrary")),
        name="proj",
    )(x, wt, wt_extra)


def _cmlp_prompt_kernel(u_ref, v_ref, lng_ref, lnb_ref, ws_ref, bst_ref, og_ref, ya_ref, *, n_groups, n_chunks):
    row = lax.broadcasted_iota(jnp.int32, (MLP_CHUNK, MLP_CHUNK), 0)
    col = lax.broadcasted_iota(jnp.int32, (MLP_CHUNK, MLP_CHUNK), 1)
    cg = [(c, g) for c in range(n_chunks) for g in range(n_groups)]
    rows = lambda c: slice(c * MLP_CHUNK, (c + 1) * MLP_CHUNK)
    cols = lambda g: slice(g * GROUP, (g + 1) * GROUP)
    w = [jnp.where(row >= col, ws_ref[g], 0.0).astype(BF16) for g in range(n_groups)]
    n = range(len(cg))
    gv = [_gelu(v_ref[rows(c), cols(g)]) for c, g in cg]
    mu = [jnp.mean(gv[i], axis=-1, keepdims=True) for i in n]
    xc = [gv[i] - mu[i] for i in n]
    var = [jnp.mean(xc[i] * xc[i], axis=-1, keepdims=True) for i in n]
    vh = [xc[i] * lax.rsqrt(var[i] + EPS) * lng_ref[:, cols(g)] + lnb_ref[:, cols(g)] for i, (c, g) in enumerate(cg)]
    mixed = [jnp.dot(w[g], vh[i].astype(BF16), preferred_element_type=F32) + bst_ref[:, g:g + 1]
             for i, (c, g) in enumerate(cg)]
    y = [_gelu(u_ref[rows(c), cols(g)]) * mixed[i] for i, (c, g) in enumerate(cg)]
    ms = [jnp.mean(y[i] * y[i], axis=-1, keepdims=True) for i in n]
    for i, (c, g) in enumerate(cg):
        ya_ref[rows(c), cols(g)] = (y[i] * lax.rsqrt(ms[i] + EPS) * og_ref[:, cols(g)]).astype(ya_ref.dtype)


def _chunk_mlp_prompt(h, ln_g, ln_b, ws, bs, out_g):
    m = h.shape[0]
    n_groups, d_a = ws.shape[0], ws.shape[0] * GROUP
    vec = pl.BlockSpec((1, d_a), lambda i: (0, 0))
    tm = CMLP_CHUNKS * MLP_CHUNK
    return pl.pallas_call(
        functools.partial(_cmlp_prompt_kernel, n_groups=n_groups, n_chunks=CMLP_CHUNKS),
        grid=(m // tm,),
        in_specs=[pl.BlockSpec((tm, d_a), lambda i: (i, 0)),
                  pl.BlockSpec((tm, d_a), lambda i: (i, 1)),
                  vec, vec,
                  pl.BlockSpec((n_groups, MLP_CHUNK, MLP_CHUNK), lambda i: (0, 0, 0)),
                  pl.BlockSpec((MLP_CHUNK, n_groups), lambda i: (0, 0)),
                  vec],
        out_specs=pl.BlockSpec((tm, d_a), lambda i: (i, 0)),
        out_shape=jax.ShapeDtypeStruct((m, d_a), BF16),
        compiler_params=_params(("arbitrary",)),
        name="chunk_mlp_prompt",
    )(h, h, ln_g.reshape(1, d_a), ln_b.reshape(1, d_a), ws, bs.T, out_g.reshape(1, d_a))


def _cmlp_sample_kernel(u_ref, v_ref, lng_ref, lnb_ref, wv_ref, bv_ref, og_ref, ya_ref, vr_ref, *, n_pos):
    vh = []
    for t in range(n_pos):
        vt = _layer_norm(_gelu(v_ref[t]), lng_ref[0], lnb_ref[0])
        vr_ref[:, t, :] = vt
        vh.append(vt)
    for t in range(n_pos):
        mixed = wv_ref[t, 0:1, :] * vh[0]
        for s in range(1, t + 1):
            mixed = mixed + wv_ref[t, s:s + 1, :] * vh[s]
        mixed = mixed + bv_ref[0, t:t + 1, :]
        y = _gelu(u_ref[t]) * mixed
        ya_ref[t] = _rms_norm(y, og_ref[0]).astype(ya_ref.dtype)


def _chunk_mlp_sample(h3, ln_g, ln_b, ws, bs, out_g):
    n_pos, nb, _ = h3.shape
    n_groups = ws.shape[0]
    d_a = n_groups * GROUP
    wv = jnp.repeat(jnp.transpose(ws[:, :n_pos, :n_pos], (1, 2, 0)), GROUP, axis=-1)
    bv = jnp.repeat(bs[:, :n_pos].T, GROUP, axis=-1).reshape(1, n_pos, d_a)
    vec = pl.BlockSpec((1, 1, GROUP), lambda g: (0, 0, g))
    act = pl.BlockSpec((n_pos, nb, GROUP), lambda g: (0, 0, g))
    return pl.pallas_call(
        functools.partial(_cmlp_sample_kernel, n_pos=n_pos),
        grid=(n_groups,),
        in_specs=[act,
                  pl.BlockSpec((n_pos, nb, GROUP), lambda g: (0, 0, n_groups + g)),
                  vec, vec,
                  pl.BlockSpec((n_pos, n_pos, GROUP), lambda g: (0, 0, g)),
                  pl.BlockSpec((1, n_pos, GROUP), lambda g: (0, 0, g)),
                  vec],
        out_specs=[act, pl.BlockSpec((nb, n_pos, GROUP), lambda g: (0, 0, g))],
        out_shape=[jax.ShapeDtypeStruct((n_pos, nb, d_a), BF16),
                   jax.ShapeDtypeStruct((nb, n_pos, d_a), F32)],
        compiler_params=_params(("arbitrary",)),
        name="chunk_mlp_sample",
    )(h3, h3, ln_g.reshape(1, 1, d_a), ln_b.reshape(1, 1, d_a), wv, bv, out_g.reshape(1, 1, d_a))


def _conv_taps(xbuf_ref, w_ref, tm, shift, halo, width):
    acc = None
    for j in range(width):
        off = halo - (width - 1 - j) * shift
        term = w_ref[j:j + 1, :] * xbuf_ref[off:off + tm, :]
        acc = term if acc is None else acc + term
    return acc


def _qkv_prep_kernel(*refs, tm, shift, halo, n_heads, has_buf):
    if has_buf:
        (xq_ref, xk_ref, xv_ref, bq_ref, bk_ref, bv_ref, wq_ref, wk_ref, wv_ref,
         q_ref, k_ref, v_ref, xbuf_ref, carry_ref) = refs
        bufs = (bq_ref, bk_ref, bv_ref)
    else:
        (xq_ref, xk_ref, xv_ref, wq_ref, wk_ref, wv_ref,
         q_ref, k_ref, v_ref, xbuf_ref, carry_ref) = refs
        bufs = (None, None, None)
    first = pl.program_id(1) == 0
    parts = ((xq_ref, wq_ref, q_ref, GROUP ** -0.5), (xk_ref, wk_ref, k_ref, 1.0), (xv_ref, wv_ref, v_ref, None))
    for p, (x_ref, w_ref, o_ref, scale) in enumerate(parts):
        @pl.when(first)
        def _(p=p):
            if has_buf:
                xbuf_ref[0:halo, :] = bufs[p][...]
            else:
                xbuf_ref[0:halo, :] = jnp.zeros((halo, xbuf_ref.shape[1]), F32)

        @pl.when(jnp.logical_not(first))
        def _(p=p):
            xbuf_ref[0:halo, :] = carry_ref[p]

        xbuf_ref[halo:halo + tm, :] = x_ref[...]
        carry_ref[p] = xbuf_ref[tm:tm + halo, :]
        y = _silu(_conv_taps(xbuf_ref, w_ref, tm, shift, halo, CONV_QKV))
        if scale is None:
            o_ref[...] = y
        else:
            for h in range(n_heads):
                sl = slice(h * GROUP, (h + 1) * GROUP)
                yh = y[:, sl]
                inv = lax.rsqrt(jnp.sum(yh * yh, axis=-1, keepdims=True) + 1e-6)
                o_ref[:, sl] = yh * (inv * scale if scale != 1.0 else inv)


def _qkv_prep(h, conv_w, bufs, n_seq, rows_per_seq, tm, shift, col0):
    d_b = conv_w.shape[1] // 3
    n_heads = d_b // GROUP
    has_buf = bufs is not None
    halo = (CONV_QKV - 1) * shift if has_buf else SUBLANES
    tiles = rows_per_seq // tm
    cb = col0 // d_b
    x_specs = [pl.BlockSpec((tm, d_b), lambda s, i, c=c: (s * tiles + i, cb + c)) for c in range(3)]
    w_specs = [pl.BlockSpec((CONV_QKV, d_b), lambda s, i, c=c: (0, c)) for c in range(3)]
    o_spec = pl.BlockSpec((tm, d_b), lambda s, i: (s * tiles + i, 0))
    args, specs = [h, h, h], list(x_specs)
    if has_buf:
        specs += [pl.BlockSpec((halo, d_b), lambda s, i, c=c: (s, c)) for c in range(3)]
        args += [bufs, bufs, bufs]
    specs += w_specs
    args += [conv_w, conv_w, conv_w]
    m = n_seq * rows_per_seq
    return pl.pallas_call(
        functools.partial(_qkv_prep_kernel, tm=tm, shift=shift, halo=halo, n_heads=n_heads, has_buf=has_buf),
        grid=(n_seq, tiles),
        in_specs=specs,
        out_specs=[o_spec, o_spec, o_spec],
        out_shape=[jax.ShapeDtypeStruct((m, d_b), F32)] * 3,
        scratch_shapes=[pltpu.VMEM((halo + tm, d_b), F32), pltpu.VMEM((3, halo, d_b), F32)],
        compiler_params=_params(("arbitrary", "arbitrary")),
        name="qkv_prep",
    )(*args)


def _gates(ab, alog, dtb):
    g = -jnp.exp(alog) * jax.nn.softplus(ab + dtb)
    beta = jax.nn.sigmoid(ab)
    return g, beta


def _delta_prompt_kernel(q_ref, k_ref, v_ref, z_ref, ab_ref, alog_ref, dtb_ref, ng_ref,
                         yb_ref, sout_ref, s_ref, *, n_heads, n_chunks, ns):
    c = DN_CHUNK
    ci = pl.program_id(1)

    @pl.when(ci == 0)
    def _():
        s_ref[...] = jnp.zeros(s_ref.shape, F32)

    row = lax.broadcasted_iota(jnp.int32, (c, c), 0)
    col = lax.broadcasted_iota(jnp.int32, (c, c), 1)
    causal = row >= col
    strict = row > col
    eye = (row == col).astype(F32)
    tril = causal.astype(F32)
    n_double = (c - 1).bit_length() - 1
    ch = [(s, h) for s in range(ns) for h in range(n_heads)]
    sl = lambda h: slice(h * GROUP, (h + 1) * GROUP)
    gates = [_gates(ab_ref[s], alog_ref[...], dtb_ref[...]) for s in range(ns)]
    gc_s = [jnp.dot(tril, gates[s][0], precision=lax.Precision.HIGHEST, preferred_element_type=F32)
            for s in range(ns)]
    gct_s = [g.T for g in gc_s]
    q = [q_ref[s, :, sl(h)] for s, h in ch]
    k = [k_ref[s, :, sl(h)] for s, h in ch]
    gcol = [gc_s[s][:, h:h + 1] for s, h in ch]
    beta = [gates[s][1][:, n_heads + h:n_heads + h + 1] for s, h in ch]
    n = range(len(ch))
    decay = [jnp.exp(jnp.where(causal, gcol[i] - gct_s[s][h:h + 1, :], -jnp.inf)) for i, (s, h) in enumerate(ch)]
    kb = [k[i] * beta[i] for i in n]
    kbf = [k[i].astype(BF16) for i in n]
    kq = [lax.dot_general(jnp.concatenate([kb[i], q[i]], axis=0).astype(BF16), kbf[i], _NT,
                          preferred_element_type=F32) for i in n]
    qk = [kq[i][c:] * decay[i] for i in n]
    y = [jnp.where(strict, -(kq[i][:c] * decay[i]), 0.0) for i in n]
    x = [eye + y[i] for i in n]
    y = [_bdot(y[i], y[i]) for i in n]
    for _ in range(n_double - 1):
        t = [_bdot(jnp.concatenate([x[i], y[i]], axis=0), y[i]) for i in n]
        x = [x[i] + t[i][:c] for i in n]
        y = [t[i][c:] for i in n]
    x = [x[i] + _bdot(x[i], y[i]) for i in n]
    egc = [jnp.exp(gcol[i]) for i in n]
    sol = [_bdot(x[i], jnp.concatenate([v_ref[s, :, sl(h)] * beta[i], kb[i] * egc[i]], axis=1))
           for i, (s, h) in enumerate(ch)]
    s_old = [s_ref[s, h] for s, h in ch]
    r = [_bdot(jnp.concatenate([sol[i][:, GROUP:], q[i] * egc[i]], axis=0), s_old[i]) for i in n]
    v_new = [sol[i][:, :GROUP] - r[i][:c] for i in n]
    o = [r[i][c:] + _bdot(qk[i], v_new[i]) for i in n]
    gl = [gc_s[s][c - 1:c, h:h + 1] for s, h in ch]
    upd = [lax.dot_general((k[i] * jnp.exp(gl[i] - gcol[i])).astype(BF16), v_new[i].astype(BF16), _TN,
                           preferred_element_type=F32) for i in n]
    ms = [jnp.mean(o[i] * o[i], axis=-1, keepdims=True) for i in n]
    for i, (s, h) in enumerate(ch):
        s_ref[s, h] = s_old[i] * jnp.exp(gl[i]) + upd[i]
        on = o[i] * lax.rsqrt(ms[i] + EPS) * ng_ref[...]
        yb_ref[s, :, sl(h)] = (on * _silu(z_ref[s, :, sl(h)])).astype(yb_ref.dtype)

    @pl.when(ci == n_chunks - 1)
    def _():
        sout_ref[...] = s_ref[...]


def _delta_prompt(q, k, v, h, z_col0, ab, alog, dtb, norm_g, n_seq, rows_per_seq, ns):
    d_b = q.shape[1]
    n_heads = d_b // GROUP
    n_chunks = rows_per_seq // DN_CHUNK
    r3 = lambda t: t.reshape(n_seq, rows_per_seq, t.shape[-1])
    act = pl.BlockSpec((ns, DN_CHUNK, d_b), lambda s, i: (s, i, 0))
    vec = pl.BlockSpec((1, LANES), lambda s, i: (0, 0))
    zb = z_col0 // d_b
    yb, s_new = pl.pallas_call(
        functools.partial(_delta_prompt_kernel, n_heads=n_heads, n_chunks=n_chunks, ns=ns),
        grid=(n_seq // ns, n_chunks),
        in_specs=[act, act, act,
                  pl.BlockSpec((ns, DN_CHUNK, d_b), lambda s, i: (s, i, zb)),
                  pl.BlockSpec((ns, DN_CHUNK, LANES), lambda s, i: (s, i, 0)),
                  vec, vec, vec],
        out_specs=[act, pl.BlockSpec((ns, n_heads, GROUP, GROUP), lambda s, i: (s, 0, 0, 0))],
        out_shape=[jax.ShapeDtypeStruct((n_seq, rows_per_seq, d_b), BF16),
                   jax.ShapeDtypeStruct((n_seq, n_heads, GROUP, GROUP), F32)],
        scratch_shapes=[pltpu.VMEM((ns, n_heads, GROUP, GROUP), F32)],
        compiler_params=_params(("arbitrary", "arbitrary")),
        name="delta_prompt",
    )(r3(q), r3(k), r3(v), r3(h), r3(ab), alog, dtb, norm_g)
    return yb.reshape(n_seq * rows_per_seq, d_b), s_new


def _delta_sample_kernel(q_ref, k_ref, v_ref, z_ref, ab_ref, alog_ref, dtb_ref, ng_ref, s0_ref,
                         yb_ref, s1_ref, lhs_ref, res_ref, kt_ref, vn_ref, gl_ref, *, n_heads, n_pos, bb):
    nt = n_pos
    hs = range(n_heads)
    sls = [slice(h * GROUP, (h + 1) * GROUP) for h in hs]
    gates = [_gates(ab_ref[t], alog_ref[...], dtb_ref[...]) for t in range(nt)]
    zeros_pad = jnp.zeros((bb, GROUP), F32)
    ts = range(nt)
    q = [[q_ref[t, :, sls[h]] for t in ts] for h in hs]
    k = [[k_ref[t, :, sls[h]] for t in ts] for h in hs]
    beta = [[gates[t][1][:, n_heads + h:n_heads + h + 1] for t in ts] for h in hs]
    gc = []
    for h in hs:
        g = [gates[t][0][:, h:h + 1] for t in ts]
        for t in range(1, nt):
            g[t] = g[t - 1] + g[t]
        gc.append(g)
    kb = [[k[h][t] * beta[h][t] for t in ts] for h in hs]
    kk = [[[jnp.sum(kb[h][t] * k[h][s], axis=-1, keepdims=True) for s in range(t)] for t in ts] for h in hs]
    qk = [[[jnp.sum(q[h][t] * k[h][s], axis=-1, keepdims=True) for s in range(t + 1)] for t in ts] for h in hs]
    dec = [[[jnp.exp(gc[h][t] - gc[h][s]) for s in range(t + 1)] for t in ts] for h in hs]
    a = [[[kk[h][t][s] * dec[h][t][s] for s in range(t)] for t in ts] for h in hs]
    qk = [[[qk[h][t][s] * dec[h][t][s] for s in range(t + 1)] for t in ts] for h in hs]
    egc = [[jnp.exp(gc[h][t]) for t in ts] for h in hs]
    u_blk = [[v_ref[t, :, sls[h]] * beta[h][t] for t in ts] for h in hs]
    w_blk = [[kb[h][t] * egc[h][t] for t in ts] for h in hs]
    for t in ts:
        for h in hs:
            for s in range(t):
                u_blk[h][t] = u_blk[h][t] - a[h][t][s] * u_blk[h][s]
                w_blk[h][t] = w_blk[h][t] - a[h][t][s] * w_blk[h][s]
    for h in hs:
        for t in ts:
            lhs_ref[h, :, t, :] = w_blk[h][t]
            lhs_ref[h, :, nt + t, :] = q[h][t] * egc[h][t]
        gl_ref[h] = jnp.broadcast_to(egc[h][nt - 1], (bb, GROUP))

    for h in hs:
        for b in range(bb):
            res_ref[h, b] = _bdot(lhs_ref[h, b], s0_ref[b, h])

    v_new = [[u_blk[h][t] - res_ref[h, :, t, :] for t in ts] for h in hs]
    o = [[res_ref[h, :, nt + t, :] for t in ts] for h in hs]
    for h in hs:
        for t in ts:
            for s in range(t + 1):
                o[h][t] = o[h][t] + qk[h][t][s] * v_new[h][s]
    ms = [[jnp.mean(o[h][t] * o[h][t], axis=-1, keepdims=True) for t in ts] for h in hs]
    for h in hs:
        gl = gc[h][nt - 1]
        for t in ts:
            on = o[h][t] * lax.rsqrt(ms[h][t] + EPS) * ng_ref[...]
            yb_ref[t, :, sls[h]] = (on * _silu(z_ref[t, :, sls[h]])).astype(yb_ref.dtype)
            kt_ref[h, :, t, :] = k[h][t] * jnp.exp(gl - gc[h][t])
            vn_ref[h, :, t, :] = v_new[h][t]
            kt_ref[h, :, nt + t, :] = zeros_pad
            vn_ref[h, :, nt + t, :] = zeros_pad

    for h in hs:
        for b in range(bb):
            upd = lax.dot_general(kt_ref[h, b].astype(BF16), vn_ref[h, b].astype(BF16), _TN,
                                  preferred_element_type=F32)
            s1_ref[b, h] = s0_ref[b, h] * gl_ref[h, b:b + 1, :] + upd


def _delta_sample(q, k, v, h3, z_col0, ab, alog, dtb, norm_g, s0, bb):
    n_pos, nb, d_b = q.shape
    n_heads = d_b // GROUP
    assert 2 * n_pos == SUBLANES, "one state read packs [w | q] rows of a sequence into one 8-row tile"
    zb = z_col0 // d_b
    act = pl.BlockSpec((n_pos, bb, d_b), lambda i: (0, i, 0))
    vec = pl.BlockSpec((1, LANES), lambda i: (0, 0))
    st = pl.BlockSpec((bb, n_heads, GROUP, GROUP), lambda i: (i, 0, 0, 0))
    tile = pltpu.VMEM((n_heads, bb, SUBLANES, GROUP), F32)
    return pl.pallas_call(
        functools.partial(_delta_sample_kernel, n_heads=n_heads, n_pos=n_pos, bb=bb),
        grid=(nb // bb,),
        in_specs=[act, act, act,
                  pl.BlockSpec((n_pos, bb, d_b), lambda i: (0, i, zb)),
                  pl.BlockSpec((n_pos, bb, LANES), lambda i: (0, i, 0)),
                  vec, vec, vec, st],
        out_specs=[act, st],
        out_shape=[jax.ShapeDtypeStruct((n_pos, nb, d_b), BF16),
                   jax.ShapeDtypeStruct(s0.shape, F32)],
        scratch_shapes=[tile, tile, tile, tile, pltpu.VMEM((n_heads, bb, GROUP), F32)],
        compiler_params=_params(("arbitrary",)),
        name="delta_sample",
    )(q, k, v, h3, ab, alog, dtb, norm_g, s0)


def _out_proj_kernel(ya_ref, yb_ref, w_ref, r_ref, g_ref, b_ref, o_ref, ob_ref, z_ref, *, n_tiles, d_a, alpha):
    i = pl.program_id(0)

    @pl.when(i == 0)
    def _():
        z_ref[...] = jnp.zeros(z_ref.shape, F32)

    def emit_ln():
        y = _layer_norm(z_ref[...], g_ref[...], b_ref[...])
        o_ref[...] = y
        ob_ref[...] = y.astype(ob_ref.dtype)

    @pl.when(i < n_tiles)
    def _():
        emit_ln()
        acc = (jnp.dot(ya_ref[...], w_ref[0:d_a, :], preferred_element_type=F32)
               + jnp.dot(yb_ref[...], w_ref[d_a:, :], preferred_element_type=F32))
        z_ref[...] = alpha * r_ref[...] + acc

    @pl.when(i == n_tiles)
    def _():
        emit_ln()


def _out_proj_ln(ya, yb, w, resid, g, b, alpha, tm):
    m, d_a = ya.shape
    k, d = w.shape
    n_tiles = m // tm
    cur = lambda i: (jnp.minimum(i, n_tiles - 1), 0)
    prev = lambda i: (jnp.maximum(i - 1, 0), 0)
    vec = pl.BlockSpec((1, d), lambda i: (0, 0))
    return pl.pallas_call(
        functools.partial(_out_proj_kernel, n_tiles=n_tiles, d_a=d_a, alpha=alpha),
        grid=(n_tiles + 1,),
        in_specs=[pl.BlockSpec((tm, d_a), cur),
                  pl.BlockSpec((tm, k - d_a), cur),
                  pl.BlockSpec((k, d), lambda i: (0, 0), pipeline_mode=pl.Buffered(1)),
                  pl.BlockSpec((tm, d), cur), vec, vec],
        out_specs=[pl.BlockSpec((tm, d), prev), pl.BlockSpec((tm, d), prev)],
        out_shape=[jax.ShapeDtypeStruct((m, d), F32), jax.ShapeDtypeStruct((m, d), BF16)],
        scratch_shapes=[pltpu.VMEM((tm, d), F32)],
        compiler_params=_params(("arbitrary",)),
        name="out_proj_ln1",
    )(ya, yb, w, resid, g.reshape(1, d), b.reshape(1, d))


def _down_kernel(x_ref, w_ref, r_ref, g_ref, b_ref, o_ref, *rest, n_tiles, nk, nc, alpha, n_pos):
    z_ref = rest[-1]
    cw = z_ref.shape[-1]
    slab = lambda c: slice(c * cw, (c + 1) * cw)
    i, kk = pl.program_id(0), pl.program_id(1)

    def emit_ln():
        z = [z_ref[c] for c in range(nc)]
        inv_d = 1.0 / (nc * cw)
        mu = sum(jnp.sum(zc, axis=-1, keepdims=True) for zc in z) * inv_d
        xc = [zc - mu for zc in z]
        var = sum(jnp.sum(c * c, axis=-1, keepdims=True) for c in xc) * inv_d
        rs = lax.rsqrt(var + EPS)
        for c in range(nc):
            y = xc[c] * rs * g_ref[:, slab(c)] + b_ref[:, slab(c)]
            if n_pos:
                nb = y.shape[0] // n_pos
                for t in range(n_pos):
                    o_ref[:, t, slab(c)] = y[t * nb:(t + 1) * nb]
            else:
                o_ref[:, slab(c)] = y

    @pl.when(jnp.logical_and(i == 0, kk == 0))
    def _():
        z_ref[...] = jnp.zeros(z_ref.shape, F32)

    wb = w_ref[...].astype(BF16)
    if len(rest) == 2:
        rest[0][...] = wb

    @pl.when(kk == 0)
    def _():
        emit_ln()
        part = jnp.dot(x_ref[...], wb, preferred_element_type=F32)
        for c in range(nc):
            if nc == nk:
                z_ref[c] = part[:, slab(c)] + alpha * r_ref[...] if c == 0 else part[:, slab(c)]
            else:
                z_ref[c] = part[:, slab(c)] + alpha * r_ref[:, slab(c)]

    @pl.when(jnp.logical_and(kk > 0, i < n_tiles))
    def _():
        part = jnp.dot(x_ref[...], wb, preferred_element_type=F32)
        for c in range(nc):
            z_ref[c] += part[:, slab(c)]
        if nc == nk:
            z_ref[kk] += alpha * r_ref[...]


def _down_ln(x, w, resid, g, b, alpha, tm, tk, emit_w=False, n_pos=0):
    m, kdim = x.shape
    d = w.shape[1]
    nk = kdim // tk
    n_tiles = m // tm
    nc = nk if (d % nk == 0 and (d // nk) % LANES == 0) else 1
    row = lambda i: jnp.minimum(i, n_tiles - 1)
    kt = lambda i, kk: jnp.where(i == n_tiles, nk - 1, kk)
    vec = pl.BlockSpec((1, d), lambda i, kk: (0, 0))
    wspec = pl.BlockSpec((tk, d), lambda i, kk: (kt(i, kk), 0))
    if nc == nk:
        rspec = pl.BlockSpec((tm, d // nc), lambda i, kk: (row(i), kt(i, kk)))
    else:
        rspec = pl.BlockSpec((tm, d), lambda i, kk: (row(i), 0))
    if n_pos:
        assert n_tiles == 1
        out_specs = [pl.BlockSpec((m // n_pos, n_pos, d), lambda i, kk: (0, 0, 0))]
        out_shape = [jax.ShapeDtypeStruct((m // n_pos, n_pos, d), F32)]
    else:
        out_specs = [pl.BlockSpec((tm, d), lambda i, kk: (jnp.where(kk == 0, jnp.maximum(i - 1, 0), row(i)), 0))]
        out_shape = [jax.ShapeDtypeStruct((m, d), F32)]
    if emit_w:
        out_specs.append(wspec)
        out_shape.append(jax.ShapeDtypeStruct(w.shape, BF16))
    outs = pl.pallas_call(
        functools.partial(_down_kernel, n_tiles=n_tiles, nk=nk, nc=nc, alpha=alpha, n_pos=n_pos),
        grid=(n_tiles + 1, nk),
        in_specs=[pl.BlockSpec((tm, tk), lambda i, kk: (row(i), kt(i, kk))), wspec, rspec, vec, vec],
        out_specs=out_specs,
        out_shape=out_shape,
        scratch_shapes=[pltpu.VMEM((nc, tm, d // nc), F32)],
        compiler_params=_params(("arbitrary", "arbitrary"), DOWN_VMEM_LIMIT),
        name="down_ln2",
    )(x, w, resid, g.reshape(1, d), b.reshape(1, d))
    return tuple(outs) if emit_w else outs[0]


def _ffn_up_kernel(x_ref, wg_ref, wv_ref, cwg_ref, cwv_ref, cbg_ref, cbv_ref,
                   act_ref, tg_ref, tv_ref, wgo_ref, wvo_ref, wgb_ref, wvb_ref, carry_ref, *, tm):
    halo = SUBLANES
    first = pl.program_id(2) == 0

    @pl.when(jnp.logical_and(pl.program_id(1) == 0, first))
    def _():
        for w_ref, wb_ref, wo_ref in ((wg_ref, wgb_ref, wgo_ref), (wv_ref, wvb_ref, wvo_ref)):
            wb = w_ref[...].astype(BF16)
            wb_ref[...] = wb
            wo_ref[...] = wb

    @pl.when(first)
    def _():
        carry_ref[...] = jnp.zeros(carry_ref.shape, F32)

    x = x_ref[...]
    conv = []
    parts = ((wgb_ref, cwg_ref, cbg_ref, tg_ref), (wvb_ref, cwv_ref, cbv_ref, tv_ref))
    row = lax.broadcasted_iota(jnp.int32, (halo, act_ref.shape[1]), 0)
    for p, (w_ref, cw_ref, cb_ref, t_ref) in enumerate(parts):
        up = jnp.dot(x, w_ref[...], preferred_element_type=F32)
        hist = carry_ref[p]
        tail = up[tm - halo:tm, :]
        carry_ref[p] = tail
        t_ref[0] = tail
        acc = cw_ref[CONV_FFN - 1:CONV_FFN, :] * up
        for dd in range(1, CONV_FFN):
            xs = pltpu.roll(up, dd, axis=0)
            head = jnp.where(row < dd, pltpu.roll(hist, dd, axis=0), xs[0:halo])
            xs = jnp.concatenate([head, xs[halo:]], axis=0)
            acc = acc + cw_ref[CONV_FFN - 1 - dd:CONV_FFN - dd, :] * xs
        conv.append(acc + cb_ref[...])
    act_ref[...] = (_silu(conv[0]) * conv[1]).astype(act_ref.dtype)


def _ffn_up_carried_kernel(x_ref, wg_ref, wv_ref, buf_ref, cw_ref, cb_ref, act_ref, t_ref, xbuf_ref, gate_ref,
                           *, shift):
    p, j = pl.program_id(0), pl.program_id(1)
    tm, halo = x_ref.shape[0], buf_ref.shape[0]
    xbuf_ref[0:halo, :] = buf_ref[...]

    def conv_of(w_ref):
        xbuf_ref[halo:halo + tm, :] = jnp.dot(x_ref[...], w_ref[...], preferred_element_type=F32)
        tail = xbuf_ref[tm:tm + halo, :]
        for t in range(halo // shift):
            t_ref[:, t, :] = tail[t * shift:(t + 1) * shift]
        return _conv_taps(xbuf_ref, cw_ref, tm, shift, halo, CONV_FFN) + cb_ref[...]

    @pl.when(p == 0)
    def _():
        gate_ref[j] = conv_of(wg_ref)

    @pl.when(p == 1)
    def _():
        act_ref[...] = (_silu(gate_ref[j]) * conv_of(wv_ref)).astype(act_ref.dtype)


def _ffn_up_carried(x, w_gate_b, w_val_b, conv_w, conv_b, bufs, tn, shift):
    m, d = x.shape
    d_ff = w_gate_b.shape[1]
    nj = d_ff // tn
    halo = (CONV_FFN - 1) * shift
    col = lambda p, j: (0, p * nj + j)
    return pl.pallas_call(
        functools.partial(_ffn_up_carried_kernel, shift=shift),
        grid=(2, nj),
        in_specs=[pl.BlockSpec((m, d), lambda p, j: (0, 0)),
                  pl.BlockSpec((d, tn), lambda p, j: (0, jnp.where(p == 0, j, nj - 1))),
                  pl.BlockSpec((d, tn), lambda p, j: (0, jnp.where(p == 1, j, 0))),
                  pl.BlockSpec((halo, tn), col),
                  pl.BlockSpec((CONV_FFN, tn), col),
                  pl.BlockSpec((1, tn), col)],
        out_specs=[pl.BlockSpec((m, tn), lambda p, j: (0, jnp.where(p == 1, j, 0))),
                   pl.BlockSpec((shift, halo // shift, tn), lambda p, j: (0, 0, p * nj + j))],
        out_shape=[jax.ShapeDtypeStruct((m, d_ff), BF16),
                   jax.ShapeDtypeStruct((shift, halo // shift, 2 * d_ff), F32)],
        scratch_shapes=[pltpu.VMEM((halo + m, tn), F32), pltpu.VMEM((nj, m, tn), F32)],
        compiler_params=_params(("arbitrary", "arbitrary")),
        name="ffn_up_carried",
    )(x, w_gate_b, w_val_b, bufs, conv_w, conv_b.reshape(1, 2 * d_ff))


def _ffn_up(x, w_up, conv_w, conv_b, n_seq, rows_per_seq, tm, tn):
    m, d = x.shape
    d_ff = w_up.shape[1] // 2
    nj = d_ff // tn
    halo = SUBLANES
    tiles = rows_per_seq // tm
    wspec = lambda off: pl.BlockSpec((d, tn), lambda j, s, i: (0, off + j))
    cwspec = lambda off: pl.BlockSpec((CONV_FFN, tn), lambda j, s, i: (0, off + j))
    cbspec = lambda off: pl.BlockSpec((1, tn), lambda j, s, i: (0, off + j))
    tspec = pl.BlockSpec((1, halo, tn), lambda j, s, i: (s, 0, j))
    tshape = jax.ShapeDtypeStruct((n_seq, halo, d_ff), F32)
    cb2 = conv_b.reshape(1, 2 * d_ff)
    return pl.pallas_call(
        functools.partial(_ffn_up_kernel, tm=tm),
        grid=(nj, n_seq, tiles),
        in_specs=[pl.BlockSpec((tm, d), lambda j, s, i: (s * tiles + i, 0)), wspec(0), wspec(nj),
                  cwspec(0), cwspec(nj), cbspec(0), cbspec(nj)],
        out_specs=[pl.BlockSpec((tm, tn), lambda j, s, i: (s * tiles + i, j)), tspec, tspec, wspec(0), wspec(0)],
        out_shape=[jax.ShapeDtypeStruct((m, d_ff), BF16), tshape, tshape]
                  + [jax.ShapeDtypeStruct((d, d_ff), BF16)] * 2,
        scratch_shapes=[pltpu.VMEM((d, tn), BF16)] * 2 + [pltpu.VMEM((2, halo, tn), F32)],
        compiler_params=_params(("arbitrary", "arbitrary", "arbitrary")),
        name="ffn_up",
    )(x, w_up, w_up, conv_w, conv_w, cb2, cb2)


def _pad_lanes(vec, offset=0):
    out = jnp.zeros((1, LANES), F32)
    return out.at[0, offset:offset + vec.shape[0]].set(vec.astype(F32))


def _mixer_half(x2, lw, alpha, n_seq, rows_per_seq, shift, state, w_in_b):
    (w_in, conv_qkv_w, a_log, dt_bias, norm_g, gm_ln_g, gm_ln_b, gm_ws, gm_bs, gm_out_g, w_out,
     ln1_g, ln1_b) = lw
    m, d_model = x2.shape
    n_groups = gm_ws.shape[0]
    d_a = n_groups * GROUP
    d_qkv = conv_qkv_w.shape[1]
    d_b = d_qkv // 3
    n_heads = d_b // GROUP
    main_cols = 2 * d_a + d_qkv + d_b
    w_ab = jnp.pad(w_in[main_cols:].astype(BF16), ((0, LANES - 2 * n_heads), (0, 0)))
    if w_in_b is None:
        h, ab, w_in_b = _matmul(x2, w_in, w_ab, main_cols, min(m, PROJ_TM), PROJ_TN, emit_w=True)
    else:
        h, ab = _matmul(x2, w_in_b, w_ab, main_cols, min(m, PROJ_TM), PROJ_TN)
    alog, dtb, ng = _pad_lanes(a_log), _pad_lanes(dt_bias), norm_g.reshape(1, GROUP).astype(F32)

    if state is None:
        ya = _chunk_mlp_prompt(h, gm_ln_g, gm_ln_b, gm_ws, gm_bs, gm_out_g)
        v_rows = None
        q, k, v = _qkv_prep(h, conv_qkv_w, None, n_seq, rows_per_seq, QKV_TM, shift, 2 * d_a)
        yb, s_new = _delta_prompt(q, k, v, h, 2 * d_a + d_qkv, ab, alog, dtb, ng, n_seq, rows_per_seq, DELTA_SEQS)
    else:
        s_dn, buf_qkv = state
        n_pos = rows_per_seq // shift
        h3 = h.reshape(n_pos, shift, main_cols)
        ya3, v_rows = _chunk_mlp_sample(h3, gm_ln_g, gm_ln_b, gm_ws, gm_bs, gm_out_g)
        ya = ya3.reshape(m, d_a)
        q, k, v = _qkv_prep(h, conv_qkv_w, buf_qkv, n_seq, rows_per_seq, rows_per_seq, shift, 2 * d_a)
        r3 = lambda t: t.reshape(n_pos, shift, t.shape[-1])
        yb3, s_new = _delta_sample(r3(q), r3(k), r3(v), h3, 2 * d_a + d_qkv, r3(ab), alog, dtb, ng, s_dn,
                                   DELTA_SAMPLE_ROWS)
        yb = yb3.reshape(m, d_b)

    x1, x1b = _out_proj_ln(ya, yb, w_out, x2, ln1_g, ln1_b, alpha, min(m, OUT_TM))
    return x1, x1b, s_new, h, v_rows, w_in_b


def kernel(x_prompt, x_sample, state_dn, state_conv_qkv, state_conv_ffn, w_in, conv_qkv_w, dn_a_log, dn_dt_bias, dn_norm_g, gm_ln_g, gm_ln_b, gm_ws, gm_bs, gm_out_g, w_out, ln1_g, ln1_b, w_up, conv_ffn_w, conv_ffn_b, w_down, ln2_g, ln2_b):
    depth = w_in.shape[0]
    bp, lp, d_model = x_prompt.shape
    bs, ls, _ = x_sample.shape
    d_qkv = conv_qkv_w.shape[-1]
    d_a = gm_ws.shape[1] * GROUP
    alpha = (2.0 * depth) ** 0.25

    hp = x_prompt.reshape(bp * lp, d_model)
    hs = jnp.transpose(x_sample, (1, 0, 2)).reshape(ls * bs, d_model)
    outs = [[] for _ in range(7)]
    w_out = w_out.astype(BF16)
    w_in = jnp.swapaxes(w_in, 1, 2)
    for l in range(depth):
        lw = tuple(t[l] for t in (w_in, conv_qkv_w, dn_a_log, dn_dt_bias, dn_norm_g, gm_ln_g, gm_ln_b, gm_ws,
                                  gm_bs, gm_out_g, w_out, ln1_g, ln1_b))
        ffn_w = (conv_ffn_w[l], conv_ffn_b[l])
        ln2 = (ln2_g[l], ln2_b[l], alpha)
        n_pos = ls if l == depth - 1 else 0
        bq = jnp.transpose(state_conv_qkv[l], (1, 0, 2)).reshape((CONV_QKV - 1) * bs, d_qkv)
        bf = jnp.transpose(state_conv_ffn[l], (1, 0, 2)).reshape((CONV_FFN - 1) * bs, -1)
        x1_s, x1b_s, s_s, h_s, v_s, w_in_b = _mixer_half(hs, lw, alpha, 1, ls * bs, bs, (state_dn[l], bq), None)
        x1_p, x1b_p, s_p, h_p, _, _ = _mixer_half(hp, lw, alpha, bp, lp, 1, None, w_in_b)
        act_p, tg_p, tv_p, w_gate_b, w_val_b = _ffn_up(x1b_p, w_up[l], *ffn_w, bp, lp, FFN_TM, FFN_TN)
        act_s, cf_s = _ffn_up_carried(x1b_s, w_gate_b, w_val_b, *ffn_w, bf, FFN_TN, bs)
        hs, w_down_b = _down_ln(act_s, w_down[l], x1_s, *ln2, min(ls * bs, DOWN_EMIT_TM), DOWN_EMIT_TK,
                                emit_w=True, n_pos=n_pos)
        hp = _down_ln(act_p, w_down_b, x1_p, *ln2, min(bp * lp, DOWN_TM), DOWN_TK)
        h_s3 = h_s.reshape(ls, bs, -1)
        cq_s = jnp.transpose(h_s3[ls - (CONV_QKV - 1):, :, 2 * d_a:2 * d_a + d_qkv], (1, 0, 2))
        h_p3 = h_p.reshape(bp, lp, -1)
        cq_p = h_p3[:, lp - (CONV_QKV - 1):, 2 * d_a:2 * d_a + d_qkv]
        n_tail = tg_p.shape[1]
        cf_p = jnp.concatenate([tg_p, tv_p], axis=-1)[:, n_tail - (CONV_FFN - 1):]
        for acc, val in zip(outs, (s_p, cq_p, cf_p, s_s, cq_s, cf_s, v_s)):
            acc.append(val)
    return (hp.reshape(bp, lp, d_model), hs) + tuple(jnp.stack(o) for o in outs)
```

```python
import functools

import jax
import jax.numpy as jnp
from jax import lax
from jax.experimental import pallas as pl
from jax.experimental.pallas import tpu as pltpu

F32 = jnp.float32
BF16 = jnp.bfloat16

LANES = 128
SUBLANES = 8
VMEM_LIMIT = 56 * 1024 * 1024
DOWN_VMEM_LIMIT = 62 * 1024 * 1024

GROUP = 128
MLP_CHUNK = 128
DN_CHUNK = 64
CONV_QKV = 4
CONV_FFN = 3
CMLP_CHUNKS = 8
EPS = 1e-5

PROJ_TM, PROJ_TN = 1024, 1024
QKV_TM = 512
DELTA_SEQS = 4
DELTA_SAMPLE_ROWS = 16
OUT_TM = 512
FFN_TM, FFN_TN = 1024, 512
DOWN_TM, DOWN_TK = 1024, 1408
DOWN_EMIT_TM, DOWN_EMIT_TK = 512, 1408

_NT = (((1,), (1,)), ((), ()))
_TN = (((0,), (0,)), ((), ()))


def _params(sem, vmem_limit=VMEM_LIMIT):
    return pltpu.CompilerParams(dimension_semantics=sem, vmem_limit_bytes=vmem_limit)


def _bdot(a, b):
    return jnp.dot(a.astype(BF16), b.astype(BF16), preferred_element_type=F32)


def _gelu(x):
    return 0.5 * x * (1.0 + lax.erf(x * (0.5 ** 0.5)))


def _silu(x):
    return x * jax.nn.sigmoid(x)


def _layer_norm(x, g, b):
    mu = jnp.mean(x, axis=-1, keepdims=True)
    xc = x - mu
    var = jnp.mean(xc * xc, axis=-1, keepdims=True)
    return xc * lax.rsqrt(var + EPS) * g + b


def _rms_norm(x, g):
    ms = jnp.mean(x * x, axis=-1, keepdims=True)
    return x * lax.rsqrt(ms + EPS) * g


def _mm_kernel(x_ref, w_ref, we_ref, o_ref, e_ref, *rest):
    xb_ref = rest[-1]

    @pl.when(pl.program_id(1) == 0)
    def _():
        xb = x_ref[...].astype(BF16)
        xb_ref[...] = xb
        e_ref[...] = lax.dot_general(xb, we_ref[...], _NT, preferred_element_type=F32)

    wb = w_ref[...].astype(BF16)
    if len(rest) == 2:
        rest[0][...] = wb
    o_ref[...] = lax.dot_general(xb_ref[...], wb, _NT, preferred_element_type=F32)


def _matmul(x, wt, wt_extra, n_cols, tm, tn, emit_w=False):
    m, k = x.shape
    ne = wt_extra.shape[0]
    out_specs = [pl.BlockSpec((tm, tn), lambda i, j: (i, j)), pl.BlockSpec((tm, ne), lambda i, j: (i, 0))]
    out_shape = [jax.ShapeDtypeStruct((m, n_cols), F32), jax.ShapeDtypeStruct((m, ne), F32)]
    if emit_w:
        out_specs.append(pl.BlockSpec((tn, k), lambda i, j: (j, 0)))
        out_shape.append(jax.ShapeDtypeStruct((n_cols, k), BF16))
    return pl.pallas_call(
        _mm_kernel,
        grid=(m // tm, n_cols // tn),
        in_specs=[pl.BlockSpec((tm, k), lambda i, j: (i, 0)),
                  pl.BlockSpec((tn, k), lambda i, j: (j, 0)),
                  pl.BlockSpec((ne, k), lambda i, j: (0, 0))],
        out_specs=out_specs,
        out_shape=out_shape,
        scratch_shapes=[pltpu.VMEM((tm, k), BF16)],
        compiler_params=_params(("arbitrary", "arbitrary")),
        name="proj",
    )(x, wt, wt_extra)


def _cmlp_prompt_kernel(u_ref, v_ref, lng_ref, lnb_ref, ws_ref, bst_ref, og_ref, ya_ref, *, n_groups, n_chunks):
    row = lax.broadcasted_iota(jnp.int32, (MLP_CHUNK, MLP_CHUNK), 0)
    col = lax.broadcasted_iota(jnp.int32, (MLP_CHUNK, MLP_CHUNK), 1)
    cg = [(c, g) for c in range(n_chunks) for g in range(n_groups)]
    rows = lambda c: slice(c * MLP_CHUNK, (c + 1) * MLP_CHUNK)
    cols = lambda g: slice(g * GROUP, (g + 1) * GROUP)
    w = [jnp.where(row >= col, ws_ref[g], 0.0).astype(BF16) for g in range(n_groups)]
    n = range(len(cg))
    gv = [_gelu(v_ref[rows(c), cols(g)]) for c, g in cg]
    mu = [jnp.mean(gv[i], axis=-1, keepdims=True) for i in n]
    xc = [gv[i] - mu[i] for i in n]
    var = [jnp.mean(xc[i] * xc[i], axis=-1, keepdims=True) for i in n]
    vh = [xc[i] * lax.rsqrt(var[i] + EPS) * lng_ref[:, cols(g)] + lnb_ref[:, cols(g)] for i, (c, g) in enumerate(cg)]
    mixed = [jnp.dot(w[g], vh[i].astype(BF16), preferred_element_type=F32) + bst_ref[:, g:g + 1]
             for i, (c, g) in enumerate(cg)]
    y = [_gelu(u_ref[rows(c), cols(g)]) * mixed[i] for i, (c, g) in enumerate(cg)]
    ms = [jnp.mean(y[i] * y[i], axis=-1, keepdims=True) for i in n]
    for i, (c, g) in enumerate(cg):
        ya_ref[rows(c), cols(g)] = (y[i] * lax.rsqrt(ms[i] + EPS) * og_ref[:, cols(g)]).astype(ya_ref.dtype)


def _chunk_mlp_prompt(h, ln_g, ln_b, ws, bs, out_g):
    m = h.shape[0]
    n_groups, d_a = ws.shape[0], ws.shape[0] * GROUP
    vec = pl.BlockSpec((1, d_a), lambda i: (0, 0))
    tm = CMLP_CHUNKS * MLP_CHUNK
    return pl.pallas_call(
        functools.partial(_cmlp_prompt_kernel, n_groups=n_groups, n_chunks=CMLP_CHUNKS),
        grid=(m // tm,),
        in_specs=[pl.BlockSpec((tm, d_a), lambda i: (i, 0)),
                  pl.BlockSpec((tm, d_a), lambda i: (i, 1)),
                  vec, vec,
                  pl.BlockSpec((n_groups, MLP_CHUNK, MLP_CHUNK), lambda i: (0, 0, 0)),
                  pl.BlockSpec((MLP_CHUNK, n_groups), lambda i: (0, 0)),
                  vec],
        out_specs=pl.BlockSpec((tm, d_a), lambda i: (i, 0)),
        out_shape=jax.ShapeDtypeStruct((m, d_a), BF16),
        compiler_params=_params(("arbitrary",)),
        name="chunk_mlp_prompt",
    )(h, h, ln_g.reshape(1, d_a), ln_b.reshape(1, d_a), ws, bs.T, out_g.reshape(1, d_a))


def _cmlp_sample_kernel(u_ref, v_ref, lng_ref, lnb_ref, wv_ref, bv_ref, og_ref, ya_ref, vr_ref, *, n_pos):
    vh = []
    for t in range(n_pos):
        vt = _layer_norm(_gelu(v_ref[t]), lng_ref[0], lnb_ref[0])
        vr_ref[:, t, :] = vt
        vh.append(vt)
    for t in range(n_pos):
        mixed = wv_ref[t, 0:1, :] * vh[0]
        for s in range(1, t + 1):
            mixed = mixed + wv_ref[t, s:s + 1, :] * vh[s]
        mixed = mixed + bv_ref[0, t:t + 1, :]
        y = _gelu(u_ref[t]) * mixed
        ya_ref[t] = _rms_norm(y, og_ref[0]).astype(ya_ref.dtype)


def _chunk_mlp_sample(h3, ln_g, ln_b, ws, bs, out_g):
    n_pos, nb, _ = h3.shape
    n_groups = ws.shape[0]
    d_a = n_groups * GROUP
    wv = jnp.repeat(jnp.transpose(ws[:, :n_pos, :n_pos], (1, 2, 0)), GROUP, axis=-1)
    bv = jnp.repeat(bs[:, :n_pos].T, GROUP, axis=-1).reshape(1, n_pos, d_a)
    vec = pl.BlockSpec((1, 1, GROUP), lambda g: (0, 0, g))
    act = pl.BlockSpec((n_pos, nb, GROUP), lambda g: (0, 0, g))
    return pl.pallas_call(
        functools.partial(_cmlp_sample_kernel, n_pos=n_pos),
        grid=(n_groups,),
        in_specs=[act,
                  pl.BlockSpec((n_pos, nb, GROUP), lambda g: (0, 0, n_groups + g)),
                  vec, vec,
                  pl.BlockSpec((n_pos, n_pos, GROUP), lambda g: (0, 0, g)),
                  pl.BlockSpec((1, n_pos, GROUP), lambda g: (0, 0, g)),
                  vec],
        out_specs=[act, pl.BlockSpec((nb, n_pos, GROUP), lambda g: (0, 0, g))],
        out_shape=[jax.ShapeDtypeStruct((n_pos, nb, d_a), BF16),
                   jax.ShapeDtypeStruct((nb, n_pos, d_a), F32)],
        compiler_params=_params(("arbitrary",)),
        name="chunk_mlp_sample",
    )(h3, h3, ln_g.reshape(1, 1, d_a), ln_b.reshape(1, 1, d_a), wv, bv, out_g.reshape(1, 1, d_a))


def _conv_taps(xbuf_ref, w_ref, tm, shift, halo, width):
    acc = None
    for j in range(width):
        off = halo - (width - 1 - j) * shift
        term = w_ref[j:j + 1, :] * xbuf_ref[off:off + tm, :]
        acc = term if acc is None else acc + term
    return acc


def _qkv_prep_kernel(*refs, tm, shift, halo, n_heads, has_buf):
    if has_buf:
        (xq_ref, xk_ref, xv_ref, bq_ref, bk_ref, bv_ref, wq_ref, wk_ref, wv_ref,
         q_ref, k_ref, v_ref, xbuf_ref, carry_ref) = refs
        bufs = (bq_ref, bk_ref, bv_ref)
    else:
        (xq_ref, xk_ref, xv_ref, wq_ref, wk_ref, wv_ref,
         q_ref, k_ref, v_ref, xbuf_ref, carry_ref) = refs
        bufs = (None, None, None)
    first = pl.program_id(1) == 0
    parts = ((xq_ref, wq_ref, q_ref, GROUP ** -0.5), (xk_ref, wk_ref, k_ref, 1.0), (xv_ref, wv_ref, v_ref, None))
    for p, (x_ref, w_ref, o_ref, scale) in enumerate(parts):
        @pl.when(first)
        def _(p=p):
            if has_buf:
                xbuf_ref[0:halo, :] = bufs[p][...]
            else:
                xbuf_ref[0:halo, :] = jnp.zeros((halo, xbuf_ref.shape[1]), F32)

        @pl.when(jnp.logical_not(first))
        def _(p=p):
            xbuf_ref[0:halo, :] = carry_ref[p]

        xbuf_ref[halo:halo + tm, :] = x_ref[...]
        carry_ref[p] = xbuf_ref[tm:tm + halo, :]
        y = _silu(_conv_taps(xbuf_ref, w_ref, tm, shift, halo, CONV_QKV))
        if scale is None:
            o_ref[...] = y
        else:
            for h in range(n_heads):
                sl = slice(h * GROUP, (h + 1) * GROUP)
                yh = y[:, sl]
                inv = lax.rsqrt(jnp.sum(yh * yh, axis=-1, keepdims=True) + 1e-6)
                o_ref[:, sl] = yh * (inv * scale if scale != 1.0 else inv)


def _qkv_prep(h, conv_w, bufs, n_seq, rows_per_seq, tm, shift, col0):
    d_b = conv_w.shape[1] // 3
    n_heads = d_b // GROUP
    has_buf = bufs is not None
    halo = (CONV_QKV - 1) * shift if has_buf else SUBLANES
    tiles = rows_per_seq // tm
    cb = col0 // d_b
    x_specs = [pl.BlockSpec((tm, d_b), lambda s, i, c=c: (s * tiles + i, cb + c)) for c in range(3)]
    w_specs = [pl.BlockSpec((CONV_QKV, d_b), lambda s, i, c=c: (0, c)) for c in range(3)]
    o_spec = pl.BlockSpec((tm, d_b), lambda s, i: (s * tiles + i, 0))
    args, specs = [h, h, h], list(x_specs)
    if has_buf:
        specs += [pl.BlockSpec((halo, d_b), lambda s, i, c=c: (s, c)) for c in range(3)]
        args += [bufs, bufs, bufs]
    specs += w_specs
    args += [conv_w, conv_w, conv_w]
    m = n_seq * rows_per_seq
    return pl.pallas_call(
        functools.partial(_qkv_prep_kernel, tm=tm, shift=shift, halo=halo, n_heads=n_heads, has_buf=has_buf),
        grid=(n_seq, tiles),
        in_specs=specs,
        out_specs=[o_spec, o_spec, o_spec],
        out_shape=[jax.ShapeDtypeStruct((m, d_b), F32)] * 3,
        scratch_shapes=[pltpu.VMEM((halo + tm, d_b), F32), pltpu.VMEM((3, halo, d_b), F32)],
        compiler_params=_params(("arbitrary", "arbitrary")),
        name="qkv_prep",
    )(*args)


def _gates(ab, alog, dtb):
    g = -jnp.exp(alog) * jax.nn.softplus(ab + dtb)
    beta = jax.nn.sigmoid(ab)
    return g, beta


def _delta_prompt_kernel(q_ref, k_ref, v_ref, z_ref, ab_ref, alog_ref, dtb_ref, ng_ref,
                         yb_ref, sout_ref, s_ref, *, n_heads, n_chunks, ns):
    c = DN_CHUNK
    ci = pl.program_id(1)

    @pl.when(ci == 0)
    def _():
        s_ref[...] = jnp.zeros(s_ref.shape, F32)

    row = lax.broadcasted_iota(jnp.int32, (c, c), 0)
    col = lax.broadcasted_iota(jnp.int32, (c, c), 1)
    causal = row >= col
    strict = row > col
    eye = (row == col).astype(F32)
    tril = causal.astype(F32)
    n_double = (c - 1).bit_length() - 1
    ch = [(s, h) for s in range(ns) for h in range(n_heads)]
    sl = lambda h: slice(h * GROUP, (h + 1) * GROUP)
    gates = [_gates(ab_ref[s], alog_ref[...], dtb_ref[...]) for s in range(ns)]
    gc_s = [jnp.dot(tril, gates[s][0], precision=lax.Precision.HIGHEST, preferred_element_type=F32)
            for s in range(ns)]
    gct_s = [g.T for g in gc_s]
    q = [q_ref[s, :, sl(h)] for s, h in ch]
    k = [k_ref[s, :, sl(h)] for s, h in ch]
    gcol = [gc_s[s][:, h:h + 1] for s, h in ch]
    beta = [gates[s][1][:, n_heads + h:n_heads + h + 1] for s, h in ch]
    n = range(len(ch))
    decay = [jnp.exp(jnp.where(causal, gcol[i] - gct_s[s][h:h + 1, :], -jnp.inf)) for i, (s, h) in enumerate(ch)]
    kb = [k[i] * beta[i] for i in n]
    kbf = [k[i].astype(BF16) for i in n]
    kq = [lax.dot_general(jnp.concatenate([kb[i], q[i]], axis=0).astype(BF16), kbf[i], _NT,
                          preferred_element_type=F32) for i in n]
    qk = [kq[i][c:] * decay[i] for i in n]
    y = [jnp.where(strict, -(kq[i][:c] * decay[i]), 0.0) for i in n]
    x = [eye + y[i] for i in n]
    y = [_bdot(y[i], y[i]) for i in n]
    for _ in range(n_double - 1):
        t = [_bdot(jnp.concatenate([x[i], y[i]], axis=0), y[i]) for i in n]
        x = [x[i] + t[i][:c] for i in n]
        y = [t[i][c:] for i in n]
    x = [x[i] + _bdot(x[i], y[i]) for i in n]
    egc = [jnp.exp(gcol[i]) for i in n]
    sol = [_bdot(x[i], jnp.concatenate([v_ref[s, :, sl(h)] * beta[i], kb[i] * egc[i]], axis=1))
           for i, (s, h) in enumerate(ch)]
    s_old = [s_ref[s, h] for s, h in ch]
    r = [_bdot(jnp.concatenate([sol[i][:, GROUP:], q[i] * egc[i]], axis=0), s_old[i]) for i in n]
    v_new = [sol[i][:, :GROUP] - r[i][:c] for i in n]
    o = [r[i][c:] + _bdot(qk[i], v_new[i]) for i in n]
    gl = [gc_s[s][c - 1:c, h:h + 1] for s, h in ch]
    upd = [lax.dot_general((k[i] * jnp.exp(gl[i] - gcol[i])).astype(BF16), v_new[i].astype(BF16), _TN,
                           preferred_element_type=F32) for i in n]
    ms = [jnp.mean(o[i] * o[i], axis=-1, keepdims=True) for i in n]
    for i, (s, h) in enumerate(ch):
        s_ref[s, h] = s_old[i] * jnp.exp(gl[i]) + upd[i]
        on = o[i] * lax.rsqrt(ms[i] + EPS) * ng_ref[...]
        yb_ref[s, :, sl(h)] = (on * _silu(z_ref[s, :, sl(h)])).astype(yb_ref.dtype)

    @pl.when(ci == n_chunks - 1)
    def _():
        sout_ref[...] = s_ref[...]


def _delta_prompt(q, k, v, h, z_col0, ab, alog, dtb, norm_g, n_seq, rows_per_seq, ns):
    d_b = q.shape[1]
    n_heads = d_b // GROUP
    n_chunks = rows_per_seq // DN_CHUNK
    r3 = lambda t: t.reshape(n_seq, rows_per_seq, t.shape[-1])
    act = pl.BlockSpec((ns, DN_CHUNK, d_b), lambda s, i: (s, i, 0))
    vec = pl.BlockSpec((1, LANES), lambda s, i: (0, 0))
    zb = z_col0 // d_b
    yb, s_new = pl.pallas_call(
        functools.partial(_delta_prompt_kernel, n_heads=n_heads, n_chunks=n_chunks, ns=ns),
        grid=(n_seq // ns, n_chunks),
        in_specs=[act, act, act,
                  pl.BlockSpec((ns, DN_CHUNK, d_b), lambda s, i: (s, i, zb)),
                  pl.BlockSpec((ns, DN_CHUNK, LANES), lambda s, i: (s, i, 0)),
                  vec, vec, vec],
        out_specs=[act, pl.BlockSpec((ns, n_heads, GROUP, GROUP), lambda s, i: (s, 0, 0, 0))],
        out_shape=[jax.ShapeDtypeStruct((n_seq, rows_per_seq, d_b), BF16),
                   jax.ShapeDtypeStruct((n_seq, n_heads, GROUP, GROUP), F32)],
        scratch_shapes=[pltpu.VMEM((ns, n_heads, GROUP, GROUP), F32)],
        compiler_params=_params(("arbitrary", "arbitrary")),
        name="delta_prompt",
    )(r3(q), r3(k), r3(v), r3(h), r3(ab), alog, dtb, norm_g)
    return yb.reshape(n_seq * rows_per_seq, d_b), s_new


def _delta_sample_kernel(q_ref, k_ref, v_ref, z_ref, ab_ref, alog_ref, dtb_ref, ng_ref, s0_ref,
                         yb_ref, s1_ref, lhs_ref, res_ref, kt_ref, vn_ref, gl_ref, *, n_heads, n_pos, bb):
    nt = n_pos
    hs = range(n_heads)
    sls = [slice(h * GROUP, (h + 1) * GROUP) for h in hs]
    gates = [_gates(ab_ref[t], alog_ref[...], dtb_ref[...]) for t in range(nt)]
    zeros_pad = jnp.zeros((bb, GROUP), F32)
    ts = range(nt)
    q = [[q_ref[t, :, sls[h]] for t in ts] for h in hs]
    k = [[k_ref[t, :, sls[h]] for t in ts] for h in hs]
    beta = [[gates[t][1][:, n_heads + h:n_heads + h + 1] for t in ts] for h in hs]
    gc = []
    for h in hs:
        g = [gates[t][0][:, h:h + 1] for t in ts]
        for t in range(1, nt):
            g[t] = g[t - 1] + g[t]
        gc.append(g)
    kb = [[k[h][t] * beta[h][t] for t in ts] for h in hs]
    kk = [[[jnp.sum(kb[h][t] * k[h][s], axis=-1, keepdims=True) for s in range(t)] for t in ts] for h in hs]
    qk = [[[jnp.sum(q[h][t] * k[h][s], axis=-1, keepdims=True) for s in range(t + 1)] for t in ts] for h in hs]
    dec = [[[jnp.exp(gc[h][t] - gc[h][s]) for s in range(t + 1)] for t in ts] for h in hs]
    a = [[[kk[h][t][s] * dec[h][t][s] for s in range(t)] for t in ts] for h in hs]
    qk = [[[qk[h][t][s] * dec[h][t][s] for s in range(t + 1)] for t in ts] for h in hs]
    egc = [[jnp.exp(gc[h][t]) for t in ts] for h in hs]
    u_blk = [[v_ref[t, :, sls[h]] * beta[h][t] for t in ts] for h in hs]
    w_blk = [[kb[h][t] * egc[h][t] for t in ts] for h in hs]
    for t in ts:
        for h in hs:
            for s in range(t):
                u_blk[h][t] = u_blk[h][t] - a[h][t][s] * u_blk[h][s]
                w_blk[h][t] = w_blk[h][t] - a[h][t][s] * w_blk[h][s]
    for h in hs:
        for t in ts:
            lhs_ref[h, :, t, :] = w_blk[h][t]
            lhs_ref[h, :, nt + t, :] = q[h][t] * egc[h][t]
        gl_ref[h] = jnp.broadcast_to(egc[h][nt - 1], (bb, GROUP))

    for h in hs:
        for b in range(bb):
            res_ref[h, b] = _bdot(lhs_ref[h, b], s0_ref[b, h])

    v_new = [[u_blk[h][t] - res_ref[h, :, t, :] for t in ts] for h in hs]
    o = [[res_ref[h, :, nt + t, :] for t in ts] for h in hs]
    for h in hs:
        for t in ts:
            for s in range(t + 1):
                o[h][t] = o[h][t] + qk[h][t][s] * v_new[h][s]
    ms = [[jnp.mean(o[h][t] * o[h][t], axis=-1, keepdims=True) for t in ts] for h in hs]
    for h in hs:
        gl = gc[h][nt - 1]
        for t in ts:
            on = o[h][t] * lax.rsqrt(ms[h][t] + EPS) * ng_ref[...]
            yb_ref[t, :, sls[h]] = (on * _silu(z_ref[t, :, sls[h]])).astype(yb_ref.dtype)
            kt_ref[h, :, t, :] = k[h][t] * jnp.exp(gl - gc[h][t])
            vn_ref[h, :, t, :] = v_new[h][t]
            kt_ref[h, :, nt + t, :] = zeros_pad
            vn_ref[h, :, nt + t, :] = zeros_pad

    for h in hs:
        for b in range(bb):
            upd = lax.dot_general(kt_ref[h, b].astype(BF16), vn_ref[h, b].astype(BF16), _TN,
                                  preferred_element_type=F32)
            s1_ref[b, h] = s0_ref[b, h] * gl_ref[h, b:b + 1, :] + upd


def _delta_sample(q, k, v, h3, z_col0, ab, alog, dtb, norm_g, s0, bb):
    n_pos, nb, d_b = q.shape
    n_heads = d_b // GROUP
    assert 2 * n_pos == SUBLANES, "one state read packs [w | q] rows of a sequence into one 8-row tile"
    zb = z_col0 // d_b
    act = pl.BlockSpec((n_pos, bb, d_b), lambda i: (0, i, 0))
    vec = pl.BlockSpec((1, LANES), lambda i: (0, 0))
    st = pl.BlockSpec((bb, n_heads, GROUP, GROUP), lambda i: (i, 0, 0, 0))
    tile = pltpu.VMEM((n_heads, bb, SUBLANES, GROUP), F32)
    return pl.pallas_call(
        functools.partial(_delta_sample_kernel, n_heads=n_heads, n_pos=n_pos, bb=bb),
        grid=(nb // bb,),
        in_specs=[act, act, act,
                  pl.BlockSpec((n_pos, bb, d_b), lambda i: (0, i, zb)),
                  pl.BlockSpec((n_pos, bb, LANES), lambda i: (0, i, 0)),
                  vec, vec, vec, st],
        out_specs=[act, st],
        out_shape=[jax.ShapeDtypeStruct((n_pos, nb, d_b), BF16),
                   jax.ShapeDtypeStruct(s0.shape, F32)],
        scratch_shapes=[tile, tile, tile, tile, pltpu.VMEM((n_heads, bb, GROUP), F32)],
        compiler_params=_params(("arbitrary",)),
        name="delta_sample",
    )(q, k, v, h3, ab, alog, dtb, norm_g, s0)


def _out_proj_kernel(ya_ref, yb_ref, w_ref, r_ref, g_ref, b_ref, o_ref, ob_ref, z_ref, *, n_tiles, d_a, alpha):
    i = pl.program_id(0)

    @pl.when(i == 0)
    def _():
        z_ref[...] = jnp.zeros(z_ref.shape, F32)

    def emit_ln():
        y = _layer_norm(z_ref[...], g_ref[...], b_ref[...])
        o_ref[...] = y
        ob_ref[...] = y.astype(ob_ref.dtype)

    @pl.when(i < n_tiles)
    def _():
        emit_ln()
        acc = (jnp.dot(ya_ref[...], w_ref[0:d_a, :], preferred_element_type=F32)
               + jnp.dot(yb_ref[...], w_ref[d_a:, :], preferred_element_type=F32))
        z_ref[...] = alpha * r_ref[...] + acc

    @pl.when(i == n_tiles)
    def _():
        emit_ln()


def _out_proj_ln(ya, yb, w, resid, g, b, alpha, tm):
    m, d_a = ya.shape
    k, d = w.shape
    n_tiles = m // tm
    cur = lambda i: (jnp.minimum(i, n_tiles - 1), 0)
    prev = lambda i: (jnp.maximum(i - 1, 0), 0)
    vec = pl.BlockSpec((1, d), lambda i: (0, 0))
    return pl.pallas_call(
        functools.partial(_out_proj_kernel, n_tiles=n_tiles, d_a=d_a, alpha=alpha),
        grid=(n_tiles + 1,),
        in_specs=[pl.BlockSpec((tm, d_a), cur),
                  pl.BlockSpec((tm, k - d_a), cur),
                  pl.BlockSpec((k, d), lambda i: (0, 0), pipeline_mode=pl.Buffered(1)),
                  pl.BlockSpec((tm, d), cur), vec, vec],
        out_specs=[pl.BlockSpec((tm, d), prev), pl.BlockSpec((tm, d), prev)],
        out_shape=[jax.ShapeDtypeStruct((m, d), F32), jax.ShapeDtypeStruct((m, d), BF16)],
        scratch_shapes=[pltpu.VMEM((tm, d), F32)],
        compiler_params=_params(("arbitrary",)),
        name="out_proj_ln1",
    )(ya, yb, w, resid, g.reshape(1, d), b.reshape(1, d))


def _down_kernel(x_ref, w_ref, r_ref, g_ref, b_ref, o_ref, *rest, n_tiles, nk, nc, alpha, n_pos):
    z_ref = rest[-1]
    cw = z_ref.shape[-1]
    slab = lambda c: slice(c * cw, (c + 1) * cw)
    i, kk = pl.program_id(0), pl.program_id(1)

    def emit_ln():
        z = [z_ref[c] for c in range(nc)]
        inv_d = 1.0 / (nc * cw)
        mu = sum(jnp.sum(zc, axis=-1, keepdims=True) for zc in z) * inv_d
        xc = [zc - mu for zc in z]
        var = sum(jnp.sum(c * c, axis=-1, keepdims=True) for c in xc) * inv_d
        rs = lax.rsqrt(var + EPS)
        for c in range(nc):
            y = xc[c] * rs * g_ref[:, slab(c)] + b_ref[:, slab(c)]
            if n_pos:
                nb = y.shape[0] // n_pos
                for t in range(n_pos):
                    o_ref[:, t, slab(c)] = y[t * nb:(t + 1) * nb]
            else:
                o_ref[:, slab(c)] = y

    @pl.when(jnp.logical_and(i == 0, kk == 0))
    def _():
        z_ref[...] = jnp.zeros(z_ref.shape, F32)

    wb = w_ref[...].astype(BF16)
    if len(rest) == 2:
        rest[0][...] = wb

    @pl.when(kk == 0)
    def _():
        emit_ln()
        part = jnp.dot(x_ref[...], wb, preferred_element_type=F32)
        for c in range(nc):
            if nc == nk:
                z_ref[c] = part[:, slab(c)] + alpha * r_ref[...] if c == 0 else part[:, slab(c)]
            else:
                z_ref[c] = part[:, slab(c)] + alpha * r_ref[:, slab(c)]

    @pl.when(jnp.logical_and(kk > 0, i < n_tiles))
    def _():
        part = jnp.dot(x_ref[...], wb, preferred_element_type=F32)
        for c in range(nc):
            z_ref[c] += part[:, slab(c)]
        if nc == nk:
            z_ref[kk] += alpha * r_ref[...]


def _down_ln(x, w, resid, g, b, alpha, tm, tk, emit_w=False, n_pos=0):
    m, kdim = x.shape
    d = w.shape[1]
    nk = kdim // tk
    n_tiles = m // tm
    nc = nk if (d % nk == 0 and (d // nk) % LANES == 0) else 1
    row = lambda i: jnp.minimum(i, n_tiles - 1)
    kt = lambda i, kk: jnp.where(i == n_tiles, nk - 1, kk)
    vec = pl.BlockSpec((1, d), lambda i, kk: (0, 0))
    wspec = pl.BlockSpec((tk, d), lambda i, kk: (kt(i, kk), 0))
    if nc == nk:
        rspec = pl.BlockSpec((tm, d // nc), lambda i, kk: (row(i), kt(i, kk)))
    else:
        rspec = pl.BlockSpec((tm, d), lambda i, kk: (row(i), 0))
    if n_pos:
        assert n_tiles == 1
        out_specs = [pl.BlockSpec((m // n_pos, n_pos, d), lambda i, kk: (0, 0, 0))]
        out_shape = [jax.ShapeDtypeStruct((m // n_pos, n_pos, d), F32)]
    else:
        out_specs = [pl.BlockSpec((tm, d), lambda i, kk: (jnp.where(kk == 0, jnp.maximum(i - 1, 0), row(i)), 0))]
        out_shape = [jax.ShapeDtypeStruct((m, d), F32)]
    if emit_w:
        out_specs.append(wspec)
        out_shape.append(jax.ShapeDtypeStruct(w.shape, BF16))
    outs = pl.pallas_call(
        functools.partial(_down_kernel, n_tiles=n_tiles, nk=nk, nc=nc, alpha=alpha, n_pos=n_pos),
        grid=(n_tiles + 1, nk),
        in_specs=[pl.BlockSpec((tm, tk), lambda i, kk: (row(i), kt(i, kk))), wspec, rspec, vec, vec],
        out_specs=out_specs,
        out_shape=out_shape,
        scratch_shapes=[pltpu.VMEM((nc, tm, d // nc), F32)],
        compiler_params=_params(("arbitrary", "arbitrary"), DOWN_VMEM_LIMIT),
        name="down_ln2",
    )(x, w, resid, g.reshape(1, d), b.reshape(1, d))
    return tuple(outs) if emit_w else outs[0]


def _ffn_up_kernel(x_ref, wg_ref, wv_ref, cwg_ref, cwv_ref, cbg_ref, cbv_ref,
                   act_ref, tg_ref, tv_ref, wgo_ref, wvo_ref, wgb_ref, wvb_ref, carry_ref, *, tm):
    halo = SUBLANES
    first = pl.program_id(2) == 0

    @pl.when(jnp.logical_and(pl.program_id(1) == 0, first))
    def _():
        for w_ref, wb_ref, wo_ref in ((wg_ref, wgb_ref, wgo_ref), (wv_ref, wvb_ref, wvo_ref)):
            wb = w_ref[...].astype(BF16)
            wb_ref[...] = wb
            wo_ref[...] = wb

    @pl.when(first)
    def _():
        carry_ref[...] = jnp.zeros(carry_ref.shape, F32)

    x = x_ref[...]
    conv = []
    parts = ((wgb_ref, cwg_ref, cbg_ref, tg_ref), (wvb_ref, cwv_ref, cbv_ref, tv_ref))
    row = lax.broadcasted_iota(jnp.int32, (halo, act_ref.shape[1]), 0)
    for p, (w_ref, cw_ref, cb_ref, t_ref) in enumerate(parts):
        up = jnp.dot(x, w_ref[...], preferred_element_type=F32)
        hist = carry_ref[p]
        tail = up[tm - halo:tm, :]
        carry_ref[p] = tail
        t_ref[0] = tail
        acc = cw_ref[CONV_FFN - 1:CONV_FFN, :] * up
        for dd in range(1, CONV_FFN):
            xs = pltpu.roll(up, dd, axis=0)
            head = jnp.where(row < dd, pltpu.roll(hist, dd, axis=0), xs[0:halo])
            xs = jnp.concatenate([head, xs[halo:]], axis=0)
            acc = acc + cw_ref[CONV_FFN - 1 - dd:CONV_FFN - dd, :] * xs
        conv.append(acc + cb_ref[...])
    act_ref[...] = (_silu(conv[0]) * conv[1]).astype(act_ref.dtype)


def _ffn_up_carried_kernel(x_ref, wg_ref, wv_ref, buf_ref, cw_ref, cb_ref, act_ref, t_ref, xbuf_ref, gate_ref,
                           *, shift):
    p, j = pl.program_id(0), pl.program_id(1)
    tm, halo = x_ref.shape[0], buf_ref.shape[0]
    xbuf_ref[0:halo, :] = buf_ref[...]

    def conv_of(w_ref):
        xbuf_ref[halo:halo + tm, :] = jnp.dot(x_ref[...], w_ref[...], preferred_element_type=F32)
        tail = xbuf_ref[tm:tm + halo, :]
        for t in range(halo // shift):
            t_ref[:, t, :] = tail[t * shift:(t + 1) * shift]
        return _conv_taps(xbuf_ref, cw_ref, tm, shift, halo, CONV_FFN) + cb_ref[...]

    @pl.when(p == 0)
    def _():
        gate_ref[j] = conv_of(wg_ref)

    @pl.when(p == 1)
    def _():
        act_ref[...] = (_silu(gate_ref[j]) * conv_of(wv_ref)).astype(act_ref.dtype)


def _ffn_up_carried(x, w_gate_b, w_val_b, conv_w, conv_b, bufs, tn, shift):
    m, d = x.shape
    d_ff = w_gate_b.shape[1]
    nj = d_ff // tn
    halo = (CONV_FFN - 1) * shift
    col = lambda p, j: (0, p * nj + j)
    return pl.pallas_call(
        functools.partial(_ffn_up_carried_kernel, shift=shift),
        grid=(2, nj),
        in_specs=[pl.BlockSpec((m, d), lambda p, j: (0, 0)),
                  pl.BlockSpec((d, tn), lambda p, j: (0, jnp.where(p == 0, j, nj - 1))),
                  pl.BlockSpec((d, tn), lambda p, j: (0, jnp.where(p == 1, j, 0))),
                  pl.BlockSpec((halo, tn), col),
                  pl.BlockSpec((CONV_FFN, tn), col),
                  pl.BlockSpec((1, tn), col)],
        out_specs=[pl.BlockSpec((m, tn), lambda p, j: (0, jnp.where(p == 1, j, 0))),
                   pl.BlockSpec((shift, halo // shift, tn), lambda p, j: (0, 0, p * nj + j))],
        out_shape=[jax.ShapeDtypeStruct((m, d_ff), BF16),
                   jax.ShapeDtypeStruct((shift, halo // shift, 2 * d_ff), F32)],
        scratch_shapes=[pltpu.VMEM((halo + m, tn), F32), pltpu.VMEM((nj, m, tn), F32)],
        compiler_params=_params(("arbitrary", "arbitrary")),
        name="ffn_up_carried",
    )(x, w_gate_b, w_val_b, bufs, conv_w, conv_b.reshape(1, 2 * d_ff))


def _ffn_up(x, w_up, conv_w, conv_b, n_seq, rows_per_seq, tm, tn):
    m, d = x.shape
    d_ff = w_up.shape[1] // 2
    nj = d_ff // tn
    halo = SUBLANES
    tiles = rows_per_seq // tm
    wspec = lambda off: pl.BlockSpec((d, tn), lambda j, s, i: (0, off + j))
    cwspec = lambda off: pl.BlockSpec((CONV_FFN, tn), lambda j, s, i: (0, off + j))
    cbspec = lambda off: pl.BlockSpec((1, tn), lambda j, s, i: (0, off + j))
    tspec = pl.BlockSpec((1, halo, tn), lambda j, s, i: (s, 0, j))
    tshape = jax.ShapeDtypeStruct((n_seq, halo, d_ff), F32)
    cb2 = conv_b.reshape(1, 2 * d_ff)
    return pl.pallas_call(
        functools.partial(_ffn_up_kernel, tm=tm),
        grid=(nj, n_seq, tiles),
        in_specs=[pl.BlockSpec((tm, d), lambda j, s, i: (s * tiles + i, 0)), wspec(0), wspec(nj),
                  cwspec(0), cwspec(nj), cbspec(0), cbspec(nj)],
        out_specs=[pl.BlockSpec((tm, tn), lambda j, s, i: (s * tiles + i, j)), tspec, tspec, wspec(0), wspec(0)],
        out_shape=[jax.ShapeDtypeStruct((m, d_ff), BF16), tshape, tshape]
                  + [jax.ShapeDtypeStruct((d, d_ff), BF16)] * 2,
        scratch_shapes=[pltpu.VMEM((d, tn), BF16)] * 2 + [pltpu.VMEM((2, halo, tn), F32)],
        compiler_params=_params(("arbitrary", "arbitrary", "arbitrary")),
        name="ffn_up",
    )(x, w_up, w_up, conv_w, conv_w, cb2, cb2)


def _pad_lanes(vec, offset=0):
    out = jnp.zeros((1, LANES), F32)
    return out.at[0, offset:offset + vec.shape[0]].set(vec.astype(F32))


def _mixer_half(x2, lw, alpha, n_seq, rows_per_seq, shift, state, w_in_b):
    (w_in, conv_qkv_w, a_log, dt_bias, norm_g, gm_ln_g, gm_ln_b, gm_ws, gm_bs, gm_out_g, w_out,
     ln1_g, ln1_b) = lw
    m, d_model = x2.shape
    n_groups = gm_ws.shape[0]
    d_a = n_groups * GROUP
    d_qkv = conv_qkv_w.shape[1]
    d_b = d_qkv // 3
    n_heads = d_b // GROUP
    main_cols = 2 * d_a + d_qkv + d_b
    w_ab = jnp.pad(w_in[main_cols:].astype(BF16), ((0, LANES - 2 * n_heads), (0, 0)))
    if w_in_b is None:
        h, ab, w_in_b = _matmul(x2, w_in, w_ab, main_cols, min(m, PROJ_TM), PROJ_TN, emit_w=True)
    else:
        h, ab = _matmul(x2, w_in_b, w_ab, main_cols, min(m, PROJ_TM), PROJ_TN)
    alog, dtb, ng = _pad_lanes(a_log), _pad_lanes(dt_bias), norm_g.reshape(1, GROUP).astype(F32)

    if state is None:
        ya = _chunk_mlp_prompt(h, gm_ln_g, gm_ln_b, gm_ws, gm_bs, gm_out_g)
        v_rows = None
        q, k, v = _qkv_prep(h, conv_qkv_w, None, n_seq, rows_per_seq, QKV_TM, shift, 2 * d_a)
        yb, s_new = _delta_prompt(q, k, v, h, 2 * d_a + d_qkv, ab, alog, dtb, ng, n_seq, rows_per_seq, DELTA_SEQS)
    else:
        s_dn, buf_qkv = state
        n_pos = rows_per_seq // shift
        h3 = h.reshape(n_pos, shift, main_cols)
        ya3, v_rows = _chunk_mlp_sample(h3, gm_ln_g, gm_ln_b, gm_ws, gm_bs, gm_out_g)
        ya = ya3.reshape(m, d_a)
        q, k, v = _qkv_prep(h, conv_qkv_w, buf_qkv, n_seq, rows_per_seq, rows_per_seq, shift, 2 * d_a)
        r3 = lambda t: t.reshape(n_pos, shift, t.shape[-1])
        yb3, s_new = _delta_sample(r3(q), r3(k), r3(v), h3, 2 * d_a + d_qkv, r3(ab), alog, dtb, ng, s_dn,
                                   DELTA_SAMPLE_ROWS)
        yb = yb3.reshape(m, d_b)

    x1, x1b = _out_proj_ln(ya, yb, w_out, x2, ln1_g, ln1_b, alpha, min(m, OUT_TM))
    return x1, x1b, s_new, h, v_rows, w_in_b


def kernel(x_prompt, x_sample, state_dn, state_conv_qkv, state_conv_ffn, w_in, conv_qkv_w, dn_a_log, dn_dt_bias, dn_norm_g, gm_ln_g, gm_ln_b, gm_ws, gm_bs, gm_out_g, w_out, ln1_g, ln1_b, w_up, conv_ffn_w, conv_ffn_b, w_down, ln2_g, ln2_b):
    depth = w_in.shape[0]
    bp, lp, d_model = x_prompt.shape
    bs, ls, _ = x_sample.shape
    d_qkv = conv_qkv_w.shape[-1]
    d_a = gm_ws.shape[1] * GROUP
    alpha = (2.0 * depth) ** 0.25

    hp = x_prompt.reshape(bp * lp, d_model)
    hs = jnp.transpose(x_sample, (1, 0, 2)).reshape(ls * bs, d_model)
    outs = [[] for _ in range(7)]
    w_out = w_out.astype(BF16)
    w_in = jnp.swapaxes(w_in, 1, 2)
    for l in range(depth):
        lw = tuple(t[l] for t in (w_in, conv_qkv_w, dn_a_log, dn_dt_bias, dn_norm_g, gm_ln_g, gm_ln_b, gm_ws,
                                  gm_bs, gm_out_g, w_out, ln1_g, ln1_b))
        ffn_w = (conv_ffn_w[l], conv_ffn_b[l])
        ln2 = (ln2_g[l], ln2_b[l], alpha)
        n_pos = ls if l == depth - 1 else 0
        bq = jnp.transpose(state_conv_qkv[l], (1, 0, 2)).reshape((CONV_QKV - 1) * bs, d_qkv)
        bf = jnp.transpose(state_conv_ffn[l], (1, 0, 2)).reshape((CONV_FFN - 1) * bs, -1)
        x1_s, x1b_s, s_s, h_s, v_s, w_in_b = _mixer_half(hs, lw, alpha, 1, ls * bs, bs, (state_dn[l], bq), None)
        x1_p, x1b_p, s_p, h_p, _, _ = _mixer_half(hp, lw, alpha, bp, lp, 1, None, w_in_b)
        act_p, tg_p, tv_p, w_gate_b, w_val_b = _ffn_up(x1b_p, w_up[l], *ffn_w, bp, lp, FFN_TM, FFN_TN)
        act_s, cf_s = _ffn_up_carried(x1b_s, w_gate_b, w_val_b, *ffn_w, bf, FFN_TN, bs)
        hs, w_down_b = _down_ln(act_s, w_down[l], x1_s, *ln2, min(ls * bs, DOWN_EMIT_TM), DOWN_EMIT_TK,
                                emit_w=True, n_pos=n_pos)
        hp = _down_ln(act_p, w_down_b, x1_p, *ln2, min(bp * lp, DOWN_TM), DOWN_TK)
        h_s3 = h_s.reshape(ls, bs, -1)
        cq_s = jnp.transpose(h_s3[ls - (CONV_QKV - 1):, :, 2 * d_a:2 * d_a + d_qkv], (1, 0, 2))
        h_p3 = h_p.reshape(bp, lp, -1)
        cq_p = h_p3[:, lp - (CONV_QKV - 1):, 2 * d_a:2 * d_a + d_qkv]
        n_tail = tg_p.shape[1]
        cf_p = jnp.concatenate([tg_p, tv_p], axis=-1)[:, n_tail - (CONV_FFN - 1):]
        for acc, val in zip(outs, (s_p, cq_p, cf_p, s_s, cq_s, cf_s, v_s)):
            acc.append(val)
    return (hp.reshape(bp, lp, d_model), hs) + tuple(jnp.stack(o) for o in outs)
```

```python
import functools

import jax
import jax.numpy as jnp
from jax import lax
from jax.experimental import pallas as pl
from jax.experimental.pallas import tpu as pltpu

F32 = jnp.float32
BF16 = jnp.bfloat16

LANES = 128
SUBLANES = 8
VMEM_LIMIT = 56 * 1024 * 1024
DOWN_VMEM_LIMIT = 62 * 1024 * 1024

GROUP = 128
MLP_CHUNK = 128
DN_CHUNK = 64
CONV_QKV = 4
CONV_FFN = 3
CMLP_CHUNKS = 8
EPS = 1e-5

PROJ_TM, PROJ_TN = 1024, 1536
QKV_TM = 512
DELTA_SEQS = 4
DELTA_SAMPLE_ROWS = 16
OUT_TM = 512
FFN_TM, FFN_TN = 1024, 512
DOWN_TM, DOWN_TK = 1024, 1408
DOWN_EMIT_TM, DOWN_EMIT_TK = 512, 1408

_NT = (((1,), (1,)), ((), ()))
_TN = (((0,), (0,)), ((), ()))


def _params(sem, vmem_limit=VMEM_LIMIT):
    return pltpu.CompilerParams(dimension_semantics=sem, vmem_limit_bytes=vmem_limit)


def _bdot(a, b):
    return jnp.dot(a.astype(BF16), b.astype(BF16), preferred_element_type=F32)


def _gelu(x):
    return 0.5 * x * (1.0 + lax.erf(x * (0.5 ** 0.5)))


def _silu(x):
    return x * jax.nn.sigmoid(x)


def _layer_norm(x, g, b):
    mu = jnp.mean(x, axis=-1, keepdims=True)
    xc = x - mu
    var = jnp.mean(xc * xc, axis=-1, keepdims=True)
    return xc * lax.rsqrt(var + EPS) * g + b


def _rms_norm(x, g):
    ms = jnp.mean(x * x, axis=-1, keepdims=True)
    return x * lax.rsqrt(ms + EPS) * g


def _mm_kernel(x_ref, w_ref, we_ref, o_ref, e_ref, *rest):
    xb_ref = rest[-1]

    @pl.when(pl.program_id(1) == 0)
    def _():
        xb = x_ref[...].astype(BF16)
        xb_ref[...] = xb
        e_ref[...] = lax.dot_general(xb, we_ref[...], _NT, preferred_element_type=F32)

    wb = w_ref[...].astype(BF16)
    if len(rest) == 2:
        rest[0][...] = wb
    o_ref[...] = lax.dot_general(xb_ref[...], wb, _NT, preferred_element_type=F32)


def _matmul(x, wt, wt_extra, n_cols, tm, tn, emit_w=False):
    m, k = x.shape
    ne = wt_extra.shape[0]
    out_specs = [pl.BlockSpec((tm, tn), lambda i, j: (i, j)), pl.BlockSpec((tm, ne), lambda i, j: (i, 0))]
    out_shape = [jax.ShapeDtypeStruct((m, n_cols), F32), jax.ShapeDtypeStruct((m, ne), F32)]
    if emit_w:
        out_specs.append(pl.BlockSpec((tn, k), lambda i, j: (j, 0)))
        out_shape.append(jax.ShapeDtypeStruct((n_cols, k), BF16))
    return pl.pallas_call(
        _mm_kernel,
        grid=(m // tm, n_cols // tn),
        in_specs=[pl.BlockSpec((tm, k), lambda i, j: (i, 0)),
                  pl.BlockSpec((tn, k), lambda i, j: (j, 0)),
                  pl.BlockSpec((ne, k), lambda i, j: (0, 0))],
        out_specs=out_specs,
        out_shape=out_shape,
        scratch_shapes=[pltpu.VMEM((tm, k), BF16)],
        compiler_params=_params(("arbitrary", "arbitrary")),
        name="proj",
    )(x, wt, wt_extra)


def _cmlp_prompt_kernel(u_ref, v_ref, lng_ref, lnb_ref, ws_ref, bst_ref, og_ref, ya_ref, *, n_groups, n_chunks):
    row = lax.broadcasted_iota(jnp.int32, (MLP_CHUNK, MLP_CHUNK), 0)
    col = lax.broadcasted_iota(jnp.int32, (MLP_CHUNK, MLP_CHUNK), 1)
    cg = [(c, g) for c in range(n_chunks) for g in range(n_groups)]
    rows = lambda c: slice(c * MLP_CHUNK, (c + 1) * MLP_CHUNK)
    cols = lambda g: slice(g * GROUP, (g + 1) * GROUP)
    w = [jnp.where(row >= col, ws_ref[g], 0.0).astype(BF16) for g in range(n_groups)]
    n = range(len(cg))
    gv = [_gelu(v_ref[rows(c), cols(g)]) for c, g in cg]
    mu = [jnp.mean(gv[i], axis=-1, keepdims=True) for i in n]
    xc = [gv[i] - mu[i] for i in n]
    var = [jnp.mean(xc[i] * xc[i], axis=-1, keepdims=True) for i in n]
    vh = [xc[i] * lax.rsqrt(var[i] + EPS) * lng_ref[:, cols(g)] + lnb_ref[:, cols(g)] for i, (c, g) in enumerate(cg)]
    mixed = [jnp.dot(w[g], vh[i].astype(BF16), preferred_element_type=F32) + bst_ref[:, g:g + 1]
             for i, (c, g) in enumerate(cg)]
    y = [_gelu(u_ref[rows(c), cols(g)]) * mixed[i] for i, (c, g) in enumerate(cg)]
    ms = [jnp.mean(y[i] * y[i], axis=-1, keepdims=True) for i in n]
    for i, (c, g) in enumerate(cg):
        ya_ref[rows(c), cols(g)] = (y[i] * lax.rsqrt(ms[i] + EPS) * og_ref[:, cols(g)]).astype(ya_ref.dtype)


def _chunk_mlp_prompt(h, ln_g, ln_b, ws, bs, out_g):
    m = h.shape[0]
    n_groups, d_a = ws.shape[0], ws.shape[0] * GROUP
    vec = pl.BlockSpec((1, d_a), lambda i: (0, 0))
    tm = CMLP_CHUNKS * MLP_CHUNK
    return pl.pallas_call(
        functools.partial(_cmlp_prompt_kernel, n_groups=n_groups, n_chunks=CMLP_CHUNKS),
        grid=(m // tm,),
        in_specs=[pl.BlockSpec((tm, d_a), lambda i: (i, 0)),
                  pl.BlockSpec((tm, d_a), lambda i: (i, 1)),
                  vec, vec,
                  pl.BlockSpec((n_groups, MLP_CHUNK, MLP_CHUNK), lambda i: (0, 0, 0)),
                  pl.BlockSpec((MLP_CHUNK, n_groups), lambda i: (0, 0)),
                  vec],
        out_specs=pl.BlockSpec((tm, d_a), lambda i: (i, 0)),
        out_shape=jax.ShapeDtypeStruct((m, d_a), BF16),
        compiler_params=_params(("arbitrary",)),
        name="chunk_mlp_prompt",
    )(h, h, ln_g.reshape(1, d_a), ln_b.reshape(1, d_a), ws, bs.T, out_g.reshape(1, d_a))


def _cmlp_sample_kernel(u_ref, v_ref, lng_ref, lnb_ref, wv_ref, bv_ref, og_ref, ya_ref, vr_ref, *, n_pos):
    vh = []
    for t in range(n_pos):
        vt = _layer_norm(_gelu(v_ref[t]), lng_ref[0], lnb_ref[0])
        vr_ref[:, t, :] = vt
        vh.append(vt)
    for t in range(n_pos):
        mixed = wv_ref[t, 0:1, :] * vh[0]
        for s in range(1, t + 1):
            mixed = mixed + wv_ref[t, s:s + 1, :] * vh[s]
        mixed = mixed + bv_ref[0, t:t + 1, :]
        y = _gelu(u_ref[t]) * mixed
        ya_ref[t] = _rms_norm(y, og_ref[0]).astype(ya_ref.dtype)


def _chunk_mlp_sample(h3, ln_g, ln_b, ws, bs, out_g):
    n_pos, nb, _ = h3.shape
    n_groups = ws.shape[0]
    d_a = n_groups * GROUP
    wv = jnp.repeat(jnp.transpose(ws[:, :n_pos, :n_pos], (1, 2, 0)), GROUP, axis=-1)
    bv = jnp.repeat(bs[:, :n_pos].T, GROUP, axis=-1).reshape(1, n_pos, d_a)
    vec = pl.BlockSpec((1, 1, GROUP), lambda g: (0, 0, g))
    act = pl.BlockSpec((n_pos, nb, GROUP), lambda g: (0, 0, g))
    return pl.pallas_call(
        functools.partial(_cmlp_sample_kernel, n_pos=n_pos),
        grid=(n_groups,),
        in_specs=[act,
                  pl.BlockSpec((n_pos, nb, GROUP), lambda g: (0, 0, n_groups + g)),
                  vec, vec,
                  pl.BlockSpec((n_pos, n_pos, GROUP), lambda g: (0, 0, g)),
                  pl.BlockSpec((1, n_pos, GROUP), lambda g: (0, 0, g)),
                  vec],
        out_specs=[act, pl.BlockSpec((nb, n_pos, GROUP), lambda g: (0, 0, g))],
        out_shape=[jax.ShapeDtypeStruct((n_pos, nb, d_a), BF16),
                   jax.ShapeDtypeStruct((nb, n_pos, d_a), F32)],
        compiler_params=_params(("arbitrary",)),
        name="chunk_mlp_sample",
    )(h3, h3, ln_g.reshape(1, 1, d_a), ln_b.reshape(1, 1, d_a), wv, bv, out_g.reshape(1, 1, d_a))


def _conv_taps(xbuf_ref, w_ref, tm, shift, halo, width):
    acc = None
    for j in range(width):
        off = halo - (width - 1 - j) * shift
        term = w_ref[j:j + 1, :] * xbuf_ref[off:off + tm, :]
        acc = term if acc is None else acc + term
    return acc


def _qkv_prep_kernel(*refs, tm, shift, halo, n_heads, has_buf):
    if has_buf:
        (xq_ref, xk_ref, xv_ref, bq_ref, bk_ref, bv_ref, wq_ref, wk_ref, wv_ref,
         q_ref, k_ref, v_ref, xbuf_ref, carry_ref) = refs
        bufs = (bq_ref, bk_ref, bv_ref)
    else:
        (xq_ref, xk_ref, xv_ref, wq_ref, wk_ref, wv_ref,
         q_ref, k_ref, v_ref, xbuf_ref, carry_ref) = refs
        bufs = (None, None, None)
    first = pl.program_id(1) == 0
    parts = ((xq_ref, wq_ref, q_ref, GROUP ** -0.5), (xk_ref, wk_ref, k_ref, 1.0), (xv_ref, wv_ref, v_ref, None))
    for p, (x_ref, w_ref, o_ref, scale) in enumerate(parts):
        @pl.when(first)
        def _(p=p):
            if has_buf:
                xbuf_ref[0:halo, :] = bufs[p][...]
            else:
                xbuf_ref[0:halo, :] = jnp.zeros((halo, xbuf_ref.shape[1]), F32)

        @pl.when(jnp.logical_not(first))
        def _(p=p):
            xbuf_ref[0:halo, :] = carry_ref[p]

        xbuf_ref[halo:halo + tm, :] = x_ref[...]
        carry_ref[p] = xbuf_ref[tm:tm + halo, :]
        y = _silu(_conv_taps(xbuf_ref, w_ref, tm, shift, halo, CONV_QKV))
        if scale is None:
            o_ref[...] = y
        else:
            for h in range(n_heads):
                sl = slice(h * GROUP, (h + 1) * GROUP)
                yh = y[:, sl]
                inv = lax.rsqrt(jnp.sum(yh * yh, axis=-1, keepdims=True) + 1e-6)
                o_ref[:, sl] = yh * (inv * scale if scale != 1.0 else inv)


def _qkv_prep(h, conv_w, bufs, n_seq, rows_per_seq, tm, shift, col0):
    d_b = conv_w.shape[1] // 3
    n_heads = d_b // GROUP
    has_buf = bufs is not None
    halo = (CONV_QKV - 1) * shift if has_buf else SUBLANES
    tiles = rows_per_seq // tm
    cb = col0 // d_b
    x_specs = [pl.BlockSpec((tm, d_b), lambda s, i, c=c: (s * tiles + i, cb + c)) for c in range(3)]
    w_specs = [pl.BlockSpec((CONV_QKV, d_b), lambda s, i, c=c: (0, c)) for c in range(3)]
    o_spec = pl.BlockSpec((tm, d_b), lambda s, i: (s * tiles + i, 0))
    args, specs = [h, h, h], list(x_specs)
    if has_buf:
        specs += [pl.BlockSpec((halo, d_b), lambda s, i, c=c: (s, c)) for c in range(3)]
        args += [bufs, bufs, bufs]
    specs += w_specs
    args += [conv_w, conv_w, conv_w]
    m = n_seq * rows_per_seq
    return pl.pallas_call(
        functools.partial(_qkv_prep_kernel, tm=tm, shift=shift, halo=halo, n_heads=n_heads, has_buf=has_buf),
        grid=(n_seq, tiles),
        in_specs=specs,
        out_specs=[o_spec, o_spec, o_spec],
        out_shape=[jax.ShapeDtypeStruct((m, d_b), F32)] * 3,
        scratch_shapes=[pltpu.VMEM((halo + tm, d_b), F32), pltpu.VMEM((3, halo, d_b), F32)],
        compiler_params=_params(("arbitrary", "arbitrary")),
        name="qkv_prep",
    )(*args)


def _gates(ab, alog, dtb):
    g = -jnp.exp(alog) * jax.nn.softplus(ab + dtb)
    beta = jax.nn.sigmoid(ab)
    return g, beta


def _delta_prompt_kernel(q_ref, k_ref, v_ref, z_ref, ab_ref, alog_ref, dtb_ref, ng_ref,
                         yb_ref, sout_ref, s_ref, *, n_heads, n_chunks, ns):
    c = DN_CHUNK
    ci = pl.program_id(1)

    @pl.when(ci == 0)
    def _():
        s_ref[...] = jnp.zeros(s_ref.shape, F32)

    row = lax.broadcasted_iota(jnp.int32, (c, c), 0)
    col = lax.broadcasted_iota(jnp.int32, (c, c), 1)
    causal = row >= col
    strict = row > col
    eye = (row == col).astype(F32)
    tril = causal.astype(F32)
    n_double = (c - 1).bit_length() - 1
    ch = [(s, h) for s in range(ns) for h in range(n_heads)]
    sl = lambda h: slice(h * GROUP, (h + 1) * GROUP)
    gates = [_gates(ab_ref[s], alog_ref[...], dtb_ref[...]) for s in range(ns)]
    gc_s = [jnp.dot(tril, gates[s][0], precision=lax.Precision.HIGHEST, preferred_element_type=F32)
            for s in range(ns)]
    gct_s = [g.T for g in gc_s]
    q = [q_ref[s, :, sl(h)] for s, h in ch]
    k = [k_ref[s, :, sl(h)] for s, h in ch]
    gcol = [gc_s[s][:, h:h + 1] for s, h in ch]
    beta = [gates[s][1][:, n_heads + h:n_heads + h + 1] for s, h in ch]
    n = range(len(ch))
    decay = [jnp.exp(jnp.where(causal, gcol[i] - gct_s[s][h:h + 1, :], -jnp.inf)) for i, (s, h) in enumerate(ch)]
    kb = [k[i] * beta[i] for i in n]
    kbf = [k[i].astype(BF16) for i in n]
    kq = [lax.dot_general(jnp.concatenate([kb[i], q[i]], axis=0).astype(BF16), kbf[i], _NT,
                          preferred_element_type=F32) for i in n]
    qk = [kq[i][c:] * decay[i] for i in n]
    y = [jnp.where(strict, -(kq[i][:c] * decay[i]), 0.0) for i in n]
    x = [eye + y[i] for i in n]
    y = [_bdot(y[i], y[i]) for i in n]
    for _ in range(n_double - 1):
        t = [_bdot(jnp.concatenate([x[i], y[i]], axis=0), y[i]) for i in n]
        x = [x[i] + t[i][:c] for i in n]
        y = [t[i][c:] for i in n]
    x = [x[i] + _bdot(x[i], y[i]) for i in n]
    egc = [jnp.exp(gcol[i]) for i in n]
    sol = [_bdot(x[i], jnp.concatenate([v_ref[s, :, sl(h)] * beta[i], kb[i] * egc[i]], axis=1))
           for i, (s, h) in enumerate(ch)]
    s_old = [s_ref[s, h] for s, h in ch]
    r = [_bdot(jnp.concatenate([sol[i][:, GROUP:], q[i] * egc[i]], axis=0), s_old[i]) for i in n]
    v_new = [sol[i][:, :GROUP] - r[i][:c] for i in n]
    o = [r[i][c:] + _bdot(qk[i], v_new[i]) for i in n]
    gl = [gc_s[s][c - 1:c, h:h + 1] for s, h in ch]
    upd = [lax.dot_general((k[i] * jnp.exp(gl[i] - gcol[i])).astype(BF16), v_new[i].astype(BF16), _TN,
                           preferred_element_type=F32) for i in n]
    ms = [jnp.mean(o[i] * o[i], axis=-1, keepdims=True) for i in n]
    for i, (s, h) in enumerate(ch):
        s_ref[s, h] = s_old[i] * jnp.exp(gl[i]) + upd[i]
        on = o[i] * lax.rsqrt(ms[i] + EPS) * ng_ref[...]
        yb_ref[s, :, sl(h)] = (on * _silu(z_ref[s, :, sl(h)])).astype(yb_ref.dtype)

    @pl.when(ci == n_chunks - 1)
    def _():
        sout_ref[...] = s_ref[...]


def _delta_prompt(q, k, v, h, z_col0, ab, alog, dtb, norm_g, n_seq, rows_per_seq, ns):
    d_b = q.shape[1]
    n_heads = d_b // GROUP
    n_chunks = rows_per_seq // DN_CHUNK
    r3 = lambda t: t.reshape(n_seq, rows_per_seq, t.shape[-1])
    act = pl.BlockSpec((ns, DN_CHUNK, d_b), lambda s, i: (s, i, 0))
    vec = pl.BlockSpec((1, LANES), lambda s, i: (0, 0))
    zb = z_col0 // d_b
    yb, s_new = pl.pallas_call(
        functools.partial(_delta_prompt_kernel, n_heads=n_heads, n_chunks=n_chunks, ns=ns),
        grid=(n_seq // ns, n_chunks),
        in_specs=[act, act, act,
                  pl.BlockSpec((ns, DN_CHUNK, d_b), lambda s, i: (s, i, zb)),
                  pl.BlockSpec((ns, DN_CHUNK, LANES), lambda s, i: (s, i, 0)),
                  vec, vec, vec],
        out_specs=[act, pl.BlockSpec((ns, n_heads, GROUP, GROUP), lambda s, i: (s, 0, 0, 0))],
        out_shape=[jax.ShapeDtypeStruct((n_seq, rows_per_seq, d_b), BF16),
                   jax.ShapeDtypeStruct((n_seq, n_heads, GROUP, GROUP), F32)],
        scratch_shapes=[pltpu.VMEM((ns, n_heads, GROUP, GROUP), F32)],
        compiler_params=_params(("arbitrary", "arbitrary")),
        name="delta_prompt",
    )(r3(q), r3(k), r3(v), r3(h), r3(ab), alog, dtb, norm_g)
    return yb.reshape(n_seq * rows_per_seq, d_b), s_new


def _delta_sample_kernel(q_ref, k_ref, v_ref, z_ref, ab_ref, alog_ref, dtb_ref, ng_ref, s0_ref,
                         yb_ref, s1_ref, lhs_ref, res_ref, kt_ref, vn_ref, gl_ref, *, n_heads, n_pos, bb):
    nt = n_pos
    hs = range(n_heads)
    sls = [slice(h * GROUP, (h + 1) * GROUP) for h in hs]
    gates = [_gates(ab_ref[t], alog_ref[...], dtb_ref[...]) for t in range(nt)]
    zeros_pad = jnp.zeros((bb, GROUP), F32)
    ts = range(nt)
    q = [[q_ref[t, :, sls[h]] for t in ts] for h in hs]
    k = [[k_ref[t, :, sls[h]] for t in ts] for h in hs]
    beta = [[gates[t][1][:, n_heads + h:n_heads + h + 1] for t in ts] for h in hs]
    gc = []
    for h in hs:
        g = [gates[t][0][:, h:h + 1] for t in ts]
        for t in range(1, nt):
            g[t] = g[t - 1] + g[t]
        gc.append(g)
    kb = [[k[h][t] * beta[h][t] for t in ts] for h in hs]
    kk = [[[jnp.sum(kb[h][t] * k[h][s], axis=-1, keepdims=True) for s in range(t)] for t in ts] for h in hs]
    qk = [[[jnp.sum(q[h][t] * k[h][s], axis=-1, keepdims=True) for s in range(t + 1)] for t in ts] for h in hs]
    dec = [[[jnp.exp(gc[h][t] - gc[h][s]) for s in range(t + 1)] for t in ts] for h in hs]
    a = [[[kk[h][t][s] * dec[h][t][s] for s in range(t)] for t in ts] for h in hs]
    qk = [[[qk[h][t][s] * dec[h][t][s] for s in range(t + 1)] for t in ts] for h in hs]
    egc = [[jnp.exp(gc[h][t]) for t in ts] for h in hs]
    u_blk = [[v_ref[t, :, sls[h]] * beta[h][t] for t in ts] for h in hs]
    w_blk = [[kb[h][t] * egc[h][t] for t in ts] for h in hs]
    for t in ts:
        for h in hs:
            for s in range(t):
                u_blk[h][t] = u_blk[h][t] - a[h][t][s] * u_blk[h][s]
                w_blk[h][t] = w_blk[h][t] - a[h][t][s] * w_blk[h][s]
    for h in hs:
        for t in ts:
            lhs_ref[h, :, t, :] = w_blk[h][t]
            lhs_ref[h, :, nt + t, :] = q[h][t] * egc[h][t]
        gl_ref[h] = jnp.broadcast_to(egc[h][nt - 1], (bb, GROUP))

    for h in hs:
        for b in range(bb):
            res_ref[h, b] = _bdot(lhs_ref[h, b], s0_ref[b, h])

    v_new = [[u_blk[h][t] - res_ref[h, :, t, :] for t in ts] for h in hs]
    o = [[res_ref[h, :, nt + t, :] for t in ts] for h in hs]
    for h in hs:
        for t in ts:
            for s in range(t + 1):
                o[h][t] = o[h][t] + qk[h][t][s] * v_new[h][s]
    ms = [[jnp.mean(o[h][t] * o[h][t], axis=-1, keepdims=True) for t in ts] for h in hs]
    for h in hs:
        gl = gc[h][nt - 1]
        for t in ts:
            on = o[h][t] * lax.rsqrt(ms[h][t] + EPS) * ng_ref[...]
            yb_ref[t, :, sls[h]] = (on * _silu(z_ref[t, :, sls[h]])).astype(yb_ref.dtype)
            kt_ref[h, :, t, :] = k[h][t] * jnp.exp(gl - gc[h][t])
            vn_ref[h, :, t, :] = v_new[h][t]
            kt_ref[h, :, nt + t, :] = zeros_pad
            vn_ref[h, :, nt + t, :] = zeros_pad

    for h in hs:
        for b in range(bb):
            upd = lax.dot_general(kt_ref[h, b].astype(BF16), vn_ref[h, b].astype(BF16), _TN,
                                  preferred_element_type=F32)
            s1_ref[b, h] = s0_ref[b, h] * gl_ref[h, b:b + 1, :] + upd


def _delta_sample(q, k, v, h3, z_col0, ab, alog, dtb, norm_g, s0, bb):
    n_pos, nb, d_b = q.shape
    n_heads = d_b // GROUP
    assert 2 * n_pos == SUBLANES, "one state read packs [w | q] rows of a sequence into one 8-row tile"
    zb = z_col0 // d_b
    act = pl.BlockSpec((n_pos, bb, d_b), lambda i: (0, i, 0))
    vec = pl.BlockSpec((1, LANES), lambda i: (0, 0))
    st = pl.BlockSpec((bb, n_heads, GROUP, GROUP), lambda i: (i, 0, 0, 0))
    tile = pltpu.VMEM((n_heads, bb, SUBLANES, GROUP), F32)
    return pl.pallas_call(
        functools.partial(_delta_sample_kernel, n_heads=n_heads, n_pos=n_pos, bb=bb),
        grid=(nb // bb,),
        in_specs=[act, act, act,
                  pl.BlockSpec((n_pos, bb, d_b), lambda i: (0, i, zb)),
                  pl.BlockSpec((n_pos, bb, LANES), lambda i: (0, i, 0)),
                  vec, vec, vec, st],
        out_specs=[act, st],
        out_shape=[jax.ShapeDtypeStruct((n_pos, nb, d_b), BF16),
                   jax.ShapeDtypeStruct(s0.shape, F32)],
        scratch_shapes=[tile, tile, tile, tile, pltpu.VMEM((n_heads, bb, GROUP), F32)],
        compiler_params=_params(("arbitrary",)),
        name="delta_sample",
    )(q, k, v, h3, ab, alog, dtb, norm_g, s0)


def _out_proj_kernel(ya_ref, yb_ref, w_ref, r_ref, g_ref, b_ref, o_ref, ob_ref, z_ref, *, n_tiles, d_a, alpha):
    i = pl.program_id(0)

    @pl.when(i == 0)
    def _():
        z_ref[...] = jnp.zeros(z_ref.shape, F32)

    def emit_ln():
        y = _layer_norm(z_ref[...], g_ref[...], b_ref[...])
        o_ref[...] = y
        ob_ref[...] = y.astype(ob_ref.dtype)

    @pl.when(i < n_tiles)
    def _():
        emit_ln()
        acc = (jnp.dot(ya_ref[...], w_ref[0:d_a, :], preferred_element_type=F32)
               + jnp.dot(yb_ref[...], w_ref[d_a:, :], preferred_element_type=F32))
        z_ref[...] = alpha * r_ref[...] + acc

    @pl.when(i == n_tiles)
    def _():
        emit_ln()


def _out_proj_ln(ya, yb, w, resid, g, b, alpha, tm):
    m, d_a = ya.shape
    k, d = w.shape
    n_tiles = m // tm
    cur = lambda i: (jnp.minimum(i, n_tiles - 1), 0)
    prev = lambda i: (jnp.maximum(i - 1, 0), 0)
    vec = pl.BlockSpec((1, d), lambda i: (0, 0))
    return pl.pallas_call(
        functools.partial(_out_proj_kernel, n_tiles=n_tiles, d_a=d_a, alpha=alpha),
        grid=(n_tiles + 1,),
        in_specs=[pl.BlockSpec((tm, d_a), cur),
                  pl.BlockSpec((tm, k - d_a), cur),
                  pl.BlockSpec((k, d), lambda i: (0, 0), pipeline_mode=pl.Buffered(1)),
                  pl.BlockSpec((tm, d), cur), vec, vec],
        out_specs=[pl.BlockSpec((tm, d), prev), pl.BlockSpec((tm, d), prev)],
        out_shape=[jax.ShapeDtypeStruct((m, d), F32), jax.ShapeDtypeStruct((m, d), BF16)],
        scratch_shapes=[pltpu.VMEM((tm, d), F32)],
        compiler_params=_params(("arbitrary",)),
        name="out_proj_ln1",
    )(ya, yb, w, resid, g.reshape(1, d), b.reshape(1, d))


def _down_kernel(x_ref, w_ref, r_ref, g_ref, b_ref, o_ref, *rest, n_tiles, nk, nc, alpha, n_pos):
    z_ref = rest[-1]
    cw = z_ref.shape[-1]
    slab = lambda c: slice(c * cw, (c + 1) * cw)
    i, kk = pl.program_id(0), pl.program_id(1)

    def emit_ln():
        z = [z_ref[c] for c in range(nc)]
        inv_d = 1.0 / (nc * cw)
        mu = sum(jnp.sum(zc, axis=-1, keepdims=True) for zc in z) * inv_d
        xc = [zc - mu for zc in z]
        var = sum(jnp.sum(c * c, axis=-1, keepdims=True) for c in xc) * inv_d
        rs = lax.rsqrt(var + EPS)
        for c in range(nc):
            y = xc[c] * rs * g_ref[:, slab(c)] + b_ref[:, slab(c)]
            if n_pos:
                nb = y.shape[0] // n_pos
                for t in range(n_pos):
                    o_ref[:, t, slab(c)] = y[t * nb:(t + 1) * nb]
            else:
                o_ref[:, slab(c)] = y

    @pl.when(jnp.logical_and(i == 0, kk == 0))
    def _():
        z_ref[...] = jnp.zeros(z_ref.shape, F32)

    wb = w_ref[...].astype(BF16)
    if len(rest) == 2:
        rest[0][...] = wb

    @pl.when(kk == 0)
    def _():
        emit_ln()
        part = jnp.dot(x_ref[...], wb, preferred_element_type=F32)
        for c in range(nc):
            if nc == nk:
                z_ref[c] = part[:, slab(c)] + alpha * r_ref[...] if c == 0 else part[:, slab(c)]
            else:
                z_ref[c] = part[:, slab(c)] + alpha * r_ref[:, slab(c)]

    @pl.when(jnp.logical_and(kk > 0, i < n_tiles))
    def _():
        part = jnp.dot(x_ref[...], wb, preferred_element_type=F32)
        for c in range(nc):
            z_ref[c] += part[:, slab(c)]
        if nc == nk:
            z_ref[kk] += alpha * r_ref[...]


def _down_ln(x, w, resid, g, b, alpha, tm, tk, emit_w=False, n_pos=0):
    m, kdim = x.shape
    d = w.shape[1]
    nk = kdim // tk
    n_tiles = m // tm
    nc = nk if (d % nk == 0 and (d // nk) % LANES == 0) else 1
    row = lambda i: jnp.minimum(i, n_tiles - 1)
    kt = lambda i, kk: jnp.where(i == n_tiles, nk - 1, kk)
    vec = pl.BlockSpec((1, d), lambda i, kk: (0, 0))
    wspec = pl.BlockSpec((tk, d), lambda i, kk: (kt(i, kk), 0))
    if nc == nk:
        rspec = pl.BlockSpec((tm, d // nc), lambda i, kk: (row(i), kt(i, kk)))
    else:
        rspec = pl.BlockSpec((tm, d), lambda i, kk: (row(i), 0))
    if n_pos:
        assert n_tiles == 1
        out_specs = [pl.BlockSpec((m // n_pos, n_pos, d), lambda i, kk: (0, 0, 0))]
        out_shape = [jax.ShapeDtypeStruct((m // n_pos, n_pos, d), F32)]
    else:
        out_specs = [pl.BlockSpec((tm, d), lambda i, kk: (jnp.where(kk == 0, jnp.maximum(i - 1, 0), row(i)), 0))]
        out_shape = [jax.ShapeDtypeStruct((m, d), F32)]
    if emit_w:
        out_specs.append(wspec)
        out_shape.append(jax.ShapeDtypeStruct(w.shape, BF16))
    outs = pl.pallas_call(
        functools.partial(_down_kernel, n_tiles=n_tiles, nk=nk, nc=nc, alpha=alpha, n_pos=n_pos),
        grid=(n_tiles + 1, nk),
        in_specs=[pl.BlockSpec((tm, tk), lambda i, kk: (row(i), kt(i, kk))), wspec, rspec, vec, vec],
        out_specs=out_specs,
        out_shape=out_shape,
        scratch_shapes=[pltpu.VMEM((nc, tm, d // nc), F32)],
        compiler_params=_params(("arbitrary", "arbitrary"), DOWN_VMEM_LIMIT),
        name="down_ln2",
    )(x, w, resid, g.reshape(1, d), b.reshape(1, d))
    return tuple(outs) if emit_w else outs[0]


def _ffn_up_kernel(x_ref, wg_ref, wv_ref, cwg_ref, cwv_ref, cbg_ref, cbv_ref,
                   act_ref, tg_ref, tv_ref, wgo_ref, wvo_ref, wgb_ref, wvb_ref, carry_ref, *, tm):
    halo = SUBLANES
    first = pl.program_id(2) == 0

    @pl.when(jnp.logical_and(pl.program_id(1) == 0, first))
    def _():
        for w_ref, wb_ref, wo_ref in ((wg_ref, wgb_ref, wgo_ref), (wv_ref, wvb_ref, wvo_ref)):
            wb = w_ref[...].astype(BF16)
            wb_ref[...] = wb
            wo_ref[...] = wb

    @pl.when(first)
    def _():
        carry_ref[...] = jnp.zeros(carry_ref.shape, F32)

    x = x_ref[...]
    conv = []
    parts = ((wgb_ref, cwg_ref, cbg_ref, tg_ref), (wvb_ref, cwv_ref, cbv_ref, tv_ref))
    row = lax.broadcasted_iota(jnp.int32, (halo, act_ref.shape[1]), 0)
    for p, (w_ref, cw_ref, cb_ref, t_ref) in enumerate(parts):
        up = jnp.dot(x, w_ref[...], preferred_element_type=F32)
        hist = carry_ref[p]
        tail = up[tm - halo:tm, :]
        carry_ref[p] = tail
        t_ref[0] = tail
        acc = cw_ref[CONV_FFN - 1:CONV_FFN, :] * up
        for dd in range(1, CONV_FFN):
            xs = pltpu.roll(up, dd, axis=0)
            head = jnp.where(row < dd, pltpu.roll(hist, dd, axis=0), xs[0:halo])
            xs = jnp.concatenate([head, xs[halo:]], axis=0)
            acc = acc + cw_ref[CONV_FFN - 1 - dd:CONV_FFN - dd, :] * xs
        conv.append(acc + cb_ref[...])
    act_ref[...] = (_silu(conv[0]) * conv[1]).astype(act_ref.dtype)


def _ffn_up_carried_kernel(x_ref, wg_ref, wv_ref, buf_ref, cw_ref, cb_ref, act_ref, t_ref, xbuf_ref, gate_ref,
                           *, shift):
    p, j = pl.program_id(0), pl.program_id(1)
    tm, halo = x_ref.shape[0], buf_ref.shape[0]
    xbuf_ref[0:halo, :] = buf_ref[...]

    def conv_of(w_ref):
        xbuf_ref[halo:halo + tm, :] = jnp.dot(x_ref[...], w_ref[...], preferred_element_type=F32)
        tail = xbuf_ref[tm:tm + halo, :]
        for t in range(halo // shift):
            t_ref[:, t, :] = tail[t * shift:(t + 1) * shift]
        return _conv_taps(xbuf_ref, cw_ref, tm, shift, halo, CONV_FFN) + cb_ref[...]

    @pl.when(p == 0)
    def _():
        gate_ref[j] = conv_of(wg_ref)

    @pl.when(p == 1)
    def _():
        act_ref[...] = (_silu(gate_ref[j]) * conv_of(wv_ref)).astype(act_ref.dtype)


def _ffn_up_carried(x, w_gate_b, w_val_b, conv_w, conv_b, bufs, tn, shift):
    m, d = x.shape
    d_ff = w_gate_b.shape[1]
    nj = d_ff // tn
    halo = (CONV_FFN - 1) * shift
    col = lambda p, j: (0, p * nj + j)
    return pl.pallas_call(
        functools.partial(_ffn_up_carried_kernel, shift=shift),
        grid=(2, nj),
        in_specs=[pl.BlockSpec((m, d), lambda p, j: (0, 0)),
                  pl.BlockSpec((d, tn), lambda p, j: (0, jnp.where(p == 0, j, nj - 1))),
                  pl.BlockSpec((d, tn), lambda p, j: (0, jnp.where(p == 1, j, 0))),
                  pl.BlockSpec((halo, tn), col),
                  pl.BlockSpec((CONV_FFN, tn), col),
                  pl.BlockSpec((1, tn), col)],
        out_specs=[pl.BlockSpec((m, tn), lambda p, j: (0, jnp.where(p == 1, j, 0))),
                   pl.BlockSpec((shift, halo // shift, tn), lambda p, j: (0, 0, p * nj + j))],
        out_shape=[jax.ShapeDtypeStruct((m, d_ff), BF16),
                   jax.ShapeDtypeStruct((shift, halo // shift, 2 * d_ff), F32)],
        scratch_shapes=[pltpu.VMEM((halo + m, tn), F32), pltpu.VMEM((nj, m, tn), F32)],
        compiler_params=_params(("arbitrary", "arbitrary")),
        name="ffn_up_carried",
    )(x, w_gate_b, w_val_b, bufs, conv_w, conv_b.reshape(1, 2 * d_ff))


def _ffn_up(x, w_up, conv_w, conv_b, n_seq, rows_per_seq, tm, tn):
    m, d = x.shape
    d_ff = w_up.shape[1] // 2
    nj = d_ff // tn
    halo = SUBLANES
    tiles = rows_per_seq // tm
    wspec = lambda off: pl.BlockSpec((d, tn), lambda j, s, i: (0, off + j))
    cwspec = lambda off: pl.BlockSpec((CONV_FFN, tn), lambda j, s, i: (0, off + j))
    cbspec = lambda off: pl.BlockSpec((1, tn), lambda j, s, i: (0, off + j))
    tspec = pl.BlockSpec((1, halo, tn), lambda j, s, i: (s, 0, j))
    tshape = jax.ShapeDtypeStruct((n_seq, halo, d_ff), F32)
    cb2 = conv_b.reshape(1, 2 * d_ff)
    return pl.pallas_call(
        functools.partial(_ffn_up_kernel, tm=tm),
        grid=(nj, n_seq, tiles),
        in_specs=[pl.BlockSpec((tm, d), lambda j, s, i: (s * tiles + i, 0)), wspec(0), wspec(nj),
                  cwspec(0), cwspec(nj), cbspec(0), cbspec(nj)],
        out_specs=[pl.BlockSpec((tm, tn), lambda j, s, i: (s * tiles + i, j)), tspec, tspec, wspec(0), wspec(0)],
        out_shape=[jax.ShapeDtypeStruct((m, d_ff), BF16), tshape, tshape]
                  + [jax.ShapeDtypeStruct((d, d_ff), BF16)] * 2,
        scratch_shapes=[pltpu.VMEM((d, tn), BF16)] * 2 + [pltpu.VMEM((2, halo, tn), F32)],
        compiler_params=_params(("arbitrary", "arbitrary", "arbitrary")),
        name="ffn_up",
    )(x, w_up, w_up, conv_w, conv_w, cb2, cb2)


def _pad_lanes(vec, offset=0):
    out = jnp.zeros((1, LANES), F32)
    return out.at[0, offset:offset + vec.shape[0]].set(vec.astype(F32))


def _mixer_half(x2, lw, alpha, n_seq, rows_per_seq, shift, state, w_in_b):
    (w_in, conv_qkv_w, a_log, dt_bias, norm_g, gm_ln_g, gm_ln_b, gm_ws, gm_bs, gm_out_g, w_out,
     ln1_g, ln1_b) = lw
    m, d_model = x2.shape
    n_groups = gm_ws.shape[0]
    d_a = n_groups * GROUP
    d_qkv = conv_qkv_w.shape[1]
    d_b = d_qkv // 3
    n_heads = d_b // GROUP
    main_cols = 2 * d_a + d_qkv + d_b
    w_ab = jnp.pad(w_in[main_cols:].astype(BF16), ((0, LANES - 2 * n_heads), (0, 0)))
    if w_in_b is None:
        h, ab, w_in_b = _matmul(x2, w_in, w_ab, main_cols, min(m, PROJ_TM), PROJ_TN, emit_w=True)
    else:
        h, ab = _matmul(x2, w_in_b, w_ab, main_cols, min(m, PROJ_TM), PROJ_TN)
    alog, dtb, ng = _pad_lanes(a_log), _pad_lanes(dt_bias), norm_g.reshape(1, GROUP).astype(F32)

    if state is None:
        ya = _chunk_mlp_prompt(h, gm_ln_g, gm_ln_b, gm_ws, gm_bs, gm_out_g)
        v_rows = None
        q, k, v = _qkv_prep(h, conv_qkv_w, None, n_seq, rows_per_seq, QKV_TM, shift, 2 * d_a)
        yb, s_new = _delta_prompt(q, k, v, h, 2 * d_a + d_qkv, ab, alog, dtb, ng, n_seq, rows_per_seq, DELTA_SEQS)
    else:
        s_dn, buf_qkv = state
        n_pos = rows_per_seq // shift
        h3 = h.reshape(n_pos, shift, main_cols)
        ya3, v_rows = _chunk_mlp_sample(h3, gm_ln_g, gm_ln_b, gm_ws, gm_bs, gm_out_g)
        ya = ya3.reshape(m, d_a)
        q, k, v = _qkv_prep(h, conv_qkv_w, buf_qkv, n_seq, rows_per_seq, rows_per_seq, shift, 2 * d_a)
        r3 = lambda t: t.reshape(n_pos, shift, t.shape[-1])
        yb3, s_new = _delta_sample(r3(q), r3(k), r3(v), h3, 2 * d_a + d_qkv, r3(ab), alog, dtb, ng, s_dn,
                                   DELTA_SAMPLE_ROWS)
        yb = yb3.reshape(m, d_b)

    x1, x1b = _out_proj_ln(ya, yb, w_out, x2, ln1_g, ln1_b, alpha, min(m, OUT_TM))
    return x1, x1b, s_new, h, v_rows, w_in_b


def kernel(x_prompt, x_sample, state_dn, state_conv_qkv, state_conv_ffn, w_in, conv_qkv_w, dn_a_log, dn_dt_bias, dn_norm_g, gm_ln_g, gm_ln_b, gm_ws, gm_bs, gm_out_g, w_out, ln1_g, ln1_b, w_up, conv_ffn_w, conv_ffn_b, w_down, ln2_g, ln2_b):
    depth = w_in.shape[0]
    bp, lp, d_model = x_prompt.shape
    bs, ls, _ = x_sample.shape
    d_qkv = conv_qkv_w.shape[-1]
    d_a = gm_ws.shape[1] * GROUP
    alpha = (2.0 * depth) ** 0.25

    hp = x_prompt.reshape(bp * lp, d_model)
    hs = jnp.transpose(x_sample, (1, 0, 2)).reshape(ls * bs, d_model)
    outs = [[] for _ in range(7)]
    w_out = w_out.astype(BF16)
    w_in = jnp.swapaxes(w_in, 1, 2)
    for l in range(depth):
        lw = tuple(t[l] for t in (w_in, conv_qkv_w, dn_a_log, dn_dt_bias, dn_norm_g, gm_ln_g, gm_ln_b, gm_ws,
                                  gm_bs, gm_out_g, w_out, ln1_g, ln1_b))
        ffn_w = (conv_ffn_w[l], conv_ffn_b[l])
        ln2 = (ln2_g[l], ln2_b[l], alpha)
        n_pos = ls if l == depth - 1 else 0
        bq = jnp.transpose(state_conv_qkv[l], (1, 0, 2)).reshape((CONV_QKV - 1) * bs, d_qkv)
        bf = jnp.transpose(state_conv_ffn[l], (1, 0, 2)).reshape((CONV_FFN - 1) * bs, -1)
        x1_s, x1b_s, s_s, h_s, v_s, w_in_b = _mixer_half(hs, lw, alpha, 1, ls * bs, bs, (state_dn[l], bq), None)
        x1_p, x1b_p, s_p, h_p, _, _ = _mixer_half(hp, lw, alpha, bp, lp, 1, None, w_in_b)
        act_p, tg_p, tv_p, w_gate_b, w_val_b = _ffn_up(x1b_p, w_up[l], *ffn_w, bp, lp, FFN_TM, FFN_TN)
        act_s, cf_s = _ffn_up_carried(x1b_s, w_gate_b, w_val_b, *ffn_w, bf, FFN_TN, bs)
        hs, w_down_b = _down_ln(act_s, w_down[l], x1_s, *ln2, min(ls * bs, DOWN_EMIT_TM), DOWN_EMIT_TK,
                                emit_w=True, n_pos=n_pos)
        hp = _down_ln(act_p, w_down_b, x1_p, *ln2, min(bp * lp, DOWN_TM), DOWN_TK)
        h_s3 = h_s.reshape(ls, bs, -1)
        cq_s = jnp.transpose(h_s3[ls - (CONV_QKV - 1):, :, 2 * d_a:2 * d_a + d_qkv], (1, 0, 2))
        h_p3 = h_p.reshape(bp, lp, -1)
        cq_p = h_p3[:, lp - (CONV_QKV - 1):, 2 * d_a:2 * d_a + d_qkv]
        n_tail = tg_p.shape[1]
        cf_p = jnp.concatenate([tg_p, tv_p], axis=-1)[:, n_tail - (CONV_FFN - 1):]
        for acc, val in zip(outs, (s_p, cq_p, cf_p, s_s, cq_s, cf_s, v_s)):
            acc.append(val)
    return (hp.reshape(bp, lp, d_model), hs) + tuple(jnp.stack(o) for o in outs)
```

```python
import functools

import jax
import jax.numpy as jnp
from jax import lax
from jax.experimental import pallas as pl
from jax.experimental.pallas import tpu as pltpu

F32 = jnp.float32
BF16 = jnp.bfloat16

LANES = 128
SUBLANES = 8
VMEM_LIMIT = 56 * 1024 * 1024
BIG_VMEM_LIMIT = 62 * 1024 * 1024

GROUP = 128
MLP_CHUNK = 128
DN_CHUNK = 64
CONV_QKV = 4
CONV_FFN = 3
CMLP_CHUNKS = 8
EPS = 1e-5

PROJ_TM, PROJ_TN = 1024, 2048
PROJ_EMIT_TN = 1536
QKV_TM = 512
DELTA_SEQS = 4
DELTA_SAMPLE_ROWS = 16
OUT_TM = 512
FFN_TM, FFN_TN = 1024, 512
DOWN_TM, DOWN_TK = 1024, 1408
DOWN_EMIT_TM, DOWN_EMIT_TK = 512, 1408

_NT = (((1,), (1,)), ((), ()))
_TN = (((0,), (0,)), ((), ()))


def _params(sem, vmem_limit=VMEM_LIMIT):
    return pltpu.CompilerParams(dimension_semantics=sem, vmem_limit_bytes=vmem_limit)


def _bdot(a, b):
    return jnp.dot(a.astype(BF16), b.astype(BF16), preferred_element_type=F32)


def _gelu(x):
    return 0.5 * x * (1.0 + lax.erf(x * (0.5 ** 0.5)))


def _silu(x):
    return x * jax.nn.sigmoid(x)


def _layer_norm(x, g, b):
    mu = jnp.mean(x, axis=-1, keepdims=True)
    xc = x - mu
    var = jnp.mean(xc * xc, axis=-1, keepdims=True)
    return xc * lax.rsqrt(var + EPS) * g + b


def _rms_norm(x, g):
    ms = jnp.mean(x * x, axis=-1, keepdims=True)
    return x * lax.rsqrt(ms + EPS) * g


def _mm_kernel(x_ref, w_ref, we_ref, o_ref, e_ref, *rest):
    xb_ref = rest[-1]

    @pl.when(pl.program_id(1) == 0)
    def _():
        xb = x_ref[...].astype(BF16)
        xb_ref[...] = xb
        e_ref[...] = lax.dot_general(xb, we_ref[...], _NT, preferred_element_type=F32)

    wb = w_ref[...].astype(BF16)
    if len(rest) == 2:
        rest[0][...] = wb
    o_ref[...] = lax.dot_general(xb_ref[...], wb, _NT, preferred_element_type=F32)


def _matmul(x, wt, wt_extra, n_cols, tm, tn, emit_w=False):
    m, k = x.shape
    ne = wt_extra.shape[0]
    out_specs = [pl.BlockSpec((tm, tn), lambda i, j: (i, j)), pl.BlockSpec((tm, ne), lambda i, j: (i, 0))]
    out_shape = [jax.ShapeDtypeStruct((m, n_cols), F32), jax.ShapeDtypeStruct((m, ne), F32)]
    if emit_w:
        out_specs.append(pl.BlockSpec((tn, k), lambda i, j: (j, 0)))
        out_shape.append(jax.ShapeDtypeStruct((n_cols, k), BF16))
    return pl.pallas_call(
        _mm_kernel,
        grid=(m // tm, n_cols // tn),
        in_specs=[pl.BlockSpec((tm, k), lambda i, j: (i, 0)),
                  pl.BlockSpec((tn, k), lambda i, j: (j, 0)),
                  pl.BlockSpec((ne, k), lambda i, j: (0, 0))],
        out_specs=out_specs,
        out_shape=out_shape,
        scratch_shapes=[pltpu.VMEM((tm, k), BF16)],
        compiler_params=_params(("arbitrary", "arbitrary"), BIG_VMEM_LIMIT),
        name="proj",
    )(x, wt, wt_extra)


def _cmlp_prompt_kernel(u_ref, v_ref, lng_ref, lnb_ref, ws_ref, bst_ref, og_ref, ya_ref, *, n_groups, n_chunks):
    row = lax.broadcasted_iota(jnp.int32, (MLP_CHUNK, MLP_CHUNK), 0)
    col = lax.broadcasted_iota(jnp.int32, (MLP_CHUNK, MLP_CHUNK), 1)
    cg = [(c, g) for c in range(n_chunks) for g in range(n_groups)]
    rows = lambda c: slice(c * MLP_CHUNK, (c + 1) * MLP_CHUNK)
    cols = lambda g: slice(g * GROUP, (g + 1) * GROUP)
    w = [jnp.where(row >= col, ws_ref[g], 0.0).astype(BF16) for g in range(n_groups)]
    n = range(len(cg))
    gv = [_gelu(v_ref[rows(c), cols(g)]) for c, g in cg]
    mu = [jnp.mean(gv[i], axis=-1, keepdims=True) for i in n]
    xc = [gv[i] - mu[i] for i in n]
    var = [jnp.mean(xc[i] * xc[i], axis=-1, keepdims=True) for i in n]
    vh = [xc[i] * lax.rsqrt(var[i] + EPS) * lng_ref[:, cols(g)] + lnb_ref[:, cols(g)] for i, (c, g) in enumerate(cg)]
    mixed = [jnp.dot(w[g], vh[i].astype(BF16), preferred_element_type=F32) + bst_ref[:, g:g + 1]
             for i, (c, g) in enumerate(cg)]
    y = [_gelu(u_ref[rows(c), cols(g)]) * mixed[i] for i, (c, g) in enumerate(cg)]
    ms = [jnp.mean(y[i] * y[i], axis=-1, keepdims=True) for i in n]
    for i, (c, g) in enumerate(cg):
        ya_ref[rows(c), cols(g)] = (y[i] * lax.rsqrt(ms[i] + EPS) * og_ref[:, cols(g)]).astype(ya_ref.dtype)


def _chunk_mlp_prompt(h, ln_g, ln_b, ws, bs, out_g):
    m = h.shape[0]
    n_groups, d_a = ws.shape[0], ws.shape[0] * GROUP
    vec = pl.BlockSpec((1, d_a), lambda i: (0, 0))
    tm = CMLP_CHUNKS * MLP_CHUNK
    return pl.pallas_call(
        functools.partial(_cmlp_prompt_kernel, n_groups=n_groups, n_chunks=CMLP_CHUNKS),
        grid=(m // tm,),
        in_specs=[pl.BlockSpec((tm, d_a), lambda i: (i, 0)),
                  pl.BlockSpec((tm, d_a), lambda i: (i, 1)),
                  vec, vec,
                  pl.BlockSpec((n_groups, MLP_CHUNK, MLP_CHUNK), lambda i: (0, 0, 0)),
                  pl.BlockSpec((MLP_CHUNK, n_groups), lambda i: (0, 0)),
                  vec],
        out_specs=pl.BlockSpec((tm, d_a), lambda i: (i, 0)),
        out_shape=jax.ShapeDtypeStruct((m, d_a), BF16),
        compiler_params=_params(("arbitrary",)),
        name="chunk_mlp_prompt",
    )(h, h, ln_g.reshape(1, d_a), ln_b.reshape(1, d_a), ws, bs.T, out_g.reshape(1, d_a))


def _cmlp_sample_kernel(u_ref, v_ref, lng_ref, lnb_ref, wv_ref, bv_ref, og_ref, ya_ref, vr_ref, *, n_pos):
    vh = []
    for t in range(n_pos):
        vt = _layer_norm(_gelu(v_ref[t]), lng_ref[0], lnb_ref[0])
        vr_ref[:, t, :] = vt
        vh.append(vt)
    for t in range(n_pos):
        mixed = wv_ref[t, 0:1, :] * vh[0]
        for s in range(1, t + 1):
            mixed = mixed + wv_ref[t, s:s + 1, :] * vh[s]
        mixed = mixed + bv_ref[0, t:t + 1, :]
        y = _gelu(u_ref[t]) * mixed
        ya_ref[t] = _rms_norm(y, og_ref[0]).astype(ya_ref.dtype)


def _chunk_mlp_sample(h3, ln_g, ln_b, ws, bs, out_g):
    n_pos, nb, _ = h3.shape
    n_groups = ws.shape[0]
    d_a = n_groups * GROUP
    wv = jnp.repeat(jnp.transpose(ws[:, :n_pos, :n_pos], (1, 2, 0)), GROUP, axis=-1)
    bv = jnp.repeat(bs[:, :n_pos].T, GROUP, axis=-1).reshape(1, n_pos, d_a)
    vec = pl.BlockSpec((1, 1, GROUP), lambda g: (0, 0, g))
    act = pl.BlockSpec((n_pos, nb, GROUP), lambda g: (0, 0, g))
    return pl.pallas_call(
        functools.partial(_cmlp_sample_kernel, n_pos=n_pos),
        grid=(n_groups,),
        in_specs=[act,
                  pl.BlockSpec((n_pos, nb, GROUP), lambda g: (0, 0, n_groups + g)),
                  vec, vec,
                  pl.BlockSpec((n_pos, n_pos, GROUP), lambda g: (0, 0, g)),
                  pl.BlockSpec((1, n_pos, GROUP), lambda g: (0, 0, g)),
                  vec],
        out_specs=[act, pl.BlockSpec((nb, n_pos, GROUP), lambda g: (0, 0, g))],
        out_shape=[jax.ShapeDtypeStruct((n_pos, nb, d_a), BF16),
                   jax.ShapeDtypeStruct((nb, n_pos, d_a), F32)],
        compiler_params=_params(("arbitrary",)),
        name="chunk_mlp_sample",
    )(h3, h3, ln_g.reshape(1, 1, d_a), ln_b.reshape(1, 1, d_a), wv, bv, out_g.reshape(1, 1, d_a))


def _conv_taps(xbuf_ref, w_ref, tm, shift, halo, width):
    acc = None
    for j in range(width):
        off = halo - (width - 1 - j) * shift
        term = w_ref[j:j + 1, :] * xbuf_ref[off:off + tm, :]
        acc = term if acc is None else acc + term
    return acc


def _qkv_prep_kernel(*refs, tm, shift, halo, n_heads, has_buf):
    if has_buf:
        (xq_ref, xk_ref, xv_ref, bq_ref, bk_ref, bv_ref, wq_ref, wk_ref, wv_ref,
         q_ref, k_ref, v_ref, xbuf_ref, carry_ref) = refs
        bufs = (bq_ref, bk_ref, bv_ref)
    else:
        (xq_ref, xk_ref, xv_ref, wq_ref, wk_ref, wv_ref,
         q_ref, k_ref, v_ref, xbuf_ref, carry_ref) = refs
        bufs = (None, None, None)
    first = pl.program_id(1) == 0
    parts = ((xq_ref, wq_ref, q_ref, GROUP ** -0.5), (xk_ref, wk_ref, k_ref, 1.0), (xv_ref, wv_ref, v_ref, None))
    for p, (x_ref, w_ref, o_ref, scale) in enumerate(parts):
        @pl.when(first)
        def _(p=p):
            if has_buf:
                xbuf_ref[0:halo, :] = bufs[p][...]
            else:
                xbuf_ref[0:halo, :] = jnp.zeros((halo, xbuf_ref.shape[1]), F32)

        @pl.when(jnp.logical_not(first))
        def _(p=p):
            xbuf_ref[0:halo, :] = carry_ref[p]

        xbuf_ref[halo:halo + tm, :] = x_ref[...]
        carry_ref[p] = xbuf_ref[tm:tm + halo, :]
        y = _silu(_conv_taps(xbuf_ref, w_ref, tm, shift, halo, CONV_QKV))
        if scale is None:
            o_ref[...] = y
        else:
            for h in range(n_heads):
                sl = slice(h * GROUP, (h + 1) * GROUP)
                yh = y[:, sl]
                inv = lax.rsqrt(jnp.sum(yh * yh, axis=-1, keepdims=True) + 1e-6)
                o_ref[:, sl] = yh * (inv * scale if scale != 1.0 else inv)


def _qkv_prep(h, conv_w, bufs, n_seq, rows_per_seq, tm, shift, col0):
    d_b = conv_w.shape[1] // 3
    n_heads = d_b // GROUP
    has_buf = bufs is not None
    halo = (CONV_QKV - 1) * shift if has_buf else SUBLANES
    tiles = rows_per_seq // tm
    cb = col0 // d_b
    x_specs = [pl.BlockSpec((tm, d_b), lambda s, i, c=c: (s * tiles + i, cb + c)) for c in range(3)]
    w_specs = [pl.BlockSpec((CONV_QKV, d_b), lambda s, i, c=c: (0, c)) for c in range(3)]
    o_spec = pl.BlockSpec((tm, d_b), lambda s, i: (s * tiles + i, 0))
    args, specs = [h, h, h], list(x_specs)
    if has_buf:
        specs += [pl.BlockSpec((halo, d_b), lambda s, i, c=c: (s, c)) for c in range(3)]
        args += [bufs, bufs, bufs]
    specs += w_specs
    args += [conv_w, conv_w, conv_w]
    m = n_seq * rows_per_seq
    return pl.pallas_call(
        functools.partial(_qkv_prep_kernel, tm=tm, shift=shift, halo=halo, n_heads=n_heads, has_buf=has_buf),
        grid=(n_seq, tiles),
        in_specs=specs,
        out_specs=[o_spec, o_spec, o_spec],
        out_shape=[jax.ShapeDtypeStruct((m, d_b), F32)] * 3,
        scratch_shapes=[pltpu.VMEM((halo + tm, d_b), F32), pltpu.VMEM((3, halo, d_b), F32)],
        compiler_params=_params(("arbitrary", "arbitrary")),
        name="qkv_prep",
    )(*args)


def _gates(ab, alog, dtb):
    g = -jnp.exp(alog) * jax.nn.softplus(ab + dtb)
    beta = jax.nn.sigmoid(ab)
    return g, beta


def _delta_prompt_kernel(q_ref, k_ref, v_ref, z_ref, ab_ref, alog_ref, dtb_ref, ng_ref,
                         yb_ref, sout_ref, s_ref, *, n_heads, n_chunks, ns):
    c = DN_CHUNK
    ci = pl.program_id(1)

    @pl.when(ci == 0)
    def _():
        s_ref[...] = jnp.zeros(s_ref.shape, F32)

    row = lax.broadcasted_iota(jnp.int32, (c, c), 0)
    col = lax.broadcasted_iota(jnp.int32, (c, c), 1)
    causal = row >= col
    strict = row > col
    eye = (row == col).astype(F32)
    tril = causal.astype(F32)
    n_double = (c - 1).bit_length() - 1
    ch = [(s, h) for s in range(ns) for h in range(n_heads)]
    sl = lambda h: slice(h * GROUP, (h + 1) * GROUP)
    gates = [_gates(ab_ref[s], alog_ref[...], dtb_ref[...]) for s in range(ns)]
    gc_s = [jnp.dot(tril, gates[s][0], precision=lax.Precision.HIGHEST, preferred_element_type=F32)
            for s in range(ns)]
    gct_s = [g.T for g in gc_s]
    q = [q_ref[s, :, sl(h)] for s, h in ch]
    k = [k_ref[s, :, sl(h)] for s, h in ch]
    gcol = [gc_s[s][:, h:h + 1] for s, h in ch]
    beta = [gates[s][1][:, n_heads + h:n_heads + h + 1] for s, h in ch]
    n = range(len(ch))
    decay = [jnp.exp(jnp.where(causal, gcol[i] - gct_s[s][h:h + 1, :], -jnp.inf)) for i, (s, h) in enumerate(ch)]
    kb = [k[i] * beta[i] for i in n]
    kbf = [k[i].astype(BF16) for i in n]
    kq = [lax.dot_general(jnp.concatenate([kb[i], q[i]], axis=0).astype(BF16), kbf[i], _NT,
                          preferred_element_type=F32) for i in n]
    qk = [kq[i][c:] * decay[i] for i in n]
    y = [jnp.where(strict, -(kq[i][:c] * decay[i]), 0.0) for i in n]
    x = [eye + y[i] for i in n]
    y = [_bdot(y[i], y[i]) for i in n]
    for _ in range(n_double - 1):
        t = [_bdot(jnp.concatenate([x[i], y[i]], axis=0), y[i]) for i in n]
        x = [x[i] + t[i][:c] for i in n]
        y = [t[i][c:] for i in n]
    x = [x[i] + _bdot(x[i], y[i]) for i in n]
    egc = [jnp.exp(gcol[i]) for i in n]
    sol = [_bdot(x[i], jnp.concatenate([v_ref[s, :, sl(h)] * beta[i], kb[i] * egc[i]], axis=1))
           for i, (s, h) in enumerate(ch)]
    s_old = [s_ref[s, h] for s, h in ch]
    r = [_bdot(jnp.concatenate([sol[i][:, GROUP:], q[i] * egc[i]], axis=0), s_old[i]) for i in n]
    v_new = [sol[i][:, :GROUP] - r[i][:c] for i in n]
    o = [r[i][c:] + _bdot(qk[i], v_new[i]) for i in n]
    gl = [gc_s[s][c - 1:c, h:h + 1] for s, h in ch]
    upd = [lax.dot_general((k[i] * jnp.exp(gl[i] - gcol[i])).astype(BF16), v_new[i].astype(BF16), _TN,
                           preferred_element_type=F32) for i in n]
    ms = [jnp.mean(o[i] * o[i], axis=-1, keepdims=True) for i in n]
    for i, (s, h) in enumerate(ch):
        s_ref[s, h] = s_old[i] * jnp.exp(gl[i]) + upd[i]
        on = o[i] * lax.rsqrt(ms[i] + EPS) * ng_ref[...]
        yb_ref[s, :, sl(h)] = (on * _silu(z_ref[s, :, sl(h)])).astype(yb_ref.dtype)

    @pl.when(ci == n_chunks - 1)
    def _():
        sout_ref[...] = s_ref[...]


def _delta_prompt(q, k, v, h, z_col0, ab, alog, dtb, norm_g, n_seq, rows_per_seq, ns):
    d_b = q.shape[1]
    n_heads = d_b // GROUP
    n_chunks = rows_per_seq // DN_CHUNK
    r3 = lambda t: t.reshape(n_seq, rows_per_seq, t.shape[-1])
    act = pl.BlockSpec((ns, DN_CHUNK, d_b), lambda s, i: (s, i, 0))
    vec = pl.BlockSpec((1, LANES), lambda s, i: (0, 0))
    zb = z_col0 // d_b
    yb, s_new = pl.pallas_call(
        functools.partial(_delta_prompt_kernel, n_heads=n_heads, n_chunks=n_chunks, ns=ns),
        grid=(n_seq // ns, n_chunks),
        in_specs=[act, act, act,
                  pl.BlockSpec((ns, DN_CHUNK, d_b), lambda s, i: (s, i, zb)),
                  pl.BlockSpec((ns, DN_CHUNK, LANES), lambda s, i: (s, i, 0)),
                  vec, vec, vec],
        out_specs=[act, pl.BlockSpec((ns, n_heads, GROUP, GROUP), lambda s, i: (s, 0, 0, 0))],
        out_shape=[jax.ShapeDtypeStruct((n_seq, rows_per_seq, d_b), BF16),
                   jax.ShapeDtypeStruct((n_seq, n_heads, GROUP, GROUP), F32)],
        scratch_shapes=[pltpu.VMEM((ns, n_heads, GROUP, GROUP), F32)],
        compiler_params=_params(("arbitrary", "arbitrary")),
        name="delta_prompt",
    )(r3(q), r3(k), r3(v), r3(h), r3(ab), alog, dtb, norm_g)
    return yb.reshape(n_seq * rows_per_seq, d_b), s_new


def _delta_sample_kernel(q_ref, k_ref, v_ref, z_ref, ab_ref, alog_ref, dtb_ref, ng_ref, s0_ref,
                         yb_ref, s1_ref, lhs_ref, res_ref, kt_ref, vn_ref, gl_ref, *, n_heads, n_pos, bb):
    nt = n_pos
    hs = range(n_heads)
    sls = [slice(h * GROUP, (h + 1) * GROUP) for h in hs]
    gates = [_gates(ab_ref[t], alog_ref[...], dtb_ref[...]) for t in range(nt)]
    zeros_pad = jnp.zeros((bb, GROUP), F32)
    ts = range(nt)
    q = [[q_ref[t, :, sls[h]] for t in ts] for h in hs]
    k = [[k_ref[t, :, sls[h]] for t in ts] for h in hs]
    beta = [[gates[t][1][:, n_heads + h:n_heads + h + 1] for t in ts] for h in hs]
    gc = []
    for h in hs:
        g = [gates[t][0][:, h:h + 1] for t in ts]
        for t in range(1, nt):
            g[t] = g[t - 1] + g[t]
        gc.append(g)
    kb = [[k[h][t] * beta[h][t] for t in ts] for h in hs]
    kk = [[[jnp.sum(kb[h][t] * k[h][s], axis=-1, keepdims=True) for s in range(t)] for t in ts] for h in hs]
    qk = [[[jnp.sum(q[h][t] * k[h][s], axis=-1, keepdims=True) for s in range(t + 1)] for t in ts] for h in hs]
    dec = [[[jnp.exp(gc[h][t] - gc[h][s]) for s in range(t + 1)] for t in ts] for h in hs]
    a = [[[kk[h][t][s] * dec[h][t][s] for s in range(t)] for t in ts] for h in hs]
    qk = [[[qk[h][t][s] * dec[h][t][s] for s in range(t + 1)] for t in ts] for h in hs]
    egc = [[jnp.exp(gc[h][t]) for t in ts] for h in hs]
    u_blk = [[v_ref[t, :, sls[h]] * beta[h][t] for t in ts] for h in hs]
    w_blk = [[kb[h][t] * egc[h][t] for t in ts] for h in hs]
    for t in ts:
        for h in hs:
            for s in range(t):
                u_blk[h][t] = u_blk[h][t] - a[h][t][s] * u_blk[h][s]
                w_blk[h][t] = w_blk[h][t] - a[h][t][s] * w_blk[h][s]
    for h in hs:
        for t in ts:
            lhs_ref[h, :, t, :] = w_blk[h][t]
            lhs_ref[h, :, nt + t, :] = q[h][t] * egc[h][t]
        gl_ref[h] = jnp.broadcast_to(egc[h][nt - 1], (bb, GROUP))

    for h in hs:
        for b in range(bb):
            res_ref[h, b] = _bdot(lhs_ref[h, b], s0_ref[b, h])

    v_new = [[u_blk[h][t] - res_ref[h, :, t, :] for t in ts] for h in hs]
    o = [[res_ref[h, :, nt + t, :] for t in ts] for h in hs]
    for h in hs:
        for t in ts:
            for s in range(t + 1):
                o[h][t] = o[h][t] + qk[h][t][s] * v_new[h][s]
    ms = [[jnp.mean(o[h][t] * o[h][t], axis=-1, keepdims=True) for t in ts] for h in hs]
    for h in hs:
        gl = gc[h][nt - 1]
        for t in ts:
            on = o[h][t] * lax.rsqrt(ms[h][t] + EPS) * ng_ref[...]
            yb_ref[t, :, sls[h]] = (on * _silu(z_ref[t, :, sls[h]])).astype(yb_ref.dtype)
            kt_ref[h, :, t, :] = k[h][t] * jnp.exp(gl - gc[h][t])
            vn_ref[h, :, t, :] = v_new[h][t]
            kt_ref[h, :, nt + t, :] = zeros_pad
            vn_ref[h, :, nt + t, :] = zeros_pad

    for h in hs:
        for b in range(bb):
            upd = lax.dot_general(kt_ref[h, b].astype(BF16), vn_ref[h, b].astype(BF16), _TN,
                                  preferred_element_type=F32)
            s1_ref[b, h] = s0_ref[b, h] * gl_ref[h, b:b + 1, :] + upd


def _delta_sample(q, k, v, h3, z_col0, ab, alog, dtb, norm_g, s0, bb):
    n_pos, nb, d_b = q.shape
    n_heads = d_b // GROUP
    assert 2 * n_pos == SUBLANES, "one state read packs [w | q] rows of a sequence into one 8-row tile"
    zb = z_col0 // d_b
    act = pl.BlockSpec((n_pos, bb, d_b), lambda i: (0, i, 0))
    vec = pl.BlockSpec((1, LANES), lambda i: (0, 0))
    st = pl.BlockSpec((bb, n_heads, GROUP, GROUP), lambda i: (i, 0, 0, 0))
    tile = pltpu.VMEM((n_heads, bb, SUBLANES, GROUP), F32)
    return pl.pallas_call(
        functools.partial(_delta_sample_kernel, n_heads=n_heads, n_pos=n_pos, bb=bb),
        grid=(nb // bb,),
        in_specs=[act, act, act,
                  pl.BlockSpec((n_pos, bb, d_b), lambda i: (0, i, zb)),
                  pl.BlockSpec((n_pos, bb, LANES), lambda i: (0, i, 0)),
                  vec, vec, vec, st],
        out_specs=[act, st],
        out_shape=[jax.ShapeDtypeStruct((n_pos, nb, d_b), BF16),
                   jax.ShapeDtypeStruct(s0.shape, F32)],
        scratch_shapes=[tile, tile, tile, tile, pltpu.VMEM((n_heads, bb, GROUP), F32)],
        compiler_params=_params(("arbitrary",)),
        name="delta_sample",
    )(q, k, v, h3, ab, alog, dtb, norm_g, s0)


def _out_proj_kernel(ya_ref, yb_ref, w_ref, r_ref, g_ref, b_ref, o_ref, ob_ref, z_ref, *, n_tiles, d_a, alpha):
    i = pl.program_id(0)

    @pl.when(i == 0)
    def _():
        z_ref[...] = jnp.zeros(z_ref.shape, F32)

    def emit_ln():
        y = _layer_norm(z_ref[...], g_ref[...], b_ref[...])
        o_ref[...] = y
        ob_ref[...] = y.astype(ob_ref.dtype)

    @pl.when(i < n_tiles)
    def _():
        emit_ln()
        acc = (jnp.dot(ya_ref[...], w_ref[0:d_a, :], preferred_element_type=F32)
               + jnp.dot(yb_ref[...], w_ref[d_a:, :], preferred_element_type=F32))
        z_ref[...] = alpha * r_ref[...] + acc

    @pl.when(i == n_tiles)
    def _():
        emit_ln()


def _out_proj_ln(ya, yb, w, resid, g, b, alpha, tm):
    m, d_a = ya.shape
    k, d = w.shape
    n_tiles = m // tm
    cur = lambda i: (jnp.minimum(i, n_tiles - 1), 0)
    prev = lambda i: (jnp.maximum(i - 1, 0), 0)
    vec = pl.BlockSpec((1, d), lambda i: (0, 0))
    return pl.pallas_call(
        functools.partial(_out_proj_kernel, n_tiles=n_tiles, d_a=d_a, alpha=alpha),
        grid=(n_tiles + 1,),
        in_specs=[pl.BlockSpec((tm, d_a), cur),
                  pl.BlockSpec((tm, k - d_a), cur),
                  pl.BlockSpec((k, d), lambda i: (0, 0), pipeline_mode=pl.Buffered(1)),
                  pl.BlockSpec((tm, d), cur), vec, vec],
        out_specs=[pl.BlockSpec((tm, d), prev), pl.BlockSpec((tm, d), prev)],
        out_shape=[jax.ShapeDtypeStruct((m, d), F32), jax.ShapeDtypeStruct((m, d), BF16)],
        scratch_shapes=[pltpu.VMEM((tm, d), F32)],
        compiler_params=_params(("arbitrary",)),
        name="out_proj_ln1",
    )(ya, yb, w, resid, g.reshape(1, d), b.reshape(1, d))


def _down_kernel(x_ref, w_ref, r_ref, g_ref, b_ref, o_ref, *rest, n_tiles, nk, nc, alpha, n_pos):
    z_ref = rest[-1]
    cw = z_ref.shape[-1]
    slab = lambda c: slice(c * cw, (c + 1) * cw)
    i, kk = pl.program_id(0), pl.program_id(1)

    def emit_ln():
        z = [z_ref[c] for c in range(nc)]
        inv_d = 1.0 / (nc * cw)
        mu = sum(jnp.sum(zc, axis=-1, keepdims=True) for zc in z) * inv_d
        xc = [zc - mu for zc in z]
        var = sum(jnp.sum(c * c, axis=-1, keepdims=True) for c in xc) * inv_d
        rs = lax.rsqrt(var + EPS)
        for c in range(nc):
            y = xc[c] * rs * g_ref[:, slab(c)] + b_ref[:, slab(c)]
            if n_pos:
                nb = y.shape[0] // n_pos
                for t in range(n_pos):
                    o_ref[:, t, slab(c)] = y[t * nb:(t + 1) * nb]
            else:
                o_ref[:, slab(c)] = y

    @pl.when(jnp.logical_and(i == 0, kk == 0))
    def _():
        z_ref[...] = jnp.zeros(z_ref.shape, F32)

    wb = w_ref[...].astype(BF16)
    if len(rest) == 2:
        rest[0][...] = wb

    @pl.when(kk == 0)
    def _():
        emit_ln()
        part = jnp.dot(x_ref[...], wb, preferred_element_type=F32)
        for c in range(nc):
            if nc == nk:
                z_ref[c] = part[:, slab(c)] + alpha * r_ref[...] if c == 0 else part[:, slab(c)]
            else:
                z_ref[c] = part[:, slab(c)] + alpha * r_ref[:, slab(c)]

    @pl.when(jnp.logical_and(kk > 0, i < n_tiles))
    def _():
        part = jnp.dot(x_ref[...], wb, preferred_element_type=F32)
        for c in range(nc):
            z_ref[c] += part[:, slab(c)]
        if nc == nk:
            z_ref[kk] += alpha * r_ref[...]


def _down_ln(x, w, resid, g, b, alpha, tm, tk, emit_w=False, n_pos=0):
    m, kdim = x.shape
    d = w.shape[1]
    nk = kdim // tk
    n_tiles = m // tm
    nc = nk if (d % nk == 0 and (d // nk) % LANES == 0) else 1
    row = lambda i: jnp.minimum(i, n_tiles - 1)
    kt = lambda i, kk: jnp.where(i == n_tiles, nk - 1, kk)
    vec = pl.BlockSpec((1, d), lambda i, kk: (0, 0))
    wspec = pl.BlockSpec((tk, d), lambda i, kk: (kt(i, kk), 0))
    if nc == nk:
        rspec = pl.BlockSpec((tm, d // nc), lambda i, kk: (row(i), kt(i, kk)))
    else:
        rspec = pl.BlockSpec((tm, d), lambda i, kk: (row(i), 0))
    if n_pos:
        assert n_tiles == 1
        out_specs = [pl.BlockSpec((m // n_pos, n_pos, d), lambda i, kk: (0, 0, 0))]
        out_shape = [jax.ShapeDtypeStruct((m // n_pos, n_pos, d), F32)]
    else:
        out_specs = [pl.BlockSpec((tm, d), lambda i, kk: (jnp.where(kk == 0, jnp.maximum(i - 1, 0), row(i)), 0))]
        out_shape = [jax.ShapeDtypeStruct((m, d), F32)]
    if emit_w:
        out_specs.append(wspec)
        out_shape.append(jax.ShapeDtypeStruct(w.shape, BF16))
    outs = pl.pallas_call(
        functools.partial(_down_kernel, n_tiles=n_tiles, nk=nk, nc=nc, alpha=alpha, n_pos=n_pos),
        grid=(n_tiles + 1, nk),
        in_specs=[pl.BlockSpec((tm, tk), lambda i, kk: (row(i), kt(i, kk))), wspec, rspec, vec, vec],
        out_specs=out_specs,
        out_shape=out_shape,
        scratch_shapes=[pltpu.VMEM((nc, tm, d // nc), F32)],
        compiler_params=_params(("arbitrary", "arbitrary"), BIG_VMEM_LIMIT),
        name="down_ln2",
    )(x, w, resid, g.reshape(1, d), b.reshape(1, d))
    return tuple(outs) if emit_w else outs[0]


def _ffn_up_kernel(x_ref, wg_ref, wv_ref, cwg_ref, cwv_ref, cbg_ref, cbv_ref,
                   act_ref, tg_ref, tv_ref, wgo_ref, wvo_ref, wgb_ref, wvb_ref, carry_ref, *, tm):
    halo = SUBLANES
    first = pl.program_id(2) == 0

    @pl.when(jnp.logical_and(pl.program_id(1) == 0, first))
    def _():
        for w_ref, wb_ref, wo_ref in ((wg_ref, wgb_ref, wgo_ref), (wv_ref, wvb_ref, wvo_ref)):
            wb = w_ref[...].astype(BF16)
            wb_ref[...] = wb
            wo_ref[...] = wb

    @pl.when(first)
    def _():
        carry_ref[...] = jnp.zeros(carry_ref.shape, F32)

    x = x_ref[...]
    conv = []
    parts = ((wgb_ref, cwg_ref, cbg_ref, tg_ref), (wvb_ref, cwv_ref, cbv_ref, tv_ref))
    row = lax.broadcasted_iota(jnp.int32, (halo, act_ref.shape[1]), 0)
    for p, (w_ref, cw_ref, cb_ref, t_ref) in enumerate(parts):
        up = jnp.dot(x, w_ref[...], preferred_element_type=F32)
        hist = carry_ref[p]
        tail = up[tm - halo:tm, :]
        carry_ref[p] = tail
        t_ref[0] = tail
        acc = cw_ref[CONV_FFN - 1:CONV_FFN, :] * up
        for dd in range(1, CONV_FFN):
            xs = pltpu.roll(up, dd, axis=0)
            head = jnp.where(row < dd, pltpu.roll(hist, dd, axis=0), xs[0:halo])
            xs = jnp.concatenate([head, xs[halo:]], axis=0)
            acc = acc + cw_ref[CONV_FFN - 1 - dd:CONV_FFN - dd, :] * xs
        conv.append(acc + cb_ref[...])
    act_ref[...] = (_silu(conv[0]) * conv[1]).astype(act_ref.dtype)


def _ffn_up_carried_kernel(x_ref, wg_ref, wv_ref, buf_ref, cw_ref, cb_ref, act_ref, t_ref, xbuf_ref, gate_ref,
                           *, shift):
    p, j = pl.program_id(0), pl.program_id(1)
    tm, halo = x_ref.shape[0], buf_ref.shape[0]
    xbuf_ref[0:halo, :] = buf_ref[...]

    def conv_of(w_ref):
        xbuf_ref[halo:halo + tm, :] = jnp.dot(x_ref[...], w_ref[...], preferred_element_type=F32)
        tail = xbuf_ref[tm:tm + halo, :]
        for t in range(halo // shift):
            t_ref[:, t, :] = tail[t * shift:(t + 1) * shift]
        return _conv_taps(xbuf_ref, cw_ref, tm, shift, halo, CONV_FFN) + cb_ref[...]

    @pl.when(p == 0)
    def _():
        gate_ref[j] = conv_of(wg_ref)

    @pl.when(p == 1)
    def _():
        act_ref[...] = (_silu(gate_ref[j]) * conv_of(wv_ref)).astype(act_ref.dtype)


def _ffn_up_carried(x, w_gate_b, w_val_b, conv_w, conv_b, bufs, tn, shift):
    m, d = x.shape
    d_ff = w_gate_b.shape[1]
    nj = d_ff // tn
    halo = (CONV_FFN - 1) * shift
    col = lambda p, j: (0, p * nj + j)
    return pl.pallas_call(
        functools.partial(_ffn_up_carried_kernel, shift=shift),
        grid=(2, nj),
        in_specs=[pl.BlockSpec((m, d), lambda p, j: (0, 0)),
                  pl.BlockSpec((d, tn), lambda p, j: (0, jnp.where(p == 0, j, nj - 1))),
                  pl.BlockSpec((d, tn), lambda p, j: (0, jnp.where(p == 1, j, 0))),
                  pl.BlockSpec((halo, tn), col),
                  pl.BlockSpec((CONV_FFN, tn), col),
                  pl.BlockSpec((1, tn), col)],
        out_specs=[pl.BlockSpec((m, tn), lambda p, j: (0, jnp.where(p == 1, j, 0))),
                   pl.BlockSpec((shift, halo // shift, tn), lambda p, j: (0, 0, p * nj + j))],
        out_shape=[jax.ShapeDtypeStruct((m, d_ff), BF16),
                   jax.ShapeDtypeStruct((shift, halo // shift, 2 * d_ff), F32)],
        scratch_shapes=[pltpu.VMEM((halo + m, tn), F32), pltpu.VMEM((nj, m, tn), F32)],
        compiler_params=_params(("arbitrary", "arbitrary")),
        name="ffn_up_carried",
    )(x, w_gate_b, w_val_b, bufs, conv_w, conv_b.reshape(1, 2 * d_ff))


def _ffn_up(x, w_up, conv_w, conv_b, n_seq, rows_per_seq, tm, tn):
    m, d = x.shape
    d_ff = w_up.shape[1] // 2
    nj = d_ff // tn
    halo = SUBLANES
    tiles = rows_per_seq // tm
    wspec = lambda off: pl.BlockSpec((d, tn), lambda j, s, i: (0, off + j))
    cwspec = lambda off: pl.BlockSpec((CONV_FFN, tn), lambda j, s, i: (0, off + j))
    cbspec = lambda off: pl.BlockSpec((1, tn), lambda j, s, i: (0, off + j))
    tspec = pl.BlockSpec((1, halo, tn), lambda j, s, i: (s, 0, j))
    tshape = jax.ShapeDtypeStruct((n_seq, halo, d_ff), F32)
    cb2 = conv_b.reshape(1, 2 * d_ff)
    return pl.pallas_call(
        functools.partial(_ffn_up_kernel, tm=tm),
        grid=(nj, n_seq, tiles),
        in_specs=[pl.BlockSpec((tm, d), lambda j, s, i: (s * tiles + i, 0)), wspec(0), wspec(nj),
                  cwspec(0), cwspec(nj), cbspec(0), cbspec(nj)],
        out_specs=[pl.BlockSpec((tm, tn), lambda j, s, i: (s * tiles + i, j)), tspec, tspec, wspec(0), wspec(0)],
        out_shape=[jax.ShapeDtypeStruct((m, d_ff), BF16), tshape, tshape]
                  + [jax.ShapeDtypeStruct((d, d_ff), BF16)] * 2,
        scratch_shapes=[pltpu.VMEM((d, tn), BF16)] * 2 + [pltpu.VMEM((2, halo, tn), F32)],
        compiler_params=_params(("arbitrary", "arbitrary", "arbitrary")),
        name="ffn_up",
    )(x, w_up, w_up, conv_w, conv_w, cb2, cb2)


def _pad_lanes(vec, offset=0):
    out = jnp.zeros((1, LANES), F32)
    return out.at[0, offset:offset + vec.shape[0]].set(vec.astype(F32))


def _mixer_half(x2, lw, alpha, n_seq, rows_per_seq, shift, state, w_in_b):
    (w_in, conv_qkv_w, a_log, dt_bias, norm_g, gm_ln_g, gm_ln_b, gm_ws, gm_bs, gm_out_g, w_out,
     ln1_g, ln1_b) = lw
    m, d_model = x2.shape
    n_groups = gm_ws.shape[0]
    d_a = n_groups * GROUP
    d_qkv = conv_qkv_w.shape[1]
    d_b = d_qkv // 3
    n_heads = d_b // GROUP
    main_cols = 2 * d_a + d_qkv + d_b
    w_ab = jnp.pad(w_in[main_cols:].astype(BF16), ((0, LANES - 2 * n_heads), (0, 0)))
    if w_in_b is None:
        h, ab, w_in_b = _matmul(x2, w_in, w_ab, main_cols, min(m, PROJ_TM), PROJ_EMIT_TN, emit_w=True)
    else:
        h, ab = _matmul(x2, w_in_b, w_ab, main_cols, min(m, PROJ_TM), PROJ_TN)
    alog, dtb, ng = _pad_lanes(a_log), _pad_lanes(dt_bias), norm_g.reshape(1, GROUP).astype(F32)

    if state is None:
        ya = _chunk_mlp_prompt(h, gm_ln_g, gm_ln_b, gm_ws, gm_bs, gm_out_g)
        v_rows = None
        q, k, v = _qkv_prep(h, conv_qkv_w, None, n_seq, rows_per_seq, QKV_TM, shift, 2 * d_a)
        yb, s_new = _delta_prompt(q, k, v, h, 2 * d_a + d_qkv, ab, alog, dtb, ng, n_seq, rows_per_seq, DELTA_SEQS)
    else:
        s_dn, buf_qkv = state
        n_pos = rows_per_seq // shift
        h3 = h.reshape(n_pos, shift, main_cols)
        ya3, v_rows = _chunk_mlp_sample(h3, gm_ln_g, gm_ln_b, gm_ws, gm_bs, gm_out_g)
        ya = ya3.reshape(m, d_a)
        q, k, v = _qkv_prep(h, conv_qkv_w, buf_qkv, n_seq, rows_per_seq, rows_per_seq, shift, 2 * d_a)
        r3 = lambda t: t.reshape(n_pos, shift, t.shape[-1])
        yb3, s_new = _delta_sample(r3(q), r3(k), r3(v), h3, 2 * d_a + d_qkv, r3(ab), alog, dtb, ng, s_dn,
                                   DELTA_SAMPLE_ROWS)
        yb = yb3.reshape(m, d_b)

    x1, x1b = _out_proj_ln(ya, yb, w_out, x2, ln1_g, ln1_b, alpha, min(m, OUT_TM))
    return x1, x1b, s_new, h, v_rows, w_in_b


def kernel(x_prompt, x_sample, state_dn, state_conv_qkv, state_conv_ffn, w_in, conv_qkv_w, dn_a_log, dn_dt_bias, dn_norm_g, gm_ln_g, gm_ln_b, gm_ws, gm_bs, gm_out_g, w_out, ln1_g, ln1_b, w_up, conv_ffn_w, conv_ffn_b, w_down, ln2_g, ln2_b):
    depth = w_in.shape[0]
    bp, lp, d_model = x_prompt.shape
    bs, ls, _ = x_sample.shape
    d_qkv = conv_qkv_w.shape[-1]
    d_a = gm_ws.shape[1] * GROUP
    alpha = (2.0 * depth) ** 0.25

    hp = x_prompt.reshape(bp * lp, d_model)
    hs = jnp.transpose(x_sample, (1, 0, 2)).reshape(ls * bs, d_model)
    outs = [[] for _ in range(7)]
    w_out = w_out.astype(BF16)
    w_in = jnp.swapaxes(w_in, 1, 2)
    for l in range(depth):
        lw = tuple(t[l] for t in (w_in, conv_qkv_w, dn_a_log, dn_dt_bias, dn_norm_g, gm_ln_g, gm_ln_b, gm_ws,
                                  gm_bs, gm_out_g, w_out, ln1_g, ln1_b))
        ffn_w = (conv_ffn_w[l], conv_ffn_b[l])
        ln2 = (ln2_g[l], ln2_b[l], alpha)
        n_pos = ls if l == depth - 1 else 0
        bq = jnp.transpose(state_conv_qkv[l], (1, 0, 2)).reshape((CONV_QKV - 1) * bs, d_qkv)
        bf = jnp.transpose(state_conv_ffn[l], (1, 0, 2)).reshape((CONV_FFN - 1) * bs, -1)
        x1_s, x1b_s, s_s, h_s, v_s, w_in_b = _mixer_half(hs, lw, alpha, 1, ls * bs, bs, (state_dn[l], bq), None)
        x1_p, x1b_p, s_p, h_p, _, _ = _mixer_half(hp, lw, alpha, bp, lp, 1, None, w_in_b)
        act_p, tg_p, tv_p, w_gate_b, w_val_b = _ffn_up(x1b_p, w_up[l], *ffn_w, bp, lp, FFN_TM, FFN_TN)
        act_s, cf_s = _ffn_up_carried(x1b_s, w_gate_b, w_val_b, *ffn_w, bf, FFN_TN, bs)
        hs, w_down_b = _down_ln(act_s, w_down[l], x1_s, *ln2, min(ls * bs, DOWN_EMIT_TM), DOWN_EMIT_TK,
                                emit_w=True, n_pos=n_pos)
        hp = _down_ln(act_p, w_down_b, x1_p, *ln2, min(bp * lp, DOWN_TM), DOWN_TK)
        h_s3 = h_s.reshape(ls, bs, -1)
        cq_s = jnp.transpose(h_s3[ls - (CONV_QKV - 1):, :, 2 * d_a:2 * d_a + d_qkv], (1, 0, 2))
        h_p3 = h_p.reshape(bp, lp, -1)
        cq_p = h_p3[:, lp - (CONV_QKV - 1):, 2 * d_a:2 * d_a + d_qkv]
        n_tail = tg_p.shape[1]
        cf_p = jnp.concatenate([tg_p, tv_p], axis=-1)[:, n_tail - (CONV_FFN - 1):]
        for acc, val in zip(outs, (s_p, cq_p, cf_p, s_s, cq_s, cf_s, v_s)):
            acc.append(val)
    return (hp.reshape(bp, lp, d_model), hs) + tuple(jnp.stack(o) for o in outs)
```

```python
import functools

import jax
import jax.numpy as jnp
from jax import lax
from jax.experimental import pallas as pl
from jax.experimental.pallas import tpu as pltpu

F32 = jnp.float32
BF16 = jnp.bfloat16

LANES = 128
SUBLANES = 8
VMEM_LIMIT = 56 * 1024 * 1024
BIG_VMEM_LIMIT = 62 * 1024 * 1024

GROUP = 128
MLP_CHUNK = 128
DN_CHUNK = 64
CONV_QKV = 4
CONV_FFN = 3
CMLP_CHUNKS = 8
EPS = 1e-5

PROJ_TM, PROJ_TN = 1024, 2048
PROJ_EMIT_TN = 1536
QKV_TM = 512
DELTA_SEQS = 4
DELTA_SAMPLE_ROWS = 16
OUT_TM = 512
FFN_TM, FFN_TN = 1024, 512
DOWN_TM, DOWN_TK = 1024, 1408
DOWN_EMIT_TM, DOWN_EMIT_TK = 512, 1408

_NT = (((1,), (1,)), ((), ()))
_TN = (((0,), (0,)), ((), ()))


def _params(sem, vmem_limit=VMEM_LIMIT):
    return pltpu.CompilerParams(dimension_semantics=sem, vmem_limit_bytes=vmem_limit)


def _bdot(a, b):
    return jnp.dot(a.astype(BF16), b.astype(BF16), preferred_element_type=F32)


def _gelu(x):
    return 0.5 * x * (1.0 + lax.erf(x * (0.5 ** 0.5)))


def _silu(x):
    return x * jax.nn.sigmoid(x)


def _layer_norm(x, g, b):
    mu = jnp.mean(x, axis=-1, keepdims=True)
    xc = x - mu
    var = jnp.mean(xc * xc, axis=-1, keepdims=True)
    return xc * lax.rsqrt(var + EPS) * g + b


def _rms_norm(x, g):
    ms = jnp.mean(x * x, axis=-1, keepdims=True)
    return x * lax.rsqrt(ms + EPS) * g


def _mm_kernel(x_ref, w_ref, we_ref, o_ref, e_ref, *rest):
    xb_ref = rest[-1]

    @pl.when(pl.program_id(1) == 0)
    def _():
        xb = x_ref[...].astype(BF16)
        xb_ref[...] = xb
        e_ref[...] = lax.dot_general(xb, we_ref[...], _NT, preferred_element_type=F32)

    wb = w_ref[...].astype(BF16)
    if len(rest) == 2:
        rest[0][...] = wb
    o_ref[...] = lax.dot_general(xb_ref[...], wb, _NT, preferred_element_type=F32)


def _matmul(x, wt, wt_extra, n_cols, tm, tn, emit_w=False):
    m, k = x.shape
    ne = wt_extra.shape[0]
    out_specs = [pl.BlockSpec((tm, tn), lambda i, j: (i, j)), pl.BlockSpec((tm, ne), lambda i, j: (i, 0))]
    out_shape = [jax.ShapeDtypeStruct((m, n_cols), F32), jax.ShapeDtypeStruct((m, ne), F32)]
    if emit_w:
        out_specs.append(pl.BlockSpec((tn, k), lambda i, j: (j, 0)))
        out_shape.append(jax.ShapeDtypeStruct((n_cols, k), BF16))
    return pl.pallas_call(
        _mm_kernel,
        grid=(m // tm, n_cols // tn),
        in_specs=[pl.BlockSpec((tm, k), lambda i, j: (i, 0)),
                  pl.BlockSpec((tn, k), lambda i, j: (j, 0)),
                  pl.BlockSpec((ne, k), lambda i, j: (0, 0))],
        out_specs=out_specs,
        out_shape=out_shape,
        scratch_shapes=[pltpu.VMEM((tm, k), BF16)],
        compiler_params=_params(("arbitrary", "arbitrary"), BIG_VMEM_LIMIT),
        name="proj",
    )(x, wt, wt_extra)


def _cmlp_prompt_kernel(u_ref, v_ref, lng_ref, lnb_ref, ws_ref, bst_ref, og_ref, ya_ref, *, n_groups, n_chunks):
    row = lax.broadcasted_iota(jnp.int32, (MLP_CHUNK, MLP_CHUNK), 0)
    col = lax.broadcasted_iota(jnp.int32, (MLP_CHUNK, MLP_CHUNK), 1)
    cg = [(c, g) for c in range(n_chunks) for g in range(n_groups)]
    rows = lambda c: slice(c * MLP_CHUNK, (c + 1) * MLP_CHUNK)
    cols = lambda g: slice(g * GROUP, (g + 1) * GROUP)
    w = [jnp.where(row >= col, ws_ref[g], 0.0).astype(BF16) for g in range(n_groups)]
    n = range(len(cg))
    gv = [_gelu(v_ref[rows(c), cols(g)]) for c, g in cg]
    mu = [jnp.mean(gv[i], axis=-1, keepdims=True) for i in n]
    xc = [gv[i] - mu[i] for i in n]
    var = [jnp.mean(xc[i] * xc[i], axis=-1, keepdims=True) for i in n]
    vh = [xc[i] * lax.rsqrt(var[i] + EPS) * lng_ref[:, cols(g)] + lnb_ref[:, cols(g)] for i, (c, g) in enumerate(cg)]
    mixed = [jnp.dot(w[g], vh[i].astype(BF16), preferred_element_type=F32) + bst_ref[:, g:g + 1]
             for i, (c, g) in enumerate(cg)]
    y = [_gelu(u_ref[rows(c), cols(g)]) * mixed[i] for i, (c, g) in enumerate(cg)]
    ms = [jnp.mean(y[i] * y[i], axis=-1, keepdims=True) for i in n]
    for i, (c, g) in enumerate(cg):
        ya_ref[rows(c), cols(g)] = (y[i] * lax.rsqrt(ms[i] + EPS) * og_ref[:, cols(g)]).astype(ya_ref.dtype)


def _chunk_mlp_prompt(h, ln_g, ln_b, ws, bs, out_g):
    m = h.shape[0]
    n_groups, d_a = ws.shape[0], ws.shape[0] * GROUP
    vec = pl.BlockSpec((1, d_a), lambda i: (0, 0))
    tm = CMLP_CHUNKS * MLP_CHUNK
    return pl.pallas_call(
        functools.partial(_cmlp_prompt_kernel, n_groups=n_groups, n_chunks=CMLP_CHUNKS),
        grid=(m // tm,),
        in_specs=[pl.BlockSpec((tm, d_a), lambda i: (i, 0)),
                  pl.BlockSpec((tm, d_a), lambda i: (i, 1)),
                  vec, vec,
                  pl.BlockSpec((n_groups, MLP_CHUNK, MLP_CHUNK), lambda i: (0, 0, 0)),
                  pl.BlockSpec((MLP_CHUNK, n_groups), lambda i: (0, 0)),
                  vec],
        out_specs=pl.BlockSpec((tm, d_a), lambda i: (i, 0)),
        out_shape=jax.ShapeDtypeStruct((m, d_a), BF16),
        compiler_params=_params(("arbitrary",)),
        name="chunk_mlp_prompt",
    )(h, h, ln_g.reshape(1, d_a), ln_b.reshape(1, d_a), ws, bs.T, out_g.reshape(1, d_a))


def _cmlp_sample_kernel(u_ref, v_ref, lng_ref, lnb_ref, wv_ref, bv_ref, og_ref, ya_ref, vr_ref, *, n_pos):
    vh = []
    for t in range(n_pos):
        vt = _layer_norm(_gelu(v_ref[t]), lng_ref[0], lnb_ref[0])
        vr_ref[:, t, :] = vt
        vh.append(vt)
    for t in range(n_pos):
        mixed = wv_ref[t, 0:1, :] * vh[0]
        for s in range(1, t + 1):
            mixed = mixed + wv_ref[t, s:s + 1, :] * vh[s]
        mixed = mixed + bv_ref[0, t:t + 1, :]
        y = _gelu(u_ref[t]) * mixed
        ya_ref[t] = _rms_norm(y, og_ref[0]).astype(ya_ref.dtype)


def _chunk_mlp_sample(h3, ln_g, ln_b, ws, bs, out_g):
    n_pos, nb, _ = h3.shape
    n_groups = ws.shape[0]
    d_a = n_groups * GROUP
    wv = jnp.repeat(jnp.transpose(ws[:, :n_pos, :n_pos], (1, 2, 0)), GROUP, axis=-1)
    bv = jnp.repeat(bs[:, :n_pos].T, GROUP, axis=-1).reshape(1, n_pos, d_a)
    vec = pl.BlockSpec((1, 1, GROUP), lambda g: (0, 0, g))
    act = pl.BlockSpec((n_pos, nb, GROUP), lambda g: (0, 0, g))
    return pl.pallas_call(
        functools.partial(_cmlp_sample_kernel, n_pos=n_pos),
        grid=(n_groups,),
        in_specs=[act,
                  pl.BlockSpec((n_pos, nb, GROUP), lambda g: (0, 0, n_groups + g)),
                  vec, vec,
                  pl.BlockSpec((n_pos, n_pos, GROUP), lambda g: (0, 0, g)),
                  pl.BlockSpec((1, n_pos, GROUP), lambda g: (0, 0, g)),
                  vec],
        out_specs=[act, pl.BlockSpec((nb, n_pos, GROUP), lambda g: (0, 0, g))],
        out_shape=[jax.ShapeDtypeStruct((n_pos, nb, d_a), BF16),
                   jax.ShapeDtypeStruct((nb, n_pos, d_a), F32)],
        compiler_params=_params(("arbitrary",)),
        name="chunk_mlp_sample",
    )(h3, h3, ln_g.reshape(1, 1, d_a), ln_b.reshape(1, 1, d_a), wv, bv, out_g.reshape(1, 1, d_a))


def _conv_taps(xbuf_ref, w_ref, tm, shift, halo, width):
    acc = None
    for j in range(width):
        off = halo - (width - 1 - j) * shift
        term = w_ref[j:j + 1, :] * xbuf_ref[off:off + tm, :]
        acc = term if acc is None else acc + term
    return acc


def _qkv_prep_kernel(*refs, tm, shift, halo, n_heads, has_buf):
    if has_buf:
        (xq_ref, xk_ref, xv_ref, bq_ref, bk_ref, bv_ref, wq_ref, wk_ref, wv_ref,
         q_ref, k_ref, v_ref, xbuf_ref, carry_ref) = refs
        bufs = (bq_ref, bk_ref, bv_ref)
    else:
        (xq_ref, xk_ref, xv_ref, wq_ref, wk_ref, wv_ref,
         q_ref, k_ref, v_ref, xbuf_ref, carry_ref, pbuf_ref, pcarry_ref) = refs
        bufs = (None, None, None)
    first = pl.program_id(1) == 0
    parts = ((xq_ref, wq_ref, q_ref, GROUP ** -0.5), (xk_ref, wk_ref, k_ref, 1.0), (xv_ref, wv_ref, v_ref, None))
    for p, (x_ref, w_ref, o_ref, scale) in enumerate(parts):
        @pl.when(first)
        def _(p=p):
            if has_buf:
                xbuf_ref[0:halo, :] = bufs[p][...]
            else:
                xbuf_ref[0:halo, :] = jnp.zeros((halo, xbuf_ref.shape[1]), F32)
                pbuf_ref[0:halo, :] = jnp.zeros((halo, pbuf_ref.shape[1]), F32)

        @pl.when(jnp.logical_not(first))
        def _(p=p):
            xbuf_ref[0:halo, :] = carry_ref[p]
            if not has_buf:
                pbuf_ref[0:halo, :] = pcarry_ref[p]

        xbuf_ref[halo:halo + tm, :] = x_ref[...]
        carry_ref[p] = xbuf_ref[tm:tm + halo, :]
        if has_buf:
            y = _silu(_conv_taps(xbuf_ref, w_ref, tm, shift, halo, CONV_QKV))
        else:
            x0 = xbuf_ref[halo:halo + tm, :]
            x1 = xbuf_ref[halo - 1:halo - 1 + tm, :]
            pbuf_ref[halo:halo + tm, :] = w_ref[1:2, :] * x0 + w_ref[0:1, :] * x1
            pcarry_ref[p] = pbuf_ref[tm:tm + halo, :]
            y = _silu((w_ref[3:4, :] * x0 + w_ref[2:3, :] * x1) + pbuf_ref[halo - 2:halo - 2 + tm, :])
        if scale is None:
            o_ref[...] = y
        else:
            for h in range(n_heads):
                sl = slice(h * GROUP, (h + 1) * GROUP)
                yh = y[:, sl]
                inv = lax.rsqrt(jnp.sum(yh * yh, axis=-1, keepdims=True) + 1e-6)
                o_ref[:, sl] = yh * (inv * scale if scale != 1.0 else inv)


def _qkv_prep(h, conv_w, bufs, n_seq, rows_per_seq, tm, shift, col0):
    d_b = conv_w.shape[1] // 3
    n_heads = d_b // GROUP
    has_buf = bufs is not None
    assert has_buf or (shift == 1 and CONV_QKV == 4), "the fresh-sequence path pairs 4 taps at row shift 1"
    halo = (CONV_QKV - 1) * shift if has_buf else SUBLANES
    tiles = rows_per_seq // tm
    cb = col0 // d_b
    x_specs = [pl.BlockSpec((tm, d_b), lambda s, i, c=c: (s * tiles + i, cb + c)) for c in range(3)]
    w_specs = [pl.BlockSpec((CONV_QKV, d_b), lambda s, i, c=c: (0, c)) for c in range(3)]
    o_spec = pl.BlockSpec((tm, d_b), lambda s, i: (s * tiles + i, 0))
    args, specs = [h, h, h], list(x_specs)
    if has_buf:
        specs += [pl.BlockSpec((halo, d_b), lambda s, i, c=c: (s, c)) for c in range(3)]
        args += [bufs, bufs, bufs]
    specs += w_specs
    args += [conv_w, conv_w, conv_w]
    m = n_seq * rows_per_seq
    return pl.pallas_call(
        functools.partial(_qkv_prep_kernel, tm=tm, shift=shift, halo=halo, n_heads=n_heads, has_buf=has_buf),
        grid=(n_seq, tiles),
        in_specs=specs,
        out_specs=[o_spec, o_spec, o_spec],
        out_shape=[jax.ShapeDtypeStruct((m, d_b), F32)] * 3,
        scratch_shapes=[pltpu.VMEM((halo + tm, d_b), F32), pltpu.VMEM((3, halo, d_b), F32)] * (1 if has_buf else 2),
        compiler_params=_params(("arbitrary", "arbitrary")),
        name="qkv_prep",
    )(*args)


def _gates(ab, alog, dtb):
    g = -jnp.exp(alog) * jax.nn.softplus(ab + dtb)
    beta = jax.nn.sigmoid(ab)
    return g, beta


def _delta_prompt_kernel(q_ref, k_ref, v_ref, z_ref, ab_ref, alog_ref, dtb_ref, ng_ref,
                         yb_ref, sout_ref, s_ref, *, n_heads, n_chunks, ns):
    c = DN_CHUNK
    ci = pl.program_id(1)

    @pl.when(ci == 0)
    def _():
        s_ref[...] = jnp.zeros(s_ref.shape, F32)

    row = lax.broadcasted_iota(jnp.int32, (c, c), 0)
    col = lax.broadcasted_iota(jnp.int32, (c, c), 1)
    causal = row >= col
    strict = row > col
    eye = (row == col).astype(F32)
    tril = causal.astype(F32)
    n_double = (c - 1).bit_length() - 1
    ch = [(s, h) for s in range(ns) for h in range(n_heads)]
    sl = lambda h: slice(h * GROUP, (h + 1) * GROUP)
    gates = [_gates(ab_ref[s], alog_ref[...], dtb_ref[...]) for s in range(ns)]
    gc_s = [jnp.dot(tril, gates[s][0], precision=lax.Precision.HIGHEST, preferred_element_type=F32)
            for s in range(ns)]
    gct_s = [g.T for g in gc_s]
    q = [q_ref[s, :, sl(h)] for s, h in ch]
    k = [k_ref[s, :, sl(h)] for s, h in ch]
    gcol = [gc_s[s][:, h:h + 1] for s, h in ch]
    beta = [gates[s][1][:, n_heads + h:n_heads + h + 1] for s, h in ch]
    n = range(len(ch))
    decay = [jnp.exp(jnp.where(causal, gcol[i] - gct_s[s][h:h + 1, :], -jnp.inf)) for i, (s, h) in enumerate(ch)]
    kb = [k[i] * beta[i] for i in n]
    kbf = [k[i].astype(BF16) for i in n]
    kq = [lax.dot_general(jnp.concatenate([kb[i], q[i]], axis=0).astype(BF16), kbf[i], _NT,
                          preferred_element_type=F32) for i in n]
    qk = [kq[i][c:] * decay[i] for i in n]
    y = [jnp.where(strict, -(kq[i][:c] * decay[i]), 0.0) for i in n]
    x = [eye + y[i] for i in n]
    y = [_bdot(y[i], y[i]) for i in n]
    for _ in range(n_double - 1):
        t = [_bdot(jnp.concatenate([x[i], y[i]], axis=0), y[i]) for i in n]
        x = [x[i] + t[i][:c] for i in n]
        y = [t[i][c:] for i in n]
    x = [x[i] + _bdot(x[i], y[i]) for i in n]
    egc = [jnp.exp(gcol[i]) for i in n]
    sol = [_bdot(x[i], jnp.concatenate([v_ref[s, :, sl(h)] * beta[i], kb[i] * egc[i]], axis=1))
           for i, (s, h) in enumerate(ch)]
    s_old = [s_ref[s, h] for s, h in ch]
    r = [_bdot(jnp.concatenate([sol[i][:, GROUP:], q[i] * egc[i]], axis=0), s_old[i]) for i in n]
    v_new = [sol[i][:, :GROUP] - r[i][:c] for i in n]
    o = [r[i][c:] + _bdot(qk[i], v_new[i]) for i in n]
    gl = [gc_s[s][c - 1:c, h:h + 1] for s, h in ch]
    upd = [lax.dot_general((k[i] * jnp.exp(gl[i] - gcol[i])).astype(BF16), v_new[i].astype(BF16), _TN,
                           preferred_element_type=F32) for i in n]
    ms = [jnp.mean(o[i] * o[i], axis=-1, keepdims=True) for i in n]
    for i, (s, h) in enumerate(ch):
        s_ref[s, h] = s_old[i] * jnp.exp(gl[i]) + upd[i]
        on = o[i] * lax.rsqrt(ms[i] + EPS) * ng_ref[...]
        yb_ref[s, :, sl(h)] = (on * _silu(z_ref[s, :, sl(h)])).astype(yb_ref.dtype)

    @pl.when(ci == n_chunks - 1)
    def _():
        sout_ref[...] = s_ref[...]


def _delta_prompt(q, k, v, h, z_col0, ab, alog, dtb, norm_g, n_seq, rows_per_seq, ns):
    d_b = q.shape[1]
    n_heads = d_b // GROUP
    n_chunks = rows_per_seq // DN_CHUNK
    r3 = lambda t: t.reshape(n_seq, rows_per_seq, t.shape[-1])
    act = pl.BlockSpec((ns, DN_CHUNK, d_b), lambda s, i: (s, i, 0))
    vec = pl.BlockSpec((1, LANES), lambda s, i: (0, 0))
    zb = z_col0 // d_b
    yb, s_new = pl.pallas_call(
        functools.partial(_delta_prompt_kernel, n_heads=n_heads, n_chunks=n_chunks, ns=ns),
        grid=(n_seq // ns, n_chunks),
        in_specs=[act, act, act,
                  pl.BlockSpec((ns, DN_CHUNK, d_b), lambda s, i: (s, i, zb)),
                  pl.BlockSpec((ns, DN_CHUNK, LANES), lambda s, i: (s, i, 0)),
                  vec, vec, vec],
        out_specs=[act, pl.BlockSpec((ns, n_heads, GROUP, GROUP), lambda s, i: (s, 0, 0, 0))],
        out_shape=[jax.ShapeDtypeStruct((n_seq, rows_per_seq, d_b), BF16),
                   jax.ShapeDtypeStruct((n_seq, n_heads, GROUP, GROUP), F32)],
        scratch_shapes=[pltpu.VMEM((ns, n_heads, GROUP, GROUP), F32)],
        compiler_params=_params(("arbitrary", "arbitrary")),
        name="delta_prompt",
    )(r3(q), r3(k), r3(v), r3(h), r3(ab), alog, dtb, norm_g)
    return yb.reshape(n_seq * rows_per_seq, d_b), s_new


def _delta_sample_kernel(q_ref, k_ref, v_ref, z_ref, ab_ref, alog_ref, dtb_ref, ng_ref, s0_ref,
                         yb_ref, s1_ref, lhs_ref, res_ref, kt_ref, vn_ref, gl_ref, *, n_heads, n_pos, bb):
    nt = n_pos
    hs = range(n_heads)
    sls = [slice(h * GROUP, (h + 1) * GROUP) for h in hs]
    gates = [_gates(ab_ref[t], alog_ref[...], dtb_ref[...]) for t in range(nt)]
    zeros_pad = jnp.zeros((bb, GROUP), F32)
    ts = range(nt)
    q = [[q_ref[t, :, sls[h]] for t in ts] for h in hs]
    k = [[k_ref[t, :, sls[h]] for t in ts] for h in hs]
    beta = [[gates[t][1][:, n_heads + h:n_heads + h + 1] for t in ts] for h in hs]
    gc = []
    for h in hs:
        g = [gates[t][0][:, h:h + 1] for t in ts]
        for t in range(1, nt):
            g[t] = g[t - 1] + g[t]
        gc.append(g)
    kb = [[k[h][t] * beta[h][t] for t in ts] for h in hs]
    kk = [[[jnp.sum(kb[h][t] * k[h][s], axis=-1, keepdims=True) for s in range(t)] for t in ts] for h in hs]
    qk = [[[jnp.sum(q[h][t] * k[h][s], axis=-1, keepdims=True) for s in range(t + 1)] for t in ts] for h in hs]
    dec = [[[jnp.exp(gc[h][t] - gc[h][s]) for s in range(t + 1)] for t in ts] for h in hs]
    a = [[[kk[h][t][s] * dec[h][t][s] for s in range(t)] for t in ts] for h in hs]
    qk = [[[qk[h][t][s] * dec[h][t][s] for s in range(t + 1)] for t in ts] for h in hs]
    egc = [[jnp.exp(gc[h][t]) for t in ts] for h in hs]
    u_blk = [[v_ref[t, :, sls[h]] * beta[h][t] for t in ts] for h in hs]
    w_blk = [[kb[h][t] * egc[h][t] for t in ts] for h in hs]
    for t in ts:
        for h in hs:
            for s in range(t):
                u_blk[h][t] = u_blk[h][t] - a[h][t][s] * u_blk[h][s]
                w_blk[h][t] = w_blk[h][t] - a[h][t][s] * w_blk[h][s]
    for h in hs:
        for t in ts:
            lhs_ref[h, :, t, :] = w_blk[h][t]
            lhs_ref[h, :, nt + t, :] = q[h][t] * egc[h][t]
        gl_ref[h] = jnp.broadcast_to(egc[h][nt - 1], (bb, GROUP))

    for h in hs:
        for b in range(bb):
            res_ref[h, b] = _bdot(lhs_ref[h, b], s0_ref[b, h])

    v_new = [[u_blk[h][t] - res_ref[h, :, t, :] for t in ts] for h in hs]
    o = [[res_ref[h, :, nt + t, :] for t in ts] for h in hs]
    for h in hs:
        for t in ts:
            for s in range(t + 1):
                o[h][t] = o[h][t] + qk[h][t][s] * v_new[h][s]
    ms = [[jnp.mean(o[h][t] * o[h][t], axis=-1, keepdims=True) for t in ts] for h in hs]
    for h in hs:
        gl = gc[h][nt - 1]
        for t in ts:
            on = o[h][t] * lax.rsqrt(ms[h][t] + EPS) * ng_ref[...]
            yb_ref[t, :, sls[h]] = (on * _silu(z_ref[t, :, sls[h]])).astype(yb_ref.dtype)
            kt_ref[h, :, t, :] = k[h][t] * jnp.exp(gl - gc[h][t])
            vn_ref[h, :, t, :] = v_new[h][t]
            kt_ref[h, :, nt + t, :] = zeros_pad
            vn_ref[h, :, nt + t, :] = zeros_pad

    for h in hs:
        for b in range(bb):
            upd = lax.dot_general(kt_ref[h, b].astype(BF16), vn_ref[h, b].astype(BF16), _TN,
                                  preferred_element_type=F32)
            s1_ref[b, h] = s0_ref[b, h] * gl_ref[h, b:b + 1, :] + upd


def _delta_sample(q, k, v, h3, z_col0, ab, alog, dtb, norm_g, s0, bb):
    n_pos, nb, d_b = q.shape
    n_heads = d_b // GROUP
    assert 2 * n_pos == SUBLANES, "one state read packs [w | q] rows of a sequence into one 8-row tile"
    zb = z_col0 // d_b
    act = pl.BlockSpec((n_pos, bb, d_b), lambda i: (0, i, 0))
    vec = pl.BlockSpec((1, LANES), lambda i: (0, 0))
    st = pl.BlockSpec((bb, n_heads, GROUP, GROUP), lambda i: (i, 0, 0, 0))
    tile = pltpu.VMEM((n_heads, bb, SUBLANES, GROUP), F32)
    return pl.pallas_call(
        functools.partial(_delta_sample_kernel, n_heads=n_heads, n_pos=n_pos, bb=bb),
        grid=(nb // bb,),
        in_specs=[act, act, act,
                  pl.BlockSpec((n_pos, bb, d_b), lambda i: (0, i, zb)),
                  pl.BlockSpec((n_pos, bb, LANES), lambda i: (0, i, 0)),
                  vec, vec, vec, st],
        out_specs=[act, st],
        out_shape=[jax.ShapeDtypeStruct((n_pos, nb, d_b), BF16),
                   jax.ShapeDtypeStruct(s0.shape, F32)],
        scratch_shapes=[tile, tile, tile, tile, pltpu.VMEM((n_heads, bb, GROUP), F32)],
        compiler_params=_params(("arbitrary",)),
        name="delta_sample",
    )(q, k, v, h3, ab, alog, dtb, norm_g, s0)


def _out_proj_kernel(ya_ref, yb_ref, w_ref, r_ref, g_ref, b_ref, o_ref, ob_ref, z_ref, *, n_tiles, d_a, alpha):
    i = pl.program_id(0)

    @pl.when(i == 0)
    def _():
        z_ref[...] = jnp.zeros(z_ref.shape, F32)

    def emit_ln():
        y = _layer_norm(z_ref[...], g_ref[...], b_ref[...])
        o_ref[...] = y
        ob_ref[...] = y.astype(ob_ref.dtype)

    @pl.when(i < n_tiles)
    def _():
        emit_ln()
        acc = (jnp.dot(ya_ref[...], w_ref[0:d_a, :], preferred_element_type=F32)
               + jnp.dot(yb_ref[...], w_ref[d_a:, :], preferred_element_type=F32))
        z_ref[...] = alpha * r_ref[...] + acc

    @pl.when(i == n_tiles)
    def _():
        emit_ln()


def _out_proj_ln(ya, yb, w, resid, g, b, alpha, tm):
    m, d_a = ya.shape
    k, d = w.shape
    n_tiles = m // tm
    cur = lambda i: (jnp.minimum(i, n_tiles - 1), 0)
    prev = lambda i: (jnp.maximum(i - 1, 0), 0)
    vec = pl.BlockSpec((1, d), lambda i: (0, 0))
    return pl.pallas_call(
        functools.partial(_out_proj_kernel, n_tiles=n_tiles, d_a=d_a, alpha=alpha),
        grid=(n_tiles + 1,),
        in_specs=[pl.BlockSpec((tm, d_a), cur),
                  pl.BlockSpec((tm, k - d_a), cur),
                  pl.BlockSpec((k, d), lambda i: (0, 0), pipeline_mode=pl.Buffered(1)),
                  pl.BlockSpec((tm, d), cur), vec, vec],
        out_specs=[pl.BlockSpec((tm, d), prev), pl.BlockSpec((tm, d), prev)],
        out_shape=[jax.ShapeDtypeStruct((m, d), F32), jax.ShapeDtypeStruct((m, d), BF16)],
        scratch_shapes=[pltpu.VMEM((tm, d), F32)],
        compiler_params=_params(("arbitrary",)),
        name="out_proj_ln1",
    )(ya, yb, w, resid, g.reshape(1, d), b.reshape(1, d))


def _down_kernel(x_ref, w_ref, r_ref, g_ref, b_ref, o_ref, *rest, n_tiles, nk, nc, alpha, n_pos):
    z_ref = rest[-1]
    cw = z_ref.shape[-1]
    slab = lambda c: slice(c * cw, (c + 1) * cw)
    i, kk = pl.program_id(0), pl.program_id(1)

    def emit_ln():
        z = [z_ref[c] for c in range(nc)]
        inv_d = 1.0 / (nc * cw)
        mu = sum(jnp.sum(zc, axis=-1, keepdims=True) for zc in z) * inv_d
        xc = [zc - mu for zc in z]
        var = sum(jnp.sum(c * c, axis=-1, keepdims=True) for c in xc) * inv_d
        rs = lax.rsqrt(var + EPS)
        for c in range(nc):
            y = xc[c] * rs * g_ref[:, slab(c)] + b_ref[:, slab(c)]
            if n_pos:
                nb = y.shape[0] // n_pos
                for t in range(n_pos):
                    o_ref[:, t, slab(c)] = y[t * nb:(t + 1) * nb]
            else:
                o_ref[:, slab(c)] = y

    @pl.when(jnp.logical_and(i == 0, kk == 0))
    def _():
        z_ref[...] = jnp.zeros(z_ref.shape, F32)

    wb = w_ref[...].astype(BF16)
    if len(rest) == 2:
        rest[0][...] = wb

    @pl.when(kk == 0)
    def _():
        emit_ln()
        part = jnp.dot(x_ref[...], wb, preferred_element_type=F32)
        for c in range(nc):
            if nc == nk:
                z_ref[c] = part[:, slab(c)] + alpha * r_ref[...] if c == 0 else part[:, slab(c)]
            else:
                z_ref[c] = part[:, slab(c)] + alpha * r_ref[:, slab(c)]

    @pl.when(jnp.logical_and(kk > 0, i < n_tiles))
    def _():
        part = jnp.dot(x_ref[...], wb, preferred_element_type=F32)
        for c in range(nc):
            z_ref[c] += part[:, slab(c)]
        if nc == nk:
            z_ref[kk] += alpha * r_ref[...]


def _down_ln(x, w, resid, g, b, alpha, tm, tk, emit_w=False, n_pos=0):
    m, kdim = x.shape
    d = w.shape[1]
    nk = kdim // tk
    n_tiles = m // tm
    nc = nk if (d % nk == 0 and (d // nk) % LANES == 0) else 1
    row = lambda i: jnp.minimum(i, n_tiles - 1)
    kt = lambda i, kk: jnp.where(i == n_tiles, nk - 1, kk)
    vec = pl.BlockSpec((1, d), lambda i, kk: (0, 0))
    wspec = pl.BlockSpec((tk, d), lambda i, kk: (kt(i, kk), 0))
    if nc == nk:
        rspec = pl.BlockSpec((tm, d // nc), lambda i, kk: (row(i), kt(i, kk)))
    else:
        rspec = pl.BlockSpec((tm, d), lambda i, kk: (row(i), 0))
    if n_pos:
        assert n_tiles == 1
        out_specs = [pl.BlockSpec((m // n_pos, n_pos, d), lambda i, kk: (0, 0, 0))]
        out_shape = [jax.ShapeDtypeStruct((m // n_pos, n_pos, d), F32)]
    else:
        out_specs = [pl.BlockSpec((tm, d), lambda i, kk: (jnp.where(kk == 0, jnp.maximum(i - 1, 0), row(i)), 0))]
        out_shape = [jax.ShapeDtypeStruct((m, d), F32)]
    if emit_w:
        out_specs.append(wspec)
        out_shape.append(jax.ShapeDtypeStruct(w.shape, BF16))
    outs = pl.pallas_call(
        functools.partial(_down_kernel, n_tiles=n_tiles, nk=nk, nc=nc, alpha=alpha, n_pos=n_pos),
        grid=(n_tiles + 1, nk),
        in_specs=[pl.BlockSpec((tm, tk), lambda i, kk: (row(i), kt(i, kk))), wspec, rspec, vec, vec],
        out_specs=out_specs,
        out_shape=out_shape,
        scratch_shapes=[pltpu.VMEM((nc, tm, d // nc), F32)],
        compiler_params=_params(("arbitrary", "arbitrary"), BIG_VMEM_LIMIT),
        name="down_ln2",
    )(x, w, resid, g.reshape(1, d), b.reshape(1, d))
    return tuple(outs) if emit_w else outs[0]


def _ffn_up_kernel(x_ref, wg_ref, wv_ref, cwg_ref, cwv_ref, cbg_ref, cbv_ref,
                   act_ref, tg_ref, tv_ref, wgo_ref, wvo_ref, wgb_ref, wvb_ref, carry_ref, *, tm):
    halo = SUBLANES
    first = pl.program_id(2) == 0

    @pl.when(jnp.logical_and(pl.program_id(1) == 0, first))
    def _():
        for w_ref, wb_ref, wo_ref in ((wg_ref, wgb_ref, wgo_ref), (wv_ref, wvb_ref, wvo_ref)):
            wb = w_ref[...].astype(BF16)
            wb_ref[...] = wb
            wo_ref[...] = wb

    @pl.when(first)
    def _():
        carry_ref[...] = jnp.zeros(carry_ref.shape, F32)

    x = x_ref[...]
    conv = []
    parts = ((wgb_ref, cwg_ref, cbg_ref, tg_ref), (wvb_ref, cwv_ref, cbv_ref, tv_ref))
    row = lax.broadcasted_iota(jnp.int32, (halo, act_ref.shape[1]), 0)
    for p, (w_ref, cw_ref, cb_ref, t_ref) in enumerate(parts):
        up = jnp.dot(x, w_ref[...], preferred_element_type=F32)
        hist = carry_ref[p]
        tail = up[tm - halo:tm, :]
        carry_ref[p] = tail
        t_ref[0] = tail
        acc = cw_ref[CONV_FFN - 1:CONV_FFN, :] * up
        for dd in range(1, CONV_FFN):
            xs = pltpu.roll(up, dd, axis=0)
            head = jnp.where(row < dd, pltpu.roll(hist, dd, axis=0), xs[0:halo])
            xs = jnp.concatenate([head, xs[halo:]], axis=0)
            acc = acc + cw_ref[CONV_FFN - 1 - dd:CONV_FFN - dd, :] * xs
        conv.append(acc + cb_ref[...])
    act_ref[...] = (_silu(conv[0]) * conv[1]).astype(act_ref.dtype)


def _ffn_up_carried_kernel(x_ref, wg_ref, wv_ref, buf_ref, cw_ref, cb_ref, act_ref, t_ref, xbuf_ref, gate_ref,
                           *, shift):
    p, j = pl.program_id(0), pl.program_id(1)
    tm, halo = x_ref.shape[0], buf_ref.shape[0]
    xbuf_ref[0:halo, :] = buf_ref[...]

    def conv_of(w_ref):
        xbuf_ref[halo:halo + tm, :] = jnp.dot(x_ref[...], w_ref[...], preferred_element_type=F32)
        tail = xbuf_ref[tm:tm + halo, :]
        for t in range(halo // shift):
            t_ref[:, t, :] = tail[t * shift:(t + 1) * shift]
        return _conv_taps(xbuf_ref, cw_ref, tm, shift, halo, CONV_FFN) + cb_ref[...]

    @pl.when(p == 0)
    def _():
        gate_ref[j] = conv_of(wg_ref)

    @pl.when(p == 1)
    def _():
        act_ref[...] = (_silu(gate_ref[j]) * conv_of(wv_ref)).astype(act_ref.dtype)


def _ffn_up_carried(x, w_gate_b, w_val_b, conv_w, conv_b, bufs, tn, shift):
    m, d = x.shape
    d_ff = w_gate_b.shape[1]
    nj = d_ff // tn
    halo = (CONV_FFN - 1) * shift
    col = lambda p, j: (0, p * nj + j)
    return pl.pallas_call(
        functools.partial(_ffn_up_carried_kernel, shift=shift),
        grid=(2, nj),
        in_specs=[pl.BlockSpec((m, d), lambda p, j: (0, 0)),
                  pl.BlockSpec((d, tn), lambda p, j: (0, jnp.where(p == 0, j, nj - 1))),
                  pl.BlockSpec((d, tn), lambda p, j: (0, jnp.where(p == 1, j, 0))),
                  pl.BlockSpec((halo, tn), col),
                  pl.BlockSpec((CONV_FFN, tn), col),
                  pl.BlockSpec((1, tn), col)],
        out_specs=[pl.BlockSpec((m, tn), lambda p, j: (0, jnp.where(p == 1, j, 0))),
                   pl.BlockSpec((shift, halo // shift, tn), lambda p, j: (0, 0, p * nj + j))],
        out_shape=[jax.ShapeDtypeStruct((m, d_ff), BF16),
                   jax.ShapeDtypeStruct((shift, halo // shift, 2 * d_ff), F32)],
        scratch_shapes=[pltpu.VMEM((halo + m, tn), F32), pltpu.VMEM((nj, m, tn), F32)],
        compiler_params=_params(("arbitrary", "arbitrary")),
        name="ffn_up_carried",
    )(x, w_gate_b, w_val_b, bufs, conv_w, conv_b.reshape(1, 2 * d_ff))


def _ffn_up(x, w_up, conv_w, conv_b, n_seq, rows_per_seq, tm, tn):
    m, d = x.shape
    d_ff = w_up.shape[1] // 2
    nj = d_ff // tn
    halo = SUBLANES
    tiles = rows_per_seq // tm
    wspec = lambda off: pl.BlockSpec((d, tn), lambda j, s, i: (0, off + j))
    cwspec = lambda off: pl.BlockSpec((CONV_FFN, tn), lambda j, s, i: (0, off + j))
    cbspec = lambda off: pl.BlockSpec((1, tn), lambda j, s, i: (0, off + j))
    tspec = pl.BlockSpec((1, halo, tn), lambda j, s, i: (s, 0, j))
    tshape = jax.ShapeDtypeStruct((n_seq, halo, d_ff), F32)
    cb2 = conv_b.reshape(1, 2 * d_ff)
    return pl.pallas_call(
        functools.partial(_ffn_up_kernel, tm=tm),
        grid=(nj, n_seq, tiles),
        in_specs=[pl.BlockSpec((tm, d), lambda j, s, i: (s * tiles + i, 0)), wspec(0), wspec(nj),
                  cwspec(0), cwspec(nj), cbspec(0), cbspec(nj)],
        out_specs=[pl.BlockSpec((tm, tn), lambda j, s, i: (s * tiles + i, j)), tspec, tspec, wspec(0), wspec(0)],
        out_shape=[jax.ShapeDtypeStruct((m, d_ff), BF16), tshape, tshape]
                  + [jax.ShapeDtypeStruct((d, d_ff), BF16)] * 2,
        scratch_shapes=[pltpu.VMEM((d, tn), BF16)] * 2 + [pltpu.VMEM((2, halo, tn), F32)],
        compiler_params=_params(("arbitrary", "arbitrary", "arbitrary")),
        name="ffn_up",
    )(x, w_up, w_up, conv_w, conv_w, cb2, cb2)


def _pad_lanes(vec, offset=0):
    out = jnp.zeros((1, LANES), F32)
    return out.at[0, offset:offset + vec.shape[0]].set(vec.astype(F32))


def _mixer_half(x2, lw, alpha, n_seq, rows_per_seq, shift, state, w_in_b):
    (w_in, conv_qkv_w, a_log, dt_bias, norm_g, gm_ln_g, gm_ln_b, gm_ws, gm_bs, gm_out_g, w_out,
     ln1_g, ln1_b) = lw
    m, d_model = x2.shape
    n_groups = gm_ws.shape[0]
    d_a = n_groups * GROUP
    d_qkv = conv_qkv_w.shape[1]
    d_b = d_qkv // 3
    n_heads = d_b // GROUP
    main_cols = 2 * d_a + d_qkv + d_b
    w_ab = jnp.pad(w_in[main_cols:].astype(BF16), ((0, LANES - 2 * n_heads), (0, 0)))
    if w_in_b is None:
        h, ab, w_in_b = _matmul(x2, w_in, w_ab, main_cols, min(m, PROJ_TM), PROJ_EMIT_TN, emit_w=True)
    else:
        h, ab = _matmul(x2, w_in_b, w_ab, main_cols, min(m, PROJ_TM), PROJ_TN)
    alog, dtb, ng = _pad_lanes(a_log), _pad_lanes(dt_bias), norm_g.reshape(1, GROUP).astype(F32)

    if state is None:
        ya = _chunk_mlp_prompt(h, gm_ln_g, gm_ln_b, gm_ws, gm_bs, gm_out_g)
        v_rows = None
        q, k, v = _qkv_prep(h, conv_qkv_w, None, n_seq, rows_per_seq, QKV_TM, shift, 2 * d_a)
        yb, s_new = _delta_prompt(q, k, v, h, 2 * d_a + d_qkv, ab, alog, dtb, ng, n_seq, rows_per_seq, DELTA_SEQS)
    else:
        s_dn, buf_qkv = state
        n_pos = rows_per_seq // shift
        h3 = h.reshape(n_pos, shift, main_cols)
        ya3, v_rows = _chunk_mlp_sample(h3, gm_ln_g, gm_ln_b, gm_ws, gm_bs, gm_out_g)
        ya = ya3.reshape(m, d_a)
        q, k, v = _qkv_prep(h, conv_qkv_w, buf_qkv, n_seq, rows_per_seq, rows_per_seq, shift, 2 * d_a)
        r3 = lambda t: t.reshape(n_pos, shift, t.shape[-1])
        yb3, s_new = _delta_sample(r3(q), r3(k), r3(v), h3, 2 * d_a + d_qkv, r3(ab), alog, dtb, ng, s_dn,
                                   DELTA_SAMPLE_ROWS)
        yb = yb3.reshape(m, d_b)

    x1, x1b = _out_proj_ln(ya, yb, w_out, x2, ln1_g, ln1_b, alpha, min(m, OUT_TM))
    return x1, x1b, s_new, h, v_rows, w_in_b


def kernel(x_prompt, x_sample, state_dn, state_conv_qkv, state_conv_ffn, w_in, conv_qkv_w, dn_a_log, dn_dt_bias, dn_norm_g, gm_ln_g, gm_ln_b, gm_ws, gm_bs, gm_out_g, w_out, ln1_g, ln1_b, w_up, conv_ffn_w, conv_ffn_b, w_down, ln2_g, ln2_b):
    depth = w_in.shape[0]
    bp, lp, d_model = x_prompt.shape
    bs, ls, _ = x_sample.shape
    d_qkv = conv_qkv_w.shape[-1]
    d_a = gm_ws.shape[1] * GROUP
    alpha = (2.0 * depth) ** 0.25

    hp = x_prompt.reshape(bp * lp, d_model)
    hs = jnp.transpose(x_sample, (1, 0, 2)).reshape(ls * bs, d_model)
    outs = [[] for _ in range(7)]
    w_out = w_out.astype(BF16)
    w_in = jnp.swapaxes(w_in, 1, 2)
    for l in range(depth):
        lw = tuple(t[l] for t in (w_in, conv_qkv_w, dn_a_log, dn_dt_bias, dn_norm_g, gm_ln_g, gm_ln_b, gm_ws,
                                  gm_bs, gm_out_g, w_out, ln1_g, ln1_b))
        ffn_w = (conv_ffn_w[l], conv_ffn_b[l])
        ln2 = (ln2_g[l], ln2_b[l], alpha)
        n_pos = ls if l == depth - 1 else 0
        bq = jnp.transpose(state_conv_qkv[l], (1, 0, 2)).reshape((CONV_QKV - 1) * bs, d_qkv)
        bf = jnp.transpose(state_conv_ffn[l], (1, 0, 2)).reshape((CONV_FFN - 1) * bs, -1)
        x1_s, x1b_s, s_s, h_s, v_s, w_in_b = _mixer_half(hs, lw, alpha, 1, ls * bs, bs, (state_dn[l], bq), None)
        x1_p, x1b_p, s_p, h_p, _, _ = _mixer_half(hp, lw, alpha, bp, lp, 1, None, w_in_b)
        act_p, tg_p, tv_p, w_gate_b, w_val_b = _ffn_up(x1b_p, w_up[l], *ffn_w, bp, lp, FFN_TM, FFN_TN)
        act_s, cf_s = _ffn_up_carried(x1b_s, w_gate_b, w_val_b, *ffn_w, bf, FFN_TN, bs)
        hs, w_down_b = _down_ln(act_s, w_down[l], x1_s, *ln2, min(ls * bs, DOWN_EMIT_TM), DOWN_EMIT_TK,
                                emit_w=True, n_pos=n_pos)
        hp = _down_ln(act_p, w_down_b, x1_p, *ln2, min(bp * lp, DOWN_TM), DOWN_TK)
        h_s3 = h_s.reshape(ls, bs, -1)
        cq_s = jnp.transpose(h_s3[ls - (CONV_QKV - 1):, :, 2 * d_a:2 * d_a + d_qkv], (1, 0, 2))
        h_p3 = h_p.reshape(bp, lp, -1)
        cq_p = h_p3[:, lp - (CONV_QKV - 1):, 2 * d_a:2 * d_a + d_qkv]
        n_tail = tg_p.shape[1]
        cf_p = jnp.concatenate([tg_p, tv_p], axis=-1)[:, n_tail - (CONV_FFN - 1):]
        for acc, val in zip(outs, (s_p, cq_p, cf_p, s_s, cq_s, cf_s, v_s)):
            acc.append(val)
    return (hp.reshape(bp, lp, d_model), hs) + tuple(jnp.stack(o) for o in outs)
```

```python
import functools

import jax
import jax.numpy as jnp
from jax import lax
from jax.experimental import pallas as pl
from jax.experimental.pallas import tpu as pltpu

F32 = jnp.float32
BF16 = jnp.bfloat16

LANES = 128
SUBLANES = 8
VMEM_LIMIT = 56 * 1024 * 1024
BIG_VMEM_LIMIT = 62 * 1024 * 1024

GROUP = 128
MLP_CHUNK = 128
DN_CHUNK = 64
CONV_QKV = 4
CONV_FFN = 3
CMLP_CHUNKS = 8
EPS = 1e-5

PROJ_TM, PROJ_TN = 1024, 2048
PROJ_EMIT_TN = 1536
QKV_TM = 512
DELTA_SEQS = 4
DELTA_SAMPLE_ROWS = 16
OUT_TM = 512
FFN_TM, FFN_TN = 1024, 512
DOWN_TM = 512
DOWN_EMIT_TM, DOWN_EMIT_TK = 512, 1408

_NT = (((1,), (1,)), ((), ()))
_TN = (((0,), (0,)), ((), ()))


def _params(sem, vmem_limit=VMEM_LIMIT):
    return pltpu.CompilerParams(dimension_semantics=sem, vmem_limit_bytes=vmem_limit)


def _bdot(a, b):
    return jnp.dot(a.astype(BF16), b.astype(BF16), preferred_element_type=F32)


def _gelu(x):
    return 0.5 * x * (1.0 + lax.erf(x * (0.5 ** 0.5)))


def _silu(x):
    return x * jax.nn.sigmoid(x)


def _layer_norm(x, g, b):
    mu = jnp.mean(x, axis=-1, keepdims=True)
    xc = x - mu
    var = jnp.mean(xc * xc, axis=-1, keepdims=True)
    return xc * lax.rsqrt(var + EPS) * g + b


def _rms_norm(x, g):
    ms = jnp.mean(x * x, axis=-1, keepdims=True)
    return x * lax.rsqrt(ms + EPS) * g


def _mm_kernel(x_ref, w_ref, we_ref, o_ref, e_ref, *rest):
    xb_ref = rest[-1]

    @pl.when(pl.program_id(1) == 0)
    def _():
        xb = x_ref[...].astype(BF16)
        xb_ref[...] = xb
        e_ref[...] = lax.dot_general(xb, we_ref[...], _NT, preferred_element_type=F32)

    wb = w_ref[...].astype(BF16)
    if len(rest) == 2:
        rest[0][...] = wb
    o_ref[...] = lax.dot_general(xb_ref[...], wb, _NT, preferred_element_type=F32)


def _matmul(x, wt, wt_extra, n_cols, tm, tn, emit_w=False):
    m, k = x.shape
    ne = wt_extra.shape[0]
    out_specs = [pl.BlockSpec((tm, tn), lambda i, j: (i, j)), pl.BlockSpec((tm, ne), lambda i, j: (i, 0))]
    out_shape = [jax.ShapeDtypeStruct((m, n_cols), F32), jax.ShapeDtypeStruct((m, ne), F32)]
    if emit_w:
        out_specs.append(pl.BlockSpec((tn, k), lambda i, j: (j, 0)))
        out_shape.append(jax.ShapeDtypeStruct((n_cols, k), BF16))
    return pl.pallas_call(
        _mm_kernel,
        grid=(m // tm, n_cols // tn),
        in_specs=[pl.BlockSpec((tm, k), lambda i, j: (i, 0)),
                  pl.BlockSpec((tn, k), lambda i, j: (j, 0)),
                  pl.BlockSpec((ne, k), lambda i, j: (0, 0))],
        out_specs=out_specs,
        out_shape=out_shape,
        scratch_shapes=[pltpu.VMEM((tm, k), BF16)],
        compiler_params=_params(("arbitrary", "arbitrary"), BIG_VMEM_LIMIT),
        name="proj",
    )(x, wt, wt_extra)


def _cmlp_prompt_kernel(u_ref, v_ref, lng_ref, lnb_ref, ws_ref, bst_ref, og_ref, ya_ref, *, n_groups, n_chunks):
    row = lax.broadcasted_iota(jnp.int32, (MLP_CHUNK, MLP_CHUNK), 0)
    col = lax.broadcasted_iota(jnp.int32, (MLP_CHUNK, MLP_CHUNK), 1)
    cg = [(c, g) for c in range(n_chunks) for g in range(n_groups)]
    rows = lambda c: slice(c * MLP_CHUNK, (c + 1) * MLP_CHUNK)
    cols = lambda g: slice(g * GROUP, (g + 1) * GROUP)
    w = [jnp.where(row >= col, ws_ref[g], 0.0).astype(BF16) for g in range(n_groups)]
    n = range(len(cg))
    gv = [_gelu(v_ref[rows(c), cols(g)]) for c, g in cg]
    mu = [jnp.mean(gv[i], axis=-1, keepdims=True) for i in n]
    xc = [gv[i] - mu[i] for i in n]
    var = [jnp.mean(xc[i] * xc[i], axis=-1, keepdims=True) for i in n]
    vh = [xc[i] * lax.rsqrt(var[i] + EPS) * lng_ref[:, cols(g)] + lnb_ref[:, cols(g)] for i, (c, g) in enumerate(cg)]
    mixed = [jnp.dot(w[g], vh[i].astype(BF16), preferred_element_type=F32) + bst_ref[:, g:g + 1]
             for i, (c, g) in enumerate(cg)]
    y = [_gelu(u_ref[rows(c), cols(g)]) * mixed[i] for i, (c, g) in enumerate(cg)]
    ms = [jnp.mean(y[i] * y[i], axis=-1, keepdims=True) for i in n]
    for i, (c, g) in enumerate(cg):
        ya_ref[rows(c), cols(g)] = (y[i] * lax.rsqrt(ms[i] + EPS) * og_ref[:, cols(g)]).astype(ya_ref.dtype)


def _chunk_mlp_prompt(h, ln_g, ln_b, ws, bs, out_g):
    m = h.shape[0]
    n_groups, d_a = ws.shape[0], ws.shape[0] * GROUP
    vec = pl.BlockSpec((1, d_a), lambda i: (0, 0))
    tm = CMLP_CHUNKS * MLP_CHUNK
    return pl.pallas_call(
        functools.partial(_cmlp_prompt_kernel, n_groups=n_groups, n_chunks=CMLP_CHUNKS),
        grid=(m // tm,),
        in_specs=[pl.BlockSpec((tm, d_a), lambda i: (i, 0)),
                  pl.BlockSpec((tm, d_a), lambda i: (i, 1)),
                  vec, vec,
                  pl.BlockSpec((n_groups, MLP_CHUNK, MLP_CHUNK), lambda i: (0, 0, 0)),
                  pl.BlockSpec((MLP_CHUNK, n_groups), lambda i: (0, 0)),
                  vec],
        out_specs=pl.BlockSpec((tm, d_a), lambda i: (i, 0)),
        out_shape=jax.ShapeDtypeStruct((m, d_a), BF16),
        compiler_params=_params(("arbitrary",)),
        name="chunk_mlp_prompt",
    )(h, h, ln_g.reshape(1, d_a), ln_b.reshape(1, d_a), ws, bs.T, out_g.reshape(1, d_a))


def _cmlp_sample_kernel(u_ref, v_ref, lng_ref, lnb_ref, wv_ref, bv_ref, og_ref, ya_ref, vr_ref, *, n_pos):
    vh = []
    for t in range(n_pos):
        vt = _layer_norm(_gelu(v_ref[t]), lng_ref[0], lnb_ref[0])
        vr_ref[:, t, :] = vt
        vh.append(vt)
    for t in range(n_pos):
        mixed = wv_ref[t, 0:1, :] * vh[0]
        for s in range(1, t + 1):
            mixed = mixed + wv_ref[t, s:s + 1, :] * vh[s]
        mixed = mixed + bv_ref[0, t:t + 1, :]
        y = _gelu(u_ref[t]) * mixed
        ya_ref[t] = _rms_norm(y, og_ref[0]).astype(ya_ref.dtype)


def _chunk_mlp_sample(h3, ln_g, ln_b, ws, bs, out_g):
    n_pos, nb, _ = h3.shape
    n_groups = ws.shape[0]
    d_a = n_groups * GROUP
    wv = jnp.repeat(jnp.transpose(ws[:, :n_pos, :n_pos], (1, 2, 0)), GROUP, axis=-1)
    bv = jnp.repeat(bs[:, :n_pos].T, GROUP, axis=-1).reshape(1, n_pos, d_a)
    vec = pl.BlockSpec((1, 1, GROUP), lambda g: (0, 0, g))
    act = pl.BlockSpec((n_pos, nb, GROUP), lambda g: (0, 0, g))
    return pl.pallas_call(
        functools.partial(_cmlp_sample_kernel, n_pos=n_pos),
        grid=(n_groups,),
        in_specs=[act,
                  pl.BlockSpec((n_pos, nb, GROUP), lambda g: (0, 0, n_groups + g)),
                  vec, vec,
                  pl.BlockSpec((n_pos, n_pos, GROUP), lambda g: (0, 0, g)),
                  pl.BlockSpec((1, n_pos, GROUP), lambda g: (0, 0, g)),
                  vec],
        out_specs=[act, pl.BlockSpec((nb, n_pos, GROUP), lambda g: (0, 0, g))],
        out_shape=[jax.ShapeDtypeStruct((n_pos, nb, d_a), BF16),
                   jax.ShapeDtypeStruct((nb, n_pos, d_a), F32)],
        compiler_params=_params(("arbitrary",)),
        name="chunk_mlp_sample",
    )(h3, h3, ln_g.reshape(1, 1, d_a), ln_b.reshape(1, 1, d_a), wv, bv, out_g.reshape(1, 1, d_a))


def _conv_taps(xbuf_ref, w_ref, tm, shift, halo, width):
    acc = None
    for j in range(width):
        off = halo - (width - 1 - j) * shift
        term = w_ref[j:j + 1, :] * xbuf_ref[off:off + tm, :]
        acc = term if acc is None else acc + term
    return acc


def _qkv_prep_kernel(*refs, tm, shift, halo, n_heads, has_buf):
    if has_buf:
        (xq_ref, xk_ref, xv_ref, bq_ref, bk_ref, bv_ref, wq_ref, wk_ref, wv_ref,
         q_ref, k_ref, v_ref, xbuf_ref, carry_ref) = refs
        bufs = (bq_ref, bk_ref, bv_ref)
    else:
        (xq_ref, xk_ref, xv_ref, wq_ref, wk_ref, wv_ref,
         q_ref, k_ref, v_ref, xbuf_ref, carry_ref, pbuf_ref, pcarry_ref) = refs
        bufs = (None, None, None)
    first = pl.program_id(1) == 0
    parts = ((xq_ref, wq_ref, q_ref, GROUP ** -0.5), (xk_ref, wk_ref, k_ref, 1.0), (xv_ref, wv_ref, v_ref, None))
    for p, (x_ref, w_ref, o_ref, scale) in enumerate(parts):
        @pl.when(first)
        def _(p=p):
            if has_buf:
                xbuf_ref[0:halo, :] = bufs[p][...]
            else:
                xbuf_ref[0:halo, :] = jnp.zeros((halo, xbuf_ref.shape[1]), F32)
                pbuf_ref[0:halo, :] = jnp.zeros((halo, pbuf_ref.shape[1]), F32)

        @pl.when(jnp.logical_not(first))
        def _(p=p):
            xbuf_ref[0:halo, :] = carry_ref[p]
            if not has_buf:
                pbuf_ref[0:halo, :] = pcarry_ref[p]

        xbuf_ref[halo:halo + tm, :] = x_ref[...]
        carry_ref[p] = xbuf_ref[tm:tm + halo, :]
        if has_buf:
            y = _silu(_conv_taps(xbuf_ref, w_ref, tm, shift, halo, CONV_QKV))
        else:
            x0 = xbuf_ref[halo:halo + tm, :]
            x1 = xbuf_ref[halo - 1:halo - 1 + tm, :]
            pbuf_ref[halo:halo + tm, :] = w_ref[1:2, :] * x0 + w_ref[0:1, :] * x1
            pcarry_ref[p] = pbuf_ref[tm:tm + halo, :]
            y = _silu((w_ref[3:4, :] * x0 + w_ref[2:3, :] * x1) + pbuf_ref[halo - 2:halo - 2 + tm, :])
        if scale is None:
            o_ref[...] = y
        else:
            for h in range(n_heads):
                sl = slice(h * GROUP, (h + 1) * GROUP)
                yh = y[:, sl]
                inv = lax.rsqrt(jnp.sum(yh * yh, axis=-1, keepdims=True) + 1e-6)
                o_ref[:, sl] = yh * (inv * scale if scale != 1.0 else inv)


def _qkv_prep(h, conv_w, bufs, n_seq, rows_per_seq, tm, shift, col0):
    d_b = conv_w.shape[1] // 3
    n_heads = d_b // GROUP
    has_buf = bufs is not None
    assert has_buf or (shift == 1 and CONV_QKV == 4), "the fresh-sequence path pairs 4 taps at row shift 1"
    halo = (CONV_QKV - 1) * shift if has_buf else SUBLANES
    tiles = rows_per_seq // tm
    cb = col0 // d_b
    x_specs = [pl.BlockSpec((tm, d_b), lambda s, i, c=c: (s * tiles + i, cb + c)) for c in range(3)]
    w_specs = [pl.BlockSpec((CONV_QKV, d_b), lambda s, i, c=c: (0, c)) for c in range(3)]
    o_spec = pl.BlockSpec((tm, d_b), lambda s, i: (s * tiles + i, 0))
    args, specs = [h, h, h], list(x_specs)
    if has_buf:
        specs += [pl.BlockSpec((halo, d_b), lambda s, i, c=c: (s, c)) for c in range(3)]
        args += [bufs, bufs, bufs]
    specs += w_specs
    args += [conv_w, conv_w, conv_w]
    m = n_seq * rows_per_seq
    return pl.pallas_call(
        functools.partial(_qkv_prep_kernel, tm=tm, shift=shift, halo=halo, n_heads=n_heads, has_buf=has_buf),
        grid=(n_seq, tiles),
        in_specs=specs,
        out_specs=[o_spec, o_spec, o_spec],
        out_shape=[jax.ShapeDtypeStruct((m, d_b), F32)] * 3,
        scratch_shapes=[pltpu.VMEM((halo + tm, d_b), F32), pltpu.VMEM((3, halo, d_b), F32)] * (1 if has_buf else 2),
        compiler_params=_params(("arbitrary", "arbitrary")),
        name="qkv_prep",
    )(*args)


def _gates(ab, alog, dtb):
    g = -jnp.exp(alog) * jax.nn.softplus(ab + dtb)
    beta = jax.nn.sigmoid(ab)
    return g, beta


def _delta_prompt_kernel(q_ref, k_ref, v_ref, z_ref, ab_ref, alog_ref, dtb_ref, ng_ref,
                         yb_ref, sout_ref, s_ref, *, n_heads, n_chunks, ns):
    c = DN_CHUNK
    ci = pl.program_id(1)

    @pl.when(ci == 0)
    def _():
        s_ref[...] = jnp.zeros(s_ref.shape, F32)

    row = lax.broadcasted_iota(jnp.int32, (c, c), 0)
    col = lax.broadcasted_iota(jnp.int32, (c, c), 1)
    causal = row >= col
    strict = row > col
    eye = (row == col).astype(F32)
    tril = causal.astype(F32)
    n_double = (c - 1).bit_length() - 1
    ch = [(s, h) for s in range(ns) for h in range(n_heads)]
    sl = lambda h: slice(h * GROUP, (h + 1) * GROUP)
    gates = [_gates(ab_ref[s], alog_ref[...], dtb_ref[...]) for s in range(ns)]
    gc_s = [jnp.dot(tril, gates[s][0], precision=lax.Precision.HIGHEST, preferred_element_type=F32)
            for s in range(ns)]
    gct_s = [g.T for g in gc_s]
    q = [q_ref[s, :, sl(h)] for s, h in ch]
    k = [k_ref[s, :, sl(h)] for s, h in ch]
    gcol = [gc_s[s][:, h:h + 1] for s, h in ch]
    beta = [gates[s][1][:, n_heads + h:n_heads + h + 1] for s, h in ch]
    n = range(len(ch))
    decay = [jnp.exp(jnp.where(causal, gcol[i] - gct_s[s][h:h + 1, :], -jnp.inf)) for i, (s, h) in enumerate(ch)]
    kb = [k[i] * beta[i] for i in n]
    kbf = [k[i].astype(BF16) for i in n]
    kq = [lax.dot_general(jnp.concatenate([kb[i], q[i]], axis=0).astype(BF16), kbf[i], _NT,
                          preferred_element_type=F32) for i in n]
    qk = [kq[i][c:] * decay[i] for i in n]
    y = [jnp.where(strict, -(kq[i][:c] * decay[i]), 0.0) for i in n]
    x = [eye + y[i] for i in n]
    y = [_bdot(y[i], y[i]) for i in n]
    for _ in range(n_double - 1):
        t = [_bdot(jnp.concatenate([x[i], y[i]], axis=0), y[i]) for i in n]
        x = [x[i] + t[i][:c] for i in n]
        y = [t[i][c:] for i in n]
    x = [x[i] + _bdot(x[i], y[i]) for i in n]
    egc = [jnp.exp(gcol[i]) for i in n]
    sol = [_bdot(x[i], jnp.concatenate([v_ref[s, :, sl(h)] * beta[i], kb[i] * egc[i]], axis=1))
           for i, (s, h) in enumerate(ch)]
    s_old = [s_ref[s, h] for s, h in ch]
    r = [_bdot(jnp.concatenate([sol[i][:, GROUP:], q[i] * egc[i]], axis=0), s_old[i]) for i in n]
    v_new = [sol[i][:, :GROUP] - r[i][:c] for i in n]
    o = [r[i][c:] + _bdot(qk[i], v_new[i]) for i in n]
    gl = [gc_s[s][c - 1:c, h:h + 1] for s, h in ch]
    upd = [lax.dot_general((k[i] * jnp.exp(gl[i] - gcol[i])).astype(BF16), v_new[i].astype(BF16), _TN,
                           preferred_element_type=F32) for i in n]
    ms = [jnp.mean(o[i] * o[i], axis=-1, keepdims=True) for i in n]
    for i, (s, h) in enumerate(ch):
        s_ref[s, h] = s_old[i] * jnp.exp(gl[i]) + upd[i]
        on = o[i] * lax.rsqrt(ms[i] + EPS) * ng_ref[...]
        yb_ref[s, :, sl(h)] = (on * _silu(z_ref[s, :, sl(h)])).astype(yb_ref.dtype)

    @pl.when(ci == n_chunks - 1)
    def _():
        sout_ref[...] = s_ref[...]


def _delta_prompt(q, k, v, h, z_col0, ab, alog, dtb, norm_g, n_seq, rows_per_seq, ns):
    d_b = q.shape[1]
    n_heads = d_b // GROUP
    n_chunks = rows_per_seq // DN_CHUNK
    r3 = lambda t: t.reshape(n_seq, rows_per_seq, t.shape[-1])
    act = pl.BlockSpec((ns, DN_CHUNK, d_b), lambda s, i: (s, i, 0))
    vec = pl.BlockSpec((1, LANES), lambda s, i: (0, 0))
    zb = z_col0 // d_b
    yb, s_new = pl.pallas_call(
        functools.partial(_delta_prompt_kernel, n_heads=n_heads, n_chunks=n_chunks, ns=ns),
        grid=(n_seq // ns, n_chunks),
        in_specs=[act, act, act,
                  pl.BlockSpec((ns, DN_CHUNK, d_b), lambda s, i: (s, i, zb)),
                  pl.BlockSpec((ns, DN_CHUNK, LANES), lambda s, i: (s, i, 0)),
                  vec, vec, vec],
        out_specs=[act, pl.BlockSpec((ns, n_heads, GROUP, GROUP), lambda s, i: (s, 0, 0, 0))],
        out_shape=[jax.ShapeDtypeStruct((n_seq, rows_per_seq, d_b), BF16),
                   jax.ShapeDtypeStruct((n_seq, n_heads, GROUP, GROUP), F32)],
        scratch_shapes=[pltpu.VMEM((ns, n_heads, GROUP, GROUP), F32)],
        compiler_params=_params(("arbitrary", "arbitrary")),
        name="delta_prompt",
    )(r3(q), r3(k), r3(v), r3(h), r3(ab), alog, dtb, norm_g)
    return yb.reshape(n_seq * rows_per_seq, d_b), s_new


def _delta_sample_kernel(q_ref, k_ref, v_ref, z_ref, ab_ref, alog_ref, dtb_ref, ng_ref, s0_ref,
                         yb_ref, s1_ref, lhs_ref, res_ref, kt_ref, vn_ref, gl_ref, *, n_heads, n_pos, bb):
    nt = n_pos
    hs = range(n_heads)
    sls = [slice(h * GROUP, (h + 1) * GROUP) for h in hs]
    gates = [_gates(ab_ref[t], alog_ref[...], dtb_ref[...]) for t in range(nt)]
    zeros_pad = jnp.zeros((bb, GROUP), F32)
    ts = range(nt)
    q = [[q_ref[t, :, sls[h]] for t in ts] for h in hs]
    k = [[k_ref[t, :, sls[h]] for t in ts] for h in hs]
    beta = [[gates[t][1][:, n_heads + h:n_heads + h + 1] for t in ts] for h in hs]
    gc = []
    for h in hs:
        g = [gates[t][0][:, h:h + 1] for t in ts]
        for t in range(1, nt):
            g[t] = g[t - 1] + g[t]
        gc.append(g)
    kb = [[k[h][t] * beta[h][t] for t in ts] for h in hs]
    kk = [[[jnp.sum(kb[h][t] * k[h][s], axis=-1, keepdims=True) for s in range(t)] for t in ts] for h in hs]
    qk = [[[jnp.sum(q[h][t] * k[h][s], axis=-1, keepdims=True) for s in range(t + 1)] for t in ts] for h in hs]
    dec = [[[jnp.exp(gc[h][t] - gc[h][s]) for s in range(t + 1)] for t in ts] for h in hs]
    a = [[[kk[h][t][s] * dec[h][t][s] for s in range(t)] for t in ts] for h in hs]
    qk = [[[qk[h][t][s] * dec[h][t][s] for s in range(t + 1)] for t in ts] for h in hs]
    egc = [[jnp.exp(gc[h][t]) for t in ts] for h in hs]
    u_blk = [[v_ref[t, :, sls[h]] * beta[h][t] for t in ts] for h in hs]
    w_blk = [[kb[h][t] * egc[h][t] for t in ts] for h in hs]
    for t in ts:
        for h in hs:
            for s in range(t):
                u_blk[h][t] = u_blk[h][t] - a[h][t][s] * u_blk[h][s]
                w_blk[h][t] = w_blk[h][t] - a[h][t][s] * w_blk[h][s]
    for h in hs:
        for t in ts:
            lhs_ref[h, :, t, :] = w_blk[h][t]
            lhs_ref[h, :, nt + t, :] = q[h][t] * egc[h][t]
        gl_ref[h] = jnp.broadcast_to(egc[h][nt - 1], (bb, GROUP))

    for h in hs:
        for b in range(bb):
            res_ref[h, b] = _bdot(lhs_ref[h, b], s0_ref[b, h])

    v_new = [[u_blk[h][t] - res_ref[h, :, t, :] for t in ts] for h in hs]
    o = [[res_ref[h, :, nt + t, :] for t in ts] for h in hs]
    for h in hs:
        for t in ts:
            for s in range(t + 1):
                o[h][t] = o[h][t] + qk[h][t][s] * v_new[h][s]
    ms = [[jnp.mean(o[h][t] * o[h][t], axis=-1, keepdims=True) for t in ts] for h in hs]
    for h in hs:
        gl = gc[h][nt - 1]
        for t in ts:
            on = o[h][t] * lax.rsqrt(ms[h][t] + EPS) * ng_ref[...]
            yb_ref[t, :, sls[h]] = (on * _silu(z_ref[t, :, sls[h]])).astype(yb_ref.dtype)
            kt_ref[h, :, t, :] = k[h][t] * jnp.exp(gl - gc[h][t])
            vn_ref[h, :, t, :] = v_new[h][t]
            kt_ref[h, :, nt + t, :] = zeros_pad
            vn_ref[h, :, nt + t, :] = zeros_pad

    for h in hs:
        for b in range(bb):
            upd = lax.dot_general(kt_ref[h, b].astype(BF16), vn_ref[h, b].astype(BF16), _TN,
                                  preferred_element_type=F32)
            s1_ref[b, h] = s0_ref[b, h] * gl_ref[h, b:b + 1, :] + upd


def _delta_sample(q, k, v, h3, z_col0, ab, alog, dtb, norm_g, s0, bb):
    n_pos, nb, d_b = q.shape
    n_heads = d_b // GROUP
    assert 2 * n_pos == SUBLANES, "one state read packs [w | q] rows of a sequence into one 8-row tile"
    zb = z_col0 // d_b
    act = pl.BlockSpec((n_pos, bb, d_b), lambda i: (0, i, 0))
    vec = pl.BlockSpec((1, LANES), lambda i: (0, 0))
    st = pl.BlockSpec((bb, n_heads, GROUP, GROUP), lambda i: (i, 0, 0, 0))
    tile = pltpu.VMEM((n_heads, bb, SUBLANES, GROUP), F32)
    return pl.pallas_call(
        functools.partial(_delta_sample_kernel, n_heads=n_heads, n_pos=n_pos, bb=bb),
        grid=(nb // bb,),
        in_specs=[act, act, act,
                  pl.BlockSpec((n_pos, bb, d_b), lambda i: (0, i, zb)),
                  pl.BlockSpec((n_pos, bb, LANES), lambda i: (0, i, 0)),
                  vec, vec, vec, st],
        out_specs=[act, st],
        out_shape=[jax.ShapeDtypeStruct((n_pos, nb, d_b), BF16),
                   jax.ShapeDtypeStruct(s0.shape, F32)],
        scratch_shapes=[tile, tile, tile, tile, pltpu.VMEM((n_heads, bb, GROUP), F32)],
        compiler_params=_params(("arbitrary",)),
        name="delta_sample",
    )(q, k, v, h3, ab, alog, dtb, norm_g, s0)


def _out_proj_kernel(ya_ref, yb_ref, w_ref, r_ref, g_ref, b_ref, o_ref, ob_ref, z_ref, *, n_tiles, d_a, alpha):
    i = pl.program_id(0)

    @pl.when(i == 0)
    def _():
        z_ref[...] = jnp.zeros(z_ref.shape, F32)

    def emit_ln():
        y = _layer_norm(z_ref[...], g_ref[...], b_ref[...])
        o_ref[...] = y
        ob_ref[...] = y.astype(ob_ref.dtype)

    @pl.when(i < n_tiles)
    def _():
        emit_ln()
        acc = (jnp.dot(ya_ref[...], w_ref[0:d_a, :], preferred_element_type=F32)
               + jnp.dot(yb_ref[...], w_ref[d_a:, :], preferred_element_type=F32))
        z_ref[...] = alpha * r_ref[...] + acc

    @pl.when(i == n_tiles)
    def _():
        emit_ln()


def _out_proj_ln(ya, yb, w, resid, g, b, alpha, tm):
    m, d_a = ya.shape
    k, d = w.shape
    n_tiles = m // tm
    cur = lambda i: (jnp.minimum(i, n_tiles - 1), 0)
    prev = lambda i: (jnp.maximum(i - 1, 0), 0)
    vec = pl.BlockSpec((1, d), lambda i: (0, 0))
    return pl.pallas_call(
        functools.partial(_out_proj_kernel, n_tiles=n_tiles, d_a=d_a, alpha=alpha),
        grid=(n_tiles + 1,),
        in_specs=[pl.BlockSpec((tm, d_a), cur),
                  pl.BlockSpec((tm, k - d_a), cur),
                  pl.BlockSpec((k, d), lambda i: (0, 0), pipeline_mode=pl.Buffered(1)),
                  pl.BlockSpec((tm, d), cur), vec, vec],
        out_specs=[pl.BlockSpec((tm, d), prev), pl.BlockSpec((tm, d), prev)],
        out_shape=[jax.ShapeDtypeStruct((m, d), F32), jax.ShapeDtypeStruct((m, d), BF16)],
        scratch_shapes=[pltpu.VMEM((tm, d), F32)],
        compiler_params=_params(("arbitrary",)),
        name="out_proj_ln1",
    )(ya, yb, w, resid, g.reshape(1, d), b.reshape(1, d))


def _down_kernel(x_ref, w_ref, r_ref, g_ref, b_ref, o_ref, *rest, n_tiles, nk, nc, alpha, n_pos):
    z_ref = rest[-1]
    cw = z_ref.shape[-1]
    slab = lambda c: slice(c * cw, (c + 1) * cw)
    i, kk = pl.program_id(0), pl.program_id(1)

    def emit_ln():
        z = [z_ref[c] for c in range(nc)]
        inv_d = 1.0 / (nc * cw)
        mu = sum(jnp.sum(zc, axis=-1, keepdims=True) for zc in z) * inv_d
        xc = [zc - mu for zc in z]
        var = sum(jnp.sum(c * c, axis=-1, keepdims=True) for c in xc) * inv_d
        rs = lax.rsqrt(var + EPS)
        for c in range(nc):
            y = xc[c] * rs * g_ref[:, slab(c)] + b_ref[:, slab(c)]
            if n_pos:
                nb = y.shape[0] // n_pos
                for t in range(n_pos):
                    o_ref[:, t, slab(c)] = y[t * nb:(t + 1) * nb]
            else:
                o_ref[:, slab(c)] = y

    @pl.when(jnp.logical_and(i == 0, kk == 0))
    def _():
        z_ref[...] = jnp.zeros(z_ref.shape, F32)

    wb = w_ref[...].astype(BF16)
    if len(rest) == 2:
        rest[0][...] = wb

    @pl.when(kk == 0)
    def _():
        emit_ln()
        part = jnp.dot(x_ref[...], wb, preferred_element_type=F32)
        for c in range(nc):
            if nc == nk:
                z_ref[c] = part[:, slab(c)] + alpha * r_ref[...] if c == 0 else part[:, slab(c)]
            else:
                z_ref[c] = part[:, slab(c)] + alpha * r_ref[:, slab(c)]

    @pl.when(jnp.logical_and(kk > 0, i < n_tiles))
    def _():
        part = jnp.dot(x_ref[...], wb, preferred_element_type=F32)
        for c in range(nc):
            z_ref[c] += part[:, slab(c)]
        if nc == nk:
            z_ref[kk] += alpha * r_ref[...]


def _down_resident_kernel(x_ref, w_ref, r_ref, g_ref, b_ref, o_ref, z_ref, *, n_tiles, alpha):
    i = pl.program_id(0)

    @pl.when(i == 0)
    def _():
        z_ref[...] = jnp.zeros(z_ref.shape, F32)

    @pl.when(i < n_tiles)
    def _():
        o_ref[...] = _layer_norm(z_ref[...], g_ref[...], b_ref[...])
        z_ref[...] = alpha * r_ref[...] + jnp.dot(x_ref[...], w_ref[...], preferred_element_type=F32)

    @pl.when(i == n_tiles)
    def _():
        o_ref[...] = _layer_norm(z_ref[...], g_ref[...], b_ref[...])


def _down_ln_resident(x, w, resid, g, b, alpha, tm):
    m, kdim = x.shape
    d = w.shape[1]
    n_tiles = m // tm
    cur = lambda i: (jnp.minimum(i, n_tiles - 1), 0)
    vec = pl.BlockSpec((1, d), lambda i: (0, 0))
    return pl.pallas_call(
        functools.partial(_down_resident_kernel, n_tiles=n_tiles, alpha=alpha),
        grid=(n_tiles + 1,),
        in_specs=[pl.BlockSpec((tm, kdim), cur),
                  pl.BlockSpec((kdim, d), lambda i: (0, 0), pipeline_mode=pl.Buffered(1)),
                  pl.BlockSpec((tm, d), cur), vec, vec],
        out_specs=pl.BlockSpec((tm, d), lambda i: (jnp.maximum(i - 1, 0), 0)),
        out_shape=jax.ShapeDtypeStruct((m, d), F32),
        scratch_shapes=[pltpu.VMEM((tm, d), F32)],
        compiler_params=_params(("arbitrary",), BIG_VMEM_LIMIT),
        name="down_ln2_resident",
    )(x, w, resid, g.reshape(1, d), b.reshape(1, d))


def _down_ln(x, w, resid, g, b, alpha, tm, tk, emit_w=False, n_pos=0):
    m, kdim = x.shape
    d = w.shape[1]
    nk = kdim // tk
    n_tiles = m // tm
    nc = nk if (d % nk == 0 and (d // nk) % LANES == 0) else 1
    row = lambda i: jnp.minimum(i, n_tiles - 1)
    kt = lambda i, kk: jnp.where(i == n_tiles, nk - 1, kk)
    vec = pl.BlockSpec((1, d), lambda i, kk: (0, 0))
    wspec = pl.BlockSpec((tk, d), lambda i, kk: (kt(i, kk), 0))
    if nc == nk:
        rspec = pl.BlockSpec((tm, d // nc), lambda i, kk: (row(i), kt(i, kk)))
    else:
        rspec = pl.BlockSpec((tm, d), lambda i, kk: (row(i), 0))
    if n_pos:
        assert n_tiles == 1
        out_specs = [pl.BlockSpec((m // n_pos, n_pos, d), lambda i, kk: (0, 0, 0))]
        out_shape = [jax.ShapeDtypeStruct((m // n_pos, n_pos, d), F32)]
    else:
        out_specs = [pl.BlockSpec((tm, d), lambda i, kk: (jnp.where(kk == 0, jnp.maximum(i - 1, 0), row(i)), 0))]
        out_shape = [jax.ShapeDtypeStruct((m, d), F32)]
    if emit_w:
        out_specs.append(wspec)
        out_shape.append(jax.ShapeDtypeStruct(w.shape, BF16))
    outs = pl.pallas_call(
        functools.partial(_down_kernel, n_tiles=n_tiles, nk=nk, nc=nc, alpha=alpha, n_pos=n_pos),
        grid=(n_tiles + 1, nk),
        in_specs=[pl.BlockSpec((tm, tk), lambda i, kk: (row(i), kt(i, kk))), wspec, rspec, vec, vec],
        out_specs=out_specs,
        out_shape=out_shape,
        scratch_shapes=[pltpu.VMEM((nc, tm, d // nc), F32)],
        compiler_params=_params(("arbitrary", "arbitrary"), BIG_VMEM_LIMIT),
        name="down_ln2",
    )(x, w, resid, g.reshape(1, d), b.reshape(1, d))
    return tuple(outs) if emit_w else outs[0]


def _ffn_up_kernel(x_ref, wg_ref, wv_ref, cwg_ref, cwv_ref, cbg_ref, cbv_ref,
                   act_ref, tg_ref, tv_ref, wgo_ref, wvo_ref, wgb_ref, wvb_ref, carry_ref, *, tm):
    halo = SUBLANES
    first = pl.program_id(2) == 0

    @pl.when(jnp.logical_and(pl.program_id(1) == 0, first))
    def _():
        for w_ref, wb_ref, wo_ref in ((wg_ref, wgb_ref, wgo_ref), (wv_ref, wvb_ref, wvo_ref)):
            wb = w_ref[...].astype(BF16)
            wb_ref[...] = wb
            wo_ref[...] = wb

    @pl.when(first)
    def _():
        carry_ref[...] = jnp.zeros(carry_ref.shape, F32)

    x = x_ref[...]
    conv = []
    parts = ((wgb_ref, cwg_ref, cbg_ref, tg_ref), (wvb_ref, cwv_ref, cbv_ref, tv_ref))
    row = lax.broadcasted_iota(jnp.int32, (halo, act_ref.shape[1]), 0)
    for p, (w_ref, cw_ref, cb_ref, t_ref) in enumerate(parts):
        up = jnp.dot(x, w_ref[...], preferred_element_type=F32)
        hist = carry_ref[p]
        tail = up[tm - halo:tm, :]
        carry_ref[p] = tail
        t_ref[0] = tail
        acc = cw_ref[CONV_FFN - 1:CONV_FFN, :] * up
        for dd in range(1, CONV_FFN):
            xs = pltpu.roll(up, dd, axis=0)
            head = jnp.where(row < dd, pltpu.roll(hist, dd, axis=0), xs[0:halo])
            xs = jnp.concatenate([head, xs[halo:]], axis=0)
            acc = acc + cw_ref[CONV_FFN - 1 - dd:CONV_FFN - dd, :] * xs
        conv.append(acc + cb_ref[...])
    act_ref[...] = (_silu(conv[0]) * conv[1]).astype(act_ref.dtype)


def _ffn_up_carried_kernel(x_ref, wg_ref, wv_ref, buf_ref, cw_ref, cb_ref, act_ref, t_ref, xbuf_ref, gate_ref,
                           *, shift):
    p, j = pl.program_id(0), pl.program_id(1)
    tm, halo = x_ref.shape[0], buf_ref.shape[0]
    xbuf_ref[0:halo, :] = buf_ref[...]

    def conv_of(w_ref):
        xbuf_ref[halo:halo + tm, :] = jnp.dot(x_ref[...], w_ref[...], preferred_element_type=F32)
        tail = xbuf_ref[tm:tm + halo, :]
        for t in range(halo // shift):
            t_ref[:, t, :] = tail[t * shift:(t + 1) * shift]
        return _conv_taps(xbuf_ref, cw_ref, tm, shift, halo, CONV_FFN) + cb_ref[...]

    @pl.when(p == 0)
    def _():
        gate_ref[j] = conv_of(wg_ref)

    @pl.when(p == 1)
    def _():
        act_ref[...] = (_silu(gate_ref[j]) * conv_of(wv_ref)).astype(act_ref.dtype)


def _ffn_up_carried(x, w_gate_b, w_val_b, conv_w, conv_b, bufs, tn, shift):
    m, d = x.shape
    d_ff = w_gate_b.shape[1]
    nj = d_ff // tn
    halo = (CONV_FFN - 1) * shift
    col = lambda p, j: (0, p * nj + j)
    return pl.pallas_call(
        functools.partial(_ffn_up_carried_kernel, shift=shift),
        grid=(2, nj),
        in_specs=[pl.BlockSpec((m, d), lambda p, j: (0, 0)),
                  pl.BlockSpec((d, tn), lambda p, j: (0, jnp.where(p == 0, j, nj - 1))),
                  pl.BlockSpec((d, tn), lambda p, j: (0, jnp.where(p == 1, j, 0))),
                  pl.BlockSpec((halo, tn), col),
                  pl.BlockSpec((CONV_FFN, tn), col),
                  pl.BlockSpec((1, tn), col)],
        out_specs=[pl.BlockSpec((m, tn), lambda p, j: (0, jnp.where(p == 1, j, 0))),
                   pl.BlockSpec((shift, halo // shift, tn), lambda p, j: (0, 0, p * nj + j))],
        out_shape=[jax.ShapeDtypeStruct((m, d_ff), BF16),
                   jax.ShapeDtypeStruct((shift, halo // shift, 2 * d_ff), F32)],
        scratch_shapes=[pltpu.VMEM((halo + m, tn), F32), pltpu.VMEM((nj, m, tn), F32)],
        compiler_params=_params(("arbitrary", "arbitrary")),
        name="ffn_up_carried",
    )(x, w_gate_b, w_val_b, bufs, conv_w, conv_b.reshape(1, 2 * d_ff))


def _ffn_up(x, w_up, conv_w, conv_b, n_seq, rows_per_seq, tm, tn):
    m, d = x.shape
    d_ff = w_up.shape[1] // 2
    nj = d_ff // tn
    halo = SUBLANES
    tiles = rows_per_seq // tm
    wspec = lambda off: pl.BlockSpec((d, tn), lambda j, s, i: (0, off + j))
    cwspec = lambda off: pl.BlockSpec((CONV_FFN, tn), lambda j, s, i: (0, off + j))
    cbspec = lambda off: pl.BlockSpec((1, tn), lambda j, s, i: (0, off + j))
    tspec = pl.BlockSpec((1, halo, tn), lambda j, s, i: (s, 0, j))
    tshape = jax.ShapeDtypeStruct((n_seq, halo, d_ff), F32)
    cb2 = conv_b.reshape(1, 2 * d_ff)
    return pl.pallas_call(
        functools.partial(_ffn_up_kernel, tm=tm),
        grid=(nj, n_seq, tiles),
        in_specs=[pl.BlockSpec((tm, d), lambda j, s, i: (s * tiles + i, 0)), wspec(0), wspec(nj),
                  cwspec(0), cwspec(nj), cbspec(0), cbspec(nj)],
        out_specs=[pl.BlockSpec((tm, tn), lambda j, s, i: (s * tiles + i, j)), tspec, tspec, wspec(0), wspec(0)],
        out_shape=[jax.ShapeDtypeStruct((m, d_ff), BF16), tshape, tshape]
                  + [jax.ShapeDtypeStruct((d, d_ff), BF16)] * 2,
        scratch_shapes=[pltpu.VMEM((d, tn), BF16)] * 2 + [pltpu.VMEM((2, halo, tn), F32)],
        compiler_params=_params(("arbitrary", "arbitrary", "arbitrary")),
        name="ffn_up",
    )(x, w_up, w_up, conv_w, conv_w, cb2, cb2)


def _pad_lanes(vec, offset=0):
    out = jnp.zeros((1, LANES), F32)
    return out.at[0, offset:offset + vec.shape[0]].set(vec.astype(F32))


def _mixer_half(x2, lw, alpha, n_seq, rows_per_seq, shift, state, w_in_b):
    (w_in, conv_qkv_w, a_log, dt_bias, norm_g, gm_ln_g, gm_ln_b, gm_ws, gm_bs, gm_out_g, w_out,
     ln1_g, ln1_b) = lw
    m, d_model = x2.shape
    n_groups = gm_ws.shape[0]
    d_a = n_groups * GROUP
    d_qkv = conv_qkv_w.shape[1]
    d_b = d_qkv // 3
    n_heads = d_b // GROUP
    main_cols = 2 * d_a + d_qkv + d_b
    w_ab = jnp.pad(w_in[main_cols:].astype(BF16), ((0, LANES - 2 * n_heads), (0, 0)))
    if w_in_b is None:
        h, ab, w_in_b = _matmul(x2, w_in, w_ab, main_cols, min(m, PROJ_TM), PROJ_EMIT_TN, emit_w=True)
    else:
        h, ab = _matmul(x2, w_in_b, w_ab, main_cols, min(m, PROJ_TM), PROJ_TN)
    alog, dtb, ng = _pad_lanes(a_log), _pad_lanes(dt_bias), norm_g.reshape(1, GROUP).astype(F32)

    if state is None:
        ya = _chunk_mlp_prompt(h, gm_ln_g, gm_ln_b, gm_ws, gm_bs, gm_out_g)
        v_rows = None
        q, k, v = _qkv_prep(h, conv_qkv_w, None, n_seq, rows_per_seq, QKV_TM, shift, 2 * d_a)
        yb, s_new = _delta_prompt(q, k, v, h, 2 * d_a + d_qkv, ab, alog, dtb, ng, n_seq, rows_per_seq, DELTA_SEQS)
    else:
        s_dn, buf_qkv = state
        n_pos = rows_per_seq // shift
        h3 = h.reshape(n_pos, shift, main_cols)
        ya3, v_rows = _chunk_mlp_sample(h3, gm_ln_g, gm_ln_b, gm_ws, gm_bs, gm_out_g)
        ya = ya3.reshape(m, d_a)
        q, k, v = _qkv_prep(h, conv_qkv_w, buf_qkv, n_seq, rows_per_seq, rows_per_seq, shift, 2 * d_a)
        r3 = lambda t: t.reshape(n_pos, shift, t.shape[-1])
        yb3, s_new = _delta_sample(r3(q), r3(k), r3(v), h3, 2 * d_a + d_qkv, r3(ab), alog, dtb, ng, s_dn,
                                   DELTA_SAMPLE_ROWS)
        yb = yb3.reshape(m, d_b)

    x1, x1b = _out_proj_ln(ya, yb, w_out, x2, ln1_g, ln1_b, alpha, min(m, OUT_TM))
    return x1, x1b, s_new, h, v_rows, w_in_b


def kernel(x_prompt, x_sample, state_dn, state_conv_qkv, state_conv_ffn, w_in, conv_qkv_w, dn_a_log, dn_dt_bias, dn_norm_g, gm_ln_g, gm_ln_b, gm_ws, gm_bs, gm_out_g, w_out, ln1_g, ln1_b, w_up, conv_ffn_w, conv_ffn_b, w_down, ln2_g, ln2_b):
    depth = w_in.shape[0]
    bp, lp, d_model = x_prompt.shape
    bs, ls, _ = x_sample.shape
    d_qkv = conv_qkv_w.shape[-1]
    d_a = gm_ws.shape[1] * GROUP
    alpha = (2.0 * depth) ** 0.25

    hp = x_prompt.reshape(bp * lp, d_model)
    hs = jnp.transpose(x_sample, (1, 0, 2)).reshape(ls * bs, d_model)
    outs = [[] for _ in range(7)]
    w_out = w_out.astype(BF16)
    w_in = jnp.swapaxes(w_in, 1, 2)
    for l in range(depth):
        lw = tuple(t[l] for t in (w_in, conv_qkv_w, dn_a_log, dn_dt_bias, dn_norm_g, gm_ln_g, gm_ln_b, gm_ws,
                                  gm_bs, gm_out_g, w_out, ln1_g, ln1_b))
        ffn_w = (conv_ffn_w[l], conv_ffn_b[l])
        ln2 = (ln2_g[l], ln2_b[l], alpha)
        n_pos = ls if l == depth - 1 else 0
        bq = jnp.transpose(state_conv_qkv[l], (1, 0, 2)).reshape((CONV_QKV - 1) * bs, d_qkv)
        bf = jnp.transpose(state_conv_ffn[l], (1, 0, 2)).reshape((CONV_FFN - 1) * bs, -1)
        x1_s, x1b_s, s_s, h_s, v_s, w_in_b = _mixer_half(hs, lw, alpha, 1, ls * bs, bs, (state_dn[l], bq), None)
        x1_p, x1b_p, s_p, h_p, _, _ = _mixer_half(hp, lw, alpha, bp, lp, 1, None, w_in_b)
        act_p, tg_p, tv_p, w_gate_b, w_val_b = _ffn_up(x1b_p, w_up[l], *ffn_w, bp, lp, FFN_TM, FFN_TN)
        act_s, cf_s = _ffn_up_carried(x1b_s, w_gate_b, w_val_b, *ffn_w, bf, FFN_TN, bs)
        hs, w_down_b = _down_ln(act_s, w_down[l], x1_s, *ln2, min(ls * bs, DOWN_EMIT_TM), DOWN_EMIT_TK,
                                emit_w=True, n_pos=n_pos)
        hp = _down_ln_resident(act_p, w_down_b, x1_p, *ln2, min(bp * lp, DOWN_TM))
        h_s3 = h_s.reshape(ls, bs, -1)
        cq_s = jnp.transpose(h_s3[ls - (CONV_QKV - 1):, :, 2 * d_a:2 * d_a + d_qkv], (1, 0, 2))
        h_p3 = h_p.reshape(bp, lp, -1)
        cq_p = h_p3[:, lp - (CONV_QKV - 1):, 2 * d_a:2 * d_a + d_qkv]
        n_tail = tg_p.shape[1]
        cf_p = jnp.concatenate([tg_p, tv_p], axis=-1)[:, n_tail - (CONV_FFN - 1):]
        for acc, val in zip(outs, (s_p, cq_p, cf_p, s_s, cq_s, cf_s, v_s)):
            acc.append(val)
    return (hp.reshape(bp, lp, d_model), hs) + tuple(jnp.stack(o) for o in outs)
```

```python
import functools

import jax
import jax.numpy as jnp
from jax import lax
from jax.experimental import pallas as pl
from jax.experimental.pallas import tpu as pltpu

F32 = jnp.float32
BF16 = jnp.bfloat16

LANES = 128
SUBLANES = 8
VMEM_LIMIT = 56 * 1024 * 1024
BIG_VMEM_LIMIT = 62 * 1024 * 1024

GROUP = 128
MLP_CHUNK = 128
DN_CHUNK = 64
CONV_QKV = 4
CONV_FFN = 3
CMLP_CHUNKS = 8
EPS = 1e-5

PROJ_TM, PROJ_TN = 1024, 1024
PROJ_EMIT_TN = 1536
QKV_TM = 512
DELTA_SEQS = 4
DELTA_SAMPLE_ROWS = 16
OUT_TM = 512
FFN_TM, FFN_TN = 1024, 512
DOWN_TM = 512
DOWN_EMIT_TM, DOWN_EMIT_TK = 512, 1408

_NT = (((1,), (1,)), ((), ()))
_TN = (((0,), (0,)), ((), ()))


def _params(sem, vmem_limit=VMEM_LIMIT):
    return pltpu.CompilerParams(dimension_semantics=sem, vmem_limit_bytes=vmem_limit)


def _bdot(a, b):
    return jnp.dot(a.astype(BF16), b.astype(BF16), preferred_element_type=F32)


def _gelu(x):
    return 0.5 * x * (1.0 + lax.erf(x * (0.5 ** 0.5)))


def _silu(x):
    return x * jax.nn.sigmoid(x)


def _layer_norm(x, g, b):
    mu = jnp.mean(x, axis=-1, keepdims=True)
    xc = x - mu
    var = jnp.mean(xc * xc, axis=-1, keepdims=True)
    return xc * lax.rsqrt(var + EPS) * g + b


def _rms_norm(x, g):
    ms = jnp.mean(x * x, axis=-1, keepdims=True)
    return x * lax.rsqrt(ms + EPS) * g


def _mm_kernel(x_ref, w_ref, we_ref, o_ref, e_ref, *rest, tn, resident):
    xb_ref = rest[-1]
    j = pl.program_id(1)

    @pl.when(j == 0)
    def _():
        xb = x_ref[...].astype(BF16)
        xb_ref[...] = xb
        e_ref[...] = lax.dot_general(xb, we_ref[...], _NT, preferred_element_type=F32)

    if resident:
        wb = w_ref[pl.ds(pl.multiple_of(j * tn, tn), tn), :]
    else:
        wb = w_ref[...].astype(BF16)
    if len(rest) == 2:
        rest[0][...] = wb
    o_ref[...] = lax.dot_general(xb_ref[...], wb, _NT, preferred_element_type=F32)


def _matmul(x, wt, wt_extra, n_cols, tm, tn, emit_w=False):
    m, k = x.shape
    ne = wt_extra.shape[0]
    out_specs = [pl.BlockSpec((tm, tn), lambda i, j: (i, j)), pl.BlockSpec((tm, ne), lambda i, j: (i, 0))]
    out_shape = [jax.ShapeDtypeStruct((m, n_cols), F32), jax.ShapeDtypeStruct((m, ne), F32)]
    if emit_w:
        out_specs.append(pl.BlockSpec((tn, k), lambda i, j: (j, 0)))
        out_shape.append(jax.ShapeDtypeStruct((n_cols, k), BF16))
        wspec = pl.BlockSpec((tn, k), lambda i, j: (j, 0))
    else:
        assert wt.shape == (n_cols, k) and wt.dtype == BF16
        wspec = pl.BlockSpec((n_cols, k), lambda i, j: (0, 0), pipeline_mode=pl.Buffered(1))
    return pl.pallas_call(
        functools.partial(_mm_kernel, tn=tn, resident=not emit_w),
        grid=(m // tm, n_cols // tn),
        in_specs=[pl.BlockSpec((tm, k), lambda i, j: (i, 0)),
                  wspec,
                  pl.BlockSpec((ne, k), lambda i, j: (0, 0))],
        out_specs=out_specs,
        out_shape=out_shape,
        scratch_shapes=[pltpu.VMEM((tm, k), BF16)],
        compiler_params=_params(("arbitrary", "arbitrary"), BIG_VMEM_LIMIT),
        name="proj",
    )(x, wt, wt_extra)


def _cmlp_prompt_kernel(u_ref, v_ref, lng_ref, lnb_ref, ws_ref, bst_ref, og_ref, ya_ref, *, n_groups, n_chunks):
    row = lax.broadcasted_iota(jnp.int32, (MLP_CHUNK, MLP_CHUNK), 0)
    col = lax.broadcasted_iota(jnp.int32, (MLP_CHUNK, MLP_CHUNK), 1)
    cg = [(c, g) for c in range(n_chunks) for g in range(n_groups)]
    rows = lambda c: slice(c * MLP_CHUNK, (c + 1) * MLP_CHUNK)
    cols = lambda g: slice(g * GROUP, (g + 1) * GROUP)
    w = [jnp.where(row >= col, ws_ref[g], 0.0).astype(BF16) for g in range(n_groups)]
    n = range(len(cg))
    gv = [_gelu(v_ref[rows(c), cols(g)]) for c, g in cg]
    mu = [jnp.mean(gv[i], axis=-1, keepdims=True) for i in n]
    xc = [gv[i] - mu[i] for i in n]
    var = [jnp.mean(xc[i] * xc[i], axis=-1, keepdims=True) for i in n]
    vh = [xc[i] * lax.rsqrt(var[i] + EPS) * lng_ref[:, cols(g)] + lnb_ref[:, cols(g)] for i, (c, g) in enumerate(cg)]
    mixed = [jnp.dot(w[g], vh[i].astype(BF16), preferred_element_type=F32) + bst_ref[:, g:g + 1]
             for i, (c, g) in enumerate(cg)]
    y = [_gelu(u_ref[rows(c), cols(g)]) * mixed[i] for i, (c, g) in enumerate(cg)]
    ms = [jnp.mean(y[i] * y[i], axis=-1, keepdims=True) for i in n]
    for i, (c, g) in enumerate(cg):
        ya_ref[rows(c), cols(g)] = (y[i] * lax.rsqrt(ms[i] + EPS) * og_ref[:, cols(g)]).astype(ya_ref.dtype)


def _chunk_mlp_prompt(h, ln_g, ln_b, ws, bs, out_g):
    m = h.shape[0]
    n_groups, d_a = ws.shape[0], ws.shape[0] * GROUP
    vec = pl.BlockSpec((1, d_a), lambda i: (0, 0))
    tm = CMLP_CHUNKS * MLP_CHUNK
    return pl.pallas_call(
        functools.partial(_cmlp_prompt_kernel, n_groups=n_groups, n_chunks=CMLP_CHUNKS),
        grid=(m // tm,),
        in_specs=[pl.BlockSpec((tm, d_a), lambda i: (i, 0)),
                  pl.BlockSpec((tm, d_a), lambda i: (i, 1)),
                  vec, vec,
                  pl.BlockSpec((n_groups, MLP_CHUNK, MLP_CHUNK), lambda i: (0, 0, 0)),
                  pl.BlockSpec((MLP_CHUNK, n_groups), lambda i: (0, 0)),
                  vec],
        out_specs=pl.BlockSpec((tm, d_a), lambda i: (i, 0)),
        out_shape=jax.ShapeDtypeStruct((m, d_a), BF16),
        compiler_params=_params(("arbitrary",)),
        name="chunk_mlp_prompt",
    )(h, h, ln_g.reshape(1, d_a), ln_b.reshape(1, d_a), ws, bs.T, out_g.reshape(1, d_a))


def _cmlp_sample_kernel(u_ref, v_ref, lng_ref, lnb_ref, wv_ref, bv_ref, og_ref, ya_ref, vr_ref, *, n_pos):
    vh = []
    for t in range(n_pos):
        vt = _layer_norm(_gelu(v_ref[t]), lng_ref[0], lnb_ref[0])
        vr_ref[:, t, :] = vt
        vh.append(vt)
    for t in range(n_pos):
        mixed = wv_ref[t, 0:1, :] * vh[0]
        for s in range(1, t + 1):
            mixed = mixed + wv_ref[t, s:s + 1, :] * vh[s]
        mixed = mixed + bv_ref[0, t:t + 1, :]
        y = _gelu(u_ref[t]) * mixed
        ya_ref[t] = _rms_norm(y, og_ref[0]).astype(ya_ref.dtype)


def _chunk_mlp_sample(h3, ln_g, ln_b, ws, bs, out_g):
    n_pos, nb, _ = h3.shape
    n_groups = ws.shape[0]
    d_a = n_groups * GROUP
    wv = jnp.repeat(jnp.transpose(ws[:, :n_pos, :n_pos], (1, 2, 0)), GROUP, axis=-1)
    bv = jnp.repeat(bs[:, :n_pos].T, GROUP, axis=-1).reshape(1, n_pos, d_a)
    vec = pl.BlockSpec((1, 1, GROUP), lambda g: (0, 0, g))
    act = pl.BlockSpec((n_pos, nb, GROUP), lambda g: (0, 0, g))
    return pl.pallas_call(
        functools.partial(_cmlp_sample_kernel, n_pos=n_pos),
        grid=(n_groups,),
        in_specs=[act,
                  pl.BlockSpec((n_pos, nb, GROUP), lambda g: (0, 0, n_groups + g)),
                  vec, vec,
                  pl.BlockSpec((n_pos, n_pos, GROUP), lambda g: (0, 0, g)),
                  pl.BlockSpec((1, n_pos, GROUP), lambda g: (0, 0, g)),
                  vec],
        out_specs=[act, pl.BlockSpec((nb, n_pos, GROUP), lambda g: (0, 0, g))],
        out_shape=[jax.ShapeDtypeStruct((n_pos, nb, d_a), BF16),
                   jax.ShapeDtypeStruct((nb, n_pos, d_a), F32)],
        compiler_params=_params(("arbitrary",)),
        name="chunk_mlp_sample",
    )(h3, h3, ln_g.reshape(1, 1, d_a), ln_b.reshape(1, 1, d_a), wv, bv, out_g.reshape(1, 1, d_a))


def _conv_taps(xbuf_ref, w_ref, tm, shift, halo, width):
    acc = None
    for j in range(width):
        off = halo - (width - 1 - j) * shift
        term = w_ref[j:j + 1, :] * xbuf_ref[off:off + tm, :]
        acc = term if acc is None else acc + term
    return acc


def _qkv_prep_kernel(*refs, tm, shift, halo, n_heads, has_buf):
    if has_buf:
        (xq_ref, xk_ref, xv_ref, bq_ref, bk_ref, bv_ref, wq_ref, wk_ref, wv_ref,
         q_ref, k_ref, v_ref, xbuf_ref, carry_ref) = refs
        bufs = (bq_ref, bk_ref, bv_ref)
    else:
        (xq_ref, xk_ref, xv_ref, wq_ref, wk_ref, wv_ref,
         q_ref, k_ref, v_ref, xbuf_ref, carry_ref, pbuf_ref, pcarry_ref) = refs
        bufs = (None, None, None)
    first = pl.program_id(1) == 0
    parts = ((xq_ref, wq_ref, q_ref, GROUP ** -0.5), (xk_ref, wk_ref, k_ref, 1.0), (xv_ref, wv_ref, v_ref, None))
    for p, (x_ref, w_ref, o_ref, scale) in enumerate(parts):
        @pl.when(first)
        def _(p=p):
            if has_buf:
                xbuf_ref[0:halo, :] = bufs[p][...]
            else:
                xbuf_ref[0:halo, :] = jnp.zeros((halo, xbuf_ref.shape[1]), F32)
                pbuf_ref[0:halo, :] = jnp.zeros((halo, pbuf_ref.shape[1]), F32)

        @pl.when(jnp.logical_not(first))
        def _(p=p):
            xbuf_ref[0:halo, :] = carry_ref[p]
            if not has_buf:
                pbuf_ref[0:halo, :] = pcarry_ref[p]

        xbuf_ref[halo:halo + tm, :] = x_ref[...]
        carry_ref[p] = xbuf_ref[tm:tm + halo, :]
        if has_buf:
            y = _silu(_conv_taps(xbuf_ref, w_ref, tm, shift, halo, CONV_QKV))
        else:
            x0 = xbuf_ref[halo:halo + tm, :]
            x1 = xbuf_ref[halo - 1:halo - 1 + tm, :]
            pbuf_ref[halo:halo + tm, :] = w_ref[1:2, :] * x0 + w_ref[0:1, :] * x1
            pcarry_ref[p] = pbuf_ref[tm:tm + halo, :]
            y = _silu((w_ref[3:4, :] * x0 + w_ref[2:3, :] * x1) + pbuf_ref[halo - 2:halo - 2 + tm, :])
        if scale is None:
            o_ref[...] = y
        else:
            for h in range(n_heads):
                sl = slice(h * GROUP, (h + 1) * GROUP)
                yh = y[:, sl]
                inv = lax.rsqrt(jnp.sum(yh * yh, axis=-1, keepdims=True) + 1e-6)
                o_ref[:, sl] = yh * (inv * scale if scale != 1.0 else inv)


def _qkv_prep(h, conv_w, bufs, n_seq, rows_per_seq, tm, shift, col0):
    d_b = conv_w.shape[1] // 3
    n_heads = d_b // GROUP
    has_buf = bufs is not None
    assert has_buf or (shift == 1 and CONV_QKV == 4), "the fresh-sequence path pairs 4 taps at row shift 1"
    halo = (CONV_QKV - 1) * shift if has_buf else SUBLANES
    tiles = rows_per_seq // tm
    cb = col0 // d_b
    x_specs = [pl.BlockSpec((tm, d_b), lambda s, i, c=c: (s * tiles + i, cb + c)) for c in range(3)]
    w_specs = [pl.BlockSpec((CONV_QKV, d_b), lambda s, i, c=c: (0, c)) for c in range(3)]
    o_spec = pl.BlockSpec((tm, d_b), lambda s, i: (s * tiles + i, 0))
    args, specs = [h, h, h], list(x_specs)
    if has_buf:
        specs += [pl.BlockSpec((halo, d_b), lambda s, i, c=c: (s, c)) for c in range(3)]
        args += [bufs, bufs, bufs]
    specs += w_specs
    args += [conv_w, conv_w, conv_w]
    m = n_seq * rows_per_seq
    return pl.pallas_call(
        functools.partial(_qkv_prep_kernel, tm=tm, shift=shift, halo=halo, n_heads=n_heads, has_buf=has_buf),
        grid=(n_seq, tiles),
        in_specs=specs,
        out_specs=[o_spec, o_spec, o_spec],
        out_shape=[jax.ShapeDtypeStruct((m, d_b), F32)] * 3,
        scratch_shapes=[pltpu.VMEM((halo + tm, d_b), F32), pltpu.VMEM((3, halo, d_b), F32)] * (1 if has_buf else 2),
        compiler_params=_params(("arbitrary", "arbitrary")),
        name="qkv_prep",
    )(*args)


def _gates(ab, alog, dtb):
    g = -jnp.exp(alog) * jax.nn.softplus(ab + dtb)
    beta = jax.nn.sigmoid(ab)
    return g, beta


def _delta_prompt_kernel(q_ref, k_ref, v_ref, z_ref, ab_ref, alog_ref, dtb_ref, ng_ref,
                         yb_ref, sout_ref, s_ref, *, n_heads, n_chunks, ns):
    c = DN_CHUNK
    ci = pl.program_id(1)

    @pl.when(ci == 0)
    def _():
        s_ref[...] = jnp.zeros(s_ref.shape, F32)

    row = lax.broadcasted_iota(jnp.int32, (c, c), 0)
    col = lax.broadcasted_iota(jnp.int32, (c, c), 1)
    causal = row >= col
    strict = row > col
    eye = (row == col).astype(F32)
    tril = causal.astype(F32)
    n_double = (c - 1).bit_length() - 1
    ch = [(s, h) for s in range(ns) for h in range(n_heads)]
    sl = lambda h: slice(h * GROUP, (h + 1) * GROUP)
    gates = [_gates(ab_ref[s], alog_ref[...], dtb_ref[...]) for s in range(ns)]
    gc_s = [jnp.dot(tril, gates[s][0], precision=lax.Precision.HIGHEST, preferred_element_type=F32)
            for s in range(ns)]
    gct_s = [g.T for g in gc_s]
    q = [q_ref[s, :, sl(h)] for s, h in ch]
    k = [k_ref[s, :, sl(h)] for s, h in ch]
    gcol = [gc_s[s][:, h:h + 1] for s, h in ch]
    beta = [gates[s][1][:, n_heads + h:n_heads + h + 1] for s, h in ch]
    n = range(len(ch))
    decay = [jnp.exp(jnp.where(causal, gcol[i] - gct_s[s][h:h + 1, :], -jnp.inf)) for i, (s, h) in enumerate(ch)]
    kb = [k[i] * beta[i] for i in n]
    kbf = [k[i].astype(BF16) for i in n]
    kq = [lax.dot_general(jnp.concatenate([kb[i], q[i]], axis=0).astype(BF16), kbf[i], _NT,
                          preferred_element_type=F32) for i in n]
    qk = [kq[i][c:] * decay[i] for i in n]
    y = [jnp.where(strict, -(kq[i][:c] * decay[i]), 0.0) for i in n]
    x = [eye + y[i] for i in n]
    y = [_bdot(y[i], y[i]) for i in n]
    for _ in range(n_double - 1):
        t = [_bdot(jnp.concatenate([x[i], y[i]], axis=0), y[i]) for i in n]
        x = [x[i] + t[i][:c] for i in n]
        y = [t[i][c:] for i in n]
    x = [x[i] + _bdot(x[i], y[i]) for i in n]
    egc = [jnp.exp(gcol[i]) for i in n]
    sol = [_bdot(x[i], jnp.concatenate([v_ref[s, :, sl(h)] * beta[i], kb[i] * egc[i]], axis=1))
           for i, (s, h) in enumerate(ch)]
    s_old = [s_ref[s, h] for s, h in ch]
    r = [_bdot(jnp.concatenate([sol[i][:, GROUP:], q[i] * egc[i]], axis=0), s_old[i]) for i in n]
    v_new = [sol[i][:, :GROUP] - r[i][:c] for i in n]
    o = [r[i][c:] + _bdot(qk[i], v_new[i]) for i in n]
    gl = [gc_s[s][c - 1:c, h:h + 1] for s, h in ch]
    upd = [lax.dot_general((k[i] * jnp.exp(gl[i] - gcol[i])).astype(BF16), v_new[i].astype(BF16), _TN,
                           preferred_element_type=F32) for i in n]
    ms = [jnp.mean(o[i] * o[i], axis=-1, keepdims=True) for i in n]
    for i, (s, h) in enumerate(ch):
        s_ref[s, h] = s_old[i] * jnp.exp(gl[i]) + upd[i]
        on = o[i] * lax.rsqrt(ms[i] + EPS) * ng_ref[...]
        yb_ref[s, :, sl(h)] = (on * _silu(z_ref[s, :, sl(h)])).astype(yb_ref.dtype)

    @pl.when(ci == n_chunks - 1)
    def _():
        sout_ref[...] = s_ref[...]


def _delta_prompt(q, k, v, h, z_col0, ab, alog, dtb, norm_g, n_seq, rows_per_seq, ns):
    d_b = q.shape[1]
    n_heads = d_b // GROUP
    n_chunks = rows_per_seq // DN_CHUNK
    r3 = lambda t: t.reshape(n_seq, rows_per_seq, t.shape[-1])
    act = pl.BlockSpec((ns, DN_CHUNK, d_b), lambda s, i: (s, i, 0))
    vec = pl.BlockSpec((1, LANES), lambda s, i: (0, 0))
    zb = z_col0 // d_b
    yb, s_new = pl.pallas_call(
        functools.partial(_delta_prompt_kernel, n_heads=n_heads, n_chunks=n_chunks, ns=ns),
        grid=(n_seq // ns, n_chunks),
        in_specs=[act, act, act,
                  pl.BlockSpec((ns, DN_CHUNK, d_b), lambda s, i: (s, i, zb)),
                  pl.BlockSpec((ns, DN_CHUNK, LANES), lambda s, i: (s, i, 0)),
                  vec, vec, vec],
        out_specs=[act, pl.BlockSpec((ns, n_heads, GROUP, GROUP), lambda s, i: (s, 0, 0, 0))],
        out_shape=[jax.ShapeDtypeStruct((n_seq, rows_per_seq, d_b), BF16),
                   jax.ShapeDtypeStruct((n_seq, n_heads, GROUP, GROUP), F32)],
        scratch_shapes=[pltpu.VMEM((ns, n_heads, GROUP, GROUP), F32)],
        compiler_params=_params(("arbitrary", "arbitrary")),
        name="delta_prompt",
    )(r3(q), r3(k), r3(v), r3(h), r3(ab), alog, dtb, norm_g)
    return yb.reshape(n_seq * rows_per_seq, d_b), s_new


def _delta_sample_kernel(q_ref, k_ref, v_ref, z_ref, ab_ref, alog_ref, dtb_ref, ng_ref, s0_ref,
                         yb_ref, s1_ref, lhs_ref, res_ref, kt_ref, vn_ref, gl_ref, *, n_heads, n_pos, bb):
    nt = n_pos
    hs = range(n_heads)
    sls = [slice(h * GROUP, (h + 1) * GROUP) for h in hs]
    gates = [_gates(ab_ref[t], alog_ref[...], dtb_ref[...]) for t in range(nt)]
    zeros_pad = jnp.zeros((bb, GROUP), F32)
    ts = range(nt)
    q = [[q_ref[t, :, sls[h]] for t in ts] for h in hs]
    k = [[k_ref[t, :, sls[h]] for t in ts] for h in hs]
    beta = [[gates[t][1][:, n_heads + h:n_heads + h + 1] for t in ts] for h in hs]
    gc = []
    for h in hs:
        g = [gates[t][0][:, h:h + 1] for t in ts]
        for t in range(1, nt):
            g[t] = g[t - 1] + g[t]
        gc.append(g)
    kb = [[k[h][t] * beta[h][t] for t in ts] for h in hs]
    kk = [[[jnp.sum(kb[h][t] * k[h][s], axis=-1, keepdims=True) for s in range(t)] for t in ts] for h in hs]
    qk = [[[jnp.sum(q[h][t] * k[h][s], axis=-1, keepdims=True) for s in range(t + 1)] for t in ts] for h in hs]
    dec = [[[jnp.exp(gc[h][t] - gc[h][s]) for s in range(t + 1)] for t in ts] for h in hs]
    a = [[[kk[h][t][s] * dec[h][t][s] for s in range(t)] for t in ts] for h in hs]
    qk = [[[qk[h][t][s] * dec[h][t][s] for s in range(t + 1)] for t in ts] for h in hs]
    egc = [[jnp.exp(gc[h][t]) for t in ts] for h in hs]
    u_blk = [[v_ref[t, :, sls[h]] * beta[h][t] for t in ts] for h in hs]
    w_blk = [[kb[h][t] * egc[h][t] for t in ts] for h in hs]
    for t in ts:
        for h in hs:
            for s in range(t):
                u_blk[h][t] = u_blk[h][t] - a[h][t][s] * u_blk[h][s]
                w_blk[h][t] = w_blk[h][t] - a[h][t][s] * w_blk[h][s]
    for h in hs:
        for t in ts:
            lhs_ref[h, :, t, :] = w_blk[h][t]
            lhs_ref[h, :, nt + t, :] = q[h][t] * egc[h][t]
        gl_ref[h] = jnp.broadcast_to(egc[h][nt - 1], (bb, GROUP))

    for h in hs:
        for b in range(bb):
            res_ref[h, b] = _bdot(lhs_ref[h, b], s0_ref[b, h])

    v_new = [[u_blk[h][t] - res_ref[h, :, t, :] for t in ts] for h in hs]
    o = [[res_ref[h, :, nt + t, :] for t in ts] for h in hs]
    for h in hs:
        for t in ts:
            for s in range(t + 1):
                o[h][t] = o[h][t] + qk[h][t][s] * v_new[h][s]
    ms = [[jnp.mean(o[h][t] * o[h][t], axis=-1, keepdims=True) for t in ts] for h in hs]
    for h in hs:
        gl = gc[h][nt - 1]
        for t in ts:
            on = o[h][t] * lax.rsqrt(ms[h][t] + EPS) * ng_ref[...]
            yb_ref[t, :, sls[h]] = (on * _silu(z_ref[t, :, sls[h]])).astype(yb_ref.dtype)
            kt_ref[h, :, t, :] = k[h][t] * jnp.exp(gl - gc[h][t])
            vn_ref[h, :, t, :] = v_new[h][t]
            kt_ref[h, :, nt + t, :] = zeros_pad
            vn_ref[h, :, nt + t, :] = zeros_pad

    for h in hs:
        for b in range(bb):
            upd = lax.dot_general(kt_ref[h, b].astype(BF16), vn_ref[h, b].astype(BF16), _TN,
                                  preferred_element_type=F32)
            s1_ref[b, h] = s0_ref[b, h] * gl_ref[h, b:b + 1, :] + upd


def _delta_sample(q, k, v, h3, z_col0, ab, alog, dtb, norm_g, s0, bb):
    n_pos, nb, d_b = q.shape
    n_heads = d_b // GROUP
    assert 2 * n_pos == SUBLANES, "one state read packs [w | q] rows of a sequence into one 8-row tile"
    zb = z_col0 // d_b
    act = pl.BlockSpec((n_pos, bb, d_b), lambda i: (0, i, 0))
    vec = pl.BlockSpec((1, LANES), lambda i: (0, 0))
    st = pl.BlockSpec((bb, n_heads, GROUP, GROUP), lambda i: (i, 0, 0, 0))
    tile = pltpu.VMEM((n_heads, bb, SUBLANES, GROUP), F32)
    return pl.pallas_call(
        functools.partial(_delta_sample_kernel, n_heads=n_heads, n_pos=n_pos, bb=bb),
        grid=(nb // bb,),
        in_specs=[act, act, act,
                  pl.BlockSpec((n_pos, bb, d_b), lambda i: (0, i, zb)),
                  pl.BlockSpec((n_pos, bb, LANES), lambda i: (0, i, 0)),
                  vec, vec, vec, st],
        out_specs=[act, st],
        out_shape=[jax.ShapeDtypeStruct((n_pos, nb, d_b), BF16),
                   jax.ShapeDtypeStruct(s0.shape, F32)],
        scratch_shapes=[tile, tile, tile, tile, pltpu.VMEM((n_heads, bb, GROUP), F32)],
        compiler_params=_params(("arbitrary",)),
        name="delta_sample",
    )(q, k, v, h3, ab, alog, dtb, norm_g, s0)


def _out_proj_kernel(ya_ref, yb_ref, w_ref, r_ref, g_ref, b_ref, o_ref, ob_ref, z_ref, *, n_tiles, d_a, alpha):
    i = pl.program_id(0)

    @pl.when(i == 0)
    def _():
        z_ref[...] = jnp.zeros(z_ref.shape, F32)

    def emit_ln():
        y = _layer_norm(z_ref[...], g_ref[...], b_ref[...])
        o_ref[...] = y
        ob_ref[...] = y.astype(ob_ref.dtype)

    @pl.when(i < n_tiles)
    def _():
        emit_ln()
        acc = (jnp.dot(ya_ref[...], w_ref[0:d_a, :], preferred_element_type=F32)
               + jnp.dot(yb_ref[...], w_ref[d_a:, :], preferred_element_type=F32))
        z_ref[...] = alpha * r_ref[...] + acc

    @pl.when(i == n_tiles)
    def _():
        emit_ln()


def _out_proj_ln(ya, yb, w, resid, g, b, alpha, tm):
    m, d_a = ya.shape
    k, d = w.shape
    n_tiles = m // tm
    cur = lambda i: (jnp.minimum(i, n_tiles - 1), 0)
    prev = lambda i: (jnp.maximum(i - 1, 0), 0)
    vec = pl.BlockSpec((1, d), lambda i: (0, 0))
    return pl.pallas_call(
        functools.partial(_out_proj_kernel, n_tiles=n_tiles, d_a=d_a, alpha=alpha),
        grid=(n_tiles + 1,),
        in_specs=[pl.BlockSpec((tm, d_a), cur),
                  pl.BlockSpec((tm, k - d_a), cur),
                  pl.BlockSpec((k, d), lambda i: (0, 0), pipeline_mode=pl.Buffered(1)),
                  pl.BlockSpec((tm, d), cur), vec, vec],
        out_specs=[pl.BlockSpec((tm, d), prev), pl.BlockSpec((tm, d), prev)],
        out_shape=[jax.ShapeDtypeStruct((m, d), F32), jax.ShapeDtypeStruct((m, d), BF16)],
        scratch_shapes=[pltpu.VMEM((tm, d), F32)],
        compiler_params=_params(("arbitrary",)),
        name="out_proj_ln1",
    )(ya, yb, w, resid, g.reshape(1, d), b.reshape(1, d))


def _down_kernel(x_ref, w_ref, r_ref, g_ref, b_ref, o_ref, *rest, n_tiles, nk, nc, alpha, n_pos):
    z_ref = rest[-1]
    cw = z_ref.shape[-1]
    slab = lambda c: slice(c * cw, (c + 1) * cw)
    i, kk = pl.program_id(0), pl.program_id(1)

    def emit_ln():
        z = [z_ref[c] for c in range(nc)]
        inv_d = 1.0 / (nc * cw)
        mu = sum(jnp.sum(zc, axis=-1, keepdims=True) for zc in z) * inv_d
        xc = [zc - mu for zc in z]
        var = sum(jnp.sum(c * c, axis=-1, keepdims=True) for c in xc) * inv_d
        rs = lax.rsqrt(var + EPS)
        for c in range(nc):
            y = xc[c] * rs * g_ref[:, slab(c)] + b_ref[:, slab(c)]
            if n_pos:
                nb = y.shape[0] // n_pos
                for t in range(n_pos):
                    o_ref[:, t, slab(c)] = y[t * nb:(t + 1) * nb]
            else:
                o_ref[:, slab(c)] = y

    @pl.when(jnp.logical_and(i == 0, kk == 0))
    def _():
        z_ref[...] = jnp.zeros(z_ref.shape, F32)

    wb = w_ref[...].astype(BF16)
    if len(rest) == 2:
        rest[0][...] = wb

    @pl.when(kk == 0)
    def _():
        emit_ln()
        part = jnp.dot(x_ref[...], wb, preferred_element_type=F32)
        for c in range(nc):
            if nc == nk:
                z_ref[c] = part[:, slab(c)] + alpha * r_ref[...] if c == 0 else part[:, slab(c)]
            else:
                z_ref[c] = part[:, slab(c)] + alpha * r_ref[:, slab(c)]

    @pl.when(jnp.logical_and(kk > 0, i < n_tiles))
    def _():
        part = jnp.dot(x_ref[...], wb, preferred_element_type=F32)
        for c in range(nc):
            z_ref[c] += part[:, slab(c)]
        if nc == nk:
            z_ref[kk] += alpha * r_ref[...]


def _down_resident_kernel(x_ref, w_ref, r_ref, g_ref, b_ref, o_ref, z_ref, *, n_tiles, alpha):
    i = pl.program_id(0)

    @pl.when(i == 0)
    def _():
        z_ref[...] = jnp.zeros(z_ref.shape, F32)

    @pl.when(i < n_tiles)
    def _():
        o_ref[...] = _layer_norm(z_ref[...], g_ref[...], b_ref[...])
        z_ref[...] = alpha * r_ref[...] + jnp.dot(x_ref[...], w_ref[...], preferred_element_type=F32)

    @pl.when(i == n_tiles)
    def _():
        o_ref[...] = _layer_norm(z_ref[...], g_ref[...], b_ref[...])


def _down_ln_resident(x, w, resid, g, b, alpha, tm):
    m, kdim = x.shape
    d = w.shape[1]
    n_tiles = m // tm
    cur = lambda i: (jnp.minimum(i, n_tiles - 1), 0)
    vec = pl.BlockSpec((1, d), lambda i: (0, 0))
    return pl.pallas_call(
        functools.partial(_down_resident_kernel, n_tiles=n_tiles, alpha=alpha),
        grid=(n_tiles + 1,),
        in_specs=[pl.BlockSpec((tm, kdim), cur),
                  pl.BlockSpec((kdim, d), lambda i: (0, 0), pipeline_mode=pl.Buffered(1)),
                  pl.BlockSpec((tm, d), cur), vec, vec],
        out_specs=pl.BlockSpec((tm, d), lambda i: (jnp.maximum(i - 1, 0), 0)),
        out_shape=jax.ShapeDtypeStruct((m, d), F32),
        scratch_shapes=[pltpu.VMEM((tm, d), F32)],
        compiler_params=_params(("arbitrary",), BIG_VMEM_LIMIT),
        name="down_ln2_resident",
    )(x, w, resid, g.reshape(1, d), b.reshape(1, d))


def _down_ln(x, w, resid, g, b, alpha, tm, tk, emit_w=False, n_pos=0):
    m, kdim = x.shape
    d = w.shape[1]
    nk = kdim // tk
    n_tiles = m // tm
    nc = nk if (d % nk == 0 and (d // nk) % LANES == 0) else 1
    row = lambda i: jnp.minimum(i, n_tiles - 1)
    kt = lambda i, kk: jnp.where(i == n_tiles, nk - 1, kk)
    vec = pl.BlockSpec((1, d), lambda i, kk: (0, 0))
    wspec = pl.BlockSpec((tk, d), lambda i, kk: (kt(i, kk), 0))
    if nc == nk:
        rspec = pl.BlockSpec((tm, d // nc), lambda i, kk: (row(i), kt(i, kk)))
    else:
        rspec = pl.BlockSpec((tm, d), lambda i, kk: (row(i), 0))
    if n_pos:
        assert n_tiles == 1
        out_specs = [pl.BlockSpec((m // n_pos, n_pos, d), lambda i, kk: (0, 0, 0))]
        out_shape = [jax.ShapeDtypeStruct((m // n_pos, n_pos, d), F32)]
    else:
        out_specs = [pl.BlockSpec((tm, d), lambda i, kk: (jnp.where(kk == 0, jnp.maximum(i - 1, 0), row(i)), 0))]
        out_shape = [jax.ShapeDtypeStruct((m, d), F32)]
    if emit_w:
        out_specs.append(wspec)
        out_shape.append(jax.ShapeDtypeStruct(w.shape, BF16))
    outs = pl.pallas_call(
        functools.partial(_down_kernel, n_tiles=n_tiles, nk=nk, nc=nc, alpha=alpha, n_pos=n_pos),
        grid=(n_tiles + 1, nk),
        in_specs=[pl.BlockSpec((tm, tk), lambda i, kk: (row(i), kt(i, kk))), wspec, rspec, vec, vec],
        out_specs=out_specs,
        out_shape=out_shape,
        scratch_shapes=[pltpu.VMEM((nc, tm, d // nc), F32)],
        compiler_params=_params(("arbitrary", "arbitrary"), BIG_VMEM_LIMIT),
        name="down_ln2",
    )(x, w, resid, g.reshape(1, d), b.reshape(1, d))
    return tuple(outs) if emit_w else outs[0]


def _ffn_up_kernel(x_ref, wg_ref, wv_ref, cwg_ref, cwv_ref, cbg_ref, cbv_ref,
                   act_ref, tg_ref, tv_ref, wgo_ref, wvo_ref, wgb_ref, wvb_ref, carry_ref, *, tm):
    halo = SUBLANES
    first = pl.program_id(2) == 0

    @pl.when(jnp.logical_and(pl.program_id(1) == 0, first))
    def _():
        for w_ref, wb_ref, wo_ref in ((wg_ref, wgb_ref, wgo_ref), (wv_ref, wvb_ref, wvo_ref)):
            wb = w_ref[...].astype(BF16)
            wb_ref[...] = wb
            wo_ref[...] = wb

    @pl.when(first)
    def _():
        carry_ref[...] = jnp.zeros(carry_ref.shape, F32)

    x = x_ref[...]
    conv = []
    parts = ((wgb_ref, cwg_ref, cbg_ref, tg_ref), (wvb_ref, cwv_ref, cbv_ref, tv_ref))
    row = lax.broadcasted_iota(jnp.int32, (halo, act_ref.shape[1]), 0)
    for p, (w_ref, cw_ref, cb_ref, t_ref) in enumerate(parts):
        up = jnp.dot(x, w_ref[...], preferred_element_type=F32)
        hist = carry_ref[p]
        tail = up[tm - halo:tm, :]
        carry_ref[p] = tail
        t_ref[0] = tail
        acc = cw_ref[CONV_FFN - 1:CONV_FFN, :] * up
        for dd in range(1, CONV_FFN):
            xs = pltpu.roll(up, dd, axis=0)
            head = jnp.where(row < dd, pltpu.roll(hist, dd, axis=0), xs[0:halo])
            xs = jnp.concatenate([head, xs[halo:]], axis=0)
            acc = acc + cw_ref[CONV_FFN - 1 - dd:CONV_FFN - dd, :] * xs
        conv.append(acc + cb_ref[...])
    act_ref[...] = (_silu(conv[0]) * conv[1]).astype(act_ref.dtype)


def _ffn_up_carried_kernel(x_ref, wg_ref, wv_ref, buf_ref, cw_ref, cb_ref, act_ref, t_ref, xbuf_ref, gate_ref,
                           *, shift):
    p, j = pl.program_id(0), pl.program_id(1)
    tm, halo = x_ref.shape[0], buf_ref.shape[0]
    xbuf_ref[0:halo, :] = buf_ref[...]

    def conv_of(w_ref):
        xbuf_ref[halo:halo + tm, :] = jnp.dot(x_ref[...], w_ref[...], preferred_element_type=F32)
        tail = xbuf_ref[tm:tm + halo, :]
        for t in range(halo // shift):
            t_ref[:, t, :] = tail[t * shift:(t + 1) * shift]
        return _conv_taps(xbuf_ref, cw_ref, tm, shift, halo, CONV_FFN) + cb_ref[...]

    @pl.when(p == 0)
    def _():
        gate_ref[j] = conv_of(wg_ref)

    @pl.when(p == 1)
    def _():
        act_ref[...] = (_silu(gate_ref[j]) * conv_of(wv_ref)).astype(act_ref.dtype)


def _ffn_up_carried(x, w_gate_b, w_val_b, conv_w, conv_b, bufs, tn, shift):
    m, d = x.shape
    d_ff = w_gate_b.shape[1]
    nj = d_ff // tn
    halo = (CONV_FFN - 1) * shift
    col = lambda p, j: (0, p * nj + j)
    return pl.pallas_call(
        functools.partial(_ffn_up_carried_kernel, shift=shift),
        grid=(2, nj),
        in_specs=[pl.BlockSpec((m, d), lambda p, j: (0, 0)),
                  pl.BlockSpec((d, tn), lambda p, j: (0, jnp.where(p == 0, j, nj - 1))),
                  pl.BlockSpec((d, tn), lambda p, j: (0, jnp.where(p == 1, j, 0))),
                  pl.BlockSpec((halo, tn), col),
                  pl.BlockSpec((CONV_FFN, tn), col),
                  pl.BlockSpec((1, tn), col)],
        out_specs=[pl.BlockSpec((m, tn), lambda p, j: (0, jnp.where(p == 1, j, 0))),
                   pl.BlockSpec((shift, halo // shift, tn), lambda p, j: (0, 0, p * nj + j))],
        out_shape=[jax.ShapeDtypeStruct((m, d_ff), BF16),
                   jax.ShapeDtypeStruct((shift, halo // shift, 2 * d_ff), F32)],
        scratch_shapes=[pltpu.VMEM((halo + m, tn), F32), pltpu.VMEM((nj, m, tn), F32)],
        compiler_params=_params(("arbitrary", "arbitrary")),
        name="ffn_up_carried",
    )(x, w_gate_b, w_val_b, bufs, conv_w, conv_b.reshape(1, 2 * d_ff))


def _ffn_up(x, w_up, conv_w, conv_b, n_seq, rows_per_seq, tm, tn):
    m, d = x.shape
    d_ff = w_up.shape[1] // 2
    nj = d_ff // tn
    halo = SUBLANES
    tiles = rows_per_seq // tm
    wspec = lambda off: pl.BlockSpec((d, tn), lambda j, s, i: (0, off + j))
    cwspec = lambda off: pl.BlockSpec((CONV_FFN, tn), lambda j, s, i: (0, off + j))
    cbspec = lambda off: pl.BlockSpec((1, tn), lambda j, s, i: (0, off + j))
    tspec = pl.BlockSpec((1, halo, tn), lambda j, s, i: (s, 0, j))
    tshape = jax.ShapeDtypeStruct((n_seq, halo, d_ff), F32)
    cb2 = conv_b.reshape(1, 2 * d_ff)
    return pl.pallas_call(
        functools.partial(_ffn_up_kernel, tm=tm),
        grid=(nj, n_seq, tiles),
        in_specs=[pl.BlockSpec((tm, d), lambda j, s, i: (s * tiles + i, 0)), wspec(0), wspec(nj),
                  cwspec(0), cwspec(nj), cbspec(0), cbspec(nj)],
        out_specs=[pl.BlockSpec((tm, tn), lambda j, s, i: (s * tiles + i, j)), tspec, tspec, wspec(0), wspec(0)],
        out_shape=[jax.ShapeDtypeStruct((m, d_ff), BF16), tshape, tshape]
                  + [jax.ShapeDtypeStruct((d, d_ff), BF16)] * 2,
        scratch_shapes=[pltpu.VMEM((d, tn), BF16)] * 2 + [pltpu.VMEM((2, halo, tn), F32)],
        compiler_params=_params(("arbitrary", "arbitrary", "arbitrary")),
        name="ffn_up",
    )(x, w_up, w_up, conv_w, conv_w, cb2, cb2)


def _pad_lanes(vec, offset=0):
    out = jnp.zeros((1, LANES), F32)
    return out.at[0, offset:offset + vec.shape[0]].set(vec.astype(F32))


def _mixer_half(x2, lw, alpha, n_seq, rows_per_seq, shift, state, w_in_b):
    (w_in, conv_qkv_w, a_log, dt_bias, norm_g, gm_ln_g, gm_ln_b, gm_ws, gm_bs, gm_out_g, w_out,
     ln1_g, ln1_b) = lw
    m, d_model = x2.shape
    n_groups = gm_ws.shape[0]
    d_a = n_groups * GROUP
    d_qkv = conv_qkv_w.shape[1]
    d_b = d_qkv // 3
    n_heads = d_b // GROUP
    main_cols = 2 * d_a + d_qkv + d_b
    w_ab = jnp.pad(w_in[main_cols:].astype(BF16), ((0, LANES - 2 * n_heads), (0, 0)))
    if w_in_b is None:
        h, ab, w_in_b = _matmul(x2, w_in, w_ab, main_cols, min(m, PROJ_TM), PROJ_EMIT_TN, emit_w=True)
    else:
        h, ab = _matmul(x2, w_in_b, w_ab, main_cols, min(m, PROJ_TM), PROJ_TN)
    alog, dtb, ng = _pad_lanes(a_log), _pad_lanes(dt_bias), norm_g.reshape(1, GROUP).astype(F32)

    if state is None:
        ya = _chunk_mlp_prompt(h, gm_ln_g, gm_ln_b, gm_ws, gm_bs, gm_out_g)
        v_rows = None
        q, k, v = _qkv_prep(h, conv_qkv_w, None, n_seq, rows_per_seq, QKV_TM, shift, 2 * d_a)
        yb, s_new = _delta_prompt(q, k, v, h, 2 * d_a + d_qkv, ab, alog, dtb, ng, n_seq, rows_per_seq, DELTA_SEQS)
    else:
        s_dn, buf_qkv = state
        n_pos = rows_per_seq // shift
        h3 = h.reshape(n_pos, shift, main_cols)
        ya3, v_rows = _chunk_mlp_sample(h3, gm_ln_g, gm_ln_b, gm_ws, gm_bs, gm_out_g)
        ya = ya3.reshape(m, d_a)
        q, k, v = _qkv_prep(h, conv_qkv_w, buf_qkv, n_seq, rows_per_seq, rows_per_seq, shift, 2 * d_a)
        r3 = lambda t: t.reshape(n_pos, shift, t.shape[-1])
        yb3, s_new = _delta_sample(r3(q), r3(k), r3(v), h3, 2 * d_a + d_qkv, r3(ab), alog, dtb, ng, s_dn,
                                   DELTA_SAMPLE_ROWS)
        yb = yb3.reshape(m, d_b)

    x1, x1b = _out_proj_ln(ya, yb, w_out, x2, ln1_g, ln1_b, alpha, min(m, OUT_TM))
    return x1, x1b, s_new, h, v_rows, w_in_b


def kernel(x_prompt, x_sample, state_dn, state_conv_qkv, state_conv_ffn, w_in, conv_qkv_w, dn_a_log, dn_dt_bias, dn_norm_g, gm_ln_g, gm_ln_b, gm_ws, gm_bs, gm_out_g, w_out, ln1_g, ln1_b, w_up, conv_ffn_w, conv_ffn_b, w_down, ln2_g, ln2_b):
    depth = w_in.shape[0]
    bp, lp, d_model = x_prompt.shape
    bs, ls, _ = x_sample.shape
    d_qkv = conv_qkv_w.shape[-1]
    d_a = gm_ws.shape[1] * GROUP
    alpha = (2.0 * depth) ** 0.25

    hp = x_prompt.reshape(bp * lp, d_model)
    hs = jnp.transpose(x_sample, (1, 0, 2)).reshape(ls * bs, d_model)
    outs = [[] for _ in range(7)]
    w_out = w_out.astype(BF16)
    w_in = jnp.swapaxes(w_in, 1, 2)
    for l in range(depth):
        lw = tuple(t[l] for t in (w_in, conv_qkv_w, dn_a_log, dn_dt_bias, dn_norm_g, gm_ln_g, gm_ln_b, gm_ws,
                                  gm_bs, gm_out_g, w_out, ln1_g, ln1_b))
        ffn_w = (conv_ffn_w[l], conv_ffn_b[l])
        ln2 = (ln2_g[l], ln2_b[l], alpha)
        n_pos = ls if l == depth - 1 else 0
        bq = jnp.transpose(state_conv_qkv[l], (1, 0, 2)).reshape((CONV_QKV - 1) * bs, d_qkv)
        bf = jnp.transpose(state_conv_ffn[l], (1, 0, 2)).reshape((CONV_FFN - 1) * bs, -1)
        x1_s, x1b_s, s_s, h_s, v_s, w_in_b = _mixer_half(hs, lw, alpha, 1, ls * bs, bs, (state_dn[l], bq), None)
        x1_p, x1b_p, s_p, h_p, _, _ = _mixer_half(hp, lw, alpha, bp, lp, 1, None, w_in_b)
        act_p, tg_p, tv_p, w_gate_b, w_val_b = _ffn_up(x1b_p, w_up[l], *ffn_w, bp, lp, FFN_TM, FFN_TN)
        act_s, cf_s = _ffn_up_carried(x1b_s, w_gate_b, w_val_b, *ffn_w, bf, FFN_TN, bs)
        hs, w_down_b = _down_ln(act_s, w_down[l], x1_s, *ln2, min(ls * bs, DOWN_EMIT_TM), DOWN_EMIT_TK,
                                emit_w=True, n_pos=n_pos)
        hp = _down_ln_resident(act_p, w_down_b, x1_p, *ln2, min(bp * lp, DOWN_TM))
        h_s3 = h_s.reshape(ls, bs, -1)
        cq_s = jnp.transpose(h_s3[ls - (CONV_QKV - 1):, :, 2 * d_a:2 * d_a + d_qkv], (1, 0, 2))
        h_p3 = h_p.reshape(bp, lp, -1)
        cq_p = h_p3[:, lp - (CONV_QKV - 1):, 2 * d_a:2 * d_a + d_qkv]
        n_tail = tg_p.shape[1]
        cf_p = jnp.concatenate([tg_p, tv_p], axis=-1)[:, n_tail - (CONV_FFN - 1):]
        for acc, val in zip(outs, (s_p, cq_p, cf_p, s_s, cq_s, cf_s, v_s)):
            acc.append(val)
    return (hp.reshape(bp, lp, d_model), hs) + tuple(jnp.stack(o) for o in outs)
```

```python
import functools

import jax
import jax.numpy as jnp
from jax import lax
from jax.experimental import pallas as pl
from jax.experimental.pallas import tpu as pltpu

F32 = jnp.float32
BF16 = jnp.bfloat16

LANES = 128
SUBLANES = 8
VMEM_LIMIT = 56 * 1024 * 1024
BIG_VMEM_LIMIT = 62 * 1024 * 1024

GROUP = 128
MLP_CHUNK = 128
DN_CHUNK = 64
CONV_QKV = 4
CONV_FFN = 3
CMLP_CHUNKS = 8
EPS = 1e-5

PROJ_TM, PROJ_TN = 1024, 2048
PROJ_EMIT_TN = 1536
QKV_TM = 512
DELTA_SEQS = 4
DELTA_SAMPLE_ROWS = 16
OUT_TM = 512
FFN_TM, FFN_TN = 1024, 512
DOWN_TM = 512
DOWN_EMIT_TM, DOWN_EMIT_TK = 512, 1408

_NT = (((1,), (1,)), ((), ()))
_TN = (((0,), (0,)), ((), ()))


def _params(sem, vmem_limit=VMEM_LIMIT):
    return pltpu.CompilerParams(dimension_semantics=sem, vmem_limit_bytes=vmem_limit)


def _bdot(a, b):
    return jnp.dot(a.astype(BF16), b.astype(BF16), preferred_element_type=F32)


def _gelu(x):
    return 0.5 * x * (1.0 + lax.erf(x * (0.5 ** 0.5)))


def _silu(x):
    return x * jax.nn.sigmoid(x)


def _layer_norm(x, g, b):
    mu = jnp.mean(x, axis=-1, keepdims=True)
    xc = x - mu
    var = jnp.mean(xc * xc, axis=-1, keepdims=True)
    return xc * lax.rsqrt(var + EPS) * g + b


def _rms_norm(x, g):
    ms = jnp.mean(x * x, axis=-1, keepdims=True)
    return x * lax.rsqrt(ms + EPS) * g


def _mm_kernel(x_ref, w_ref, we_ref, o_ref, e_ref, *rest):
    xb_ref = rest[-1]

    @pl.when(pl.program_id(1) == 0)
    def _():
        xb = x_ref[...].astype(BF16)
        xb_ref[...] = xb
        e_ref[...] = lax.dot_general(xb, we_ref[...], _NT, preferred_element_type=F32)

    wb = w_ref[...].astype(BF16)
    if len(rest) == 2:
        rest[0][...] = wb
    o_ref[...] = lax.dot_general(xb_ref[...], wb, _NT, preferred_element_type=F32)


def _matmul(x, wt, wt_extra, n_cols, tm, tn, emit_w=False):
    m, k = x.shape
    ne = wt_extra.shape[0]
    out_specs = [pl.BlockSpec((tm, tn), lambda i, j: (i, j)), pl.BlockSpec((tm, ne), lambda i, j: (i, 0))]
    out_shape = [jax.ShapeDtypeStruct((m, n_cols), F32), jax.ShapeDtypeStruct((m, ne), F32)]
    if emit_w:
        out_specs.append(pl.BlockSpec((tn, k), lambda i, j: (j, 0)))
        out_shape.append(jax.ShapeDtypeStruct((n_cols, k), BF16))
    return pl.pallas_call(
        _mm_kernel,
        grid=(m // tm, n_cols // tn),
        in_specs=[pl.BlockSpec((tm, k), lambda i, j: (i, 0)),
                  pl.BlockSpec((tn, k), lambda i, j: (j, 0)),
                  pl.BlockSpec((ne, k), lambda i, j: (0, 0))],
        out_specs=out_specs,
        out_shape=out_shape,
        scratch_shapes=[pltpu.VMEM((tm, k), BF16)],
        compiler_params=_params(("arbitrary", "arbitrary"), BIG_VMEM_LIMIT),
        name="proj",
    )(x, wt, wt_extra)


def _cmlp_prompt_kernel(u_ref, v_ref, lng_ref, lnb_ref, ws_ref, bst_ref, og_ref, ya_ref, *, n_groups, n_chunks):
    row = lax.broadcasted_iota(jnp.int32, (MLP_CHUNK, MLP_CHUNK), 0)
    col = lax.broadcasted_iota(jnp.int32, (MLP_CHUNK, MLP_CHUNK), 1)
    cg = [(c, g) for c in range(n_chunks) for g in range(n_groups)]
    rows = lambda c: slice(c * MLP_CHUNK, (c + 1) * MLP_CHUNK)
    cols = lambda g: slice(g * GROUP, (g + 1) * GROUP)
    w = [jnp.where(row >= col, ws_ref[g], 0.0).astype(BF16) for g in range(n_groups)]
    n = range(len(cg))
    gv = [_gelu(v_ref[rows(c), cols(g)]) for c, g in cg]
    mu = [jnp.mean(gv[i], axis=-1, keepdims=True) for i in n]
    xc = [gv[i] - mu[i] for i in n]
    var = [jnp.mean(xc[i] * xc[i], axis=-1, keepdims=True) for i in n]
    vh = [xc[i] * lax.rsqrt(var[i] + EPS) * lng_ref[:, cols(g)] + lnb_ref[:, cols(g)] for i, (c, g) in enumerate(cg)]
    mixed = [jnp.dot(w[g], vh[i].astype(BF16), preferred_element_type=F32) + bst_ref[:, g:g + 1]
             for i, (c, g) in enumerate(cg)]
    y = [_gelu(u_ref[rows(c), cols(g)]) * mixed[i] for i, (c, g) in enumerate(cg)]
    ms = [jnp.mean(y[i] * y[i], axis=-1, keepdims=True) for i in n]
    for i, (c, g) in enumerate(cg):
        ya_ref[rows(c), cols(g)] = (y[i] * lax.rsqrt(ms[i] + EPS) * og_ref[:, cols(g)]).astype(ya_ref.dtype)


def _chunk_mlp_prompt(h, ln_g, ln_b, ws, bs, out_g):
    m = h.shape[0]
    n_groups, d_a = ws.shape[0], ws.shape[0] * GROUP
    vec = pl.BlockSpec((1, d_a), lambda i: (0, 0))
    tm = CMLP_CHUNKS * MLP_CHUNK
    return pl.pallas_call(
        functools.partial(_cmlp_prompt_kernel, n_groups=n_groups, n_chunks=CMLP_CHUNKS),
        grid=(m // tm,),
        in_specs=[pl.BlockSpec((tm, d_a), lambda i: (i, 0)),
                  pl.BlockSpec((tm, d_a), lambda i: (i, 1)),
                  vec, vec,
                  pl.BlockSpec((n_groups, MLP_CHUNK, MLP_CHUNK), lambda i: (0, 0, 0)),
                  pl.BlockSpec((MLP_CHUNK, n_groups), lambda i: (0, 0)),
                  vec],
        out_specs=pl.BlockSpec((tm, d_a), lambda i: (i, 0)),
        out_shape=jax.ShapeDtypeStruct((m, d_a), BF16),
        compiler_params=_params(("arbitrary",)),
        name="chunk_mlp_prompt",
    )(h, h, ln_g.reshape(1, d_a), ln_b.reshape(1, d_a), ws, bs.T, out_g.reshape(1, d_a))


def _cmlp_sample_kernel(u_ref, v_ref, lng_ref, lnb_ref, wv_ref, bv_ref, og_ref, ya_ref, vr_ref, *, n_pos):
    vh = []
    for t in range(n_pos):
        vt = _layer_norm(_gelu(v_ref[t]), lng_ref[0], lnb_ref[0])
        vr_ref[:, t, :] = vt
        vh.append(vt)
    for t in range(n_pos):
        mixed = wv_ref[t, 0:1, :] * vh[0]
        for s in range(1, t + 1):
            mixed = mixed + wv_ref[t, s:s + 1, :] * vh[s]
        mixed = mixed + bv_ref[0, t:t + 1, :]
        y = _gelu(u_ref[t]) * mixed
        ya_ref[t] = _rms_norm(y, og_ref[0]).astype(ya_ref.dtype)


def _chunk_mlp_sample(h3, ln_g, ln_b, ws, bs, out_g):
    n_pos, nb, _ = h3.shape
    n_groups = ws.shape[0]
    d_a = n_groups * GROUP
    wv = jnp.repeat(jnp.transpose(ws[:, :n_pos, :n_pos], (1, 2, 0)), GROUP, axis=-1)
    bv = jnp.repeat(bs[:, :n_pos].T, GROUP, axis=-1).reshape(1, n_pos, d_a)
    vec = pl.BlockSpec((1, 1, GROUP), lambda g: (0, 0, g))
    act = pl.BlockSpec((n_pos, nb, GROUP), lambda g: (0, 0, g))
    return pl.pallas_call(
        functools.partial(_cmlp_sample_kernel, n_pos=n_pos),
        grid=(n_groups,),
        in_specs=[act,
                  pl.BlockSpec((n_pos, nb, GROUP), lambda g: (0, 0, n_groups + g)),
                  vec, vec,
                  pl.BlockSpec((n_pos, n_pos, GROUP), lambda g: (0, 0, g)),
                  pl.BlockSpec((1, n_pos, GROUP), lambda g: (0, 0, g)),
                  vec],
        out_specs=[act, pl.BlockSpec((nb, n_pos, GROUP), lambda g: (0, 0, g))],
        out_shape=[jax.ShapeDtypeStruct((n_pos, nb, d_a), BF16),
                   jax.ShapeDtypeStruct((nb, n_pos, d_a), F32)],
        compiler_params=_params(("arbitrary",)),
        name="chunk_mlp_sample",
    )(h3, h3, ln_g.reshape(1, 1, d_a), ln_b.reshape(1, 1, d_a), wv, bv, out_g.reshape(1, 1, d_a))


def _conv_taps(xbuf_ref, w_ref, tm, shift, halo, width):
    acc = None
    for j in range(width):
        off = halo - (width - 1 - j) * shift
        term = w_ref[j:j + 1, :] * xbuf_ref[off:off + tm, :]
        acc = term if acc is None else acc + term
    return acc


def _qkv_prep_kernel(*refs, tm, shift, halo, n_heads, has_buf):
    if has_buf:
        (xq_ref, xk_ref, xv_ref, bq_ref, bk_ref, bv_ref, wq_ref, wk_ref, wv_ref,
         q_ref, k_ref, v_ref, xbuf_ref, carry_ref) = refs
        bufs = (bq_ref, bk_ref, bv_ref)
    else:
        (xq_ref, xk_ref, xv_ref, wq_ref, wk_ref, wv_ref,
         q_ref, k_ref, v_ref, xbuf_ref, carry_ref, pbuf_ref, pcarry_ref) = refs
        bufs = (None, None, None)
    first = pl.program_id(1) == 0
    parts = ((xq_ref, wq_ref, q_ref, GROUP ** -0.5), (xk_ref, wk_ref, k_ref, 1.0), (xv_ref, wv_ref, v_ref, None))
    for p, (x_ref, w_ref, o_ref, scale) in enumerate(parts):
        @pl.when(first)
        def _(p=p):
            if has_buf:
                xbuf_ref[0:halo, :] = bufs[p][...]
            else:
                xbuf_ref[0:halo, :] = jnp.zeros((halo, xbuf_ref.shape[1]), F32)
                pbuf_ref[0:halo, :] = jnp.zeros((halo, pbuf_ref.shape[1]), F32)

        @pl.when(jnp.logical_not(first))
        def _(p=p):
            xbuf_ref[0:halo, :] = carry_ref[p]
            if not has_buf:
                pbuf_ref[0:halo, :] = pcarry_ref[p]

        xbuf_ref[halo:halo + tm, :] = x_ref[...]
        carry_ref[p] = xbuf_ref[tm:tm + halo, :]
        if has_buf:
            y = _silu(_conv_taps(xbuf_ref, w_ref, tm, shift, halo, CONV_QKV))
        else:
            x0 = xbuf_ref[halo:halo + tm, :]
            x1 = xbuf_ref[halo - 1:halo - 1 + tm, :]
            pbuf_ref[halo:halo + tm, :] = w_ref[1:2, :] * x0 + w_ref[0:1, :] * x1
            pcarry_ref[p] = pbuf_ref[tm:tm + halo, :]
            y = _silu((w_ref[3:4, :] * x0 + w_ref[2:3, :] * x1) + pbuf_ref[halo - 2:halo - 2 + tm, :])
        if scale is None:
            o_ref[...] = y
        else:
            for h in range(n_heads):
                sl = slice(h * GROUP, (h + 1) * GROUP)
                yh = y[:, sl]
                inv = lax.rsqrt(jnp.sum(yh * yh, axis=-1, keepdims=True) + 1e-6)
                o_ref[:, sl] = yh * (inv * scale if scale != 1.0 else inv)


def _qkv_prep(h, conv_w, bufs, n_seq, rows_per_seq, tm, shift, col0):
    d_b = conv_w.shape[1] // 3
    n_heads = d_b // GROUP
    has_buf = bufs is not None
    assert has_buf or (shift == 1 and CONV_QKV == 4), "the fresh-sequence path pairs 4 taps at row shift 1"
    halo = (CONV_QKV - 1) * shift if has_buf else SUBLANES
    tiles = rows_per_seq // tm
    cb = col0 // d_b
    x_specs = [pl.BlockSpec((tm, d_b), lambda s, i, c=c: (s * tiles + i, cb + c)) for c in range(3)]
    w_specs = [pl.BlockSpec((CONV_QKV, d_b), lambda s, i, c=c: (0, c)) for c in range(3)]
    o_spec = pl.BlockSpec((tm, d_b), lambda s, i: (s * tiles + i, 0))
    args, specs = [h, h, h], list(x_specs)
    if has_buf:
        specs += [pl.BlockSpec((halo, d_b), lambda s, i, c=c: (s, c)) for c in range(3)]
        args += [bufs, bufs, bufs]
    specs += w_specs
    args += [conv_w, conv_w, conv_w]
    m = n_seq * rows_per_seq
    return pl.pallas_call(
        functools.partial(_qkv_prep_kernel, tm=tm, shift=shift, halo=halo, n_heads=n_heads, has_buf=has_buf),
        grid=(n_seq, tiles),
        in_specs=specs,
        out_specs=[o_spec, o_spec, o_spec],
        out_shape=[jax.ShapeDtypeStruct((m, d_b), F32)] * 3,
        scratch_shapes=[pltpu.VMEM((halo + tm, d_b), F32), pltpu.VMEM((3, halo, d_b), F32)] * (1 if has_buf else 2),
        compiler_params=_params(("arbitrary", "arbitrary")),
        name="qkv_prep",
    )(*args)


def _gates(ab, alog, dtb):
    g = -jnp.exp(alog) * jax.nn.softplus(ab + dtb)
    beta = jax.nn.sigmoid(ab)
    return g, beta


def _delta_prompt_kernel(q_ref, k_ref, v_ref, z_ref, ab_ref, alog_ref, dtb_ref, ng_ref,
                         yb_ref, sout_ref, s_ref, *, n_heads, n_chunks, ns):
    c = DN_CHUNK
    ci = pl.program_id(1)

    @pl.when(ci == 0)
    def _():
        s_ref[...] = jnp.zeros(s_ref.shape, F32)

    row = lax.broadcasted_iota(jnp.int32, (c, c), 0)
    col = lax.broadcasted_iota(jnp.int32, (c, c), 1)
    causal = row >= col
    strict = row > col
    eye = (row == col).astype(F32)
    tril = causal.astype(F32)
    n_double = (c - 1).bit_length() - 1
    ch = [(s, h) for s in range(ns) for h in range(n_heads)]
    sl = lambda h: slice(h * GROUP, (h + 1) * GROUP)
    gates = [_gates(ab_ref[s], alog_ref[...], dtb_ref[...]) for s in range(ns)]
    gc_s = [jnp.dot(tril, gates[s][0], precision=lax.Precision.HIGHEST, preferred_element_type=F32)
            for s in range(ns)]
    gct_s = [g.T for g in gc_s]
    q = [q_ref[s, :, sl(h)] for s, h in ch]
    k = [k_ref[s, :, sl(h)] for s, h in ch]
    gcol = [gc_s[s][:, h:h + 1] for s, h in ch]
    beta = [gates[s][1][:, n_heads + h:n_heads + h + 1] for s, h in ch]
    n = range(len(ch))
    decay = [jnp.exp(jnp.where(causal, gcol[i] - gct_s[s][h:h + 1, :], -jnp.inf)) for i, (s, h) in enumerate(ch)]
    kb = [k[i] * beta[i] for i in n]
    kbf = [k[i].astype(BF16) for i in n]
    kq = [lax.dot_general(jnp.concatenate([kb[i], q[i]], axis=0).astype(BF16), kbf[i], _NT,
                          preferred_element_type=F32) for i in n]
    qk = [kq[i][c:] * decay[i] for i in n]
    y = [jnp.where(strict, -(kq[i][:c] * decay[i]), 0.0) for i in n]
    x = [eye + y[i] for i in n]
    y = [_bdot(y[i], y[i]) for i in n]
    for _ in range(n_double - 1):
        t = [_bdot(jnp.concatenate([x[i], y[i]], axis=0), y[i]) for i in n]
        x = [x[i] + t[i][:c] for i in n]
        y = [t[i][c:] for i in n]
    x = [x[i] + _bdot(x[i], y[i]) for i in n]
    egc = [jnp.exp(gcol[i]) for i in n]
    sol = [_bdot(x[i], jnp.concatenate([v_ref[s, :, sl(h)] * beta[i], kb[i] * egc[i]], axis=1))
           for i, (s, h) in enumerate(ch)]
    s_old = [s_ref[s, h] for s, h in ch]
    r = [_bdot(jnp.concatenate([sol[i][:, GROUP:], q[i] * egc[i]], axis=0), s_old[i]) for i in n]
    v_new = [sol[i][:, :GROUP] - r[i][:c] for i in n]
    o = [r[i][c:] + _bdot(qk[i], v_new[i]) for i in n]
    gl = [gc_s[s][c - 1:c, h:h + 1] for s, h in ch]
    upd = [lax.dot_general((k[i] * jnp.exp(gl[i] - gcol[i])).astype(BF16), v_new[i].astype(BF16), _TN,
                           preferred_element_type=F32) for i in n]
    ms = [jnp.mean(o[i] * o[i], axis=-1, keepdims=True) for i in n]
    for i, (s, h) in enumerate(ch):
        s_ref[s, h] = s_old[i] * jnp.exp(gl[i]) + upd[i]
        on = o[i] * lax.rsqrt(ms[i] + EPS) * ng_ref[...]
        yb_ref[s, :, sl(h)] = (on * _silu(z_ref[s, :, sl(h)])).astype(yb_ref.dtype)

    @pl.when(ci == n_chunks - 1)
    def _():
        sout_ref[...] = s_ref[...]


def _delta_prompt(q, k, v, h, z_col0, ab, alog, dtb, norm_g, n_seq, rows_per_seq, ns):
    d_b = q.shape[1]
    n_heads = d_b // GROUP
    n_chunks = rows_per_seq // DN_CHUNK
    r3 = lambda t: t.reshape(n_seq, rows_per_seq, t.shape[-1])
    act = pl.BlockSpec((ns, DN_CHUNK, d_b), lambda s, i: (s, i, 0))
    vec = pl.BlockSpec((1, LANES), lambda s, i: (0, 0))
    zb = z_col0 // d_b
    yb, s_new = pl.pallas_call(
        functools.partial(_delta_prompt_kernel, n_heads=n_heads, n_chunks=n_chunks, ns=ns),
        grid=(n_seq // ns, n_chunks),
        in_specs=[act, act, act,
                  pl.BlockSpec((ns, DN_CHUNK, d_b), lambda s, i: (s, i, zb)),
                  pl.BlockSpec((ns, DN_CHUNK, LANES), lambda s, i: (s, i, 0)),
                  vec, vec, vec],
        out_specs=[act, pl.BlockSpec((ns, n_heads, GROUP, GROUP), lambda s, i: (s, 0, 0, 0))],
        out_shape=[jax.ShapeDtypeStruct((n_seq, rows_per_seq, d_b), BF16),
                   jax.ShapeDtypeStruct((n_seq, n_heads, GROUP, GROUP), F32)],
        scratch_shapes=[pltpu.VMEM((ns, n_heads, GROUP, GROUP), F32)],
        compiler_params=_params(("arbitrary", "arbitrary")),
        name="delta_prompt",
    )(r3(q), r3(k), r3(v), r3(h), r3(ab), alog, dtb, norm_g)
    return yb.reshape(n_seq * rows_per_seq, d_b), s_new


def _delta_sample_kernel(q_ref, k_ref, v_ref, z_ref, ab_ref, alog_ref, dtb_ref, ng_ref, s0_ref,
                         yb_ref, s1_ref, lhs_ref, res_ref, kt_ref, vn_ref, gl_ref, *, n_heads, n_pos, bb):
    nt = n_pos
    hs = range(n_heads)
    sls = [slice(h * GROUP, (h + 1) * GROUP) for h in hs]
    gates = [_gates(ab_ref[t], alog_ref[...], dtb_ref[...]) for t in range(nt)]
    zeros_pad = jnp.zeros((bb, GROUP), F32)
    ts = range(nt)
    q = [[q_ref[t, :, sls[h]] for t in ts] for h in hs]
    k = [[k_ref[t, :, sls[h]] for t in ts] for h in hs]
    beta = [[gates[t][1][:, n_heads + h:n_heads + h + 1] for t in ts] for h in hs]
    gc = []
    for h in hs:
        g = [gates[t][0][:, h:h + 1] for t in ts]
        for t in range(1, nt):
            g[t] = g[t - 1] + g[t]
        gc.append(g)
    kb = [[k[h][t] * beta[h][t] for t in ts] for h in hs]
    kk = [[[jnp.sum(kb[h][t] * k[h][s], axis=-1, keepdims=True) for s in range(t)] for t in ts] for h in hs]
    qk = [[[jnp.sum(q[h][t] * k[h][s], axis=-1, keepdims=True) for s in range(t + 1)] for t in ts] for h in hs]
    dec = [[[jnp.exp(gc[h][t] - gc[h][s]) for s in range(t + 1)] for t in ts] for h in hs]
    a = [[[kk[h][t][s] * dec[h][t][s] for s in range(t)] for t in ts] for h in hs]
    qk = [[[qk[h][t][s] * dec[h][t][s] for s in range(t + 1)] for t in ts] for h in hs]
    egc = [[jnp.exp(gc[h][t]) for t in ts] for h in hs]
    u_blk = [[v_ref[t, :, sls[h]] * beta[h][t] for t in ts] for h in hs]
    w_blk = [[kb[h][t] * egc[h][t] for t in ts] for h in hs]
    for t in ts:
        for h in hs:
            for s in range(t):
                u_blk[h][t] = u_blk[h][t] - a[h][t][s] * u_blk[h][s]
                w_blk[h][t] = w_blk[h][t] - a[h][t][s] * w_blk[h][s]
    for h in hs:
        for t in ts:
            lhs_ref[h, :, t, :] = w_blk[h][t]
            lhs_ref[h, :, nt + t, :] = q[h][t] * egc[h][t]
        gl_ref[h] = jnp.broadcast_to(egc[h][nt - 1], (bb, GROUP))

    for h in hs:
        for b in range(bb):
            res_ref[h, b] = _bdot(lhs_ref[h, b], s0_ref[b, h])

    v_new = [[u_blk[h][t] - res_ref[h, :, t, :] for t in ts] for h in hs]
    o = [[res_ref[h, :, nt + t, :] for t in ts] for h in hs]
    for h in hs:
        for t in ts:
            for s in range(t + 1):
                o[h][t] = o[h][t] + qk[h][t][s] * v_new[h][s]
    ms = [[jnp.mean(o[h][t] * o[h][t], axis=-1, keepdims=True) for t in ts] for h in hs]
    for h in hs:
        gl = gc[h][nt - 1]
        for t in ts:
            on = o[h][t] * lax.rsqrt(ms[h][t] + EPS) * ng_ref[...]
            yb_ref[t, :, sls[h]] = (on * _silu(z_ref[t, :, sls[h]])).astype(yb_ref.dtype)
            kt_ref[h, :, t, :] = k[h][t] * jnp.exp(gl - gc[h][t])
            vn_ref[h, :, t, :] = v_new[h][t]
            kt_ref[h, :, nt + t, :] = zeros_pad
            vn_ref[h, :, nt + t, :] = zeros_pad

    for h in hs:
        for b in range(bb):
            upd = lax.dot_general(kt_ref[h, b].astype(BF16), vn_ref[h, b].astype(BF16), _TN,
                                  preferred_element_type=F32)
            s1_ref[b, h] = s0_ref[b, h] * gl_ref[h, b:b + 1, :] + upd


def _delta_sample(q, k, v, h3, z_col0, ab, alog, dtb, norm_g, s0, bb):
    n_pos, nb, d_b = q.shape
    n_heads = d_b // GROUP
    assert 2 * n_pos == SUBLANES, "one state read packs [w | q] rows of a sequence into one 8-row tile"
    zb = z_col0 // d_b
    act = pl.BlockSpec((n_pos, bb, d_b), lambda i: (0, i, 0))
    vec = pl.BlockSpec((1, LANES), lambda i: (0, 0))
    st = pl.BlockSpec((bb, n_heads, GROUP, GROUP), lambda i: (i, 0, 0, 0))
    tile = pltpu.VMEM((n_heads, bb, SUBLANES, GROUP), F32)
    return pl.pallas_call(
        functools.partial(_delta_sample_kernel, n_heads=n_heads, n_pos=n_pos, bb=bb),
        grid=(nb // bb,),
        in_specs=[act, act, act,
                  pl.BlockSpec((n_pos, bb, d_b), lambda i: (0, i, zb)),
                  pl.BlockSpec((n_pos, bb, LANES), lambda i: (0, i, 0)),
                  vec, vec, vec, st],
        out_specs=[act, st],
        out_shape=[jax.ShapeDtypeStruct((n_pos, nb, d_b), BF16),
                   jax.ShapeDtypeStruct(s0.shape, F32)],
        scratch_shapes=[tile, tile, tile, tile, pltpu.VMEM((n_heads, bb, GROUP), F32)],
        compiler_params=_params(("arbitrary",)),
        name="delta_sample",
    )(q, k, v, h3, ab, alog, dtb, norm_g, s0)


RESID_SLOTS = 3


def _out_proj_kernel(ya_ref, yb_ref, w_ref, r_hbm, g_ref, b_ref, o_ref, ob_ref, z_ref, rbuf_ref, rsem,
                     *, n_tiles, d_a, alpha):
    i = pl.program_id(0)
    tm = z_ref.shape[0]

    def resid_copy(t):
        slot = t % RESID_SLOTS
        return pltpu.make_async_copy(r_hbm.at[pl.ds(t * tm, tm), :], rbuf_ref.at[slot], rsem.at[slot])

    @pl.when(i == 0)
    def _():
        z_ref[...] = jnp.zeros(z_ref.shape, F32)
        for t in range(min(RESID_SLOTS - 1, n_tiles)):
            resid_copy(t).start()

    @pl.when(i + RESID_SLOTS - 1 < n_tiles)
    def _():
        resid_copy(i + RESID_SLOTS - 1).start()

    def emit_ln():
        y = _layer_norm(z_ref[...], g_ref[...], b_ref[...])
        o_ref[...] = y
        ob_ref[...] = y.astype(ob_ref.dtype)

    @pl.when(i < n_tiles)
    def _():
        emit_ln()
        acc = (jnp.dot(ya_ref[...], w_ref[0:d_a, :], preferred_element_type=F32)
               + jnp.dot(yb_ref[...], w_ref[d_a:, :], preferred_element_type=F32))
        resid_copy(i).wait()
        z_ref[...] = alpha * rbuf_ref[i % RESID_SLOTS] + acc

    @pl.when(i == n_tiles)
    def _():
        emit_ln()


def _out_proj_ln(ya, yb, w, resid, g, b, alpha, tm):
    m, d_a = ya.shape
    k, d = w.shape
    n_tiles = m // tm
    cur = lambda i: (jnp.minimum(i, n_tiles - 1), 0)
    prev = lambda i: (jnp.maximum(i - 1, 0), 0)
    vec = pl.BlockSpec((1, d), lambda i: (0, 0))
    return pl.pallas_call(
        functools.partial(_out_proj_kernel, n_tiles=n_tiles, d_a=d_a, alpha=alpha),
        grid=(n_tiles + 1,),
        in_specs=[pl.BlockSpec((tm, d_a), cur),
                  pl.BlockSpec((tm, k - d_a), cur),
                  pl.BlockSpec((k, d), lambda i: (0, 0), pipeline_mode=pl.Buffered(1)),
                  pl.BlockSpec(memory_space=pl.ANY), vec, vec],
        out_specs=[pl.BlockSpec((tm, d), prev), pl.BlockSpec((tm, d), prev)],
        out_shape=[jax.ShapeDtypeStruct((m, d), F32), jax.ShapeDtypeStruct((m, d), BF16)],
        scratch_shapes=[pltpu.VMEM((tm, d), F32), pltpu.VMEM((RESID_SLOTS, tm, d), F32),
                        pltpu.SemaphoreType.DMA((RESID_SLOTS,))],
        compiler_params=_params(("arbitrary",)),
        name="out_proj_ln1",
    )(ya, yb, w, resid, g.reshape(1, d), b.reshape(1, d))


def _down_kernel(x_ref, w_ref, r_ref, g_ref, b_ref, o_ref, *rest, n_tiles, nk, nc, alpha, n_pos):
    z_ref = rest[-1]
    cw = z_ref.shape[-1]
    slab = lambda c: slice(c * cw, (c + 1) * cw)
    i, kk = pl.program_id(0), pl.program_id(1)

    def emit_ln():
        z = [z_ref[c] for c in range(nc)]
        inv_d = 1.0 / (nc * cw)
        mu = sum(jnp.sum(zc, axis=-1, keepdims=True) for zc in z) * inv_d
        xc = [zc - mu for zc in z]
        var = sum(jnp.sum(c * c, axis=-1, keepdims=True) for c in xc) * inv_d
        rs = lax.rsqrt(var + EPS)
        for c in range(nc):
            y = xc[c] * rs * g_ref[:, slab(c)] + b_ref[:, slab(c)]
            if n_pos:
                nb = y.shape[0] // n_pos
                for t in range(n_pos):
                    o_ref[:, t, slab(c)] = y[t * nb:(t + 1) * nb]
            else:
                o_ref[:, slab(c)] = y

    @pl.when(jnp.logical_and(i == 0, kk == 0))
    def _():
        z_ref[...] = jnp.zeros(z_ref.shape, F32)

    wb = w_ref[...].astype(BF16)
    if len(rest) == 2:
        rest[0][...] = wb

    @pl.when(kk == 0)
    def _():
        emit_ln()
        part = jnp.dot(x_ref[...], wb, preferred_element_type=F32)
        for c in range(nc):
            if nc == nk:
                z_ref[c] = part[:, slab(c)] + alpha * r_ref[...] if c == 0 else part[:, slab(c)]
            else:
                z_ref[c] = part[:, slab(c)] + alpha * r_ref[:, slab(c)]

    @pl.when(jnp.logical_and(kk > 0, i < n_tiles))
    def _():
        part = jnp.dot(x_ref[...], wb, preferred_element_type=F32)
        for c in range(nc):
            z_ref[c] += part[:, slab(c)]
        if nc == nk:
            z_ref[kk] += alpha * r_ref[...]


def _down_resident_kernel(x_ref, w_ref, r_ref, g_ref, b_ref, o_ref, z_ref, *, n_tiles, alpha):
    i = pl.program_id(0)

    @pl.when(i == 0)
    def _():
        z_ref[...] = jnp.zeros(z_ref.shape, F32)

    @pl.when(i < n_tiles)
    def _():
        o_ref[...] = _layer_norm(z_ref[...], g_ref[...], b_ref[...])
        z_ref[...] = alpha * r_ref[...] + jnp.dot(x_ref[...], w_ref[...], preferred_element_type=F32)

    @pl.when(i == n_tiles)
    def _():
        o_ref[...] = _layer_norm(z_ref[...], g_ref[...], b_ref[...])


def _down_ln_resident(x, w, resid, g, b, alpha, tm):
    m, kdim = x.shape
    d = w.shape[1]
    n_tiles = m // tm
    cur = lambda i: (jnp.minimum(i, n_tiles - 1), 0)
    vec = pl.BlockSpec((1, d), lambda i: (0, 0))
    return pl.pallas_call(
        functools.partial(_down_resident_kernel, n_tiles=n_tiles, alpha=alpha),
        grid=(n_tiles + 1,),
        in_specs=[pl.BlockSpec((tm, kdim), cur),
                  pl.BlockSpec((kdim, d), lambda i: (0, 0), pipeline_mode=pl.Buffered(1)),
                  pl.BlockSpec((tm, d), cur), vec, vec],
        out_specs=pl.BlockSpec((tm, d), lambda i: (jnp.maximum(i - 1, 0), 0)),
        out_shape=jax.ShapeDtypeStruct((m, d), F32),
        scratch_shapes=[pltpu.VMEM((tm, d), F32)],
        compiler_params=_params(("arbitrary",), BIG_VMEM_LIMIT),
        name="down_ln2_resident",
    )(x, w, resid, g.reshape(1, d), b.reshape(1, d))


def _down_ln(x, w, resid, g, b, alpha, tm, tk, emit_w=False, n_pos=0):
    m, kdim = x.shape
    d = w.shape[1]
    nk = kdim // tk
    n_tiles = m // tm
    nc = nk if (d % nk == 0 and (d // nk) % LANES == 0) else 1
    row = lambda i: jnp.minimum(i, n_tiles - 1)
    kt = lambda i, kk: jnp.where(i == n_tiles, nk - 1, kk)
    vec = pl.BlockSpec((1, d), lambda i, kk: (0, 0))
    wspec = pl.BlockSpec((tk, d), lambda i, kk: (kt(i, kk), 0))
    if nc == nk:
        rspec = pl.BlockSpec((tm, d // nc), lambda i, kk: (row(i), kt(i, kk)))
    else:
        rspec = pl.BlockSpec((tm, d), lambda i, kk: (row(i), 0))
    if n_pos:
        assert n_tiles == 1
        out_specs = [pl.BlockSpec((m // n_pos, n_pos, d), lambda i, kk: (0, 0, 0))]
        out_shape = [jax.ShapeDtypeStruct((m // n_pos, n_pos, d), F32)]
    else:
        out_specs = [pl.BlockSpec((tm, d), lambda i, kk: (jnp.where(kk == 0, jnp.maximum(i - 1, 0), row(i)), 0))]
        out_shape = [jax.ShapeDtypeStruct((m, d), F32)]
    if emit_w:
        out_specs.append(wspec)
        out_shape.append(jax.ShapeDtypeStruct(w.shape, BF16))
    outs = pl.pallas_call(
        functools.partial(_down_kernel, n_tiles=n_tiles, nk=nk, nc=nc, alpha=alpha, n_pos=n_pos),
        grid=(n_tiles + 1, nk),
        in_specs=[pl.BlockSpec((tm, tk), lambda i, kk: (row(i), kt(i, kk))), wspec, rspec, vec, vec],
        out_specs=out_specs,
        out_shape=out_shape,
        scratch_shapes=[pltpu.VMEM((nc, tm, d // nc), F32)],
        compiler_params=_params(("arbitrary", "arbitrary"), BIG_VMEM_LIMIT),
        name="down_ln2",
    )(x, w, resid, g.reshape(1, d), b.reshape(1, d))
    return tuple(outs) if emit_w else outs[0]


def _ffn_up_kernel(x_ref, wg_ref, wv_ref, cwg_ref, cwv_ref, cbg_ref, cbv_ref,
                   act_ref, tg_ref, tv_ref, wgo_ref, wvo_ref, wgb_ref, wvb_ref, carry_ref, *, tm):
    halo = SUBLANES
    first = pl.program_id(2) == 0

    @pl.when(jnp.logical_and(pl.program_id(1) == 0, first))
    def _():
        for w_ref, wb_ref, wo_ref in ((wg_ref, wgb_ref, wgo_ref), (wv_ref, wvb_ref, wvo_ref)):
            wb = w_ref[...].astype(BF16)
            wb_ref[...] = wb
            wo_ref[...] = wb

    @pl.when(first)
    def _():
        carry_ref[...] = jnp.zeros(carry_ref.shape, F32)

    x = x_ref[...]
    conv = []
    parts = ((wgb_ref, cwg_ref, cbg_ref, tg_ref), (wvb_ref, cwv_ref, cbv_ref, tv_ref))
    row = lax.broadcasted_iota(jnp.int32, (halo, act_ref.shape[1]), 0)
    for p, (w_ref, cw_ref, cb_ref, t_ref) in enumerate(parts):
        up = jnp.dot(x, w_ref[...], preferred_element_type=F32)
        hist = carry_ref[p]
        tail = up[tm - halo:tm, :]
        carry_ref[p] = tail
        t_ref[0] = tail
        acc = cw_ref[CONV_FFN - 1:CONV_FFN, :] * up
        for dd in range(1, CONV_FFN):
            xs = pltpu.roll(up, dd, axis=0)
            head = jnp.where(row < dd, pltpu.roll(hist, dd, axis=0), xs[0:halo])
            xs = jnp.concatenate([head, xs[halo:]], axis=0)
            acc = acc + cw_ref[CONV_FFN - 1 - dd:CONV_FFN - dd, :] * xs
        conv.append(acc + cb_ref[...])
    act_ref[...] = (_silu(conv[0]) * conv[1]).astype(act_ref.dtype)


def _ffn_up_carried_kernel(x_ref, wg_ref, wv_ref, buf_ref, cw_ref, cb_ref, act_ref, t_ref, xbuf_ref, gate_ref,
                           *, shift):
    p, j = pl.program_id(0), pl.program_id(1)
    tm, halo = x_ref.shape[0], buf_ref.shape[0]
    xbuf_ref[0:halo, :] = buf_ref[...]

    def conv_of(w_ref):
        xbuf_ref[halo:halo + tm, :] = jnp.dot(x_ref[...], w_ref[...], preferred_element_type=F32)
        tail = xbuf_ref[tm:tm + halo, :]
        for t in range(halo // shift):
            t_ref[:, t, :] = tail[t * shift:(t + 1) * shift]
        return _conv_taps(xbuf_ref, cw_ref, tm, shift, halo, CONV_FFN) + cb_ref[...]

    @pl.when(p == 0)
    def _():
        gate_ref[j] = conv_of(wg_ref)

    @pl.when(p == 1)
    def _():
        act_ref[...] = (_silu(gate_ref[j]) * conv_of(wv_ref)).astype(act_ref.dtype)


def _ffn_up_carried(x, w_gate_b, w_val_b, conv_w, conv_b, bufs, tn, shift):
    m, d = x.shape
    d_ff = w_gate_b.shape[1]
    nj = d_ff // tn
    halo = (CONV_FFN - 1) * shift
    col = lambda p, j: (0, p * nj + j)
    return pl.pallas_call(
        functools.partial(_ffn_up_carried_kernel, shift=shift),
        grid=(2, nj),
        in_specs=[pl.BlockSpec((m, d), lambda p, j: (0, 0)),
                  pl.BlockSpec((d, tn), lambda p, j: (0, jnp.where(p == 0, j, nj - 1))),
                  pl.BlockSpec((d, tn), lambda p, j: (0, jnp.where(p == 1, j, 0))),
                  pl.BlockSpec((halo, tn), col),
                  pl.BlockSpec((CONV_FFN, tn), col),
                  pl.BlockSpec((1, tn), col)],
        out_specs=[pl.BlockSpec((m, tn), lambda p, j: (0, jnp.where(p == 1, j, 0))),
                   pl.BlockSpec((shift, halo // shift, tn), lambda p, j: (0, 0, p * nj + j))],
        out_shape=[jax.ShapeDtypeStruct((m, d_ff), BF16),
                   jax.ShapeDtypeStruct((shift, halo // shift, 2 * d_ff), F32)],
        scratch_shapes=[pltpu.VMEM((halo + m, tn), F32), pltpu.VMEM((nj, m, tn), F32)],
        compiler_params=_params(("arbitrary", "arbitrary")),
        name="ffn_up_carried",
    )(x, w_gate_b, w_val_b, bufs, conv_w, conv_b.reshape(1, 2 * d_ff))


def _ffn_up(x, w_up, conv_w, conv_b, n_seq, rows_per_seq, tm, tn):
    m, d = x.shape
    d_ff = w_up.shape[1] // 2
    nj = d_ff // tn
    halo = SUBLANES
    tiles = rows_per_seq // tm
    wspec = lambda off: pl.BlockSpec((d, tn), lambda j, s, i: (0, off + j))
    cwspec = lambda off: pl.BlockSpec((CONV_FFN, tn), lambda j, s, i: (0, off + j))
    cbspec = lambda off: pl.BlockSpec((1, tn), lambda j, s, i: (0, off + j))
    tspec = pl.BlockSpec((1, halo, tn), lambda j, s, i: (s, 0, j))
    tshape = jax.ShapeDtypeStruct((n_seq, halo, d_ff), F32)
    cb2 = conv_b.reshape(1, 2 * d_ff)
    return pl.pallas_call(
        functools.partial(_ffn_up_kernel, tm=tm),
        grid=(nj, n_seq, tiles),
        in_specs=[pl.BlockSpec((tm, d), lambda j, s, i: (s * tiles + i, 0)), wspec(0), wspec(nj),
                  cwspec(0), cwspec(nj), cbspec(0), cbspec(nj)],
        out_specs=[pl.BlockSpec((tm, tn), lambda j, s, i: (s * tiles + i, j)), tspec, tspec, wspec(0), wspec(0)],
        out_shape=[jax.ShapeDtypeStruct((m, d_ff), BF16), tshape, tshape]
                  + [jax.ShapeDtypeStruct((d, d_ff), BF16)] * 2,
        scratch_shapes=[pltpu.VMEM((d, tn), BF16)] * 2 + [pltpu.VMEM((2, halo, tn), F32)],
        compiler_params=_params(("arbitrary", "arbitrary", "arbitrary")),
        name="ffn_up",
    )(x, w_up, w_up, conv_w, conv_w, cb2, cb2)


def _pad_lanes(vec, offset=0):
    out = jnp.zeros((1, LANES), F32)
    return out.at[0, offset:offset + vec.shape[0]].set(vec.astype(F32))


def _mixer_half(x2, lw, alpha, n_seq, rows_per_seq, shift, state, w_in_b):
    (w_in, conv_qkv_w, a_log, dt_bias, norm_g, gm_ln_g, gm_ln_b, gm_ws, gm_bs, gm_out_g, w_out,
     ln1_g, ln1_b) = lw
    m, d_model = x2.shape
    n_groups = gm_ws.shape[0]
    d_a = n_groups * GROUP
    d_qkv = conv_qkv_w.shape[1]
    d_b = d_qkv // 3
    n_heads = d_b // GROUP
    main_cols = 2 * d_a + d_qkv + d_b
    w_ab = jnp.pad(w_in[main_cols:].astype(BF16), ((0, LANES - 2 * n_heads), (0, 0)))
    if w_in_b is None:
        h, ab, w_in_b = _matmul(x2, w_in, w_ab, main_cols, min(m, PROJ_TM), PROJ_EMIT_TN, emit_w=True)
    else:
        h, ab = _matmul(x2, w_in_b, w_ab, main_cols, min(m, PROJ_TM), PROJ_TN)
    alog, dtb, ng = _pad_lanes(a_log), _pad_lanes(dt_bias), norm_g.reshape(1, GROUP).astype(F32)

    if state is None:
        ya = _chunk_mlp_prompt(h, gm_ln_g, gm_ln_b, gm_ws, gm_bs, gm_out_g)
        v_rows = None
        q, k, v = _qkv_prep(h, conv_qkv_w, None, n_seq, rows_per_seq, QKV_TM, shift, 2 * d_a)
        yb, s_new = _delta_prompt(q, k, v, h, 2 * d_a + d_qkv, ab, alog, dtb, ng, n_seq, rows_per_seq, DELTA_SEQS)
    else:
        s_dn, buf_qkv = state
        n_pos = rows_per_seq // shift
        h3 = h.reshape(n_pos, shift, main_cols)
        ya3, v_rows = _chunk_mlp_sample(h3, gm_ln_g, gm_ln_b, gm_ws, gm_bs, gm_out_g)
        ya = ya3.reshape(m, d_a)
        q, k, v = _qkv_prep(h, conv_qkv_w, buf_qkv, n_seq, rows_per_seq, rows_per_seq, shift, 2 * d_a)
        r3 = lambda t: t.reshape(n_pos, shift, t.shape[-1])
        yb3, s_new = _delta_sample(r3(q), r3(k), r3(v), h3, 2 * d_a + d_qkv, r3(ab), alog, dtb, ng, s_dn,
                                   DELTA_SAMPLE_ROWS)
        yb = yb3.reshape(m, d_b)

    x1, x1b = _out_proj_ln(ya, yb, w_out, x2, ln1_g, ln1_b, alpha, min(m, OUT_TM))
    return x1, x1b, s_new, h, v_rows, w_in_b


def kernel(x_prompt, x_sample, state_dn, state_conv_qkv, state_conv_ffn, w_in, conv_qkv_w, dn_a_log, dn_dt_bias, dn_norm_g, gm_ln_g, gm_ln_b, gm_ws, gm_bs, gm_out_g, w_out, ln1_g, ln1_b, w_up, conv_ffn_w, conv_ffn_b, w_down, ln2_g, ln2_b):
    depth = w_in.shape[0]
    bp, lp, d_model = x_prompt.shape
    bs, ls, _ = x_sample.shape
    d_qkv = conv_qkv_w.shape[-1]
    d_a = gm_ws.shape[1] * GROUP
    alpha = (2.0 * depth) ** 0.25

    hp = x_prompt.reshape(bp * lp, d_model)
    hs = jnp.transpose(x_sample, (1, 0, 2)).reshape(ls * bs, d_model)
    outs = [[] for _ in range(7)]
    w_out = w_out.astype(BF16)
    w_in = jnp.swapaxes(w_in, 1, 2)
    for l in range(depth):
        lw = tuple(t[l] for t in (w_in, conv_qkv_w, dn_a_log, dn_dt_bias, dn_norm_g, gm_ln_g, gm_ln_b, gm_ws,
                                  gm_bs, gm_out_g, w_out, ln1_g, ln1_b))
        ffn_w = (conv_ffn_w[l], conv_ffn_b[l])
        ln2 = (ln2_g[l], ln2_b[l], alpha)
        n_pos = ls if l == depth - 1 else 0
        bq = jnp.transpose(state_conv_qkv[l], (1, 0, 2)).reshape((CONV_QKV - 1) * bs, d_qkv)
        bf = jnp.transpose(state_conv_ffn[l], (1, 0, 2)).reshape((CONV_FFN - 1) * bs, -1)
        x1_s, x1b_s, s_s, h_s, v_s, w_in_b = _mixer_half(hs, lw, alpha, 1, ls * bs, bs, (state_dn[l], bq), None)
        x1_p, x1b_p, s_p, h_p, _, _ = _mixer_half(hp, lw, alpha, bp, lp, 1, None, w_in_b)
        act_p, tg_p, tv_p, w_gate_b, w_val_b = _ffn_up(x1b_p, w_up[l], *ffn_w, bp, lp, FFN_TM, FFN_TN)
        act_s, cf_s = _ffn_up_carried(x1b_s, w_gate_b, w_val_b, *ffn_w, bf, FFN_TN, bs)
        hs, w_down_b = _down_ln(act_s, w_down[l], x1_s, *ln2, min(ls * bs, DOWN_EMIT_TM), DOWN_EMIT_TK,
                                emit_w=True, n_pos=n_pos)
        hp = _down_ln_resident(act_p, w_down_b, x1_p, *ln2, min(bp * lp, DOWN_TM))
        h_s3 = h_s.reshape(ls, bs, -1)
        cq_s = jnp.transpose(h_s3[ls - (CONV_QKV - 1):, :, 2 * d_a:2 * d_a + d_qkv], (1, 0, 2))
        h_p3 = h_p.reshape(bp, lp, -1)
        cq_p = h_p3[:, lp - (CONV_QKV - 1):, 2 * d_a:2 * d_a + d_qkv]
        n_tail = tg_p.shape[1]
        cf_p = jnp.concatenate([tg_p, tv_p], axis=-1)[:, n_tail - (CONV_FFN - 1):]
        for acc, val in zip(outs, (s_p, cq_p, cf_p, s_s, cq_s, cf_s, v_s)):
            acc.append(val)
    return (hp.reshape(bp, lp, d_model), hs) + tuple(jnp.stack(o) for o in outs)
```

```python
import functools

import jax
import jax.numpy as jnp
from jax import lax
from jax.experimental import pallas as pl
from jax.experimental.pallas import tpu as pltpu

F32 = jnp.float32
BF16 = jnp.bfloat16

LANES = 128
SUBLANES = 8
VMEM_LIMIT = 56 * 1024 * 1024
BIG_VMEM_LIMIT = 62 * 1024 * 1024

GROUP = 128
MLP_CHUNK = 128
DN_CHUNK = 64
CONV_QKV = 4
CONV_FFN = 3
CMLP_CHUNKS = 8
EPS = 1e-5

PROJ_TM, PROJ_TN = 1024, 2048
PROJ_EMIT_TN = 1536
QKV_TM = 512
DELTA_SEQS = 4
DELTA_SAMPLE_ROWS = 16
OUT_TM = 512
FFN_TM, FFN_TN = 1024, 512
DOWN_TM = 512
DOWN_EMIT_TM, DOWN_EMIT_TK = 512, 1408

_NT = (((1,), (1,)), ((), ()))
_TN = (((0,), (0,)), ((), ()))


def _params(sem, vmem_limit=VMEM_LIMIT):
    return pltpu.CompilerParams(dimension_semantics=sem, vmem_limit_bytes=vmem_limit)


def _bdot(a, b):
    return jnp.dot(a.astype(BF16), b.astype(BF16), preferred_element_type=F32)


def _gelu(x):
    return 0.5 * x * (1.0 + lax.erf(x * (0.5 ** 0.5)))


def _silu(x):
    return (0.5 * x) * (1.0 + jnp.tanh(0.5 * x))


def _layer_norm(x, g, b):
    mu = jnp.mean(x, axis=-1, keepdims=True)
    xc = x - mu
    var = jnp.mean(xc * xc, axis=-1, keepdims=True)
    return xc * lax.rsqrt(var + EPS) * g + b


def _rms_norm(x, g):
    ms = jnp.mean(x * x, axis=-1, keepdims=True)
    return x * lax.rsqrt(ms + EPS) * g


def _mm_kernel(x_ref, w_ref, we_ref, o_ref, e_ref, *rest):
    xb_ref = rest[-1]

    @pl.when(pl.program_id(1) == 0)
    def _():
        xb = x_ref[...].astype(BF16)
        xb_ref[...] = xb
        e_ref[...] = lax.dot_general(xb, we_ref[...], _NT, preferred_element_type=F32)

    wb = w_ref[...].astype(BF16)
    if len(rest) == 2:
        rest[0][...] = wb
    o_ref[...] = lax.dot_general(xb_ref[...], wb, _NT, preferred_element_type=F32)


def _matmul(x, wt, wt_extra, n_cols, tm, tn, emit_w=False):
    m, k = x.shape
    ne = wt_extra.shape[0]
    out_specs = [pl.BlockSpec((tm, tn), lambda i, j: (i, j)), pl.BlockSpec((tm, ne), lambda i, j: (i, 0))]
    out_shape = [jax.ShapeDtypeStruct((m, n_cols), F32), jax.ShapeDtypeStruct((m, ne), F32)]
    if emit_w:
        out_specs.append(pl.BlockSpec((tn, k), lambda i, j: (j, 0)))
        out_shape.append(jax.ShapeDtypeStruct((n_cols, k), BF16))
    return pl.pallas_call(
        _mm_kernel,
        grid=(m // tm, n_cols // tn),
        in_specs=[pl.BlockSpec((tm, k), lambda i, j: (i, 0)),
                  pl.BlockSpec((tn, k), lambda i, j: (j, 0)),
                  pl.BlockSpec((ne, k), lambda i, j: (0, 0))],
        out_specs=out_specs,
        out_shape=out_shape,
        scratch_shapes=[pltpu.VMEM((tm, k), BF16)],
        compiler_params=_params(("arbitrary", "arbitrary"), BIG_VMEM_LIMIT),
        name="proj",
    )(x, wt, wt_extra)


def _cmlp_prompt_kernel(u_ref, v_ref, lng_ref, lnb_ref, ws_ref, bst_ref, og_ref, ya_ref, *, n_groups, n_chunks):
    row = lax.broadcasted_iota(jnp.int32, (MLP_CHUNK, MLP_CHUNK), 0)
    col = lax.broadcasted_iota(jnp.int32, (MLP_CHUNK, MLP_CHUNK), 1)
    cg = [(c, g) for c in range(n_chunks) for g in range(n_groups)]
    rows = lambda c: slice(c * MLP_CHUNK, (c + 1) * MLP_CHUNK)
    cols = lambda g: slice(g * GROUP, (g + 1) * GROUP)
    w = [jnp.where(row >= col, ws_ref[g], 0.0).astype(BF16) for g in range(n_groups)]
    n = range(len(cg))
    gv = [_gelu(v_ref[rows(c), cols(g)]) for c, g in cg]
    mu = [jnp.mean(gv[i], axis=-1, keepdims=True) for i in n]
    xc = [gv[i] - mu[i] for i in n]
    var = [jnp.mean(xc[i] * xc[i], axis=-1, keepdims=True) for i in n]
    vh = [xc[i] * lax.rsqrt(var[i] + EPS) * lng_ref[:, cols(g)] + lnb_ref[:, cols(g)] for i, (c, g) in enumerate(cg)]
    mixed = [jnp.dot(w[g], vh[i].astype(BF16), preferred_element_type=F32) + bst_ref[:, g:g + 1]
             for i, (c, g) in enumerate(cg)]
    y = [_gelu(u_ref[rows(c), cols(g)]) * mixed[i] for i, (c, g) in enumerate(cg)]
    ms = [jnp.mean(y[i] * y[i], axis=-1, keepdims=True) for i in n]
    for i, (c, g) in enumerate(cg):
        ya_ref[rows(c), cols(g)] = (y[i] * lax.rsqrt(ms[i] + EPS) * og_ref[:, cols(g)]).astype(ya_ref.dtype)


def _chunk_mlp_prompt(h, ln_g, ln_b, ws, bs, out_g):
    m = h.shape[0]
    n_groups, d_a = ws.shape[0], ws.shape[0] * GROUP
    vec = pl.BlockSpec((1, d_a), lambda i: (0, 0))
    tm = CMLP_CHUNKS * MLP_CHUNK
    return pl.pallas_call(
        functools.partial(_cmlp_prompt_kernel, n_groups=n_groups, n_chunks=CMLP_CHUNKS),
        grid=(m // tm,),
        in_specs=[pl.BlockSpec((tm, d_a), lambda i: (i, 0)),
                  pl.BlockSpec((tm, d_a), lambda i: (i, 1)),
                  vec, vec,
                  pl.BlockSpec((n_groups, MLP_CHUNK, MLP_CHUNK), lambda i: (0, 0, 0)),
                  pl.BlockSpec((MLP_CHUNK, n_groups), lambda i: (0, 0)),
                  vec],
        out_specs=pl.BlockSpec((tm, d_a), lambda i: (i, 0)),
        out_shape=jax.ShapeDtypeStruct((m, d_a), BF16),
        compiler_params=_params(("arbitrary",)),
        name="chunk_mlp_prompt",
    )(h, h, ln_g.reshape(1, d_a), ln_b.reshape(1, d_a), ws, bs.T, out_g.reshape(1, d_a))


def _cmlp_sample_kernel(u_ref, v_ref, lng_ref, lnb_ref, wv_ref, bv_ref, og_ref, ya_ref, vr_ref, *, n_pos):
    vh = []
    for t in range(n_pos):
        vt = _layer_norm(_gelu(v_ref[t]), lng_ref[0], lnb_ref[0])
        vr_ref[:, t, :] = vt
        vh.append(vt)
    for t in range(n_pos):
        mixed = wv_ref[t, 0:1, :] * vh[0]
        for s in range(1, t + 1):
            mixed = mixed + wv_ref[t, s:s + 1, :] * vh[s]
        mixed = mixed + bv_ref[0, t:t + 1, :]
        y = _gelu(u_ref[t]) * mixed
        ya_ref[t] = _rms_norm(y, og_ref[0]).astype(ya_ref.dtype)


def _chunk_mlp_sample(h3, ln_g, ln_b, ws, bs, out_g):
    n_pos, nb, _ = h3.shape
    n_groups = ws.shape[0]
    d_a = n_groups * GROUP
    wv = jnp.repeat(jnp.transpose(ws[:, :n_pos, :n_pos], (1, 2, 0)), GROUP, axis=-1)
    bv = jnp.repeat(bs[:, :n_pos].T, GROUP, axis=-1).reshape(1, n_pos, d_a)
    vec = pl.BlockSpec((1, 1, GROUP), lambda g: (0, 0, g))
    act = pl.BlockSpec((n_pos, nb, GROUP), lambda g: (0, 0, g))
    return pl.pallas_call(
        functools.partial(_cmlp_sample_kernel, n_pos=n_pos),
        grid=(n_groups,),
        in_specs=[act,
                  pl.BlockSpec((n_pos, nb, GROUP), lambda g: (0, 0, n_groups + g)),
                  vec, vec,
                  pl.BlockSpec((n_pos, n_pos, GROUP), lambda g: (0, 0, g)),
                  pl.BlockSpec((1, n_pos, GROUP), lambda g: (0, 0, g)),
                  vec],
        out_specs=[act, pl.BlockSpec((nb, n_pos, GROUP), lambda g: (0, 0, g))],
        out_shape=[jax.ShapeDtypeStruct((n_pos, nb, d_a), BF16),
                   jax.ShapeDtypeStruct((nb, n_pos, d_a), F32)],
        compiler_params=_params(("arbitrary",)),
        name="chunk_mlp_sample",
    )(h3, h3, ln_g.reshape(1, 1, d_a), ln_b.reshape(1, 1, d_a), wv, bv, out_g.reshape(1, 1, d_a))


def _conv_taps(xbuf_ref, w_ref, tm, shift, halo, width):
    acc = None
    for j in range(width):
        off = halo - (width - 1 - j) * shift
        term = w_ref[j:j + 1, :] * xbuf_ref[off:off + tm, :]
        acc = term if acc is None else acc + term
    return acc


def _qkv_prep_kernel(*refs, tm, shift, halo, n_heads, has_buf):
    if has_buf:
        (xq_ref, xk_ref, xv_ref, bq_ref, bk_ref, bv_ref, wq_ref, wk_ref, wv_ref,
         q_ref, k_ref, v_ref, xbuf_ref, carry_ref) = refs
        bufs = (bq_ref, bk_ref, bv_ref)
    else:
        (xq_ref, xk_ref, xv_ref, wq_ref, wk_ref, wv_ref,
         q_ref, k_ref, v_ref, xbuf_ref, carry_ref, pbuf_ref, pcarry_ref) = refs
        bufs = (None, None, None)
    first = pl.program_id(1) == 0
    parts = ((xq_ref, wq_ref, q_ref, GROUP ** -0.5), (xk_ref, wk_ref, k_ref, 1.0), (xv_ref, wv_ref, v_ref, None))
    for p, (x_ref, w_ref, o_ref, scale) in enumerate(parts):
        @pl.when(first)
        def _(p=p):
            if has_buf:
                xbuf_ref[0:halo, :] = bufs[p][...]
            else:
                xbuf_ref[0:halo, :] = jnp.zeros((halo, xbuf_ref.shape[1]), F32)
                pbuf_ref[0:halo, :] = jnp.zeros((halo, pbuf_ref.shape[1]), F32)

        @pl.when(jnp.logical_not(first))
        def _(p=p):
            xbuf_ref[0:halo, :] = carry_ref[p]
            if not has_buf:
                pbuf_ref[0:halo, :] = pcarry_ref[p]

        xbuf_ref[halo:halo + tm, :] = x_ref[...]
        carry_ref[p] = xbuf_ref[tm:tm + halo, :]
        if has_buf:
            y = _silu(_conv_taps(xbuf_ref, w_ref, tm, shift, halo, CONV_QKV))
        else:
            x0 = xbuf_ref[halo:halo + tm, :]
            x1 = xbuf_ref[halo - 1:halo - 1 + tm, :]
            pbuf_ref[halo:halo + tm, :] = w_ref[1:2, :] * x0 + w_ref[0:1, :] * x1
            pcarry_ref[p] = pbuf_ref[tm:tm + halo, :]
            y = _silu((w_ref[3:4, :] * x0 + w_ref[2:3, :] * x1) + pbuf_ref[halo - 2:halo - 2 + tm, :])
        if scale is None:
            o_ref[...] = y
        else:
            for h in range(n_heads):
                sl = slice(h * GROUP, (h + 1) * GROUP)
                yh = y[:, sl]
                inv = lax.rsqrt(jnp.sum(yh * yh, axis=-1, keepdims=True) + 1e-6)
                o_ref[:, sl] = yh * (inv * scale if scale != 1.0 else inv)


def _qkv_prep(h, conv_w, bufs, n_seq, rows_per_seq, tm, shift, col0):
    d_b = conv_w.shape[1] // 3
    n_heads = d_b // GROUP
    has_buf = bufs is not None
    assert has_buf or (shift == 1 and CONV_QKV == 4), "the fresh-sequence path pairs 4 taps at row shift 1"
    halo = (CONV_QKV - 1) * shift if has_buf else SUBLANES
    tiles = rows_per_seq // tm
    cb = col0 // d_b
    x_specs = [pl.BlockSpec((tm, d_b), lambda s, i, c=c: (s * tiles + i, cb + c)) for c in range(3)]
    w_specs = [pl.BlockSpec((CONV_QKV, d_b), lambda s, i, c=c: (0, c)) for c in range(3)]
    o_spec = pl.BlockSpec((tm, d_b), lambda s, i: (s * tiles + i, 0))
    args, specs = [h, h, h], list(x_specs)
    if has_buf:
        specs += [pl.BlockSpec((halo, d_b), lambda s, i, c=c: (s, c)) for c in range(3)]
        args += [bufs, bufs, bufs]
    specs += w_specs
    args += [conv_w, conv_w, conv_w]
    m = n_seq * rows_per_seq
    return pl.pallas_call(
        functools.partial(_qkv_prep_kernel, tm=tm, shift=shift, halo=halo, n_heads=n_heads, has_buf=has_buf),
        grid=(n_seq, tiles),
        in_specs=specs,
        out_specs=[o_spec, o_spec, o_spec],
        out_shape=[jax.ShapeDtypeStruct((m, d_b), F32)] * 3,
        scratch_shapes=[pltpu.VMEM((halo + tm, d_b), F32), pltpu.VMEM((3, halo, d_b), F32)] * (1 if has_buf else 2),
        compiler_params=_params(("arbitrary", "arbitrary")),
        name="qkv_prep",
    )(*args)


def _gates(ab, alog, dtb):
    g = -jnp.exp(alog) * jax.nn.softplus(ab + dtb)
    beta = jax.nn.sigmoid(ab)
    return g, beta


def _delta_prompt_kernel(q_ref, k_ref, v_ref, z_ref, ab_ref, alog_ref, dtb_ref, ng_ref,
                         yb_ref, sout_ref, s_ref, *, n_heads, n_chunks, ns):
    c = DN_CHUNK
    ci = pl.program_id(1)

    @pl.when(ci == 0)
    def _():
        s_ref[...] = jnp.zeros(s_ref.shape, F32)

    row = lax.broadcasted_iota(jnp.int32, (c, c), 0)
    col = lax.broadcasted_iota(jnp.int32, (c, c), 1)
    causal = row >= col
    strict = row > col
    eye = (row == col).astype(F32)
    tril = causal.astype(F32)
    n_double = (c - 1).bit_length() - 1
    ch = [(s, h) for s in range(ns) for h in range(n_heads)]
    sl = lambda h: slice(h * GROUP, (h + 1) * GROUP)
    gates = [_gates(ab_ref[s], alog_ref[...], dtb_ref[...]) for s in range(ns)]
    gc_s = [jnp.dot(tril, gates[s][0], precision=lax.Precision.HIGHEST, preferred_element_type=F32)
            for s in range(ns)]
    gct_s = [g.T for g in gc_s]
    q = [q_ref[s, :, sl(h)] for s, h in ch]
    k = [k_ref[s, :, sl(h)] for s, h in ch]
    gcol = [gc_s[s][:, h:h + 1] for s, h in ch]
    beta = [gates[s][1][:, n_heads + h:n_heads + h + 1] for s, h in ch]
    n = range(len(ch))
    decay = [jnp.exp(jnp.where(causal, gcol[i] - gct_s[s][h:h + 1, :], -jnp.inf)) for i, (s, h) in enumerate(ch)]
    kb = [k[i] * beta[i] for i in n]
    kbf = [k[i].astype(BF16) for i in n]
    kq = [lax.dot_general(jnp.concatenate([kb[i], q[i]], axis=0).astype(BF16), kbf[i], _NT,
                          preferred_element_type=F32) for i in n]
    qk = [kq[i][c:] * decay[i] for i in n]
    y = [jnp.where(strict, -(kq[i][:c] * decay[i]), 0.0) for i in n]
    x = [eye + y[i] for i in n]
    y = [_bdot(y[i], y[i]) for i in n]
    for _ in range(n_double - 1):
        t = [_bdot(jnp.concatenate([x[i], y[i]], axis=0), y[i]) for i in n]
        x = [x[i] + t[i][:c] for i in n]
        y = [t[i][c:] for i in n]
    x = [x[i] + _bdot(x[i], y[i]) for i in n]
    egc = [jnp.exp(gcol[i]) for i in n]
    sol = [_bdot(x[i], jnp.concatenate([v_ref[s, :, sl(h)] * beta[i], kb[i] * egc[i]], axis=1))
           for i, (s, h) in enumerate(ch)]
    s_old = [s_ref[s, h] for s, h in ch]
    r = [_bdot(jnp.concatenate([sol[i][:, GROUP:], q[i] * egc[i]], axis=0), s_old[i]) for i in n]
    v_new = [sol[i][:, :GROUP] - r[i][:c] for i in n]
    o = [r[i][c:] + _bdot(qk[i], v_new[i]) for i in n]
    gl = [gc_s[s][c - 1:c, h:h + 1] for s, h in ch]
    upd = [lax.dot_general((k[i] * jnp.exp(gl[i] - gcol[i])).astype(BF16), v_new[i].astype(BF16), _TN,
                           preferred_element_type=F32) for i in n]
    ms = [jnp.mean(o[i] * o[i], axis=-1, keepdims=True) for i in n]
    for i, (s, h) in enumerate(ch):
        s_ref[s, h] = s_old[i] * jnp.exp(gl[i]) + upd[i]
        on = o[i] * lax.rsqrt(ms[i] + EPS) * ng_ref[...]
        yb_ref[s, :, sl(h)] = (on * _silu(z_ref[s, :, sl(h)])).astype(yb_ref.dtype)

    @pl.when(ci == n_chunks - 1)
    def _():
        sout_ref[...] = s_ref[...]


def _delta_prompt(q, k, v, h, z_col0, ab, alog, dtb, norm_g, n_seq, rows_per_seq, ns):
    d_b = q.shape[1]
    n_heads = d_b // GROUP
    n_chunks = rows_per_seq // DN_CHUNK
    r3 = lambda t: t.reshape(n_seq, rows_per_seq, t.shape[-1])
    act = pl.BlockSpec((ns, DN_CHUNK, d_b), lambda s, i: (s, i, 0))
    vec = pl.BlockSpec((1, LANES), lambda s, i: (0, 0))
    zb = z_col0 // d_b
    yb, s_new = pl.pallas_call(
        functools.partial(_delta_prompt_kernel, n_heads=n_heads, n_chunks=n_chunks, ns=ns),
        grid=(n_seq // ns, n_chunks),
        in_specs=[act, act, act,
                  pl.BlockSpec((ns, DN_CHUNK, d_b), lambda s, i: (s, i, zb)),
                  pl.BlockSpec((ns, DN_CHUNK, LANES), lambda s, i: (s, i, 0)),
                  vec, vec, vec],
        out_specs=[act, pl.BlockSpec((ns, n_heads, GROUP, GROUP), lambda s, i: (s, 0, 0, 0))],
        out_shape=[jax.ShapeDtypeStruct((n_seq, rows_per_seq, d_b), BF16),
                   jax.ShapeDtypeStruct((n_seq, n_heads, GROUP, GROUP), F32)],
        scratch_shapes=[pltpu.VMEM((ns, n_heads, GROUP, GROUP), F32)],
        compiler_params=_params(("arbitrary", "arbitrary")),
        name="delta_prompt",
    )(r3(q), r3(k), r3(v), r3(h), r3(ab), alog, dtb, norm_g)
    return yb.reshape(n_seq * rows_per_seq, d_b), s_new


def _delta_sample_kernel(q_ref, k_ref, v_ref, z_ref, ab_ref, alog_ref, dtb_ref, ng_ref, s0_ref,
                         yb_ref, s1_ref, lhs_ref, res_ref, kt_ref, vn_ref, gl_ref, *, n_heads, n_pos, bb):
    nt = n_pos
    hs = range(n_heads)
    sls = [slice(h * GROUP, (h + 1) * GROUP) for h in hs]
    gates = [_gates(ab_ref[t], alog_ref[...], dtb_ref[...]) for t in range(nt)]
    zeros_pad = jnp.zeros((bb, GROUP), F32)
    ts = range(nt)
    q = [[q_ref[t, :, sls[h]] for t in ts] for h in hs]
    k = [[k_ref[t, :, sls[h]] for t in ts] for h in hs]
    beta = [[gates[t][1][:, n_heads + h:n_heads + h + 1] for t in ts] for h in hs]
    gc = []
    for h in hs:
        g = [gates[t][0][:, h:h + 1] for t in ts]
        for t in range(1, nt):
            g[t] = g[t - 1] + g[t]
        gc.append(g)
    kb = [[k[h][t] * beta[h][t] for t in ts] for h in hs]
    kk = [[[jnp.sum(kb[h][t] * k[h][s], axis=-1, keepdims=True) for s in range(t)] for t in ts] for h in hs]
    qk = [[[jnp.sum(q[h][t] * k[h][s], axis=-1, keepdims=True) for s in range(t + 1)] for t in ts] for h in hs]
    dec = [[[jnp.exp(gc[h][t] - gc[h][s]) for s in range(t + 1)] for t in ts] for h in hs]
    a = [[[kk[h][t][s] * dec[h][t][s] for s in range(t)] for t in ts] for h in hs]
    qk = [[[qk[h][t][s] * dec[h][t][s] for s in range(t + 1)] for t in ts] for h in hs]
    egc = [[jnp.exp(gc[h][t]) for t in ts] for h in hs]
    u_blk = [[v_ref[t, :, sls[h]] * beta[h][t] for t in ts] for h in hs]
    w_blk = [[kb[h][t] * egc[h][t] for t in ts] for h in hs]
    for t in ts:
        for h in hs:
            for s in range(t):
                u_blk[h][t] = u_blk[h][t] - a[h][t][s] * u_blk[h][s]
                w_blk[h][t] = w_blk[h][t] - a[h][t][s] * w_blk[h][s]
    for h in hs:
        for t in ts:
            lhs_ref[h, :, t, :] = w_blk[h][t]
            lhs_ref[h, :, nt + t, :] = q[h][t] * egc[h][t]
        gl_ref[h] = jnp.broadcast_to(egc[h][nt - 1], (bb, GROUP))

    for h in hs:
        for b in range(bb):
            res_ref[h, b] = _bdot(lhs_ref[h, b], s0_ref[b, h])

    v_new = [[u_blk[h][t] - res_ref[h, :, t, :] for t in ts] for h in hs]
    o = [[res_ref[h, :, nt + t, :] for t in ts] for h in hs]
    for h in hs:
        for t in ts:
            for s in range(t + 1):
                o[h][t] = o[h][t] + qk[h][t][s] * v_new[h][s]
    ms = [[jnp.mean(o[h][t] * o[h][t], axis=-1, keepdims=True) for t in ts] for h in hs]
    for h in hs:
        gl = gc[h][nt - 1]
        for t in ts:
            on = o[h][t] * lax.rsqrt(ms[h][t] + EPS) * ng_ref[...]
            yb_ref[t, :, sls[h]] = (on * _silu(z_ref[t, :, sls[h]])).astype(yb_ref.dtype)
            kt_ref[h, :, t, :] = k[h][t] * jnp.exp(gl - gc[h][t])
            vn_ref[h, :, t, :] = v_new[h][t]
            kt_ref[h, :, nt + t, :] = zeros_pad
            vn_ref[h, :, nt + t, :] = zeros_pad

    for h in hs:
        for b in range(bb):
            upd = lax.dot_general(kt_ref[h, b].astype(BF16), vn_ref[h, b].astype(BF16), _TN,
                                  preferred_element_type=F32)
            s1_ref[b, h] = s0_ref[b, h] * gl_ref[h, b:b + 1, :] + upd


def _delta_sample(q, k, v, h3, z_col0, ab, alog, dtb, norm_g, s0, bb):
    n_pos, nb, d_b = q.shape
    n_heads = d_b // GROUP
    assert 2 * n_pos == SUBLANES, "one state read packs [w | q] rows of a sequence into one 8-row tile"
    zb = z_col0 // d_b
    act = pl.BlockSpec((n_pos, bb, d_b), lambda i: (0, i, 0))
    vec = pl.BlockSpec((1, LANES), lambda i: (0, 0))
    st = pl.BlockSpec((bb, n_heads, GROUP, GROUP), lambda i: (i, 0, 0, 0))
    tile = pltpu.VMEM((n_heads, bb, SUBLANES, GROUP), F32)
    return pl.pallas_call(
        functools.partial(_delta_sample_kernel, n_heads=n_heads, n_pos=n_pos, bb=bb),
        grid=(nb // bb,),
        in_specs=[act, act, act,
                  pl.BlockSpec((n_pos, bb, d_b), lambda i: (0, i, zb)),
                  pl.BlockSpec((n_pos, bb, LANES), lambda i: (0, i, 0)),
                  vec, vec, vec, st],
        out_specs=[act, st],
        out_shape=[jax.ShapeDtypeStruct((n_pos, nb, d_b), BF16),
                   jax.ShapeDtypeStruct(s0.shape, F32)],
        scratch_shapes=[tile, tile, tile, tile, pltpu.VMEM((n_heads, bb, GROUP), F32)],
        compiler_params=_params(("arbitrary",)),
        name="delta_sample",
    )(q, k, v, h3, ab, alog, dtb, norm_g, s0)


def _out_proj_kernel(ya_ref, yb_ref, w_ref, r_ref, g_ref, b_ref, o_ref, ob_ref, z_ref, *, n_tiles, d_a, alpha):
    i = pl.program_id(0)

    @pl.when(i == 0)
    def _():
        z_ref[...] = jnp.zeros(z_ref.shape, F32)

    def emit_ln():
        y = _layer_norm(z_ref[...], g_ref[...], b_ref[...])
        o_ref[...] = y
        ob_ref[...] = y.astype(ob_ref.dtype)

    @pl.when(i < n_tiles)
    def _():
        emit_ln()
        acc = (jnp.dot(ya_ref[...], w_ref[0:d_a, :], preferred_element_type=F32)
               + jnp.dot(yb_ref[...], w_ref[d_a:, :], preferred_element_type=F32))
        z_ref[...] = alpha * r_ref[...] + acc

    @pl.when(i == n_tiles)
    def _():
        emit_ln()


def _out_proj_ln(ya, yb, w, resid, g, b, alpha, tm):
    m, d_a = ya.shape
    k, d = w.shape
    n_tiles = m // tm
    cur = lambda i: (jnp.minimum(i, n_tiles - 1), 0)
    prev = lambda i: (jnp.maximum(i - 1, 0), 0)
    vec = pl.BlockSpec((1, d), lambda i: (0, 0))
    return pl.pallas_call(
        functools.partial(_out_proj_kernel, n_tiles=n_tiles, d_a=d_a, alpha=alpha),
        grid=(n_tiles + 1,),
        in_specs=[pl.BlockSpec((tm, d_a), cur),
                  pl.BlockSpec((tm, k - d_a), cur),
                  pl.BlockSpec((k, d), lambda i: (0, 0), pipeline_mode=pl.Buffered(1)),
                  pl.BlockSpec((tm, d), cur), vec, vec],
        out_specs=[pl.BlockSpec((tm, d), prev), pl.BlockSpec((tm, d), prev)],
        out_shape=[jax.ShapeDtypeStruct((m, d), F32), jax.ShapeDtypeStruct((m, d), BF16)],
        scratch_shapes=[pltpu.VMEM((tm, d), F32)],
        compiler_params=_params(("arbitrary",)),
        name="out_proj_ln1",
    )(ya, yb, w, resid, g.reshape(1, d), b.reshape(1, d))


def _down_kernel(x_ref, w_ref, r_ref, g_ref, b_ref, o_ref, *rest, n_tiles, nk, nc, alpha, n_pos):
    z_ref = rest[-1]
    cw = z_ref.shape[-1]
    slab = lambda c: slice(c * cw, (c + 1) * cw)
    i, kk = pl.program_id(0), pl.program_id(1)

    def emit_ln():
        z = [z_ref[c] for c in range(nc)]
        inv_d = 1.0 / (nc * cw)
        mu = sum(jnp.sum(zc, axis=-1, keepdims=True) for zc in z) * inv_d
        xc = [zc - mu for zc in z]
        var = sum(jnp.sum(c * c, axis=-1, keepdims=True) for c in xc) * inv_d
        rs = lax.rsqrt(var + EPS)
        for c in range(nc):
            y = xc[c] * rs * g_ref[:, slab(c)] + b_ref[:, slab(c)]
            if n_pos:
                nb = y.shape[0] // n_pos
                for t in range(n_pos):
                    o_ref[:, t, slab(c)] = y[t * nb:(t + 1) * nb]
            else:
                o_ref[:, slab(c)] = y

    @pl.when(jnp.logical_and(i == 0, kk == 0))
    def _():
        z_ref[...] = jnp.zeros(z_ref.shape, F32)

    wb = w_ref[...].astype(BF16)
    if len(rest) == 2:
        rest[0][...] = wb

    @pl.when(kk == 0)
    def _():
        emit_ln()
        part = jnp.dot(x_ref[...], wb, preferred_element_type=F32)
        for c in range(nc):
            if nc == nk:
                z_ref[c] = part[:, slab(c)] + alpha * r_ref[...] if c == 0 else part[:, slab(c)]
            else:
                z_ref[c] = part[:, slab(c)] + alpha * r_ref[:, slab(c)]

    @pl.when(jnp.logical_and(kk > 0, i < n_tiles))
    def _():
        part = jnp.dot(x_ref[...], wb, preferred_element_type=F32)
        for c in range(nc):
            z_ref[c] += part[:, slab(c)]
        if nc == nk:
            z_ref[kk] += alpha * r_ref[...]


def _down_resident_kernel(x_ref, w_ref, r_ref, g_ref, b_ref, o_ref, z_ref, *, n_tiles, alpha):
    i = pl.program_id(0)

    @pl.when(i == 0)
    def _():
        z_ref[...] = jnp.zeros(z_ref.shape, F32)

    @pl.when(i < n_tiles)
    def _():
        o_ref[...] = _layer_norm(z_ref[...], g_ref[...], b_ref[...])
        z_ref[...] = alpha * r_ref[...] + jnp.dot(x_ref[...], w_ref[...], preferred_element_type=F32)

    @pl.when(i == n_tiles)
    def _():
        o_ref[...] = _layer_norm(z_ref[...], g_ref[...], b_ref[...])


def _down_ln_resident(x, w, resid, g, b, alpha, tm):
    m, kdim = x.shape
    d = w.shape[1]
    n_tiles = m // tm
    cur = lambda i: (jnp.minimum(i, n_tiles - 1), 0)
    vec = pl.BlockSpec((1, d), lambda i: (0, 0))
    return pl.pallas_call(
        functools.partial(_down_resident_kernel, n_tiles=n_tiles, alpha=alpha),
        grid=(n_tiles + 1,),
        in_specs=[pl.BlockSpec((tm, kdim), cur),
                  pl.BlockSpec((kdim, d), lambda i: (0, 0), pipeline_mode=pl.Buffered(1)),
                  pl.BlockSpec((tm, d), cur), vec, vec],
        out_specs=pl.BlockSpec((tm, d), lambda i: (jnp.maximum(i - 1, 0), 0)),
        out_shape=jax.ShapeDtypeStruct((m, d), F32),
        scratch_shapes=[pltpu.VMEM((tm, d), F32)],
        compiler_params=_params(("arbitrary",), BIG_VMEM_LIMIT),
        name="down_ln2_resident",
    )(x, w, resid, g.reshape(1, d), b.reshape(1, d))


def _down_ln(x, w, resid, g, b, alpha, tm, tk, emit_w=False, n_pos=0):
    m, kdim = x.shape
    d = w.shape[1]
    nk = kdim // tk
    n_tiles = m // tm
    nc = nk if (d % nk == 0 and (d // nk) % LANES == 0) else 1
    row = lambda i: jnp.minimum(i, n_tiles - 1)
    kt = lambda i, kk: jnp.where(i == n_tiles, nk - 1, kk)
    vec = pl.BlockSpec((1, d), lambda i, kk: (0, 0))
    wspec = pl.BlockSpec((tk, d), lambda i, kk: (kt(i, kk), 0))
    if nc == nk:
        rspec = pl.BlockSpec((tm, d // nc), lambda i, kk: (row(i), kt(i, kk)))
    else:
        rspec = pl.BlockSpec((tm, d), lambda i, kk: (row(i), 0))
    if n_pos:
        assert n_tiles == 1
        out_specs = [pl.BlockSpec((m // n_pos, n_pos, d), lambda i, kk: (0, 0, 0))]
        out_shape = [jax.ShapeDtypeStruct((m // n_pos, n_pos, d), F32)]
    else:
        out_specs = [pl.BlockSpec((tm, d), lambda i, kk: (jnp.where(kk == 0, jnp.maximum(i - 1, 0), row(i)), 0))]
        out_shape = [jax.ShapeDtypeStruct((m, d), F32)]
    if emit_w:
        out_specs.append(wspec)
        out_shape.append(jax.ShapeDtypeStruct(w.shape, BF16))
    outs = pl.pallas_call(
        functools.partial(_down_kernel, n_tiles=n_tiles, nk=nk, nc=nc, alpha=alpha, n_pos=n_pos),
        grid=(n_tiles + 1, nk),
        in_specs=[pl.BlockSpec((tm, tk), lambda i, kk: (row(i), kt(i, kk))), wspec, rspec, vec, vec],
        out_specs=out_specs,
        out_shape=out_shape,
        scratch_shapes=[pltpu.VMEM((nc, tm, d // nc), F32)],
        compiler_params=_params(("arbitrary", "arbitrary"), BIG_VMEM_LIMIT),
        name="down_ln2",
    )(x, w, resid, g.reshape(1, d), b.reshape(1, d))
    return tuple(outs) if emit_w else outs[0]


def _ffn_up_kernel(x_ref, wg_ref, wv_ref, cwg_ref, cwv_ref, cbg_ref, cbv_ref,
                   act_ref, tg_ref, tv_ref, wgo_ref, wvo_ref, wgb_ref, wvb_ref, carry_ref, *, tm):
    halo = SUBLANES
    first = pl.program_id(2) == 0

    @pl.when(jnp.logical_and(pl.program_id(1) == 0, first))
    def _():
        for w_ref, wb_ref, wo_ref in ((wg_ref, wgb_ref, wgo_ref), (wv_ref, wvb_ref, wvo_ref)):
            wb = w_ref[...].astype(BF16)
            wb_ref[...] = wb
            wo_ref[...] = wb

    @pl.when(first)
    def _():
        carry_ref[...] = jnp.zeros(carry_ref.shape, F32)

    x = x_ref[...]
    conv = []
    parts = ((wgb_ref, cwg_ref, cbg_ref, tg_ref), (wvb_ref, cwv_ref, cbv_ref, tv_ref))
    row = lax.broadcasted_iota(jnp.int32, (halo, act_ref.shape[1]), 0)
    for p, (w_ref, cw_ref, cb_ref, t_ref) in enumerate(parts):
        up = jnp.dot(x, w_ref[...], preferred_element_type=F32)
        hist = carry_ref[p]
        tail = up[tm - halo:tm, :]
        carry_ref[p] = tail
        t_ref[0] = tail
        acc = cw_ref[CONV_FFN - 1:CONV_FFN, :] * up
        for dd in range(1, CONV_FFN):
            xs = pltpu.roll(up, dd, axis=0)
            head = jnp.where(row < dd, pltpu.roll(hist, dd, axis=0), xs[0:halo])
            xs = jnp.concatenate([head, xs[halo:]], axis=0)
            acc = acc + cw_ref[CONV_FFN - 1 - dd:CONV_FFN - dd, :] * xs
        conv.append(acc + cb_ref[...])
    act_ref[...] = (_silu(conv[0]) * conv[1]).astype(act_ref.dtype)


def _ffn_up_carried_kernel(x_ref, wg_ref, wv_ref, buf_ref, cw_ref, cb_ref, act_ref, t_ref, xbuf_ref, gate_ref,
                           *, shift):
    p, j = pl.program_id(0), pl.program_id(1)
    tm, halo = x_ref.shape[0], buf_ref.shape[0]
    xbuf_ref[0:halo, :] = buf_ref[...]

    def conv_of(w_ref):
        xbuf_ref[halo:halo + tm, :] = jnp.dot(x_ref[...], w_ref[...], preferred_element_type=F32)
        tail = xbuf_ref[tm:tm + halo, :]
        for t in range(halo // shift):
            t_ref[:, t, :] = tail[t * shift:(t + 1) * shift]
        return _conv_taps(xbuf_ref, cw_ref, tm, shift, halo, CONV_FFN) + cb_ref[...]

    @pl.when(p == 0)
    def _():
        gate_ref[j] = conv_of(wg_ref)

    @pl.when(p == 1)
    def _():
        act_ref[...] = (_silu(gate_ref[j]) * conv_of(wv_ref)).astype(act_ref.dtype)


def _ffn_up_carried(x, w_gate_b, w_val_b, conv_w, conv_b, bufs, tn, shift):
    m, d = x.shape
    d_ff = w_gate_b.shape[1]
    nj = d_ff // tn
    halo = (CONV_FFN - 1) * shift
    col = lambda p, j: (0, p * nj + j)
    return pl.pallas_call(
        functools.partial(_ffn_up_carried_kernel, shift=shift),
        grid=(2, nj),
        in_specs=[pl.BlockSpec((m, d), lambda p, j: (0, 0)),
                  pl.BlockSpec((d, tn), lambda p, j: (0, jnp.where(p == 0, j, nj - 1))),
                  pl.BlockSpec((d, tn), lambda p, j: (0, jnp.where(p == 1, j, 0))),
                  pl.BlockSpec((halo, tn), col),
                  pl.BlockSpec((CONV_FFN, tn), col),
                  pl.BlockSpec((1, tn), col)],
        out_specs=[pl.BlockSpec((m, tn), lambda p, j: (0, jnp.where(p == 1, j, 0))),
                   pl.BlockSpec((shift, halo // shift, tn), lambda p, j: (0, 0, p * nj + j))],
        out_shape=[jax.ShapeDtypeStruct((m, d_ff), BF16),
                   jax.ShapeDtypeStruct((shift, halo // shift, 2 * d_ff), F32)],
        scratch_shapes=[pltpu.VMEM((halo + m, tn), F32), pltpu.VMEM((nj, m, tn), F32)],
        compiler_params=_params(("arbitrary", "arbitrary")),
        name="ffn_up_carried",
    )(x, w_gate_b, w_val_b, bufs, conv_w, conv_b.reshape(1, 2 * d_ff))


def _ffn_up(x, w_up, conv_w, conv_b, n_seq, rows_per_seq, tm, tn):
    m, d = x.shape
    d_ff = w_up.shape[1] // 2
    nj = d_ff // tn
    halo = SUBLANES
    tiles = rows_per_seq // tm
    wspec = lambda off: pl.BlockSpec((d, tn), lambda j, s, i: (0, off + j))
    cwspec = lambda off: pl.BlockSpec((CONV_FFN, tn), lambda j, s, i: (0, off + j))
    cbspec = lambda off: pl.BlockSpec((1, tn), lambda j, s, i: (0, off + j))
    tspec = pl.BlockSpec((1, halo, tn), lambda j, s, i: (s, 0, j))
    tshape = jax.ShapeDtypeStruct((n_seq, halo, d_ff), F32)
    cb2 = conv_b.reshape(1, 2 * d_ff)
    return pl.pallas_call(
        functools.partial(_ffn_up_kernel, tm=tm),
        grid=(nj, n_seq, tiles),
        in_specs=[pl.BlockSpec((tm, d), lambda j, s, i: (s * tiles + i, 0)), wspec(0), wspec(nj),
                  cwspec(0), cwspec(nj), cbspec(0), cbspec(nj)],
        out_specs=[pl.BlockSpec((tm, tn), lambda j, s, i: (s * tiles + i, j)), tspec, tspec, wspec(0), wspec(0)],
        out_shape=[jax.ShapeDtypeStruct((m, d_ff), BF16), tshape, tshape]
                  + [jax.ShapeDtypeStruct((d, d_ff), BF16)] * 2,
        scratch_shapes=[pltpu.VMEM((d, tn), BF16)] * 2 + [pltpu.VMEM((2, halo, tn), F32)],
        compiler_params=_params(("arbitrary", "arbitrary", "arbitrary")),
        name="ffn_up",
    )(x, w_up, w_up, conv_w, conv_w, cb2, cb2)


def _pad_lanes(vec, offset=0):
    out = jnp.zeros((1, LANES), F32)
    return out.at[0, offset:offset + vec.shape[0]].set(vec.astype(F32))


def _mixer_half(x2, lw, alpha, n_seq, rows_per_seq, shift, state, w_in_b):
    (w_in, conv_qkv_w, a_log, dt_bias, norm_g, gm_ln_g, gm_ln_b, gm_ws, gm_bs, gm_out_g, w_out,
     ln1_g, ln1_b) = lw
    m, d_model = x2.shape
    n_groups = gm_ws.shape[0]
    d_a = n_groups * GROUP
    d_qkv = conv_qkv_w.shape[1]
    d_b = d_qkv // 3
    n_heads = d_b // GROUP
    main_cols = 2 * d_a + d_qkv + d_b
    w_ab = jnp.pad(w_in[main_cols:].astype(BF16), ((0, LANES - 2 * n_heads), (0, 0)))
    if w_in_b is None:
        h, ab, w_in_b = _matmul(x2, w_in, w_ab, main_cols, min(m, PROJ_TM), PROJ_EMIT_TN, emit_w=True)
    else:
        h, ab = _matmul(x2, w_in_b, w_ab, main_cols, min(m, PROJ_TM), PROJ_TN)
    alog, dtb, ng = _pad_lanes(a_log), _pad_lanes(dt_bias), norm_g.reshape(1, GROUP).astype(F32)

    if state is None:
        ya = _chunk_mlp_prompt(h, gm_ln_g, gm_ln_b, gm_ws, gm_bs, gm_out_g)
        v_rows = None
        q, k, v = _qkv_prep(h, conv_qkv_w, None, n_seq, rows_per_seq, QKV_TM, shift, 2 * d_a)
        yb, s_new = _delta_prompt(q, k, v, h, 2 * d_a + d_qkv, ab, alog, dtb, ng, n_seq, rows_per_seq, DELTA_SEQS)
    else:
        s_dn, buf_qkv = state
        n_pos = rows_per_seq // shift
        h3 = h.reshape(n_pos, shift, main_cols)
        ya3, v_rows = _chunk_mlp_sample(h3, gm_ln_g, gm_ln_b, gm_ws, gm_bs, gm_out_g)
        ya = ya3.reshape(m, d_a)
        q, k, v = _qkv_prep(h, conv_qkv_w, buf_qkv, n_seq, rows_per_seq, rows_per_seq, shift, 2 * d_a)
        r3 = lambda t: t.reshape(n_pos, shift, t.shape[-1])
        yb3, s_new = _delta_sample(r3(q), r3(k), r3(v), h3, 2 * d_a + d_qkv, r3(ab), alog, dtb, ng, s_dn,
                                   DELTA_SAMPLE_ROWS)
        yb = yb3.reshape(m, d_b)

    x1, x1b = _out_proj_ln(ya, yb, w_out, x2, ln1_g, ln1_b, alpha, min(m, OUT_TM))
    return x1, x1b, s_new, h, v_rows, w_in_b


def kernel(x_prompt, x_sample, state_dn, state_conv_qkv, state_conv_ffn, w_in, conv_qkv_w, dn_a_log, dn_dt_bias, dn_norm_g, gm_ln_g, gm_ln_b, gm_ws, gm_bs, gm_out_g, w_out, ln1_g, ln1_b, w_up, conv_ffn_w, conv_ffn_b, w_down, ln2_g, ln2_b):
    depth = w_in.shape[0]
    bp, lp, d_model = x_prompt.shape
    bs, ls, _ = x_sample.shape
    d_qkv = conv_qkv_w.shape[-1]
    d_a = gm_ws.shape[1] * GROUP
    alpha = (2.0 * depth) ** 0.25

    hp = x_prompt.reshape(bp * lp, d_model)
    hs = jnp.transpose(x_sample, (1, 0, 2)).reshape(ls * bs, d_model)
    outs = [[] for _ in range(7)]
    w_out = w_out.astype(BF16)
    w_in = jnp.swapaxes(w_in, 1, 2)
    for l in range(depth):
        lw = tuple(t[l] for t in (w_in, conv_qkv_w, dn_a_log, dn_dt_bias, dn_norm_g, gm_ln_g, gm_ln_b, gm_ws,
                                  gm_bs, gm_out_g, w_out, ln1_g, ln1_b))
        ffn_w = (conv_ffn_w[l], conv_ffn_b[l])
        ln2 = (ln2_g[l], ln2_b[l], alpha)
        n_pos = ls if l == depth - 1 else 0
        bq = jnp.transpose(state_conv_qkv[l], (1, 0, 2)).reshape((CONV_QKV - 1) * bs, d_qkv)
        bf = jnp.transpose(state_conv_ffn[l], (1, 0, 2)).reshape((CONV_FFN - 1) * bs, -1)
        x1_s, x1b_s, s_s, h_s, v_s, w_in_b = _mixer_half(hs, lw, alpha, 1, ls * bs, bs, (state_dn[l], bq), None)
        x1_p, x1b_p, s_p, h_p, _, _ = _mixer_half(hp, lw, alpha, bp, lp, 1, None, w_in_b)
        act_p, tg_p, tv_p, w_gate_b, w_val_b = _ffn_up(x1b_p, w_up[l], *ffn_w, bp, lp, FFN_TM, FFN_TN)
        act_s, cf_s = _ffn_up_carried(x1b_s, w_gate_b, w_val_b, *ffn_w, bf, FFN_TN, bs)
        hs, w_down_b = _down_ln(act_s, w_down[l], x1_s, *ln2, min(ls * bs, DOWN_EMIT_TM), DOWN_EMIT_TK,
                                emit_w=True, n_pos=n_pos)
        hp = _down_ln_resident(act_p, w_down_b, x1_p, *ln2, min(bp * lp, DOWN_TM))
        h_s3 = h_s.reshape(ls, bs, -1)
        cq_s = jnp.transpose(h_s3[ls - (CONV_QKV - 1):, :, 2 * d_a:2 * d_a + d_qkv], (1, 0, 2))
        h_p3 = h_p.reshape(bp, lp, -1)
        cq_p = h_p3[:, lp - (CONV_QKV - 1):, 2 * d_a:2 * d_a + d_qkv]
        n_tail = tg_p.shape[1]
        cf_p = jnp.concatenate([tg_p, tv_p], axis=-1)[:, n_tail - (CONV_FFN - 1):]
        for acc, val in zip(outs, (s_p, cq_p, cf_p, s_s, cq_s, cf_s, v_s)):
            acc.append(val)
    return (hp.reshape(bp, lp, d_model), hs) + tuple(jnp.stack(o) for o in outs)
```
